```python
import jax, jax.numpy as jnp
from jax import lax
import numpy as np

D_MODEL = 1024
BATCH = 16
SEQ = 4096
DEPTH = 2

D_MIX = D_MODEL
CA = D_MIX // 2
A_GROUPS = 8
CONV_WIDTH = 31
CB = D_MIX // 2
GB = 4
DB = CB // GB
CHUNK = 128
N_HEADS = 16
HEAD_DIM = D_MIX // N_HEADS
Q_BLOCK = 128
D_FF = 2816
N_EVEN = (DEPTH + 1) // 2
N_ODD = DEPTH // 2
EPS = 1e-6

kernel_name = "hybrid_conv_gmlp_stickbreaking_macaron"


def rms_norm(x, g):
    xf = x.astype(jnp.float32)
    y = xf * lax.rsqrt(jnp.mean(xf * xf, axis=-1, keepdims=True) + EPS)
    return (y * g.astype(jnp.float32)).astype(x.dtype)


def layer_norm(x, g, b):
    xf = x.astype(jnp.float32)
    mu = jnp.mean(xf, axis=-1, keepdims=True)
    xc = xf - mu
    var = jnp.mean(xc * xc, axis=-1, keepdims=True)
    y = xc * lax.rsqrt(var + EPS)
    return (y * g.astype(jnp.float32) + b.astype(jnp.float32)).astype(x.dtype)


def swiglu(h, w_gate, w_up, w_down):
    return (jax.nn.silu(h @ w_gate) * (h @ w_up)) @ w_down


def causal_depthwise_conv(x, w, b):
    k, c = w.shape
    y = lax.conv_general_dilated(
        x, w[:, None, :], window_strides=(1,), padding=[(k - 1, 0)],
        dimension_numbers=("NWC", "WIO", "NWC"), feature_group_count=c)
    return y + b


def conv_gating_mixer(h, w_in, conv_w, conv_b, ln_a_g, ln_a_b, ln_v_g, ln_v_b,
                      sp_w, sp_b, w_out):
    bsz, s, _ = h.shape
    z = h @ w_in
    a_val, a_gate, u, v = jnp.split(z, [CA, 2 * CA, 2 * CA + CB], axis=-1)
    a = a_val * jax.nn.sigmoid(a_gate)
    a = causal_depthwise_conv(a, conv_w, conv_b)
    a = jax.nn.silu(layer_norm(a, ln_a_g, ln_a_b))
    v = layer_norm(v.reshape(bsz, s, GB, DB), ln_v_g, ln_v_b)
    v = v.reshape(bsz, s // CHUNK, CHUNK, GB, DB)
    causal = jnp.tril(jnp.ones((CHUNK, CHUNK), dtype=bool))
    w_s = jnp.where(causal[None], sp_w, 0)
    v = jnp.einsum("gts,bcsgd->bctgd", w_s, v) + sp_b.T[:, :, None]
    b_out = u * v.reshape(bsz, s, CB)
    return jnp.concatenate([a, b_out], axis=-1) @ w_out


def stick_breaking_attention(q, k, v):
    bsz, s, h, dh = q.shape
    nb = s // Q_BLOCK
    scale = dh ** -0.5

    def to_blocks(t):
        return t.reshape(bsz, nb, Q_BLOCK, h, dh).transpose(1, 0, 3, 2, 4)

    qb, kb, vb = to_blocks(q), to_blocks(k), to_blocks(v)
    pos = jnp.arange(Q_BLOCK)

    def per_query_block(args):
        qi, qblk = args
        qf = qblk.astype(jnp.float32)

        def step(carry, xs):
            kj, kblk, vblk = xs

            def accumulate(c):
                log_rem, acc = c
                z = jnp.einsum("bhtd,bhsd->bhts", qf, kblk.astype(jnp.float32)) * scale
                mask = (kj * Q_BLOCK + pos[None, :]) < (qi * Q_BLOCK + pos[:, None])
                log_keep = jnp.where(mask, -jax.nn.softplus(z), 0.0)
                suffix = lax.cumsum(log_keep, axis=3, reverse=True) - log_keep
                log_a = jax.nn.log_sigmoid(z) + suffix + log_rem[..., None]
                a = jnp.where(mask, jnp.exp(log_a), 0.0)
                acc = acc + jnp.einsum("bhts,bhsd->bhtd", a, vblk.astype(jnp.float32))
                return (log_rem + jnp.sum(log_keep, axis=-1), acc)

            return lax.cond(kj <= qi, accumulate, lambda c: c, carry), None

        init = (jnp.zeros((bsz, h, Q_BLOCK), jnp.float32),
                jnp.zeros((bsz, h, Q_BLOCK, dh), jnp.float32))
        (_, acc), _ = lax.scan(step, init, (jnp.arange(nb), kb, vb), reverse=True)
        return acc.astype(q.dtype)

    out = lax.map(per_query_block, (jnp.arange(nb), qb))
    return out.transpose(1, 0, 3, 2, 4).reshape(bsz, s, h * dh)


def stick_breaking_mixer(h, w_qkv, w_o):
    bsz, s, _ = h.shape
    q, k, v = jnp.split(h @ w_qkv, 3, axis=-1)
    q = q.reshape(bsz, s, N_HEADS, HEAD_DIM)
    k = k.reshape(bsz, s, N_HEADS, HEAD_DIM)
    v = v.reshape(bsz, s, N_HEADS, HEAD_DIM)
    return stick_breaking_attention(q, k, v) @ w_o


def _fwd_setup_inputs(seed: int = 0) -> dict:
    key = jax.random.key(seed)
    ks = iter(jax.random.split(key, 32))
    f32 = jnp.float32

    def nrm(shape, fan_in):
        return jax.random.normal(next(ks), shape, f32) * (fan_in ** -0.5)

    def gain(shape):
        return 1.0 + 0.02 * jax.random.normal(next(ks), shape, f32)

    def bias(shape, s=0.02):
        return s * jax.random.normal(next(ks), shape, f32)

    return {
        "x": jax.random.normal(next(ks), (BATCH, SEQ, D_MODEL), f32),
        "g_ffn1": gain((DEPTH, D_MODEL)),
        "w_ffn1_gate": nrm((DEPTH, D_MODEL, D_FF), D_MODEL),
        "w_ffn1_up": nrm((DEPTH, D_MODEL, D_FF), D_MODEL),
        "w_ffn1_down": nrm((DEPTH, D_FF, D_MODEL), D_FF),
        "g_mix": gain((DEPTH, D_MODEL)),
        "w_in_ab": nrm((N_EVEN, D_MODEL, 2 * CA + 2 * CB), D_MODEL),
        "conv_w": nrm((N_EVEN, CONV_WIDTH, CA), CONV_WIDTH),
        "conv_b": bias((N_EVEN, CA)),
        "ln_a_g": gain((N_EVEN, CA)),
        "ln_a_b": bias((N_EVEN, CA)),
        "ln_v_g": gain((N_EVEN, GB, DB)),
        "ln_v_b": bias((N_EVEN, GB, DB)),
        "sp_w": nrm((N_EVEN, GB, CHUNK, CHUNK), CHUNK),
        "sp_b": 1.0 + bias((N_EVEN, GB, CHUNK), 0.1),
        "w_out_ab": nrm((N_EVEN, CA + CB, D_MODEL), CA + CB),
        "w_qkv": nrm((N_ODD, D_MODEL, 3 * D_MIX), D_MODEL),
        "w_o": nrm((N_ODD, D_MIX, D_MODEL), D_MIX),
        "g_ffn2": gain((DEPTH, D_MODEL)),
        "w_ffn2_gate": nrm((DEPTH, D_MODEL, D_FF), D_MODEL),
        "w_ffn2_up": nrm((DEPTH, D_MODEL, D_FF), D_MODEL),
        "w_ffn2_down": nrm((DEPTH, D_FF, D_MODEL), D_FF),
        "g_final": gain((D_MODEL,)),
    }


def _fwd_reference(x, g_ffn1, w_ffn1_gate, w_ffn1_up, w_ffn1_down, g_mix, w_in_ab,
              conv_w, conv_b, ln_a_g, ln_a_b, ln_v_g, ln_v_b, sp_w, sp_b, w_out_ab,
              w_qkv, w_o, g_ffn2, w_ffn2_gate, w_ffn2_up, w_ffn2_down, g_final):
    for l in range(DEPTH):
        x = x + 0.5 * swiglu(rms_norm(x, g_ffn1[l]), w_ffn1_gate[l], w_ffn1_up[l], w_ffn1_down[l])
        h = rms_norm(x, g_mix[l])
        i = l // 2
        if l % 2 == 0:
            x = x + conv_gating_mixer(h, w_in_ab[i], conv_w[i], conv_b[i], ln_a_g[i], ln_a_b[i],
                                      ln_v_g[i], ln_v_b[i], sp_w[i], sp_b[i], w_out_ab[i])
        else:
            x = x + stick_breaking_mixer(h, w_qkv[i], w_o[i])
        x = x + 0.5 * swiglu(rms_norm(x, g_ffn2[l]), w_ffn2_gate[l], w_ffn2_up[l], w_ffn2_down[l])
    return rms_norm(x, g_final)


import jax as _jax
import jax.numpy as _jnp

TWIN_FORMAT = 'train_step'
FWD_PARAMS = ['x', 'g_ffn1', 'w_ffn1_gate', 'w_ffn1_up', 'w_ffn1_down', 'g_mix', 'w_in_ab', 'conv_w', 'conv_b', 'ln_a_g', 'ln_a_b', 'ln_v_g', 'ln_v_b', 'sp_w', 'sp_b', 'w_out_ab', 'w_qkv', 'w_o', 'g_ffn2', 'w_ffn2_gate', 'w_ffn2_up', 'w_ffn2_down', 'g_final']
TWIN_WEIGHTS = ['g_ffn1', 'w_ffn1_gate', 'w_ffn1_up', 'w_ffn1_down', 'g_mix', 'w_in_ab', 'conv_w', 'conv_b', 'ln_a_g', 'ln_a_b', 'ln_v_g', 'ln_v_b', 'sp_w', 'sp_b', 'w_out_ab', 'w_qkv', 'w_o', 'g_ffn2', 'w_ffn2_gate', 'w_ffn2_up', 'w_ffn2_down', 'g_final']
TWIN_DIFF_INPUT = 'x'
TWIN_INPUTS = ['x', 'g_ffn1', 'w_ffn1_gate', 'w_ffn1_up', 'w_ffn1_down', 'g_mix', 'w_in_ab', 'conv_w', 'conv_b', 'ln_a_g', 'ln_a_b', 'ln_v_g', 'ln_v_b', 'sp_w', 'sp_b', 'w_out_ab', 'w_qkv', 'w_o', 'g_ffn2', 'w_ffn2_gate', 'w_ffn2_up', 'w_ffn2_down', 'g_final', 'loss_target', 'm_g_ffn1', 'm_w_ffn1_gate', 'm_w_ffn1_up', 'm_w_ffn1_down', 'm_g_mix', 'm_w_in_ab', 'm_conv_w', 'm_conv_b', 'm_ln_a_g', 'm_ln_a_b', 'm_ln_v_g', 'm_ln_v_b', 'm_sp_w', 'm_sp_b', 'm_w_out_ab', 'm_w_qkv', 'm_w_o', 'm_g_ffn2', 'm_w_ffn2_gate', 'm_w_ffn2_up', 'm_w_ffn2_down', 'm_g_final', 'v_g_ffn1', 'v_w_ffn1_gate', 'v_w_ffn1_up', 'v_w_ffn1_down', 'v_g_mix', 'v_w_in_ab', 'v_conv_w', 'v_conv_b', 'v_ln_a_g', 'v_ln_a_b', 'v_ln_v_g', 'v_ln_v_b', 'v_sp_w', 'v_sp_b', 'v_w_out_ab', 'v_w_qkv', 'v_w_o', 'v_g_ffn2', 'v_w_ffn2_gate', 'v_w_ffn2_up', 'v_w_ffn2_down', 'v_g_final']
TWIN_OUTPUTS = ['loss', 'grad_x', 'grad_g_ffn1', 'grad_w_ffn1_gate', 'grad_w_ffn1_up', 'grad_w_ffn1_down', 'grad_g_mix', 'grad_w_in_ab', 'grad_conv_w', 'grad_conv_b', 'grad_ln_a_g', 'grad_ln_a_b', 'grad_ln_v_g', 'grad_ln_v_b', 'grad_sp_w', 'grad_sp_b', 'grad_w_out_ab', 'grad_w_qkv', 'grad_w_o', 'grad_g_ffn2', 'grad_w_ffn2_gate', 'grad_w_ffn2_up', 'grad_w_ffn2_down', 'grad_g_final', 'delta_g_ffn1', 'delta_w_ffn1_gate', 'delta_w_ffn1_up', 'delta_w_ffn1_down', 'delta_g_mix', 'delta_w_in_ab', 'delta_conv_w', 'delta_conv_b', 'delta_ln_a_g', 'delta_ln_a_b', 'delta_ln_v_g', 'delta_ln_v_b', 'delta_sp_w', 'delta_sp_b', 'delta_w_out_ab', 'delta_w_qkv', 'delta_w_o', 'delta_g_ffn2', 'delta_w_ffn2_gate', 'delta_w_ffn2_up', 'delta_w_ffn2_down', 'delta_g_final', 'new_m_g_ffn1', 'new_m_w_ffn1_gate', 'new_m_w_ffn1_up', 'new_m_w_ffn1_down', 'new_m_g_mix', 'new_m_w_in_ab', 'new_m_conv_w', 'new_m_conv_b', 'new_m_ln_a_g', 'new_m_ln_a_b', 'new_m_ln_v_g', 'new_m_ln_v_b', 'new_m_sp_w', 'new_m_sp_b', 'new_m_w_out_ab', 'new_m_w_qkv', 'new_m_w_o', 'new_m_g_ffn2', 'new_m_w_ffn2_gate', 'new_m_w_ffn2_up', 'new_m_w_ffn2_down', 'new_m_g_final', 'new_v_g_ffn1', 'new_v_w_ffn1_gate', 'new_v_w_ffn1_up', 'new_v_w_ffn1_down', 'new_v_g_mix', 'new_v_w_in_ab', 'new_v_conv_w', 'new_v_conv_b', 'new_v_ln_a_g', 'new_v_ln_a_b', 'new_v_ln_v_g', 'new_v_ln_v_b', 'new_v_sp_w', 'new_v_sp_b', 'new_v_w_out_ab', 'new_v_w_qkv', 'new_v_w_o', 'new_v_g_ffn2', 'new_v_w_ffn2_gate', 'new_v_w_ffn2_up', 'new_v_w_ffn2_down', 'new_v_g_final']
TWIN_LEAF_KINDS = {'loss': 'loss', 'grad_x': 'grad_x', 'grad_g_ffn1': 'grad_w', 'grad_w_ffn1_gate': 'grad_w', 'grad_w_ffn1_up': 'grad_w', 'grad_w_ffn1_down': 'grad_w', 'grad_g_mix': 'grad_w', 'grad_w_in_ab': 'grad_w', 'grad_conv_w': 'grad_w', 'grad_conv_b': 'grad_w', 'grad_ln_a_g': 'grad_w', 'grad_ln_a_b': 'grad_w', 'grad_ln_v_g': 'grad_w', 'grad_ln_v_b': 'grad_w', 'grad_sp_w': 'grad_w', 'grad_sp_b': 'grad_w', 'grad_w_out_ab': 'grad_w', 'grad_w_qkv': 'grad_w', 'grad_w_o': 'grad_w', 'grad_g_ffn2': 'grad_w', 'grad_w_ffn2_gate': 'grad_w', 'grad_w_ffn2_up': 'grad_w', 'grad_w_ffn2_down': 'grad_w', 'grad_g_final': 'grad_w', 'delta_g_ffn1': 'delta_w', 'delta_w_ffn1_gate': 'delta_w', 'delta_w_ffn1_up': 'delta_w', 'delta_w_ffn1_down': 'delta_w', 'delta_g_mix': 'delta_w', 'delta_w_in_ab': 'delta_w', 'delta_conv_w': 'delta_w', 'delta_conv_b': 'delta_w', 'delta_ln_a_g': 'delta_w', 'delta_ln_a_b': 'delta_w', 'delta_ln_v_g': 'delta_w', 'delta_ln_v_b': 'delta_w', 'delta_sp_w': 'delta_w', 'delta_sp_b': 'delta_w', 'delta_w_out_ab': 'delta_w', 'delta_w_qkv': 'delta_w', 'delta_w_o': 'delta_w', 'delta_g_ffn2': 'delta_w', 'delta_w_ffn2_gate': 'delta_w', 'delta_w_ffn2_up': 'delta_w', 'delta_w_ffn2_down': 'delta_w', 'delta_g_final': 'delta_w', 'new_m_g_ffn1': 'new_m', 'new_m_w_ffn1_gate': 'new_m', 'new_m_w_ffn1_up': 'new_m', 'new_m_w_ffn1_down': 'new_m', 'new_m_g_mix': 'new_m', 'new_m_w_in_ab': 'new_m', 'new_m_conv_w': 'new_m', 'new_m_conv_b': 'new_m', 'new_m_ln_a_g': 'new_m', 'new_m_ln_a_b': 'new_m', 'new_m_ln_v_g': 'new_m', 'new_m_ln_v_b': 'new_m', 'new_m_sp_w': 'new_m', 'new_m_sp_b': 'new_m', 'new_m_w_out_ab': 'new_m', 'new_m_w_qkv': 'new_m', 'new_m_w_o': 'new_m', 'new_m_g_ffn2': 'new_m', 'new_m_w_ffn2_gate': 'new_m', 'new_m_w_ffn2_up': 'new_m', 'new_m_w_ffn2_down': 'new_m', 'new_m_g_final': 'new_m', 'new_v_g_ffn1': 'new_v', 'new_v_w_ffn1_gate': 'new_v', 'new_v_w_ffn1_up': 'new_v', 'new_v_w_ffn1_down': 'new_v', 'new_v_g_mix': 'new_v', 'new_v_w_in_ab': 'new_v', 'new_v_conv_w': 'new_v', 'new_v_conv_b': 'new_v', 'new_v_ln_a_g': 'new_v', 'new_v_ln_a_b': 'new_v', 'new_v_ln_v_g': 'new_v', 'new_v_ln_v_b': 'new_v', 'new_v_sp_w': 'new_v', 'new_v_sp_b': 'new_v', 'new_v_w_out_ab': 'new_v', 'new_v_w_qkv': 'new_v', 'new_v_w_o': 'new_v', 'new_v_g_ffn2': 'new_v', 'new_v_w_ffn2_gate': 'new_v', 'new_v_w_ffn2_up': 'new_v', 'new_v_w_ffn2_down': 'new_v', 'new_v_g_final': 'new_v'}


def _forward(args):
    return _fwd_reference(*[args[k] for k in FWD_PARAMS])


def _output_shape():
    out = _jax.eval_shape(lambda: _forward(_fwd_setup_inputs(0)))
    return out.shape, out.dtype

N_MICROBATCH = 1
ADAM_LR = 0.001
ADAM_B1 = 0.9
ADAM_B2 = 0.999
ADAM_EPS = 1e-08
ADAM_WD = 0.01
ADAM_STEP = 10
PER_EXAMPLE_BATCH_AXIS = {'x': 0, 'loss_target': 0}
SHARED_INPUTS = []
_WEIGHT_DTYPES = {'g_ffn1': _jnp.float32, 'w_ffn1_gate': _jnp.float32, 'w_ffn1_up': _jnp.float32, 'w_ffn1_down': _jnp.float32, 'g_mix': _jnp.float32, 'w_in_ab': _jnp.float32, 'conv_w': _jnp.float32, 'conv_b': _jnp.float32, 'ln_a_g': _jnp.float32, 'ln_a_b': _jnp.float32, 'ln_v_g': _jnp.float32, 'ln_v_b': _jnp.float32, 'sp_w': _jnp.float32, 'sp_b': _jnp.float32, 'w_out_ab': _jnp.float32, 'w_qkv': _jnp.float32, 'w_o': _jnp.float32, 'g_ffn2': _jnp.float32, 'w_ffn2_gate': _jnp.float32, 'w_ffn2_up': _jnp.float32, 'w_ffn2_down': _jnp.float32, 'g_final': _jnp.float32}
MOMENT_SCALE = {'g_ffn1': 1.025884e-01, 'w_ffn1_gate': 4.374029e-02, 'w_ffn1_up': 4.237556e-02, 'w_ffn1_down': 7.021516e-02, 'g_mix': 1.854371e-01, 'w_in_ab': 1.510062e-01, 'conv_w': 1.225867e-01, 'conv_b': 2.898995e-01, 'ln_a_g': 1.394397e-01, 'ln_a_b': 1.321180e-01, 'ln_v_g': 1.453679e-01, 'ln_v_b': 1.255303e-01, 'sp_w': 1.336990e-01, 'sp_b': 1.943538e-01, 'w_out_ab': 1.884452e-01, 'w_qkv': 7.085133e-02, 'w_o': 1.028765e-01, 'g_ffn2': 7.466594e-02, 'w_ffn2_gate': 3.153305e-02, 'w_ffn2_up': 3.060651e-02, 'w_ffn2_down': 5.078891e-02, 'g_final': 6.394658e+01}


def _to_microbatches(a, axis):
    t = _jnp.moveaxis(a, axis, 0)
    t = t.reshape((N_MICROBATCH, t.shape[0] // N_MICROBATCH) + t.shape[1:])
    return _jnp.moveaxis(t, 1, axis + 1)


def setup_inputs(seed: int = 0) -> dict:
    inp = _fwd_setup_inputs(seed)
    key = _jax.random.fold_in(_jax.random.key(seed), 7919)
    shape, _ = _output_shape()
    out = dict(inp)
    out["loss_target"] = _jax.random.normal(_jax.random.fold_in(key, 0), shape, _jnp.float32)
    for i, name in enumerate(TWIN_WEIGHTS):
        w = inp[name].astype(_jnp.float32)
        if MOMENT_SCALE is None:
            s = _jnp.sqrt(_jnp.mean(_jnp.square(w)) + 1e-30)
        else:
            s = MOMENT_SCALE[name]
        km, kv = _jax.random.split(_jax.random.fold_in(key, i + 1))
        out[name] = w
        out["m_" + name] = s * _jax.random.normal(km, w.shape, _jnp.float32)
        out["v_" + name] = (s * s) * _jax.random.uniform(kv, w.shape, _jnp.float32, 0.5, 1.5)
    if N_MICROBATCH > 1:
        for name, axis in PER_EXAMPLE_BATCH_AXIS.items():
            out[name] = _to_microbatches(out[name], axis)
    return {'x': out['x'], 'g_ffn1': out['g_ffn1'], 'w_ffn1_gate': out['w_ffn1_gate'], 'w_ffn1_up': out['w_ffn1_up'], 'w_ffn1_down': out['w_ffn1_down'], 'g_mix': out['g_mix'], 'w_in_ab': out['w_in_ab'], 'conv_w': out['conv_w'], 'conv_b': out['conv_b'], 'ln_a_g': out['ln_a_g'], 'ln_a_b': out['ln_a_b'], 'ln_v_g': out['ln_v_g'], 'ln_v_b': out['ln_v_b'], 'sp_w': out['sp_w'], 'sp_b': out['sp_b'], 'w_out_ab': out['w_out_ab'], 'w_qkv': out['w_qkv'], 'w_o': out['w_o'], 'g_ffn2': out['g_ffn2'], 'w_ffn2_gate': out['w_ffn2_gate'], 'w_ffn2_up': out['w_ffn2_up'], 'w_ffn2_down': out['w_ffn2_down'], 'g_final': out['g_final'], 'loss_target': out['loss_target'], 'm_g_ffn1': out['m_g_ffn1'], 'm_w_ffn1_gate': out['m_w_ffn1_gate'], 'm_w_ffn1_up': out['m_w_ffn1_up'], 'm_w_ffn1_down': out['m_w_ffn1_down'], 'm_g_mix': out['m_g_mix'], 'm_w_in_ab': out['m_w_in_ab'], 'm_conv_w': out['m_conv_w'], 'm_conv_b': out['m_conv_b'], 'm_ln_a_g': out['m_ln_a_g'], 'm_ln_a_b': out['m_ln_a_b'], 'm_ln_v_g': out['m_ln_v_g'], 'm_ln_v_b': out['m_ln_v_b'], 'm_sp_w': out['m_sp_w'], 'm_sp_b': out['m_sp_b'], 'm_w_out_ab': out['m_w_out_ab'], 'm_w_qkv': out['m_w_qkv'], 'm_w_o': out['m_w_o'], 'm_g_ffn2': out['m_g_ffn2'], 'm_w_ffn2_gate': out['m_w_ffn2_gate'], 'm_w_ffn2_up': out['m_w_ffn2_up'], 'm_w_ffn2_down': out['m_w_ffn2_down'], 'm_g_final': out['m_g_final'], 'v_g_ffn1': out['v_g_ffn1'], 'v_w_ffn1_gate': out['v_w_ffn1_gate'], 'v_w_ffn1_up': out['v_w_ffn1_up'], 'v_w_ffn1_down': out['v_w_ffn1_down'], 'v_g_mix': out['v_g_mix'], 'v_w_in_ab': out['v_w_in_ab'], 'v_conv_w': out['v_conv_w'], 'v_conv_b': out['v_conv_b'], 'v_ln_a_g': out['v_ln_a_g'], 'v_ln_a_b': out['v_ln_a_b'], 'v_ln_v_g': out['v_ln_v_g'], 'v_ln_v_b': out['v_ln_v_b'], 'v_sp_w': out['v_sp_w'], 'v_sp_b': out['v_sp_b'], 'v_w_out_ab': out['v_w_out_ab'], 'v_w_qkv': out['v_w_qkv'], 'v_w_o': out['v_w_o'], 'v_g_ffn2': out['v_g_ffn2'], 'v_w_ffn2_gate': out['v_w_ffn2_gate'], 'v_w_ffn2_up': out['v_w_ffn2_up'], 'v_w_ffn2_down': out['v_w_ffn2_down'], 'v_g_final': out['v_g_final']}


def _loss(weights, diff, rest, loss_target):
    with _jax.named_scope("forward"):
        args = {**rest, TWIN_DIFF_INPUT: diff, **{k: w.astype(_WEIGHT_DTYPES[k]) for k, w in weights.items()}}
        y = _forward(args)
    with _jax.named_scope("loss_head"):
        err = _jnp.square(y.astype(_jnp.float32) - loss_target)
        return 0.5 * _jnp.sum(_jnp.mean(err, axis=-1)) if err.ndim else 0.5 * err


def _adamw(w, g, m, v):
    m = ADAM_B1 * m + (1.0 - ADAM_B1) * g
    v = ADAM_B2 * v + (1.0 - ADAM_B2) * _jnp.square(g)
    m_hat = m / (1.0 - ADAM_B1 ** ADAM_STEP)
    v_hat = v / (1.0 - ADAM_B2 ** ADAM_STEP)
    delta = -ADAM_LR * (m_hat / (_jnp.sqrt(v_hat) + ADAM_EPS) + ADAM_WD * w)
    return delta, m, v


def reference(x, g_ffn1, w_ffn1_gate, w_ffn1_up, w_ffn1_down, g_mix, w_in_ab, conv_w, conv_b, ln_a_g, ln_a_b, ln_v_g, ln_v_b, sp_w, sp_b, w_out_ab, w_qkv, w_o, g_ffn2, w_ffn2_gate, w_ffn2_up, w_ffn2_down, g_final, loss_target, m_g_ffn1, m_w_ffn1_gate, m_w_ffn1_up, m_w_ffn1_down, m_g_mix, m_w_in_ab, m_conv_w, m_conv_b, m_ln_a_g, m_ln_a_b, m_ln_v_g, m_ln_v_b, m_sp_w, m_sp_b, m_w_out_ab, m_w_qkv, m_w_o, m_g_ffn2, m_w_ffn2_gate, m_w_ffn2_up, m_w_ffn2_down, m_g_final, v_g_ffn1, v_w_ffn1_gate, v_w_ffn1_up, v_w_ffn1_down, v_g_mix, v_w_in_ab, v_conv_w, v_conv_b, v_ln_a_g, v_ln_a_b, v_ln_v_g, v_ln_v_b, v_sp_w, v_sp_b, v_w_out_ab, v_w_qkv, v_w_o, v_g_ffn2, v_w_ffn2_gate, v_w_ffn2_up, v_w_ffn2_down, v_g_final):
    given = dict(x=x, g_ffn1=g_ffn1, w_ffn1_gate=w_ffn1_gate, w_ffn1_up=w_ffn1_up, w_ffn1_down=w_ffn1_down, g_mix=g_mix, w_in_ab=w_in_ab, conv_w=conv_w, conv_b=conv_b, ln_a_g=ln_a_g, ln_a_b=ln_a_b, ln_v_g=ln_v_g, ln_v_b=ln_v_b, sp_w=sp_w, sp_b=sp_b, w_out_ab=w_out_ab, w_qkv=w_qkv, w_o=w_o, g_ffn2=g_ffn2, w_ffn2_gate=w_ffn2_gate, w_ffn2_up=w_ffn2_up, w_ffn2_down=w_ffn2_down, g_final=g_final, loss_target=loss_target, m_g_ffn1=m_g_ffn1, m_w_ffn1_gate=m_w_ffn1_gate, m_w_ffn1_up=m_w_ffn1_up, m_w_ffn1_down=m_w_ffn1_down, m_g_mix=m_g_mix, m_w_in_ab=m_w_in_ab, m_conv_w=m_conv_w, m_conv_b=m_conv_b, m_ln_a_g=m_ln_a_g, m_ln_a_b=m_ln_a_b, m_ln_v_g=m_ln_v_g, m_ln_v_b=m_ln_v_b, m_sp_w=m_sp_w, m_sp_b=m_sp_b, m_w_out_ab=m_w_out_ab, m_w_qkv=m_w_qkv, m_w_o=m_w_o, m_g_ffn2=m_g_ffn2, m_w_ffn2_gate=m_w_ffn2_gate, m_w_ffn2_up=m_w_ffn2_up, m_w_ffn2_down=m_w_ffn2_down, m_g_final=m_g_final, v_g_ffn1=v_g_ffn1, v_w_ffn1_gate=v_w_ffn1_gate, v_w_ffn1_up=v_w_ffn1_up, v_w_ffn1_down=v_w_ffn1_down, v_g_mix=v_g_mix, v_w_in_ab=v_w_in_ab, v_conv_w=v_conv_w, v_conv_b=v_conv_b, v_ln_a_g=v_ln_a_g, v_ln_a_b=v_ln_a_b, v_ln_v_g=v_ln_v_g, v_ln_v_b=v_ln_v_b, v_sp_w=v_sp_w, v_sp_b=v_sp_b, v_w_out_ab=v_w_out_ab, v_w_qkv=v_w_qkv, v_w_o=v_w_o, v_g_ffn2=v_g_ffn2, v_w_ffn2_gate=v_w_ffn2_gate, v_w_ffn2_up=v_w_ffn2_up, v_w_ffn2_down=v_w_ffn2_down, v_g_final=v_g_final)
    weights = {n: given[n] for n in TWIN_WEIGHTS}
    shared = {n: given[n] for n in SHARED_INPUTS}
    per_example = {n: given[n] for n in ['x']}
    grad_fn = _jax.value_and_grad(_loss, argnums=(0, 1))

    def one_microbatch(ex, loss_target):
        ex = dict(ex)
        diff = ex.pop(TWIN_DIFF_INPUT)
        return grad_fn(weights, diff, {**shared, **ex}, loss_target)

    if N_MICROBATCH == 1:
        loss, (grad_w, grad_x) = one_microbatch(per_example, given["loss_target"])
    else:
        def body(carry, xs):
            loss_sum, grad_sum = carry
            l_k, (gw_k, gx_k) = one_microbatch(xs[0], xs[1])
            with _jax.named_scope("update"):
                return (loss_sum + l_k, _jax.tree.map(_jnp.add, grad_sum, gw_k)), gx_k

        init = (_jnp.zeros((), _jnp.float32), _jax.tree.map(_jnp.zeros_like, weights))
        (loss, grad_w), grad_x = _jax.lax.scan(body, init, (per_example, given["loss_target"]))
    with _jax.named_scope("update"):
        delta_w, new_m, new_v = {}, {}, {}
        for n in TWIN_WEIGHTS:
            delta_w[n], new_m[n], new_v[n] = _adamw(weights[n], grad_w[n], given["m_" + n], given["v_" + n])
    return (loss, grad_x, *[grad_w[n] for n in TWIN_WEIGHTS], *[delta_w[n] for n in TWIN_WEIGHTS],
            *[new_m[n] for n in TWIN_WEIGHTS], *[new_v[n] for n in TWIN_WEIGHTS])
```

```python
import functools

import jax
import jax.numpy as jnp
from jax import lax
from jax.experimental import pallas as pl
from jax.experimental.pallas import tpu as pltpu

F32 = jnp.float32
BF16 = jnp.bfloat16
EPS = 1e-6
HEAD_DIM = 64
CONV_WIDTH = 31
CHUNK = 128
KBLK = 128
LANES = 128
HALO = 32
ADAM_LR, ADAM_B1, ADAM_B2, ADAM_EPS, ADAM_WD, ADAM_STEP = 0.001, 0.9, 0.999, 1e-08, 0.01, 10
VMEM_LIMIT = 56 * 1024 * 1024
MESH = pl.DeviceIdType.MESH
N_CHIPS = 4
N_DEV = 8


def _cparams(sem):
    return pltpu.CompilerParams(dimension_semantics=sem, vmem_limit_bytes=VMEM_LIMIT)


def _nt(a, b):
    return lax.dot_general(a, b, (((1,), (1,)), ((), ())), preferred_element_type=F32)


def _tn(a, b):
    return lax.dot_general(a, b, (((0,), (0,)), ((), ())), preferred_element_type=F32)


def _nn(a, b):
    return jnp.dot(a, b, preferred_element_type=F32)


def _sigmoid(x):
    return 1.0 / (1.0 + jnp.exp(-x))


def _tile(t, want):
    if t <= want:
        return t
    for cand in range(want - want % 8, 7, -8):
        if t % cand == 0:
            return cand
    raise ValueError((t, want))


def rmsnorm_fwd(x, g, name):
    t, d = x.shape
    tm = _tile(t, 512)

    def body(x_ref, g_ref, h_ref):
        xv = x_ref[...]
        r = lax.rsqrt(jnp.mean(xv * xv, axis=-1, keepdims=True) + EPS)
        h_ref[...] = (xv * r * g_ref[...]).astype(BF16)

    return pl.pallas_call(
        body, name=name, grid=(t // tm,),
        in_specs=[pl.BlockSpec((tm, d), lambda i: (i, 0)), pl.BlockSpec((1, d), lambda i: (0, 0))],
        out_specs=pl.BlockSpec((tm, d), lambda i: (i, 0)),
        out_shape=jax.ShapeDtypeStruct((t, d), BF16),
        compiler_params=_cparams(("parallel",)),
    )(x, g)


def colmm(h, ws, nu, out_dtype, name):
    t, k = h.shape
    j, _, nj = ws[0].shape
    per = nj // nu
    units = j * per
    tm = _tile(t, 512)
    nw = len(ws)

    def body(*refs):
        h_ref = refs[0]
        hv = h_ref[...]
        for n in range(nw):
            refs[1 + nw + n][0] = _nn(hv, refs[1 + n][0]).astype(out_dtype)

    w_spec = pl.BlockSpec((1, k, nu), lambda u, i: (u // per, 0, u % per))
    o_spec = pl.BlockSpec((1, tm, nu), lambda u, i: (u, i, 0))
    outs = pl.pallas_call(
        body, name=name, grid=(units, t // tm),
        in_specs=[pl.BlockSpec((tm, k), lambda u, i: (i, 0))] + [w_spec] * nw,
        out_specs=[o_spec] * nw,
        out_shape=[jax.ShapeDtypeStruct((units, t, nu), out_dtype)] * nw,
        compiler_params=_cparams(("parallel", "parallel")),
    )(h, *ws)
    return outs


def rowmm(a_list, w, resid, scale, name):
    swiglu = len(a_list) == 2
    u_n, t, ku = a_list[0].shape
    n = w.shape[2]
    tm = _tile(t, 256)

    def body(*refs):
        a_refs = refs[:len(a_list)]
        w_ref, r_ref, o_ref = refs[len(a_list):]
        acc = jnp.zeros((tm, n), F32)
        for u in range(u_n):
            if swiglu:
                gv = a_refs[0][u].astype(F32)
                av = (gv * _sigmoid(gv) * a_refs[1][u].astype(F32)).astype(BF16)
            else:
                av = a_refs[0][u]
            acc = acc + _nn(av, w_ref[u])
        o_ref[...] = r_ref[...] + scale * acc

    a_spec = pl.BlockSpec((u_n, tm, ku), lambda i: (0, i, 0))
    return pl.pallas_call(
        body, name=name, grid=(t // tm,),
        in_specs=[a_spec] * len(a_list) + [pl.BlockSpec((u_n, ku, n), lambda i: (0, 0, 0)),
                                           pl.BlockSpec((tm, n), lambda i: (i, 0))],
        out_specs=pl.BlockSpec((tm, n), lambda i: (i, 0)),
        out_shape=jax.ShapeDtypeStruct((t, n), F32),
        compiler_params=_cparams(("parallel",)),
    )(*a_list, w, resid)


def rowmm_t(dyb, w, scale, out_dtype, name, gu=None):
    t, n = dyb.shape
    u_n, ku, _ = w.shape
    tm = _tile(t, 512)

    def body(*refs):
        if gu is None:
            dy_ref, w_ref, o_ref = refs
            o_ref[0] = (scale * _nt(dy_ref[...], w_ref[0])).astype(out_dtype)
        else:
            dy_ref, w_ref, g_ref, u_ref, dg_ref, du_ref, a_ref = refs
            dact = scale * _nt(dy_ref[...], w_ref[0])
            gv = g_ref[0].astype(F32)
            uv = u_ref[0].astype(F32)
            s = _sigmoid(gv)
            silu = gv * s
            dg_ref[0] = (dact * uv * (s * (1.0 + gv * (1.0 - s)))).astype(BF16)
            du_ref[0] = (dact * silu).astype(BF16)
            a_ref[0] = (silu * uv).astype(BF16)

    blk = pl.BlockSpec((1, tm, ku), lambda u, i: (u, i, 0))
    in_specs = [pl.BlockSpec((tm, n), lambda u, i: (i, 0)), pl.BlockSpec((1, ku, n), lambda u, i: (u, 0, 0))]
    if gu is None:
        return pl.pallas_call(
            body, name=name, grid=(u_n, t // tm), in_specs=in_specs, out_specs=blk,
            out_shape=jax.ShapeDtypeStruct((u_n, t, ku), out_dtype),
            compiler_params=_cparams(("parallel", "parallel")),
        )(dyb, w)
    return pl.pallas_call(
        body, name=name, grid=(u_n, t // tm), in_specs=in_specs + [blk, blk], out_specs=[blk] * 3,
        out_shape=[jax.ShapeDtypeStruct((u_n, t, ku), BF16)] * 3,
        compiler_params=_cparams(("parallel", "parallel")),
    )(dyb, w, *gu)


def colmm_t(dzs, ws, nu, x, g, dy_in, name):
    t, k = x.shape
    j, _, nj = ws[0].shape
    per = nj // nu
    units = j * per
    nw = len(ws)
    tm = _tile(t, 256)

    def body(*refs):
        dz_refs = refs[:nw]
        w_refs = refs[nw:2 * nw]
        x_ref, g_ref, dy_ref, dx_ref, dxb_ref, dg_ref = refs[2 * nw:]
        i = pl.program_id(0)
        dh = jnp.zeros((tm, k), F32)
        for n in range(nw):
            for u in range(units):
                wv = w_refs[n][u // per, :, (u % per) * nu:(u % per + 1) * nu]
                dh = dh + _nt(dz_refs[n][u], wv)
        xv = x_ref[...]
        gv = g_ref[...]
        r = lax.rsqrt(jnp.mean(xv * xv, axis=-1, keepdims=True) + EPS)
        uu = dh * gv
        dx = dy_ref[...] + r * uu - xv * (r * r * r * jnp.mean(uu * xv, axis=-1, keepdims=True))
        dx_ref[...] = dx
        dxb_ref[...] = dx.astype(BF16)
        part = jnp.sum(dh * (xv * r), axis=0, keepdims=True)

        @pl.when(i == 0)
        def _():
            dg_ref[...] = part

        @pl.when(i > 0)
        def _():
            dg_ref[...] += part

    dz_spec = pl.BlockSpec((units, tm, nu), lambda i: (0, i, 0))
    w_spec = pl.BlockSpec((j, k, nj), lambda i: (0, 0, 0))
    row = pl.BlockSpec((tm, k), lambda i: (i, 0))
    vec = pl.BlockSpec((1, k), lambda i: (0, 0))
    return pl.pallas_call(
        body, name=name, grid=(t // tm,),
        in_specs=[dz_spec] * nw + [w_spec] * nw + [row, vec, row],
        out_specs=[row, row, vec],
        out_shape=[jax.ShapeDtypeStruct((t, k), F32), jax.ShapeDtypeStruct((t, k), BF16),
                   jax.ShapeDtypeStruct((1, k), F32)],
        compiler_params=_cparams(("arbitrary",)),
    )(*dzs, *ws, x, g, dy_in)


def dw_col(h, dzs, j, nu, name):
    t, k = h.shape
    units = dzs[0].shape[0]
    per = units // j
    nw = len(dzs)
    tt = _tile(t, 512)

    def body(*refs):
        h_ref = refs[0]
        s = pl.program_id(1)
        hv = h_ref[...]
        for n in range(nw):
            part = _tn(hv, refs[1 + n][0])
            o_ref = refs[1 + nw + n]

            @pl.when(s == 0)
            def _():
                o_ref[0] = part

            @pl.when(s > 0)
            def _():
                o_ref[0] += part

    return pl.pallas_call(
        body, name=name, grid=(units, t // tt),
        in_specs=[pl.BlockSpec((tt, k), lambda u, s: (s, 0))] + [pl.BlockSpec((1, tt, nu), lambda u, s: (u, s, 0))] * nw,
        out_specs=[pl.BlockSpec((1, k, nu), lambda u, s: (u // per, 0, u % per))] * nw,
        out_shape=[jax.ShapeDtypeStruct((j, k, per * nu), F32)] * nw,
        compiler_params=_cparams(("parallel", "arbitrary")),
    )(h, *dzs)


def dw_row(a, dyb, scale, name):
    u_n, t, ku = a.shape
    n = dyb.shape[1]
    tt = _tile(t, 512)

    def body(a_ref, dy_ref, o_ref):
        s = pl.program_id(1)
        part = scale * _tn(a_ref[0], dy_ref[...])

        @pl.when(s == 0)
        def _():
            o_ref[0] = part

        @pl.when(s > 0)
        def _():
            o_ref[0] += part

    return pl.pallas_call(
        body, name=name, grid=(u_n, t // tt),
        in_specs=[pl.BlockSpec((1, tt, ku), lambda u, s: (u, s, 0)), pl.BlockSpec((tt, n), lambda u, s: (s, 0))],
        out_specs=pl.BlockSpec((1, ku, n), lambda u, s: (u, 0, 0)),
        out_shape=jax.ShapeDtypeStruct((u_n, ku, n), F32),
        compiler_params=_cparams(("parallel", "arbitrary")),
    )(a, dyb)


def loss_head(x, g, target):
    t, d = x.shape
    tm = _tile(t, 256)

    def body(x_ref, g_ref, t_ref, loss_ref, dx_ref, dxb_ref, dg_ref):
        i = pl.program_id(0)
        xv = x_ref[...]
        gv = g_ref[...]
        r = lax.rsqrt(jnp.mean(xv * xv, axis=-1, keepdims=True) + EPS)
        xh = xv * r
        err = xh * gv - t_ref[...]
        dy = err * (1.0 / d)
        uu = dy * gv
        dx = r * uu - xv * (r * r * r * jnp.mean(uu * xv, axis=-1, keepdims=True))
        dx_ref[...] = dx
        dxb_ref[...] = dx.astype(BF16)
        dg_part = jnp.sum(dy * xh, axis=0, keepdims=True)
        row = jnp.sum(err * err, axis=-1, keepdims=True) * (0.5 / d)
        l_part = jnp.zeros((8, LANES), F32) + jnp.sum(row, axis=0, keepdims=True)

        @pl.when(i == 0)
        def _():
            dg_ref[...] = dg_part
            loss_ref[...] = l_part

        @pl.when(i > 0)
        def _():
            dg_ref[...] += dg_part
            loss_ref[...] += l_part

    row = pl.BlockSpec((tm, d), lambda i: (i, 0))
    vec = pl.BlockSpec((1, d), lambda i: (0, 0))
    return pl.pallas_call(
        body, name="loss_head", grid=(t // tm,),
        in_specs=[row, vec, row],
        out_specs=[pl.BlockSpec((8, LANES), lambda i: (0, 0)), row, row, vec],
        out_shape=[jax.ShapeDtypeStruct((8, LANES), F32), jax.ShapeDtypeStruct((t, d), F32),
                   jax.ShapeDtypeStruct((t, d), BF16), jax.ShapeDtypeStruct((1, d), F32)],
        compiler_params=_cparams(("arbitrary",)),
    )(x, g, target)


def _split(v):
    hi = v.astype(BF16)
    lo = (v - hi.astype(F32)).astype(BF16)
    return hi, lo


def _keysum(v, m_ext):
    hi, lo = _split(v)
    cs = _nn(hi, m_ext) + _nn(lo, m_ext)
    return cs[:, :KBLK], cs[:, KBLK:]


def _softplus_parts(z):
    sp = jnp.maximum(z, 0.0) + jnp.log(1.0 + jnp.exp(-jnp.abs(z)))
    return sp, z - sp


def _sum_matrices():
    r = lax.broadcasted_iota(jnp.int32, (KBLK, 2 * KBLK), 0)
    c = lax.broadcasted_iota(jnp.int32, (KBLK, 2 * KBLK), 1)
    suffix = jnp.where((r > c) | (c >= KBLK), 1.0, 0.0).astype(BF16)
    prefix = jnp.where((r <= c) | (c >= KBLK), 1.0, 0.0).astype(BF16)
    return suffix, prefix


def attn_fwd(qkv, n_seq, seq):
    t = qkv.shape[1]
    n_pairs = (qkv.shape[0] // 3) * 2
    nq = seq // KBLK
    suffix_m, _ = _sum_matrices()

    def body(q_ref, k_ref, v_ref, m_ref, o_ref, tot_ref):
        qi = pl.program_id(2)
        lane = lax.broadcasted_iota(jnp.int32, (KBLK, LANES), 1)
        is_a = lane < HEAD_DIM
        q2 = q_ref[0] * jnp.asarray(HEAD_DIM ** -0.5, BF16)
        qs = (jnp.where(is_a, q2, jnp.zeros_like(q2)), jnp.where(is_a, jnp.zeros_like(q2), q2))
        m_ext = m_ref[...]
        row = lax.broadcasted_iota(jnp.int32, (KBLK, KBLK), 0)
        col = lax.broadcasted_iota(jnp.int32, (KBLK, KBLK), 1)
        diag_mask = col < row

        def block(kj, carry, mask):
            off = pl.multiple_of(kj * KBLK, KBLK)
            k2 = k_ref[0, pl.ds(off, KBLK), :]
            v2 = v_ref[0, pl.ds(off, KBLK), :]
            out = []
            for h in range(2):
                rem, acc = carry[h]
                z = _nt(qs[h], k2)
                sp, ls = _softplus_parts(z)
                lk = -sp if mask is None else jnp.where(mask, -sp, 0.0)
                suf, total = _keysum(lk, m_ext)
                a = jnp.exp(ls + suf + rem)
                if mask is not None:
                    a = jnp.where(mask, a, 0.0)
                out.append((rem + total, acc + _nn(a.astype(BF16), v2)))
            return tuple(out)

        zero = jnp.zeros((KBLK, LANES), F32)
        carry = block(qi, ((zero, zero), (zero, zero)), diag_mask)
        carry = lax.fori_loop(0, qi, lambda it, c: block(qi - 1 - it, c, None), carry)
        o_ref[0] = jnp.where(is_a, carry[0][1], carry[1][1]).astype(BF16)
        tot_ref[...] = jnp.where(is_a, carry[0][0], carry[1][0])

    upp = qkv.shape[0] // 3
    return pl.pallas_call(
        body, name="attn_fwd", grid=(n_seq, n_pairs, nq),
        in_specs=[pl.BlockSpec((1, KBLK, LANES), lambda b, p, i: (p // 2, b * nq + i, p % 2)),
                  pl.BlockSpec((1, seq, LANES), lambda b, p, i: (upp + p // 2, b, p % 2)),
                  pl.BlockSpec((1, seq, LANES), lambda b, p, i: (2 * upp + p // 2, b, p % 2)),
                  pl.BlockSpec((KBLK, 2 * KBLK), lambda b, p, i: (0, 0))],
        out_specs=[pl.BlockSpec((1, KBLK, LANES), lambda b, p, i: (p // 2, b * nq + i, p % 2)),
                   pl.BlockSpec((KBLK, LANES), lambda b, p, i: (b * nq + i, p))],
        out_shape=[jax.ShapeDtypeStruct((upp, t, 2 * LANES), BF16), jax.ShapeDtypeStruct((t, n_pairs * LANES), F32)],
        compiler_params=_cparams(("parallel", "parallel", "parallel")),
    )(qkv, qkv, qkv, suffix_m)


def attn_bwd(qkv, do, tot, n_seq, seq):
    t = qkv.shape[1]
    upp = qkv.shape[0] // 3
    n_pairs = upp * 2
    nq = seq // KBLK
    _, prefix_m = _sum_matrices()
    scale = HEAD_DIM ** -0.5

    def body(q_ref, k_ref, v_ref, do_ref, tot_ref, m_ref, dq_ref, dk_ref, dv_ref, dk_acc, dv_acc):
        qi = pl.program_id(2)
        lane = lax.broadcasted_iota(jnp.int32, (KBLK, LANES), 1)
        is_a = lane < HEAD_DIM

        def halves(v2):
            z2 = jnp.zeros_like(v2)
            return jnp.where(is_a, v2, z2), jnp.where(is_a, z2, v2)

        qs = halves(q_ref[0] * jnp.asarray(scale, BF16))
        dos = halves(do_ref[0])
        tot2 = tot_ref[...]
        swapped = pltpu.roll(tot2, HEAD_DIM, 1)
        tots = (jnp.where(is_a, tot2, swapped), jnp.where(is_a, swapped, tot2))
        m_ext = m_ref[...]
        row = lax.broadcasted_iota(jnp.int32, (KBLK, KBLK), 0)
        col = lax.broadcasted_iota(jnp.int32, (KBLK, KBLK), 1)
        diag_mask = col < row

        @pl.when(qi == 0)
        def _():
            dk_acc[...] = jnp.zeros_like(dk_acc)
            dv_acc[...] = jnp.zeros_like(dv_acc)

        def block(kj, carry, mask):
            off = pl.multiple_of(kj * KBLK, KBLK)
            k2 = k_ref[0, pl.ds(off, KBLK), :]
            v2 = v_ref[0, pl.ds(off, KBLK), :]
            ks = halves(k2)
            dq = carry[2]
            dk_part = jnp.zeros((KBLK, LANES), F32)
            dv_part = jnp.zeros((KBLK, LANES), F32)
            out = []
            for h in range(2):
                pre, gpre = carry[h]
                z = _nt(qs[h], k2)
                sp, ls = _softplus_parts(z)
                lk = -sp if mask is None else jnp.where(mask, -sp, 0.0)
                pin, ptot = _keysum(lk, m_ext)
                a = jnp.exp(ls + (tots[h] - (pre + pin)))
                if mask is not None:
                    a = jnp.where(mask, a, 0.0)
                g = a * _nt(dos[h], v2)
                gin, gtot = _keysum(g, m_ext)
                beta = jnp.exp(ls)
                dz = g - beta * (gpre + gin)
                if mask is not None:
                    dz = jnp.where(mask, dz, 0.0)
                dzb = dz.astype(BF16)
                dq = dq + _nn(dzb, ks[h])
                dk_part = dk_part + _tn(dzb, qs[h])
                dv_part = dv_part + _tn(a.astype(BF16), dos[h])
                out.append((pre + ptot, gpre + gtot))
            dk_acc[pl.ds(off, KBLK), :] += dk_part
            dv_acc[pl.ds(off, KBLK), :] += dv_part
            return (out[0], out[1], dq)

        zero = jnp.zeros((KBLK, LANES), F32)
        carry = lax.fori_loop(0, qi, lambda kj, c: block(kj, c, None), ((zero, zero), (zero, zero), zero))
        carry = block(qi, carry, diag_mask)
        dq_ref[0] = (carry[2] * scale).astype(BF16)

        @pl.when(qi == nq - 1)
        def _():
            dk_ref[0] = dk_acc[...].astype(BF16)
            dv_ref[0] = dv_acc[...].astype(BF16)

    qblk = lambda b, p, i: (p // 2, b * nq + i, p % 2)
    kv_out = pl.BlockSpec((1, seq, LANES), lambda b, p, i: (p // 2, b, p % 2))
    shp = jax.ShapeDtypeStruct((upp, t, 2 * LANES), BF16)
    return pl.pallas_call(
        body, name="attn_bwd", grid=(n_seq, n_pairs, nq),
        in_specs=[pl.BlockSpec((1, KBLK, LANES), qblk),
                  pl.BlockSpec((1, seq, LANES), lambda b, p, i: (upp + p // 2, b, p % 2)),
                  pl.BlockSpec((1, seq, LANES), lambda b, p, i: (2 * upp + p // 2, b, p % 2)),
                  pl.BlockSpec((1, KBLK, LANES), qblk),
                  pl.BlockSpec((KBLK, LANES), lambda b, p, i: (b * nq + i, p)),
                  pl.BlockSpec((KBLK, 2 * KBLK), lambda b, p, i: (0, 0))],
        out_specs=[pl.BlockSpec((1, KBLK, LANES), qblk), kv_out, kv_out],
        out_shape=[shp, shp, shp],
        scratch_shapes=[pltpu.VMEM((seq, LANES), F32), pltpu.VMEM((seq, LANES), F32)],
        compiler_params=_cparams(("parallel", "parallel", "arbitrary")),
    )(qkv, qkv, qkv, do, tot, prefix_m)


def _ln_stats(v):
    mu = jnp.mean(v, axis=-1, keepdims=True)
    vc = v - mu
    rstd = lax.rsqrt(jnp.mean(vc * vc, axis=-1, keepdims=True) + EPS)
    return vc * rstd, rstd


def _glu_into(a0_ref, av_ref, ag_ref, hv_ref, hg_ref, first):
    hv = hv_ref[0].astype(F32)
    hg = hg_ref[0].astype(F32)
    a0_ref[0:HALO, :] = jnp.where(first, 0.0, hv * _sigmoid(hg))
    av = av_ref[0].astype(F32)
    ag = ag_ref[0].astype(F32)
    a0_ref[HALO:, :] = av * _sigmoid(ag)


def _tril_mask():
    r = lax.broadcasted_iota(jnp.int32, (CHUNK, CHUNK), 0)
    c = lax.broadcasted_iota(jnp.int32, (CHUNK, CHUNK), 1)
    return c <= r


def mix_fwd(z, conv_w, conv_b, ln_a_g, ln_a_b, ln_v_g, ln_v_b, sp_w, sp_bt, seq):
    _, t, c = z.shape
    tm = _tile(seq, 512)
    tiles_per_seq = seq // tm
    groups = c // LANES
    hb = tm // HALO

    def body(av_ref, ag_ref, u_ref, v_ref, hv_ref, hg_ref, cw_ref, cb_ref, lag_ref, lab_ref, lvg_ref, lvb_ref,
             spw_ref, spb_ref, cat_ref, a1_ref, a0_ref):
        i = pl.program_id(0)
        _glu_into(a0_ref, av_ref, ag_ref, hv_ref, hg_ref, i % tiles_per_seq == 0)
        acc = jnp.zeros((tm, c), F32) + cb_ref[...]
        for k in range(CONV_WIDTH):
            acc = acc + cw_ref[k:k + 1, :] * a0_ref[pl.ds(HALO - (CONV_WIDTH - 1) + k, tm), :]
        a1_ref[...] = acc
        xh, _ = _ln_stats(acc)
        a2 = xh * lag_ref[...] + lab_ref[...]
        a3 = (a2 * _sigmoid(a2)).astype(BF16)
        half = c // 2
        cat_ref[0] = a3[:, :half]
        cat_ref[1] = a3[:, half:]
        tril = _tril_mask()
        for g in range(groups):
            sl = slice(g * LANES, (g + 1) * LANES)
            xh, _ = _ln_stats(v_ref[0][:, sl].astype(F32))
            vn = (xh * lvg_ref[:, sl] + lvb_ref[:, sl]).astype(BF16)
            w = jnp.where(tril, spw_ref[g], 0.0).astype(BF16)
            bias = spb_ref[:, g:g + 1]
            for ch in range(tm // CHUNK):
                rows = slice(ch * CHUNK, (ch + 1) * CHUNK)
                vs = _nn(w, vn[rows]) + bias
                bo = (u_ref[0][rows, sl].astype(F32) * vs).astype(BF16)
                cat_ref[2 + (g * LANES) // half, rows, (g * LANES) % half:(g * LANES) % half + LANES] = bo

    unit = lambda u: pl.BlockSpec((1, tm, c), lambda i: (u, i, 0))
    halo = lambda u: pl.BlockSpec((1, HALO, c), lambda i: (u, jnp.maximum(i * hb - 1, 0), 0))
    vec = pl.BlockSpec((1, c), lambda i: (0, 0))
    return pl.pallas_call(
        body, name="mix_fwd", grid=(t // tm,),
        in_specs=[unit(0), unit(1), unit(2), unit(3), halo(0), halo(1),
                  pl.BlockSpec((CONV_WIDTH, c), lambda i: (0, 0)), vec, vec, vec, vec, vec,
                  pl.BlockSpec((groups, CHUNK, CHUNK), lambda i: (0, 0, 0)),
                  pl.BlockSpec((CHUNK, groups), lambda i: (0, 0))],
        out_specs=[pl.BlockSpec((4, tm, c // 2), lambda i: (0, i, 0)), pl.BlockSpec((tm, c), lambda i: (i, 0))],
        out_shape=[jax.ShapeDtypeStruct((4, t, c // 2), BF16), jax.ShapeDtypeStruct((t, c), F32)],
        scratch_shapes=[pltpu.VMEM((HALO + tm, c), F32)],
        compiler_params=_cparams(("parallel",)),
    )(z, z, z, z, z, z, conv_w, conv_b, ln_a_g, ln_a_b, ln_v_g, ln_v_b, sp_w, sp_bt)


def mix_bwd_point(dcat, z, a1, ln_a_g, ln_a_b, ln_v_g, ln_v_b, sp_w, sp_wt, sp_bt, seq):
    _, t, c = z.shape
    tm = _tile(seq, 512)
    groups = c // LANES
    half = c // 2

    def body(dc_ref, u_ref, v_ref, a1_ref, lag_ref, lab_ref, lvg_ref, lvb_ref, spw_ref, spwt_ref, spb_ref,
             dz_ref, da1_ref, dcb_ref, dlag_ref, dlab_ref, dlvg_ref, dlvb_ref, dspw_ref, dspb_ref):
        i = pl.program_id(0)
        last = pl.num_programs(0) - 1

        @pl.when(i == 0)
        def _():
            for r in (dcb_ref, dlag_ref, dlab_ref, dlvg_ref, dlvb_ref, dspw_ref, dspb_ref):
                r[...] = jnp.zeros_like(r)

        da3 = jnp.concatenate([dc_ref[0], dc_ref[1]], axis=-1)
        xh, rstd = _ln_stats(a1_ref[...])
        a2 = xh * lag_ref[...] + lab_ref[...]
        s = _sigmoid(a2)
        da2 = da3 * (s * (1.0 + a2 * (1.0 - s)))
        dlag_ref[...] += jnp.sum(da2 * xh, axis=0, keepdims=True)
        dlab_ref[...] += jnp.sum(da2, axis=0, keepdims=True)
        dxh = da2 * lag_ref[...]
        da1 = rstd * (dxh - jnp.mean(dxh, axis=-1, keepdims=True) - xh * jnp.mean(dxh * xh, axis=-1, keepdims=True))
        da1_ref[...] = da1
        dcb_ref[...] += jnp.sum(da1, axis=0, keepdims=True)

        tril = _tril_mask()
        for g in range(groups):
            sl = slice(g * LANES, (g + 1) * LANES)
            xh, rstd = _ln_stats(v_ref[0][:, sl].astype(F32))
            lg = lvg_ref[:, sl]
            vnb = (xh * lg + lvb_ref[:, sl]).astype(BF16)
            w = jnp.where(tril, spw_ref[g], 0.0).astype(BF16)
            wt = jnp.where(tril.T, spwt_ref[g], 0.0).astype(BF16)
            bias = spb_ref[:, g:g + 1]
            dbo_all = dc_ref[2 + (g * LANES) // half][:, (g * LANES) % half:(g * LANES) % half + LANES]
            dvn_parts = []
            dw_acc = jnp.zeros((CHUNK, CHUNK), F32)
            db_acc = jnp.zeros((CHUNK, LANES), F32)
            for ch in range(tm // CHUNK):
                rows = slice(ch * CHUNK, (ch + 1) * CHUNK)
                vs = _nn(w, vnb[rows]) + bias
                dbo = dbo_all[rows]
                uv = u_ref[0][rows, sl].astype(F32)
                dz_ref[0, rows, sl] = (dbo * vs).astype(BF16)
                dvs = dbo * uv
                dvsb = dvs.astype(BF16)
                dvn_parts.append(_nn(wt, dvsb))
                dw_acc = dw_acc + _nt(dvsb, vnb[rows])
                db_acc = db_acc + dvs
            dvn = jnp.concatenate(dvn_parts, axis=0)
            dspw_ref[g] += jnp.where(tril, dw_acc, 0.0)
            dspb_ref[g] += db_acc
            dlvg_ref[:, sl] += jnp.sum(dvn * xh, axis=0, keepdims=True)
            dlvb_ref[:, sl] += jnp.sum(dvn, axis=0, keepdims=True)
            dxh = dvn * lg
            dv = rstd * (dxh - jnp.mean(dxh, axis=-1, keepdims=True) - xh * jnp.mean(dxh * xh, axis=-1, keepdims=True))
            dz_ref[1, :, sl] = dv.astype(BF16)

        @pl.when(i == last)
        def _():
            for g in range(groups):
                dspb_ref[g] = jnp.zeros((CHUNK, LANES), F32) + jnp.sum(dspb_ref[g], axis=-1, keepdims=True)

    unit = lambda u: pl.BlockSpec((1, tm, c), lambda i: (u, i, 0))
    vec = pl.BlockSpec((1, c), lambda i: (0, 0))
    sq = pl.BlockSpec((groups, CHUNK, CHUNK), lambda i: (0, 0, 0))
    vshape = jax.ShapeDtypeStruct((1, c), F32)
    sshape = jax.ShapeDtypeStruct((groups, CHUNK, CHUNK), F32)
    return pl.pallas_call(
        body, name="mix_bwd_point", grid=(t // tm,),
        in_specs=[pl.BlockSpec((4, tm, half), lambda i: (0, i, 0)), unit(2), unit(3),
                  pl.BlockSpec((tm, c), lambda i: (i, 0)), vec, vec, vec, vec, sq, sq,
                  pl.BlockSpec((CHUNK, groups), lambda i: (0, 0))],
        out_specs=[pl.BlockSpec((2, tm, c), lambda i: (1, i, 0)), pl.BlockSpec((tm, c), lambda i: (i, 0)),
                   vec, vec, vec, vec, vec, sq, sq],
        out_shape=[jax.ShapeDtypeStruct((4, t, c), BF16), jax.ShapeDtypeStruct((t, c), F32),
                   vshape, vshape, vshape, vshape, vshape, sshape, sshape],
        compiler_params=_cparams(("arbitrary",)),
    )(dcat, z, z, a1, ln_a_g, ln_a_b, ln_v_g, ln_v_b, sp_w, sp_wt, sp_bt)


def mix_bwd_conv(dz, da1, z, conv_w, seq):
    _, t, c = z.shape
    tm = _tile(seq, 512)
    tiles_per_seq = seq // tm
    hb = tm // HALO
    n_halo_blocks = t // HALO

    def body(dz_in_ref, d_ref, dh_ref, av_ref, ag_ref, hv_ref, hg_ref, cw_ref, dz_ref, dcw_ref, a0_ref, d1_ref):
        del dz_in_ref
        i = pl.program_id(0)
        _glu_into(a0_ref, av_ref, ag_ref, hv_ref, hg_ref, i % tiles_per_seq == 0)
        d1_ref[0:tm, :] = d_ref[...]
        d1_ref[tm:, :] = jnp.where((i + 1) % tiles_per_seq == 0, 0.0, dh_ref[...])

        @pl.when(i == 0)
        def _():
            dcw_ref[...] = jnp.zeros_like(dcw_ref)

        d1 = d_ref[...]
        da0 = jnp.zeros((tm, c), F32)
        for k in range(CONV_WIDTH):
            back = CONV_WIDTH - 1 - k
            da0 = da0 + cw_ref[k:k + 1, :] * d1_ref[pl.ds(back, tm), :]
            dcw_ref[k:k + 1, :] += jnp.sum(d1 * a0_ref[pl.ds(HALO - back, tm), :], axis=0, keepdims=True)
        av = av_ref[0].astype(F32)
        s = _sigmoid(ag_ref[0].astype(F32))
        dz_ref[0] = (da0 * s).astype(BF16)
        dz_ref[1] = (da0 * av * s * (1.0 - s)).astype(BF16)

    unit = lambda u: pl.BlockSpec((1, tm, c), lambda i: (u, i, 0))
    halo = lambda u: pl.BlockSpec((1, HALO, c), lambda i: (u, jnp.maximum(i * hb - 1, 0), 0))
    return pl.pallas_call(
        body, name="mix_bwd_conv", grid=(t // tm,),
        in_specs=[pl.BlockSpec(memory_space=pl.ANY), pl.BlockSpec((tm, c), lambda i: (i, 0)),
                  pl.BlockSpec((HALO, c), lambda i: (jnp.minimum((i + 1) * hb, n_halo_blocks - 1), 0)),
                  unit(0), unit(1), halo(0), halo(1), pl.BlockSpec((CONV_WIDTH, c), lambda i: (0, 0))],
        out_specs=[pl.BlockSpec((2, tm, c), lambda i: (0, i, 0)), pl.BlockSpec((CONV_WIDTH, c), lambda i: (0, 0))],
        out_shape=[jax.ShapeDtypeStruct(dz.shape, BF16), jax.ShapeDtypeStruct((CONV_WIDTH, c), F32)],
        scratch_shapes=[pltpu.VMEM((HALO + tm, c), F32), pltpu.VMEM((tm + HALO, c), F32)],
        input_output_aliases={0: 0},
        compiler_params=_cparams(("arbitrary",)),
    )(dz, da1, da1, z, z, z, z, conv_w)


CHIP_FLIPS = ((1, 0), (0, 1), (1, 1))
ANY = pl.BlockSpec(memory_space=pl.ANY)


def _place():
    return lax.axis_index("x"), lax.axis_index("y"), lax.axis_index("c")


def _flip(v, f):
    return 1 - v if f else v


def allgather_weights(shards, smalls):
    n, ns = len(shards), len(smalls)

    def body(*refs):
        ins, sins = refs[:n], refs[n:n + ns]
        outs, souts = refs[n + ns:2 * n + ns], refs[2 * n + ns:2 * n + 2 * ns]
        ici_send, ici_recv, d2d_send, d2d_recv, sm_send, sm_recv, loc = refs[2 * n + 2 * ns:]
        x, y, c = _place()
        k = 2 * x + y
        sibling = (x, y, 1 - c)
        pending = []
        for a in range(n):
            cp = pltpu.make_async_copy(ins[a], outs[a].at[k], loc.at[a])
            cp.start()
            pending.append(cp.wait)
        for a in range(ns):
            cp = pltpu.make_async_copy(sins[a], souts[a].at[k], loc.at[n + a])
            cp.start()
            pending.append(cp.wait)

        def half(a):
            hr = shards[a].shape[0] // 2
            return pl.ds(pl.multiple_of(c * hr, 16), hr)

        for a in range(n):
            for o, (fx, fy) in enumerate(CHIP_FLIPS):
                cp = pltpu.make_async_remote_copy(
                    src_ref=ins[a].at[half(a)], dst_ref=outs[a].at[k, half(a)],
                    send_sem=ici_send.at[3 * a + o], recv_sem=ici_recv.at[3 * a + o],
                    device_id=(_flip(x, fx), _flip(y, fy), c), device_id_type=MESH)
                cp.start()
                pending.append(cp.wait_send)
        for a in range(ns):
            for o, (fx, fy) in enumerate(CHIP_FLIPS):
                cp = pltpu.make_async_remote_copy(
                    src_ref=sins[a], dst_ref=souts[a].at[k],
                    send_sem=sm_send.at[3 * a + o], recv_sem=sm_recv.at[3 * a + o],
                    device_id=(_flip(x, fx), _flip(y, fy), c), device_id_type=MESH)
                cp.start()
                pending.append(cp.wait_send)
        for a in range(n):
            for o, (fx, fy) in enumerate(CHIP_FLIPS):
                kk = 2 * _flip(x, fx) + _flip(y, fy)
                landed = outs[a].at[kk, half(a)]
                pltpu.make_async_remote_copy(
                    src_ref=landed, dst_ref=landed, send_sem=ici_send.at[3 * a + o], recv_sem=ici_recv.at[3 * a + o],
                    device_id=sibling, device_id_type=MESH).wait_recv()
                cp = pltpu.make_async_remote_copy(
                    src_ref=landed, dst_ref=landed, send_sem=d2d_send.at[3 * a + o], recv_sem=d2d_recv.at[3 * a + o],
                    device_id=sibling, device_id_type=MESH)
                cp.start()
                pending.append(cp.wait_send)
        for a in range(n):
            hr = shards[a].shape[0] // 2
            other = pl.ds(pl.multiple_of((1 - c) * hr, 16), hr)
            for o, (fx, fy) in enumerate(CHIP_FLIPS):
                kk = 2 * _flip(x, fx) + _flip(y, fy)
                got = outs[a].at[kk, other]
                pltpu.make_async_remote_copy(
                    src_ref=got, dst_ref=got, send_sem=d2d_send.at[3 * a + o], recv_sem=d2d_recv.at[3 * a + o],
                    device_id=sibling, device_id_type=MESH).wait_recv()
        for a in range(ns):
            for o, (fx, fy) in enumerate(CHIP_FLIPS):
                kk = 2 * _flip(x, fx) + _flip(y, fy)
                got = souts[a].at[kk]
                pltpu.make_async_remote_copy(
                    src_ref=got, dst_ref=got, send_sem=sm_send.at[3 * a + o], recv_sem=sm_recv.at[3 * a + o],
                    device_id=sibling, device_id_type=MESH).wait_recv()
        for w in pending:
            w()

    out_shape = ([jax.ShapeDtypeStruct((N_CHIPS,) + s.shape, s.dtype) for s in shards]
                 + [jax.ShapeDtypeStruct((N_CHIPS,) + s.shape, s.dtype) for s in smalls])
    dma = pltpu.SemaphoreType.DMA
    res = pl.pallas_call(
        body, name="allgather_weights", in_specs=[ANY] * (n + ns), out_specs=[ANY] * (n + ns), out_shape=out_shape,
        scratch_shapes=[dma((3 * n,)), dma((3 * n,)), dma((3 * n,)), dma((3 * n,)), dma((3 * ns,)), dma((3 * ns,)),
                        dma((n + ns,))],
        compiler_params=pltpu.CompilerParams(has_side_effects=True),
    )(*shards, *smalls)
    return res[:n], res[n:]


def rs_exchange(grads):
    n = len(grads)

    def body(*refs):
        ins, outs = refs[:n], refs[n:2 * n]
        send, recv = refs[2 * n:]
        x, y, c = _place()
        cps = []
        for a in range(n):
            cp = pltpu.make_async_remote_copy(
                src_ref=ins[a].at[:, 1 - c], dst_ref=outs[a], send_sem=send.at[a], recv_sem=recv.at[a],
                device_id=(x, y, 1 - c), device_id_type=MESH)
            cp.start()
            cps.append(cp)
        for cp in cps:
            cp.wait()

    dma = pltpu.SemaphoreType.DMA
    return pl.pallas_call(
        body, name="rs_exchange", in_specs=[ANY] * n, out_specs=[ANY] * n,
        out_shape=[jax.ShapeDtypeStruct((g.shape[0],) + g.shape[2:], g.dtype) for g in grads],
        scratch_shapes=[dma((n,)), dma((n,))],
        compiler_params=pltpu.CompilerParams(has_side_effects=True),
    )(*grads)


def rs_add(g, sib, core, out_dtype, name):
    nk, _, hr, cc = g.shape
    rb = _tile(hr, 256)

    def body(core_ref, g_ref, s_ref, o_ref):
        del core_ref
        o_ref[0] = (g_ref[0, 0] + s_ref[0]).astype(out_dtype)

    return pl.pallas_call(
        body, name=name,
        grid_spec=pltpu.PrefetchScalarGridSpec(
            num_scalar_prefetch=1, grid=(nk, hr // rb),
            in_specs=[pl.BlockSpec((1, 1, rb, cc), lambda k, i, core_ref: (k, core_ref[0], i, 0)),
                      pl.BlockSpec((1, rb, cc), lambda k, i, core_ref: (k, i, 0))],
            out_specs=pl.BlockSpec((1, rb, cc), lambda k, i, core_ref: (k, i, 0))),
        out_shape=jax.ShapeDtypeStruct((nk, hr, cc), out_dtype),
        compiler_params=_cparams(("parallel", "parallel")),
    )(core, g, sib)


def rs_send(parts):
    n = len(parts)

    def body(*refs):
        ins, outs = refs[:n], refs[n:2 * n]
        send, recv, loc = refs[2 * n:]
        x, y, c = _place()
        k = 2 * x + y
        waits = []
        for a in range(n):
            cp = pltpu.make_async_copy(ins[a].at[k], outs[a].at[3], loc.at[a])
            cp.start()
            waits.append(cp.wait)
            for o, (fx, fy) in enumerate(CHIP_FLIPS):
                kk = 2 * _flip(x, fx) + _flip(y, fy)
                cp = pltpu.make_async_remote_copy(
                    src_ref=ins[a].at[kk], dst_ref=outs[a].at[o], send_sem=send.at[3 * a + o], recv_sem=recv.at[3 * a + o],
                    device_id=(_flip(x, fx), _flip(y, fy), c), device_id_type=MESH)
                cp.start()
                waits.append(cp.wait)
        for w in waits:
            w()

    dma = pltpu.SemaphoreType.DMA
    return pl.pallas_call(
        body, name="rs_send", in_specs=[ANY] * n, out_specs=[ANY] * n,
        out_shape=[jax.ShapeDtypeStruct(p.shape, p.dtype) for p in parts],
        scratch_shapes=[dma((3 * n,)), dma((3 * n,)), dma((n,))],
        compiler_params=pltpu.CompilerParams(has_side_effects=True),
    )(*parts)


def rs_sum(recv, name):
    _, hr, cc = recv.shape
    rb = _tile(hr, 256)

    def body(r_ref, o_ref):
        o_ref[...] = ((r_ref[3].astype(F32) + r_ref[0].astype(F32)) + r_ref[1].astype(F32)) + r_ref[2].astype(F32)

    return pl.pallas_call(
        body, name=name, grid=(hr // rb,),
        in_specs=[pl.BlockSpec((4, rb, cc), lambda i: (0, i, 0))],
        out_specs=pl.BlockSpec((rb, cc), lambda i: (i, 0)),
        out_shape=jax.ShapeDtypeStruct((hr, cc), F32),
        compiler_params=_cparams(("parallel",)),
    )(recv)


def rs_share(sums, groups):
    n = len(sums)

    def body(*refs):
        ins, outs = refs[:n], refs[n:n + len(groups)]
        send, recv, loc = refs[n + len(groups):]
        x, y, c = _place()
        waits = []
        for gi, members in enumerate(groups):
            for layer, a in enumerate(members):
                cp = pltpu.make_async_copy(ins[a], outs[gi].at[layer, c], loc.at[a])
                cp.start()
                waits.append(cp.wait)
                cp = pltpu.make_async_remote_copy(
                    src_ref=ins[a], dst_ref=outs[gi].at[layer, c], send_sem=send.at[a], recv_sem=recv.at[a],
                    device_id=(x, y, 1 - c), device_id_type=MESH)
                cp.start()
                waits.append(cp.wait_send)
        for gi, members in enumerate(groups):
            for layer, a in enumerate(members):
                got = outs[gi].at[layer, 1 - c]
                pltpu.make_async_remote_copy(
                    src_ref=got, dst_ref=got, send_sem=send.at[a], recv_sem=recv.at[a],
                    device_id=(x, y, 1 - c), device_id_type=MESH).wait_recv()
        for w in waits:
            w()

    dma = pltpu.SemaphoreType.DMA
    return pl.pallas_call(
        body, name="rs_share", in_specs=[ANY] * n, out_specs=[ANY] * len(groups),
        out_shape=[jax.ShapeDtypeStruct((len(m), 2) + sums[m[0]].shape, F32) for m in groups],
        scratch_shapes=[dma((n,)), dma((n,)), dma((n,))],
        compiler_params=pltpu.CompilerParams(has_side_effects=True),
    )(*sums)


def allreduce_small(v):
    r, w = v.shape

    def body(v_ref, o_ref, buf, send, recv, loc):
        x, y, c = _place()
        me = 4 * x + 2 * y + c
        mine = pltpu.make_async_copy(v_ref, buf.at[me], loc)
        mine.start()
        cps = []
        for o in range(1, N_DEV):
            fx, fy, fc = (o >> 2) & 1, (o >> 1) & 1, o & 1
            cp = pltpu.make_async_remote_copy(
                src_ref=v_ref, dst_ref=buf.at[me], send_sem=send.at[o - 1], recv_sem=recv.at[o - 1],
                device_id=(_flip(x, fx), _flip(y, fy), _flip(c, fc)), device_id_type=MESH)
            cp.start()
            cps.append(cp)
        for o in range(1, N_DEV):
            fx, fy, fc = (o >> 2) & 1, (o >> 1) & 1, o & 1
            peer = 4 * _flip(x, fx) + 2 * _flip(y, fy) + _flip(c, fc)
            pltpu.make_async_remote_copy(
                src_ref=v_ref, dst_ref=buf.at[peer], send_sem=send.at[o - 1], recv_sem=recv.at[o - 1],
                device_id=(x, y, c), device_id_type=MESH).wait_recv()
        for cp in cps:
            cp.wait_send()
        mine.wait()
        acc = buf[0]
        for d in range(1, N_DEV):
            acc = acc + buf[d]
        o_ref[...] = acc

    dma = pltpu.SemaphoreType.DMA
    vm = pl.BlockSpec(memory_space=pltpu.VMEM)
    return pl.pallas_call(
        body, name="allreduce_small", in_specs=[vm], out_specs=vm,
        out_shape=jax.ShapeDtypeStruct((r, w), F32),
        scratch_shapes=[pltpu.VMEM((N_DEV, r, w), F32), dma((N_DEV - 1,)), dma((N_DEV - 1,)), dma],
        compiler_params=pltpu.CompilerParams(has_side_effects=True, vmem_limit_bytes=VMEM_LIMIT),
    )(v)


def adamw(w, g, m, v, name):
    r, cc = w.shape
    rb = _tile(r, 256)

    def body(w_ref, g_ref, m_ref, v_ref, d_ref, nm_ref, nv_ref):
        gv = g_ref[...]
        nm = ADAM_B1 * m_ref[...] + (1.0 - ADAM_B1) * gv
        nv = ADAM_B2 * v_ref[...] + (1.0 - ADAM_B2) * (gv * gv)
        m_hat = nm / (1.0 - ADAM_B1 ** ADAM_STEP)
        v_hat = nv / (1.0 - ADAM_B2 ** ADAM_STEP)
        d_ref[...] = -ADAM_LR * (m_hat / (jnp.sqrt(v_hat) + ADAM_EPS) + ADAM_WD * w_ref[...])
        nm_ref[...] = nm
        nv_ref[...] = nv

    blk = pl.BlockSpec((rb, cc), lambda i: (i, 0))
    shp = jax.ShapeDtypeStruct((r, cc), F32)
    return pl.pallas_call(
        body, name=name, grid=(r // rb,), in_specs=[blk] * 4, out_specs=[blk] * 3, out_shape=[shp] * 3,
        compiler_params=_cparams(("parallel",)),
    )(w, g, m, v)


WEIGHTS = ['g_ffn1', 'w_ffn1_gate', 'w_ffn1_up', 'w_ffn1_down', 'g_mix', 'w_in_ab', 'conv_w', 'conv_b', 'ln_a_g',
           'ln_a_b', 'ln_v_g', 'ln_v_b', 'sp_w', 'sp_b', 'w_out_ab', 'w_qkv', 'w_o', 'g_ffn2', 'w_ffn2_gate',
           'w_ffn2_up', 'w_ffn2_down', 'g_final']
BIG = ['w_ffn1_gate', 'w_ffn1_up', 'w_ffn1_down', 'w_in_ab', 'w_out_ab', 'w_qkv', 'w_o', 'w_ffn2_gate', 'w_ffn2_up',
       'w_ffn2_down']
SMALL = ['g_ffn1', 'g_mix', 'g_ffn2', 'g_final', 'conv_b', 'ln_a_g', 'ln_a_b', 'ln_v_g', 'ln_v_b', 'sp_b', 'sp_w']


def _rows(a):
    return a.reshape(-1, LANES)


def _pack(parts):
    v = jnp.concatenate([_rows(p) for p in parts], axis=0)
    pad = (-v.shape[0]) % 8
    return jnp.pad(v, ((0, pad), (0, 0)))


def _unpack(v, shapes):
    out, r = [], 0
    for s in shapes:
        n = 1
        for d in s:
            n *= d
        n //= LANES
        out.append(v[r:r + n].reshape(s))
        r += n
    return out


def kernel(x, g_ffn1, w_ffn1_gate, w_ffn1_up, w_ffn1_down, g_mix, w_in_ab, conv_w, conv_b, ln_a_g, ln_a_b, ln_v_g, ln_v_b, sp_w, sp_b, w_out_ab, w_qkv, w_o, g_ffn2, w_ffn2_gate, w_ffn2_up, w_ffn2_down, g_final, loss_target, m_g_ffn1, m_w_ffn1_gate, m_w_ffn1_up, m_w_ffn1_down, m_g_mix, m_w_in_ab, m_conv_w, m_conv_b, m_ln_a_g, m_ln_a_b, m_ln_v_g, m_ln_v_b, m_sp_w, m_sp_b, m_w_out_ab, m_w_qkv, m_w_o, m_g_ffn2, m_w_ffn2_gate, m_w_ffn2_up, m_w_ffn2_down, m_g_final, v_g_ffn1, v_w_ffn1_gate, v_w_ffn1_up, v_w_ffn1_down, v_g_mix, v_w_in_ab, v_conv_w, v_conv_b, v_ln_a_g, v_ln_a_b, v_ln_v_g, v_ln_v_b, v_sp_w, v_sp_b, v_w_out_ab, v_w_qkv, v_w_o, v_g_ffn2, v_w_ffn2_gate, v_w_ffn2_up, v_w_ffn2_down, v_g_final):
    p = dict(locals())
    n_seq, seq, d = x.shape
    t = n_seq * seq
    depth = g_ffn1.shape[0]
    core = lax.axis_index("c")
    chip = 2 * lax.axis_index("x") + lax.axis_index("y")
    xf = x.reshape(t, d)
    target = loss_target.reshape(t, d)

    items = []
    for name in BIG:
        for layer in range(p[name].shape[0]):
            items.append((name, layer))
    shards = [p[name][layer].astype(BF16) for name, layer in items]
    gathered, (conv_w4,) = allgather_weights(shards, [conv_w[0]])
    wt = {it: g for it, g in zip(items, gathered)}
    c_mix = conv_w4.shape[2] * N_CHIPS
    conv_full = jnp.transpose(conv_w4, (1, 0, 2)).reshape(CONV_WIDTH, c_mix)
    vec = lambda a: a.reshape(1, -1)
    sp_bt = sp_b[0].T
    sp_wt = jnp.transpose(sp_w[0], (0, 2, 1))
    d_ff = w_ffn1_gate.shape[2]
    n_in = w_in_ab.shape[2]
    n_qkv = w_qkv.shape[2] // 3

    saved = []
    xc = xf
    for layer in range(depth):
        s = {}
        for half, (gn, wn) in enumerate((('g_ffn1', 'w_ffn1'), ('g_ffn2', 'w_ffn2'))):
            if half == 1:
                s['x_mix'] = xc
                s['h_mix'] = rmsnorm_fwd(xc, vec(g_mix[layer]), "norm_mix")
                if layer % 2 == 0:
                    (z,) = colmm(s['h_mix'], [wt[('w_in_ab', layer // 2)]], n_in, BF16, "mm_in")
                    cat, a1 = mix_fwd(z, conv_full, conv_b, ln_a_g, ln_a_b, vec(ln_v_g), vec(ln_v_b), sp_w[0], sp_bt, seq)
                    s.update(z=z, cat=cat, a1=a1)
                    xc = rowmm([cat], wt[('w_out_ab', layer // 2)], xc, 1.0, "mm_out")
                else:
                    (qkv,) = colmm(s['h_mix'], [wt[('w_qkv', layer // 2)]], n_qkv, BF16, "mm_qkv")
                    o, tot = attn_fwd(qkv, n_seq, seq)
                    s.update(qkv=qkv, o=o, tot=tot)
                    xc = rowmm([o], wt[('w_o', layer // 2)], xc, 1.0, "mm_o")
            s['x' + wn] = xc
            h = rmsnorm_fwd(xc, vec(p[gn][layer]), "norm_ffn")
            gate, up = colmm(h, [wt[(wn + '_gate', layer)], wt[(wn + '_up', layer)]], d_ff, BF16, "ffn_gateup")
            xc = rowmm([gate, up], wt[(wn + '_down', layer)], xc, 0.5, "ffn_down")
            s.update({'h' + wn: h, 'gate' + wn: gate, 'up' + wn: up})
        saved.append(s)

    loss8, dx, dxb, dg_final = loss_head(xc, vec(g_final), target)
    loss = lax.psum(loss8[0, 0], ("x", "y", "c"))

    gw = {}
    gs = {}
    for layer in reversed(range(depth)):
        s = saved[layer]
        for half, (gn, wn) in reversed(list(enumerate((('g_ffn1', 'w_ffn1'), ('g_ffn2', 'w_ffn2'))))):
            wd = wt[(wn + '_down', layer)]
            dgate, dup, act = rowmm_t(dxb, wd, 0.5, BF16, "ffn_bwd_act", gu=(s['gate' + wn], s['up' + wn]))
            gw[(wn + '_down', layer)] = dw_row(act, dxb, 0.5, "ffn_dw_down")
            gw[(wn + '_gate', layer)], gw[(wn + '_up', layer)] = dw_col(s['h' + wn], [dgate, dup], N_CHIPS, d_ff, "ffn_dw_gateup")
            dx, dxb, dg = colmm_t([dgate, dup], [wt[(wn + '_gate', layer)], wt[(wn + '_up', layer)]], d_ff,
                                  s['x' + wn], vec(p[gn][layer]), dx, "ffn_bwd_in")
            gs[(gn, layer)] = dg
            if half == 1:
                if layer % 2 == 0:
                    i = layer // 2
                    w_out = wt[('w_out_ab', i)]
                    dcat = rowmm_t(dxb, w_out, 1.0, F32, "mm_out_t")
                    gw[('w_out_ab', i)] = dw_row(s['cat'], dxb, 1.0, "dw_out")
                    dz, da1, dcb, dlag, dlab, dlvg, dlvb, dspw, dspb = mix_bwd_point(
                        dcat, s['z'], s['a1'], ln_a_g, ln_a_b, vec(ln_v_g), vec(ln_v_b), sp_w[0], sp_wt, sp_bt, seq)
                    dz, dcw = mix_bwd_conv(dz, da1, s['z'], conv_full, seq)
                    gs.update({('conv_b', i): dcb, ('ln_a_g', i): dlag, ('ln_a_b', i): dlab, ('ln_v_g', i): dlvg,
                               ('ln_v_b', i): dlvb, ('sp_w', i): dspw, ('sp_b', i): dspb[:, :, 0], ('conv_w', i): dcw})
                    (gw[('w_in_ab', i)],) = dw_col(s['h_mix'], [dz], N_CHIPS, n_in, "dw_in")
                    dx, dxb, dg = colmm_t([dz], [wt[('w_in_ab', i)]], n_in, s['x_mix'], vec(g_mix[layer]), dx, "mm_in_t")
                else:
                    i = layer // 2
                    w_o4 = wt[('w_o', i)]
                    do = rowmm_t(dxb, w_o4, 1.0, BF16, "mm_o_t")
                    gw[('w_o', i)] = dw_row(s['o'], dxb, 1.0, "dw_o")
                    dq, dk, dv = attn_bwd(s['qkv'], do, s['tot'], n_seq, seq)
                    dqkv = jnp.concatenate([dq, dk, dv], axis=0)
                    (gw[('w_qkv', i)],) = dw_col(s['h_mix'], [dqkv], N_CHIPS, n_qkv, "dw_qkv")
                    dx, dxb, dg = colmm_t([dqkv], [wt[('w_qkv', i)]], n_qkv, s['x_mix'], vec(g_mix[layer]), dx, "mm_qkv_t")
                gs[('g_mix', layer)] = dg
    grad_x = dx.reshape(x.shape)

    core1 = core.reshape(1).astype(jnp.int32)
    g4 = [gw[it].reshape(N_CHIPS, 2, gw[it].shape[1] // 2, gw[it].shape[2]) for it in items]
    sib = rs_exchange(g4)
    parts = [rs_add(g, sb, core1, F32, "rs_add") for g, sb in zip(g4, sib)]
    recv = rs_send(parts)
    sums = [rs_sum(r, "rs_sum") for r in recv]
    groups = [[items.index((name, layer)) for layer in range(p[name].shape[0])] for name in BIG]
    shared = rs_share(sums, groups)
    grads = {name: sh.reshape(p[name].shape) for name, sh in zip(BIG, shared)}

    stack = lambda name: jnp.concatenate([gs[(name, layer)].reshape((1,) + p[name].shape[1:]) for layer in range(p[name].shape[0])], axis=0)
    small_g = [stack(name) if name != 'g_final' else dg_final.reshape(p[name].shape) for name in SMALL]
    packed = _pack(small_g + [gs[('conv_w', 0)]])
    red = allreduce_small(packed)
    outs = _unpack(red, [p[name].shape for name in SMALL] + [(CONV_WIDTH, c_mix)])
    for name, g in zip(SMALL, outs[:-1]):
        grads[name] = g
    conv_g = outs[-1].reshape(CONV_WIDTH, N_CHIPS, c_mix // N_CHIPS)
    grads['conv_w'] = lax.dynamic_index_in_dim(conv_g, chip, axis=1, keepdims=False).reshape(conv_w.shape)

    delta, new_m, new_v = {}, {}, {}
    for name in BIG:
        shp = p[name].shape
        two = lambda a: a.reshape(shp[0] * shp[1], shp[2])
        dl, nm, nv = adamw(two(p[name]), two(grads[name]), two(p['m_' + name]), two(p['v_' + name]), "adamw")
        delta[name], new_m[name], new_v[name] = dl.reshape(shp), nm.reshape(shp), nv.reshape(shp)
    small_names = SMALL + ['conv_w']
    pk = lambda pre: _pack([p[pre + name] for name in small_names])
    dl, nm, nv = adamw(pk(''), _pack([grads[name] for name in small_names]), pk('m_'), pk('v_'), "adamw_small")
    shapes = [p[name].shape for name in small_names]
    for dst, val in ((delta, dl), (new_m, nm), (new_v, nv)):
        for name, a in zip(small_names, _unpack(val, shapes)):
            dst[name] = a

    return (loss, grad_x, *[grads[n] for n in WEIGHTS], *[delta[n] for n in WEIGHTS],
            *[new_m[n] for n in WEIGHTS], *[new_v[n] for n in WEIGHTS])
```

```python
import functools

import jax
import jax.numpy as jnp
from jax import lax
from jax.experimental import pallas as pl
from jax.experimental.pallas import tpu as pltpu

F32 = jnp.float32
BF16 = jnp.bfloat16
EPS = 1e-6
HEAD_DIM = 64
CONV_WIDTH = 31
CHUNK = 128
KBLK = 128
ATT_BLOCK = 512
LANES = 128
HALO = 32
ADAM_LR, ADAM_B1, ADAM_B2, ADAM_EPS, ADAM_WD, ADAM_STEP = 0.001, 0.9, 0.999, 1e-08, 0.01, 10
VMEM_LIMIT = 56 * 1024 * 1024
MESH = pl.DeviceIdType.MESH
N_CHIPS = 4
N_DEV = 8


def _cparams(sem):
    return pltpu.CompilerParams(dimension_semantics=sem, vmem_limit_bytes=VMEM_LIMIT)


def _nt(a, b):
    return lax.dot_general(a, b, (((1,), (1,)), ((), ())), preferred_element_type=F32)


def _tn(a, b):
    return lax.dot_general(a, b, (((0,), (0,)), ((), ())), preferred_element_type=F32)


def _nn(a, b):
    return jnp.dot(a, b, preferred_element_type=F32)


def _sigmoid(x):
    return 1.0 / (1.0 + jnp.exp(-x))


def _tile(t, want):
    if t <= want:
        return t
    for cand in range(want - want % 8, 7, -8):
        if t % cand == 0:
            return cand
    raise ValueError((t, want))


def rmsnorm_fwd(x, g, name):
    t, d = x.shape
    tm = _tile(t, 512)

    def body(x_ref, g_ref, h_ref):
        xv = x_ref[...]
        r = lax.rsqrt(jnp.mean(xv * xv, axis=-1, keepdims=True) + EPS)
        h_ref[...] = (xv * r * g_ref[...]).astype(BF16)

    return pl.pallas_call(
        body, name=name, grid=(t // tm,),
        in_specs=[pl.BlockSpec((tm, d), lambda i: (i, 0)), pl.BlockSpec((1, d), lambda i: (0, 0))],
        out_specs=pl.BlockSpec((tm, d), lambda i: (i, 0)),
        out_shape=jax.ShapeDtypeStruct((t, d), BF16),
        compiler_params=_cparams(("parallel",)),
    )(x, g)


def colmm(h, ws, nu, out_dtype, name):
    t, k = h.shape
    j, _, nj = ws[0].shape
    per = nj // nu
    units = j * per
    tm = _tile(t, 512)
    nw = len(ws)

    def body(*refs):
        h_ref = refs[0]
        hv = h_ref[...]
        for n in range(nw):
            refs[1 + nw + n][0] = _nn(hv, refs[1 + n][0]).astype(out_dtype)

    w_spec = pl.BlockSpec((1, k, nu), lambda u, i: (u // per, 0, u % per))
    o_spec = pl.BlockSpec((1, tm, nu), lambda u, i: (u, i, 0))
    outs = pl.pallas_call(
        body, name=name, grid=(units, t // tm),
        in_specs=[pl.BlockSpec((tm, k), lambda u, i: (i, 0))] + [w_spec] * nw,
        out_specs=[o_spec] * nw,
        out_shape=[jax.ShapeDtypeStruct((units, t, nu), out_dtype)] * nw,
        compiler_params=_cparams(("parallel", "parallel")),
    )(h, *ws)
    return outs


def rowmm(a_list, w, resid, scale, name):
    swiglu = len(a_list) == 2
    u_n, t, ku = a_list[0].shape
    n = w.shape[2]
    tm = _tile(t, 256)

    def body(*refs):
        a_refs = refs[:len(a_list)]
        w_ref, r_ref, o_ref = refs[len(a_list):]
        acc = jnp.zeros((tm, n), F32)
        for u in range(u_n):
            if swiglu:
                gv = a_refs[0][u].astype(F32)
                av = (gv * _sigmoid(gv) * a_refs[1][u].astype(F32)).astype(BF16)
            else:
                av = a_refs[0][u]
            acc = acc + _nn(av, w_ref[u])
        o_ref[...] = r_ref[...] + scale * acc

    a_spec = pl.BlockSpec((u_n, tm, ku), lambda i: (0, i, 0))
    return pl.pallas_call(
        body, name=name, grid=(t // tm,),
        in_specs=[a_spec] * len(a_list) + [pl.BlockSpec((u_n, ku, n), lambda i: (0, 0, 0)),
                                           pl.BlockSpec((tm, n), lambda i: (i, 0))],
        out_specs=pl.BlockSpec((tm, n), lambda i: (i, 0)),
        out_shape=jax.ShapeDtypeStruct((t, n), F32),
        compiler_params=_cparams(("parallel",)),
    )(*a_list, w, resid)


def rowmm_t(dyb, w, scale, out_dtype, name, gu=None):
    t, n = dyb.shape
    u_n, ku, _ = w.shape
    tm = _tile(t, 512)

    def body(*refs):
        if gu is None:
            dy_ref, w_ref, o_ref = refs
            o_ref[0] = (scale * _nt(dy_ref[...], w_ref[0])).astype(out_dtype)
        else:
            dy_ref, w_ref, g_ref, u_ref, dg_ref, du_ref, a_ref = refs
            dact = scale * _nt(dy_ref[...], w_ref[0])
            gv = g_ref[0].astype(F32)
            uv = u_ref[0].astype(F32)
            s = _sigmoid(gv)
            silu = gv * s
            dg_ref[0] = (dact * uv * (s * (1.0 + gv * (1.0 - s)))).astype(BF16)
            du_ref[0] = (dact * silu).astype(BF16)
            a_ref[0] = (silu * uv).astype(BF16)

    blk = pl.BlockSpec((1, tm, ku), lambda u, i: (u, i, 0))
    in_specs = [pl.BlockSpec((tm, n), lambda u, i: (i, 0)), pl.BlockSpec((1, ku, n), lambda u, i: (u, 0, 0))]
    if gu is None:
        return pl.pallas_call(
            body, name=name, grid=(u_n, t // tm), in_specs=in_specs, out_specs=blk,
            out_shape=jax.ShapeDtypeStruct((u_n, t, ku), out_dtype),
            compiler_params=_cparams(("parallel", "parallel")),
        )(dyb, w)
    return pl.pallas_call(
        body, name=name, grid=(u_n, t // tm), in_specs=in_specs + [blk, blk], out_specs=[blk] * 3,
        out_shape=[jax.ShapeDtypeStruct((u_n, t, ku), BF16)] * 3,
        compiler_params=_cparams(("parallel", "parallel")),
    )(dyb, w, *gu)


def colmm_t(dzs, ws, nu, x, g, dy_in, name):
    t, k = x.shape
    j, _, nj = ws[0].shape
    per = nj // nu
    units = j * per
    nw = len(ws)
    tm = _tile(t, 256)

    def body(*refs):
        dz_refs = refs[:nw]
        w_refs = refs[nw:2 * nw]
        x_ref, g_ref, dy_ref, dx_ref, dxb_ref, dg_ref = refs[2 * nw:]
        i = pl.program_id(0)
        dh = jnp.zeros((tm, k), F32)
        for n in range(nw):
            for u in range(units):
                wv = w_refs[n][u // per, :, (u % per) * nu:(u % per + 1) * nu]
                dh = dh + _nt(dz_refs[n][u], wv)
        xv = x_ref[...]
        gv = g_ref[...]
        r = lax.rsqrt(jnp.mean(xv * xv, axis=-1, keepdims=True) + EPS)
        uu = dh * gv
        dx = dy_ref[...] + r * uu - xv * (r * r * r * jnp.mean(uu * xv, axis=-1, keepdims=True))
        dx_ref[...] = dx
        dxb_ref[...] = dx.astype(BF16)
        part = jnp.sum(dh * (xv * r), axis=0, keepdims=True)

        @pl.when(i == 0)
        def _():
            dg_ref[...] = part

        @pl.when(i > 0)
        def _():
            dg_ref[...] += part

    dz_spec = pl.BlockSpec((units, tm, nu), lambda i: (0, i, 0))
    w_spec = pl.BlockSpec((j, k, nj), lambda i: (0, 0, 0))
    row = pl.BlockSpec((tm, k), lambda i: (i, 0))
    vec = pl.BlockSpec((1, k), lambda i: (0, 0))
    return pl.pallas_call(
        body, name=name, grid=(t // tm,),
        in_specs=[dz_spec] * nw + [w_spec] * nw + [row, vec, row],
        out_specs=[row, row, vec],
        out_shape=[jax.ShapeDtypeStruct((t, k), F32), jax.ShapeDtypeStruct((t, k), BF16),
                   jax.ShapeDtypeStruct((1, k), F32)],
        compiler_params=_cparams(("arbitrary",)),
    )(*dzs, *ws, x, g, dy_in)


def dw_col(h, dzs, j, nu, name):
    t, k = h.shape
    units = dzs[0].shape[0]
    per = units // j
    nw = len(dzs)
    tt = _tile(t, 512)

    def body(*refs):
        h_ref = refs[0]
        s = pl.program_id(1)
        hv = h_ref[...]
        for n in range(nw):
            part = _tn(hv, refs[1 + n][0])
            o_ref = refs[1 + nw + n]

            @pl.when(s == 0)
            def _():
                o_ref[0] = part

            @pl.when(s > 0)
            def _():
                o_ref[0] += part

    return pl.pallas_call(
        body, name=name, grid=(units, t // tt),
        in_specs=[pl.BlockSpec((tt, k), lambda u, s: (s, 0))] + [pl.BlockSpec((1, tt, nu), lambda u, s: (u, s, 0))] * nw,
        out_specs=[pl.BlockSpec((1, k, nu), lambda u, s: (u // per, 0, u % per))] * nw,
        out_shape=[jax.ShapeDtypeStruct((j, k, per * nu), F32)] * nw,
        compiler_params=_cparams(("parallel", "arbitrary")),
    )(h, *dzs)


def dw_row(a, dyb, scale, name):
    u_n, t, ku = a.shape
    n = dyb.shape[1]
    tt = _tile(t, 512)

    def body(a_ref, dy_ref, o_ref):
        s = pl.program_id(1)
        part = scale * _tn(a_ref[0], dy_ref[...])

        @pl.when(s == 0)
        def _():
            o_ref[0] = part

        @pl.when(s > 0)
        def _():
            o_ref[0] += part

    return pl.pallas_call(
        body, name=name, grid=(u_n, t // tt),
        in_specs=[pl.BlockSpec((1, tt, ku), lambda u, s: (u, s, 0)), pl.BlockSpec((tt, n), lambda u, s: (s, 0))],
        out_specs=pl.BlockSpec((1, ku, n), lambda u, s: (u, 0, 0)),
        out_shape=jax.ShapeDtypeStruct((u_n, ku, n), F32),
        compiler_params=_cparams(("parallel", "arbitrary")),
    )(a, dyb)


def loss_head(x, g, target):
    t, d = x.shape
    tm = _tile(t, 256)

    def body(x_ref, g_ref, t_ref, loss_ref, dx_ref, dxb_ref, dg_ref):
        i = pl.program_id(0)
        xv = x_ref[...]
        gv = g_ref[...]
        r = lax.rsqrt(jnp.mean(xv * xv, axis=-1, keepdims=True) + EPS)
        xh = xv * r
        err = xh * gv - t_ref[...]
        dy = err * (1.0 / d)
        uu = dy * gv
        dx = r * uu - xv * (r * r * r * jnp.mean(uu * xv, axis=-1, keepdims=True))
        dx_ref[...] = dx
        dxb_ref[...] = dx.astype(BF16)
        dg_part = jnp.sum(dy * xh, axis=0, keepdims=True)
        row = jnp.sum(err * err, axis=-1, keepdims=True) * (0.5 / d)
        l_part = jnp.zeros((8, LANES), F32) + jnp.sum(row, axis=0, keepdims=True)

        @pl.when(i == 0)
        def _():
            dg_ref[...] = dg_part
            loss_ref[...] = l_part

        @pl.when(i > 0)
        def _():
            dg_ref[...] += dg_part
            loss_ref[...] += l_part

    row = pl.BlockSpec((tm, d), lambda i: (i, 0))
    vec = pl.BlockSpec((1, d), lambda i: (0, 0))
    return pl.pallas_call(
        body, name="loss_head", grid=(t // tm,),
        in_specs=[row, vec, row],
        out_specs=[pl.BlockSpec((8, LANES), lambda i: (0, 0)), row, row, vec],
        out_shape=[jax.ShapeDtypeStruct((8, LANES), F32), jax.ShapeDtypeStruct((t, d), F32),
                   jax.ShapeDtypeStruct((t, d), BF16), jax.ShapeDtypeStruct((1, d), F32)],
        compiler_params=_cparams(("arbitrary",)),
    )(x, g, target)


def _split(v):
    hi = v.astype(BF16)
    lo = (v - hi.astype(F32)).astype(BF16)
    return hi, lo


def _keysums(v, m_ext):
    hi, lo = _split(v)
    outs = []
    for j in range(v.shape[1] // KBLK):
        sl = slice(j * KBLK, (j + 1) * KBLK)
        cs = _nn(jnp.concatenate([hi[:, sl], lo[:, sl]], axis=1), m_ext)
        outs.append((cs[:, :KBLK], cs[:, KBLK:]))
    return outs


def _softplus_parts(z):
    sp = jnp.maximum(z, 0.0) + jnp.log(1.0 + jnp.exp(-jnp.abs(z)))
    return sp, z - sp


def _sum_matrices():
    r = lax.broadcasted_iota(jnp.int32, (2 * KBLK, 2 * KBLK), 0) % KBLK
    c = lax.broadcasted_iota(jnp.int32, (2 * KBLK, 2 * KBLK), 1)
    suffix = jnp.where((r > c) | (c >= KBLK), 1.0, 0.0).astype(BF16)
    prefix = jnp.where((r <= c) | (c >= KBLK), 1.0, 0.0).astype(BF16)
    return suffix, prefix


def attn_fwd(qkv, n_seq, seq):
    t = qkv.shape[1]
    n_pairs = (qkv.shape[0] // 3) * 2
    bq = min(ATT_BLOCK, seq)
    nq = seq // bq
    nsub = bq // KBLK
    suffix_m, _ = _sum_matrices()

    def body(q_ref, k_ref, v_ref, m_ref, o_ref, tot_ref):
        qi = pl.program_id(2)
        lane = lax.broadcasted_iota(jnp.int32, (bq, LANES), 1)
        is_a = lane < HEAD_DIM
        q2 = q_ref[0] * jnp.asarray(HEAD_DIM ** -0.5, BF16)
        qs = (jnp.where(is_a, q2, jnp.zeros_like(q2)), jnp.where(is_a, jnp.zeros_like(q2), q2))
        m_ext = m_ref[...]
        row = lax.broadcasted_iota(jnp.int32, (bq, bq), 0)
        col = lax.broadcasted_iota(jnp.int32, (bq, bq), 1)
        diag_mask = col < row

        def block(kj, carry, mask):
            off = pl.multiple_of(kj * bq, bq)
            k2 = k_ref[0, pl.ds(off, bq), :]
            v2 = v_ref[0, pl.ds(off, bq), :]
            out = []
            for h in range(2):
                rem, acc = carry[h]
                z = _nt(qs[h], k2)
                sp, ls = _softplus_parts(z)
                lk = -sp if mask is None else jnp.where(mask, -sp, 0.0)
                sums = _keysums(lk, m_ext)
                parts = [None] * nsub
                for j in reversed(range(nsub)):
                    suf, total = sums[j]
                    parts[j] = jnp.exp(ls[:, j * KBLK:(j + 1) * KBLK] + suf + rem)
                    rem = rem + total
                a = jnp.concatenate(parts, axis=1)
                if mask is not None:
                    a = jnp.where(mask, a, 0.0)
                out.append((rem, acc + _nn(a.astype(BF16), v2)))
            return tuple(out)

        zero = jnp.zeros((bq, LANES), F32)
        carry = block(qi, ((zero, zero), (zero, zero)), diag_mask)
        carry = lax.fori_loop(0, qi, lambda it, c: block(qi - 1 - it, c, None), carry)
        o_ref[0] = jnp.where(is_a, carry[0][1], carry[1][1]).astype(BF16)
        tot_ref[...] = jnp.where(is_a, carry[0][0], carry[1][0])

    upp = qkv.shape[0] // 3
    return pl.pallas_call(
        body, name="attn_fwd", grid=(n_seq, n_pairs, nq),
        in_specs=[pl.BlockSpec((1, bq, LANES), lambda b, p, i: (p // 2, b * nq + i, p % 2)),
                  pl.BlockSpec((1, seq, LANES), lambda b, p, i: (upp + p // 2, b, p % 2)),
                  pl.BlockSpec((1, seq, LANES), lambda b, p, i: (2 * upp + p // 2, b, p % 2)),
                  pl.BlockSpec((2 * KBLK, 2 * KBLK), lambda b, p, i: (0, 0))],
        out_specs=[pl.BlockSpec((1, bq, LANES), lambda b, p, i: (p // 2, b * nq + i, p % 2)),
                   pl.BlockSpec((bq, LANES), lambda b, p, i: (b * nq + i, p))],
        out_shape=[jax.ShapeDtypeStruct((upp, t, 2 * LANES), BF16), jax.ShapeDtypeStruct((t, n_pairs * LANES), F32)],
        compiler_params=_cparams(("parallel", "parallel", "parallel")),
    )(qkv, qkv, qkv, suffix_m)


def attn_bwd(qkv, do, tot, n_seq, seq):
    t = qkv.shape[1]
    upp = qkv.shape[0] // 3
    n_pairs = upp * 2
    bq = min(ATT_BLOCK, seq)
    nq = seq // bq
    nsub = bq // KBLK
    _, prefix_m = _sum_matrices()
    scale = HEAD_DIM ** -0.5

    def body(q_ref, k_ref, v_ref, do_ref, tot_ref, m_ref, dq_ref, dk_ref, dv_ref, dk_acc, dv_acc):
        qi = pl.program_id(2)
        lane = lax.broadcasted_iota(jnp.int32, (bq, LANES), 1)
        is_a = lane < HEAD_DIM

        def halves(v2):
            z2 = jnp.zeros_like(v2)
            return jnp.where(is_a, v2, z2), jnp.where(is_a, z2, v2)

        qs = halves(q_ref[0] * jnp.asarray(scale, BF16))
        dos = halves(do_ref[0])
        tot2 = tot_ref[...]
        swapped = pltpu.roll(tot2, HEAD_DIM, 1)
        tots = (jnp.where(is_a, tot2, swapped), jnp.where(is_a, swapped, tot2))
        m_ext = m_ref[...]
        row = lax.broadcasted_iota(jnp.int32, (bq, bq), 0)
        col = lax.broadcasted_iota(jnp.int32, (bq, bq), 1)
        diag_mask = col < row

        @pl.when(qi == 0)
        def _():
            dk_acc[...] = jnp.zeros_like(dk_acc)
            dv_acc[...] = jnp.zeros_like(dv_acc)

        def block(kj, carry, mask):
            off = pl.multiple_of(kj * bq, bq)
            k2 = k_ref[0, pl.ds(off, bq), :]
            v2 = v_ref[0, pl.ds(off, bq), :]
            ks = halves(k2)
            dq = carry[2]
            dk_part = jnp.zeros((bq, LANES), F32)
            dv_part = jnp.zeros((bq, LANES), F32)
            out = []
            for h in range(2):
                pre, gpre = carry[h]
                z = _nt(qs[h], k2)
                sp, ls = _softplus_parts(z)
                lk = -sp if mask is None else jnp.where(mask, -sp, 0.0)
                sums = _keysums(lk, m_ext)
                parts = []
                for j in range(nsub):
                    pin, ptot = sums[j]
                    parts.append(jnp.exp(ls[:, j * KBLK:(j + 1) * KBLK] + (tots[h] - (pre + pin))))
                    pre = pre + ptot
                a = jnp.concatenate(parts, axis=1)
                if mask is not None:
                    a = jnp.where(mask, a, 0.0)
                g = a * _nt(dos[h], v2)
                gsums = _keysums(g, m_ext)
                parts = []
                for j in range(nsub):
                    gin, gtot = gsums[j]
                    parts.append(gpre + gin)
                    gpre = gpre + gtot
                dz = g - jnp.exp(ls) * jnp.concatenate(parts, axis=1)
                if mask is not None:
                    dz = jnp.where(mask, dz, 0.0)
                dzb = dz.astype(BF16)
                dq = dq + _nn(dzb, ks[h])
                dk_part = dk_part + _tn(dzb, qs[h])
                dv_part = dv_part + _tn(a.astype(BF16), dos[h])
                out.append((pre, gpre))
            dk_acc[pl.ds(off, bq), :] += dk_part
            dv_acc[pl.ds(off, bq), :] += dv_part
            return (out[0], out[1], dq)

        zero = jnp.zeros((bq, LANES), F32)
        carry = lax.fori_loop(0, qi, lambda kj, c: block(kj, c, None), ((zero, zero), (zero, zero), zero))
        carry = block(qi, carry, diag_mask)
        dq_ref[0] = (carry[2] * scale).astype(BF16)

        @pl.when(qi == nq - 1)
        def _():
            dk_ref[0] = dk_acc[...].astype(BF16)
            dv_ref[0] = dv_acc[...].astype(BF16)

    qblk = lambda b, p, i: (p // 2, b * nq + i, p % 2)
    kv_out = pl.BlockSpec((1, seq, LANES), lambda b, p, i: (p // 2, b, p % 2))
    shp = jax.ShapeDtypeStruct((upp, t, 2 * LANES), BF16)
    return pl.pallas_call(
        body, name="attn_bwd", grid=(n_seq, n_pairs, nq),
        in_specs=[pl.BlockSpec((1, bq, LANES), qblk),
                  pl.BlockSpec((1, seq, LANES), lambda b, p, i: (upp + p // 2, b, p % 2)),
                  pl.BlockSpec((1, seq, LANES), lambda b, p, i: (2 * upp + p // 2, b, p % 2)),
                  pl.BlockSpec((1, bq, LANES), qblk),
                  pl.BlockSpec((bq, LANES), lambda b, p, i: (b * nq + i, p)),
                  pl.BlockSpec((2 * KBLK, 2 * KBLK), lambda b, p, i: (0, 0))],
        out_specs=[pl.BlockSpec((1, bq, LANES), qblk), kv_out, kv_out],
        out_shape=[shp, shp, shp],
        scratch_shapes=[pltpu.VMEM((seq, LANES), F32), pltpu.VMEM((seq, LANES), F32)],
        compiler_params=_cparams(("parallel", "parallel", "arbitrary")),
    )(qkv, qkv, qkv, do, tot, prefix_m)


def _ln_stats(v):
    mu = jnp.mean(v, axis=-1, keepdims=True)
    vc = v - mu
    rstd = lax.rsqrt(jnp.mean(vc * vc, axis=-1, keepdims=True) + EPS)
    return vc * rstd, rstd


def _glu_into(a0_ref, av_ref, ag_ref, hv_ref, hg_ref, first):
    hv = hv_ref[0].astype(F32)
    hg = hg_ref[0].astype(F32)
    a0_ref[0:HALO, :] = jnp.where(first, 0.0, hv * _sigmoid(hg))
    av = av_ref[0].astype(F32)
    ag = ag_ref[0].astype(F32)
    a0_ref[HALO:, :] = av * _sigmoid(ag)


def _tril_mask():
    r = lax.broadcasted_iota(jnp.int32, (CHUNK, CHUNK), 0)
    c = lax.broadcasted_iota(jnp.int32, (CHUNK, CHUNK), 1)
    return c <= r


def mix_fwd(z, conv_w, conv_b, ln_a_g, ln_a_b, ln_v_g, ln_v_b, sp_w, sp_bt, seq):
    _, t, c = z.shape
    tm = _tile(seq, 512)
    tiles_per_seq = seq // tm
    groups = c // LANES
    hb = tm // HALO

    def body(av_ref, ag_ref, u_ref, v_ref, hv_ref, hg_ref, cw_ref, cb_ref, lag_ref, lab_ref, lvg_ref, lvb_ref,
             spw_ref, spb_ref, cat_ref, a1_ref, a0_ref):
        i = pl.program_id(0)
        _glu_into(a0_ref, av_ref, ag_ref, hv_ref, hg_ref, i % tiles_per_seq == 0)
        acc = jnp.zeros((tm, c), F32) + cb_ref[...]
        for k in range(CONV_WIDTH):
            acc = acc + cw_ref[k:k + 1, :] * a0_ref[pl.ds(HALO - (CONV_WIDTH - 1) + k, tm), :]
        a1_ref[...] = acc
        xh, _ = _ln_stats(acc)
        a2 = xh * lag_ref[...] + lab_ref[...]
        a3 = (a2 * _sigmoid(a2)).astype(BF16)
        half = c // 2
        cat_ref[0] = a3[:, :half]
        cat_ref[1] = a3[:, half:]
        tril = _tril_mask()
        for g in range(groups):
            sl = slice(g * LANES, (g + 1) * LANES)
            xh, _ = _ln_stats(v_ref[0][:, sl].astype(F32))
            vn = (xh * lvg_ref[:, sl] + lvb_ref[:, sl]).astype(BF16)
            w = jnp.where(tril, spw_ref[g], 0.0).astype(BF16)
            bias = spb_ref[:, g:g + 1]
            for ch in range(tm // CHUNK):
                rows = slice(ch * CHUNK, (ch + 1) * CHUNK)
                vs = _nn(w, vn[rows]) + bias
                bo = (u_ref[0][rows, sl].astype(F32) * vs).astype(BF16)
                cat_ref[2 + (g * LANES) // half, rows, (g * LANES) % half:(g * LANES) % half + LANES] = bo

    unit = lambda u: pl.BlockSpec((1, tm, c), lambda i: (u, i, 0))
    halo = lambda u: pl.BlockSpec((1, HALO, c), lambda i: (u, jnp.maximum(i * hb - 1, 0), 0))
    vec = pl.BlockSpec((1, c), lambda i: (0, 0))
    return pl.pallas_call(
        body, name="mix_fwd", grid=(t // tm,),
        in_specs=[unit(0), unit(1), unit(2), unit(3), halo(0), halo(1),
                  pl.BlockSpec((CONV_WIDTH, c), lambda i: (0, 0)), vec, vec, vec, vec, vec,
                  pl.BlockSpec((groups, CHUNK, CHUNK), lambda i: (0, 0, 0)),
                  pl.BlockSpec((CHUNK, groups), lambda i: (0, 0))],
        out_specs=[pl.BlockSpec((4, tm, c // 2), lambda i: (0, i, 0)), pl.BlockSpec((tm, c), lambda i: (i, 0))],
        out_shape=[jax.ShapeDtypeStruct((4, t, c // 2), BF16), jax.ShapeDtypeStruct((t, c), F32)],
        scratch_shapes=[pltpu.VMEM((HALO + tm, c), F32)],
        compiler_params=_cparams(("parallel",)),
    )(z, z, z, z, z, z, conv_w, conv_b, ln_a_g, ln_a_b, ln_v_g, ln_v_b, sp_w, sp_bt)


def mix_bwd_point(dcat, z, a1, ln_a_g, ln_a_b, ln_v_g, ln_v_b, sp_w, sp_wt, sp_bt, seq):
    _, t, c = z.shape
    tm = _tile(seq, 512)
    groups = c // LANES
    half = c // 2

    def body(dc_ref, u_ref, v_ref, a1_ref, lag_ref, lab_ref, lvg_ref, lvb_ref, spw_ref, spwt_ref, spb_ref,
             dz_ref, da1_ref, dcb_ref, dlag_ref, dlab_ref, dlvg_ref, dlvb_ref, dspw_ref, dspb_ref):
        i = pl.program_id(0)
        last = pl.num_programs(0) - 1

        @pl.when(i == 0)
        def _():
            for r in (dcb_ref, dlag_ref, dlab_ref, dlvg_ref, dlvb_ref, dspw_ref, dspb_ref):
                r[...] = jnp.zeros_like(r)

        da3 = jnp.concatenate([dc_ref[0], dc_ref[1]], axis=-1)
        xh, rstd = _ln_stats(a1_ref[...])
        a2 = xh * lag_ref[...] + lab_ref[...]
        s = _sigmoid(a2)
        da2 = da3 * (s * (1.0 + a2 * (1.0 - s)))
        dlag_ref[...] += jnp.sum(da2 * xh, axis=0, keepdims=True)
        dlab_ref[...] += jnp.sum(da2, axis=0, keepdims=True)
        dxh = da2 * lag_ref[...]
        da1 = rstd * (dxh - jnp.mean(dxh, axis=-1, keepdims=True) - xh * jnp.mean(dxh * xh, axis=-1, keepdims=True))
        da1_ref[...] = da1
        dcb_ref[...] += jnp.sum(da1, axis=0, keepdims=True)

        tril = _tril_mask()
        for g in range(groups):
            sl = slice(g * LANES, (g + 1) * LANES)
            xh, rstd = _ln_stats(v_ref[0][:, sl].astype(F32))
            lg = lvg_ref[:, sl]
            vnb = (xh * lg + lvb_ref[:, sl]).astype(BF16)
            w = jnp.where(tril, spw_ref[g], 0.0).astype(BF16)
            wt = jnp.where(tril.T, spwt_ref[g], 0.0).astype(BF16)
            bias = spb_ref[:, g:g + 1]
            dbo_all = dc_ref[2 + (g * LANES) // half][:, (g * LANES) % half:(g * LANES) % half + LANES]
            dvn_parts = []
            dw_acc = jnp.zeros((CHUNK, CHUNK), F32)
            db_acc = jnp.zeros((CHUNK, LANES), F32)
            for ch in range(tm // CHUNK):
                rows = slice(ch * CHUNK, (ch + 1) * CHUNK)
                vs = _nn(w, vnb[rows]) + bias
                dbo = dbo_all[rows]
                uv = u_ref[0][rows, sl].astype(F32)
                dz_ref[0, rows, sl] = (dbo * vs).astype(BF16)
                dvs = dbo * uv
                dvsb = dvs.astype(BF16)
                dvn_parts.append(_nn(wt, dvsb))
                dw_acc = dw_acc + _nt(dvsb, vnb[rows])
                db_acc = db_acc + dvs
            dvn = jnp.concatenate(dvn_parts, axis=0)
            dspw_ref[g] += jnp.where(tril, dw_acc, 0.0)
            dspb_ref[g] += db_acc
            dlvg_ref[:, sl] += jnp.sum(dvn * xh, axis=0, keepdims=True)
            dlvb_ref[:, sl] += jnp.sum(dvn, axis=0, keepdims=True)
            dxh = dvn * lg
            dv = rstd * (dxh - jnp.mean(dxh, axis=-1, keepdims=True) - xh * jnp.mean(dxh * xh, axis=-1, keepdims=True))
            dz_ref[1, :, sl] = dv.astype(BF16)

        @pl.when(i == last)
        def _():
            for g in range(groups):
                dspb_ref[g] = jnp.zeros((CHUNK, LANES), F32) + jnp.sum(dspb_ref[g], axis=-1, keepdims=True)

    unit = lambda u: pl.BlockSpec((1, tm, c), lambda i: (u, i, 0))
    vec = pl.BlockSpec((1, c), lambda i: (0, 0))
    sq = pl.BlockSpec((groups, CHUNK, CHUNK), lambda i: (0, 0, 0))
    vshape = jax.ShapeDtypeStruct((1, c), F32)
    sshape = jax.ShapeDtypeStruct((groups, CHUNK, CHUNK), F32)
    return pl.pallas_call(
        body, name="mix_bwd_point", grid=(t // tm,),
        in_specs=[pl.BlockSpec((4, tm, half), lambda i: (0, i, 0)), unit(2), unit(3),
                  pl.BlockSpec((tm, c), lambda i: (i, 0)), vec, vec, vec, vec, sq, sq,
                  pl.BlockSpec((CHUNK, groups), lambda i: (0, 0))],
        out_specs=[pl.BlockSpec((2, tm, c), lambda i: (1, i, 0)), pl.BlockSpec((tm, c), lambda i: (i, 0)),
                   vec, vec, vec, vec, vec, sq, sq],
        out_shape=[jax.ShapeDtypeStruct((4, t, c), BF16), jax.ShapeDtypeStruct((t, c), F32),
                   vshape, vshape, vshape, vshape, vshape, sshape, sshape],
        compiler_params=_cparams(("arbitrary",)),
    )(dcat, z, z, a1, ln_a_g, ln_a_b, ln_v_g, ln_v_b, sp_w, sp_wt, sp_bt)


def mix_bwd_conv(dz, da1, z, conv_w, seq):
    _, t, c = z.shape
    tm = _tile(seq, 512)
    tiles_per_seq = seq // tm
    hb = tm // HALO
    n_halo_blocks = t // HALO

    def body(dz_in_ref, d_ref, dh_ref, av_ref, ag_ref, hv_ref, hg_ref, cw_ref, dz_ref, dcw_ref, a0_ref, d1_ref):
        del dz_in_ref
        i = pl.program_id(0)
        _glu_into(a0_ref, av_ref, ag_ref, hv_ref, hg_ref, i % tiles_per_seq == 0)
        d1_ref[0:tm, :] = d_ref[...]
        d1_ref[tm:, :] = jnp.where((i + 1) % tiles_per_seq == 0, 0.0, dh_ref[...])

        @pl.when(i == 0)
        def _():
            dcw_ref[...] = jnp.zeros_like(dcw_ref)

        d1 = d_ref[...]
        da0 = jnp.zeros((tm, c), F32)
        for k in range(CONV_WIDTH):
            back = CONV_WIDTH - 1 - k
            da0 = da0 + cw_ref[k:k + 1, :] * d1_ref[pl.ds(back, tm), :]
            dcw_ref[k:k + 1, :] += jnp.sum(d1 * a0_ref[pl.ds(HALO - back, tm), :], axis=0, keepdims=True)
        av = av_ref[0].astype(F32)
        s = _sigmoid(ag_ref[0].astype(F32))
        dz_ref[0] = (da0 * s).astype(BF16)
        dz_ref[1] = (da0 * av * s * (1.0 - s)).astype(BF16)

    unit = lambda u: pl.BlockSpec((1, tm, c), lambda i: (u, i, 0))
    halo = lambda u: pl.BlockSpec((1, HALO, c), lambda i: (u, jnp.maximum(i * hb - 1, 0), 0))
    return pl.pallas_call(
        body, name="mix_bwd_conv", grid=(t // tm,),
        in_specs=[pl.BlockSpec(memory_space=pl.ANY), pl.BlockSpec((tm, c), lambda i: (i, 0)),
                  pl.BlockSpec((HALO, c), lambda i: (jnp.minimum((i + 1) * hb, n_halo_blocks - 1), 0)),
                  unit(0), unit(1), halo(0), halo(1), pl.BlockSpec((CONV_WIDTH, c), lambda i: (0, 0))],
        out_specs=[pl.BlockSpec((2, tm, c), lambda i: (0, i, 0)), pl.BlockSpec((CONV_WIDTH, c), lambda i: (0, 0))],
        out_shape=[jax.ShapeDtypeStruct(dz.shape, BF16), jax.ShapeDtypeStruct((CONV_WIDTH, c), F32)],
        scratch_shapes=[pltpu.VMEM((HALO + tm, c), F32), pltpu.VMEM((tm + HALO, c), F32)],
        input_output_aliases={0: 0},
        compiler_params=_cparams(("arbitrary",)),
    )(dz, da1, da1, z, z, z, z, conv_w)


CHIP_FLIPS = ((1, 0), (0, 1), (1, 1))
ANY = pl.BlockSpec(memory_space=pl.ANY)


def _place():
    return lax.axis_index("x"), lax.axis_index("y"), lax.axis_index("c")


def _flip(v, f):
    return 1 - v if f else v


def allgather_weights(shards, smalls):
    n, ns = len(shards), len(smalls)

    def body(*refs):
        ins, sins = refs[:n], refs[n:n + ns]
        outs, souts = refs[n + ns:2 * n + ns], refs[2 * n + ns:2 * n + 2 * ns]
        ici_send, ici_recv, d2d_send, d2d_recv, sm_send, sm_recv, loc = refs[2 * n + 2 * ns:]
        x, y, c = _place()
        k = 2 * x + y
        sibling = (x, y, 1 - c)
        pending = []
        for a in range(n):
            cp = pltpu.make_async_copy(ins[a], outs[a].at[k], loc.at[a])
            cp.start()
            pending.append(cp.wait)
        for a in range(ns):
            cp = pltpu.make_async_copy(sins[a], souts[a].at[k], loc.at[n + a])
            cp.start()
            pending.append(cp.wait)

        def half(a):
            hr = shards[a].shape[0] // 2
            return pl.ds(pl.multiple_of(c * hr, 16), hr)

        for a in range(n):
            for o, (fx, fy) in enumerate(CHIP_FLIPS):
                cp = pltpu.make_async_remote_copy(
                    src_ref=ins[a].at[half(a)], dst_ref=outs[a].at[k, half(a)],
                    send_sem=ici_send.at[3 * a + o], recv_sem=ici_recv.at[3 * a + o],
                    device_id=(_flip(x, fx), _flip(y, fy), c), device_id_type=MESH)
                cp.start()
                pending.append(cp.wait_send)
        for a in range(ns):
            for o, (fx, fy) in enumerate(CHIP_FLIPS):
                cp = pltpu.make_async_remote_copy(
                    src_ref=sins[a], dst_ref=souts[a].at[k],
                    send_sem=sm_send.at[3 * a + o], recv_sem=sm_recv.at[3 * a + o],
                    device_id=(_flip(x, fx), _flip(y, fy), c), device_id_type=MESH)
                cp.start()
                pending.append(cp.wait_send)
        for a in range(n):
            for o, (fx, fy) in enumerate(CHIP_FLIPS):
                kk = 2 * _flip(x, fx) + _flip(y, fy)
                landed = outs[a].at[kk, half(a)]
                pltpu.make_async_remote_copy(
                    src_ref=landed, dst_ref=landed, send_sem=ici_send.at[3 * a + o], recv_sem=ici_recv.at[3 * a + o],
                    device_id=sibling, device_id_type=MESH).wait_recv()
                cp = pltpu.make_async_remote_copy(
                    src_ref=landed, dst_ref=landed, send_sem=d2d_send.at[3 * a + o], recv_sem=d2d_recv.at[3 * a + o],
                    device_id=sibling, device_id_type=MESH)
                cp.start()
                pending.append(cp.wait_send)
        for a in range(n):
            hr = shards[a].shape[0] // 2
            other = pl.ds(pl.multiple_of((1 - c) * hr, 16), hr)
            for o, (fx, fy) in enumerate(CHIP_FLIPS):
                kk = 2 * _flip(x, fx) + _flip(y, fy)
                got = outs[a].at[kk, other]
                pltpu.make_async_remote_copy(
                    src_ref=got, dst_ref=got, send_sem=d2d_send.at[3 * a + o], recv_sem=d2d_recv.at[3 * a + o],
                    device_id=sibling, device_id_type=MESH).wait_recv()
        for a in range(ns):
            for o, (fx, fy) in enumerate(CHIP_FLIPS):
                kk = 2 * _flip(x, fx) + _flip(y, fy)
                got = souts[a].at[kk]
                pltpu.make_async_remote_copy(
                    src_ref=got, dst_ref=got, send_sem=sm_send.at[3 * a + o], recv_sem=sm_recv.at[3 * a + o],
                    device_id=sibling, device_id_type=MESH).wait_recv()
        for w in pending:
            w()

    out_shape = ([jax.ShapeDtypeStruct((N_CHIPS,) + s.shape, s.dtype) for s in shards]
                 + [jax.ShapeDtypeStruct((N_CHIPS,) + s.shape, s.dtype) for s in smalls])
    dma = pltpu.SemaphoreType.DMA
    res = pl.pallas_call(
        body, name="allgather_weights", in_specs=[ANY] * (n + ns), out_specs=[ANY] * (n + ns), out_shape=out_shape,
        scratch_shapes=[dma((3 * n,)), dma((3 * n,)), dma((3 * n,)), dma((3 * n,)), dma((3 * ns,)), dma((3 * ns,)),
                        dma((n + ns,))],
        compiler_params=pltpu.CompilerParams(has_side_effects=True),
    )(*shards, *smalls)
    return res[:n], res[n:]


def rs_exchange(grads):
    n = len(grads)

    def body(*refs):
        ins, outs = refs[:n], refs[n:2 * n]
        send, recv = refs[2 * n:]
        x, y, c = _place()
        cps = []
        for a in range(n):
            cp = pltpu.make_async_remote_copy(
                src_ref=ins[a].at[:, 1 - c], dst_ref=outs[a], send_sem=send.at[a], recv_sem=recv.at[a],
                device_id=(x, y, 1 - c), device_id_type=MESH)
            cp.start()
            cps.append(cp)
        for cp in cps:
            cp.wait()

    dma = pltpu.SemaphoreType.DMA
    return pl.pallas_call(
        body, name="rs_exchange", in_specs=[ANY] * n, out_specs=[ANY] * n,
        out_shape=[jax.ShapeDtypeStruct((g.shape[0],) + g.shape[2:], g.dtype) for g in grads],
        scratch_shapes=[dma((n,)), dma((n,))],
        compiler_params=pltpu.CompilerParams(has_side_effects=True),
    )(*grads)


def rs_add(g, sib, core, out_dtype, name):
    nk, _, hr, cc = g.shape
    rb = _tile(hr, 256)

    def body(core_ref, g_ref, s_ref, o_ref):
        del core_ref
        o_ref[0] = (g_ref[0, 0] + s_ref[0]).astype(out_dtype)

    return pl.pallas_call(
        body, name=name,
        grid_spec=pltpu.PrefetchScalarGridSpec(
            num_scalar_prefetch=1, grid=(nk, hr // rb),
            in_specs=[pl.BlockSpec((1, 1, rb, cc), lambda k, i, core_ref: (k, core_ref[0], i, 0)),
                      pl.BlockSpec((1, rb, cc), lambda k, i, core_ref: (k, i, 0))],
            out_specs=pl.BlockSpec((1, rb, cc), lambda k, i, core_ref: (k, i, 0))),
        out_shape=jax.ShapeDtypeStruct((nk, hr, cc), out_dtype),
        compiler_params=_cparams(("parallel", "parallel")),
    )(core, g, sib)


def rs_send(parts):
    n = len(parts)

    def body(*refs):
        ins, outs = refs[:n], refs[n:2 * n]
        send, recv, loc = refs[2 * n:]
        x, y, c = _place()
        k = 2 * x + y
        waits = []
        for a in range(n):
            cp = pltpu.make_async_copy(ins[a].at[k], outs[a].at[3], loc.at[a])
            cp.start()
            waits.append(cp.wait)
            for o, (fx, fy) in enumerate(CHIP_FLIPS):
                kk = 2 * _flip(x, fx) + _flip(y, fy)
                cp = pltpu.make_async_remote_copy(
                    src_ref=ins[a].at[kk], dst_ref=outs[a].at[o], send_sem=send.at[3 * a + o], recv_sem=recv.at[3 * a + o],
                    device_id=(_flip(x, fx), _flip(y, fy), c), device_id_type=MESH)
                cp.start()
                waits.append(cp.wait)
        for w in waits:
            w()

    dma = pltpu.SemaphoreType.DMA
    return pl.pallas_call(
        body, name="rs_send", in_specs=[ANY] * n, out_specs=[ANY] * n,
        out_shape=[jax.ShapeDtypeStruct(p.shape, p.dtype) for p in parts],
        scratch_shapes=[dma((3 * n,)), dma((3 * n,)), dma((n,))],
        compiler_params=pltpu.CompilerParams(has_side_effects=True),
    )(*parts)


def rs_sum(recv, name):
    _, hr, cc = recv.shape
    rb = _tile(hr, 256)

    def body(r_ref, o_ref):
        o_ref[...] = ((r_ref[3].astype(F32) + r_ref[0].astype(F32)) + r_ref[1].astype(F32)) + r_ref[2].astype(F32)

    return pl.pallas_call(
        body, name=name, grid=(hr // rb,),
        in_specs=[pl.BlockSpec((4, rb, cc), lambda i: (0, i, 0))],
        out_specs=pl.BlockSpec((rb, cc), lambda i: (i, 0)),
        out_shape=jax.ShapeDtypeStruct((hr, cc), F32),
        compiler_params=_cparams(("parallel",)),
    )(recv)


def rs_share(sums, groups):
    n = len(sums)

    def body(*refs):
        ins, outs = refs[:n], refs[n:n + len(groups)]
        send, recv, loc = refs[n + len(groups):]
        x, y, c = _place()
        waits = []
        for gi, members in enumerate(groups):
            for layer, a in enumerate(members):
                cp = pltpu.make_async_copy(ins[a], outs[gi].at[layer, c], loc.at[a])
                cp.start()
                waits.append(cp.wait)
                cp = pltpu.make_async_remote_copy(
                    src_ref=ins[a], dst_ref=outs[gi].at[layer, c], send_sem=send.at[a], recv_sem=recv.at[a],
                    device_id=(x, y, 1 - c), device_id_type=MESH)
                cp.start()
                waits.append(cp.wait_send)
        for gi, members in enumerate(groups):
            for layer, a in enumerate(members):
                got = outs[gi].at[layer, 1 - c]
                pltpu.make_async_remote_copy(
                    src_ref=got, dst_ref=got, send_sem=send.at[a], recv_sem=recv.at[a],
                    device_id=(x, y, 1 - c), device_id_type=MESH).wait_recv()
        for w in waits:
            w()

    dma = pltpu.SemaphoreType.DMA
    return pl.pallas_call(
        body, name="rs_share", in_specs=[ANY] * n, out_specs=[ANY] * len(groups),
        out_shape=[jax.ShapeDtypeStruct((len(m), 2) + sums[m[0]].shape, F32) for m in groups],
        scratch_shapes=[dma((n,)), dma((n,)), dma((n,))],
        compiler_params=pltpu.CompilerParams(has_side_effects=True),
    )(*sums)


def allreduce_small(v):
    r, w = v.shape

    def body(v_ref, o_ref, buf, send, recv, loc):
        x, y, c = _place()
        me = 4 * x + 2 * y + c
        mine = pltpu.make_async_copy(v_ref, buf.at[me], loc)
        mine.start()
        cps = []
        for o in range(1, N_DEV):
            fx, fy, fc = (o >> 2) & 1, (o >> 1) & 1, o & 1
            cp = pltpu.make_async_remote_copy(
                src_ref=v_ref, dst_ref=buf.at[me], send_sem=send.at[o - 1], recv_sem=recv.at[o - 1],
                device_id=(_flip(x, fx), _flip(y, fy), _flip(c, fc)), device_id_type=MESH)
            cp.start()
            cps.append(cp)
        for o in range(1, N_DEV):
            fx, fy, fc = (o >> 2) & 1, (o >> 1) & 1, o & 1
            peer = 4 * _flip(x, fx) + 2 * _flip(y, fy) + _flip(c, fc)
            pltpu.make_async_remote_copy(
                src_ref=v_ref, dst_ref=buf.at[peer], send_sem=send.at[o - 1], recv_sem=recv.at[o - 1],
                device_id=(x, y, c), device_id_type=MESH).wait_recv()
        for cp in cps:
            cp.wait_send()
        mine.wait()
        acc = buf[0]
        for d in range(1, N_DEV):
            acc = acc + buf[d]
        o_ref[...] = acc

    dma = pltpu.SemaphoreType.DMA
    vm = pl.BlockSpec(memory_space=pltpu.VMEM)
    return pl.pallas_call(
        body, name="allreduce_small", in_specs=[vm], out_specs=vm,
        out_shape=jax.ShapeDtypeStruct((r, w), F32),
        scratch_shapes=[pltpu.VMEM((N_DEV, r, w), F32), dma((N_DEV - 1,)), dma((N_DEV - 1,)), dma],
        compiler_params=pltpu.CompilerParams(has_side_effects=True, vmem_limit_bytes=VMEM_LIMIT),
    )(v)


def adamw(w, g, m, v, name):
    r, cc = w.shape
    rb = _tile(r, 256)

    def body(w_ref, g_ref, m_ref, v_ref, d_ref, nm_ref, nv_ref):
        gv = g_ref[...]
        nm = ADAM_B1 * m_ref[...] + (1.0 - ADAM_B1) * gv
        nv = ADAM_B2 * v_ref[...] + (1.0 - ADAM_B2) * (gv * gv)
        m_hat = nm / (1.0 - ADAM_B1 ** ADAM_STEP)
        v_hat = nv / (1.0 - ADAM_B2 ** ADAM_STEP)
        d_ref[...] = -ADAM_LR * (m_hat / (jnp.sqrt(v_hat) + ADAM_EPS) + ADAM_WD * w_ref[...])
        nm_ref[...] = nm
        nv_ref[...] = nv

    blk = pl.BlockSpec((rb, cc), lambda i: (i, 0))
    shp = jax.ShapeDtypeStruct((r, cc), F32)
    return pl.pallas_call(
        body, name=name, grid=(r // rb,), in_specs=[blk] * 4, out_specs=[blk] * 3, out_shape=[shp] * 3,
        compiler_params=_cparams(("parallel",)),
    )(w, g, m, v)


WEIGHTS = ['g_ffn1', 'w_ffn1_gate', 'w_ffn1_up', 'w_ffn1_down', 'g_mix', 'w_in_ab', 'conv_w', 'conv_b', 'ln_a_g',
           'ln_a_b', 'ln_v_g', 'ln_v_b', 'sp_w', 'sp_b', 'w_out_ab', 'w_qkv', 'w_o', 'g_ffn2', 'w_ffn2_gate',
           'w_ffn2_up', 'w_ffn2_down', 'g_final']
BIG = ['w_ffn1_gate', 'w_ffn1_up', 'w_ffn1_down', 'w_in_ab', 'w_out_ab', 'w_qkv', 'w_o', 'w_ffn2_gate', 'w_ffn2_up',
       'w_ffn2_down']
SMALL = ['g_ffn1', 'g_mix', 'g_ffn2', 'g_final', 'conv_b', 'ln_a_g', 'ln_a_b', 'ln_v_g', 'ln_v_b', 'sp_b', 'sp_w']


def _rows(a):
    return a.reshape(-1, LANES)


def _pack(parts):
    v = jnp.concatenate([_rows(p) for p in parts], axis=0)
    pad = (-v.shape[0]) % 8
    return jnp.pad(v, ((0, pad), (0, 0)))


def _unpack(v, shapes):
    out, r = [], 0
    for s in shapes:
        n = 1
        for d in s:
            n *= d
        n //= LANES
        out.append(v[r:r + n].reshape(s))
        r += n
    return out


def kernel(x, g_ffn1, w_ffn1_gate, w_ffn1_up, w_ffn1_down, g_mix, w_in_ab, conv_w, conv_b, ln_a_g, ln_a_b, ln_v_g, ln_v_b, sp_w, sp_b, w_out_ab, w_qkv, w_o, g_ffn2, w_ffn2_gate, w_ffn2_up, w_ffn2_down, g_final, loss_target, m_g_ffn1, m_w_ffn1_gate, m_w_ffn1_up, m_w_ffn1_down, m_g_mix, m_w_in_ab, m_conv_w, m_conv_b, m_ln_a_g, m_ln_a_b, m_ln_v_g, m_ln_v_b, m_sp_w, m_sp_b, m_w_out_ab, m_w_qkv, m_w_o, m_g_ffn2, m_w_ffn2_gate, m_w_ffn2_up, m_w_ffn2_down, m_g_final, v_g_ffn1, v_w_ffn1_gate, v_w_ffn1_up, v_w_ffn1_down, v_g_mix, v_w_in_ab, v_conv_w, v_conv_b, v_ln_a_g, v_ln_a_b, v_ln_v_g, v_ln_v_b, v_sp_w, v_sp_b, v_w_out_ab, v_w_qkv, v_w_o, v_g_ffn2, v_w_ffn2_gate, v_w_ffn2_up, v_w_ffn2_down, v_g_final):
    p = dict(locals())
    n_seq, seq, d = x.shape
    t = n_seq * seq
    depth = g_ffn1.shape[0]
    core = lax.axis_index("c")
    chip = 2 * lax.axis_index("x") + lax.axis_index("y")
    xf = x.reshape(t, d)
    target = loss_target.reshape(t, d)

    items = []
    for name in BIG:
        for layer in range(p[name].shape[0]):
            items.append((name, layer))
    shards = [p[name][layer].astype(BF16) for name, layer in items]
    gathered, (conv_w4,) = allgather_weights(shards, [conv_w[0]])
    wt = {it: g for it, g in zip(items, gathered)}
    c_mix = conv_w4.shape[2] * N_CHIPS
    conv_full = jnp.transpose(conv_w4, (1, 0, 2)).reshape(CONV_WIDTH, c_mix)
    vec = lambda a: a.reshape(1, -1)
    sp_bt = sp_b[0].T
    sp_wt = jnp.transpose(sp_w[0], (0, 2, 1))
    d_ff = w_ffn1_gate.shape[2]
    n_in = w_in_ab.shape[2]
    n_qkv = w_qkv.shape[2] // 3

    saved = []
    xc = xf
    for layer in range(depth):
        s = {}
        for half, (gn, wn) in enumerate((('g_ffn1', 'w_ffn1'), ('g_ffn2', 'w_ffn2'))):
            if half == 1:
                s['x_mix'] = xc
                s['h_mix'] = rmsnorm_fwd(xc, vec(g_mix[layer]), "norm_mix")
                if layer % 2 == 0:
                    (z,) = colmm(s['h_mix'], [wt[('w_in_ab', layer // 2)]], n_in, BF16, "mm_in")
                    cat, a1 = mix_fwd(z, conv_full, conv_b, ln_a_g, ln_a_b, vec(ln_v_g), vec(ln_v_b), sp_w[0], sp_bt, seq)
                    s.update(z=z, cat=cat, a1=a1)
                    xc = rowmm([cat], wt[('w_out_ab', layer // 2)], xc, 1.0, "mm_out")
                else:
                    (qkv,) = colmm(s['h_mix'], [wt[('w_qkv', layer // 2)]], n_qkv, BF16, "mm_qkv")
                    o, tot = attn_fwd(qkv, n_seq, seq)
                    s.update(qkv=qkv, o=o, tot=tot)
                    xc = rowmm([o], wt[('w_o', layer // 2)], xc, 1.0, "mm_o")
            s['x' + wn] = xc
            h = rmsnorm_fwd(xc, vec(p[gn][layer]), "norm_ffn")
            gate, up = colmm(h, [wt[(wn + '_gate', layer)], wt[(wn + '_up', layer)]], d_ff, BF16, "ffn_gateup")
            xc = rowmm([gate, up], wt[(wn + '_down', layer)], xc, 0.5, "ffn_down")
            s.update({'h' + wn: h, 'gate' + wn: gate, 'up' + wn: up})
        saved.append(s)

    loss8, dx, dxb, dg_final = loss_head(xc, vec(g_final), target)
    loss = lax.psum(loss8[0, 0], ("x", "y", "c"))

    gw = {}
    gs = {}
    for layer in reversed(range(depth)):
        s = saved[layer]
        for half, (gn, wn) in reversed(list(enumerate((('g_ffn1', 'w_ffn1'), ('g_ffn2', 'w_ffn2'))))):
            wd = wt[(wn + '_down', layer)]
            dgate, dup, act = rowmm_t(dxb, wd, 0.5, BF16, "ffn_bwd_act", gu=(s['gate' + wn], s['up' + wn]))
            gw[(wn + '_down', layer)] = dw_row(act, dxb, 0.5, "ffn_dw_down")
            gw[(wn + '_gate', layer)], gw[(wn + '_up', layer)] = dw_col(s['h' + wn], [dgate, dup], N_CHIPS, d_ff, "ffn_dw_gateup")
            dx, dxb, dg = colmm_t([dgate, dup], [wt[(wn + '_gate', layer)], wt[(wn + '_up', layer)]], d_ff,
                                  s['x' + wn], vec(p[gn][layer]), dx, "ffn_bwd_in")
            gs[(gn, layer)] = dg
            if half == 1:
                if layer % 2 == 0:
                    i = layer // 2
                    w_out = wt[('w_out_ab', i)]
                    dcat = rowmm_t(dxb, w_out, 1.0, F32, "mm_out_t")
                    gw[('w_out_ab', i)] = dw_row(s['cat'], dxb, 1.0, "dw_out")
                    dz, da1, dcb, dlag, dlab, dlvg, dlvb, dspw, dspb = mix_bwd_point(
                        dcat, s['z'], s['a1'], ln_a_g, ln_a_b, vec(ln_v_g), vec(ln_v_b), sp_w[0], sp_wt, sp_bt, seq)
                    dz, dcw = mix_bwd_conv(dz, da1, s['z'], conv_full, seq)
                    gs.update({('conv_b', i): dcb, ('ln_a_g', i): dlag, ('ln_a_b', i): dlab, ('ln_v_g', i): dlvg,
                               ('ln_v_b', i): dlvb, ('sp_w', i): dspw, ('sp_b', i): dspb[:, :, 0], ('conv_w', i): dcw})
                    (gw[('w_in_ab', i)],) = dw_col(s['h_mix'], [dz], N_CHIPS, n_in, "dw_in")
                    dx, dxb, dg = colmm_t([dz], [wt[('w_in_ab', i)]], n_in, s['x_mix'], vec(g_mix[layer]), dx, "mm_in_t")
                else:
                    i = layer // 2
                    w_o4 = wt[('w_o', i)]
                    do = rowmm_t(dxb, w_o4, 1.0, BF16, "mm_o_t")
                    gw[('w_o', i)] = dw_row(s['o'], dxb, 1.0, "dw_o")
                    dq, dk, dv = attn_bwd(s['qkv'], do, s['tot'], n_seq, seq)
                    dqkv = jnp.concatenate([dq, dk, dv], axis=0)
                    (gw[('w_qkv', i)],) = dw_col(s['h_mix'], [dqkv], N_CHIPS, n_qkv, "dw_qkv")
                    dx, dxb, dg = colmm_t([dqkv], [wt[('w_qkv', i)]], n_qkv, s['x_mix'], vec(g_mix[layer]), dx, "mm_qkv_t")
                gs[('g_mix', layer)] = dg
    grad_x = dx.reshape(x.shape)

    core1 = core.reshape(1).astype(jnp.int32)
    g4 = [gw[it].reshape(N_CHIPS, 2, gw[it].shape[1] // 2, gw[it].shape[2]) for it in items]
    sib = rs_exchange(g4)
    parts = [rs_add(g, sb, core1, F32, "rs_add") for g, sb in zip(g4, sib)]
    recv = rs_send(parts)
    sums = [rs_sum(r, "rs_sum") for r in recv]
    groups = [[items.index((name, layer)) for layer in range(p[name].shape[0])] for name in BIG]
    shared = rs_share(sums, groups)
    grads = {name: sh.reshape(p[name].shape) for name, sh in zip(BIG, shared)}

    stack = lambda name: jnp.concatenate([gs[(name, layer)].reshape((1,) + p[name].shape[1:]) for layer in range(p[name].shape[0])], axis=0)
    small_g = [stack(name) if name != 'g_final' else dg_final.reshape(p[name].shape) for name in SMALL]
    packed = _pack(small_g + [gs[('conv_w', 0)]])
    red = allreduce_small(packed)
    outs = _unpack(red, [p[name].shape for name in SMALL] + [(CONV_WIDTH, c_mix)])
    for name, g in zip(SMALL, outs[:-1]):
        grads[name] = g
    conv_g = outs[-1].reshape(CONV_WIDTH, N_CHIPS, c_mix // N_CHIPS)
    grads['conv_w'] = lax.dynamic_index_in_dim(conv_g, chip, axis=1, keepdims=False).reshape(conv_w.shape)

    delta, new_m, new_v = {}, {}, {}
    for name in BIG:
        shp = p[name].shape
        two = lambda a: a.reshape(shp[0] * shp[1], shp[2])
        dl, nm, nv = adamw(two(p[name]), two(grads[name]), two(p['m_' + name]), two(p['v_' + name]), "adamw")
        delta[name], new_m[name], new_v[name] = dl.reshape(shp), nm.reshape(shp), nv.reshape(shp)
    small_names = SMALL + ['conv_w']
    pk = lambda pre: _pack([p[pre + name] for name in small_names])
    dl, nm, nv = adamw(pk(''), _pack([grads[name] for name in small_names]), pk('m_'), pk('v_'), "adamw_small")
    shapes = [p[name].shape for name in small_names]
    for dst, val in ((delta, dl), (new_m, nm), (new_v, nv)):
        for name, a in zip(small_names, _unpack(val, shapes)):
            dst[name] = a

    return (loss, grad_x, *[grads[n] for n in WEIGHTS], *[delta[n] for n in WEIGHTS],
            *[new_m[n] for n in WEIGHTS], *[new_v[n] for n in WEIGHTS])
```

```python
import functools

import jax
import jax.numpy as jnp
from jax import lax
from jax.experimental import pallas as pl
from jax.experimental.pallas import tpu as pltpu

F32 = jnp.float32
BF16 = jnp.bfloat16
EPS = 1e-6
HEAD_DIM = 64
CONV_WIDTH = 31
CHUNK = 128
KBLK = 128
ATT_BLOCK = 512
LANES = 128
HALO = 32
ADAM_LR, ADAM_B1, ADAM_B2, ADAM_EPS, ADAM_WD, ADAM_STEP = 0.001, 0.9, 0.999, 1e-08, 0.01, 10
VMEM_LIMIT = 56 * 1024 * 1024
MESH = pl.DeviceIdType.MESH
N_CHIPS = 4
N_DEV = 8
REDUCE_DTYPE = BF16


def _cparams(sem):
    return pltpu.CompilerParams(dimension_semantics=sem, vmem_limit_bytes=VMEM_LIMIT)


def _nt(a, b):
    return lax.dot_general(a, b, (((1,), (1,)), ((), ())), preferred_element_type=F32)


def _tn(a, b):
    return lax.dot_general(a, b, (((0,), (0,)), ((), ())), preferred_element_type=F32)


def _nn(a, b):
    return jnp.dot(a, b, preferred_element_type=F32)


def _sigmoid(x):
    return 1.0 / (1.0 + jnp.exp(-x))


def _tile(t, want):
    if t <= want:
        return t
    for cand in range(want - want % 8, 7, -8):
        if t % cand == 0:
            return cand
    raise ValueError((t, want))


def rmsnorm_fwd(x, g, name):
    t, d = x.shape
    tm = _tile(t, 512)

    def body(x_ref, g_ref, h_ref):
        xv = x_ref[...]
        r = lax.rsqrt(jnp.mean(xv * xv, axis=-1, keepdims=True) + EPS)
        h_ref[...] = (xv * r * g_ref[...]).astype(BF16)

    return pl.pallas_call(
        body, name=name, grid=(t // tm,),
        in_specs=[pl.BlockSpec((tm, d), lambda i: (i, 0)), pl.BlockSpec((1, d), lambda i: (0, 0))],
        out_specs=pl.BlockSpec((tm, d), lambda i: (i, 0)),
        out_shape=jax.ShapeDtypeStruct((t, d), BF16),
        compiler_params=_cparams(("parallel",)),
    )(x, g)


def colmm(h, ws, nu, out_dtype, name):
    t, k = h.shape
    j, _, nj = ws[0].shape
    per = nj // nu
    units = j * per
    tm = _tile(t, 512)
    nw = len(ws)

    def body(*refs):
        h_ref = refs[0]
        hv = h_ref[...]
        for n in range(nw):
            refs[1 + nw + n][0] = _nn(hv, refs[1 + n][0]).astype(out_dtype)

    w_spec = pl.BlockSpec((1, k, nu), lambda u, i: (u // per, 0, u % per))
    o_spec = pl.BlockSpec((1, tm, nu), lambda u, i: (u, i, 0))
    outs = pl.pallas_call(
        body, name=name, grid=(units, t // tm),
        in_specs=[pl.BlockSpec((tm, k), lambda u, i: (i, 0))] + [w_spec] * nw,
        out_specs=[o_spec] * nw,
        out_shape=[jax.ShapeDtypeStruct((units, t, nu), out_dtype)] * nw,
        compiler_params=_cparams(("parallel", "parallel")),
    )(h, *ws)
    return outs


def rowmm(a_list, w, resid, scale, name):
    swiglu = len(a_list) == 2
    u_n, t, ku = a_list[0].shape
    n = w.shape[2]
    tm = _tile(t, 256)

    def body(*refs):
        a_refs = refs[:len(a_list)]
        w_ref, r_ref, o_ref = refs[len(a_list):]
        acc = jnp.zeros((tm, n), F32)
        for u in range(u_n):
            if swiglu:
                gv = a_refs[0][u].astype(F32)
                av = (gv * _sigmoid(gv) * a_refs[1][u].astype(F32)).astype(BF16)
            else:
                av = a_refs[0][u]
            acc = acc + _nn(av, w_ref[u])
        o_ref[...] = r_ref[...] + scale * acc

    a_spec = pl.BlockSpec((u_n, tm, ku), lambda i: (0, i, 0))
    return pl.pallas_call(
        body, name=name, grid=(t // tm,),
        in_specs=[a_spec] * len(a_list) + [pl.BlockSpec((u_n, ku, n), lambda i: (0, 0, 0)),
                                           pl.BlockSpec((tm, n), lambda i: (i, 0))],
        out_specs=pl.BlockSpec((tm, n), lambda i: (i, 0)),
        out_shape=jax.ShapeDtypeStruct((t, n), F32),
        compiler_params=_cparams(("parallel",)),
    )(*a_list, w, resid)


def rowmm_t(dyb, w, scale, out_dtype, name, gu=None):
    t, n = dyb.shape
    u_n, ku, _ = w.shape
    tm = _tile(t, 512)

    def body(*refs):
        if gu is None:
            dy_ref, w_ref, o_ref = refs
            o_ref[0] = (scale * _nt(dy_ref[...], w_ref[0])).astype(out_dtype)
        else:
            dy_ref, w_ref, g_ref, u_ref, dg_ref, du_ref, a_ref = refs
            dact = scale * _nt(dy_ref[...], w_ref[0])
            gv = g_ref[0].astype(F32)
            uv = u_ref[0].astype(F32)
            s = _sigmoid(gv)
            silu = gv * s
            dg_ref[0] = (dact * uv * (s * (1.0 + gv * (1.0 - s)))).astype(BF16)
            du_ref[0] = (dact * silu).astype(BF16)
            a_ref[0] = (silu * uv).astype(BF16)

    blk = pl.BlockSpec((1, tm, ku), lambda u, i: (u, i, 0))
    in_specs = [pl.BlockSpec((tm, n), lambda u, i: (i, 0)), pl.BlockSpec((1, ku, n), lambda u, i: (u, 0, 0))]
    if gu is None:
        return pl.pallas_call(
            body, name=name, grid=(u_n, t // tm), in_specs=in_specs, out_specs=blk,
            out_shape=jax.ShapeDtypeStruct((u_n, t, ku), out_dtype),
            compiler_params=_cparams(("parallel", "parallel")),
        )(dyb, w)
    return pl.pallas_call(
        body, name=name, grid=(u_n, t // tm), in_specs=in_specs + [blk, blk], out_specs=[blk] * 3,
        out_shape=[jax.ShapeDtypeStruct((u_n, t, ku), BF16)] * 3,
        compiler_params=_cparams(("parallel", "parallel")),
    )(dyb, w, *gu)


def colmm_t(dzs, ws, nu, x, g, dy_in, name):
    t, k = x.shape
    j, _, nj = ws[0].shape
    per = nj // nu
    units = j * per
    nw = len(ws)
    tm = _tile(t, 256)

    def body(*refs):
        dz_refs = refs[:nw]
        w_refs = refs[nw:2 * nw]
        x_ref, g_ref, dy_ref, dx_ref, dxb_ref, dg_ref = refs[2 * nw:]
        i = pl.program_id(0)
        dh = jnp.zeros((tm, k), F32)
        for n in range(nw):
            for u in range(units):
                wv = w_refs[n][u // per, :, (u % per) * nu:(u % per + 1) * nu]
                dh = dh + _nt(dz_refs[n][u], wv)
        xv = x_ref[...]
        gv = g_ref[...]
        r = lax.rsqrt(jnp.mean(xv * xv, axis=-1, keepdims=True) + EPS)
        uu = dh * gv
        dx = dy_ref[...] + r * uu - xv * (r * r * r * jnp.mean(uu * xv, axis=-1, keepdims=True))
        dx_ref[...] = dx
        dxb_ref[...] = dx.astype(BF16)
        part = jnp.sum(dh * (xv * r), axis=0, keepdims=True)

        @pl.when(i == 0)
        def _():
            dg_ref[...] = part

        @pl.when(i > 0)
        def _():
            dg_ref[...] += part

    dz_spec = pl.BlockSpec((units, tm, nu), lambda i: (0, i, 0))
    w_spec = pl.BlockSpec((j, k, nj), lambda i: (0, 0, 0))
    row = pl.BlockSpec((tm, k), lambda i: (i, 0))
    vec = pl.BlockSpec((1, k), lambda i: (0, 0))
    return pl.pallas_call(
        body, name=name, grid=(t // tm,),
        in_specs=[dz_spec] * nw + [w_spec] * nw + [row, vec, row],
        out_specs=[row, row, vec],
        out_shape=[jax.ShapeDtypeStruct((t, k), F32), jax.ShapeDtypeStruct((t, k), BF16),
                   jax.ShapeDtypeStruct((1, k), F32)],
        compiler_params=_cparams(("arbitrary",)),
    )(*dzs, *ws, x, g, dy_in)


def dw_col(h, dzs, j, nu, name):
    t, k = h.shape
    units = dzs[0].shape[0]
    per = units // j
    nw = len(dzs)
    tt = _tile(t, 512)

    def body(*refs):
        h_ref = refs[0]
        s = pl.program_id(1)
        hv = h_ref[...]
        for n in range(nw):
            part = _tn(hv, refs[1 + n][0])
            o_ref = refs[1 + nw + n]

            @pl.when(s == 0)
            def _():
                o_ref[0] = part

            @pl.when(s > 0)
            def _():
                o_ref[0] += part

    return pl.pallas_call(
        body, name=name, grid=(units, t // tt),
        in_specs=[pl.BlockSpec((tt, k), lambda u, s: (s, 0))] + [pl.BlockSpec((1, tt, nu), lambda u, s: (u, s, 0))] * nw,
        out_specs=[pl.BlockSpec((1, k, nu), lambda u, s: (u // per, 0, u % per))] * nw,
        out_shape=[jax.ShapeDtypeStruct((j, k, per * nu), F32)] * nw,
        compiler_params=_cparams(("parallel", "arbitrary")),
    )(h, *dzs)


def dw_row(a, dyb, scale, name):
    u_n, t, ku = a.shape
    n = dyb.shape[1]
    tt = _tile(t, 512)

    def body(a_ref, dy_ref, o_ref):
        s = pl.program_id(1)
        part = scale * _tn(a_ref[0], dy_ref[...])

        @pl.when(s == 0)
        def _():
            o_ref[0] = part

        @pl.when(s > 0)
        def _():
            o_ref[0] += part

    return pl.pallas_call(
        body, name=name, grid=(u_n, t // tt),
        in_specs=[pl.BlockSpec((1, tt, ku), lambda u, s: (u, s, 0)), pl.BlockSpec((tt, n), lambda u, s: (s, 0))],
        out_specs=pl.BlockSpec((1, ku, n), lambda u, s: (u, 0, 0)),
        out_shape=jax.ShapeDtypeStruct((u_n, ku, n), F32),
        compiler_params=_cparams(("parallel", "arbitrary")),
    )(a, dyb)


def loss_head(x, g, target):
    t, d = x.shape
    tm = _tile(t, 256)

    def body(x_ref, g_ref, t_ref, loss_ref, dx_ref, dxb_ref, dg_ref):
        i = pl.program_id(0)
        xv = x_ref[...]
        gv = g_ref[...]
        r = lax.rsqrt(jnp.mean(xv * xv, axis=-1, keepdims=True) + EPS)
        xh = xv * r
        err = xh * gv - t_ref[...]
        dy = err * (1.0 / d)
        uu = dy * gv
        dx = r * uu - xv * (r * r * r * jnp.mean(uu * xv, axis=-1, keepdims=True))
        dx_ref[...] = dx
        dxb_ref[...] = dx.astype(BF16)
        dg_part = jnp.sum(dy * xh, axis=0, keepdims=True)
        row = jnp.sum(err * err, axis=-1, keepdims=True) * (0.5 / d)
        l_part = jnp.zeros((8, LANES), F32) + jnp.sum(row, axis=0, keepdims=True)

        @pl.when(i == 0)
        def _():
            dg_ref[...] = dg_part
            loss_ref[...] = l_part

        @pl.when(i > 0)
        def _():
            dg_ref[...] += dg_part
            loss_ref[...] += l_part

    row = pl.BlockSpec((tm, d), lambda i: (i, 0))
    vec = pl.BlockSpec((1, d), lambda i: (0, 0))
    return pl.pallas_call(
        body, name="loss_head", grid=(t // tm,),
        in_specs=[row, vec, row],
        out_specs=[pl.BlockSpec((8, LANES), lambda i: (0, 0)), row, row, vec],
        out_shape=[jax.ShapeDtypeStruct((8, LANES), F32), jax.ShapeDtypeStruct((t, d), F32),
                   jax.ShapeDtypeStruct((t, d), BF16), jax.ShapeDtypeStruct((1, d), F32)],
        compiler_params=_cparams(("arbitrary",)),
    )(x, g, target)


def _split(v):
    hi = v.astype(BF16)
    lo = (v - hi.astype(F32)).astype(BF16)
    return hi, lo


def _keysums(v, m_ext):
    hi, lo = _split(v)
    outs = []
    for j in range(v.shape[1] // KBLK):
        sl = slice(j * KBLK, (j + 1) * KBLK)
        cs = _nn(jnp.concatenate([hi[:, sl], lo[:, sl]], axis=1), m_ext)
        outs.append((cs[:, :KBLK], cs[:, KBLK:]))
    return outs


def _softplus_parts(z):
    sp = jnp.maximum(z, 0.0) + jnp.log(1.0 + jnp.exp(-jnp.abs(z)))
    return sp, z - sp


def _sum_matrices():
    r = lax.broadcasted_iota(jnp.int32, (2 * KBLK, 2 * KBLK), 0) % KBLK
    c = lax.broadcasted_iota(jnp.int32, (2 * KBLK, 2 * KBLK), 1)
    suffix = jnp.where((r > c) | (c >= KBLK), 1.0, 0.0).astype(BF16)
    prefix = jnp.where((r <= c) | (c >= KBLK), 1.0, 0.0).astype(BF16)
    return suffix, prefix


def attn_fwd(qkv, n_seq, seq):
    t = qkv.shape[1]
    n_pairs = (qkv.shape[0] // 3) * 2
    bq = min(ATT_BLOCK, seq)
    nq = seq // bq
    nsub = bq // KBLK
    suffix_m, _ = _sum_matrices()

    def body(q_ref, k_ref, v_ref, m_ref, o_ref, tot_ref):
        qi = pl.program_id(2)
        lane = lax.broadcasted_iota(jnp.int32, (bq, LANES), 1)
        is_a = lane < HEAD_DIM
        q2 = q_ref[0] * jnp.asarray(HEAD_DIM ** -0.5, BF16)
        qs = (jnp.where(is_a, q2, jnp.zeros_like(q2)), jnp.where(is_a, jnp.zeros_like(q2), q2))
        m_ext = m_ref[...]
        row = lax.broadcasted_iota(jnp.int32, (bq, bq), 0)
        col = lax.broadcasted_iota(jnp.int32, (bq, bq), 1)
        diag_mask = col < row

        def block(kj, carry, mask):
            off = pl.multiple_of(kj * bq, bq)
            k2 = k_ref[0, pl.ds(off, bq), :]
            v2 = v_ref[0, pl.ds(off, bq), :]
            out = []
            for h in range(2):
                rem, acc = carry[h]
                z = _nt(qs[h], k2)
                sp, ls = _softplus_parts(z)
                lk = -sp if mask is None else jnp.where(mask, -sp, 0.0)
                sums = _keysums(lk, m_ext)
                parts = [None] * nsub
                for j in reversed(range(nsub)):
                    suf, total = sums[j]
                    parts[j] = jnp.exp(ls[:, j * KBLK:(j + 1) * KBLK] + suf + rem)
                    rem = rem + total
                a = jnp.concatenate(parts, axis=1)
                if mask is not None:
                    a = jnp.where(mask, a, 0.0)
                out.append((rem, acc + _nn(a.astype(BF16), v2)))
            return tuple(out)

        zero = jnp.zeros((bq, LANES), F32)
        carry = block(qi, ((zero, zero), (zero, zero)), diag_mask)
        carry = lax.fori_loop(0, qi, lambda it, c: block(qi - 1 - it, c, None), carry)
        o_ref[0] = jnp.where(is_a, carry[0][1], carry[1][1]).astype(BF16)
        tot_ref[...] = jnp.where(is_a, carry[0][0], carry[1][0])

    upp = qkv.shape[0] // 3
    return pl.pallas_call(
        body, name="attn_fwd", grid=(n_seq, n_pairs, nq),
        in_specs=[pl.BlockSpec((1, bq, LANES), lambda b, p, i: (p // 2, b * nq + i, p % 2)),
                  pl.BlockSpec((1, seq, LANES), lambda b, p, i: (upp + p // 2, b, p % 2)),
                  pl.BlockSpec((1, seq, LANES), lambda b, p, i: (2 * upp + p // 2, b, p % 2)),
                  pl.BlockSpec((2 * KBLK, 2 * KBLK), lambda b, p, i: (0, 0))],
        out_specs=[pl.BlockSpec((1, bq, LANES), lambda b, p, i: (p // 2, b * nq + i, p % 2)),
                   pl.BlockSpec((bq, LANES), lambda b, p, i: (b * nq + i, p))],
        out_shape=[jax.ShapeDtypeStruct((upp, t, 2 * LANES), BF16), jax.ShapeDtypeStruct((t, n_pairs * LANES), F32)],
        compiler_params=_cparams(("parallel", "parallel", "parallel")),
    )(qkv, qkv, qkv, suffix_m)


def attn_bwd(qkv, do, tot, n_seq, seq):
    t = qkv.shape[1]
    upp = qkv.shape[0] // 3
    n_pairs = upp * 2
    bq = min(ATT_BLOCK, seq)
    nq = seq // bq
    nsub = bq // KBLK
    _, prefix_m = _sum_matrices()
    scale = HEAD_DIM ** -0.5

    def body(q_ref, k_ref, v_ref, do_ref, tot_ref, m_ref, dq_ref, dk_ref, dv_ref, dk_acc, dv_acc):
        qi = pl.program_id(2)
        lane = lax.broadcasted_iota(jnp.int32, (bq, LANES), 1)
        is_a = lane < HEAD_DIM

        def halves(v2):
            z2 = jnp.zeros_like(v2)
            return jnp.where(is_a, v2, z2), jnp.where(is_a, z2, v2)

        qs = halves(q_ref[0] * jnp.asarray(scale, BF16))
        dos = halves(do_ref[0])
        tot2 = tot_ref[...]
        swapped = pltpu.roll(tot2, HEAD_DIM, 1)
        tots = (jnp.where(is_a, tot2, swapped), jnp.where(is_a, swapped, tot2))
        m_ext = m_ref[...]
        row = lax.broadcasted_iota(jnp.int32, (bq, bq), 0)
        col = lax.broadcasted_iota(jnp.int32, (bq, bq), 1)
        diag_mask = col < row

        @pl.when(qi == 0)
        def _():
            dk_acc[...] = jnp.zeros_like(dk_acc)
            dv_acc[...] = jnp.zeros_like(dv_acc)

        def block(kj, carry, mask):
            off = pl.multiple_of(kj * bq, bq)
            k2 = k_ref[0, pl.ds(off, bq), :]
            v2 = v_ref[0, pl.ds(off, bq), :]
            ks = halves(k2)
            dq = carry[2]
            dk_part = jnp.zeros((bq, LANES), F32)
            dv_part = jnp.zeros((bq, LANES), F32)
            out = []
            for h in range(2):
                pre, gpre = carry[h]
                z = _nt(qs[h], k2)
                sp, ls = _softplus_parts(z)
                lk = -sp if mask is None else jnp.where(mask, -sp, 0.0)
                sums = _keysums(lk, m_ext)
                parts = []
                for j in range(nsub):
                    pin, ptot = sums[j]
                    parts.append(jnp.exp(ls[:, j * KBLK:(j + 1) * KBLK] + (tots[h] - (pre + pin))))
                    pre = pre + ptot
                a = jnp.concatenate(parts, axis=1)
                if mask is not None:
                    a = jnp.where(mask, a, 0.0)
                g = a * _nt(dos[h], v2)
                gsums = _keysums(g, m_ext)
                parts = []
                for j in range(nsub):
                    gin, gtot = gsums[j]
                    parts.append(gpre + gin)
                    gpre = gpre + gtot
                dz = g - jnp.exp(ls) * jnp.concatenate(parts, axis=1)
                if mask is not None:
                    dz = jnp.where(mask, dz, 0.0)
                dzb = dz.astype(BF16)
                dq = dq + _nn(dzb, ks[h])
                dk_part = dk_part + _tn(dzb, qs[h])
                dv_part = dv_part + _tn(a.astype(BF16), dos[h])
                out.append((pre, gpre))
            dk_acc[pl.ds(off, bq), :] += dk_part
            dv_acc[pl.ds(off, bq), :] += dv_part
            return (out[0], out[1], dq)

        zero = jnp.zeros((bq, LANES), F32)
        carry = lax.fori_loop(0, qi, lambda kj, c: block(kj, c, None), ((zero, zero), (zero, zero), zero))
        carry = block(qi, carry, diag_mask)
        dq_ref[0] = (carry[2] * scale).astype(BF16)

        @pl.when(qi == nq - 1)
        def _():
            dk_ref[0] = dk_acc[...].astype(BF16)
            dv_ref[0] = dv_acc[...].astype(BF16)

    qblk = lambda b, p, i: (p // 2, b * nq + i, p % 2)
    kv_out = pl.BlockSpec((1, seq, LANES), lambda b, p, i: (p // 2, b, p % 2))
    shp = jax.ShapeDtypeStruct((upp, t, 2 * LANES), BF16)
    return pl.pallas_call(
        body, name="attn_bwd", grid=(n_seq, n_pairs, nq),
        in_specs=[pl.BlockSpec((1, bq, LANES), qblk),
                  pl.BlockSpec((1, seq, LANES), lambda b, p, i: (upp + p // 2, b, p % 2)),
                  pl.BlockSpec((1, seq, LANES), lambda b, p, i: (2 * upp + p // 2, b, p % 2)),
                  pl.BlockSpec((1, bq, LANES), qblk),
                  pl.BlockSpec((bq, LANES), lambda b, p, i: (b * nq + i, p)),
                  pl.BlockSpec((2 * KBLK, 2 * KBLK), lambda b, p, i: (0, 0))],
        out_specs=[pl.BlockSpec((1, bq, LANES), qblk), kv_out, kv_out],
        out_shape=[shp, shp, shp],
        scratch_shapes=[pltpu.VMEM((seq, LANES), F32), pltpu.VMEM((seq, LANES), F32)],
        compiler_params=_cparams(("parallel", "parallel", "arbitrary")),
    )(qkv, qkv, qkv, do, tot, prefix_m)


def _ln_stats(v):
    mu = jnp.mean(v, axis=-1, keepdims=True)
    vc = v - mu
    rstd = lax.rsqrt(jnp.mean(vc * vc, axis=-1, keepdims=True) + EPS)
    return vc * rstd, rstd


def _glu_into(a0_ref, av_ref, ag_ref, hv_ref, hg_ref, first):
    hv = hv_ref[0].astype(F32)
    hg = hg_ref[0].astype(F32)
    a0_ref[0:HALO, :] = jnp.where(first, 0.0, hv * _sigmoid(hg))
    av = av_ref[0].astype(F32)
    ag = ag_ref[0].astype(F32)
    a0_ref[HALO:, :] = av * _sigmoid(ag)


def _tril_mask():
    r = lax.broadcasted_iota(jnp.int32, (CHUNK, CHUNK), 0)
    c = lax.broadcasted_iota(jnp.int32, (CHUNK, CHUNK), 1)
    return c <= r


def mix_fwd(z, conv_w, conv_b, ln_a_g, ln_a_b, ln_v_g, ln_v_b, sp_w, sp_bt, seq):
    _, t, c = z.shape
    tm = _tile(seq, 512)
    tiles_per_seq = seq // tm
    groups = c // LANES
    hb = tm // HALO

    def body(av_ref, ag_ref, u_ref, v_ref, hv_ref, hg_ref, cw_ref, cb_ref, lag_ref, lab_ref, lvg_ref, lvb_ref,
             spw_ref, spb_ref, cat_ref, a1_ref, a0_ref):
        i = pl.program_id(0)
        _glu_into(a0_ref, av_ref, ag_ref, hv_ref, hg_ref, i % tiles_per_seq == 0)
        acc = jnp.zeros((tm, c), F32) + cb_ref[...]
        for k in range(CONV_WIDTH):
            acc = acc + cw_ref[k:k + 1, :] * a0_ref[pl.ds(HALO - (CONV_WIDTH - 1) + k, tm), :]
        a1_ref[...] = acc
        xh, _ = _ln_stats(acc)
        a2 = xh * lag_ref[...] + lab_ref[...]
        a3 = (a2 * _sigmoid(a2)).astype(BF16)
        half = c // 2
        cat_ref[0] = a3[:, :half]
        cat_ref[1] = a3[:, half:]
        tril = _tril_mask()
        for g in range(groups):
            sl = slice(g * LANES, (g + 1) * LANES)
            xh, _ = _ln_stats(v_ref[0][:, sl].astype(F32))
            vn = (xh * lvg_ref[:, sl] + lvb_ref[:, sl]).astype(BF16)
            w = jnp.where(tril, spw_ref[g], 0.0).astype(BF16)
            bias = spb_ref[:, g:g + 1]
            for ch in range(tm // CHUNK):
                rows = slice(ch * CHUNK, (ch + 1) * CHUNK)
                vs = _nn(w, vn[rows]) + bias
                bo = (u_ref[0][rows, sl].astype(F32) * vs).astype(BF16)
                cat_ref[2 + (g * LANES) // half, rows, (g * LANES) % half:(g * LANES) % half + LANES] = bo

    unit = lambda u: pl.BlockSpec((1, tm, c), lambda i: (u, i, 0))
    halo = lambda u: pl.BlockSpec((1, HALO, c), lambda i: (u, jnp.maximum(i * hb - 1, 0), 0))
    vec = pl.BlockSpec((1, c), lambda i: (0, 0))
    return pl.pallas_call(
        body, name="mix_fwd", grid=(t // tm,),
        in_specs=[unit(0), unit(1), unit(2), unit(3), halo(0), halo(1),
                  pl.BlockSpec((CONV_WIDTH, c), lambda i: (0, 0)), vec, vec, vec, vec, vec,
                  pl.BlockSpec((groups, CHUNK, CHUNK), lambda i: (0, 0, 0)),
                  pl.BlockSpec((CHUNK, groups), lambda i: (0, 0))],
        out_specs=[pl.BlockSpec((4, tm, c // 2), lambda i: (0, i, 0)), pl.BlockSpec((tm, c), lambda i: (i, 0))],
        out_shape=[jax.ShapeDtypeStruct((4, t, c // 2), BF16), jax.ShapeDtypeStruct((t, c), F32)],
        scratch_shapes=[pltpu.VMEM((HALO + tm, c), F32)],
        compiler_params=_cparams(("parallel",)),
    )(z, z, z, z, z, z, conv_w, conv_b, ln_a_g, ln_a_b, ln_v_g, ln_v_b, sp_w, sp_bt)


def mix_bwd_point(dcat, z, a1, ln_a_g, ln_a_b, ln_v_g, ln_v_b, sp_w, sp_wt, sp_bt, seq):
    _, t, c = z.shape
    tm = _tile(seq, 512)
    groups = c // LANES
    half = c // 2

    def body(dc_ref, u_ref, v_ref, a1_ref, lag_ref, lab_ref, lvg_ref, lvb_ref, spw_ref, spwt_ref, spb_ref,
             dz_ref, da1_ref, dcb_ref, dlag_ref, dlab_ref, dlvg_ref, dlvb_ref, dspw_ref, dspb_ref):
        i = pl.program_id(0)
        last = pl.num_programs(0) - 1

        @pl.when(i == 0)
        def _():
            for r in (dcb_ref, dlag_ref, dlab_ref, dlvg_ref, dlvb_ref, dspw_ref, dspb_ref):
                r[...] = jnp.zeros_like(r)

        da3 = jnp.concatenate([dc_ref[0], dc_ref[1]], axis=-1)
        xh, rstd = _ln_stats(a1_ref[...])
        a2 = xh * lag_ref[...] + lab_ref[...]
        s = _sigmoid(a2)
        da2 = da3 * (s * (1.0 + a2 * (1.0 - s)))
        dlag_ref[...] += jnp.sum(da2 * xh, axis=0, keepdims=True)
        dlab_ref[...] += jnp.sum(da2, axis=0, keepdims=True)
        dxh = da2 * lag_ref[...]
        da1 = rstd * (dxh - jnp.mean(dxh, axis=-1, keepdims=True) - xh * jnp.mean(dxh * xh, axis=-1, keepdims=True))
        da1_ref[...] = da1
        dcb_ref[...] += jnp.sum(da1, axis=0, keepdims=True)

        tril = _tril_mask()
        for g in range(groups):
            sl = slice(g * LANES, (g + 1) * LANES)
            xh, rstd = _ln_stats(v_ref[0][:, sl].astype(F32))
            lg = lvg_ref[:, sl]
            vnb = (xh * lg + lvb_ref[:, sl]).astype(BF16)
            w = jnp.where(tril, spw_ref[g], 0.0).astype(BF16)
            wt = jnp.where(tril.T, spwt_ref[g], 0.0).astype(BF16)
            bias = spb_ref[:, g:g + 1]
            dbo_all = dc_ref[2 + (g * LANES) // half][:, (g * LANES) % half:(g * LANES) % half + LANES]
            dvn_parts = []
            dw_acc = jnp.zeros((CHUNK, CHUNK), F32)
            db_acc = jnp.zeros((CHUNK, LANES), F32)
            for ch in range(tm // CHUNK):
                rows = slice(ch * CHUNK, (ch + 1) * CHUNK)
                vs = _nn(w, vnb[rows]) + bias
                dbo = dbo_all[rows]
                uv = u_ref[0][rows, sl].astype(F32)
                dz_ref[0, rows, sl] = (dbo * vs).astype(BF16)
                dvs = dbo * uv
                dvsb = dvs.astype(BF16)
                dvn_parts.append(_nn(wt, dvsb))
                dw_acc = dw_acc + _nt(dvsb, vnb[rows])
                db_acc = db_acc + dvs
            dvn = jnp.concatenate(dvn_parts, axis=0)
            dspw_ref[g] += jnp.where(tril, dw_acc, 0.0)
            dspb_ref[g] += db_acc
            dlvg_ref[:, sl] += jnp.sum(dvn * xh, axis=0, keepdims=True)
            dlvb_ref[:, sl] += jnp.sum(dvn, axis=0, keepdims=True)
            dxh = dvn * lg
            dv = rstd * (dxh - jnp.mean(dxh, axis=-1, keepdims=True) - xh * jnp.mean(dxh * xh, axis=-1, keepdims=True))
            dz_ref[1, :, sl] = dv.astype(BF16)

        @pl.when(i == last)
        def _():
            for g in range(groups):
                dspb_ref[g] = jnp.zeros((CHUNK, LANES), F32) + jnp.sum(dspb_ref[g], axis=-1, keepdims=True)

    unit = lambda u: pl.BlockSpec((1, tm, c), lambda i: (u, i, 0))
    vec = pl.BlockSpec((1, c), lambda i: (0, 0))
    sq = pl.BlockSpec((groups, CHUNK, CHUNK), lambda i: (0, 0, 0))
    vshape = jax.ShapeDtypeStruct((1, c), F32)
    sshape = jax.ShapeDtypeStruct((groups, CHUNK, CHUNK), F32)
    return pl.pallas_call(
        body, name="mix_bwd_point", grid=(t // tm,),
        in_specs=[pl.BlockSpec((4, tm, half), lambda i: (0, i, 0)), unit(2), unit(3),
                  pl.BlockSpec((tm, c), lambda i: (i, 0)), vec, vec, vec, vec, sq, sq,
                  pl.BlockSpec((CHUNK, groups), lambda i: (0, 0))],
        out_specs=[pl.BlockSpec((2, tm, c), lambda i: (1, i, 0)), pl.BlockSpec((tm, c), lambda i: (i, 0)),
                   vec, vec, vec, vec, vec, sq, sq],
        out_shape=[jax.ShapeDtypeStruct((4, t, c), BF16), jax.ShapeDtypeStruct((t, c), F32),
                   vshape, vshape, vshape, vshape, vshape, sshape, sshape],
        compiler_params=_cparams(("arbitrary",)),
    )(dcat, z, z, a1, ln_a_g, ln_a_b, ln_v_g, ln_v_b, sp_w, sp_wt, sp_bt)


def mix_bwd_conv(dz, da1, z, conv_w, seq):
    _, t, c = z.shape
    tm = _tile(seq, 512)
    tiles_per_seq = seq // tm
    hb = tm // HALO
    n_halo_blocks = t // HALO

    def body(dz_in_ref, d_ref, dh_ref, av_ref, ag_ref, hv_ref, hg_ref, cw_ref, dz_ref, dcw_ref, a0_ref, d1_ref):
        del dz_in_ref
        i = pl.program_id(0)
        _glu_into(a0_ref, av_ref, ag_ref, hv_ref, hg_ref, i % tiles_per_seq == 0)
        d1_ref[0:tm, :] = d_ref[...]
        d1_ref[tm:, :] = jnp.where((i + 1) % tiles_per_seq == 0, 0.0, dh_ref[...])

        @pl.when(i == 0)
        def _():
            dcw_ref[...] = jnp.zeros_like(dcw_ref)

        d1 = d_ref[...]
        da0 = jnp.zeros((tm, c), F32)
        for k in range(CONV_WIDTH):
            back = CONV_WIDTH - 1 - k
            da0 = da0 + cw_ref[k:k + 1, :] * d1_ref[pl.ds(back, tm), :]
            dcw_ref[k:k + 1, :] += jnp.sum(d1 * a0_ref[pl.ds(HALO - back, tm), :], axis=0, keepdims=True)
        av = av_ref[0].astype(F32)
        s = _sigmoid(ag_ref[0].astype(F32))
        dz_ref[0] = (da0 * s).astype(BF16)
        dz_ref[1] = (da0 * av * s * (1.0 - s)).astype(BF16)

    unit = lambda u: pl.BlockSpec((1, tm, c), lambda i: (u, i, 0))
    halo = lambda u: pl.BlockSpec((1, HALO, c), lambda i: (u, jnp.maximum(i * hb - 1, 0), 0))
    return pl.pallas_call(
        body, name="mix_bwd_conv", grid=(t // tm,),
        in_specs=[pl.BlockSpec(memory_space=pl.ANY), pl.BlockSpec((tm, c), lambda i: (i, 0)),
                  pl.BlockSpec((HALO, c), lambda i: (jnp.minimum((i + 1) * hb, n_halo_blocks - 1), 0)),
                  unit(0), unit(1), halo(0), halo(1), pl.BlockSpec((CONV_WIDTH, c), lambda i: (0, 0))],
        out_specs=[pl.BlockSpec((2, tm, c), lambda i: (0, i, 0)), pl.BlockSpec((CONV_WIDTH, c), lambda i: (0, 0))],
        out_shape=[jax.ShapeDtypeStruct(dz.shape, BF16), jax.ShapeDtypeStruct((CONV_WIDTH, c), F32)],
        scratch_shapes=[pltpu.VMEM((HALO + tm, c), F32), pltpu.VMEM((tm + HALO, c), F32)],
        input_output_aliases={0: 0},
        compiler_params=_cparams(("arbitrary",)),
    )(dz, da1, da1, z, z, z, z, conv_w)


CHIP_FLIPS = ((1, 0), (0, 1), (1, 1))
ANY = pl.BlockSpec(memory_space=pl.ANY)


def _place():
    return lax.axis_index("x"), lax.axis_index("y"), lax.axis_index("c")


def _flip(v, f):
    return 1 - v if f else v


def place_shard(w, chip, dtype, name):
    r, cc = w.shape
    rb = _tile(r, 512)

    def body(chip_ref, w_ref, o_ref):
        del chip_ref
        o_ref[0] = w_ref[...].astype(dtype)

    return pl.pallas_call(
        body, name=name,
        grid_spec=pltpu.PrefetchScalarGridSpec(
            num_scalar_prefetch=1, grid=(r // rb,),
            in_specs=[pl.BlockSpec((rb, cc), lambda i, chip_ref: (i, 0))],
            out_specs=pl.BlockSpec((1, rb, cc), lambda i, chip_ref: (chip_ref[0], i, 0))),
        out_shape=jax.ShapeDtypeStruct((N_CHIPS, r, cc), dtype),
        compiler_params=_cparams(("parallel",)),
    )(chip, w)


def allgather_weights(shards, smalls):
    n, ns = len(shards), len(smalls)

    def body(*refs):
        ins, sins = refs[:n], refs[n:n + ns]
        outs, souts = refs[n + ns:2 * n + ns], refs[2 * n + ns:2 * n + 2 * ns]
        ici_send, ici_recv, d2d_send, d2d_recv, sm_send, sm_recv = refs[2 * n + 2 * ns:]
        x, y, c = _place()
        k = 2 * x + y
        sibling = (x, y, 1 - c)
        pending = []

        def half(a):
            hr = shards[a].shape[1] // 2
            return pl.ds(pl.multiple_of(c * hr, 16), hr)

        for a in range(n):
            for o, (fx, fy) in enumerate(CHIP_FLIPS):
                cp = pltpu.make_async_remote_copy(
                    src_ref=ins[a].at[k, half(a)], dst_ref=outs[a].at[k, half(a)],
                    send_sem=ici_send.at[3 * a + o], recv_sem=ici_recv.at[3 * a + o],
                    device_id=(_flip(x, fx), _flip(y, fy), c), device_id_type=MESH)
                cp.start()
                pending.append(cp.wait_send)
        for a in range(ns):
            for o, (fx, fy) in enumerate(CHIP_FLIPS):
                cp = pltpu.make_async_remote_copy(
                    src_ref=sins[a].at[k], dst_ref=souts[a].at[k],
                    send_sem=sm_send.at[3 * a + o], recv_sem=sm_recv.at[3 * a + o],
                    device_id=(_flip(x, fx), _flip(y, fy), c), device_id_type=MESH)
                cp.start()
                pending.append(cp.wait_send)
        for a in range(n):
            for o, (fx, fy) in enumerate(CHIP_FLIPS):
                kk = 2 * _flip(x, fx) + _flip(y, fy)
                landed = outs[a].at[kk, half(a)]
                pltpu.make_async_remote_copy(
                    src_ref=landed, dst_ref=landed, send_sem=ici_send.at[3 * a + o], recv_sem=ici_recv.at[3 * a + o],
                    device_id=sibling, device_id_type=MESH).wait_recv()
                cp = pltpu.make_async_remote_copy(
                    src_ref=landed, dst_ref=landed, send_sem=d2d_send.at[3 * a + o], recv_sem=d2d_recv.at[3 * a + o],
                    device_id=sibling, device_id_type=MESH)
                cp.start()
                pending.append(cp.wait_send)
        for a in range(n):
            hr = shards[a].shape[1] // 2
            other = pl.ds(pl.multiple_of((1 - c) * hr, 16), hr)
            for o, (fx, fy) in enumerate(CHIP_FLIPS):
                kk = 2 * _flip(x, fx) + _flip(y, fy)
                got = outs[a].at[kk, other]
                pltpu.make_async_remote_copy(
                    src_ref=got, dst_ref=got, send_sem=d2d_send.at[3 * a + o], recv_sem=d2d_recv.at[3 * a + o],
                    device_id=sibling, device_id_type=MESH).wait_recv()
        for a in range(ns):
            for o, (fx, fy) in enumerate(CHIP_FLIPS):
                kk = 2 * _flip(x, fx) + _flip(y, fy)
                got = souts[a].at[kk]
                pltpu.make_async_remote_copy(
                    src_ref=got, dst_ref=got, send_sem=sm_send.at[3 * a + o], recv_sem=sm_recv.at[3 * a + o],
                    device_id=sibling, device_id_type=MESH).wait_recv()
        for w in pending:
            w()

    out_shape = [jax.ShapeDtypeStruct(s.shape, s.dtype) for s in list(shards) + list(smalls)]
    dma = pltpu.SemaphoreType.DMA
    res = pl.pallas_call(
        body, name="allgather_weights", in_specs=[ANY] * (n + ns), out_specs=[ANY] * (n + ns), out_shape=out_shape,
        scratch_shapes=[dma((3 * n,)), dma((3 * n,)), dma((3 * n,)), dma((3 * n,)), dma((3 * ns,)), dma((3 * ns,))],
        input_output_aliases={i: i for i in range(n + ns)},
        compiler_params=pltpu.CompilerParams(has_side_effects=True),
    )(*shards, *smalls)
    return res[:n], res[n:]


def rs_exchange(grads):
    n = len(grads)

    def body(*refs):
        ins, outs = refs[:n], refs[n:2 * n]
        send, recv = refs[2 * n:]
        x, y, c = _place()
        cps = []
        for a in range(n):
            cp = pltpu.make_async_remote_copy(
                src_ref=ins[a].at[:, 1 - c], dst_ref=outs[a], send_sem=send.at[a], recv_sem=recv.at[a],
                device_id=(x, y, 1 - c), device_id_type=MESH)
            cp.start()
            cps.append(cp)
        for cp in cps:
            cp.wait()

    dma = pltpu.SemaphoreType.DMA
    return pl.pallas_call(
        body, name="rs_exchange", in_specs=[ANY] * n, out_specs=[ANY] * n,
        out_shape=[jax.ShapeDtypeStruct((g.shape[0],) + g.shape[2:], g.dtype) for g in grads],
        scratch_shapes=[dma((n,)), dma((n,))],
        compiler_params=pltpu.CompilerParams(has_side_effects=True),
    )(*grads)


def rs_add(g, sib, core, out_dtype, name):
    nk, _, hr, cc = g.shape
    rb = _tile(hr, 256)

    def body(core_ref, g_ref, s_ref, o_ref):
        del core_ref
        o_ref[0] = (g_ref[0, 0] + s_ref[0]).astype(out_dtype)

    return pl.pallas_call(
        body, name=name,
        grid_spec=pltpu.PrefetchScalarGridSpec(
            num_scalar_prefetch=1, grid=(nk, hr // rb),
            in_specs=[pl.BlockSpec((1, 1, rb, cc), lambda k, i, core_ref: (k, core_ref[0], i, 0)),
                      pl.BlockSpec((1, rb, cc), lambda k, i, core_ref: (k, i, 0))],
            out_specs=pl.BlockSpec((1, rb, cc), lambda k, i, core_ref: (k, i, 0))),
        out_shape=jax.ShapeDtypeStruct((nk, hr, cc), out_dtype),
        compiler_params=_cparams(("parallel", "parallel")),
    )(core, g, sib)


def rs_send(parts):
    n = len(parts)

    def body(*refs):
        ins, outs = refs[:n], refs[n:2 * n]
        send, recv = refs[2 * n:]
        x, y, c = _place()
        waits = []
        for a in range(n):
            for o, (fx, fy) in enumerate(CHIP_FLIPS):
                kk = 2 * _flip(x, fx) + _flip(y, fy)
                cp = pltpu.make_async_remote_copy(
                    src_ref=ins[a].at[kk], dst_ref=outs[a].at[o], send_sem=send.at[3 * a + o], recv_sem=recv.at[3 * a + o],
                    device_id=(_flip(x, fx), _flip(y, fy), c), device_id_type=MESH)
                cp.start()
                waits.append(cp.wait)
        for w in waits:
            w()

    dma = pltpu.SemaphoreType.DMA
    return pl.pallas_call(
        body, name="rs_send", in_specs=[ANY] * n, out_specs=[ANY] * n,
        out_shape=[jax.ShapeDtypeStruct((3,) + p.shape[1:], p.dtype) for p in parts],
        scratch_shapes=[dma((3 * n,)), dma((3 * n,))],
        compiler_params=pltpu.CompilerParams(has_side_effects=True),
    )(*parts)


def rs_sum(recv, part, where, full, layer, n_layers, name):
    _, hr, cc = recv.shape
    rb = _tile(hr, 256)

    def body(*refs):
        r_ref, p_ref, o_ref = refs[1], refs[2], refs[-1]
        o_ref[0, 0] = ((p_ref[0].astype(F32) + r_ref[0].astype(F32)) + r_ref[1].astype(F32)) + r_ref[2].astype(F32)

    in_specs = [pl.BlockSpec((3, rb, cc), lambda i, w_ref: (0, i, 0)),
                pl.BlockSpec((1, rb, cc), lambda i, w_ref: (w_ref[0], i, 0))]
    args = [where, recv, part]
    aliases = {}
    if full is not None:
        in_specs.append(ANY)
        args.append(full)
        aliases = {3: 0}
    return pl.pallas_call(
        body, name=name,
        grid_spec=pltpu.PrefetchScalarGridSpec(
            num_scalar_prefetch=1, grid=(hr // rb,), in_specs=in_specs,
            out_specs=pl.BlockSpec((1, 1, rb, cc), lambda i, w_ref: (layer, w_ref[1], i, 0))),
        out_shape=jax.ShapeDtypeStruct((n_layers, 2, hr, cc), F32),
        input_output_aliases=aliases,
        compiler_params=_cparams(("parallel",)),
    )(*args)


def rs_share(fulls):
    n = len(fulls)

    def body(*refs):
        ins, outs = refs[:n], refs[n:2 * n]
        send, recv = refs[2 * n:]
        x, y, c = _place()
        cps = []
        for a in range(n):
            cp = pltpu.make_async_remote_copy(
                src_ref=ins[a].at[:, c], dst_ref=outs[a].at[:, c], send_sem=send.at[a], recv_sem=recv.at[a],
                device_id=(x, y, 1 - c), device_id_type=MESH)
            cp.start()
            cps.append(cp)
        for a in range(n):
            got = outs[a].at[:, 1 - c]
            pltpu.make_async_remote_copy(
                src_ref=got, dst_ref=got, send_sem=send.at[a], recv_sem=recv.at[a],
                device_id=(x, y, 1 - c), device_id_type=MESH).wait_recv()
        for cp in cps:
            cp.wait_send()

    dma = pltpu.SemaphoreType.DMA
    return pl.pallas_call(
        body, name="rs_share", in_specs=[ANY] * n, out_specs=[ANY] * n,
        out_shape=[jax.ShapeDtypeStruct(f.shape, f.dtype) for f in fulls],
        scratch_shapes=[dma((n,)), dma((n,))],
        input_output_aliases={i: i for i in range(n)},
        compiler_params=pltpu.CompilerParams(has_side_effects=True),
    )(*fulls)


def allreduce_small(v):
    r, w = v.shape

    def body(v_ref, o_ref, buf, send, recv, loc):
        x, y, c = _place()
        me = 4 * x + 2 * y + c
        mine = pltpu.make_async_copy(v_ref, buf.at[me], loc)
        mine.start()
        cps = []
        for o in range(1, N_DEV):
            fx, fy, fc = (o >> 2) & 1, (o >> 1) & 1, o & 1
            cp = pltpu.make_async_remote_copy(
                src_ref=v_ref, dst_ref=buf.at[me], send_sem=send.at[o - 1], recv_sem=recv.at[o - 1],
                device_id=(_flip(x, fx), _flip(y, fy), _flip(c, fc)), device_id_type=MESH)
            cp.start()
            cps.append(cp)
        for o in range(1, N_DEV):
            fx, fy, fc = (o >> 2) & 1, (o >> 1) & 1, o & 1
            peer = 4 * _flip(x, fx) + 2 * _flip(y, fy) + _flip(c, fc)
            pltpu.make_async_remote_copy(
                src_ref=v_ref, dst_ref=buf.at[peer], send_sem=send.at[o - 1], recv_sem=recv.at[o - 1],
                device_id=(x, y, c), device_id_type=MESH).wait_recv()
        for cp in cps:
            cp.wait_send()
        mine.wait()
        acc = buf[0]
        for d in range(1, N_DEV):
            acc = acc + buf[d]
        o_ref[...] = acc

    dma = pltpu.SemaphoreType.DMA
    vm = pl.BlockSpec(memory_space=pltpu.VMEM)
    return pl.pallas_call(
        body, name="allreduce_small", in_specs=[vm], out_specs=vm,
        out_shape=jax.ShapeDtypeStruct((r, w), F32),
        scratch_shapes=[pltpu.VMEM((N_DEV, r, w), F32), dma((N_DEV - 1,)), dma((N_DEV - 1,)), dma],
        compiler_params=pltpu.CompilerParams(has_side_effects=True, vmem_limit_bytes=VMEM_LIMIT),
    )(v)


def adamw(w, g, m, v, name):
    r, cc = w.shape
    rb = _tile(r, 256)

    def body(w_ref, g_ref, m_ref, v_ref, d_ref, nm_ref, nv_ref):
        gv = g_ref[...]
        nm = ADAM_B1 * m_ref[...] + (1.0 - ADAM_B1) * gv
        nv = ADAM_B2 * v_ref[...] + (1.0 - ADAM_B2) * (gv * gv)
        m_hat = nm / (1.0 - ADAM_B1 ** ADAM_STEP)
        v_hat = nv / (1.0 - ADAM_B2 ** ADAM_STEP)
        d_ref[...] = -ADAM_LR * (m_hat / (jnp.sqrt(v_hat) + ADAM_EPS) + ADAM_WD * w_ref[...])
        nm_ref[...] = nm
        nv_ref[...] = nv

    blk = pl.BlockSpec((rb, cc), lambda i: (i, 0))
    shp = jax.ShapeDtypeStruct((r, cc), F32)
    return pl.pallas_call(
        body, name=name, grid=(r // rb,), in_specs=[blk] * 4, out_specs=[blk] * 3, out_shape=[shp] * 3,
        compiler_params=_cparams(("parallel",)),
    )(w, g, m, v)


WEIGHTS = ['g_ffn1', 'w_ffn1_gate', 'w_ffn1_up', 'w_ffn1_down', 'g_mix', 'w_in_ab', 'conv_w', 'conv_b', 'ln_a_g',
           'ln_a_b', 'ln_v_g', 'ln_v_b', 'sp_w', 'sp_b', 'w_out_ab', 'w_qkv', 'w_o', 'g_ffn2', 'w_ffn2_gate',
           'w_ffn2_up', 'w_ffn2_down', 'g_final']
BIG = ['w_ffn1_gate', 'w_ffn1_up', 'w_ffn1_down', 'w_in_ab', 'w_out_ab', 'w_qkv', 'w_o', 'w_ffn2_gate', 'w_ffn2_up',
       'w_ffn2_down']
SMALL = ['g_ffn1', 'g_mix', 'g_ffn2', 'g_final', 'conv_b', 'ln_a_g', 'ln_a_b', 'ln_v_g', 'ln_v_b', 'sp_b', 'sp_w']


def _rows(a):
    return a.reshape(-1, LANES)


def _pack(parts):
    v = jnp.concatenate([_rows(p) for p in parts], axis=0)
    pad = (-v.shape[0]) % 8
    return jnp.pad(v, ((0, pad), (0, 0)))


def _unpack(v, shapes):
    out, r = [], 0
    for s in shapes:
        n = 1
        for d in s:
            n *= d
        n //= LANES
        out.append(v[r:r + n].reshape(s))
        r += n
    return out


def kernel(x, g_ffn1, w_ffn1_gate, w_ffn1_up, w_ffn1_down, g_mix, w_in_ab, conv_w, conv_b, ln_a_g, ln_a_b, ln_v_g, ln_v_b, sp_w, sp_b, w_out_ab, w_qkv, w_o, g_ffn2, w_ffn2_gate, w_ffn2_up, w_ffn2_down, g_final, loss_target, m_g_ffn1, m_w_ffn1_gate, m_w_ffn1_up, m_w_ffn1_down, m_g_mix, m_w_in_ab, m_conv_w, m_conv_b, m_ln_a_g, m_ln_a_b, m_ln_v_g, m_ln_v_b, m_sp_w, m_sp_b, m_w_out_ab, m_w_qkv, m_w_o, m_g_ffn2, m_w_ffn2_gate, m_w_ffn2_up, m_w_ffn2_down, m_g_final, v_g_ffn1, v_w_ffn1_gate, v_w_ffn1_up, v_w_ffn1_down, v_g_mix, v_w_in_ab, v_conv_w, v_conv_b, v_ln_a_g, v_ln_a_b, v_ln_v_g, v_ln_v_b, v_sp_w, v_sp_b, v_w_out_ab, v_w_qkv, v_w_o, v_g_ffn2, v_w_ffn2_gate, v_w_ffn2_up, v_w_ffn2_down, v_g_final):
    p = dict(locals())
    n_seq, seq, d = x.shape
    t = n_seq * seq
    depth = g_ffn1.shape[0]
    core = lax.axis_index("c")
    chip = 2 * lax.axis_index("x") + lax.axis_index("y")
    xf = x.reshape(t, d)
    target = loss_target.reshape(t, d)

    items = []
    for name in BIG:
        for layer in range(p[name].shape[0]):
            items.append((name, layer))
    chip1 = chip.reshape(1).astype(jnp.int32)
    shards = [place_shard(p[name][layer], chip1, BF16, "place_shard") for name, layer in items]
    gathered, (conv_w4,) = allgather_weights(shards, [place_shard(conv_w[0], chip1, F32, "place_conv_w")])
    wt = {it: g for it, g in zip(items, gathered)}
    c_mix = conv_w4.shape[2] * N_CHIPS
    conv_full = jnp.transpose(conv_w4, (1, 0, 2)).reshape(CONV_WIDTH, c_mix)
    vec = lambda a: a.reshape(1, -1)
    sp_bt = sp_b[0].T
    sp_wt = jnp.transpose(sp_w[0], (0, 2, 1))
    d_ff = w_ffn1_gate.shape[2]
    n_in = w_in_ab.shape[2]
    n_qkv = w_qkv.shape[2] // 3

    saved = []
    xc = xf
    for layer in range(depth):
        s = {}
        for half, (gn, wn) in enumerate((('g_ffn1', 'w_ffn1'), ('g_ffn2', 'w_ffn2'))):
            if half == 1:
                s['x_mix'] = xc
                s['h_mix'] = rmsnorm_fwd(xc, vec(g_mix[layer]), "norm_mix")
                if layer % 2 == 0:
                    (z,) = colmm(s['h_mix'], [wt[('w_in_ab', layer // 2)]], n_in, BF16, "mm_in")
                    cat, a1 = mix_fwd(z, conv_full, conv_b, ln_a_g, ln_a_b, vec(ln_v_g), vec(ln_v_b), sp_w[0], sp_bt, seq)
                    s.update(z=z, cat=cat, a1=a1)
                    xc = rowmm([cat], wt[('w_out_ab', layer // 2)], xc, 1.0, "mm_out")
                else:
                    (qkv,) = colmm(s['h_mix'], [wt[('w_qkv', layer // 2)]], n_qkv, BF16, "mm_qkv")
                    o, tot = attn_fwd(qkv, n_seq, seq)
                    s.update(qkv=qkv, o=o, tot=tot)
                    xc = rowmm([o], wt[('w_o', layer // 2)], xc, 1.0, "mm_o")
            s['x' + wn] = xc
            h = rmsnorm_fwd(xc, vec(p[gn][layer]), "norm_ffn")
            gate, up = colmm(h, [wt[(wn + '_gate', layer)], wt[(wn + '_up', layer)]], d_ff, BF16, "ffn_gateup")
            xc = rowmm([gate, up], wt[(wn + '_down', layer)], xc, 0.5, "ffn_down")
            s.update({'h' + wn: h, 'gate' + wn: gate, 'up' + wn: up})
        saved.append(s)

    loss8, dx, dxb, dg_final = loss_head(xc, vec(g_final), target)
    loss = lax.psum(loss8[0, 0], ("x", "y", "c"))

    gw = {}
    gs = {}
    for layer in reversed(range(depth)):
        s = saved[layer]
        for half, (gn, wn) in reversed(list(enumerate((('g_ffn1', 'w_ffn1'), ('g_ffn2', 'w_ffn2'))))):
            wd = wt[(wn + '_down', layer)]
            dgate, dup, act = rowmm_t(dxb, wd, 0.5, BF16, "ffn_bwd_act", gu=(s['gate' + wn], s['up' + wn]))
            gw[(wn + '_down', layer)] = dw_row(act, dxb, 0.5, "ffn_dw_down")
            gw[(wn + '_gate', layer)], gw[(wn + '_up', layer)] = dw_col(s['h' + wn], [dgate, dup], N_CHIPS, d_ff, "ffn_dw_gateup")
            dx, dxb, dg = colmm_t([dgate, dup], [wt[(wn + '_gate', layer)], wt[(wn + '_up', layer)]], d_ff,
                                  s['x' + wn], vec(p[gn][layer]), dx, "ffn_bwd_in")
            gs[(gn, layer)] = dg
            if half == 1:
                if layer % 2 == 0:
                    i = layer // 2
                    w_out = wt[('w_out_ab', i)]
                    dcat = rowmm_t(dxb, w_out, 1.0, F32, "mm_out_t")
                    gw[('w_out_ab', i)] = dw_row(s['cat'], dxb, 1.0, "dw_out")
                    dz, da1, dcb, dlag, dlab, dlvg, dlvb, dspw, dspb = mix_bwd_point(
                        dcat, s['z'], s['a1'], ln_a_g, ln_a_b, vec(ln_v_g), vec(ln_v_b), sp_w[0], sp_wt, sp_bt, seq)
                    dz, dcw = mix_bwd_conv(dz, da1, s['z'], conv_full, seq)
                    gs.update({('conv_b', i): dcb, ('ln_a_g', i): dlag, ('ln_a_b', i): dlab, ('ln_v_g', i): dlvg,
                               ('ln_v_b', i): dlvb, ('sp_w', i): dspw, ('sp_b', i): dspb[:, :, 0], ('conv_w', i): dcw})
                    (gw[('w_in_ab', i)],) = dw_col(s['h_mix'], [dz], N_CHIPS, n_in, "dw_in")
                    dx, dxb, dg = colmm_t([dz], [wt[('w_in_ab', i)]], n_in, s['x_mix'], vec(g_mix[layer]), dx, "mm_in_t")
                else:
                    i = layer // 2
                    w_o4 = wt[('w_o', i)]
                    do = rowmm_t(dxb, w_o4, 1.0, BF16, "mm_o_t")
                    gw[('w_o', i)] = dw_row(s['o'], dxb, 1.0, "dw_o")
                    dq, dk, dv = attn_bwd(s['qkv'], do, s['tot'], n_seq, seq)
                    dqkv = jnp.concatenate([dq, dk, dv], axis=0)
                    (gw[('w_qkv', i)],) = dw_col(s['h_mix'], [dqkv], N_CHIPS, n_qkv, "dw_qkv")
                    dx, dxb, dg = colmm_t([dqkv], [wt[('w_qkv', i)]], n_qkv, s['x_mix'], vec(g_mix[layer]), dx, "mm_qkv_t")
                gs[('g_mix', layer)] = dg
    grad_x = dx.reshape(x.shape)

    core1 = core.reshape(1).astype(jnp.int32)
    g4 = [gw[it].reshape(N_CHIPS, 2, gw[it].shape[1] // 2, gw[it].shape[2]) for it in items]
    sib = rs_exchange(g4)
    parts = [rs_add(g, sb, core1, REDUCE_DTYPE, "rs_add") for g, sb in zip(g4, sib)]
    recv = dict(zip(items, rs_send(parts)))
    part = dict(zip(items, parts))
    where = jnp.stack([chip, core]).astype(jnp.int32)
    fulls = []
    for name in BIG:
        full = None
        n_layers = p[name].shape[0]
        for layer in range(n_layers):
            full = rs_sum(recv[(name, layer)], part[(name, layer)], where, full, layer, n_layers, "rs_sum")
        fulls.append(full)
    shared = rs_share(fulls)
    grads = {name: sh.reshape(p[name].shape) for name, sh in zip(BIG, shared)}

    stack = lambda name: jnp.concatenate([gs[(name, layer)].reshape((1,) + p[name].shape[1:]) for layer in range(p[name].shape[0])], axis=0)
    small_g = [stack(name) if name != 'g_final' else dg_final.reshape(p[name].shape) for name in SMALL]
    packed = _pack(small_g + [gs[('conv_w', 0)]])
    red = allreduce_small(packed)
    outs = _unpack(red, [p[name].shape for name in SMALL] + [(CONV_WIDTH, c_mix)])
    for name, g in zip(SMALL, outs[:-1]):
        grads[name] = g
    conv_g = outs[-1].reshape(CONV_WIDTH, N_CHIPS, c_mix // N_CHIPS)
    grads['conv_w'] = lax.dynamic_index_in_dim(conv_g, chip, axis=1, keepdims=False).reshape(conv_w.shape)

    delta, new_m, new_v = {}, {}, {}
    for name in BIG:
        shp = p[name].shape
        two = lambda a: a.reshape(shp[0] * shp[1], shp[2])
        dl, nm, nv = adamw(two(p[name]), two(grads[name]), two(p['m_' + name]), two(p['v_' + name]), "adamw")
        delta[name], new_m[name], new_v[name] = dl.reshape(shp), nm.reshape(shp), nv.reshape(shp)
    small_names = SMALL + ['conv_w']
    pk = lambda pre: _pack([p[pre + name] for name in small_names])
    dl, nm, nv = adamw(pk(''), _pack([grads[name] for name in small_names]), pk('m_'), pk('v_'), "adamw_small")
    shapes = [p[name].shape for name in small_names]
    for dst, val in ((delta, dl), (new_m, nm), (new_v, nv)):
        for name, a in zip(small_names, _unpack(val, shapes)):
            dst[name] = a

    return (loss, grad_x, *[grads[n] for n in WEIGHTS], *[delta[n] for n in WEIGHTS],
            *[new_m[n] for n in WEIGHTS], *[new_v[n] for n in WEIGHTS])
```

```python
import functools

import jax
import jax.numpy as jnp
from jax import lax
from jax.experimental import pallas as pl
from jax.experimental.pallas import tpu as pltpu

F32 = jnp.float32
BF16 = jnp.bfloat16
EPS = 1e-6
HEAD_DIM = 64
CONV_WIDTH = 31
CHUNK = 128
KBLK = 128
ATT_BLOCK = 512
STICK_GONE = -110.0
LANES = 128
HALO = 32
ADAM_LR, ADAM_B1, ADAM_B2, ADAM_EPS, ADAM_WD, ADAM_STEP = 0.001, 0.9, 0.999, 1e-08, 0.01, 10
VMEM_LIMIT = 56 * 1024 * 1024
MESH = pl.DeviceIdType.MESH
N_CHIPS = 4
N_DEV = 8
REDUCE_DTYPE = BF16


def _cparams(sem):
    return pltpu.CompilerParams(dimension_semantics=sem, vmem_limit_bytes=VMEM_LIMIT)


def _nt(a, b):
    return lax.dot_general(a, b, (((1,), (1,)), ((), ())), preferred_element_type=F32)


def _tn(a, b):
    return lax.dot_general(a, b, (((0,), (0,)), ((), ())), preferred_element_type=F32)


def _nn(a, b):
    return jnp.dot(a, b, preferred_element_type=F32)


def _sigmoid(x):
    return 1.0 / (1.0 + jnp.exp(-x))


def _tile(t, want):
    if t <= want:
        return t
    for cand in range(want - want % 8, 7, -8):
        if t % cand == 0:
            return cand
    raise ValueError((t, want))


def rmsnorm_fwd(x, g, name):
    t, d = x.shape
    tm = _tile(t, 512)

    def body(x_ref, g_ref, h_ref):
        xv = x_ref[...]
        r = lax.rsqrt(jnp.mean(xv * xv, axis=-1, keepdims=True) + EPS)
        h_ref[...] = (xv * r * g_ref[...]).astype(BF16)

    return pl.pallas_call(
        body, name=name, grid=(t // tm,),
        in_specs=[pl.BlockSpec((tm, d), lambda i: (i, 0)), pl.BlockSpec((1, d), lambda i: (0, 0))],
        out_specs=pl.BlockSpec((tm, d), lambda i: (i, 0)),
        out_shape=jax.ShapeDtypeStruct((t, d), BF16),
        compiler_params=_cparams(("parallel",)),
    )(x, g)


def colmm(h, ws, nu, out_dtype, name):
    t, k = h.shape
    j, _, nj = ws[0].shape
    per = nj // nu
    units = j * per
    tm = _tile(t, 512)
    nw = len(ws)

    def body(*refs):
        h_ref = refs[0]
        hv = h_ref[...]
        for n in range(nw):
            refs[1 + nw + n][0] = _nn(hv, refs[1 + n][0]).astype(out_dtype)

    w_spec = pl.BlockSpec((1, k, nu), lambda u, i: (u // per, 0, u % per))
    o_spec = pl.BlockSpec((1, tm, nu), lambda u, i: (u, i, 0))
    outs = pl.pallas_call(
        body, name=name, grid=(units, t // tm),
        in_specs=[pl.BlockSpec((tm, k), lambda u, i: (i, 0))] + [w_spec] * nw,
        out_specs=[o_spec] * nw,
        out_shape=[jax.ShapeDtypeStruct((units, t, nu), out_dtype)] * nw,
        compiler_params=_cparams(("parallel", "parallel")),
    )(h, *ws)
    return outs


def rowmm(a_list, w, resid, scale, name):
    swiglu = len(a_list) == 2
    u_n, t, ku = a_list[0].shape
    n = w.shape[2]
    tm = _tile(t, 256)

    def body(*refs):
        a_refs = refs[:len(a_list)]
        w_ref, r_ref, o_ref = refs[len(a_list):]
        acc = jnp.zeros((tm, n), F32)
        for u in range(u_n):
            if swiglu:
                gv = a_refs[0][u].astype(F32)
                av = (gv * _sigmoid(gv) * a_refs[1][u].astype(F32)).astype(BF16)
            else:
                av = a_refs[0][u]
            acc = acc + _nn(av, w_ref[u])
        o_ref[...] = r_ref[...] + scale * acc

    a_spec = pl.BlockSpec((u_n, tm, ku), lambda i: (0, i, 0))
    return pl.pallas_call(
        body, name=name, grid=(t // tm,),
        in_specs=[a_spec] * len(a_list) + [pl.BlockSpec((u_n, ku, n), lambda i: (0, 0, 0)),
                                           pl.BlockSpec((tm, n), lambda i: (i, 0))],
        out_specs=pl.BlockSpec((tm, n), lambda i: (i, 0)),
        out_shape=jax.ShapeDtypeStruct((t, n), F32),
        compiler_params=_cparams(("parallel",)),
    )(*a_list, w, resid)


def rowmm_t(dyb, w, scale, out_dtype, name, gu=None):
    t, n = dyb.shape
    u_n, ku, _ = w.shape
    tm = _tile(t, 512)

    def body(*refs):
        if gu is None:
            dy_ref, w_ref, o_ref = refs
            o_ref[0] = (scale * _nt(dy_ref[...], w_ref[0])).astype(out_dtype)
        else:
            dy_ref, w_ref, g_ref, u_ref, dg_ref, du_ref, a_ref = refs
            dact = scale * _nt(dy_ref[...], w_ref[0])
            gv = g_ref[0].astype(F32)
            uv = u_ref[0].astype(F32)
            s = _sigmoid(gv)
            silu = gv * s
            dg_ref[0] = (dact * uv * (s * (1.0 + gv * (1.0 - s)))).astype(BF16)
            du_ref[0] = (dact * silu).astype(BF16)
            a_ref[0] = (silu * uv).astype(BF16)

    blk = pl.BlockSpec((1, tm, ku), lambda u, i: (u, i, 0))
    in_specs = [pl.BlockSpec((tm, n), lambda u, i: (i, 0)), pl.BlockSpec((1, ku, n), lambda u, i: (u, 0, 0))]
    if gu is None:
        return pl.pallas_call(
            body, name=name, grid=(u_n, t // tm), in_specs=in_specs, out_specs=blk,
            out_shape=jax.ShapeDtypeStruct((u_n, t, ku), out_dtype),
            compiler_params=_cparams(("parallel", "parallel")),
        )(dyb, w)
    return pl.pallas_call(
        body, name=name, grid=(u_n, t // tm), in_specs=in_specs + [blk, blk], out_specs=[blk] * 3,
        out_shape=[jax.ShapeDtypeStruct((u_n, t, ku), BF16)] * 3,
        compiler_params=_cparams(("parallel", "parallel")),
    )(dyb, w, *gu)


def colmm_t(dzs, ws, nu, x, g, dy_in, name):
    t, k = x.shape
    j, _, nj = ws[0].shape
    per = nj // nu
    units = j * per
    nw = len(ws)
    tm = _tile(t, 256)

    def body(*refs):
        dz_refs = refs[:nw]
        w_refs = refs[nw:2 * nw]
        x_ref, g_ref, dy_ref, dx_ref, dxb_ref, dg_ref = refs[2 * nw:]
        i = pl.program_id(0)
        dh = jnp.zeros((tm, k), F32)
        for n in range(nw):
            for u in range(units):
                wv = w_refs[n][u // per, :, (u % per) * nu:(u % per + 1) * nu]
                dh = dh + _nt(dz_refs[n][u], wv)
        xv = x_ref[...]
        gv = g_ref[...]
        r = lax.rsqrt(jnp.mean(xv * xv, axis=-1, keepdims=True) + EPS)
        uu = dh * gv
        dx = dy_ref[...] + r * uu - xv * (r * r * r * jnp.mean(uu * xv, axis=-1, keepdims=True))
        dx_ref[...] = dx
        dxb_ref[...] = dx.astype(BF16)
        part = jnp.sum(dh * (xv * r), axis=0, keepdims=True)

        @pl.when(i == 0)
        def _():
            dg_ref[...] = part

        @pl.when(i > 0)
        def _():
            dg_ref[...] += part

    dz_spec = pl.BlockSpec((units, tm, nu), lambda i: (0, i, 0))
    w_spec = pl.BlockSpec((j, k, nj), lambda i: (0, 0, 0))
    row = pl.BlockSpec((tm, k), lambda i: (i, 0))
    vec = pl.BlockSpec((1, k), lambda i: (0, 0))
    return pl.pallas_call(
        body, name=name, grid=(t // tm,),
        in_specs=[dz_spec] * nw + [w_spec] * nw + [row, vec, row],
        out_specs=[row, row, vec],
        out_shape=[jax.ShapeDtypeStruct((t, k), F32), jax.ShapeDtypeStruct((t, k), BF16),
                   jax.ShapeDtypeStruct((1, k), F32)],
        compiler_params=_cparams(("arbitrary",)),
    )(*dzs, *ws, x, g, dy_in)


def dw_col(h, dzs, j, nu, name):
    t, k = h.shape
    units = dzs[0].shape[0]
    per = units // j
    nw = len(dzs)
    tt = _tile(t, 512)

    def body(*refs):
        h_ref = refs[0]
        s = pl.program_id(1)
        hv = h_ref[...]
        for n in range(nw):
            part = _tn(hv, refs[1 + n][0])
            o_ref = refs[1 + nw + n]

            @pl.when(s == 0)
            def _():
                o_ref[0] = part

            @pl.when(s > 0)
            def _():
                o_ref[0] += part

    return pl.pallas_call(
        body, name=name, grid=(units, t // tt),
        in_specs=[pl.BlockSpec((tt, k), lambda u, s: (s, 0))] + [pl.BlockSpec((1, tt, nu), lambda u, s: (u, s, 0))] * nw,
        out_specs=[pl.BlockSpec((1, k, nu), lambda u, s: (u // per, 0, u % per))] * nw,
        out_shape=[jax.ShapeDtypeStruct((j, k, per * nu), F32)] * nw,
        compiler_params=_cparams(("parallel", "arbitrary")),
    )(h, *dzs)


def dw_row(a, dyb, scale, name):
    u_n, t, ku = a.shape
    n = dyb.shape[1]
    tt = _tile(t, 512)

    def body(a_ref, dy_ref, o_ref):
        s = pl.program_id(1)
        part = scale * _tn(a_ref[0], dy_ref[...])

        @pl.when(s == 0)
        def _():
            o_ref[0] = part

        @pl.when(s > 0)
        def _():
            o_ref[0] += part

    return pl.pallas_call(
        body, name=name, grid=(u_n, t // tt),
        in_specs=[pl.BlockSpec((1, tt, ku), lambda u, s: (u, s, 0)), pl.BlockSpec((tt, n), lambda u, s: (s, 0))],
        out_specs=pl.BlockSpec((1, ku, n), lambda u, s: (u, 0, 0)),
        out_shape=jax.ShapeDtypeStruct((u_n, ku, n), F32),
        compiler_params=_cparams(("parallel", "arbitrary")),
    )(a, dyb)


def loss_head(x, g, target):
    t, d = x.shape
    tm = _tile(t, 256)

    def body(x_ref, g_ref, t_ref, loss_ref, dx_ref, dxb_ref, dg_ref):
        i = pl.program_id(0)
        xv = x_ref[...]
        gv = g_ref[...]
        r = lax.rsqrt(jnp.mean(xv * xv, axis=-1, keepdims=True) + EPS)
        xh = xv * r
        err = xh * gv - t_ref[...]
        dy = err * (1.0 / d)
        uu = dy * gv
        dx = r * uu - xv * (r * r * r * jnp.mean(uu * xv, axis=-1, keepdims=True))
        dx_ref[...] = dx
        dxb_ref[...] = dx.astype(BF16)
        dg_part = jnp.sum(dy * xh, axis=0, keepdims=True)
        row = jnp.sum(err * err, axis=-1, keepdims=True) * (0.5 / d)
        l_part = jnp.zeros((8, LANES), F32) + jnp.sum(row, axis=0, keepdims=True)

        @pl.when(i == 0)
        def _():
            dg_ref[...] = dg_part
            loss_ref[...] = l_part

        @pl.when(i > 0)
        def _():
            dg_ref[...] += dg_part
            loss_ref[...] += l_part

    row = pl.BlockSpec((tm, d), lambda i: (i, 0))
    vec = pl.BlockSpec((1, d), lambda i: (0, 0))
    return pl.pallas_call(
        body, name="loss_head", grid=(t // tm,),
        in_specs=[row, vec, row],
        out_specs=[pl.BlockSpec((8, LANES), lambda i: (0, 0)), row, row, vec],
        out_shape=[jax.ShapeDtypeStruct((8, LANES), F32), jax.ShapeDtypeStruct((t, d), F32),
                   jax.ShapeDtypeStruct((t, d), BF16), jax.ShapeDtypeStruct((1, d), F32)],
        compiler_params=_cparams(("arbitrary",)),
    )(x, g, target)


def _split(v):
    hi = v.astype(BF16)
    lo = (v - hi.astype(F32)).astype(BF16)
    return hi, lo


def _keysums(v, m_ext):
    hi, lo = _split(v)
    outs = []
    for j in range(v.shape[1] // KBLK):
        sl = slice(j * KBLK, (j + 1) * KBLK)
        cs = _nn(jnp.concatenate([hi[:, sl], lo[:, sl]], axis=1), m_ext)
        outs.append((cs[:, :KBLK], cs[:, KBLK:]))
    return outs


def _softplus_parts(z):
    sp = jnp.maximum(z, 0.0) + jnp.log(1.0 + jnp.exp(-jnp.abs(z)))
    return sp, z - sp


def _sum_matrices():
    r = lax.broadcasted_iota(jnp.int32, (2 * KBLK, 2 * KBLK), 0) % KBLK
    c = lax.broadcasted_iota(jnp.int32, (2 * KBLK, 2 * KBLK), 1)
    suffix = jnp.where((r > c) | (c >= KBLK), 1.0, 0.0).astype(BF16)
    prefix = jnp.where((r <= c) | (c >= KBLK), 1.0, 0.0).astype(BF16)
    return suffix, prefix


def attn_fwd(qkv, n_seq, seq):
    t = qkv.shape[1]
    n_pairs = (qkv.shape[0] // 3) * 2
    bq = min(ATT_BLOCK, seq)
    nq = seq // bq
    nsub = bq // KBLK
    suffix_m, _ = _sum_matrices()

    def body(q_ref, k_ref, v_ref, m_ref, o_ref, tot_ref, cnt_ref):
        qi = pl.program_id(2)
        step_id = (pl.program_id(0) * n_pairs + pl.program_id(1)) * nq + qi
        lane = lax.broadcasted_iota(jnp.int32, (bq, LANES), 1)
        is_a = lane < HEAD_DIM
        q2 = q_ref[0] * jnp.asarray(HEAD_DIM ** -0.5, BF16)
        qs = (jnp.where(is_a, q2, jnp.zeros_like(q2)), jnp.where(is_a, jnp.zeros_like(q2), q2))
        m_ext = m_ref[...]
        row = lax.broadcasted_iota(jnp.int32, (bq, bq), 0)
        col = lax.broadcasted_iota(jnp.int32, (bq, bq), 1)
        diag_mask = col < row

        def block(kj, carry, mask):
            off = pl.multiple_of(kj * bq, bq)
            k2 = k_ref[0, pl.ds(off, bq), :]
            v2 = v_ref[0, pl.ds(off, bq), :]
            out = []
            for h in range(2):
                rem, acc = carry[h]
                z = _nt(qs[h], k2)
                sp, ls = _softplus_parts(z)
                lk = -sp if mask is None else jnp.where(mask, -sp, 0.0)
                sums = _keysums(lk, m_ext)
                parts = [None] * nsub
                for j in reversed(range(nsub)):
                    suf, total = sums[j]
                    parts[j] = jnp.exp(ls[:, j * KBLK:(j + 1) * KBLK] + suf + rem)
                    rem = rem + total
                a = jnp.concatenate(parts, axis=1)
                if mask is not None:
                    a = jnp.where(mask, a, 0.0)
                out.append((rem, acc + _nn(a.astype(BF16), v2)))
            return tuple(out)

        def most_left(c):
            return jnp.maximum(jnp.max(c[0][0]), jnp.max(c[1][0]))

        def more(s):
            return (s[0] < qi) & (s[1] > STICK_GONE)

        def step(s):
            c = block(qi - 1 - s[0], s[2], None)
            return s[0] + 1, most_left(c), c

        zero = jnp.zeros((bq, LANES), F32)
        carry = block(qi, ((zero, zero), (zero, zero)), diag_mask)
        n_left, _, carry = lax.while_loop(more, step, (jnp.int32(0), most_left(carry), carry))
        o_ref[0] = jnp.where(is_a, carry[0][1], carry[1][1]).astype(BF16)
        tot_ref[...] = jnp.where(is_a, carry[0][0], carry[1][0])
        cnt_ref[step_id] = n_left.astype(F32)

    upp = qkv.shape[0] // 3
    return pl.pallas_call(
        body, name="attn_fwd", grid=(n_seq, n_pairs, nq),
        in_specs=[pl.BlockSpec((1, bq, LANES), lambda b, p, i: (p // 2, b * nq + i, p % 2)),
                  pl.BlockSpec((1, seq, LANES), lambda b, p, i: (upp + p // 2, b, p % 2)),
                  pl.BlockSpec((1, seq, LANES), lambda b, p, i: (2 * upp + p // 2, b, p % 2)),
                  pl.BlockSpec((2 * KBLK, 2 * KBLK), lambda b, p, i: (0, 0))],
        out_specs=[pl.BlockSpec((1, bq, LANES), lambda b, p, i: (p // 2, b * nq + i, p % 2)),
                   pl.BlockSpec((bq, LANES), lambda b, p, i: (b * nq + i, p)),
                   pl.BlockSpec(memory_space=pltpu.SMEM)],
        out_shape=[jax.ShapeDtypeStruct((upp, t, 2 * LANES), BF16), jax.ShapeDtypeStruct((t, n_pairs * LANES), F32),
                   jax.ShapeDtypeStruct((n_seq * n_pairs * nq,), F32)],
        compiler_params=_cparams(("arbitrary", "arbitrary", "arbitrary")),
    )(qkv, qkv, qkv, suffix_m)


def attn_bwd(qkv, do, tot, cnt, n_seq, seq):
    t = qkv.shape[1]
    upp = qkv.shape[0] // 3
    n_pairs = upp * 2
    bq = min(ATT_BLOCK, seq)
    nq = seq // bq
    nsub = bq // KBLK
    _, prefix_m = _sum_matrices()
    scale = HEAD_DIM ** -0.5

    def body(q_ref, k_ref, v_ref, do_ref, tot_ref, m_ref, cnt_ref, dq_ref, dk_ref, dv_ref, dk_acc, dv_acc):
        qi = pl.program_id(2)
        step_id = (pl.program_id(0) * n_pairs + pl.program_id(1)) * nq + qi
        n_left = jnp.clip(cnt_ref[step_id].astype(jnp.int32), 0, qi)
        lane = lax.broadcasted_iota(jnp.int32, (bq, LANES), 1)
        is_a = lane < HEAD_DIM

        def halves(v2):
            z2 = jnp.zeros_like(v2)
            return jnp.where(is_a, v2, z2), jnp.where(is_a, z2, v2)

        qs = halves(q_ref[0] * jnp.asarray(scale, BF16))
        dos = halves(do_ref[0])
        tot2 = tot_ref[...]
        swapped = pltpu.roll(tot2, HEAD_DIM, 1)
        tots = (jnp.where(is_a, tot2, swapped), jnp.where(is_a, swapped, tot2))
        m_ext = m_ref[...]
        row = lax.broadcasted_iota(jnp.int32, (bq, bq), 0)
        col = lax.broadcasted_iota(jnp.int32, (bq, bq), 1)
        diag_mask = col < row

        @pl.when(qi == 0)
        def _():
            dk_acc[...] = jnp.zeros_like(dk_acc)
            dv_acc[...] = jnp.zeros_like(dv_acc)

        def block(kj, carry, mask):
            off = pl.multiple_of(kj * bq, bq)
            k2 = k_ref[0, pl.ds(off, bq), :]
            v2 = v_ref[0, pl.ds(off, bq), :]
            ks = halves(k2)
            dq = carry[2]
            dk_part = jnp.zeros((bq, LANES), F32)
            dv_part = jnp.zeros((bq, LANES), F32)
            out = []
            for h in range(2):
                pre, gpre = carry[h]
                z = _nt(qs[h], k2)
                sp, ls = _softplus_parts(z)
                lk = -sp if mask is None else jnp.where(mask, -sp, 0.0)
                sums = _keysums(lk, m_ext)
                parts = []
                for j in range(nsub):
                    pin, ptot = sums[j]
                    parts.append(jnp.exp(ls[:, j * KBLK:(j + 1) * KBLK] + (tots[h] - (pre + pin))))
                    pre = pre + ptot
                a = jnp.concatenate(parts, axis=1)
                if mask is not None:
                    a = jnp.where(mask, a, 0.0)
                g = a * _nt(dos[h], v2)
                gsums = _keysums(g, m_ext)
                parts = []
                for j in range(nsub):
                    gin, gtot = gsums[j]
                    parts.append(gpre + gin)
                    gpre = gpre + gtot
                dz = g - jnp.exp(ls) * jnp.concatenate(parts, axis=1)
                if mask is not None:
                    dz = jnp.where(mask, dz, 0.0)
                dzb = dz.astype(BF16)
                dq = dq + _nn(dzb, ks[h])
                dk_part = dk_part + _tn(dzb, qs[h])
                dv_part = dv_part + _tn(a.astype(BF16), dos[h])
                out.append((pre, gpre))
            dk_acc[pl.ds(off, bq), :] += dk_part
            dv_acc[pl.ds(off, bq), :] += dv_part
            return (out[0], out[1], dq)

        zero = jnp.zeros((bq, LANES), F32)
        carry = lax.fori_loop(qi - n_left, qi, lambda kj, c: block(kj, c, None), ((zero, zero), (zero, zero), zero))
        carry = block(qi, carry, diag_mask)
        dq_ref[0] = (carry[2] * scale).astype(BF16)

        @pl.when(qi == nq - 1)
        def _():
            dk_ref[0] = dk_acc[...].astype(BF16)
            dv_ref[0] = dv_acc[...].astype(BF16)

    qblk = lambda b, p, i: (p // 2, b * nq + i, p % 2)
    kv_out = pl.BlockSpec((1, seq, LANES), lambda b, p, i: (p // 2, b, p % 2))
    shp = jax.ShapeDtypeStruct((upp, t, 2 * LANES), BF16)
    return pl.pallas_call(
        body, name="attn_bwd", grid=(n_seq, n_pairs, nq),
        in_specs=[pl.BlockSpec((1, bq, LANES), qblk),
                  pl.BlockSpec((1, seq, LANES), lambda b, p, i: (upp + p // 2, b, p % 2)),
                  pl.BlockSpec((1, seq, LANES), lambda b, p, i: (2 * upp + p // 2, b, p % 2)),
                  pl.BlockSpec((1, bq, LANES), qblk),
                  pl.BlockSpec((bq, LANES), lambda b, p, i: (b * nq + i, p)),
                  pl.BlockSpec((2 * KBLK, 2 * KBLK), lambda b, p, i: (0, 0)),
                  pl.BlockSpec(memory_space=pltpu.SMEM)],
        out_specs=[pl.BlockSpec((1, bq, LANES), qblk), kv_out, kv_out],
        out_shape=[shp, shp, shp],
        scratch_shapes=[pltpu.VMEM((seq, LANES), F32), pltpu.VMEM((seq, LANES), F32)],
        compiler_params=_cparams(("parallel", "parallel", "arbitrary")),
    )(qkv, qkv, qkv, do, tot, prefix_m, cnt)


def _ln_stats(v):
    mu = jnp.mean(v, axis=-1, keepdims=True)
    vc = v - mu
    rstd = lax.rsqrt(jnp.mean(vc * vc, axis=-1, keepdims=True) + EPS)
    return vc * rstd, rstd


def _glu_into(a0_ref, av_ref, ag_ref, hv_ref, hg_ref, first):
    hv = hv_ref[0].astype(F32)
    hg = hg_ref[0].astype(F32)
    a0_ref[0:HALO, :] = jnp.where(first, 0.0, hv * _sigmoid(hg))
    av = av_ref[0].astype(F32)
    ag = ag_ref[0].astype(F32)
    a0_ref[HALO:, :] = av * _sigmoid(ag)


def _tril_mask():
    r = lax.broadcasted_iota(jnp.int32, (CHUNK, CHUNK), 0)
    c = lax.broadcasted_iota(jnp.int32, (CHUNK, CHUNK), 1)
    return c <= r


def mix_fwd(z, conv_w, conv_b, ln_a_g, ln_a_b, ln_v_g, ln_v_b, sp_w, sp_bt, seq):
    _, t, c = z.shape
    tm = _tile(seq, 512)
    tiles_per_seq = seq // tm
    groups = c // LANES
    hb = tm // HALO

    def body(av_ref, ag_ref, u_ref, v_ref, hv_ref, hg_ref, cw_ref, cb_ref, lag_ref, lab_ref, lvg_ref, lvb_ref,
             spw_ref, spb_ref, cat_ref, a1_ref, a0_ref):
        i = pl.program_id(0)
        _glu_into(a0_ref, av_ref, ag_ref, hv_ref, hg_ref, i % tiles_per_seq == 0)
        acc = jnp.zeros((tm, c), F32) + cb_ref[...]
        for k in range(CONV_WIDTH):
            acc = acc + cw_ref[k:k + 1, :] * a0_ref[pl.ds(HALO - (CONV_WIDTH - 1) + k, tm), :]
        a1_ref[...] = acc
        xh, _ = _ln_stats(acc)
        a2 = xh * lag_ref[...] + lab_ref[...]
        a3 = (a2 * _sigmoid(a2)).astype(BF16)
        half = c // 2
        cat_ref[0] = a3[:, :half]
        cat_ref[1] = a3[:, half:]
        tril = _tril_mask()
        for g in range(groups):
            sl = slice(g * LANES, (g + 1) * LANES)
            xh, _ = _ln_stats(v_ref[0][:, sl].astype(F32))
            vn = (xh * lvg_ref[:, sl] + lvb_ref[:, sl]).astype(BF16)
            w = jnp.where(tril, spw_ref[g], 0.0).astype(BF16)
            bias = spb_ref[:, g:g + 1]
            for ch in range(tm // CHUNK):
                rows = slice(ch * CHUNK, (ch + 1) * CHUNK)
                vs = _nn(w, vn[rows]) + bias
                bo = (u_ref[0][rows, sl].astype(F32) * vs).astype(BF16)
                cat_ref[2 + (g * LANES) // half, rows, (g * LANES) % half:(g * LANES) % half + LANES] = bo

    unit = lambda u: pl.BlockSpec((1, tm, c), lambda i: (u, i, 0))
    halo = lambda u: pl.BlockSpec((1, HALO, c), lambda i: (u, jnp.maximum(i * hb - 1, 0), 0))
    vec = pl.BlockSpec((1, c), lambda i: (0, 0))
    return pl.pallas_call(
        body, name="mix_fwd", grid=(t // tm,),
        in_specs=[unit(0), unit(1), unit(2), unit(3), halo(0), halo(1),
                  pl.BlockSpec((CONV_WIDTH, c), lambda i: (0, 0)), vec, vec, vec, vec, vec,
                  pl.BlockSpec((groups, CHUNK, CHUNK), lambda i: (0, 0, 0)),
                  pl.BlockSpec((CHUNK, groups), lambda i: (0, 0))],
        out_specs=[pl.BlockSpec((4, tm, c // 2), lambda i: (0, i, 0)), pl.BlockSpec((tm, c), lambda i: (i, 0))],
        out_shape=[jax.ShapeDtypeStruct((4, t, c // 2), BF16), jax.ShapeDtypeStruct((t, c), F32)],
        scratch_shapes=[pltpu.VMEM((HALO + tm, c), F32)],
        compiler_params=_cparams(("parallel",)),
    )(z, z, z, z, z, z, conv_w, conv_b, ln_a_g, ln_a_b, ln_v_g, ln_v_b, sp_w, sp_bt)


def mix_bwd_point(dcat, z, a1, ln_a_g, ln_a_b, ln_v_g, ln_v_b, sp_w, sp_wt, sp_bt, seq):
    _, t, c = z.shape
    tm = _tile(seq, 512)
    groups = c // LANES
    half = c // 2

    def body(dc_ref, u_ref, v_ref, a1_ref, lag_ref, lab_ref, lvg_ref, lvb_ref, spw_ref, spwt_ref, spb_ref,
             dz_ref, da1_ref, dcb_ref, dlag_ref, dlab_ref, dlvg_ref, dlvb_ref, dspw_ref, dspb_ref):
        i = pl.program_id(0)
        last = pl.num_programs(0) - 1

        @pl.when(i == 0)
        def _():
            for r in (dcb_ref, dlag_ref, dlab_ref, dlvg_ref, dlvb_ref, dspw_ref, dspb_ref):
                r[...] = jnp.zeros_like(r)

        da3 = jnp.concatenate([dc_ref[0], dc_ref[1]], axis=-1)
        xh, rstd = _ln_stats(a1_ref[...])
        a2 = xh * lag_ref[...] + lab_ref[...]
        s = _sigmoid(a2)
        da2 = da3 * (s * (1.0 + a2 * (1.0 - s)))
        dlag_ref[...] += jnp.sum(da2 * xh, axis=0, keepdims=True)
        dlab_ref[...] += jnp.sum(da2, axis=0, keepdims=True)
        dxh = da2 * lag_ref[...]
        da1 = rstd * (dxh - jnp.mean(dxh, axis=-1, keepdims=True) - xh * jnp.mean(dxh * xh, axis=-1, keepdims=True))
        da1_ref[...] = da1
        dcb_ref[...] += jnp.sum(da1, axis=0, keepdims=True)

        tril = _tril_mask()
        for g in range(groups):
            sl = slice(g * LANES, (g + 1) * LANES)
            xh, rstd = _ln_stats(v_ref[0][:, sl].astype(F32))
            lg = lvg_ref[:, sl]
            vnb = (xh * lg + lvb_ref[:, sl]).astype(BF16)
            w = jnp.where(tril, spw_ref[g], 0.0).astype(BF16)
            wt = jnp.where(tril.T, spwt_ref[g], 0.0).astype(BF16)
            bias = spb_ref[:, g:g + 1]
            dbo_all = dc_ref[2 + (g * LANES) // half][:, (g * LANES) % half:(g * LANES) % half + LANES]
            dvn_parts = []
            dw_acc = jnp.zeros((CHUNK, CHUNK), F32)
            db_acc = jnp.zeros((CHUNK, LANES), F32)
            for ch in range(tm // CHUNK):
                rows = slice(ch * CHUNK, (ch + 1) * CHUNK)
                vs = _nn(w, vnb[rows]) + bias
                dbo = dbo_all[rows]
                uv = u_ref[0][rows, sl].astype(F32)
                dz_ref[0, rows, sl] = (dbo * vs).astype(BF16)
                dvs = dbo * uv
                dvsb = dvs.astype(BF16)
                dvn_parts.append(_nn(wt, dvsb))
                dw_acc = dw_acc + _nt(dvsb, vnb[rows])
                db_acc = db_acc + dvs
            dvn = jnp.concatenate(dvn_parts, axis=0)
            dspw_ref[g] += jnp.where(tril, dw_acc, 0.0)
            dspb_ref[g] += db_acc
            dlvg_ref[:, sl] += jnp.sum(dvn * xh, axis=0, keepdims=True)
            dlvb_ref[:, sl] += jnp.sum(dvn, axis=0, keepdims=True)
            dxh = dvn * lg
            dv = rstd * (dxh - jnp.mean(dxh, axis=-1, keepdims=True) - xh * jnp.mean(dxh * xh, axis=-1, keepdims=True))
            dz_ref[1, :, sl] = dv.astype(BF16)

        @pl.when(i == last)
        def _():
            for g in range(groups):
                dspb_ref[g] = jnp.zeros((CHUNK, LANES), F32) + jnp.sum(dspb_ref[g], axis=-1, keepdims=True)

    unit = lambda u: pl.BlockSpec((1, tm, c), lambda i: (u, i, 0))
    vec = pl.BlockSpec((1, c), lambda i: (0, 0))
    sq = pl.BlockSpec((groups, CHUNK, CHUNK), lambda i: (0, 0, 0))
    vshape = jax.ShapeDtypeStruct((1, c), F32)
    sshape = jax.ShapeDtypeStruct((groups, CHUNK, CHUNK), F32)
    return pl.pallas_call(
        body, name="mix_bwd_point", grid=(t // tm,),
        in_specs=[pl.BlockSpec((4, tm, half), lambda i: (0, i, 0)), unit(2), unit(3),
                  pl.BlockSpec((tm, c), lambda i: (i, 0)), vec, vec, vec, vec, sq, sq,
                  pl.BlockSpec((CHUNK, groups), lambda i: (0, 0))],
        out_specs=[pl.BlockSpec((2, tm, c), lambda i: (1, i, 0)), pl.BlockSpec((tm, c), lambda i: (i, 0)),
                   vec, vec, vec, vec, vec, sq, sq],
        out_shape=[jax.ShapeDtypeStruct((4, t, c), BF16), jax.ShapeDtypeStruct((t, c), F32),
                   vshape, vshape, vshape, vshape, vshape, sshape, sshape],
        compiler_params=_cparams(("arbitrary",)),
    )(dcat, z, z, a1, ln_a_g, ln_a_b, ln_v_g, ln_v_b, sp_w, sp_wt, sp_bt)


def mix_bwd_conv(dz, da1, z, conv_w, seq):
    _, t, c = z.shape
    tm = _tile(seq, 512)
    tiles_per_seq = seq // tm
    hb = tm // HALO
    n_halo_blocks = t // HALO

    def body(dz_in_ref, d_ref, dh_ref, av_ref, ag_ref, hv_ref, hg_ref, cw_ref, dz_ref, dcw_ref, a0_ref, d1_ref):
        del dz_in_ref
        i = pl.program_id(0)
        _glu_into(a0_ref, av_ref, ag_ref, hv_ref, hg_ref, i % tiles_per_seq == 0)
        d1_ref[0:tm, :] = d_ref[...]
        d1_ref[tm:, :] = jnp.where((i + 1) % tiles_per_seq == 0, 0.0, dh_ref[...])

        @pl.when(i == 0)
        def _():
            dcw_ref[...] = jnp.zeros_like(dcw_ref)

        d1 = d_ref[...]
        da0 = jnp.zeros((tm, c), F32)
        for k in range(CONV_WIDTH):
            back = CONV_WIDTH - 1 - k
            da0 = da0 + cw_ref[k:k + 1, :] * d1_ref[pl.ds(back, tm), :]
            dcw_ref[k:k + 1, :] += jnp.sum(d1 * a0_ref[pl.ds(HALO - back, tm), :], axis=0, keepdims=True)
        av = av_ref[0].astype(F32)
        s = _sigmoid(ag_ref[0].astype(F32))
        dz_ref[0] = (da0 * s).astype(BF16)
        dz_ref[1] = (da0 * av * s * (1.0 - s)).astype(BF16)

    unit = lambda u: pl.BlockSpec((1, tm, c), lambda i: (u, i, 0))
    halo = lambda u: pl.BlockSpec((1, HALO, c), lambda i: (u, jnp.maximum(i * hb - 1, 0), 0))
    return pl.pallas_call(
        body, name="mix_bwd_conv", grid=(t // tm,),
        in_specs=[pl.BlockSpec(memory_space=pl.ANY), pl.BlockSpec((tm, c), lambda i: (i, 0)),
                  pl.BlockSpec((HALO, c), lambda i: (jnp.minimum((i + 1) * hb, n_halo_blocks - 1), 0)),
                  unit(0), unit(1), halo(0), halo(1), pl.BlockSpec((CONV_WIDTH, c), lambda i: (0, 0))],
        out_specs=[pl.BlockSpec((2, tm, c), lambda i: (0, i, 0)), pl.BlockSpec((CONV_WIDTH, c), lambda i: (0, 0))],
        out_shape=[jax.ShapeDtypeStruct(dz.shape, BF16), jax.ShapeDtypeStruct((CONV_WIDTH, c), F32)],
        scratch_shapes=[pltpu.VMEM((HALO + tm, c), F32), pltpu.VMEM((tm + HALO, c), F32)],
        input_output_aliases={0: 0},
        compiler_params=_cparams(("arbitrary",)),
    )(dz, da1, da1, z, z, z, z, conv_w)


CHIP_FLIPS = ((1, 0), (0, 1), (1, 1))
ANY = pl.BlockSpec(memory_space=pl.ANY)


def _place():
    return lax.axis_index("x"), lax.axis_index("y"), lax.axis_index("c")


def _flip(v, f):
    return 1 - v if f else v


def place_shard(w, chip, dtype, name):
    r, cc = w.shape
    rb = _tile(r, 512)

    def body(chip_ref, w_ref, o_ref):
        del chip_ref
        o_ref[0] = w_ref[...].astype(dtype)

    return pl.pallas_call(
        body, name=name,
        grid_spec=pltpu.PrefetchScalarGridSpec(
            num_scalar_prefetch=1, grid=(r // rb,),
            in_specs=[pl.BlockSpec((rb, cc), lambda i, chip_ref: (i, 0))],
            out_specs=pl.BlockSpec((1, rb, cc), lambda i, chip_ref: (chip_ref[0], i, 0))),
        out_shape=jax.ShapeDtypeStruct((N_CHIPS, r, cc), dtype),
        compiler_params=_cparams(("parallel",)),
    )(chip, w)


def allgather_weights(shards, smalls):
    n, ns = len(shards), len(smalls)

    def body(*refs):
        ins, sins = refs[:n], refs[n:n + ns]
        outs, souts = refs[n + ns:2 * n + ns], refs[2 * n + ns:2 * n + 2 * ns]
        ici_send, ici_recv, d2d_send, d2d_recv, sm_send, sm_recv = refs[2 * n + 2 * ns:]
        x, y, c = _place()
        k = 2 * x + y
        sibling = (x, y, 1 - c)
        pending = []

        def half(a):
            hr = shards[a].shape[1] // 2
            return pl.ds(pl.multiple_of(c * hr, 16), hr)

        for a in range(n):
            for o, (fx, fy) in enumerate(CHIP_FLIPS):
                cp = pltpu.make_async_remote_copy(
                    src_ref=ins[a].at[k, half(a)], dst_ref=outs[a].at[k, half(a)],
                    send_sem=ici_send.at[3 * a + o], recv_sem=ici_recv.at[3 * a + o],
                    device_id=(_flip(x, fx), _flip(y, fy), c), device_id_type=MESH)
                cp.start()
                pending.append(cp.wait_send)
        for a in range(ns):
            for o, (fx, fy) in enumerate(CHIP_FLIPS):
                cp = pltpu.make_async_remote_copy(
                    src_ref=sins[a].at[k], dst_ref=souts[a].at[k],
                    send_sem=sm_send.at[3 * a + o], recv_sem=sm_recv.at[3 * a + o],
                    device_id=(_flip(x, fx), _flip(y, fy), c), device_id_type=MESH)
                cp.start()
                pending.append(cp.wait_send)
        for a in range(n):
            for o, (fx, fy) in enumerate(CHIP_FLIPS):
                kk = 2 * _flip(x, fx) + _flip(y, fy)
                landed = outs[a].at[kk, half(a)]
                pltpu.make_async_remote_copy(
                    src_ref=landed, dst_ref=landed, send_sem=ici_send.at[3 * a + o], recv_sem=ici_recv.at[3 * a + o],
                    device_id=sibling, device_id_type=MESH).wait_recv()
                cp = pltpu.make_async_remote_copy(
                    src_ref=landed, dst_ref=landed, send_sem=d2d_send.at[3 * a + o], recv_sem=d2d_recv.at[3 * a + o],
                    device_id=sibling, device_id_type=MESH)
                cp.start()
                pending.append(cp.wait_send)
        for a in range(n):
            hr = shards[a].shape[1] // 2
            other = pl.ds(pl.multiple_of((1 - c) * hr, 16), hr)
            for o, (fx, fy) in enumerate(CHIP_FLIPS):
                kk = 2 * _flip(x, fx) + _flip(y, fy)
                got = outs[a].at[kk, other]
                pltpu.make_async_remote_copy(
                    src_ref=got, dst_ref=got, send_sem=d2d_send.at[3 * a + o], recv_sem=d2d_recv.at[3 * a + o],
                    device_id=sibling, device_id_type=MESH).wait_recv()
        for a in range(ns):
            for o, (fx, fy) in enumerate(CHIP_FLIPS):
                kk = 2 * _flip(x, fx) + _flip(y, fy)
                got = souts[a].at[kk]
                pltpu.make_async_remote_copy(
                    src_ref=got, dst_ref=got, send_sem=sm_send.at[3 * a + o], recv_sem=sm_recv.at[3 * a + o],
                    device_id=sibling, device_id_type=MESH).wait_recv()
        for w in pending:
            w()

    out_shape = [jax.ShapeDtypeStruct(s.shape, s.dtype) for s in list(shards) + list(smalls)]
    dma = pltpu.SemaphoreType.DMA
    res = pl.pallas_call(
        body, name="allgather_weights", in_specs=[ANY] * (n + ns), out_specs=[ANY] * (n + ns), out_shape=out_shape,
        scratch_shapes=[dma((3 * n,)), dma((3 * n,)), dma((3 * n,)), dma((3 * n,)), dma((3 * ns,)), dma((3 * ns,))],
        input_output_aliases={i: i for i in range(n + ns)},
        compiler_params=pltpu.CompilerParams(has_side_effects=True),
    )(*shards, *smalls)
    return res[:n], res[n:]


def rs_exchange(grads):
    n = len(grads)

    def body(*refs):
        ins, outs = refs[:n], refs[n:2 * n]
        send, recv = refs[2 * n:]
        x, y, c = _place()
        cps = []
        for a in range(n):
            cp = pltpu.make_async_remote_copy(
                src_ref=ins[a].at[:, 1 - c], dst_ref=outs[a], send_sem=send.at[a], recv_sem=recv.at[a],
                device_id=(x, y, 1 - c), device_id_type=MESH)
            cp.start()
            cps.append(cp)
        for cp in cps:
            cp.wait()

    dma = pltpu.SemaphoreType.DMA
    return pl.pallas_call(
        body, name="rs_exchange", in_specs=[ANY] * n, out_specs=[ANY] * n,
        out_shape=[jax.ShapeDtypeStruct((g.shape[0],) + g.shape[2:], g.dtype) for g in grads],
        scratch_shapes=[dma((n,)), dma((n,))],
        compiler_params=pltpu.CompilerParams(has_side_effects=True),
    )(*grads)


def rs_add(g, sib, core, out_dtype, name):
    nk, _, hr, cc = g.shape
    rb = _tile(hr, 256)

    def body(core_ref, g_ref, s_ref, o_ref):
        del core_ref
        o_ref[0] = (g_ref[0, 0] + s_ref[0]).astype(out_dtype)

    return pl.pallas_call(
        body, name=name,
        grid_spec=pltpu.PrefetchScalarGridSpec(
            num_scalar_prefetch=1, grid=(nk, hr // rb),
            in_specs=[pl.BlockSpec((1, 1, rb, cc), lambda k, i, core_ref: (k, core_ref[0], i, 0)),
                      pl.BlockSpec((1, rb, cc), lambda k, i, core_ref: (k, i, 0))],
            out_specs=pl.BlockSpec((1, rb, cc), lambda k, i, core_ref: (k, i, 0))),
        out_shape=jax.ShapeDtypeStruct((nk, hr, cc), out_dtype),
        compiler_params=_cparams(("parallel", "parallel")),
    )(core, g, sib)


def rs_send(parts):
    n = len(parts)

    def body(*refs):
        ins, outs = refs[:n], refs[n:2 * n]
        send, recv = refs[2 * n:]
        x, y, c = _place()
        waits = []
        for a in range(n):
            for o, (fx, fy) in enumerate(CHIP_FLIPS):
                kk = 2 * _flip(x, fx) + _flip(y, fy)
                cp = pltpu.make_async_remote_copy(
                    src_ref=ins[a].at[kk], dst_ref=outs[a].at[o], send_sem=send.at[3 * a + o], recv_sem=recv.at[3 * a + o],
                    device_id=(_flip(x, fx), _flip(y, fy), c), device_id_type=MESH)
                cp.start()
                waits.append(cp.wait)
        for w in waits:
            w()

    dma = pltpu.SemaphoreType.DMA
    return pl.pallas_call(
        body, name="rs_send", in_specs=[ANY] * n, out_specs=[ANY] * n,
        out_shape=[jax.ShapeDtypeStruct((3,) + p.shape[1:], p.dtype) for p in parts],
        scratch_shapes=[dma((3 * n,)), dma((3 * n,))],
        compiler_params=pltpu.CompilerParams(has_side_effects=True),
    )(*parts)


def rs_sum(recv, part, where, full, layer, n_layers, name):
    _, hr, cc = recv.shape
    rb = _tile(hr, 256)

    def body(*refs):
        r_ref, p_ref, o_ref = refs[1], refs[2], refs[-1]
        o_ref[0, 0] = ((p_ref[0].astype(F32) + r_ref[0].astype(F32)) + r_ref[1].astype(F32)) + r_ref[2].astype(F32)

    in_specs = [pl.BlockSpec((3, rb, cc), lambda i, w_ref: (0, i, 0)),
                pl.BlockSpec((1, rb, cc), lambda i, w_ref: (w_ref[0], i, 0))]
    args = [where, recv, part]
    aliases = {}
    if full is not None:
        in_specs.append(ANY)
        args.append(full)
        aliases = {3: 0}
    return pl.pallas_call(
        body, name=name,
        grid_spec=pltpu.PrefetchScalarGridSpec(
            num_scalar_prefetch=1, grid=(hr // rb,), in_specs=in_specs,
            out_specs=pl.BlockSpec((1, 1, rb, cc), lambda i, w_ref: (layer, w_ref[1], i, 0))),
        out_shape=jax.ShapeDtypeStruct((n_layers, 2, hr, cc), F32),
        input_output_aliases=aliases,
        compiler_params=_cparams(("parallel",)),
    )(*args)


def rs_share(fulls):
    n = len(fulls)

    def body(*refs):
        ins, outs = refs[:n], refs[n:2 * n]
        send, recv = refs[2 * n:]
        x, y, c = _place()
        cps = []
        for a in range(n):
            cp = pltpu.make_async_remote_copy(
                src_ref=ins[a].at[:, c], dst_ref=outs[a].at[:, c], send_sem=send.at[a], recv_sem=recv.at[a],
                device_id=(x, y, 1 - c), device_id_type=MESH)
            cp.start()
            cps.append(cp)
        for a in range(n):
            got = outs[a].at[:, 1 - c]
            pltpu.make_async_remote_copy(
                src_ref=got, dst_ref=got, send_sem=send.at[a], recv_sem=recv.at[a],
                device_id=(x, y, 1 - c), device_id_type=MESH).wait_recv()
        for cp in cps:
            cp.wait_send()

    dma = pltpu.SemaphoreType.DMA
    return pl.pallas_call(
        body, name="rs_share", in_specs=[ANY] * n, out_specs=[ANY] * n,
        out_shape=[jax.ShapeDtypeStruct(f.shape, f.dtype) for f in fulls],
        scratch_shapes=[dma((n,)), dma((n,))],
        input_output_aliases={i: i for i in range(n)},
        compiler_params=pltpu.CompilerParams(has_side_effects=True),
    )(*fulls)


def allreduce_small(v):
    r, w = v.shape

    def body(v_ref, o_ref, buf, send, recv, loc):
        x, y, c = _place()
        me = 4 * x + 2 * y + c
        mine = pltpu.make_async_copy(v_ref, buf.at[me], loc)
        mine.start()
        cps = []
        for o in range(1, N_DEV):
            fx, fy, fc = (o >> 2) & 1, (o >> 1) & 1, o & 1
            cp = pltpu.make_async_remote_copy(
                src_ref=v_ref, dst_ref=buf.at[me], send_sem=send.at[o - 1], recv_sem=recv.at[o - 1],
                device_id=(_flip(x, fx), _flip(y, fy), _flip(c, fc)), device_id_type=MESH)
            cp.start()
            cps.append(cp)
        for o in range(1, N_DEV):
            fx, fy, fc = (o >> 2) & 1, (o >> 1) & 1, o & 1
            peer = 4 * _flip(x, fx) + 2 * _flip(y, fy) + _flip(c, fc)
            pltpu.make_async_remote_copy(
                src_ref=v_ref, dst_ref=buf.at[peer], send_sem=send.at[o - 1], recv_sem=recv.at[o - 1],
                device_id=(x, y, c), device_id_type=MESH).wait_recv()
        for cp in cps:
            cp.wait_send()
        mine.wait()
        acc = buf[0]
        for d in range(1, N_DEV):
            acc = acc + buf[d]
        o_ref[...] = acc

    dma = pltpu.SemaphoreType.DMA
    vm = pl.BlockSpec(memory_space=pltpu.VMEM)
    return pl.pallas_call(
        body, name="allreduce_small", in_specs=[vm], out_specs=vm,
        out_shape=jax.ShapeDtypeStruct((r, w), F32),
        scratch_shapes=[pltpu.VMEM((N_DEV, r, w), F32), dma((N_DEV - 1,)), dma((N_DEV - 1,)), dma],
        compiler_params=pltpu.CompilerParams(has_side_effects=True, vmem_limit_bytes=VMEM_LIMIT),
    )(v)


def adamw(w, g, m, v, name):
    r, cc = w.shape
    rb = _tile(r, 256)

    def body(w_ref, g_ref, m_ref, v_ref, d_ref, nm_ref, nv_ref):
        gv = g_ref[...]
        nm = ADAM_B1 * m_ref[...] + (1.0 - ADAM_B1) * gv
        nv = ADAM_B2 * v_ref[...] + (1.0 - ADAM_B2) * (gv * gv)
        m_hat = nm / (1.0 - ADAM_B1 ** ADAM_STEP)
        v_hat = nv / (1.0 - ADAM_B2 ** ADAM_STEP)
        d_ref[...] = -ADAM_LR * (m_hat / (jnp.sqrt(v_hat) + ADAM_EPS) + ADAM_WD * w_ref[...])
        nm_ref[...] = nm
        nv_ref[...] = nv

    blk = pl.BlockSpec((rb, cc), lambda i: (i, 0))
    shp = jax.ShapeDtypeStruct((r, cc), F32)
    return pl.pallas_call(
        body, name=name, grid=(r // rb,), in_specs=[blk] * 4, out_specs=[blk] * 3, out_shape=[shp] * 3,
        compiler_params=_cparams(("parallel",)),
    )(w, g, m, v)


WEIGHTS = ['g_ffn1', 'w_ffn1_gate', 'w_ffn1_up', 'w_ffn1_down', 'g_mix', 'w_in_ab', 'conv_w', 'conv_b', 'ln_a_g',
           'ln_a_b', 'ln_v_g', 'ln_v_b', 'sp_w', 'sp_b', 'w_out_ab', 'w_qkv', 'w_o', 'g_ffn2', 'w_ffn2_gate',
           'w_ffn2_up', 'w_ffn2_down', 'g_final']
BIG = ['w_ffn1_gate', 'w_ffn1_up', 'w_ffn1_down', 'w_in_ab', 'w_out_ab', 'w_qkv', 'w_o', 'w_ffn2_gate', 'w_ffn2_up',
       'w_ffn2_down']
SMALL = ['g_ffn1', 'g_mix', 'g_ffn2', 'g_final', 'conv_b', 'ln_a_g', 'ln_a_b', 'ln_v_g', 'ln_v_b', 'sp_b', 'sp_w']


def _rows(a):
    return a.reshape(-1, LANES)


def _pack(parts):
    v = jnp.concatenate([_rows(p) for p in parts], axis=0)
    pad = (-v.shape[0]) % 8
    return jnp.pad(v, ((0, pad), (0, 0)))


def _unpack(v, shapes):
    out, r = [], 0
    for s in shapes:
        n = 1
        for d in s:
            n *= d
        n //= LANES
        out.append(v[r:r + n].reshape(s))
        r += n
    return out


def kernel(x, g_ffn1, w_ffn1_gate, w_ffn1_up, w_ffn1_down, g_mix, w_in_ab, conv_w, conv_b, ln_a_g, ln_a_b, ln_v_g, ln_v_b, sp_w, sp_b, w_out_ab, w_qkv, w_o, g_ffn2, w_ffn2_gate, w_ffn2_up, w_ffn2_down, g_final, loss_target, m_g_ffn1, m_w_ffn1_gate, m_w_ffn1_up, m_w_ffn1_down, m_g_mix, m_w_in_ab, m_conv_w, m_conv_b, m_ln_a_g, m_ln_a_b, m_ln_v_g, m_ln_v_b, m_sp_w, m_sp_b, m_w_out_ab, m_w_qkv, m_w_o, m_g_ffn2, m_w_ffn2_gate, m_w_ffn2_up, m_w_ffn2_down, m_g_final, v_g_ffn1, v_w_ffn1_gate, v_w_ffn1_up, v_w_ffn1_down, v_g_mix, v_w_in_ab, v_conv_w, v_conv_b, v_ln_a_g, v_ln_a_b, v_ln_v_g, v_ln_v_b, v_sp_w, v_sp_b, v_w_out_ab, v_w_qkv, v_w_o, v_g_ffn2, v_w_ffn2_gate, v_w_ffn2_up, v_w_ffn2_down, v_g_final):
    p = dict(locals())
    n_seq, seq, d = x.shape
    t = n_seq * seq
    depth = g_ffn1.shape[0]
    core = lax.axis_index("c")
    chip = 2 * lax.axis_index("x") + lax.axis_index("y")
    xf = x.reshape(t, d)
    target = loss_target.reshape(t, d)

    items = []
    for name in BIG:
        for layer in range(p[name].shape[0]):
            items.append((name, layer))
    chip1 = chip.reshape(1).astype(jnp.int32)
    shards = [place_shard(p[name][layer], chip1, BF16, "place_shard") for name, layer in items]
    gathered, (conv_w4,) = allgather_weights(shards, [place_shard(conv_w[0], chip1, F32, "place_conv_w")])
    wt = {it: g for it, g in zip(items, gathered)}
    c_mix = conv_w4.shape[2] * N_CHIPS
    conv_full = jnp.transpose(conv_w4, (1, 0, 2)).reshape(CONV_WIDTH, c_mix)
    vec = lambda a: a.reshape(1, -1)
    sp_bt = sp_b[0].T
    sp_wt = jnp.transpose(sp_w[0], (0, 2, 1))
    d_ff = w_ffn1_gate.shape[2]
    n_in = w_in_ab.shape[2]
    n_qkv = w_qkv.shape[2] // 3

    saved = []
    xc = xf
    for layer in range(depth):
        s = {}
        for half, (gn, wn) in enumerate((('g_ffn1', 'w_ffn1'), ('g_ffn2', 'w_ffn2'))):
            if half == 1:
                s['x_mix'] = xc
                s['h_mix'] = rmsnorm_fwd(xc, vec(g_mix[layer]), "norm_mix")
                if layer % 2 == 0:
                    (z,) = colmm(s['h_mix'], [wt[('w_in_ab', layer // 2)]], n_in, BF16, "mm_in")
                    cat, a1 = mix_fwd(z, conv_full, conv_b, ln_a_g, ln_a_b, vec(ln_v_g), vec(ln_v_b), sp_w[0], sp_bt, seq)
                    s.update(z=z, cat=cat, a1=a1)
                    xc = rowmm([cat], wt[('w_out_ab', layer // 2)], xc, 1.0, "mm_out")
                else:
                    (qkv,) = colmm(s['h_mix'], [wt[('w_qkv', layer // 2)]], n_qkv, BF16, "mm_qkv")
                    o, tot, cnt = attn_fwd(qkv, n_seq, seq)
                    s.update(qkv=qkv, o=o, tot=tot, cnt=cnt)
                    xc = rowmm([o], wt[('w_o', layer // 2)], xc, 1.0, "mm_o")
            s['x' + wn] = xc
            h = rmsnorm_fwd(xc, vec(p[gn][layer]), "norm_ffn")
            gate, up = colmm(h, [wt[(wn + '_gate', layer)], wt[(wn + '_up', layer)]], d_ff, BF16, "ffn_gateup")
            xc = rowmm([gate, up], wt[(wn + '_down', layer)], xc, 0.5, "ffn_down")
            s.update({'h' + wn: h, 'gate' + wn: gate, 'up' + wn: up})
        saved.append(s)

    loss8, dx, dxb, dg_final = loss_head(xc, vec(g_final), target)
    loss = lax.psum(loss8[0, 0], ("x", "y", "c"))

    gw = {}
    gs = {}
    for layer in reversed(range(depth)):
        s = saved[layer]
        for half, (gn, wn) in reversed(list(enumerate((('g_ffn1', 'w_ffn1'), ('g_ffn2', 'w_ffn2'))))):
            wd = wt[(wn + '_down', layer)]
            dgate, dup, act = rowmm_t(dxb, wd, 0.5, BF16, "ffn_bwd_act", gu=(s['gate' + wn], s['up' + wn]))
            gw[(wn + '_down', layer)] = dw_row(act, dxb, 0.5, "ffn_dw_down")
            gw[(wn + '_gate', layer)], gw[(wn + '_up', layer)] = dw_col(s['h' + wn], [dgate, dup], N_CHIPS, d_ff, "ffn_dw_gateup")
            dx, dxb, dg = colmm_t([dgate, dup], [wt[(wn + '_gate', layer)], wt[(wn + '_up', layer)]], d_ff,
                                  s['x' + wn], vec(p[gn][layer]), dx, "ffn_bwd_in")
            gs[(gn, layer)] = dg
            if half == 1:
                if layer % 2 == 0:
                    i = layer // 2
                    w_out = wt[('w_out_ab', i)]
                    dcat = rowmm_t(dxb, w_out, 1.0, F32, "mm_out_t")
                    gw[('w_out_ab', i)] = dw_row(s['cat'], dxb, 1.0, "dw_out")
                    dz, da1, dcb, dlag, dlab, dlvg, dlvb, dspw, dspb = mix_bwd_point(
                        dcat, s['z'], s['a1'], ln_a_g, ln_a_b, vec(ln_v_g), vec(ln_v_b), sp_w[0], sp_wt, sp_bt, seq)
                    dz, dcw = mix_bwd_conv(dz, da1, s['z'], conv_full, seq)
                    gs.update({('conv_b', i): dcb, ('ln_a_g', i): dlag, ('ln_a_b', i): dlab, ('ln_v_g', i): dlvg,
                               ('ln_v_b', i): dlvb, ('sp_w', i): dspw, ('sp_b', i): dspb[:, :, 0], ('conv_w', i): dcw})
                    (gw[('w_in_ab', i)],) = dw_col(s['h_mix'], [dz], N_CHIPS, n_in, "dw_in")
                    dx, dxb, dg = colmm_t([dz], [wt[('w_in_ab', i)]], n_in, s['x_mix'], vec(g_mix[layer]), dx, "mm_in_t")
                else:
                    i = layer // 2
                    w_o4 = wt[('w_o', i)]
                    do = rowmm_t(dxb, w_o4, 1.0, BF16, "mm_o_t")
                    gw[('w_o', i)] = dw_row(s['o'], dxb, 1.0, "dw_o")
                    dq, dk, dv = attn_bwd(s['qkv'], do, s['tot'], s['cnt'], n_seq, seq)
                    dqkv = jnp.concatenate([dq, dk, dv], axis=0)
                    (gw[('w_qkv', i)],) = dw_col(s['h_mix'], [dqkv], N_CHIPS, n_qkv, "dw_qkv")
                    dx, dxb, dg = colmm_t([dqkv], [wt[('w_qkv', i)]], n_qkv, s['x_mix'], vec(g_mix[layer]), dx, "mm_qkv_t")
                gs[('g_mix', layer)] = dg
    grad_x = dx.reshape(x.shape)

    core1 = core.reshape(1).astype(jnp.int32)
    g4 = [gw[it].reshape(N_CHIPS, 2, gw[it].shape[1] // 2, gw[it].shape[2]) for it in items]
    sib = rs_exchange(g4)
    parts = [rs_add(g, sb, core1, REDUCE_DTYPE, "rs_add") for g, sb in zip(g4, sib)]
    recv = dict(zip(items, rs_send(parts)))
    part = dict(zip(items, parts))
    where = jnp.stack([chip, core]).astype(jnp.int32)
    fulls = []
    for name in BIG:
        full = None
        n_layers = p[name].shape[0]
        for layer in range(n_layers):
            full = rs_sum(recv[(name, layer)], part[(name, layer)], where, full, layer, n_layers, "rs_sum")
        fulls.append(full)
    shared = rs_share(fulls)
    grads = {name: sh.reshape(p[name].shape) for name, sh in zip(BIG, shared)}

    stack = lambda name: jnp.concatenate([gs[(name, layer)].reshape((1,) + p[name].shape[1:]) for layer in range(p[name].shape[0])], axis=0)
    small_g = [stack(name) if name != 'g_final' else dg_final.reshape(p[name].shape) for name in SMALL]
    packed = _pack(small_g + [gs[('conv_w', 0)]])
    red = allreduce_small(packed)
    outs = _unpack(red, [p[name].shape for name in SMALL] + [(CONV_WIDTH, c_mix)])
    for name, g in zip(SMALL, outs[:-1]):
        grads[name] = g
    conv_g = outs[-1].reshape(CONV_WIDTH, N_CHIPS, c_mix // N_CHIPS)
    grads['conv_w'] = lax.dynamic_index_in_dim(conv_g, chip, axis=1, keepdims=False).reshape(conv_w.shape)

    delta, new_m, new_v = {}, {}, {}
    for name in BIG:
        shp = p[name].shape
        two = lambda a: a.reshape(shp[0] * shp[1], shp[2])
        dl, nm, nv = adamw(two(p[name]), two(grads[name]), two(p['m_' + name]), two(p['v_' + name]), "adamw")
        delta[name], new_m[name], new_v[name] = dl.reshape(shp), nm.reshape(shp), nv.reshape(shp)
    small_names = SMALL + ['conv_w']
    pk = lambda pre: _pack([p[pre + name] for name in small_names])
    dl, nm, nv = adamw(pk(''), _pack([grads[name] for name in small_names]), pk('m_'), pk('v_'), "adamw_small")
    shapes = [p[name].shape for name in small_names]
    for dst, val in ((delta, dl), (new_m, nm), (new_v, nv)):
        for name, a in zip(small_names, _unpack(val, shapes)):
            dst[name] = a

    return (loss, grad_x, *[grads[n] for n in WEIGHTS], *[delta[n] for n in WEIGHTS],
            *[new_m[n] for n in WEIGHTS], *[new_v[n] for n in WEIGHTS])
```

```python
import functools

import jax
import jax.numpy as jnp
from jax import lax
from jax.experimental import pallas as pl
from jax.experimental.pallas import tpu as pltpu

F32 = jnp.float32
BF16 = jnp.bfloat16
EPS = 1e-6
HEAD_DIM = 64
CONV_WIDTH = 31
CHUNK = 128
KBLK = 128
ATT_BLOCK = 512
DW_TOKENS = 2048
STICK_GONE = -110.0
LANES = 128
HALO = 32
ADAM_LR, ADAM_B1, ADAM_B2, ADAM_EPS, ADAM_WD, ADAM_STEP = 0.001, 0.9, 0.999, 1e-08, 0.01, 10
VMEM_LIMIT = 56 * 1024 * 1024
MESH = pl.DeviceIdType.MESH
N_CHIPS = 4
N_DEV = 8
REDUCE_DTYPE = BF16


def _cparams(sem):
    return pltpu.CompilerParams(dimension_semantics=sem, vmem_limit_bytes=VMEM_LIMIT)


def _nt(a, b):
    return lax.dot_general(a, b, (((1,), (1,)), ((), ())), preferred_element_type=F32)


def _tn(a, b):
    return lax.dot_general(a, b, (((0,), (0,)), ((), ())), preferred_element_type=F32)


def _nn(a, b):
    return jnp.dot(a, b, preferred_element_type=F32)


def _sigmoid(x):
    return 0.5 * jnp.tanh(0.5 * x) + 0.5


def _tile(t, want):
    if t <= want:
        return t
    for cand in range(want - want % 8, 7, -8):
        if t % cand == 0:
            return cand
    raise ValueError((t, want))


def rmsnorm_fwd(x, g, name):
    t, d = x.shape
    tm = _tile(t, 512)

    def body(x_ref, g_ref, h_ref):
        xv = x_ref[...]
        r = lax.rsqrt(jnp.mean(xv * xv, axis=-1, keepdims=True) + EPS)
        h_ref[...] = (xv * r * g_ref[...]).astype(BF16)

    return pl.pallas_call(
        body, name=name, grid=(t // tm,),
        in_specs=[pl.BlockSpec((tm, d), lambda i: (i, 0)), pl.BlockSpec((1, d), lambda i: (0, 0))],
        out_specs=pl.BlockSpec((tm, d), lambda i: (i, 0)),
        out_shape=jax.ShapeDtypeStruct((t, d), BF16),
        compiler_params=_cparams(("parallel",)),
    )(x, g)


def colmm(h, ws, nu, out_dtype, name):
    t, k = h.shape
    j, _, nj = ws[0].shape
    per = nj // nu
    units = j * per
    tm = _tile(t, 512)
    nw = len(ws)

    def body(*refs):
        h_ref = refs[0]
        hv = h_ref[...]
        for n in range(nw):
            res = _nn(hv, refs[1 + n][0]).astype(out_dtype)
            for u in range(per):
                refs[1 + nw + n][u] = res[:, u * nu:(u + 1) * nu]

    w_spec = pl.BlockSpec((1, k, nj), lambda s, i: (s, 0, 0))
    o_spec = pl.BlockSpec((per, tm, nu), lambda s, i: (s, i, 0))
    outs = pl.pallas_call(
        body, name=name, grid=(j, t // tm),
        in_specs=[pl.BlockSpec((tm, k), lambda s, i: (i, 0))] + [w_spec] * nw,
        out_specs=[o_spec] * nw,
        out_shape=[jax.ShapeDtypeStruct((units, t, nu), out_dtype)] * nw,
        compiler_params=_cparams(("parallel", "parallel")),
    )(h, *ws)
    return outs


def rowmm(a_list, w, resid, scale, name):
    swiglu = len(a_list) == 2
    u_n, t, ku = a_list[0].shape
    n = w.shape[2]
    tm = _tile(t, 256)

    def body(*refs):
        a_refs = refs[:len(a_list)]
        w_ref, r_ref, o_ref = refs[len(a_list):]
        acc = jnp.zeros((tm, n), F32)
        for u in range(u_n):
            if swiglu:
                gv = a_refs[0][u].astype(F32)
                av = (gv * _sigmoid(gv) * a_refs[1][u].astype(F32)).astype(BF16)
            else:
                av = a_refs[0][u]
            acc = acc + _nn(av, w_ref[u])
        o_ref[...] = r_ref[...] + scale * acc

    a_spec = pl.BlockSpec((u_n, tm, ku), lambda i: (0, i, 0))
    return pl.pallas_call(
        body, name=name, grid=(t // tm,),
        in_specs=[a_spec] * len(a_list) + [pl.BlockSpec((u_n, ku, n), lambda i: (0, 0, 0)),
                                           pl.BlockSpec((tm, n), lambda i: (i, 0))],
        out_specs=pl.BlockSpec((tm, n), lambda i: (i, 0)),
        out_shape=jax.ShapeDtypeStruct((t, n), F32),
        compiler_params=_cparams(("parallel",)),
    )(*a_list, w, resid)


def rowmm_t(dyb, w, scale, out_dtype, name, gu=None):
    t, n = dyb.shape
    u_n, ku, _ = w.shape
    tm = _tile(t, 512)

    def body(*refs):
        if gu is None:
            dy_ref, w_ref, o_ref = refs
            o_ref[0] = (scale * _nt(dy_ref[...], w_ref[0])).astype(out_dtype)
        else:
            dy_ref, w_ref, g_ref, u_ref, dg_ref, du_ref, a_ref = refs
            dact = scale * _nt(dy_ref[...], w_ref[0])
            gv = g_ref[0].astype(F32)
            uv = u_ref[0].astype(F32)
            s = _sigmoid(gv)
            silu = gv * s
            dg_ref[0] = (dact * uv * (s * (1.0 + gv * (1.0 - s)))).astype(BF16)
            du_ref[0] = (dact * silu).astype(BF16)
            a_ref[0] = (silu * uv).astype(BF16)

    blk = pl.BlockSpec((1, tm, ku), lambda u, i: (u, i, 0))
    in_specs = [pl.BlockSpec((tm, n), lambda u, i: (i, 0)), pl.BlockSpec((1, ku, n), lambda u, i: (u, 0, 0))]
    if gu is None:
        return pl.pallas_call(
            body, name=name, grid=(u_n, t // tm), in_specs=in_specs, out_specs=blk,
            out_shape=jax.ShapeDtypeStruct((u_n, t, ku), out_dtype),
            compiler_params=_cparams(("parallel", "parallel")),
        )(dyb, w)
    return pl.pallas_call(
        body, name=name, grid=(u_n, t // tm), in_specs=in_specs + [blk, blk], out_specs=[blk] * 3,
        out_shape=[jax.ShapeDtypeStruct((u_n, t, ku), BF16)] * 3,
        compiler_params=_cparams(("parallel", "parallel")),
    )(dyb, w, *gu)


def colmm_t(dzs, ws, nu, x, g, dy_in, name):
    t, k = x.shape
    j, _, nj = ws[0].shape
    per = nj // nu
    units = j * per
    nw = len(ws)
    tm = _tile(t, 256)

    def body(*refs):
        dz_refs = refs[:nw]
        w_refs = refs[nw:2 * nw]
        x_ref, g_ref, dy_ref, dx_ref, dxb_ref, dg_ref = refs[2 * nw:]
        i = pl.program_id(0)
        dh = jnp.zeros((tm, k), F32)
        for n in range(nw):
            for u in range(units):
                wv = w_refs[n][u // per, :, (u % per) * nu:(u % per + 1) * nu]
                dh = dh + _nt(dz_refs[n][u], wv)
        xv = x_ref[...]
        gv = g_ref[...]
        r = lax.rsqrt(jnp.mean(xv * xv, axis=-1, keepdims=True) + EPS)
        uu = dh * gv
        dx = dy_ref[...] + r * uu - xv * (r * r * r * jnp.mean(uu * xv, axis=-1, keepdims=True))
        dx_ref[...] = dx
        dxb_ref[...] = dx.astype(BF16)
        part = jnp.sum(dh * (xv * r), axis=0, keepdims=True)

        @pl.when(i == 0)
        def _():
            dg_ref[...] = part

        @pl.when(i > 0)
        def _():
            dg_ref[...] += part

    dz_spec = pl.BlockSpec((units, tm, nu), lambda i: (0, i, 0))
    w_spec = pl.BlockSpec((j, k, nj), lambda i: (0, 0, 0))
    row = pl.BlockSpec((tm, k), lambda i: (i, 0))
    vec = pl.BlockSpec((1, k), lambda i: (0, 0))
    return pl.pallas_call(
        body, name=name, grid=(t // tm,),
        in_specs=[dz_spec] * nw + [w_spec] * nw + [row, vec, row],
        out_specs=[row, row, vec],
        out_shape=[jax.ShapeDtypeStruct((t, k), F32), jax.ShapeDtypeStruct((t, k), BF16),
                   jax.ShapeDtypeStruct((1, k), F32)],
        compiler_params=_cparams(("arbitrary",)),
    )(*dzs, *ws, x, g, dy_in)


def dw_col(h, dzs, j, nu, name):
    t, k = h.shape
    units = dzs[0].shape[0]
    per = units // j
    nw = len(dzs)
    tt = _tile(t, DW_TOKENS)

    def body(*refs):
        h_ref = refs[0]
        s = pl.program_id(1)
        hv = h_ref[...]
        for n in range(nw):
            o_ref = refs[1 + nw + n]
            for u in range(per):
                part = _tn(hv, refs[1 + n][u])
                cols = slice(u * nu, (u + 1) * nu)

                @pl.when(s == 0)
                def _():
                    o_ref[0, :, cols] = part

                @pl.when(s > 0)
                def _():
                    o_ref[0, :, cols] += part

    return pl.pallas_call(
        body, name=name, grid=(j, t // tt),
        in_specs=[pl.BlockSpec((tt, k), lambda u, s: (s, 0))] + [pl.BlockSpec((per, tt, nu), lambda u, s: (u, s, 0))] * nw,
        out_specs=[pl.BlockSpec((1, k, per * nu), lambda u, s: (u, 0, 0))] * nw,
        out_shape=[jax.ShapeDtypeStruct((j, k, per * nu), F32)] * nw,
        compiler_params=_cparams(("parallel", "arbitrary")),
    )(h, *dzs)


def dw_row(a, dyb, scale, name):
    u_n, t, ku = a.shape
    n = dyb.shape[1]
    tt = _tile(t, DW_TOKENS)

    def body(a_ref, dy_ref, o_ref):
        s = pl.program_id(1)
        part = scale * _tn(a_ref[0], dy_ref[...])

        @pl.when(s == 0)
        def _():
            o_ref[0] = part

        @pl.when(s > 0)
        def _():
            o_ref[0] += part

    return pl.pallas_call(
        body, name=name, grid=(u_n, t // tt),
        in_specs=[pl.BlockSpec((1, tt, ku), lambda u, s: (u, s, 0)), pl.BlockSpec((tt, n), lambda u, s: (s, 0))],
        out_specs=pl.BlockSpec((1, ku, n), lambda u, s: (u, 0, 0)),
        out_shape=jax.ShapeDtypeStruct((u_n, ku, n), F32),
        compiler_params=_cparams(("parallel", "arbitrary")),
    )(a, dyb)


def loss_head(x, g, target):
    t, d = x.shape
    tm = _tile(t, 256)

    def body(x_ref, g_ref, t_ref, loss_ref, dx_ref, dxb_ref, dg_ref):
        i = pl.program_id(0)
        xv = x_ref[...]
        gv = g_ref[...]
        r = lax.rsqrt(jnp.mean(xv * xv, axis=-1, keepdims=True) + EPS)
        xh = xv * r
        err = xh * gv - t_ref[...]
        dy = err * (1.0 / d)
        uu = dy * gv
        dx = r * uu - xv * (r * r * r * jnp.mean(uu * xv, axis=-1, keepdims=True))
        dx_ref[...] = dx
        dxb_ref[...] = dx.astype(BF16)
        dg_part = jnp.sum(dy * xh, axis=0, keepdims=True)
        row = jnp.sum(err * err, axis=-1, keepdims=True) * (0.5 / d)
        l_part = jnp.zeros((8, LANES), F32) + jnp.sum(row, axis=0, keepdims=True)

        @pl.when(i == 0)
        def _():
            dg_ref[...] = dg_part
            loss_ref[...] = l_part

        @pl.when(i > 0)
        def _():
            dg_ref[...] += dg_part
            loss_ref[...] += l_part

    row = pl.BlockSpec((tm, d), lambda i: (i, 0))
    vec = pl.BlockSpec((1, d), lambda i: (0, 0))
    return pl.pallas_call(
        body, name="loss_head", grid=(t // tm,),
        in_specs=[row, vec, row],
        out_specs=[pl.BlockSpec((8, LANES), lambda i: (0, 0)), row, row, vec],
        out_shape=[jax.ShapeDtypeStruct((8, LANES), F32), jax.ShapeDtypeStruct((t, d), F32),
                   jax.ShapeDtypeStruct((t, d), BF16), jax.ShapeDtypeStruct((1, d), F32)],
        compiler_params=_cparams(("arbitrary",)),
    )(x, g, target)


def _split(v):
    hi = v.astype(BF16)
    lo = (v - hi.astype(F32)).astype(BF16)
    return hi, lo


def _keysums(v, m_ext):
    hi, lo = _split(v)
    outs = []
    for j in range(v.shape[1] // KBLK):
        sl = slice(j * KBLK, (j + 1) * KBLK)
        cs = _nn(jnp.concatenate([hi[:, sl], lo[:, sl]], axis=1), m_ext)
        outs.append((cs[:, :KBLK], cs[:, KBLK:]))
    return outs


def _softplus_parts(z):
    sp = jnp.maximum(z, 0.0) + jnp.log(1.0 + jnp.exp(-jnp.abs(z)))
    return sp, z - sp


def _sum_matrices():
    r = lax.broadcasted_iota(jnp.int32, (2 * KBLK, 2 * KBLK), 0) % KBLK
    c = lax.broadcasted_iota(jnp.int32, (2 * KBLK, 2 * KBLK), 1)
    suffix = jnp.where((r > c) | (c >= KBLK), 1.0, 0.0).astype(BF16)
    prefix = jnp.where((r <= c) | (c >= KBLK), 1.0, 0.0).astype(BF16)
    return suffix, prefix


def attn_fwd(qkv, n_seq, seq):
    t = qkv.shape[1]
    n_pairs = (qkv.shape[0] // 3) * 2
    bq = min(ATT_BLOCK, seq)
    nq = seq // bq
    nsub = bq // KBLK
    suffix_m, _ = _sum_matrices()

    def body(q_ref, k_ref, v_ref, m_ref, o_ref, tot_ref, cnt_ref):
        qi = pl.program_id(2)
        step_id = (pl.program_id(0) * n_pairs + pl.program_id(1)) * nq + qi
        lane = lax.broadcasted_iota(jnp.int32, (bq, LANES), 1)
        is_a = lane < HEAD_DIM
        q2 = q_ref[0] * jnp.asarray(HEAD_DIM ** -0.5, BF16)
        qs = (jnp.where(is_a, q2, jnp.zeros_like(q2)), jnp.where(is_a, jnp.zeros_like(q2), q2))
        m_ext = m_ref[...]
        row = lax.broadcasted_iota(jnp.int32, (bq, bq), 0)
        col = lax.broadcasted_iota(jnp.int32, (bq, bq), 1)
        diag_mask = col < row

        def block(kj, carry, mask):
            off = pl.multiple_of(kj * bq, bq)
            k2 = k_ref[0, pl.ds(off, bq), :]
            v2 = v_ref[0, pl.ds(off, bq), :]
            out = []
            for h in range(2):
                rem, acc = carry[h]
                z = _nt(qs[h], k2)
                sp, ls = _softplus_parts(z)
                lk = -sp if mask is None else jnp.where(mask, -sp, 0.0)
                sums = _keysums(lk, m_ext)
                parts = [None] * nsub
                for j in reversed(range(nsub)):
                    suf, total = sums[j]
                    parts[j] = jnp.exp(ls[:, j * KBLK:(j + 1) * KBLK] + suf + rem)
                    rem = rem + total
                a = jnp.concatenate(parts, axis=1)
                if mask is not None:
                    a = jnp.where(mask, a, 0.0)
                out.append((rem, acc + _nn(a.astype(BF16), v2)))
            return tuple(out)

        def most_left(c):
            return jnp.maximum(jnp.max(c[0][0]), jnp.max(c[1][0]))

        def more(s):
            return (s[0] < qi) & (s[1] > STICK_GONE)

        def step(s):
            c = block(qi - 1 - s[0], s[2], None)
            return s[0] + 1, most_left(c), c

        zero = jnp.zeros((bq, LANES), F32)
        carry = block(qi, ((zero, zero), (zero, zero)), diag_mask)
        n_left, _, carry = lax.while_loop(more, step, (jnp.int32(0), most_left(carry), carry))
        o_ref[0] = jnp.where(is_a, carry[0][1], carry[1][1]).astype(BF16)
        tot_ref[...] = jnp.where(is_a, carry[0][0], carry[1][0])
        cnt_ref[step_id] = n_left.astype(F32)

    upp = qkv.shape[0] // 3
    return pl.pallas_call(
        body, name="attn_fwd", grid=(n_seq, n_pairs, nq),
        in_specs=[pl.BlockSpec((1, bq, LANES), lambda b, p, i: (p // 2, b * nq + i, p % 2)),
                  pl.BlockSpec((1, seq, LANES), lambda b, p, i: (upp + p // 2, b, p % 2)),
                  pl.BlockSpec((1, seq, LANES), lambda b, p, i: (2 * upp + p // 2, b, p % 2)),
                  pl.BlockSpec((2 * KBLK, 2 * KBLK), lambda b, p, i: (0, 0))],
        out_specs=[pl.BlockSpec((1, bq, LANES), lambda b, p, i: (p // 2, b * nq + i, p % 2)),
                   pl.BlockSpec((bq, LANES), lambda b, p, i: (b * nq + i, p)),
                   pl.BlockSpec(memory_space=pltpu.SMEM)],
        out_shape=[jax.ShapeDtypeStruct((upp, t, 2 * LANES), BF16), jax.ShapeDtypeStruct((t, n_pairs * LANES), F32),
                   jax.ShapeDtypeStruct((n_seq * n_pairs * nq,), F32)],
        compiler_params=_cparams(("arbitrary", "arbitrary", "arbitrary")),
    )(qkv, qkv, qkv, suffix_m)


def attn_bwd(qkv, do, tot, cnt, n_seq, seq):
    t = qkv.shape[1]
    upp = qkv.shape[0] // 3
    n_pairs = upp * 2
    bq = min(ATT_BLOCK, seq)
    nq = seq // bq
    nsub = bq // KBLK
    _, prefix_m = _sum_matrices()
    scale = HEAD_DIM ** -0.5

    def body(q_ref, k_ref, v_ref, do_ref, tot_ref, m_ref, cnt_ref, dq_ref, dk_ref, dv_ref, dk_acc, dv_acc):
        qi = pl.program_id(2)
        step_id = (pl.program_id(0) * n_pairs + pl.program_id(1)) * nq + qi
        n_left = jnp.clip(cnt_ref[step_id].astype(jnp.int32), 0, qi)
        lane = lax.broadcasted_iota(jnp.int32, (bq, LANES), 1)
        is_a = lane < HEAD_DIM

        def halves(v2):
            z2 = jnp.zeros_like(v2)
            return jnp.where(is_a, v2, z2), jnp.where(is_a, z2, v2)

        qs = halves(q_ref[0] * jnp.asarray(scale, BF16))
        dos = halves(do_ref[0])
        tot2 = tot_ref[...]
        swapped = pltpu.roll(tot2, HEAD_DIM, 1)
        tots = (jnp.where(is_a, tot2, swapped), jnp.where(is_a, swapped, tot2))
        m_ext = m_ref[...]
        row = lax.broadcasted_iota(jnp.int32, (bq, bq), 0)
        col = lax.broadcasted_iota(jnp.int32, (bq, bq), 1)
        diag_mask = col < row

        @pl.when(qi == 0)
        def _():
            dk_acc[...] = jnp.zeros_like(dk_acc)
            dv_acc[...] = jnp.zeros_like(dv_acc)

        def block(kj, carry, mask):
            off = pl.multiple_of(kj * bq, bq)
            k2 = k_ref[0, pl.ds(off, bq), :]
            v2 = v_ref[0, pl.ds(off, bq), :]
            ks = halves(k2)
            dq = carry[2]
            dk_part = jnp.zeros((bq, LANES), F32)
            dv_part = jnp.zeros((bq, LANES), F32)
            out = []
            for h in range(2):
                pre, gpre = carry[h]
                z = _nt(qs[h], k2)
                sp, ls = _softplus_parts(z)
                lk = -sp if mask is None else jnp.where(mask, -sp, 0.0)
                sums = _keysums(lk, m_ext)
                parts = []
                for j in range(nsub):
                    pin, ptot = sums[j]
                    parts.append(jnp.exp(ls[:, j * KBLK:(j + 1) * KBLK] + (tots[h] - (pre + pin))))
                    pre = pre + ptot
                a = jnp.concatenate(parts, axis=1)
                if mask is not None:
                    a = jnp.where(mask, a, 0.0)
                g = a * _nt(dos[h], v2)
                gsums = _keysums(g, m_ext)
                parts = []
                for j in range(nsub):
                    gin, gtot = gsums[j]
                    parts.append(gpre + gin)
                    gpre = gpre + gtot
                dz = g - jnp.exp(ls) * jnp.concatenate(parts, axis=1)
                if mask is not None:
                    dz = jnp.where(mask, dz, 0.0)
                dzb = dz.astype(BF16)
                dq = dq + _nn(dzb, ks[h])
                dk_part = dk_part + _tn(dzb, qs[h])
                dv_part = dv_part + _tn(a.astype(BF16), dos[h])
                out.append((pre, gpre))
            dk_acc[pl.ds(off, bq), :] += dk_part
            dv_acc[pl.ds(off, bq), :] += dv_part
            return (out[0], out[1], dq)

        zero = jnp.zeros((bq, LANES), F32)
        carry = lax.fori_loop(qi - n_left, qi, lambda kj, c: block(kj, c, None), ((zero, zero), (zero, zero), zero))
        carry = block(qi, carry, diag_mask)
        dq_ref[0] = (carry[2] * scale).astype(BF16)

        @pl.when(qi == nq - 1)
        def _():
            dk_ref[0] = dk_acc[...].astype(BF16)
            dv_ref[0] = dv_acc[...].astype(BF16)

    qblk = lambda b, p, i: (p // 2, b * nq + i, p % 2)
    kv_out = pl.BlockSpec((1, seq, LANES), lambda b, p, i: (p // 2, b, p % 2))
    shp = jax.ShapeDtypeStruct((upp, t, 2 * LANES), BF16)
    return pl.pallas_call(
        body, name="attn_bwd", grid=(n_seq, n_pairs, nq),
        in_specs=[pl.BlockSpec((1, bq, LANES), qblk),
                  pl.BlockSpec((1, seq, LANES), lambda b, p, i: (upp + p // 2, b, p % 2)),
                  pl.BlockSpec((1, seq, LANES), lambda b, p, i: (2 * upp + p // 2, b, p % 2)),
                  pl.BlockSpec((1, bq, LANES), qblk),
                  pl.BlockSpec((bq, LANES), lambda b, p, i: (b * nq + i, p)),
                  pl.BlockSpec((2 * KBLK, 2 * KBLK), lambda b, p, i: (0, 0)),
                  pl.BlockSpec(memory_space=pltpu.SMEM)],
        out_specs=[pl.BlockSpec((1, bq, LANES), qblk), kv_out, kv_out],
        out_shape=[shp, shp, shp],
        scratch_shapes=[pltpu.VMEM((seq, LANES), F32), pltpu.VMEM((seq, LANES), F32)],
        compiler_params=_cparams(("parallel", "parallel", "arbitrary")),
    )(qkv, qkv, qkv, do, tot, prefix_m, cnt)


def _ln_stats(v):
    mu = jnp.mean(v, axis=-1, keepdims=True)
    vc = v - mu
    rstd = lax.rsqrt(jnp.mean(vc * vc, axis=-1, keepdims=True) + EPS)
    return vc * rstd, rstd


def _glu_into(a0_ref, av_ref, ag_ref, hv_ref, hg_ref, first):
    hv = hv_ref[0].astype(F32)
    hg = hg_ref[0].astype(F32)
    a0_ref[0:HALO, :] = jnp.where(first, 0.0, hv * _sigmoid(hg))
    av = av_ref[0].astype(F32)
    ag = ag_ref[0].astype(F32)
    a0_ref[HALO:, :] = av * _sigmoid(ag)


def _tril_mask():
    r = lax.broadcasted_iota(jnp.int32, (CHUNK, CHUNK), 0)
    c = lax.broadcasted_iota(jnp.int32, (CHUNK, CHUNK), 1)
    return c <= r


def mix_fwd(z, conv_w, conv_b, ln_a_g, ln_a_b, ln_v_g, ln_v_b, sp_w, sp_bt, seq):
    _, t, c = z.shape
    tm = _tile(seq, 512)
    tiles_per_seq = seq // tm
    groups = c // LANES
    hb = tm // HALO

    def body(av_ref, ag_ref, u_ref, v_ref, hv_ref, hg_ref, cw_ref, cb_ref, lag_ref, lab_ref, lvg_ref, lvb_ref,
             spw_ref, spb_ref, cat_ref, a1_ref, a0_ref):
        i = pl.program_id(0)
        _glu_into(a0_ref, av_ref, ag_ref, hv_ref, hg_ref, i % tiles_per_seq == 0)
        acc = jnp.zeros((tm, c), F32) + cb_ref[...]
        for k in range(CONV_WIDTH):
            acc = acc + cw_ref[k:k + 1, :] * a0_ref[pl.ds(HALO - (CONV_WIDTH - 1) + k, tm), :]
        a1_ref[...] = acc
        xh, _ = _ln_stats(acc)
        a2 = xh * lag_ref[...] + lab_ref[...]
        a3 = (a2 * _sigmoid(a2)).astype(BF16)
        half = c // 2
        cat_ref[0] = a3[:, :half]
        cat_ref[1] = a3[:, half:]
        tril = _tril_mask()
        for g in range(groups):
            sl = slice(g * LANES, (g + 1) * LANES)
            xh, _ = _ln_stats(v_ref[0][:, sl].astype(F32))
            vn = (xh * lvg_ref[:, sl] + lvb_ref[:, sl]).astype(BF16)
            w = jnp.where(tril, spw_ref[g], 0.0).astype(BF16)
            bias = spb_ref[:, g:g + 1]
            for ch in range(tm // CHUNK):
                rows = slice(ch * CHUNK, (ch + 1) * CHUNK)
                vs = _nn(w, vn[rows]) + bias
                bo = (u_ref[0][rows, sl].astype(F32) * vs).astype(BF16)
                cat_ref[2 + (g * LANES) // half, rows, (g * LANES) % half:(g * LANES) % half + LANES] = bo

    unit = lambda u: pl.BlockSpec((1, tm, c), lambda i: (u, i, 0))
    halo = lambda u: pl.BlockSpec((1, HALO, c), lambda i: (u, jnp.maximum(i * hb - 1, 0), 0))
    vec = pl.BlockSpec((1, c), lambda i: (0, 0))
    return pl.pallas_call(
        body, name="mix_fwd", grid=(t // tm,),
        in_specs=[unit(0), unit(1), unit(2), unit(3), halo(0), halo(1),
                  pl.BlockSpec((CONV_WIDTH, c), lambda i: (0, 0)), vec, vec, vec, vec, vec,
                  pl.BlockSpec((groups, CHUNK, CHUNK), lambda i: (0, 0, 0)),
                  pl.BlockSpec((CHUNK, groups), lambda i: (0, 0))],
        out_specs=[pl.BlockSpec((4, tm, c // 2), lambda i: (0, i, 0)), pl.BlockSpec((tm, c), lambda i: (i, 0))],
        out_shape=[jax.ShapeDtypeStruct((4, t, c // 2), BF16), jax.ShapeDtypeStruct((t, c), F32)],
        scratch_shapes=[pltpu.VMEM((HALO + tm, c), F32)],
        compiler_params=_cparams(("parallel",)),
    )(z, z, z, z, z, z, conv_w, conv_b, ln_a_g, ln_a_b, ln_v_g, ln_v_b, sp_w, sp_bt)


def mix_bwd_point(dcat, z, a1, ln_a_g, ln_a_b, ln_v_g, ln_v_b, sp_w, sp_wt, sp_bt, seq):
    _, t, c = z.shape
    tm = _tile(seq, 512)
    groups = c // LANES
    half = c // 2

    def body(dc_ref, u_ref, v_ref, a1_ref, lag_ref, lab_ref, lvg_ref, lvb_ref, spw_ref, spwt_ref, spb_ref,
             dz_ref, da1_ref, dcb_ref, dlag_ref, dlab_ref, dlvg_ref, dlvb_ref, dspw_ref, dspb_ref):
        i = pl.program_id(0)
        last = pl.num_programs(0) - 1

        @pl.when(i == 0)
        def _():
            for r in (dcb_ref, dlag_ref, dlab_ref, dlvg_ref, dlvb_ref, dspw_ref, dspb_ref):
                r[...] = jnp.zeros_like(r)

        da3 = jnp.concatenate([dc_ref[0], dc_ref[1]], axis=-1)
        xh, rstd = _ln_stats(a1_ref[...])
        a2 = xh * lag_ref[...] + lab_ref[...]
        s = _sigmoid(a2)
        da2 = da3 * (s * (1.0 + a2 * (1.0 - s)))
        dlag_ref[...] += jnp.sum(da2 * xh, axis=0, keepdims=True)
        dlab_ref[...] += jnp.sum(da2, axis=0, keepdims=True)
        dxh = da2 * lag_ref[...]
        da1 = rstd * (dxh - jnp.mean(dxh, axis=-1, keepdims=True) - xh * jnp.mean(dxh * xh, axis=-1, keepdims=True))
        da1_ref[...] = da1
        dcb_ref[...] += jnp.sum(da1, axis=0, keepdims=True)

        tril = _tril_mask()
        for g in range(groups):
            sl = slice(g * LANES, (g + 1) * LANES)
            xh, rstd = _ln_stats(v_ref[0][:, sl].astype(F32))
            lg = lvg_ref[:, sl]
            vnb = (xh * lg + lvb_ref[:, sl]).astype(BF16)
            w = jnp.where(tril, spw_ref[g], 0.0).astype(BF16)
            wt = jnp.where(tril.T, spwt_ref[g], 0.0).astype(BF16)
            bias = spb_ref[:, g:g + 1]
            dbo_all = dc_ref[2 + (g * LANES) // half][:, (g * LANES) % half:(g * LANES) % half + LANES]
            dvn_parts = []
            dw_acc = jnp.zeros((CHUNK, CHUNK), F32)
            db_acc = jnp.zeros((CHUNK, LANES), F32)
            for ch in range(tm // CHUNK):
                rows = slice(ch * CHUNK, (ch + 1) * CHUNK)
                vs = _nn(w, vnb[rows]) + bias
                dbo = dbo_all[rows]
                uv = u_ref[0][rows, sl].astype(F32)
                dz_ref[0, rows, sl] = (dbo * vs).astype(BF16)
                dvs = dbo * uv
                dvsb = dvs.astype(BF16)
                dvn_parts.append(_nn(wt, dvsb))
                dw_acc = dw_acc + _nt(dvsb, vnb[rows])
                db_acc = db_acc + dvs
            dvn = jnp.concatenate(dvn_parts, axis=0)
            dspw_ref[g] += jnp.where(tril, dw_acc, 0.0)
            dspb_ref[g] += db_acc
            dlvg_ref[:, sl] += jnp.sum(dvn * xh, axis=0, keepdims=True)
            dlvb_ref[:, sl] += jnp.sum(dvn, axis=0, keepdims=True)
            dxh = dvn * lg
            dv = rstd * (dxh - jnp.mean(dxh, axis=-1, keepdims=True) - xh * jnp.mean(dxh * xh, axis=-1, keepdims=True))
            dz_ref[1, :, sl] = dv.astype(BF16)

        @pl.when(i == last)
        def _():
            for g in range(groups):
                dspb_ref[g] = jnp.zeros((CHUNK, LANES), F32) + jnp.sum(dspb_ref[g], axis=-1, keepdims=True)

    unit = lambda u: pl.BlockSpec((1, tm, c), lambda i: (u, i, 0))
    vec = pl.BlockSpec((1, c), lambda i: (0, 0))
    sq = pl.BlockSpec((groups, CHUNK, CHUNK), lambda i: (0, 0, 0))
    vshape = jax.ShapeDtypeStruct((1, c), F32)
    sshape = jax.ShapeDtypeStruct((groups, CHUNK, CHUNK), F32)
    return pl.pallas_call(
        body, name="mix_bwd_point", grid=(t // tm,),
        in_specs=[pl.BlockSpec((4, tm, half), lambda i: (0, i, 0)), unit(2), unit(3),
                  pl.BlockSpec((tm, c), lambda i: (i, 0)), vec, vec, vec, vec, sq, sq,
                  pl.BlockSpec((CHUNK, groups), lambda i: (0, 0))],
        out_specs=[pl.BlockSpec((2, tm, c), lambda i: (1, i, 0)), pl.BlockSpec((tm, c), lambda i: (i, 0)),
                   vec, vec, vec, vec, vec, sq, sq],
        out_shape=[jax.ShapeDtypeStruct((4, t, c), BF16), jax.ShapeDtypeStruct((t, c), F32),
                   vshape, vshape, vshape, vshape, vshape, sshape, sshape],
        compiler_params=_cparams(("arbitrary",)),
    )(dcat, z, z, a1, ln_a_g, ln_a_b, ln_v_g, ln_v_b, sp_w, sp_wt, sp_bt)


def mix_bwd_conv(dz, da1, z, conv_w, seq):
    _, t, c = z.shape
    tm = _tile(seq, 512)
    tiles_per_seq = seq // tm
    hb = tm // HALO
    n_halo_blocks = t // HALO

    def body(dz_in_ref, d_ref, dh_ref, av_ref, ag_ref, hv_ref, hg_ref, cw_ref, dz_ref, dcw_ref, a0_ref, d1_ref):
        del dz_in_ref
        i = pl.program_id(0)
        _glu_into(a0_ref, av_ref, ag_ref, hv_ref, hg_ref, i % tiles_per_seq == 0)
        d1_ref[0:tm, :] = d_ref[...]
        d1_ref[tm:, :] = jnp.where((i + 1) % tiles_per_seq == 0, 0.0, dh_ref[...])

        @pl.when(i == 0)
        def _():
            dcw_ref[...] = jnp.zeros_like(dcw_ref)

        d1 = d_ref[...]
        da0 = jnp.zeros((tm, c), F32)
        for k in range(CONV_WIDTH):
            back = CONV_WIDTH - 1 - k
            da0 = da0 + cw_ref[k:k + 1, :] * d1_ref[pl.ds(back, tm), :]
            dcw_ref[k:k + 1, :] += jnp.sum(d1 * a0_ref[pl.ds(HALO - back, tm), :], axis=0, keepdims=True)
        av = av_ref[0].astype(F32)
        s = _sigmoid(ag_ref[0].astype(F32))
        dz_ref[0] = (da0 * s).astype(BF16)
        dz_ref[1] = (da0 * av * s * (1.0 - s)).astype(BF16)

    unit = lambda u: pl.BlockSpec((1, tm, c), lambda i: (u, i, 0))
    halo = lambda u: pl.BlockSpec((1, HALO, c), lambda i: (u, jnp.maximum(i * hb - 1, 0), 0))
    return pl.pallas_call(
        body, name="mix_bwd_conv", grid=(t // tm,),
        in_specs=[pl.BlockSpec(memory_space=pl.ANY), pl.BlockSpec((tm, c), lambda i: (i, 0)),
                  pl.BlockSpec((HALO, c), lambda i: (jnp.minimum((i + 1) * hb, n_halo_blocks - 1), 0)),
                  unit(0), unit(1), halo(0), halo(1), pl.BlockSpec((CONV_WIDTH, c), lambda i: (0, 0))],
        out_specs=[pl.BlockSpec((2, tm, c), lambda i: (0, i, 0)), pl.BlockSpec((CONV_WIDTH, c), lambda i: (0, 0))],
        out_shape=[jax.ShapeDtypeStruct(dz.shape, BF16), jax.ShapeDtypeStruct((CONV_WIDTH, c), F32)],
        scratch_shapes=[pltpu.VMEM((HALO + tm, c), F32), pltpu.VMEM((tm + HALO, c), F32)],
        input_output_aliases={0: 0},
        compiler_params=_cparams(("arbitrary",)),
    )(dz, da1, da1, z, z, z, z, conv_w)


CHIP_FLIPS = ((1, 0), (0, 1), (1, 1))
ANY = pl.BlockSpec(memory_space=pl.ANY)


def _place():
    return lax.axis_index("x"), lax.axis_index("y"), lax.axis_index("c")


def _flip(v, f):
    return 1 - v if f else v


def place_shard(w, chip, dtype, name):
    r, cc = w.shape
    rb = _tile(r, 512)

    def body(chip_ref, w_ref, o_ref):
        del chip_ref
        o_ref[0] = w_ref[...].astype(dtype)

    return pl.pallas_call(
        body, name=name,
        grid_spec=pltpu.PrefetchScalarGridSpec(
            num_scalar_prefetch=1, grid=(r // rb,),
            in_specs=[pl.BlockSpec((rb, cc), lambda i, chip_ref: (i, 0))],
            out_specs=pl.BlockSpec((1, rb, cc), lambda i, chip_ref: (chip_ref[0], i, 0))),
        out_shape=jax.ShapeDtypeStruct((N_CHIPS, r, cc), dtype),
        compiler_params=_cparams(("parallel",)),
    )(chip, w)


def allgather_weights(shards, smalls):
    n, ns = len(shards), len(smalls)

    def body(*refs):
        ins, sins = refs[:n], refs[n:n + ns]
        outs, souts = refs[n + ns:2 * n + ns], refs[2 * n + ns:2 * n + 2 * ns]
        ici_send, ici_recv, d2d_send, d2d_recv, sm_send, sm_recv = refs[2 * n + 2 * ns:]
        x, y, c = _place()
        k = 2 * x + y
        sibling = (x, y, 1 - c)
        pending = []

        def half(a):
            hr = shards[a].shape[1] // 2
            return pl.ds(pl.multiple_of(c * hr, 16), hr)

        for a in range(n):
            for o, (fx, fy) in enumerate(CHIP_FLIPS):
                cp = pltpu.make_async_remote_copy(
                    src_ref=ins[a].at[k, half(a)], dst_ref=outs[a].at[k, half(a)],
                    send_sem=ici_send.at[3 * a + o], recv_sem=ici_recv.at[3 * a + o],
                    device_id=(_flip(x, fx), _flip(y, fy), c), device_id_type=MESH)
                cp.start()
                pending.append(cp.wait_send)
        for a in range(ns):
            for o, (fx, fy) in enumerate(CHIP_FLIPS):
                cp = pltpu.make_async_remote_copy(
                    src_ref=sins[a].at[k], dst_ref=souts[a].at[k],
                    send_sem=sm_send.at[3 * a + o], recv_sem=sm_recv.at[3 * a + o],
                    device_id=(_flip(x, fx), _flip(y, fy), c), device_id_type=MESH)
                cp.start()
                pending.append(cp.wait_send)
        for a in range(n):
            for o, (fx, fy) in enumerate(CHIP_FLIPS):
                kk = 2 * _flip(x, fx) + _flip(y, fy)
                landed = outs[a].at[kk, half(a)]
                pltpu.make_async_remote_copy(
                    src_ref=landed, dst_ref=landed, send_sem=ici_send.at[3 * a + o], recv_sem=ici_recv.at[3 * a + o],
                    device_id=sibling, device_id_type=MESH).wait_recv()
                cp = pltpu.make_async_remote_copy(
                    src_ref=landed, dst_ref=landed, send_sem=d2d_send.at[3 * a + o], recv_sem=d2d_recv.at[3 * a + o],
                    device_id=sibling, device_id_type=MESH)
                cp.start()
                pending.append(cp.wait_send)
        for a in range(n):
            hr = shards[a].shape[1] // 2
            other = pl.ds(pl.multiple_of((1 - c) * hr, 16), hr)
            for o, (fx, fy) in enumerate(CHIP_FLIPS):
                kk = 2 * _flip(x, fx) + _flip(y, fy)
                got = outs[a].at[kk, other]
                pltpu.make_async_remote_copy(
                    src_ref=got, dst_ref=got, send_sem=d2d_send.at[3 * a + o], recv_sem=d2d_recv.at[3 * a + o],
                    device_id=sibling, device_id_type=MESH).wait_recv()
        for a in range(ns):
            for o, (fx, fy) in enumerate(CHIP_FLIPS):
                kk = 2 * _flip(x, fx) + _flip(y, fy)
                got = souts[a].at[kk]
                pltpu.make_async_remote_copy(
                    src_ref=got, dst_ref=got, send_sem=sm_send.at[3 * a + o], recv_sem=sm_recv.at[3 * a + o],
                    device_id=sibling, device_id_type=MESH).wait_recv()
        for w in pending:
            w()

    out_shape = [jax.ShapeDtypeStruct(s.shape, s.dtype) for s in list(shards) + list(smalls)]
    dma = pltpu.SemaphoreType.DMA
    res = pl.pallas_call(
        body, name="allgather_weights", in_specs=[ANY] * (n + ns), out_specs=[ANY] * (n + ns), out_shape=out_shape,
        scratch_shapes=[dma((3 * n,)), dma((3 * n,)), dma((3 * n,)), dma((3 * n,)), dma((3 * ns,)), dma((3 * ns,))],
        input_output_aliases={i: i for i in range(n + ns)},
        compiler_params=pltpu.CompilerParams(has_side_effects=True),
    )(*shards, *smalls)
    return res[:n], res[n:]


def rs_exchange(grads):
    n = len(grads)

    def body(*refs):
        ins, outs = refs[:n], refs[n:2 * n]
        send, recv = refs[2 * n:]
        x, y, c = _place()
        cps = []
        for a in range(n):
            cp = pltpu.make_async_remote_copy(
                src_ref=ins[a].at[:, 1 - c], dst_ref=outs[a], send_sem=send.at[a], recv_sem=recv.at[a],
                device_id=(x, y, 1 - c), device_id_type=MESH)
            cp.start()
            cps.append(cp)
        for cp in cps:
            cp.wait()

    dma = pltpu.SemaphoreType.DMA
    return pl.pallas_call(
        body, name="rs_exchange", in_specs=[ANY] * n, out_specs=[ANY] * n,
        out_shape=[jax.ShapeDtypeStruct((g.shape[0],) + g.shape[2:], g.dtype) for g in grads],
        scratch_shapes=[dma((n,)), dma((n,))],
        compiler_params=pltpu.CompilerParams(has_side_effects=True),
    )(*grads)


def rs_add(g, sib, core, out_dtype, name):
    nk, _, hr, cc = g.shape
    rb = _tile(hr, 256)

    def body(core_ref, g_ref, s_ref, o_ref):
        del core_ref
        o_ref[0] = (g_ref[0, 0] + s_ref[0]).astype(out_dtype)

    return pl.pallas_call(
        body, name=name,
        grid_spec=pltpu.PrefetchScalarGridSpec(
            num_scalar_prefetch=1, grid=(nk, hr // rb),
            in_specs=[pl.BlockSpec((1, 1, rb, cc), lambda k, i, core_ref: (k, core_ref[0], i, 0)),
                      pl.BlockSpec((1, rb, cc), lambda k, i, core_ref: (k, i, 0))],
            out_specs=pl.BlockSpec((1, rb, cc), lambda k, i, core_ref: (k, i, 0))),
        out_shape=jax.ShapeDtypeStruct((nk, hr, cc), out_dtype),
        compiler_params=_cparams(("parallel", "parallel")),
    )(core, g, sib)


def rs_send(parts):
    n = len(parts)

    def body(*refs):
        ins, outs = refs[:n], refs[n:2 * n]
        send, recv = refs[2 * n:]
        x, y, c = _place()
        waits = []
        for a in range(n):
            for o, (fx, fy) in enumerate(CHIP_FLIPS):
                kk = 2 * _flip(x, fx) + _flip(y, fy)
                cp = pltpu.make_async_remote_copy(
                    src_ref=ins[a].at[kk], dst_ref=outs[a].at[o], send_sem=send.at[3 * a + o], recv_sem=recv.at[3 * a + o],
                    device_id=(_flip(x, fx), _flip(y, fy), c), device_id_type=MESH)
                cp.start()
                waits.append(cp.wait)
        for w in waits:
            w()

    dma = pltpu.SemaphoreType.DMA
    return pl.pallas_call(
        body, name="rs_send", in_specs=[ANY] * n, out_specs=[ANY] * n,
        out_shape=[jax.ShapeDtypeStruct((3,) + p.shape[1:], p.dtype) for p in parts],
        scratch_shapes=[dma((3 * n,)), dma((3 * n,))],
        compiler_params=pltpu.CompilerParams(has_side_effects=True),
    )(*parts)


def rs_sum(recv, part, where, full, layer, n_layers, name):
    _, hr, cc = recv.shape
    rb = _tile(hr, 256)

    def body(*refs):
        r_ref, p_ref, o_ref = refs[1], refs[2], refs[-1]
        o_ref[0, 0] = ((p_ref[0].astype(F32) + r_ref[0].astype(F32)) + r_ref[1].astype(F32)) + r_ref[2].astype(F32)

    in_specs = [pl.BlockSpec((3, rb, cc), lambda i, w_ref: (0, i, 0)),
                pl.BlockSpec((1, rb, cc), lambda i, w_ref: (w_ref[0], i, 0))]
    args = [where, recv, part]
    aliases = {}
    if full is not None:
        in_specs.append(ANY)
        args.append(full)
        aliases = {3: 0}
    return pl.pallas_call(
        body, name=name,
        grid_spec=pltpu.PrefetchScalarGridSpec(
            num_scalar_prefetch=1, grid=(hr // rb,), in_specs=in_specs,
            out_specs=pl.BlockSpec((1, 1, rb, cc), lambda i, w_ref: (layer, w_ref[1], i, 0))),
        out_shape=jax.ShapeDtypeStruct((n_layers, 2, hr, cc), F32),
        input_output_aliases=aliases,
        compiler_params=_cparams(("parallel",)),
    )(*args)


def rs_share(fulls):
    n = len(fulls)

    def body(*refs):
        ins, outs = refs[:n], refs[n:2 * n]
        send, recv = refs[2 * n:]
        x, y, c = _place()
        cps = []
        for a in range(n):
            cp = pltpu.make_async_remote_copy(
                src_ref=ins[a].at[:, c], dst_ref=outs[a].at[:, c], send_sem=send.at[a], recv_sem=recv.at[a],
                device_id=(x, y, 1 - c), device_id_type=MESH)
            cp.start()
            cps.append(cp)
        for a in range(n):
            got = outs[a].at[:, 1 - c]
            pltpu.make_async_remote_copy(
                src_ref=got, dst_ref=got, send_sem=send.at[a], recv_sem=recv.at[a],
                device_id=(x, y, 1 - c), device_id_type=MESH).wait_recv()
        for cp in cps:
            cp.wait_send()

    dma = pltpu.SemaphoreType.DMA
    return pl.pallas_call(
        body, name="rs_share", in_specs=[ANY] * n, out_specs=[ANY] * n,
        out_shape=[jax.ShapeDtypeStruct(f.shape, f.dtype) for f in fulls],
        scratch_shapes=[dma((n,)), dma((n,))],
        input_output_aliases={i: i for i in range(n)},
        compiler_params=pltpu.CompilerParams(has_side_effects=True),
    )(*fulls)


def allreduce_small(v):
    r, w = v.shape

    def body(v_ref, o_ref, buf, send, recv, loc):
        x, y, c = _place()
        me = 4 * x + 2 * y + c
        mine = pltpu.make_async_copy(v_ref, buf.at[me], loc)
        mine.start()
        cps = []
        for o in range(1, N_DEV):
            fx, fy, fc = (o >> 2) & 1, (o >> 1) & 1, o & 1
            cp = pltpu.make_async_remote_copy(
                src_ref=v_ref, dst_ref=buf.at[me], send_sem=send.at[o - 1], recv_sem=recv.at[o - 1],
                device_id=(_flip(x, fx), _flip(y, fy), _flip(c, fc)), device_id_type=MESH)
            cp.start()
            cps.append(cp)
        for o in range(1, N_DEV):
            fx, fy, fc = (o >> 2) & 1, (o >> 1) & 1, o & 1
            peer = 4 * _flip(x, fx) + 2 * _flip(y, fy) + _flip(c, fc)
            pltpu.make_async_remote_copy(
                src_ref=v_ref, dst_ref=buf.at[peer], send_sem=send.at[o - 1], recv_sem=recv.at[o - 1],
                device_id=(x, y, c), device_id_type=MESH).wait_recv()
        for cp in cps:
            cp.wait_send()
        mine.wait()
        acc = buf[0]
        for d in range(1, N_DEV):
            acc = acc + buf[d]
        o_ref[...] = acc

    dma = pltpu.SemaphoreType.DMA
    vm = pl.BlockSpec(memory_space=pltpu.VMEM)
    return pl.pallas_call(
        body, name="allreduce_small", in_specs=[vm], out_specs=vm,
        out_shape=jax.ShapeDtypeStruct((r, w), F32),
        scratch_shapes=[pltpu.VMEM((N_DEV, r, w), F32), dma((N_DEV - 1,)), dma((N_DEV - 1,)), dma],
        compiler_params=pltpu.CompilerParams(has_side_effects=True, vmem_limit_bytes=VMEM_LIMIT),
    )(v)


def adamw(w, g, m, v, name):
    r, cc = w.shape
    rb = _tile(r, 256)

    def body(w_ref, g_ref, m_ref, v_ref, d_ref, nm_ref, nv_ref):
        gv = g_ref[...]
        nm = ADAM_B1 * m_ref[...] + (1.0 - ADAM_B1) * gv
        nv = ADAM_B2 * v_ref[...] + (1.0 - ADAM_B2) * (gv * gv)
        m_hat = nm / (1.0 - ADAM_B1 ** ADAM_STEP)
        v_hat = nv / (1.0 - ADAM_B2 ** ADAM_STEP)
        d_ref[...] = -ADAM_LR * (m_hat / (jnp.sqrt(v_hat) + ADAM_EPS) + ADAM_WD * w_ref[...])
        nm_ref[...] = nm
        nv_ref[...] = nv

    blk = pl.BlockSpec((rb, cc), lambda i: (i, 0))
    shp = jax.ShapeDtypeStruct((r, cc), F32)
    return pl.pallas_call(
        body, name=name, grid=(r // rb,), in_specs=[blk] * 4, out_specs=[blk] * 3, out_shape=[shp] * 3,
        compiler_params=_cparams(("parallel",)),
    )(w, g, m, v)


WEIGHTS = ['g_ffn1', 'w_ffn1_gate', 'w_ffn1_up', 'w_ffn1_down', 'g_mix', 'w_in_ab', 'conv_w', 'conv_b', 'ln_a_g',
           'ln_a_b', 'ln_v_g', 'ln_v_b', 'sp_w', 'sp_b', 'w_out_ab', 'w_qkv', 'w_o', 'g_ffn2', 'w_ffn2_gate',
           'w_ffn2_up', 'w_ffn2_down', 'g_final']
BIG = ['w_ffn1_gate', 'w_ffn1_up', 'w_ffn1_down', 'w_in_ab', 'w_out_ab', 'w_qkv', 'w_o', 'w_ffn2_gate', 'w_ffn2_up',
       'w_ffn2_down']
SMALL = ['g_ffn1', 'g_mix', 'g_ffn2', 'g_final', 'conv_b', 'ln_a_g', 'ln_a_b', 'ln_v_g', 'ln_v_b', 'sp_b', 'sp_w']


def _rows(a):
    return a.reshape(-1, LANES)


def _pack(parts):
    v = jnp.concatenate([_rows(p) for p in parts], axis=0)
    pad = (-v.shape[0]) % 8
    return jnp.pad(v, ((0, pad), (0, 0)))


def _unpack(v, shapes):
    out, r = [], 0
    for s in shapes:
        n = 1
        for d in s:
            n *= d
        n //= LANES
        out.append(v[r:r + n].reshape(s))
        r += n
    return out


def kernel(x, g_ffn1, w_ffn1_gate, w_ffn1_up, w_ffn1_down, g_mix, w_in_ab, conv_w, conv_b, ln_a_g, ln_a_b, ln_v_g, ln_v_b, sp_w, sp_b, w_out_ab, w_qkv, w_o, g_ffn2, w_ffn2_gate, w_ffn2_up, w_ffn2_down, g_final, loss_target, m_g_ffn1, m_w_ffn1_gate, m_w_ffn1_up, m_w_ffn1_down, m_g_mix, m_w_in_ab, m_conv_w, m_conv_b, m_ln_a_g, m_ln_a_b, m_ln_v_g, m_ln_v_b, m_sp_w, m_sp_b, m_w_out_ab, m_w_qkv, m_w_o, m_g_ffn2, m_w_ffn2_gate, m_w_ffn2_up, m_w_ffn2_down, m_g_final, v_g_ffn1, v_w_ffn1_gate, v_w_ffn1_up, v_w_ffn1_down, v_g_mix, v_w_in_ab, v_conv_w, v_conv_b, v_ln_a_g, v_ln_a_b, v_ln_v_g, v_ln_v_b, v_sp_w, v_sp_b, v_w_out_ab, v_w_qkv, v_w_o, v_g_ffn2, v_w_ffn2_gate, v_w_ffn2_up, v_w_ffn2_down, v_g_final):
    p = dict(locals())
    n_seq, seq, d = x.shape
    t = n_seq * seq
    depth = g_ffn1.shape[0]
    core = lax.axis_index("c")
    chip = 2 * lax.axis_index("x") + lax.axis_index("y")
    xf = x.reshape(t, d)
    target = loss_target.reshape(t, d)

    items = []
    for name in BIG:
        for layer in range(p[name].shape[0]):
            items.append((name, layer))
    chip1 = chip.reshape(1).astype(jnp.int32)
    shards = [place_shard(p[name][layer], chip1, BF16, "place_shard") for name, layer in items]
    gathered, (conv_w4,) = allgather_weights(shards, [place_shard(conv_w[0], chip1, F32, "place_conv_w")])
    wt = {it: g for it, g in zip(items, gathered)}
    c_mix = conv_w4.shape[2] * N_CHIPS
    conv_full = jnp.transpose(conv_w4, (1, 0, 2)).reshape(CONV_WIDTH, c_mix)
    vec = lambda a: a.reshape(1, -1)
    sp_bt = sp_b[0].T
    sp_wt = jnp.transpose(sp_w[0], (0, 2, 1))
    d_ff = w_ffn1_gate.shape[2]
    n_in = w_in_ab.shape[2]
    n_qkv = w_qkv.shape[2] // 3

    saved = []
    xc = xf
    for layer in range(depth):
        s = {}
        for half, (gn, wn) in enumerate((('g_ffn1', 'w_ffn1'), ('g_ffn2', 'w_ffn2'))):
            if half == 1:
                s['x_mix'] = xc
                s['h_mix'] = rmsnorm_fwd(xc, vec(g_mix[layer]), "norm_mix")
                if layer % 2 == 0:
                    (z,) = colmm(s['h_mix'], [wt[('w_in_ab', layer // 2)]], n_in, BF16, "mm_in")
                    cat, a1 = mix_fwd(z, conv_full, conv_b, ln_a_g, ln_a_b, vec(ln_v_g), vec(ln_v_b), sp_w[0], sp_bt, seq)
                    s.update(z=z, cat=cat, a1=a1)
                    xc = rowmm([cat], wt[('w_out_ab', layer // 2)], xc, 1.0, "mm_out")
                else:
                    (qkv,) = colmm(s['h_mix'], [wt[('w_qkv', layer // 2)]], n_qkv, BF16, "mm_qkv")
                    o, tot, cnt = attn_fwd(qkv, n_seq, seq)
                    s.update(qkv=qkv, o=o, tot=tot, cnt=cnt)
                    xc = rowmm([o], wt[('w_o', layer // 2)], xc, 1.0, "mm_o")
            s['x' + wn] = xc
            h = rmsnorm_fwd(xc, vec(p[gn][layer]), "norm_ffn")
            gate, up = colmm(h, [wt[(wn + '_gate', layer)], wt[(wn + '_up', layer)]], d_ff, BF16, "ffn_gateup")
            xc = rowmm([gate, up], wt[(wn + '_down', layer)], xc, 0.5, "ffn_down")
            s.update({'h' + wn: h, 'gate' + wn: gate, 'up' + wn: up})
        saved.append(s)

    loss8, dx, dxb, dg_final = loss_head(xc, vec(g_final), target)
    loss = lax.psum(loss8[0, 0], ("x", "y", "c"))

    gw = {}
    gs = {}
    for layer in reversed(range(depth)):
        s = saved[layer]
        for half, (gn, wn) in reversed(list(enumerate((('g_ffn1', 'w_ffn1'), ('g_ffn2', 'w_ffn2'))))):
            wd = wt[(wn + '_down', layer)]
            dgate, dup, act = rowmm_t(dxb, wd, 0.5, BF16, "ffn_bwd_act", gu=(s['gate' + wn], s['up' + wn]))
            gw[(wn + '_down', layer)] = dw_row(act, dxb, 0.5, "ffn_dw_down")
            gw[(wn + '_gate', layer)], gw[(wn + '_up', layer)] = dw_col(s['h' + wn], [dgate, dup], N_CHIPS, d_ff, "ffn_dw_gateup")
            dx, dxb, dg = colmm_t([dgate, dup], [wt[(wn + '_gate', layer)], wt[(wn + '_up', layer)]], d_ff,
                                  s['x' + wn], vec(p[gn][layer]), dx, "ffn_bwd_in")
            gs[(gn, layer)] = dg
            if half == 1:
                if layer % 2 == 0:
                    i = layer // 2
                    w_out = wt[('w_out_ab', i)]
                    dcat = rowmm_t(dxb, w_out, 1.0, F32, "mm_out_t")
                    gw[('w_out_ab', i)] = dw_row(s['cat'], dxb, 1.0, "dw_out")
                    dz, da1, dcb, dlag, dlab, dlvg, dlvb, dspw, dspb = mix_bwd_point(
                        dcat, s['z'], s['a1'], ln_a_g, ln_a_b, vec(ln_v_g), vec(ln_v_b), sp_w[0], sp_wt, sp_bt, seq)
                    dz, dcw = mix_bwd_conv(dz, da1, s['z'], conv_full, seq)
                    gs.update({('conv_b', i): dcb, ('ln_a_g', i): dlag, ('ln_a_b', i): dlab, ('ln_v_g', i): dlvg,
                               ('ln_v_b', i): dlvb, ('sp_w', i): dspw, ('sp_b', i): dspb[:, :, 0], ('conv_w', i): dcw})
                    (gw[('w_in_ab', i)],) = dw_col(s['h_mix'], [dz], N_CHIPS, n_in, "dw_in")
                    dx, dxb, dg = colmm_t([dz], [wt[('w_in_ab', i)]], n_in, s['x_mix'], vec(g_mix[layer]), dx, "mm_in_t")
                else:
                    i = layer // 2
                    w_o4 = wt[('w_o', i)]
                    do = rowmm_t(dxb, w_o4, 1.0, BF16, "mm_o_t")
                    gw[('w_o', i)] = dw_row(s['o'], dxb, 1.0, "dw_o")
                    dq, dk, dv = attn_bwd(s['qkv'], do, s['tot'], s['cnt'], n_seq, seq)
                    dqkv = jnp.concatenate([dq, dk, dv], axis=0)
                    (gw[('w_qkv', i)],) = dw_col(s['h_mix'], [dqkv], N_CHIPS, n_qkv, "dw_qkv")
                    dx, dxb, dg = colmm_t([dqkv], [wt[('w_qkv', i)]], n_qkv, s['x_mix'], vec(g_mix[layer]), dx, "mm_qkv_t")
                gs[('g_mix', layer)] = dg
    grad_x = dx.reshape(x.shape)

    core1 = core.reshape(1).astype(jnp.int32)
    g4 = [gw[it].reshape(N_CHIPS, 2, gw[it].shape[1] // 2, gw[it].shape[2]) for it in items]
    sib = rs_exchange(g4)
    parts = [rs_add(g, sb, core1, REDUCE_DTYPE, "rs_add") for g, sb in zip(g4, sib)]
    recv = dict(zip(items, rs_send(parts)))
    part = dict(zip(items, parts))
    where = jnp.stack([chip, core]).astype(jnp.int32)
    fulls = []
    for name in BIG:
        full = None
        n_layers = p[name].shape[0]
        for layer in range(n_layers):
            full = rs_sum(recv[(name, layer)], part[(name, layer)], where, full, layer, n_layers, "rs_sum")
        fulls.append(full)
    shared = rs_share(fulls)
    grads = {name: sh.reshape(p[name].shape) for name, sh in zip(BIG, shared)}

    stack = lambda name: jnp.concatenate([gs[(name, layer)].reshape((1,) + p[name].shape[1:]) for layer in range(p[name].shape[0])], axis=0)
    small_g = [stack(name) if name != 'g_final' else dg_final.reshape(p[name].shape) for name in SMALL]
    packed = _pack(small_g + [gs[('conv_w', 0)]])
    red = allreduce_small(packed)
    outs = _unpack(red, [p[name].shape for name in SMALL] + [(CONV_WIDTH, c_mix)])
    for name, g in zip(SMALL, outs[:-1]):
        grads[name] = g
    conv_g = outs[-1].reshape(CONV_WIDTH, N_CHIPS, c_mix // N_CHIPS)
    grads['conv_w'] = lax.dynamic_index_in_dim(conv_g, chip, axis=1, keepdims=False).reshape(conv_w.shape)

    delta, new_m, new_v = {}, {}, {}
    for name in BIG:
        shp = p[name].shape
        two = lambda a: a.reshape(shp[0] * shp[1], shp[2])
        dl, nm, nv = adamw(two(p[name]), two(grads[name]), two(p['m_' + name]), two(p['v_' + name]), "adamw")
        delta[name], new_m[name], new_v[name] = dl.reshape(shp), nm.reshape(shp), nv.reshape(shp)
    small_names = SMALL + ['conv_w']
    pk = lambda pre: _pack([p[pre + name] for name in small_names])
    dl, nm, nv = adamw(pk(''), _pack([grads[name] for name in small_names]), pk('m_'), pk('v_'), "adamw_small")
    shapes = [p[name].shape for name in small_names]
    for dst, val in ((delta, dl), (new_m, nm), (new_v, nv)):
        for name, a in zip(small_names, _unpack(val, shapes)):
            dst[name] = a

    return (loss, grad_x, *[grads[n] for n in WEIGHTS], *[delta[n] for n in WEIGHTS],
            *[new_m[n] for n in WEIGHTS], *[new_v[n] for n in WEIGHTS])
```

```python
import functools

import jax
import jax.numpy as jnp
from jax import lax
from jax.experimental import pallas as pl
from jax.experimental.pallas import tpu as pltpu

F32 = jnp.float32
BF16 = jnp.bfloat16
EPS = 1e-6
HEAD_DIM = 64
CONV_WIDTH = 31
CHUNK = 128
KBLK = 128
ATT_BLOCK = 256
DW_TOKENS = 2048
STICK_GONE = -110.0
LANES = 128
HALO = 32
ADAM_LR, ADAM_B1, ADAM_B2, ADAM_EPS, ADAM_WD, ADAM_STEP = 0.001, 0.9, 0.999, 1e-08, 0.01, 10
VMEM_LIMIT = 56 * 1024 * 1024
MESH = pl.DeviceIdType.MESH
N_CHIPS = 4
N_DEV = 8
REDUCE_DTYPE = BF16


def _cparams(sem):
    return pltpu.CompilerParams(dimension_semantics=sem, vmem_limit_bytes=VMEM_LIMIT)


def _nt(a, b):
    return lax.dot_general(a, b, (((1,), (1,)), ((), ())), preferred_element_type=F32)


def _tn(a, b):
    return lax.dot_general(a, b, (((0,), (0,)), ((), ())), preferred_element_type=F32)


def _nn(a, b):
    return jnp.dot(a, b, preferred_element_type=F32)


def _sigmoid(x):
    return 0.5 * jnp.tanh(0.5 * x) + 0.5


def _tile(t, want):
    if t <= want:
        return t
    for cand in range(want - want % 8, 7, -8):
        if t % cand == 0:
            return cand
    raise ValueError((t, want))


def rmsnorm_fwd(x, g, name):
    t, d = x.shape
    tm = _tile(t, 512)

    def body(x_ref, g_ref, h_ref):
        xv = x_ref[...]
        r = lax.rsqrt(jnp.mean(xv * xv, axis=-1, keepdims=True) + EPS)
        h_ref[...] = (xv * r * g_ref[...]).astype(BF16)

    return pl.pallas_call(
        body, name=name, grid=(t // tm,),
        in_specs=[pl.BlockSpec((tm, d), lambda i: (i, 0)), pl.BlockSpec((1, d), lambda i: (0, 0))],
        out_specs=pl.BlockSpec((tm, d), lambda i: (i, 0)),
        out_shape=jax.ShapeDtypeStruct((t, d), BF16),
        compiler_params=_cparams(("parallel",)),
    )(x, g)


def colmm(h, ws, nu, out_dtype, name):
    t, k = h.shape
    j, _, nj = ws[0].shape
    per = nj // nu
    units = j * per
    tm = _tile(t, 512)
    nw = len(ws)

    def body(*refs):
        h_ref = refs[0]
        hv = h_ref[...]
        for n in range(nw):
            res = _nn(hv, refs[1 + n][0]).astype(out_dtype)
            for u in range(per):
                refs[1 + nw + n][u] = res[:, u * nu:(u + 1) * nu]

    w_spec = pl.BlockSpec((1, k, nj), lambda s, i: (s, 0, 0))
    o_spec = pl.BlockSpec((per, tm, nu), lambda s, i: (s, i, 0))
    outs = pl.pallas_call(
        body, name=name, grid=(j, t // tm),
        in_specs=[pl.BlockSpec((tm, k), lambda s, i: (i, 0))] + [w_spec] * nw,
        out_specs=[o_spec] * nw,
        out_shape=[jax.ShapeDtypeStruct((units, t, nu), out_dtype)] * nw,
        compiler_params=_cparams(("parallel", "parallel")),
    )(h, *ws)
    return outs


def rowmm(a_list, w, resid, scale, name):
    swiglu = len(a_list) == 2
    u_n, t, ku = a_list[0].shape
    n = w.shape[2]
    tm = _tile(t, 256)

    def body(*refs):
        a_refs = refs[:len(a_list)]
        w_ref, r_ref, o_ref = refs[len(a_list):]
        acc = jnp.zeros((tm, n), F32)
        for u in range(u_n):
            if swiglu:
                gv = a_refs[0][u].astype(F32)
                av = (gv * _sigmoid(gv) * a_refs[1][u].astype(F32)).astype(BF16)
            else:
                av = a_refs[0][u]
            acc = acc + _nn(av, w_ref[u])
        o_ref[...] = r_ref[...] + scale * acc

    a_spec = pl.BlockSpec((u_n, tm, ku), lambda i: (0, i, 0))
    return pl.pallas_call(
        body, name=name, grid=(t // tm,),
        in_specs=[a_spec] * len(a_list) + [pl.BlockSpec((u_n, ku, n), lambda i: (0, 0, 0)),
                                           pl.BlockSpec((tm, n), lambda i: (i, 0))],
        out_specs=pl.BlockSpec((tm, n), lambda i: (i, 0)),
        out_shape=jax.ShapeDtypeStruct((t, n), F32),
        compiler_params=_cparams(("parallel",)),
    )(*a_list, w, resid)


def rowmm_t(dyb, w, scale, out_dtype, name, gu=None):
    t, n = dyb.shape
    u_n, ku, _ = w.shape
    tm = _tile(t, 512)

    def body(*refs):
        if gu is None:
            dy_ref, w_ref, o_ref = refs
            o_ref[0] = (scale * _nt(dy_ref[...], w_ref[0])).astype(out_dtype)
        else:
            dy_ref, w_ref, g_ref, u_ref, dg_ref, du_ref, a_ref = refs
            dact = scale * _nt(dy_ref[...], w_ref[0])
            gv = g_ref[0].astype(F32)
            uv = u_ref[0].astype(F32)
            s = _sigmoid(gv)
            silu = gv * s
            dg_ref[0] = (dact * uv * (s * (1.0 + gv * (1.0 - s)))).astype(BF16)
            du_ref[0] = (dact * silu).astype(BF16)
            a_ref[0] = (silu * uv).astype(BF16)

    blk = pl.BlockSpec((1, tm, ku), lambda u, i: (u, i, 0))
    in_specs = [pl.BlockSpec((tm, n), lambda u, i: (i, 0)), pl.BlockSpec((1, ku, n), lambda u, i: (u, 0, 0))]
    if gu is None:
        return pl.pallas_call(
            body, name=name, grid=(u_n, t // tm), in_specs=in_specs, out_specs=blk,
            out_shape=jax.ShapeDtypeStruct((u_n, t, ku), out_dtype),
            compiler_params=_cparams(("parallel", "parallel")),
        )(dyb, w)
    return pl.pallas_call(
        body, name=name, grid=(u_n, t // tm), in_specs=in_specs + [blk, blk], out_specs=[blk] * 3,
        out_shape=[jax.ShapeDtypeStruct((u_n, t, ku), BF16)] * 3,
        compiler_params=_cparams(("parallel", "parallel")),
    )(dyb, w, *gu)


def colmm_t(dzs, ws, nu, x, g, dy_in, name):
    t, k = x.shape
    j, _, nj = ws[0].shape
    per = nj // nu
    units = j * per
    nw = len(ws)
    tm = _tile(t, 256)

    def body(*refs):
        dz_refs = refs[:nw]
        w_refs = refs[nw:2 * nw]
        x_ref, g_ref, dy_ref, dx_ref, dxb_ref, dg_ref = refs[2 * nw:]
        i = pl.program_id(0)
        dh = jnp.zeros((tm, k), F32)
        for n in range(nw):
            for u in range(units):
                wv = w_refs[n][u // per, :, (u % per) * nu:(u % per + 1) * nu]
                dh = dh + _nt(dz_refs[n][u], wv)
        xv = x_ref[...]
        gv = g_ref[...]
        r = lax.rsqrt(jnp.mean(xv * xv, axis=-1, keepdims=True) + EPS)
        uu = dh * gv
        dx = dy_ref[...] + r * uu - xv * (r * r * r * jnp.mean(uu * xv, axis=-1, keepdims=True))
        dx_ref[...] = dx
        dxb_ref[...] = dx.astype(BF16)
        part = jnp.sum(dh * (xv * r), axis=0, keepdims=True)

        @pl.when(i == 0)
        def _():
            dg_ref[...] = part

        @pl.when(i > 0)
        def _():
            dg_ref[...] += part

    dz_spec = pl.BlockSpec((units, tm, nu), lambda i: (0, i, 0))
    w_spec = pl.BlockSpec((j, k, nj), lambda i: (0, 0, 0))
    row = pl.BlockSpec((tm, k), lambda i: (i, 0))
    vec = pl.BlockSpec((1, k), lambda i: (0, 0))
    return pl.pallas_call(
        body, name=name, grid=(t // tm,),
        in_specs=[dz_spec] * nw + [w_spec] * nw + [row, vec, row],
        out_specs=[row, row, vec],
        out_shape=[jax.ShapeDtypeStruct((t, k), F32), jax.ShapeDtypeStruct((t, k), BF16),
                   jax.ShapeDtypeStruct((1, k), F32)],
        compiler_params=_cparams(("arbitrary",)),
    )(*dzs, *ws, x, g, dy_in)


def dw_col(h, dzs, j, nu, name):
    t, k = h.shape
    units = dzs[0].shape[0]
    per = units // j
    nw = len(dzs)
    tt = _tile(t, DW_TOKENS)

    def body(*refs):
        h_ref = refs[0]
        s = pl.program_id(1)
        hv = h_ref[...]
        for n in range(nw):
            o_ref = refs[1 + nw + n]
            for u in range(per):
                part = _tn(hv, refs[1 + n][u])
                cols = slice(u * nu, (u + 1) * nu)

                @pl.when(s == 0)
                def _():
                    o_ref[0, :, cols] = part

                @pl.when(s > 0)
                def _():
                    o_ref[0, :, cols] += part

    return pl.pallas_call(
        body, name=name, grid=(j, t // tt),
        in_specs=[pl.BlockSpec((tt, k), lambda u, s: (s, 0))] + [pl.BlockSpec((per, tt, nu), lambda u, s: (u, s, 0))] * nw,
        out_specs=[pl.BlockSpec((1, k, per * nu), lambda u, s: (u, 0, 0))] * nw,
        out_shape=[jax.ShapeDtypeStruct((j, k, per * nu), F32)] * nw,
        compiler_params=_cparams(("parallel", "arbitrary")),
    )(h, *dzs)


def dw_row(a, dyb, scale, name):
    u_n, t, ku = a.shape
    n = dyb.shape[1]
    tt = _tile(t, DW_TOKENS)

    def body(a_ref, dy_ref, o_ref):
        s = pl.program_id(1)
        part = scale * _tn(a_ref[0], dy_ref[...])

        @pl.when(s == 0)
        def _():
            o_ref[0] = part

        @pl.when(s > 0)
        def _():
            o_ref[0] += part

    return pl.pallas_call(
        body, name=name, grid=(u_n, t // tt),
        in_specs=[pl.BlockSpec((1, tt, ku), lambda u, s: (u, s, 0)), pl.BlockSpec((tt, n), lambda u, s: (s, 0))],
        out_specs=pl.BlockSpec((1, ku, n), lambda u, s: (u, 0, 0)),
        out_shape=jax.ShapeDtypeStruct((u_n, ku, n), F32),
        compiler_params=_cparams(("parallel", "arbitrary")),
    )(a, dyb)


def loss_head(x, g, target):
    t, d = x.shape
    tm = _tile(t, 256)

    def body(x_ref, g_ref, t_ref, loss_ref, dx_ref, dxb_ref, dg_ref):
        i = pl.program_id(0)
        xv = x_ref[...]
        gv = g_ref[...]
        r = lax.rsqrt(jnp.mean(xv * xv, axis=-1, keepdims=True) + EPS)
        xh = xv * r
        err = xh * gv - t_ref[...]
        dy = err * (1.0 / d)
        uu = dy * gv
        dx = r * uu - xv * (r * r * r * jnp.mean(uu * xv, axis=-1, keepdims=True))
        dx_ref[...] = dx
        dxb_ref[...] = dx.astype(BF16)
        dg_part = jnp.sum(dy * xh, axis=0, keepdims=True)
        row = jnp.sum(err * err, axis=-1, keepdims=True) * (0.5 / d)
        l_part = jnp.zeros((8, LANES), F32) + jnp.sum(row, axis=0, keepdims=True)

        @pl.when(i == 0)
        def _():
            dg_ref[...] = dg_part
            loss_ref[...] = l_part

        @pl.when(i > 0)
        def _():
            dg_ref[...] += dg_part
            loss_ref[...] += l_part

    row = pl.BlockSpec((tm, d), lambda i: (i, 0))
    vec = pl.BlockSpec((1, d), lambda i: (0, 0))
    return pl.pallas_call(
        body, name="loss_head", grid=(t // tm,),
        in_specs=[row, vec, row],
        out_specs=[pl.BlockSpec((8, LANES), lambda i: (0, 0)), row, row, vec],
        out_shape=[jax.ShapeDtypeStruct((8, LANES), F32), jax.ShapeDtypeStruct((t, d), F32),
                   jax.ShapeDtypeStruct((t, d), BF16), jax.ShapeDtypeStruct((1, d), F32)],
        compiler_params=_cparams(("arbitrary",)),
    )(x, g, target)


def _split(v):
    hi = v.astype(BF16)
    lo = (v - hi.astype(F32)).astype(BF16)
    return hi, lo


def _keysums(v, m_ext):
    hi, lo = _split(v)
    outs = []
    for j in range(v.shape[1] // KBLK):
        sl = slice(j * KBLK, (j + 1) * KBLK)
        cs = _nn(jnp.concatenate([hi[:, sl], lo[:, sl]], axis=1), m_ext)
        outs.append((cs[:, :KBLK], cs[:, KBLK:]))
    return outs


def _softplus_parts(z):
    sp = jnp.maximum(z, 0.0) + jnp.log(1.0 + jnp.exp(-jnp.abs(z)))
    return sp, z - sp


def _sum_matrices():
    r = lax.broadcasted_iota(jnp.int32, (2 * KBLK, 2 * KBLK), 0) % KBLK
    c = lax.broadcasted_iota(jnp.int32, (2 * KBLK, 2 * KBLK), 1)
    suffix = jnp.where((r > c) | (c >= KBLK), 1.0, 0.0).astype(BF16)
    prefix = jnp.where((r <= c) | (c >= KBLK), 1.0, 0.0).astype(BF16)
    return suffix, prefix


def attn_fwd(qkv, n_seq, seq):
    t = qkv.shape[1]
    n_pairs = (qkv.shape[0] // 3) * 2
    bq = min(ATT_BLOCK, seq)
    nq = seq // bq
    nsub = bq // KBLK
    suffix_m, _ = _sum_matrices()

    def body(q_ref, k_ref, v_ref, m_ref, o_ref, tot_ref, cnt_ref):
        qi = pl.program_id(2)
        step_id = (pl.program_id(0) * n_pairs + pl.program_id(1)) * nq + qi
        lane = lax.broadcasted_iota(jnp.int32, (bq, LANES), 1)
        is_a = lane < HEAD_DIM
        q2 = q_ref[0] * jnp.asarray(HEAD_DIM ** -0.5, BF16)
        qs = (jnp.where(is_a, q2, jnp.zeros_like(q2)), jnp.where(is_a, jnp.zeros_like(q2), q2))
        m_ext = m_ref[...]
        row = lax.broadcasted_iota(jnp.int32, (bq, bq), 0)
        col = lax.broadcasted_iota(jnp.int32, (bq, bq), 1)
        diag_mask = col < row

        def block(kj, carry, mask):
            off = pl.multiple_of(kj * bq, bq)
            k2 = k_ref[0, pl.ds(off, bq), :]
            v2 = v_ref[0, pl.ds(off, bq), :]
            out = []
            for h in range(2):
                rem, acc = carry[h]
                z = _nt(qs[h], k2)
                sp, ls = _softplus_parts(z)
                lk = -sp if mask is None else jnp.where(mask, -sp, 0.0)
                sums = _keysums(lk, m_ext)
                parts = [None] * nsub
                for j in reversed(range(nsub)):
                    suf, total = sums[j]
                    parts[j] = jnp.exp(ls[:, j * KBLK:(j + 1) * KBLK] + suf + rem)
                    rem = rem + total
                a = jnp.concatenate(parts, axis=1)
                if mask is not None:
                    a = jnp.where(mask, a, 0.0)
                out.append((rem, acc + _nn(a.astype(BF16), v2)))
            return tuple(out)

        def most_left(c):
            return jnp.maximum(jnp.max(c[0][0]), jnp.max(c[1][0]))

        def more(s):
            return (s[0] < qi) & (s[1] > STICK_GONE)

        def step(s):
            c = block(qi - 1 - s[0], s[2], None)
            return s[0] + 1, most_left(c), c

        zero = jnp.zeros((bq, LANES), F32)
        carry = block(qi, ((zero, zero), (zero, zero)), diag_mask)
        n_left, _, carry = lax.while_loop(more, step, (jnp.int32(0), most_left(carry), carry))
        o_ref[0] = jnp.where(is_a, carry[0][1], carry[1][1]).astype(BF16)
        tot_ref[...] = jnp.where(is_a, carry[0][0], carry[1][0])
        cnt_ref[step_id] = n_left.astype(F32)

    upp = qkv.shape[0] // 3
    return pl.pallas_call(
        body, name="attn_fwd", grid=(n_seq, n_pairs, nq),
        in_specs=[pl.BlockSpec((1, bq, LANES), lambda b, p, i: (p // 2, b * nq + i, p % 2)),
                  pl.BlockSpec((1, seq, LANES), lambda b, p, i: (upp + p // 2, b, p % 2)),
                  pl.BlockSpec((1, seq, LANES), lambda b, p, i: (2 * upp + p // 2, b, p % 2)),
                  pl.BlockSpec((2 * KBLK, 2 * KBLK), lambda b, p, i: (0, 0))],
        out_specs=[pl.BlockSpec((1, bq, LANES), lambda b, p, i: (p // 2, b * nq + i, p % 2)),
                   pl.BlockSpec((bq, LANES), lambda b, p, i: (b * nq + i, p)),
                   pl.BlockSpec(memory_space=pltpu.SMEM)],
        out_shape=[jax.ShapeDtypeStruct((upp, t, 2 * LANES), BF16), jax.ShapeDtypeStruct((t, n_pairs * LANES), F32),
                   jax.ShapeDtypeStruct((n_seq * n_pairs * nq,), F32)],
        compiler_params=_cparams(("arbitrary", "arbitrary", "arbitrary")),
    )(qkv, qkv, qkv, suffix_m)


def attn_bwd(qkv, do, tot, cnt, n_seq, seq):
    t = qkv.shape[1]
    upp = qkv.shape[0] // 3
    n_pairs = upp * 2
    bq = min(ATT_BLOCK, seq)
    nq = seq // bq
    nsub = bq // KBLK
    _, prefix_m = _sum_matrices()
    scale = HEAD_DIM ** -0.5

    def body(q_ref, k_ref, v_ref, do_ref, tot_ref, m_ref, cnt_ref, dq_ref, dk_ref, dv_ref, dk_acc, dv_acc):
        qi = pl.program_id(2)
        step_id = (pl.program_id(0) * n_pairs + pl.program_id(1)) * nq + qi
        n_left = jnp.clip(cnt_ref[step_id].astype(jnp.int32), 0, qi)
        lane = lax.broadcasted_iota(jnp.int32, (bq, LANES), 1)
        is_a = lane < HEAD_DIM

        def halves(v2):
            z2 = jnp.zeros_like(v2)
            return jnp.where(is_a, v2, z2), jnp.where(is_a, z2, v2)

        qs = halves(q_ref[0] * jnp.asarray(scale, BF16))
        dos = halves(do_ref[0])
        tot2 = tot_ref[...]
        swapped = pltpu.roll(tot2, HEAD_DIM, 1)
        tots = (jnp.where(is_a, tot2, swapped), jnp.where(is_a, swapped, tot2))
        m_ext = m_ref[...]
        row = lax.broadcasted_iota(jnp.int32, (bq, bq), 0)
        col = lax.broadcasted_iota(jnp.int32, (bq, bq), 1)
        diag_mask = col < row

        @pl.when(qi == 0)
        def _():
            dk_acc[...] = jnp.zeros_like(dk_acc)
            dv_acc[...] = jnp.zeros_like(dv_acc)

        def block(kj, carry, mask):
            off = pl.multiple_of(kj * bq, bq)
            k2 = k_ref[0, pl.ds(off, bq), :]
            v2 = v_ref[0, pl.ds(off, bq), :]
            ks = halves(k2)
            dq = carry[2]
            dk_part = jnp.zeros((bq, LANES), F32)
            dv_part = jnp.zeros((bq, LANES), F32)
            out = []
            for h in range(2):
                pre, gpre = carry[h]
                z = _nt(qs[h], k2)
                sp, ls = _softplus_parts(z)
                lk = -sp if mask is None else jnp.where(mask, -sp, 0.0)
                sums = _keysums(lk, m_ext)
                parts = []
                for j in range(nsub):
                    pin, ptot = sums[j]
                    parts.append(jnp.exp(ls[:, j * KBLK:(j + 1) * KBLK] + (tots[h] - (pre + pin))))
                    pre = pre + ptot
                a = jnp.concatenate(parts, axis=1)
                if mask is not None:
                    a = jnp.where(mask, a, 0.0)
                g = a * _nt(dos[h], v2)
                gsums = _keysums(g, m_ext)
                parts = []
                for j in range(nsub):
                    gin, gtot = gsums[j]
                    parts.append(gpre + gin)
                    gpre = gpre + gtot
                dz = g - jnp.exp(ls) * jnp.concatenate(parts, axis=1)
                if mask is not None:
                    dz = jnp.where(mask, dz, 0.0)
                dzb = dz.astype(BF16)
                dq = dq + _nn(dzb, ks[h])
                dk_part = dk_part + _tn(dzb, qs[h])
                dv_part = dv_part + _tn(a.astype(BF16), dos[h])
                out.append((pre, gpre))
            dk_acc[pl.ds(off, bq), :] += dk_part
            dv_acc[pl.ds(off, bq), :] += dv_part
            return (out[0], out[1], dq)

        zero = jnp.zeros((bq, LANES), F32)
        carry = lax.fori_loop(qi - n_left, qi, lambda kj, c: block(kj, c, None), ((zero, zero), (zero, zero), zero))
        carry = block(qi, carry, diag_mask)
        dq_ref[0] = (carry[2] * scale).astype(BF16)

        @pl.when(qi == nq - 1)
        def _():
            dk_ref[0] = dk_acc[...].astype(BF16)
            dv_ref[0] = dv_acc[...].astype(BF16)

    qblk = lambda b, p, i: (p // 2, b * nq + i, p % 2)
    kv_out = pl.BlockSpec((1, seq, LANES), lambda b, p, i: (p // 2, b, p % 2))
    shp = jax.ShapeDtypeStruct((upp, t, 2 * LANES), BF16)
    return pl.pallas_call(
        body, name="attn_bwd", grid=(n_seq, n_pairs, nq),
        in_specs=[pl.BlockSpec((1, bq, LANES), qblk),
                  pl.BlockSpec((1, seq, LANES), lambda b, p, i: (upp + p // 2, b, p % 2)),
                  pl.BlockSpec((1, seq, LANES), lambda b, p, i: (2 * upp + p // 2, b, p % 2)),
                  pl.BlockSpec((1, bq, LANES), qblk),
                  pl.BlockSpec((bq, LANES), lambda b, p, i: (b * nq + i, p)),
                  pl.BlockSpec((2 * KBLK, 2 * KBLK), lambda b, p, i: (0, 0)),
                  pl.BlockSpec(memory_space=pltpu.SMEM)],
        out_specs=[pl.BlockSpec((1, bq, LANES), qblk), kv_out, kv_out],
        out_shape=[shp, shp, shp],
        scratch_shapes=[pltpu.VMEM((seq, LANES), F32), pltpu.VMEM((seq, LANES), F32)],
        compiler_params=_cparams(("parallel", "parallel", "arbitrary")),
    )(qkv, qkv, qkv, do, tot, prefix_m, cnt)


def _ln_stats(v):
    mu = jnp.mean(v, axis=-1, keepdims=True)
    vc = v - mu
    rstd = lax.rsqrt(jnp.mean(vc * vc, axis=-1, keepdims=True) + EPS)
    return vc * rstd, rstd


def _glu_into(a0_ref, av_ref, ag_ref, hv_ref, hg_ref, first):
    hv = hv_ref[0].astype(F32)
    hg = hg_ref[0].astype(F32)
    a0_ref[0:HALO, :] = jnp.where(first, 0.0, hv * _sigmoid(hg))
    av = av_ref[0].astype(F32)
    ag = ag_ref[0].astype(F32)
    a0_ref[HALO:, :] = av * _sigmoid(ag)


def _tril_mask():
    r = lax.broadcasted_iota(jnp.int32, (CHUNK, CHUNK), 0)
    c = lax.broadcasted_iota(jnp.int32, (CHUNK, CHUNK), 1)
    return c <= r


def mix_fwd(z, conv_w, conv_b, ln_a_g, ln_a_b, ln_v_g, ln_v_b, sp_w, sp_bt, seq):
    _, t, c = z.shape
    tm = _tile(seq, 512)
    tiles_per_seq = seq // tm
    groups = c // LANES
    hb = tm // HALO

    def body(av_ref, ag_ref, u_ref, v_ref, hv_ref, hg_ref, cw_ref, cb_ref, lag_ref, lab_ref, lvg_ref, lvb_ref,
             spw_ref, spb_ref, cat_ref, a1_ref, a0_ref):
        i = pl.program_id(0)
        _glu_into(a0_ref, av_ref, ag_ref, hv_ref, hg_ref, i % tiles_per_seq == 0)
        acc = jnp.zeros((tm, c), F32) + cb_ref[...]
        for k in range(CONV_WIDTH):
            acc = acc + cw_ref[k:k + 1, :] * a0_ref[pl.ds(HALO - (CONV_WIDTH - 1) + k, tm), :]
        a1_ref[...] = acc
        xh, _ = _ln_stats(acc)
        a2 = xh * lag_ref[...] + lab_ref[...]
        a3 = (a2 * _sigmoid(a2)).astype(BF16)
        half = c // 2
        cat_ref[0] = a3[:, :half]
        cat_ref[1] = a3[:, half:]
        tril = _tril_mask()
        for g in range(groups):
            sl = slice(g * LANES, (g + 1) * LANES)
            xh, _ = _ln_stats(v_ref[0][:, sl].astype(F32))
            vn = (xh * lvg_ref[:, sl] + lvb_ref[:, sl]).astype(BF16)
            w = jnp.where(tril, spw_ref[g], 0.0).astype(BF16)
            bias = spb_ref[:, g:g + 1]
            for ch in range(tm // CHUNK):
                rows = slice(ch * CHUNK, (ch + 1) * CHUNK)
                vs = _nn(w, vn[rows]) + bias
                bo = (u_ref[0][rows, sl].astype(F32) * vs).astype(BF16)
                cat_ref[2 + (g * LANES) // half, rows, (g * LANES) % half:(g * LANES) % half + LANES] = bo

    unit = lambda u: pl.BlockSpec((1, tm, c), lambda i: (u, i, 0))
    halo = lambda u: pl.BlockSpec((1, HALO, c), lambda i: (u, jnp.maximum(i * hb - 1, 0), 0))
    vec = pl.BlockSpec((1, c), lambda i: (0, 0))
    return pl.pallas_call(
        body, name="mix_fwd", grid=(t // tm,),
        in_specs=[unit(0), unit(1), unit(2), unit(3), halo(0), halo(1),
                  pl.BlockSpec((CONV_WIDTH, c), lambda i: (0, 0)), vec, vec, vec, vec, vec,
                  pl.BlockSpec((groups, CHUNK, CHUNK), lambda i: (0, 0, 0)),
                  pl.BlockSpec((CHUNK, groups), lambda i: (0, 0))],
        out_specs=[pl.BlockSpec((4, tm, c // 2), lambda i: (0, i, 0)), pl.BlockSpec((tm, c), lambda i: (i, 0))],
        out_shape=[jax.ShapeDtypeStruct((4, t, c // 2), BF16), jax.ShapeDtypeStruct((t, c), F32)],
        scratch_shapes=[pltpu.VMEM((HALO + tm, c), F32)],
        compiler_params=_cparams(("parallel",)),
    )(z, z, z, z, z, z, conv_w, conv_b, ln_a_g, ln_a_b, ln_v_g, ln_v_b, sp_w, sp_bt)


def mix_bwd_point(dcat, z, a1, ln_a_g, ln_a_b, ln_v_g, ln_v_b, sp_w, sp_wt, sp_bt, seq):
    _, t, c = z.shape
    tm = _tile(seq, 512)
    groups = c // LANES
    half = c // 2

    def body(dc_ref, u_ref, v_ref, a1_ref, lag_ref, lab_ref, lvg_ref, lvb_ref, spw_ref, spwt_ref, spb_ref,
             dz_ref, da1_ref, dcb_ref, dlag_ref, dlab_ref, dlvg_ref, dlvb_ref, dspw_ref, dspb_ref):
        i = pl.program_id(0)
        last = pl.num_programs(0) - 1

        @pl.when(i == 0)
        def _():
            for r in (dcb_ref, dlag_ref, dlab_ref, dlvg_ref, dlvb_ref, dspw_ref, dspb_ref):
                r[...] = jnp.zeros_like(r)

        da3 = jnp.concatenate([dc_ref[0], dc_ref[1]], axis=-1)
        xh, rstd = _ln_stats(a1_ref[...])
        a2 = xh * lag_ref[...] + lab_ref[...]
        s = _sigmoid(a2)
        da2 = da3 * (s * (1.0 + a2 * (1.0 - s)))
        dlag_ref[...] += jnp.sum(da2 * xh, axis=0, keepdims=True)
        dlab_ref[...] += jnp.sum(da2, axis=0, keepdims=True)
        dxh = da2 * lag_ref[...]
        da1 = rstd * (dxh - jnp.mean(dxh, axis=-1, keepdims=True) - xh * jnp.mean(dxh * xh, axis=-1, keepdims=True))
        da1_ref[...] = da1
        dcb_ref[...] += jnp.sum(da1, axis=0, keepdims=True)

        tril = _tril_mask()
        for g in range(groups):
            sl = slice(g * LANES, (g + 1) * LANES)
            xh, rstd = _ln_stats(v_ref[0][:, sl].astype(F32))
            lg = lvg_ref[:, sl]
            vnb = (xh * lg + lvb_ref[:, sl]).astype(BF16)
            w = jnp.where(tril, spw_ref[g], 0.0).astype(BF16)
            wt = jnp.where(tril.T, spwt_ref[g], 0.0).astype(BF16)
            bias = spb_ref[:, g:g + 1]
            dbo_all = dc_ref[2 + (g * LANES) // half][:, (g * LANES) % half:(g * LANES) % half + LANES]
            dvn_parts = []
            dw_acc = jnp.zeros((CHUNK, CHUNK), F32)
            db_acc = jnp.zeros((CHUNK, LANES), F32)
            for ch in range(tm // CHUNK):
                rows = slice(ch * CHUNK, (ch + 1) * CHUNK)
                vs = _nn(w, vnb[rows]) + bias
                dbo = dbo_all[rows]
                uv = u_ref[0][rows, sl].astype(F32)
                dz_ref[0, rows, sl] = (dbo * vs).astype(BF16)
                dvs = dbo * uv
                dvsb = dvs.astype(BF16)
                dvn_parts.append(_nn(wt, dvsb))
                dw_acc = dw_acc + _nt(dvsb, vnb[rows])
                db_acc = db_acc + dvs
            dvn = jnp.concatenate(dvn_parts, axis=0)
            dspw_ref[g] += jnp.where(tril, dw_acc, 0.0)
            dspb_ref[g] += db_acc
            dlvg_ref[:, sl] += jnp.sum(dvn * xh, axis=0, keepdims=True)
            dlvb_ref[:, sl] += jnp.sum(dvn, axis=0, keepdims=True)
            dxh = dvn * lg
            dv = rstd * (dxh - jnp.mean(dxh, axis=-1, keepdims=True) - xh * jnp.mean(dxh * xh, axis=-1, keepdims=True))
            dz_ref[1, :, sl] = dv.astype(BF16)

        @pl.when(i == last)
        def _():
            for g in range(groups):
                dspb_ref[g] = jnp.zeros((CHUNK, LANES), F32) + jnp.sum(dspb_ref[g], axis=-1, keepdims=True)

    unit = lambda u: pl.BlockSpec((1, tm, c), lambda i: (u, i, 0))
    vec = pl.BlockSpec((1, c), lambda i: (0, 0))
    sq = pl.BlockSpec((groups, CHUNK, CHUNK), lambda i: (0, 0, 0))
    vshape = jax.ShapeDtypeStruct((1, c), F32)
    sshape = jax.ShapeDtypeStruct((groups, CHUNK, CHUNK), F32)
    return pl.pallas_call(
        body, name="mix_bwd_point", grid=(t // tm,),
        in_specs=[pl.BlockSpec((4, tm, half), lambda i: (0, i, 0)), unit(2), unit(3),
                  pl.BlockSpec((tm, c), lambda i: (i, 0)), vec, vec, vec, vec, sq, sq,
                  pl.BlockSpec((CHUNK, groups), lambda i: (0, 0))],
        out_specs=[pl.BlockSpec((2, tm, c), lambda i: (1, i, 0)), pl.BlockSpec((tm, c), lambda i: (i, 0)),
                   vec, vec, vec, vec, vec, sq, sq],
        out_shape=[jax.ShapeDtypeStruct((4, t, c), BF16), jax.ShapeDtypeStruct((t, c), F32),
                   vshape, vshape, vshape, vshape, vshape, sshape, sshape],
        compiler_params=_cparams(("arbitrary",)),
    )(dcat, z, z, a1, ln_a_g, ln_a_b, ln_v_g, ln_v_b, sp_w, sp_wt, sp_bt)


def mix_bwd_conv(dz, da1, z, conv_w, seq):
    _, t, c = z.shape
    tm = _tile(seq, 512)
    tiles_per_seq = seq // tm
    hb = tm // HALO
    n_halo_blocks = t // HALO

    def body(dz_in_ref, d_ref, dh_ref, av_ref, ag_ref, hv_ref, hg_ref, cw_ref, dz_ref, dcw_ref, a0_ref, d1_ref):
        del dz_in_ref
        i = pl.program_id(0)
        _glu_into(a0_ref, av_ref, ag_ref, hv_ref, hg_ref, i % tiles_per_seq == 0)
        d1_ref[0:tm, :] = d_ref[...]
        d1_ref[tm:, :] = jnp.where((i + 1) % tiles_per_seq == 0, 0.0, dh_ref[...])

        @pl.when(i == 0)
        def _():
            dcw_ref[...] = jnp.zeros_like(dcw_ref)

        d1 = d_ref[...]
        da0 = jnp.zeros((tm, c), F32)
        for k in range(CONV_WIDTH):
            back = CONV_WIDTH - 1 - k
            da0 = da0 + cw_ref[k:k + 1, :] * d1_ref[pl.ds(back, tm), :]
            dcw_ref[k:k + 1, :] += jnp.sum(d1 * a0_ref[pl.ds(HALO - back, tm), :], axis=0, keepdims=True)
        av = av_ref[0].astype(F32)
        s = _sigmoid(ag_ref[0].astype(F32))
        dz_ref[0] = (da0 * s).astype(BF16)
        dz_ref[1] = (da0 * av * s * (1.0 - s)).astype(BF16)

    unit = lambda u: pl.BlockSpec((1, tm, c), lambda i: (u, i, 0))
    halo = lambda u: pl.BlockSpec((1, HALO, c), lambda i: (u, jnp.maximum(i * hb - 1, 0), 0))
    return pl.pallas_call(
        body, name="mix_bwd_conv", grid=(t // tm,),
        in_specs=[pl.BlockSpec(memory_space=pl.ANY), pl.BlockSpec((tm, c), lambda i: (i, 0)),
                  pl.BlockSpec((HALO, c), lambda i: (jnp.minimum((i + 1) * hb, n_halo_blocks - 1), 0)),
                  unit(0), unit(1), halo(0), halo(1), pl.BlockSpec((CONV_WIDTH, c), lambda i: (0, 0))],
        out_specs=[pl.BlockSpec((2, tm, c), lambda i: (0, i, 0)), pl.BlockSpec((CONV_WIDTH, c), lambda i: (0, 0))],
        out_shape=[jax.ShapeDtypeStruct(dz.shape, BF16), jax.ShapeDtypeStruct((CONV_WIDTH, c), F32)],
        scratch_shapes=[pltpu.VMEM((HALO + tm, c), F32), pltpu.VMEM((tm + HALO, c), F32)],
        input_output_aliases={0: 0},
        compiler_params=_cparams(("arbitrary",)),
    )(dz, da1, da1, z, z, z, z, conv_w)


CHIP_FLIPS = ((1, 0), (0, 1), (1, 1))
ANY = pl.BlockSpec(memory_space=pl.ANY)


def _place():
    return lax.axis_index("x"), lax.axis_index("y"), lax.axis_index("c")


def _flip(v, f):
    return 1 - v if f else v


def place_shard(w, chip, dtype, name):
    r, cc = w.shape
    rb = _tile(r, 512)

    def body(chip_ref, w_ref, o_ref):
        del chip_ref
        o_ref[0] = w_ref[...].astype(dtype)

    return pl.pallas_call(
        body, name=name,
        grid_spec=pltpu.PrefetchScalarGridSpec(
            num_scalar_prefetch=1, grid=(r // rb,),
            in_specs=[pl.BlockSpec((rb, cc), lambda i, chip_ref: (i, 0))],
            out_specs=pl.BlockSpec((1, rb, cc), lambda i, chip_ref: (chip_ref[0], i, 0))),
        out_shape=jax.ShapeDtypeStruct((N_CHIPS, r, cc), dtype),
        compiler_params=_cparams(("parallel",)),
    )(chip, w)


def allgather_weights(shards, smalls):
    n, ns = len(shards), len(smalls)

    def body(*refs):
        ins, sins = refs[:n], refs[n:n + ns]
        outs, souts = refs[n + ns:2 * n + ns], refs[2 * n + ns:2 * n + 2 * ns]
        ici_send, ici_recv, d2d_send, d2d_recv, sm_send, sm_recv = refs[2 * n + 2 * ns:]
        x, y, c = _place()
        k = 2 * x + y
        sibling = (x, y, 1 - c)
        pending = []

        def half(a):
            hr = shards[a].shape[1] // 2
            return pl.ds(pl.multiple_of(c * hr, 16), hr)

        for a in range(n):
            for o, (fx, fy) in enumerate(CHIP_FLIPS):
                cp = pltpu.make_async_remote_copy(
                    src_ref=ins[a].at[k, half(a)], dst_ref=outs[a].at[k, half(a)],
                    send_sem=ici_send.at[3 * a + o], recv_sem=ici_recv.at[3 * a + o],
                    device_id=(_flip(x, fx), _flip(y, fy), c), device_id_type=MESH)
                cp.start()
                pending.append(cp.wait_send)
        for a in range(ns):
            for o, (fx, fy) in enumerate(CHIP_FLIPS):
                cp = pltpu.make_async_remote_copy(
                    src_ref=sins[a].at[k], dst_ref=souts[a].at[k],
                    send_sem=sm_send.at[3 * a + o], recv_sem=sm_recv.at[3 * a + o],
                    device_id=(_flip(x, fx), _flip(y, fy), c), device_id_type=MESH)
                cp.start()
                pending.append(cp.wait_send)
        for a in range(n):
            for o, (fx, fy) in enumerate(CHIP_FLIPS):
                kk = 2 * _flip(x, fx) + _flip(y, fy)
                landed = outs[a].at[kk, half(a)]
                pltpu.make_async_remote_copy(
                    src_ref=landed, dst_ref=landed, send_sem=ici_send.at[3 * a + o], recv_sem=ici_recv.at[3 * a + o],
                    device_id=sibling, device_id_type=MESH).wait_recv()
                cp = pltpu.make_async_remote_copy(
                    src_ref=landed, dst_ref=landed, send_sem=d2d_send.at[3 * a + o], recv_sem=d2d_recv.at[3 * a + o],
                    device_id=sibling, device_id_type=MESH)
                cp.start()
                pending.append(cp.wait_send)
        for a in range(n):
            hr = shards[a].shape[1] // 2
            other = pl.ds(pl.multiple_of((1 - c) * hr, 16), hr)
            for o, (fx, fy) in enumerate(CHIP_FLIPS):
                kk = 2 * _flip(x, fx) + _flip(y, fy)
                got = outs[a].at[kk, other]
                pltpu.make_async_remote_copy(
                    src_ref=got, dst_ref=got, send_sem=d2d_send.at[3 * a + o], recv_sem=d2d_recv.at[3 * a + o],
                    device_id=sibling, device_id_type=MESH).wait_recv()
        for a in range(ns):
            for o, (fx, fy) in enumerate(CHIP_FLIPS):
                kk = 2 * _flip(x, fx) + _flip(y, fy)
                got = souts[a].at[kk]
                pltpu.make_async_remote_copy(
                    src_ref=got, dst_ref=got, send_sem=sm_send.at[3 * a + o], recv_sem=sm_recv.at[3 * a + o],
                    device_id=sibling, device_id_type=MESH).wait_recv()
        for w in pending:
            w()

    out_shape = [jax.ShapeDtypeStruct(s.shape, s.dtype) for s in list(shards) + list(smalls)]
    dma = pltpu.SemaphoreType.DMA
    res = pl.pallas_call(
        body, name="allgather_weights", in_specs=[ANY] * (n + ns), out_specs=[ANY] * (n + ns), out_shape=out_shape,
        scratch_shapes=[dma((3 * n,)), dma((3 * n,)), dma((3 * n,)), dma((3 * n,)), dma((3 * ns,)), dma((3 * ns,))],
        input_output_aliases={i: i for i in range(n + ns)},
        compiler_params=pltpu.CompilerParams(has_side_effects=True),
    )(*shards, *smalls)
    return res[:n], res[n:]


def rs_exchange(grads):
    n = len(grads)

    def body(*refs):
        ins, outs = refs[:n], refs[n:2 * n]
        send, recv = refs[2 * n:]
        x, y, c = _place()
        cps = []
        for a in range(n):
            cp = pltpu.make_async_remote_copy(
                src_ref=ins[a].at[:, 1 - c], dst_ref=outs[a], send_sem=send.at[a], recv_sem=recv.at[a],
                device_id=(x, y, 1 - c), device_id_type=MESH)
            cp.start()
            cps.append(cp)
        for cp in cps:
            cp.wait()

    dma = pltpu.SemaphoreType.DMA
    return pl.pallas_call(
        body, name="rs_exchange", in_specs=[ANY] * n, out_specs=[ANY] * n,
        out_shape=[jax.ShapeDtypeStruct((g.shape[0],) + g.shape[2:], g.dtype) for g in grads],
        scratch_shapes=[dma((n,)), dma((n,))],
        compiler_params=pltpu.CompilerParams(has_side_effects=True),
    )(*grads)


def rs_add(g, sib, core, out_dtype, name):
    nk, _, hr, cc = g.shape
    rb = _tile(hr, 256)

    def body(core_ref, g_ref, s_ref, o_ref):
        del core_ref
        o_ref[0] = (g_ref[0, 0] + s_ref[0]).astype(out_dtype)

    return pl.pallas_call(
        body, name=name,
        grid_spec=pltpu.PrefetchScalarGridSpec(
            num_scalar_prefetch=1, grid=(nk, hr // rb),
            in_specs=[pl.BlockSpec((1, 1, rb, cc), lambda k, i, core_ref: (k, core_ref[0], i, 0)),
                      pl.BlockSpec((1, rb, cc), lambda k, i, core_ref: (k, i, 0))],
            out_specs=pl.BlockSpec((1, rb, cc), lambda k, i, core_ref: (k, i, 0))),
        out_shape=jax.ShapeDtypeStruct((nk, hr, cc), out_dtype),
        compiler_params=_cparams(("parallel", "parallel")),
    )(core, g, sib)


def rs_send(parts):
    n = len(parts)

    def body(*refs):
        ins, outs = refs[:n], refs[n:2 * n]
        send, recv = refs[2 * n:]
        x, y, c = _place()
        waits = []
        for a in range(n):
            for o, (fx, fy) in enumerate(CHIP_FLIPS):
                kk = 2 * _flip(x, fx) + _flip(y, fy)
                cp = pltpu.make_async_remote_copy(
                    src_ref=ins[a].at[kk], dst_ref=outs[a].at[o], send_sem=send.at[3 * a + o], recv_sem=recv.at[3 * a + o],
                    device_id=(_flip(x, fx), _flip(y, fy), c), device_id_type=MESH)
                cp.start()
                waits.append(cp.wait)
        for w in waits:
            w()

    dma = pltpu.SemaphoreType.DMA
    return pl.pallas_call(
        body, name="rs_send", in_specs=[ANY] * n, out_specs=[ANY] * n,
        out_shape=[jax.ShapeDtypeStruct((3,) + p.shape[1:], p.dtype) for p in parts],
        scratch_shapes=[dma((3 * n,)), dma((3 * n,))],
        compiler_params=pltpu.CompilerParams(has_side_effects=True),
    )(*parts)


def rs_sum(recv, part, where, full, layer, n_layers, name):
    _, hr, cc = recv.shape
    rb = _tile(hr, 256)

    def body(*refs):
        r_ref, p_ref, o_ref = refs[1], refs[2], refs[-1]
        o_ref[0, 0] = ((p_ref[0].astype(F32) + r_ref[0].astype(F32)) + r_ref[1].astype(F32)) + r_ref[2].astype(F32)

    in_specs = [pl.BlockSpec((3, rb, cc), lambda i, w_ref: (0, i, 0)),
                pl.BlockSpec((1, rb, cc), lambda i, w_ref: (w_ref[0], i, 0))]
    args = [where, recv, part]
    aliases = {}
    if full is not None:
        in_specs.append(ANY)
        args.append(full)
        aliases = {3: 0}
    return pl.pallas_call(
        body, name=name,
        grid_spec=pltpu.PrefetchScalarGridSpec(
            num_scalar_prefetch=1, grid=(hr // rb,), in_specs=in_specs,
            out_specs=pl.BlockSpec((1, 1, rb, cc), lambda i, w_ref: (layer, w_ref[1], i, 0))),
        out_shape=jax.ShapeDtypeStruct((n_layers, 2, hr, cc), F32),
        input_output_aliases=aliases,
        compiler_params=_cparams(("parallel",)),
    )(*args)


def rs_share(fulls):
    n = len(fulls)

    def body(*refs):
        ins, outs = refs[:n], refs[n:2 * n]
        send, recv = refs[2 * n:]
        x, y, c = _place()
        cps = []
        for a in range(n):
            cp = pltpu.make_async_remote_copy(
                src_ref=ins[a].at[:, c], dst_ref=outs[a].at[:, c], send_sem=send.at[a], recv_sem=recv.at[a],
                device_id=(x, y, 1 - c), device_id_type=MESH)
            cp.start()
            cps.append(cp)
        for a in range(n):
            got = outs[a].at[:, 1 - c]
            pltpu.make_async_remote_copy(
                src_ref=got, dst_ref=got, send_sem=send.at[a], recv_sem=recv.at[a],
                device_id=(x, y, 1 - c), device_id_type=MESH).wait_recv()
        for cp in cps:
            cp.wait_send()

    dma = pltpu.SemaphoreType.DMA
    return pl.pallas_call(
        body, name="rs_share", in_specs=[ANY] * n, out_specs=[ANY] * n,
        out_shape=[jax.ShapeDtypeStruct(f.shape, f.dtype) for f in fulls],
        scratch_shapes=[dma((n,)), dma((n,))],
        input_output_aliases={i: i for i in range(n)},
        compiler_params=pltpu.CompilerParams(has_side_effects=True),
    )(*fulls)


def allreduce_small(v):
    r, w = v.shape

    def body(v_ref, o_ref, buf, send, recv, loc):
        x, y, c = _place()
        me = 4 * x + 2 * y + c
        mine = pltpu.make_async_copy(v_ref, buf.at[me], loc)
        mine.start()
        cps = []
        for o in range(1, N_DEV):
            fx, fy, fc = (o >> 2) & 1, (o >> 1) & 1, o & 1
            cp = pltpu.make_async_remote_copy(
                src_ref=v_ref, dst_ref=buf.at[me], send_sem=send.at[o - 1], recv_sem=recv.at[o - 1],
                device_id=(_flip(x, fx), _flip(y, fy), _flip(c, fc)), device_id_type=MESH)
            cp.start()
            cps.append(cp)
        for o in range(1, N_DEV):
            fx, fy, fc = (o >> 2) & 1, (o >> 1) & 1, o & 1
            peer = 4 * _flip(x, fx) + 2 * _flip(y, fy) + _flip(c, fc)
            pltpu.make_async_remote_copy(
                src_ref=v_ref, dst_ref=buf.at[peer], send_sem=send.at[o - 1], recv_sem=recv.at[o - 1],
                device_id=(x, y, c), device_id_type=MESH).wait_recv()
        for cp in cps:
            cp.wait_send()
        mine.wait()
        acc = buf[0]
        for d in range(1, N_DEV):
            acc = acc + buf[d]
        o_ref[...] = acc

    dma = pltpu.SemaphoreType.DMA
    vm = pl.BlockSpec(memory_space=pltpu.VMEM)
    return pl.pallas_call(
        body, name="allreduce_small", in_specs=[vm], out_specs=vm,
        out_shape=jax.ShapeDtypeStruct((r, w), F32),
        scratch_shapes=[pltpu.VMEM((N_DEV, r, w), F32), dma((N_DEV - 1,)), dma((N_DEV - 1,)), dma],
        compiler_params=pltpu.CompilerParams(has_side_effects=True, vmem_limit_bytes=VMEM_LIMIT),
    )(v)


def adamw(w, g, m, v, name):
    r, cc = w.shape
    rb = _tile(r, 256)

    def body(w_ref, g_ref, m_ref, v_ref, d_ref, nm_ref, nv_ref):
        gv = g_ref[...]
        nm = ADAM_B1 * m_ref[...] + (1.0 - ADAM_B1) * gv
        nv = ADAM_B2 * v_ref[...] + (1.0 - ADAM_B2) * (gv * gv)
        m_hat = nm / (1.0 - ADAM_B1 ** ADAM_STEP)
        v_hat = nv / (1.0 - ADAM_B2 ** ADAM_STEP)
        d_ref[...] = -ADAM_LR * (m_hat / (jnp.sqrt(v_hat) + ADAM_EPS) + ADAM_WD * w_ref[...])
        nm_ref[...] = nm
        nv_ref[...] = nv

    blk = pl.BlockSpec((rb, cc), lambda i: (i, 0))
    shp = jax.ShapeDtypeStruct((r, cc), F32)
    return pl.pallas_call(
        body, name=name, grid=(r // rb,), in_specs=[blk] * 4, out_specs=[blk] * 3, out_shape=[shp] * 3,
        compiler_params=_cparams(("parallel",)),
    )(w, g, m, v)


WEIGHTS = ['g_ffn1', 'w_ffn1_gate', 'w_ffn1_up', 'w_ffn1_down', 'g_mix', 'w_in_ab', 'conv_w', 'conv_b', 'ln_a_g',
           'ln_a_b', 'ln_v_g', 'ln_v_b', 'sp_w', 'sp_b', 'w_out_ab', 'w_qkv', 'w_o', 'g_ffn2', 'w_ffn2_gate',
           'w_ffn2_up', 'w_ffn2_down', 'g_final']
BIG = ['w_ffn1_gate', 'w_ffn1_up', 'w_ffn1_down', 'w_in_ab', 'w_out_ab', 'w_qkv', 'w_o', 'w_ffn2_gate', 'w_ffn2_up',
       'w_ffn2_down']
SMALL = ['g_ffn1', 'g_mix', 'g_ffn2', 'g_final', 'conv_b', 'ln_a_g', 'ln_a_b', 'ln_v_g', 'ln_v_b', 'sp_b', 'sp_w']


def _rows(a):
    return a.reshape(-1, LANES)


def _pack(parts):
    v = jnp.concatenate([_rows(p) for p in parts], axis=0)
    pad = (-v.shape[0]) % 8
    return jnp.pad(v, ((0, pad), (0, 0)))


def _unpack(v, shapes):
    out, r = [], 0
    for s in shapes:
        n = 1
        for d in s:
            n *= d
        n //= LANES
        out.append(v[r:r + n].reshape(s))
        r += n
    return out


def kernel(x, g_ffn1, w_ffn1_gate, w_ffn1_up, w_ffn1_down, g_mix, w_in_ab, conv_w, conv_b, ln_a_g, ln_a_b, ln_v_g, ln_v_b, sp_w, sp_b, w_out_ab, w_qkv, w_o, g_ffn2, w_ffn2_gate, w_ffn2_up, w_ffn2_down, g_final, loss_target, m_g_ffn1, m_w_ffn1_gate, m_w_ffn1_up, m_w_ffn1_down, m_g_mix, m_w_in_ab, m_conv_w, m_conv_b, m_ln_a_g, m_ln_a_b, m_ln_v_g, m_ln_v_b, m_sp_w, m_sp_b, m_w_out_ab, m_w_qkv, m_w_o, m_g_ffn2, m_w_ffn2_gate, m_w_ffn2_up, m_w_ffn2_down, m_g_final, v_g_ffn1, v_w_ffn1_gate, v_w_ffn1_up, v_w_ffn1_down, v_g_mix, v_w_in_ab, v_conv_w, v_conv_b, v_ln_a_g, v_ln_a_b, v_ln_v_g, v_ln_v_b, v_sp_w, v_sp_b, v_w_out_ab, v_w_qkv, v_w_o, v_g_ffn2, v_w_ffn2_gate, v_w_ffn2_up, v_w_ffn2_down, v_g_final):
    p = dict(locals())
    n_seq, seq, d = x.shape
    t = n_seq * seq
    depth = g_ffn1.shape[0]
    core = lax.axis_index("c")
    chip = 2 * lax.axis_index("x") + lax.axis_index("y")
    xf = x.reshape(t, d)
    target = loss_target.reshape(t, d)

    items = []
    for name in BIG:
        for layer in range(p[name].shape[0]):
            items.append((name, layer))
    chip1 = chip.reshape(1).astype(jnp.int32)
    shards = [place_shard(p[name][layer], chip1, BF16, "place_shard") for name, layer in items]
    gathered, (conv_w4,) = allgather_weights(shards, [place_shard(conv_w[0], chip1, F32, "place_conv_w")])
    wt = {it: g for it, g in zip(items, gathered)}
    c_mix = conv_w4.shape[2] * N_CHIPS
    conv_full = jnp.transpose(conv_w4, (1, 0, 2)).reshape(CONV_WIDTH, c_mix)
    vec = lambda a: a.reshape(1, -1)
    sp_bt = sp_b[0].T
    sp_wt = jnp.transpose(sp_w[0], (0, 2, 1))
    d_ff = w_ffn1_gate.shape[2]
    n_in = w_in_ab.shape[2]
    n_qkv = w_qkv.shape[2] // 3

    saved = []
    xc = xf
    for layer in range(depth):
        s = {}
        for half, (gn, wn) in enumerate((('g_ffn1', 'w_ffn1'), ('g_ffn2', 'w_ffn2'))):
            if half == 1:
                s['x_mix'] = xc
                s['h_mix'] = rmsnorm_fwd(xc, vec(g_mix[layer]), "norm_mix")
                if layer % 2 == 0:
                    (z,) = colmm(s['h_mix'], [wt[('w_in_ab', layer // 2)]], n_in, BF16, "mm_in")
                    cat, a1 = mix_fwd(z, conv_full, conv_b, ln_a_g, ln_a_b, vec(ln_v_g), vec(ln_v_b), sp_w[0], sp_bt, seq)
                    s.update(z=z, cat=cat, a1=a1)
                    xc = rowmm([cat], wt[('w_out_ab', layer // 2)], xc, 1.0, "mm_out")
                else:
                    (qkv,) = colmm(s['h_mix'], [wt[('w_qkv', layer // 2)]], n_qkv, BF16, "mm_qkv")
                    o, tot, cnt = attn_fwd(qkv, n_seq, seq)
                    s.update(qkv=qkv, o=o, tot=tot, cnt=cnt)
                    xc = rowmm([o], wt[('w_o', layer // 2)], xc, 1.0, "mm_o")
            s['x' + wn] = xc
            h = rmsnorm_fwd(xc, vec(p[gn][layer]), "norm_ffn")
            gate, up = colmm(h, [wt[(wn + '_gate', layer)], wt[(wn + '_up', layer)]], d_ff, BF16, "ffn_gateup")
            xc = rowmm([gate, up], wt[(wn + '_down', layer)], xc, 0.5, "ffn_down")
            s.update({'h' + wn: h, 'gate' + wn: gate, 'up' + wn: up})
        saved.append(s)

    loss8, dx, dxb, dg_final = loss_head(xc, vec(g_final), target)
    loss = lax.psum(loss8[0, 0], ("x", "y", "c"))

    gw = {}
    gs = {}
    for layer in reversed(range(depth)):
        s = saved[layer]
        for half, (gn, wn) in reversed(list(enumerate((('g_ffn1', 'w_ffn1'), ('g_ffn2', 'w_ffn2'))))):
            wd = wt[(wn + '_down', layer)]
            dgate, dup, act = rowmm_t(dxb, wd, 0.5, BF16, "ffn_bwd_act", gu=(s['gate' + wn], s['up' + wn]))
            gw[(wn + '_down', layer)] = dw_row(act, dxb, 0.5, "ffn_dw_down")
            gw[(wn + '_gate', layer)], gw[(wn + '_up', layer)] = dw_col(s['h' + wn], [dgate, dup], N_CHIPS, d_ff, "ffn_dw_gateup")
            dx, dxb, dg = colmm_t([dgate, dup], [wt[(wn + '_gate', layer)], wt[(wn + '_up', layer)]], d_ff,
                                  s['x' + wn], vec(p[gn][layer]), dx, "ffn_bwd_in")
            gs[(gn, layer)] = dg
            if half == 1:
                if layer % 2 == 0:
                    i = layer // 2
                    w_out = wt[('w_out_ab', i)]
                    dcat = rowmm_t(dxb, w_out, 1.0, F32, "mm_out_t")
                    gw[('w_out_ab', i)] = dw_row(s['cat'], dxb, 1.0, "dw_out")
                    dz, da1, dcb, dlag, dlab, dlvg, dlvb, dspw, dspb = mix_bwd_point(
                        dcat, s['z'], s['a1'], ln_a_g, ln_a_b, vec(ln_v_g), vec(ln_v_b), sp_w[0], sp_wt, sp_bt, seq)
                    dz, dcw = mix_bwd_conv(dz, da1, s['z'], conv_full, seq)
                    gs.update({('conv_b', i): dcb, ('ln_a_g', i): dlag, ('ln_a_b', i): dlab, ('ln_v_g', i): dlvg,
                               ('ln_v_b', i): dlvb, ('sp_w', i): dspw, ('sp_b', i): dspb[:, :, 0], ('conv_w', i): dcw})
                    (gw[('w_in_ab', i)],) = dw_col(s['h_mix'], [dz], N_CHIPS, n_in, "dw_in")
                    dx, dxb, dg = colmm_t([dz], [wt[('w_in_ab', i)]], n_in, s['x_mix'], vec(g_mix[layer]), dx, "mm_in_t")
                else:
                    i = layer // 2
                    w_o4 = wt[('w_o', i)]
                    do = rowmm_t(dxb, w_o4, 1.0, BF16, "mm_o_t")
                    gw[('w_o', i)] = dw_row(s['o'], dxb, 1.0, "dw_o")
                    dq, dk, dv = attn_bwd(s['qkv'], do, s['tot'], s['cnt'], n_seq, seq)
                    dqkv = jnp.concatenate([dq, dk, dv], axis=0)
                    (gw[('w_qkv', i)],) = dw_col(s['h_mix'], [dqkv], N_CHIPS, n_qkv, "dw_qkv")
                    dx, dxb, dg = colmm_t([dqkv], [wt[('w_qkv', i)]], n_qkv, s['x_mix'], vec(g_mix[layer]), dx, "mm_qkv_t")
                gs[('g_mix', layer)] = dg
    grad_x = dx.reshape(x.shape)

    core1 = core.reshape(1).astype(jnp.int32)
    g4 = [gw[it].reshape(N_CHIPS, 2, gw[it].shape[1] // 2, gw[it].shape[2]) for it in items]
    sib = rs_exchange(g4)
    parts = [rs_add(g, sb, core1, REDUCE_DTYPE, "rs_add") for g, sb in zip(g4, sib)]
    recv = dict(zip(items, rs_send(parts)))
    part = dict(zip(items, parts))
    where = jnp.stack([chip, core]).astype(jnp.int32)
    fulls = []
    for name in BIG:
        full = None
        n_layers = p[name].shape[0]
        for layer in range(n_layers):
            full = rs_sum(recv[(name, layer)], part[(name, layer)], where, full, layer, n_layers, "rs_sum")
        fulls.append(full)
    shared = rs_share(fulls)
    grads = {name: sh.reshape(p[name].shape) for name, sh in zip(BIG, shared)}

    stack = lambda name: jnp.concatenate([gs[(name, layer)].reshape((1,) + p[name].shape[1:]) for layer in range(p[name].shape[0])], axis=0)
    small_g = [stack(name) if name != 'g_final' else dg_final.reshape(p[name].shape) for name in SMALL]
    packed = _pack(small_g + [gs[('conv_w', 0)]])
    red = allreduce_small(packed)
    outs = _unpack(red, [p[name].shape for name in SMALL] + [(CONV_WIDTH, c_mix)])
    for name, g in zip(SMALL, outs[:-1]):
        grads[name] = g
    conv_g = outs[-1].reshape(CONV_WIDTH, N_CHIPS, c_mix // N_CHIPS)
    grads['conv_w'] = lax.dynamic_index_in_dim(conv_g, chip, axis=1, keepdims=False).reshape(conv_w.shape)

    delta, new_m, new_v = {}, {}, {}
    for name in BIG:
        shp = p[name].shape
        two = lambda a: a.reshape(shp[0] * shp[1], shp[2])
        dl, nm, nv = adamw(two(p[name]), two(grads[name]), two(p['m_' + name]), two(p['v_' + name]), "adamw")
        delta[name], new_m[name], new_v[name] = dl.reshape(shp), nm.reshape(shp), nv.reshape(shp)
    small_names = SMALL + ['conv_w']
    pk = lambda pre: _pack([p[pre + name] for name in small_names])
    dl, nm, nv = adamw(pk(''), _pack([grads[name] for name in small_names]), pk('m_'), pk('v_'), "adamw_small")
    shapes = [p[name].shape for name in small_names]
    for dst, val in ((delta, dl), (new_m, nm), (new_v, nv)):
        for name, a in zip(small_names, _unpack(val, shapes)):
            dst[name] = a

    return (loss, grad_x, *[grads[n] for n in WEIGHTS], *[delta[n] for n in WEIGHTS],
            *[new_m[n] for n in WEIGHTS], *[new_v[n] for n in WEIGHTS])
```

```python
import functools

import jax
import jax.numpy as jnp
from jax import lax
from jax.experimental import pallas as pl
from jax.experimental.pallas import tpu as pltpu

F32 = jnp.float32
BF16 = jnp.bfloat16
EPS = 1e-6
HEAD_DIM = 64
CONV_WIDTH = 31
CHUNK = 128
KBLK = 128
ATT_BLOCK = 256
DW_TOKENS = 2048
STICK_GONE = -110.0
LANES = 128
HALO = 32
ADAM_LR, ADAM_B1, ADAM_B2, ADAM_EPS, ADAM_WD, ADAM_STEP = 0.001, 0.9, 0.999, 1e-08, 0.01, 10
VMEM_LIMIT = 56 * 1024 * 1024
MESH = pl.DeviceIdType.MESH
N_CHIPS = 4
N_DEV = 8
REDUCE_DTYPE = BF16


def _cparams(sem):
    return pltpu.CompilerParams(dimension_semantics=sem, vmem_limit_bytes=VMEM_LIMIT)


def _nt(a, b):
    return lax.dot_general(a, b, (((1,), (1,)), ((), ())), preferred_element_type=F32)


def _tn(a, b):
    return lax.dot_general(a, b, (((0,), (0,)), ((), ())), preferred_element_type=F32)


def _nn(a, b):
    return jnp.dot(a, b, preferred_element_type=F32)


def _sigmoid(x):
    return 0.5 * jnp.tanh(0.5 * x) + 0.5


def _tile(t, want):
    if t <= want:
        return t
    for cand in range(want - want % 8, 7, -8):
        if t % cand == 0:
            return cand
    raise ValueError((t, want))


def rmsnorm_fwd(x, g, name):
    t, d = x.shape
    tm = _tile(t, 512)

    def body(x_ref, g_ref, h_ref):
        xv = x_ref[...]
        r = lax.rsqrt(jnp.mean(xv * xv, axis=-1, keepdims=True) + EPS)
        h_ref[...] = (xv * r * g_ref[...]).astype(BF16)

    return pl.pallas_call(
        body, name=name, grid=(t // tm,),
        in_specs=[pl.BlockSpec((tm, d), lambda i: (i, 0)), pl.BlockSpec((1, d), lambda i: (0, 0))],
        out_specs=pl.BlockSpec((tm, d), lambda i: (i, 0)),
        out_shape=jax.ShapeDtypeStruct((t, d), BF16),
        compiler_params=_cparams(("parallel",)),
    )(x, g)


def colmm(h, ws, nu, out_dtype, name, carry=None):
    t, k = h.shape
    j, _, nj = ws[0].shape
    per = nj // nu
    units = j * per
    tm = _tile(t, 512)
    nw = len(ws)

    def body(*refs):
        h_ref = refs[0]
        hv = h_ref[...]
        for n in range(nw):
            res = _nn(hv, refs[1 + n][0]).astype(out_dtype)
            for u in range(per):
                refs[1 + nw + n][u] = res[:, u * nu:(u + 1) * nu]

    w_spec = pl.BlockSpec((1, k, nj), lambda s, i: (s, 0, 0))
    o_spec = pl.BlockSpec((per, tm, nu), lambda s, i: (s, i, 0))
    return _call(
        body, name=name, grid=(j, t // tm),
        in_specs=[pl.BlockSpec((tm, k), lambda s, i: (i, 0))] + [w_spec] * nw,
        out_specs=[o_spec] * nw,
        out_shape=[jax.ShapeDtypeStruct((units, t, nu), out_dtype)] * nw,
        args=[h, *ws], sem=("parallel", "parallel"), carry=carry)


def rowmm(a_list, w, resid, scale, name, carry=None):
    swiglu = len(a_list) == 2
    u_n, t, ku = a_list[0].shape
    n = w.shape[2]
    tm = _tile(t, 256)

    def body(*refs):
        a_refs = refs[:len(a_list)]
        w_ref, r_ref, o_ref = refs[len(a_list):]
        acc = jnp.zeros((tm, n), F32)
        for u in range(u_n):
            if swiglu:
                gv = a_refs[0][u].astype(F32)
                av = (gv * _sigmoid(gv) * a_refs[1][u].astype(F32)).astype(BF16)
            else:
                av = a_refs[0][u]
            acc = acc + _nn(av, w_ref[u])
        o_ref[...] = r_ref[...] + scale * acc

    a_spec = pl.BlockSpec((u_n, tm, ku), lambda i: (0, i, 0))
    (out,), carried = _call(
        body, name=name, grid=(t // tm,),
        in_specs=[a_spec] * len(a_list) + [pl.BlockSpec((u_n, ku, n), lambda i: (0, 0, 0)),
                                           pl.BlockSpec((tm, n), lambda i: (i, 0))],
        out_specs=[pl.BlockSpec((tm, n), lambda i: (i, 0))],
        out_shape=[jax.ShapeDtypeStruct((t, n), F32)],
        args=[*a_list, w, resid], sem=("parallel",), carry=carry)
    return out, carried


def rowmm_t(dyb, w, scale, out_dtype, name, gu=None):
    t, n = dyb.shape
    u_n, ku, _ = w.shape
    tm = _tile(t, 512)

    def body(*refs):
        if gu is None:
            dy_ref, w_ref, o_ref = refs
            o_ref[0] = (scale * _nt(dy_ref[...], w_ref[0])).astype(out_dtype)
        else:
            dy_ref, w_ref, g_ref, u_ref, dg_ref, du_ref, a_ref = refs
            dact = scale * _nt(dy_ref[...], w_ref[0])
            gv = g_ref[0].astype(F32)
            uv = u_ref[0].astype(F32)
            s = _sigmoid(gv)
            silu = gv * s
            dg_ref[0] = (dact * uv * (s * (1.0 + gv * (1.0 - s)))).astype(BF16)
            du_ref[0] = (dact * silu).astype(BF16)
            a_ref[0] = (silu * uv).astype(BF16)

    blk = pl.BlockSpec((1, tm, ku), lambda u, i: (u, i, 0))
    in_specs = [pl.BlockSpec((tm, n), lambda u, i: (i, 0)), pl.BlockSpec((1, ku, n), lambda u, i: (u, 0, 0))]
    if gu is None:
        return pl.pallas_call(
            body, name=name, grid=(u_n, t // tm), in_specs=in_specs, out_specs=blk,
            out_shape=jax.ShapeDtypeStruct((u_n, t, ku), out_dtype),
            compiler_params=_cparams(("parallel", "parallel")),
        )(dyb, w)
    return pl.pallas_call(
        body, name=name, grid=(u_n, t // tm), in_specs=in_specs + [blk, blk], out_specs=[blk] * 3,
        out_shape=[jax.ShapeDtypeStruct((u_n, t, ku), BF16)] * 3,
        compiler_params=_cparams(("parallel", "parallel")),
    )(dyb, w, *gu)


def colmm_t(dzs, ws, nu, x, g, dy_in, name, carry=None):
    t, k = x.shape
    j, _, nj = ws[0].shape
    per = nj // nu
    units = j * per
    nw = len(ws)
    tm = _tile(t, 256)

    def body(*refs):
        dz_refs = refs[:nw]
        w_refs = refs[nw:2 * nw]
        x_ref, g_ref, dy_ref, dx_ref, dxb_ref, dg_ref = refs[2 * nw:]
        i = pl.program_id(0)
        dh = jnp.zeros((tm, k), F32)
        for n in range(nw):
            for u in range(units):
                wv = w_refs[n][u // per, :, (u % per) * nu:(u % per + 1) * nu]
                dh = dh + _nt(dz_refs[n][u], wv)
        xv = x_ref[...]
        gv = g_ref[...]
        r = lax.rsqrt(jnp.mean(xv * xv, axis=-1, keepdims=True) + EPS)
        uu = dh * gv
        dx = dy_ref[...] + r * uu - xv * (r * r * r * jnp.mean(uu * xv, axis=-1, keepdims=True))
        dx_ref[...] = dx
        dxb_ref[...] = dx.astype(BF16)
        part = jnp.sum(dh * (xv * r), axis=0, keepdims=True)

        @pl.when(i == 0)
        def _():
            dg_ref[...] = part

        @pl.when(i > 0)
        def _():
            dg_ref[...] += part

    dz_spec = pl.BlockSpec((units, tm, nu), lambda i: (0, i, 0))
    w_spec = pl.BlockSpec((j, k, nj), lambda i: (0, 0, 0))
    row = pl.BlockSpec((tm, k), lambda i: (i, 0))
    vec = pl.BlockSpec((1, k), lambda i: (0, 0))
    return _call(
        body, name=name, grid=(t // tm,),
        in_specs=[dz_spec] * nw + [w_spec] * nw + [row, vec, row],
        out_specs=[row, row, vec],
        out_shape=[jax.ShapeDtypeStruct((t, k), F32), jax.ShapeDtypeStruct((t, k), BF16),
                   jax.ShapeDtypeStruct((1, k), F32)],
        args=[*dzs, *ws, x, g, dy_in], sem=("arbitrary",), carry=carry)


def dw_col(h, dzs, j, nu, name):
    t, k = h.shape
    units = dzs[0].shape[0]
    per = units // j
    nw = len(dzs)
    tt = _tile(t, DW_TOKENS)

    def body(*refs):
        h_ref = refs[0]
        s = pl.program_id(1)
        hv = h_ref[...]
        for n in range(nw):
            o_ref = refs[1 + nw + n]
            for u in range(per):
                part = _tn(hv, refs[1 + n][u])
                cols = slice(u * nu, (u + 1) * nu)

                @pl.when(s == 0)
                def _():
                    o_ref[0, :, cols] = part

                @pl.when(s > 0)
                def _():
                    o_ref[0, :, cols] += part

    return pl.pallas_call(
        body, name=name, grid=(j, t // tt),
        in_specs=[pl.BlockSpec((tt, k), lambda u, s: (s, 0))] + [pl.BlockSpec((per, tt, nu), lambda u, s: (u, s, 0))] * nw,
        out_specs=[pl.BlockSpec((1, k, per * nu), lambda u, s: (u, 0, 0))] * nw,
        out_shape=[jax.ShapeDtypeStruct((j, k, per * nu), F32)] * nw,
        compiler_params=_cparams(("parallel", "arbitrary")),
    )(h, *dzs)


def dw_row(a, dyb, scale, name):
    u_n, t, ku = a.shape
    n = dyb.shape[1]
    tt = _tile(t, DW_TOKENS)

    def body(a_ref, dy_ref, o_ref):
        s = pl.program_id(1)
        part = scale * _tn(a_ref[0], dy_ref[...])

        @pl.when(s == 0)
        def _():
            o_ref[0] = part

        @pl.when(s > 0)
        def _():
            o_ref[0] += part

    return pl.pallas_call(
        body, name=name, grid=(u_n, t // tt),
        in_specs=[pl.BlockSpec((1, tt, ku), lambda u, s: (u, s, 0)), pl.BlockSpec((tt, n), lambda u, s: (s, 0))],
        out_specs=pl.BlockSpec((1, ku, n), lambda u, s: (u, 0, 0)),
        out_shape=jax.ShapeDtypeStruct((u_n, ku, n), F32),
        compiler_params=_cparams(("parallel", "arbitrary")),
    )(a, dyb)


def loss_head(x, g, target):
    t, d = x.shape
    tm = _tile(t, 256)

    def body(x_ref, g_ref, t_ref, loss_ref, dx_ref, dxb_ref, dg_ref):
        i = pl.program_id(0)
        xv = x_ref[...]
        gv = g_ref[...]
        r = lax.rsqrt(jnp.mean(xv * xv, axis=-1, keepdims=True) + EPS)
        xh = xv * r
        err = xh * gv - t_ref[...]
        dy = err * (1.0 / d)
        uu = dy * gv
        dx = r * uu - xv * (r * r * r * jnp.mean(uu * xv, axis=-1, keepdims=True))
        dx_ref[...] = dx
        dxb_ref[...] = dx.astype(BF16)
        dg_part = jnp.sum(dy * xh, axis=0, keepdims=True)
        row = jnp.sum(err * err, axis=-1, keepdims=True) * (0.5 / d)
        l_part = jnp.zeros((8, LANES), F32) + jnp.sum(row, axis=0, keepdims=True)

        @pl.when(i == 0)
        def _():
            dg_ref[...] = dg_part
            loss_ref[...] = l_part

        @pl.when(i > 0)
        def _():
            dg_ref[...] += dg_part
            loss_ref[...] += l_part

    row = pl.BlockSpec((tm, d), lambda i: (i, 0))
    vec = pl.BlockSpec((1, d), lambda i: (0, 0))
    return pl.pallas_call(
        body, name="loss_head", grid=(t // tm,),
        in_specs=[row, vec, row],
        out_specs=[pl.BlockSpec((8, LANES), lambda i: (0, 0)), row, row, vec],
        out_shape=[jax.ShapeDtypeStruct((8, LANES), F32), jax.ShapeDtypeStruct((t, d), F32),
                   jax.ShapeDtypeStruct((t, d), BF16), jax.ShapeDtypeStruct((1, d), F32)],
        compiler_params=_cparams(("arbitrary",)),
    )(x, g, target)


def _split(v):
    hi = v.astype(BF16)
    lo = (v - hi.astype(F32)).astype(BF16)
    return hi, lo


def _keysums(v, m_ext):
    hi, lo = _split(v)
    outs = []
    for j in range(v.shape[1] // KBLK):
        sl = slice(j * KBLK, (j + 1) * KBLK)
        cs = _nn(jnp.concatenate([hi[:, sl], lo[:, sl]], axis=1), m_ext)
        outs.append((cs[:, :KBLK], cs[:, KBLK:]))
    return outs


def _softplus_parts(z):
    sp = jnp.maximum(z, 0.0) + jnp.log(1.0 + jnp.exp(-jnp.abs(z)))
    return sp, z - sp


def _sum_matrices():
    r = lax.broadcasted_iota(jnp.int32, (2 * KBLK, 2 * KBLK), 0) % KBLK
    c = lax.broadcasted_iota(jnp.int32, (2 * KBLK, 2 * KBLK), 1)
    suffix = jnp.where((r > c) | (c >= KBLK), 1.0, 0.0).astype(BF16)
    prefix = jnp.where((r <= c) | (c >= KBLK), 1.0, 0.0).astype(BF16)
    return suffix, prefix


def attn_fwd(qkv, n_seq, seq):
    t = qkv.shape[1]
    n_pairs = (qkv.shape[0] // 3) * 2
    bq = min(ATT_BLOCK, seq)
    nq = seq // bq
    nsub = bq // KBLK
    suffix_m, _ = _sum_matrices()

    def body(q_ref, k_ref, v_ref, m_ref, o_ref, tot_ref, cnt_ref):
        qi = pl.program_id(2)
        step_id = (pl.program_id(0) * n_pairs + pl.program_id(1)) * nq + qi
        lane = lax.broadcasted_iota(jnp.int32, (bq, LANES), 1)
        is_a = lane < HEAD_DIM
        q2 = q_ref[0] * jnp.asarray(HEAD_DIM ** -0.5, BF16)
        qs = (jnp.where(is_a, q2, jnp.zeros_like(q2)), jnp.where(is_a, jnp.zeros_like(q2), q2))
        m_ext = m_ref[...]
        row = lax.broadcasted_iota(jnp.int32, (bq, bq), 0)
        col = lax.broadcasted_iota(jnp.int32, (bq, bq), 1)
        diag_mask = col < row

        def block(kj, carry, mask):
            off = pl.multiple_of(kj * bq, bq)
            k2 = k_ref[0, pl.ds(off, bq), :]
            v2 = v_ref[0, pl.ds(off, bq), :]
            out = []
            for h in range(2):
                rem, acc = carry[h]
                z = _nt(qs[h], k2)
                sp, ls = _softplus_parts(z)
                lk = -sp if mask is None else jnp.where(mask, -sp, 0.0)
                sums = _keysums(lk, m_ext)
                parts = [None] * nsub
                for j in reversed(range(nsub)):
                    suf, total = sums[j]
                    parts[j] = jnp.exp(ls[:, j * KBLK:(j + 1) * KBLK] + suf + rem)
                    rem = rem + total
                a = jnp.concatenate(parts, axis=1)
                if mask is not None:
                    a = jnp.where(mask, a, 0.0)
                out.append((rem, acc + _nn(a.astype(BF16), v2)))
            return tuple(out)

        def most_left(c):
            return jnp.maximum(jnp.max(c[0][0]), jnp.max(c[1][0]))

        def more(s):
            return (s[0] < qi) & (s[1] > STICK_GONE)

        def step(s):
            c = block(qi - 1 - s[0], s[2], None)
            return s[0] + 1, most_left(c), c

        zero = jnp.zeros((bq, LANES), F32)
        carry = block(qi, ((zero, zero), (zero, zero)), diag_mask)
        n_left, _, carry = lax.while_loop(more, step, (jnp.int32(0), most_left(carry), carry))
        o_ref[0] = jnp.where(is_a, carry[0][1], carry[1][1]).astype(BF16)
        tot_ref[...] = jnp.where(is_a, carry[0][0], carry[1][0])
        cnt_ref[step_id] = n_left.astype(F32)

    upp = qkv.shape[0] // 3
    return pl.pallas_call(
        body, name="attn_fwd", grid=(n_seq, n_pairs, nq),
        in_specs=[pl.BlockSpec((1, bq, LANES), lambda b, p, i: (p // 2, b * nq + i, p % 2)),
                  pl.BlockSpec((1, seq, LANES), lambda b, p, i: (upp + p // 2, b, p % 2)),
                  pl.BlockSpec((1, seq, LANES), lambda b, p, i: (2 * upp + p // 2, b, p % 2)),
                  pl.BlockSpec((2 * KBLK, 2 * KBLK), lambda b, p, i: (0, 0))],
        out_specs=[pl.BlockSpec((1, bq, LANES), lambda b, p, i: (p // 2, b * nq + i, p % 2)),
                   pl.BlockSpec((bq, LANES), lambda b, p, i: (b * nq + i, p)),
                   pl.BlockSpec(memory_space=pltpu.SMEM)],
        out_shape=[jax.ShapeDtypeStruct((upp, t, 2 * LANES), BF16), jax.ShapeDtypeStruct((t, n_pairs * LANES), F32),
                   jax.ShapeDtypeStruct((n_seq * n_pairs * nq,), F32)],
        compiler_params=_cparams(("arbitrary", "arbitrary", "arbitrary")),
    )(qkv, qkv, qkv, suffix_m)


def attn_bwd(qkv, do, tot, cnt, n_seq, seq):
    t = qkv.shape[1]
    upp = qkv.shape[0] // 3
    n_pairs = upp * 2
    bq = min(ATT_BLOCK, seq)
    nq = seq // bq
    nsub = bq // KBLK
    _, prefix_m = _sum_matrices()
    scale = HEAD_DIM ** -0.5

    def body(q_ref, k_ref, v_ref, do_ref, tot_ref, m_ref, cnt_ref, dq_ref, dk_ref, dv_ref, dk_acc, dv_acc):
        qi = pl.program_id(2)
        step_id = (pl.program_id(0) * n_pairs + pl.program_id(1)) * nq + qi
        n_left = jnp.clip(cnt_ref[step_id].astype(jnp.int32), 0, qi)
        lane = lax.broadcasted_iota(jnp.int32, (bq, LANES), 1)
        is_a = lane < HEAD_DIM

        def halves(v2):
            z2 = jnp.zeros_like(v2)
            return jnp.where(is_a, v2, z2), jnp.where(is_a, z2, v2)

        qs = halves(q_ref[0] * jnp.asarray(scale, BF16))
        dos = halves(do_ref[0])
        tot2 = tot_ref[...]
        swapped = pltpu.roll(tot2, HEAD_DIM, 1)
        tots = (jnp.where(is_a, tot2, swapped), jnp.where(is_a, swapped, tot2))
        m_ext = m_ref[...]
        row = lax.broadcasted_iota(jnp.int32, (bq, bq), 0)
        col = lax.broadcasted_iota(jnp.int32, (bq, bq), 1)
        diag_mask = col < row

        @pl.when(qi == 0)
        def _():
            dk_acc[...] = jnp.zeros_like(dk_acc)
            dv_acc[...] = jnp.zeros_like(dv_acc)

        def block(kj, carry, mask):
            off = pl.multiple_of(kj * bq, bq)
            k2 = k_ref[0, pl.ds(off, bq), :]
            v2 = v_ref[0, pl.ds(off, bq), :]
            ks = halves(k2)
            dq = carry[2]
            dk_part = jnp.zeros((bq, LANES), F32)
            dv_part = jnp.zeros((bq, LANES), F32)
            out = []
            for h in range(2):
                pre, gpre = carry[h]
                z = _nt(qs[h], k2)
                sp, ls = _softplus_parts(z)
                lk = -sp if mask is None else jnp.where(mask, -sp, 0.0)
                sums = _keysums(lk, m_ext)
                parts = []
                for j in range(nsub):
                    pin, ptot = sums[j]
                    parts.append(jnp.exp(ls[:, j * KBLK:(j + 1) * KBLK] + (tots[h] - (pre + pin))))
                    pre = pre + ptot
                a = jnp.concatenate(parts, axis=1)
                if mask is not None:
                    a = jnp.where(mask, a, 0.0)
                g = a * _nt(dos[h], v2)
                gsums = _keysums(g, m_ext)
                parts = []
                for j in range(nsub):
                    gin, gtot = gsums[j]
                    parts.append(gpre + gin)
                    gpre = gpre + gtot
                dz = g - jnp.exp(ls) * jnp.concatenate(parts, axis=1)
                if mask is not None:
                    dz = jnp.where(mask, dz, 0.0)
                dzb = dz.astype(BF16)
                dq = dq + _nn(dzb, ks[h])
                dk_part = dk_part + _tn(dzb, qs[h])
                dv_part = dv_part + _tn(a.astype(BF16), dos[h])
                out.append((pre, gpre))
            dk_acc[pl.ds(off, bq), :] += dk_part
            dv_acc[pl.ds(off, bq), :] += dv_part
            return (out[0], out[1], dq)

        zero = jnp.zeros((bq, LANES), F32)
        carry = lax.fori_loop(qi - n_left, qi, lambda kj, c: block(kj, c, None), ((zero, zero), (zero, zero), zero))
        carry = block(qi, carry, diag_mask)
        dq_ref[0] = (carry[2] * scale).astype(BF16)

        @pl.when(qi == nq - 1)
        def _():
            dk_ref[0] = dk_acc[...].astype(BF16)
            dv_ref[0] = dv_acc[...].astype(BF16)

    qblk = lambda b, p, i: (p // 2, b * nq + i, p % 2)
    kv_out = pl.BlockSpec((1, seq, LANES), lambda b, p, i: (p // 2, b, p % 2))
    shp = jax.ShapeDtypeStruct((upp, t, 2 * LANES), BF16)
    return pl.pallas_call(
        body, name="attn_bwd", grid=(n_seq, n_pairs, nq),
        in_specs=[pl.BlockSpec((1, bq, LANES), qblk),
                  pl.BlockSpec((1, seq, LANES), lambda b, p, i: (upp + p // 2, b, p % 2)),
                  pl.BlockSpec((1, seq, LANES), lambda b, p, i: (2 * upp + p // 2, b, p % 2)),
                  pl.BlockSpec((1, bq, LANES), qblk),
                  pl.BlockSpec((bq, LANES), lambda b, p, i: (b * nq + i, p)),
                  pl.BlockSpec((2 * KBLK, 2 * KBLK), lambda b, p, i: (0, 0)),
                  pl.BlockSpec(memory_space=pltpu.SMEM)],
        out_specs=[pl.BlockSpec((1, bq, LANES), qblk), kv_out, kv_out],
        out_shape=[shp, shp, shp],
        scratch_shapes=[pltpu.VMEM((seq, LANES), F32), pltpu.VMEM((seq, LANES), F32)],
        compiler_params=_cparams(("parallel", "parallel", "arbitrary")),
    )(qkv, qkv, qkv, do, tot, prefix_m, cnt)


def _ln_stats(v):
    mu = jnp.mean(v, axis=-1, keepdims=True)
    vc = v - mu
    rstd = lax.rsqrt(jnp.mean(vc * vc, axis=-1, keepdims=True) + EPS)
    return vc * rstd, rstd


def _glu_into(a0_ref, av_ref, ag_ref, hv_ref, hg_ref, first):
    hv = hv_ref[0].astype(F32)
    hg = hg_ref[0].astype(F32)
    a0_ref[0:HALO, :] = jnp.where(first, 0.0, hv * _sigmoid(hg))
    av = av_ref[0].astype(F32)
    ag = ag_ref[0].astype(F32)
    a0_ref[HALO:, :] = av * _sigmoid(ag)


def _tril_mask():
    r = lax.broadcasted_iota(jnp.int32, (CHUNK, CHUNK), 0)
    c = lax.broadcasted_iota(jnp.int32, (CHUNK, CHUNK), 1)
    return c <= r


def mix_fwd(z, conv_w, conv_b, ln_a_g, ln_a_b, ln_v_g, ln_v_b, sp_w, sp_bt, seq):
    _, t, c = z.shape
    tm = _tile(seq, 512)
    tiles_per_seq = seq // tm
    groups = c // LANES
    hb = tm // HALO

    def body(av_ref, ag_ref, u_ref, v_ref, hv_ref, hg_ref, cw_ref, cb_ref, lag_ref, lab_ref, lvg_ref, lvb_ref,
             spw_ref, spb_ref, cat_ref, a1_ref, a0_ref):
        i = pl.program_id(0)
        _glu_into(a0_ref, av_ref, ag_ref, hv_ref, hg_ref, i % tiles_per_seq == 0)
        acc = jnp.zeros((tm, c), F32) + cb_ref[...]
        for k in range(CONV_WIDTH):
            acc = acc + cw_ref[k:k + 1, :] * a0_ref[pl.ds(HALO - (CONV_WIDTH - 1) + k, tm), :]
        a1_ref[...] = acc
        xh, _ = _ln_stats(acc)
        a2 = xh * lag_ref[...] + lab_ref[...]
        a3 = (a2 * _sigmoid(a2)).astype(BF16)
        half = c // 2
        cat_ref[0] = a3[:, :half]
        cat_ref[1] = a3[:, half:]
        tril = _tril_mask()
        for g in range(groups):
            sl = slice(g * LANES, (g + 1) * LANES)
            xh, _ = _ln_stats(v_ref[0][:, sl].astype(F32))
            vn = (xh * lvg_ref[:, sl] + lvb_ref[:, sl]).astype(BF16)
            w = jnp.where(tril, spw_ref[g], 0.0).astype(BF16)
            bias = spb_ref[:, g:g + 1]
            for ch in range(tm // CHUNK):
                rows = slice(ch * CHUNK, (ch + 1) * CHUNK)
                vs = _nn(w, vn[rows]) + bias
                bo = (u_ref[0][rows, sl].astype(F32) * vs).astype(BF16)
                cat_ref[2 + (g * LANES) // half, rows, (g * LANES) % half:(g * LANES) % half + LANES] = bo

    unit = lambda u: pl.BlockSpec((1, tm, c), lambda i: (u, i, 0))
    halo = lambda u: pl.BlockSpec((1, HALO, c), lambda i: (u, jnp.maximum(i * hb - 1, 0), 0))
    vec = pl.BlockSpec((1, c), lambda i: (0, 0))
    return pl.pallas_call(
        body, name="mix_fwd", grid=(t // tm,),
        in_specs=[unit(0), unit(1), unit(2), unit(3), halo(0), halo(1),
                  pl.BlockSpec((CONV_WIDTH, c), lambda i: (0, 0)), vec, vec, vec, vec, vec,
                  pl.BlockSpec((groups, CHUNK, CHUNK), lambda i: (0, 0, 0)),
                  pl.BlockSpec((CHUNK, groups), lambda i: (0, 0))],
        out_specs=[pl.BlockSpec((4, tm, c // 2), lambda i: (0, i, 0)), pl.BlockSpec((tm, c), lambda i: (i, 0))],
        out_shape=[jax.ShapeDtypeStruct((4, t, c // 2), BF16), jax.ShapeDtypeStruct((t, c), F32)],
        scratch_shapes=[pltpu.VMEM((HALO + tm, c), F32)],
        compiler_params=_cparams(("parallel",)),
    )(z, z, z, z, z, z, conv_w, conv_b, ln_a_g, ln_a_b, ln_v_g, ln_v_b, sp_w, sp_bt)


def mix_bwd_point(dcat, z, a1, ln_a_g, ln_a_b, ln_v_g, ln_v_b, sp_w, sp_wt, sp_bt, seq):
    _, t, c = z.shape
    tm = _tile(seq, 512)
    groups = c // LANES
    half = c // 2

    def body(dc_ref, u_ref, v_ref, a1_ref, lag_ref, lab_ref, lvg_ref, lvb_ref, spw_ref, spwt_ref, spb_ref,
             dz_ref, da1_ref, dcb_ref, dlag_ref, dlab_ref, dlvg_ref, dlvb_ref, dspw_ref, dspb_ref):
        i = pl.program_id(0)
        last = pl.num_programs(0) - 1

        @pl.when(i == 0)
        def _():
            for r in (dcb_ref, dlag_ref, dlab_ref, dlvg_ref, dlvb_ref, dspw_ref, dspb_ref):
                r[...] = jnp.zeros_like(r)

        da3 = jnp.concatenate([dc_ref[0], dc_ref[1]], axis=-1)
        xh, rstd = _ln_stats(a1_ref[...])
        a2 = xh * lag_ref[...] + lab_ref[...]
        s = _sigmoid(a2)
        da2 = da3 * (s * (1.0 + a2 * (1.0 - s)))
        dlag_ref[...] += jnp.sum(da2 * xh, axis=0, keepdims=True)
        dlab_ref[...] += jnp.sum(da2, axis=0, keepdims=True)
        dxh = da2 * lag_ref[...]
        da1 = rstd * (dxh - jnp.mean(dxh, axis=-1, keepdims=True) - xh * jnp.mean(dxh * xh, axis=-1, keepdims=True))
        da1_ref[...] = da1
        dcb_ref[...] += jnp.sum(da1, axis=0, keepdims=True)

        tril = _tril_mask()
        for g in range(groups):
            sl = slice(g * LANES, (g + 1) * LANES)
            xh, rstd = _ln_stats(v_ref[0][:, sl].astype(F32))
            lg = lvg_ref[:, sl]
            vnb = (xh * lg + lvb_ref[:, sl]).astype(BF16)
            w = jnp.where(tril, spw_ref[g], 0.0).astype(BF16)
            wt = jnp.where(tril.T, spwt_ref[g], 0.0).astype(BF16)
            bias = spb_ref[:, g:g + 1]
            dbo_all = dc_ref[2 + (g * LANES) // half][:, (g * LANES) % half:(g * LANES) % half + LANES]
            dvn_parts = []
            dw_acc = jnp.zeros((CHUNK, CHUNK), F32)
            db_acc = jnp.zeros((CHUNK, LANES), F32)
            for ch in range(tm // CHUNK):
                rows = slice(ch * CHUNK, (ch + 1) * CHUNK)
                vs = _nn(w, vnb[rows]) + bias
                dbo = dbo_all[rows]
                uv = u_ref[0][rows, sl].astype(F32)
                dz_ref[0, rows, sl] = (dbo * vs).astype(BF16)
                dvs = dbo * uv
                dvsb = dvs.astype(BF16)
                dvn_parts.append(_nn(wt, dvsb))
                dw_acc = dw_acc + _nt(dvsb, vnb[rows])
                db_acc = db_acc + dvs
            dvn = jnp.concatenate(dvn_parts, axis=0)
            dspw_ref[g] += jnp.where(tril, dw_acc, 0.0)
            dspb_ref[g] += db_acc
            dlvg_ref[:, sl] += jnp.sum(dvn * xh, axis=0, keepdims=True)
            dlvb_ref[:, sl] += jnp.sum(dvn, axis=0, keepdims=True)
            dxh = dvn * lg
            dv = rstd * (dxh - jnp.mean(dxh, axis=-1, keepdims=True) - xh * jnp.mean(dxh * xh, axis=-1, keepdims=True))
            dz_ref[1, :, sl] = dv.astype(BF16)

        @pl.when(i == last)
        def _():
            for g in range(groups):
                dspb_ref[g] = jnp.zeros((CHUNK, LANES), F32) + jnp.sum(dspb_ref[g], axis=-1, keepdims=True)

    unit = lambda u: pl.BlockSpec((1, tm, c), lambda i: (u, i, 0))
    vec = pl.BlockSpec((1, c), lambda i: (0, 0))
    sq = pl.BlockSpec((groups, CHUNK, CHUNK), lambda i: (0, 0, 0))
    vshape = jax.ShapeDtypeStruct((1, c), F32)
    sshape = jax.ShapeDtypeStruct((groups, CHUNK, CHUNK), F32)
    return pl.pallas_call(
        body, name="mix_bwd_point", grid=(t // tm,),
        in_specs=[pl.BlockSpec((4, tm, half), lambda i: (0, i, 0)), unit(2), unit(3),
                  pl.BlockSpec((tm, c), lambda i: (i, 0)), vec, vec, vec, vec, sq, sq,
                  pl.BlockSpec((CHUNK, groups), lambda i: (0, 0))],
        out_specs=[pl.BlockSpec((2, tm, c), lambda i: (1, i, 0)), pl.BlockSpec((tm, c), lambda i: (i, 0)),
                   vec, vec, vec, vec, vec, sq, sq],
        out_shape=[jax.ShapeDtypeStruct((4, t, c), BF16), jax.ShapeDtypeStruct((t, c), F32),
                   vshape, vshape, vshape, vshape, vshape, sshape, sshape],
        compiler_params=_cparams(("arbitrary",)),
    )(dcat, z, z, a1, ln_a_g, ln_a_b, ln_v_g, ln_v_b, sp_w, sp_wt, sp_bt)


def mix_bwd_conv(dz, da1, z, conv_w, seq):
    _, t, c = z.shape
    tm = _tile(seq, 512)
    tiles_per_seq = seq // tm
    hb = tm // HALO
    n_halo_blocks = t // HALO

    def body(dz_in_ref, d_ref, dh_ref, av_ref, ag_ref, hv_ref, hg_ref, cw_ref, dz_ref, dcw_ref, a0_ref, d1_ref):
        del dz_in_ref
        i = pl.program_id(0)
        _glu_into(a0_ref, av_ref, ag_ref, hv_ref, hg_ref, i % tiles_per_seq == 0)
        d1_ref[0:tm, :] = d_ref[...]
        d1_ref[tm:, :] = jnp.where((i + 1) % tiles_per_seq == 0, 0.0, dh_ref[...])

        @pl.when(i == 0)
        def _():
            dcw_ref[...] = jnp.zeros_like(dcw_ref)

        d1 = d_ref[...]
        da0 = jnp.zeros((tm, c), F32)
        for k in range(CONV_WIDTH):
            back = CONV_WIDTH - 1 - k
            da0 = da0 + cw_ref[k:k + 1, :] * d1_ref[pl.ds(back, tm), :]
            dcw_ref[k:k + 1, :] += jnp.sum(d1 * a0_ref[pl.ds(HALO - back, tm), :], axis=0, keepdims=True)
        av = av_ref[0].astype(F32)
        s = _sigmoid(ag_ref[0].astype(F32))
        dz_ref[0] = (da0 * s).astype(BF16)
        dz_ref[1] = (da0 * av * s * (1.0 - s)).astype(BF16)

    unit = lambda u: pl.BlockSpec((1, tm, c), lambda i: (u, i, 0))
    halo = lambda u: pl.BlockSpec((1, HALO, c), lambda i: (u, jnp.maximum(i * hb - 1, 0), 0))
    return pl.pallas_call(
        body, name="mix_bwd_conv", grid=(t // tm,),
        in_specs=[pl.BlockSpec(memory_space=pl.ANY), pl.BlockSpec((tm, c), lambda i: (i, 0)),
                  pl.BlockSpec((HALO, c), lambda i: (jnp.minimum((i + 1) * hb, n_halo_blocks - 1), 0)),
                  unit(0), unit(1), halo(0), halo(1), pl.BlockSpec((CONV_WIDTH, c), lambda i: (0, 0))],
        out_specs=[pl.BlockSpec((2, tm, c), lambda i: (0, i, 0)), pl.BlockSpec((CONV_WIDTH, c), lambda i: (0, 0))],
        out_shape=[jax.ShapeDtypeStruct(dz.shape, BF16), jax.ShapeDtypeStruct((CONV_WIDTH, c), F32)],
        scratch_shapes=[pltpu.VMEM((HALO + tm, c), F32), pltpu.VMEM((tm + HALO, c), F32)],
        input_output_aliases={0: 0},
        compiler_params=_cparams(("arbitrary",)),
    )(dz, da1, da1, z, z, z, z, conv_w)


CHIP_FLIPS = ((1, 0), (0, 1), (1, 1))
ANY = pl.BlockSpec(memory_space=pl.ANY)


def _place():
    return lax.axis_index("x"), lax.axis_index("y"), lax.axis_index("c")


def _flip(v, f):
    return 1 - v if f else v


def place_shard(w, chip, dtype, name):
    r, cc = w.shape
    rb = _tile(r, 512)

    def body(chip_ref, w_ref, o_ref):
        del chip_ref
        o_ref[0] = w_ref[...].astype(dtype)

    return pl.pallas_call(
        body, name=name,
        grid_spec=pltpu.PrefetchScalarGridSpec(
            num_scalar_prefetch=1, grid=(r // rb,),
            in_specs=[pl.BlockSpec((rb, cc), lambda i, chip_ref: (i, 0))],
            out_specs=pl.BlockSpec((1, rb, cc), lambda i, chip_ref: (chip_ref[0], i, 0))),
        out_shape=jax.ShapeDtypeStruct((N_CHIPS, r, cc), dtype),
        compiler_params=_cparams(("parallel",)),
    )(chip, w)


class Carry:
    def __init__(self, arrays, out_shapes, aliased, sem_shapes, start, finish):
        self.arrays, self.out_shapes, self.aliased, self.sem_shapes = list(arrays), list(out_shapes), aliased, list(sem_shapes)
        self.start, self.finish = start, finish


def _call(body, *, name, grid, in_specs, out_specs, out_shape, args, sem, scratch_shapes=(), carry=None):
    if carry is None:
        res = pl.pallas_call(body, name=name, grid=grid, in_specs=in_specs, out_specs=out_specs, out_shape=out_shape,
                             scratch_shapes=list(scratch_shapes), compiler_params=_cparams(sem))(*args)
        return list(res), []
    n_in, n_out, n_scr, nc = len(args), len(out_shape), len(scratch_shapes), len(carry.arrays)

    def full_body(*refs):
        ins, refs = refs[:n_in], refs[n_in:]
        c_ins, refs = refs[:nc], refs[nc:]
        outs, refs = refs[:n_out], refs[n_out:]
        c_outs, refs = refs[:nc], refs[nc:]
        scr, sems = refs[:n_scr], refs[n_scr:]
        first = functools.reduce(lambda a, b: a & b, [pl.program_id(d) == 0 for d in range(len(grid))])
        last = functools.reduce(lambda a, b: a & b, [pl.program_id(d) == grid[d] - 1 for d in range(len(grid))])

        @pl.when(first)
        def _():
            carry.start(c_ins, c_outs, sems)

        body(*ins, *outs, *scr)

        @pl.when(last)
        def _():
            carry.finish(c_ins, c_outs, sems)

    res = pl.pallas_call(
        full_body, name=name, grid=grid, in_specs=list(in_specs) + [ANY] * nc, out_specs=list(out_specs) + [ANY] * nc,
        out_shape=list(out_shape) + carry.out_shapes, scratch_shapes=list(scratch_shapes) + carry.sem_shapes,
        input_output_aliases={n_in + i: n_out + i for i in range(nc)} if carry.aliased else {},
        compiler_params=pltpu.CompilerParams(dimension_semantics=("arbitrary",) * len(grid), vmem_limit_bytes=VMEM_LIMIT,
                                             has_side_effects=True),
    )(*args, *carry.arrays)
    return list(res[:n_out]), list(res[n_out:])


def _gather_ops(shapes, whole):
    n = len(shapes)

    def rows(a, c):
        hr = shapes[a][1] // 2
        return pl.ds(pl.multiple_of(c * hr, 16), hr)

    def start(ins, outs, sems):
        ici_send, ici_recv = sems[0], sems[1]
        x, y, c = _place()
        k = 2 * x + y
        for a in range(n):
            for o, (fx, fy) in enumerate(CHIP_FLIPS):
                src = ins[a].at[k] if whole[a] else ins[a].at[k, rows(a, c)]
                dst = outs[a].at[k] if whole[a] else outs[a].at[k, rows(a, c)]
                pltpu.make_async_remote_copy(
                    src_ref=src, dst_ref=dst, send_sem=ici_send.at[3 * a + o], recv_sem=ici_recv.at[3 * a + o],
                    device_id=(_flip(x, fx), _flip(y, fy), c), device_id_type=MESH).start()

    def finish(ins, outs, sems):
        ici_send, ici_recv, d2d_send, d2d_recv = sems
        x, y, c = _place()
        k = 2 * x + y
        sibling = (x, y, 1 - c)

        def copy(ref, send, recv, a, o):
            return pltpu.make_async_remote_copy(src_ref=ref, dst_ref=ref, send_sem=send.at[3 * a + o],
                                                recv_sem=recv.at[3 * a + o], device_id=sibling, device_id_type=MESH)

        for a in range(n):
            for o, (fx, fy) in enumerate(CHIP_FLIPS):
                kk = 2 * _flip(x, fx) + _flip(y, fy)
                landed = outs[a].at[kk] if whole[a] else outs[a].at[kk, rows(a, c)]
                copy(landed, ici_send, ici_recv, a, o).wait_recv()
                if not whole[a]:
                    copy(landed, d2d_send, d2d_recv, a, o).start()
        for a in range(n):
            for o, (fx, fy) in enumerate(CHIP_FLIPS):
                kk = 2 * _flip(x, fx) + _flip(y, fy)
                mine = ins[a].at[k] if whole[a] else ins[a].at[k, rows(a, c)]
                copy(mine, ici_send, ici_recv, a, o).wait_send()
                if not whole[a]:
                    copy(outs[a].at[kk, rows(a, 1 - c)], d2d_send, d2d_recv, a, o).wait_recv()
                    copy(outs[a].at[kk, rows(a, c)], d2d_send, d2d_recv, a, o).wait_send()

    dma = pltpu.SemaphoreType.DMA
    return start, finish, [dma((3 * n,))] * 4


def gather_carry(bufs):
    start, finish, sems = _gather_ops([b.shape for b in bufs], [False] * len(bufs))
    return Carry(bufs, [jax.ShapeDtypeStruct(b.shape, b.dtype) for b in bufs], True, sems, start, finish)


def allgather_weights(shards, smalls):
    bufs = list(shards) + list(smalls)
    n = len(bufs)
    start, finish, sems = _gather_ops([b.shape for b in bufs], [False] * len(shards) + [True] * len(smalls))

    def body(*refs):
        start(refs[:n], refs[n:2 * n], refs[2 * n:])
        finish(refs[:n], refs[n:2 * n], refs[2 * n:])

    res = pl.pallas_call(
        body, name="allgather_weights", in_specs=[ANY] * n, out_specs=[ANY] * n,
        out_shape=[jax.ShapeDtypeStruct(b.shape, b.dtype) for b in bufs], scratch_shapes=sems,
        input_output_aliases={i: i for i in range(n)},
        compiler_params=pltpu.CompilerParams(has_side_effects=True),
    )(*bufs)
    return res[:len(shards)], res[len(shards):]


def rs_exchange(grads):
    n = len(grads)

    def body(*refs):
        ins, outs = refs[:n], refs[n:2 * n]
        send, recv = refs[2 * n:]
        x, y, c = _place()
        cps = []
        for a in range(n):
            cp = pltpu.make_async_remote_copy(
                src_ref=ins[a].at[:, 1 - c], dst_ref=outs[a], send_sem=send.at[a], recv_sem=recv.at[a],
                device_id=(x, y, 1 - c), device_id_type=MESH)
            cp.start()
            cps.append(cp)
        for cp in cps:
            cp.wait()

    dma = pltpu.SemaphoreType.DMA
    return pl.pallas_call(
        body, name="rs_exchange", in_specs=[ANY] * n, out_specs=[ANY] * n,
        out_shape=[jax.ShapeDtypeStruct((g.shape[0],) + g.shape[2:], g.dtype) for g in grads],
        scratch_shapes=[dma((n,)), dma((n,))],
        compiler_params=pltpu.CompilerParams(has_side_effects=True),
    )(*grads)


def rs_add(g, sib, core, out_dtype, name):
    nk, _, hr, cc = g.shape
    rb = _tile(hr, 256)

    def body(core_ref, g_ref, s_ref, o_ref):
        del core_ref
        o_ref[0] = (g_ref[0, 0] + s_ref[0]).astype(out_dtype)

    return pl.pallas_call(
        body, name=name,
        grid_spec=pltpu.PrefetchScalarGridSpec(
            num_scalar_prefetch=1, grid=(nk, hr // rb),
            in_specs=[pl.BlockSpec((1, 1, rb, cc), lambda k, i, core_ref: (k, core_ref[0], i, 0)),
                      pl.BlockSpec((1, rb, cc), lambda k, i, core_ref: (k, i, 0))],
            out_specs=pl.BlockSpec((1, rb, cc), lambda k, i, core_ref: (k, i, 0))),
        out_shape=jax.ShapeDtypeStruct((nk, hr, cc), out_dtype),
        compiler_params=_cparams(("parallel", "parallel")),
    )(core, g, sib)


def send_carry(parts):
    n = len(parts)

    def copies(ins, outs, sems):
        x, y, c = _place()
        for a in range(n):
            for o, (fx, fy) in enumerate(CHIP_FLIPS):
                kk = 2 * _flip(x, fx) + _flip(y, fy)
                yield pltpu.make_async_remote_copy(
                    src_ref=ins[a].at[kk], dst_ref=outs[a].at[o], send_sem=sems[0].at[3 * a + o],
                    recv_sem=sems[1].at[3 * a + o], device_id=(_flip(x, fx), _flip(y, fy), c), device_id_type=MESH)

    def start(ins, outs, sems):
        for cp in copies(ins, outs, sems):
            cp.start()

    def finish(ins, outs, sems):
        for cp in copies(ins, outs, sems):
            cp.wait()

    dma = pltpu.SemaphoreType.DMA
    return Carry(parts, [jax.ShapeDtypeStruct((3,) + p.shape[1:], p.dtype) for p in parts], False,
                 [dma((3 * n,)), dma((3 * n,))], start, finish)


def rs_sum(recv, part, where, full, layer, n_layers, name):
    _, hr, cc = recv.shape
    rb = _tile(hr, 256)

    def body(*refs):
        r_ref, p_ref, o_ref = refs[1], refs[2], refs[-1]
        o_ref[0, 0] = ((p_ref[0].astype(F32) + r_ref[0].astype(F32)) + r_ref[1].astype(F32)) + r_ref[2].astype(F32)

    in_specs = [pl.BlockSpec((3, rb, cc), lambda i, w_ref: (0, i, 0)),
                pl.BlockSpec((1, rb, cc), lambda i, w_ref: (w_ref[0], i, 0))]
    args = [where, recv, part]
    aliases = {}
    if full is not None:
        in_specs.append(ANY)
        args.append(full)
        aliases = {3: 0}
    return pl.pallas_call(
        body, name=name,
        grid_spec=pltpu.PrefetchScalarGridSpec(
            num_scalar_prefetch=1, grid=(hr // rb,), in_specs=in_specs,
            out_specs=pl.BlockSpec((1, 1, rb, cc), lambda i, w_ref: (layer, w_ref[1], i, 0))),
        out_shape=jax.ShapeDtypeStruct((n_layers, 2, hr, cc), F32),
        input_output_aliases=aliases,
        compiler_params=_cparams(("parallel",)),
    )(*args)


def rs_share(fulls):
    n = len(fulls)

    def body(*refs):
        ins, outs = refs[:n], refs[n:2 * n]
        send, recv = refs[2 * n:]
        x, y, c = _place()
        cps = []
        for a in range(n):
            cp = pltpu.make_async_remote_copy(
                src_ref=ins[a].at[:, c], dst_ref=outs[a].at[:, c], send_sem=send.at[a], recv_sem=recv.at[a],
                device_id=(x, y, 1 - c), device_id_type=MESH)
            cp.start()
            cps.append(cp)
        for a in range(n):
            got = outs[a].at[:, 1 - c]
            pltpu.make_async_remote_copy(
                src_ref=got, dst_ref=got, send_sem=send.at[a], recv_sem=recv.at[a],
                device_id=(x, y, 1 - c), device_id_type=MESH).wait_recv()
        for cp in cps:
            cp.wait_send()

    dma = pltpu.SemaphoreType.DMA
    return pl.pallas_call(
        body, name="rs_share", in_specs=[ANY] * n, out_specs=[ANY] * n,
        out_shape=[jax.ShapeDtypeStruct(f.shape, f.dtype) for f in fulls],
        scratch_shapes=[dma((n,)), dma((n,))],
        input_output_aliases={i: i for i in range(n)},
        compiler_params=pltpu.CompilerParams(has_side_effects=True),
    )(*fulls)


def allreduce_small(v):
    r, w = v.shape

    def body(v_ref, o_ref, buf, send, recv, loc):
        x, y, c = _place()
        me = 4 * x + 2 * y + c
        mine = pltpu.make_async_copy(v_ref, buf.at[me], loc)
        mine.start()
        cps = []
        for o in range(1, N_DEV):
            fx, fy, fc = (o >> 2) & 1, (o >> 1) & 1, o & 1
            cp = pltpu.make_async_remote_copy(
                src_ref=v_ref, dst_ref=buf.at[me], send_sem=send.at[o - 1], recv_sem=recv.at[o - 1],
                device_id=(_flip(x, fx), _flip(y, fy), _flip(c, fc)), device_id_type=MESH)
            cp.start()
            cps.append(cp)
        for o in range(1, N_DEV):
            fx, fy, fc = (o >> 2) & 1, (o >> 1) & 1, o & 1
            peer = 4 * _flip(x, fx) + 2 * _flip(y, fy) + _flip(c, fc)
            pltpu.make_async_remote_copy(
                src_ref=v_ref, dst_ref=buf.at[peer], send_sem=send.at[o - 1], recv_sem=recv.at[o - 1],
                device_id=(x, y, c), device_id_type=MESH).wait_recv()
        for cp in cps:
            cp.wait_send()
        mine.wait()
        acc = buf[0]
        for d in range(1, N_DEV):
            acc = acc + buf[d]
        o_ref[...] = acc

    dma = pltpu.SemaphoreType.DMA
    vm = pl.BlockSpec(memory_space=pltpu.VMEM)
    return pl.pallas_call(
        body, name="allreduce_small", in_specs=[vm], out_specs=vm,
        out_shape=jax.ShapeDtypeStruct((r, w), F32),
        scratch_shapes=[pltpu.VMEM((N_DEV, r, w), F32), dma((N_DEV - 1,)), dma((N_DEV - 1,)), dma],
        compiler_params=pltpu.CompilerParams(has_side_effects=True, vmem_limit_bytes=VMEM_LIMIT),
    )(v)


def adamw(w, g, m, v, name):
    r, cc = w.shape
    rb = _tile(r, 256)

    def body(w_ref, g_ref, m_ref, v_ref, d_ref, nm_ref, nv_ref):
        gv = g_ref[...]
        nm = ADAM_B1 * m_ref[...] + (1.0 - ADAM_B1) * gv
        nv = ADAM_B2 * v_ref[...] + (1.0 - ADAM_B2) * (gv * gv)
        m_hat = nm / (1.0 - ADAM_B1 ** ADAM_STEP)
        v_hat = nv / (1.0 - ADAM_B2 ** ADAM_STEP)
        d_ref[...] = -ADAM_LR * (m_hat / (jnp.sqrt(v_hat) + ADAM_EPS) + ADAM_WD * w_ref[...])
        nm_ref[...] = nm
        nv_ref[...] = nv

    blk = pl.BlockSpec((rb, cc), lambda i: (i, 0))
    shp = jax.ShapeDtypeStruct((r, cc), F32)
    return pl.pallas_call(
        body, name=name, grid=(r // rb,), in_specs=[blk] * 4, out_specs=[blk] * 3, out_shape=[shp] * 3,
        compiler_params=_cparams(("parallel",)),
    )(w, g, m, v)


WEIGHTS = ['g_ffn1', 'w_ffn1_gate', 'w_ffn1_up', 'w_ffn1_down', 'g_mix', 'w_in_ab', 'conv_w', 'conv_b', 'ln_a_g',
           'ln_a_b', 'ln_v_g', 'ln_v_b', 'sp_w', 'sp_b', 'w_out_ab', 'w_qkv', 'w_o', 'g_ffn2', 'w_ffn2_gate',
           'w_ffn2_up', 'w_ffn2_down', 'g_final']
BIG = ['w_ffn1_gate', 'w_ffn1_up', 'w_ffn1_down', 'w_in_ab', 'w_out_ab', 'w_qkv', 'w_o', 'w_ffn2_gate', 'w_ffn2_up',
       'w_ffn2_down']
SMALL = ['g_ffn1', 'g_mix', 'g_ffn2', 'g_final', 'conv_b', 'ln_a_g', 'ln_a_b', 'ln_v_g', 'ln_v_b', 'sp_b', 'sp_w']


CARRY_WEIGHTS = {"ffn_gateup": 9.2e6, "ffn_down": 6.1e6, "mm_in": 5.9e6, "mm_out": 3.3e6}


def _use_order(depth):
    order = []
    for layer in range(depth):
        order += [('w_ffn1_gate', layer), ('w_ffn1_up', layer), ('w_ffn1_down', layer)]
        order += [('w_in_ab', layer // 2), ('w_out_ab', layer // 2)] if layer % 2 == 0 else [('w_qkv', layer // 2), ('w_o', layer // 2)]
        order += [('w_ffn2_gate', layer), ('w_ffn2_up', layer), ('w_ffn2_down', layer)]
    return order


def _rows(a):
    return a.reshape(-1, LANES)


def _pack(parts):
    v = jnp.concatenate([_rows(p) for p in parts], axis=0)
    pad = (-v.shape[0]) % 8
    return jnp.pad(v, ((0, pad), (0, 0)))


def _unpack(v, shapes):
    out, r = [], 0
    for s in shapes:
        n = 1
        for d in s:
            n *= d
        n //= LANES
        out.append(v[r:r + n].reshape(s))
        r += n
    return out


def kernel(x, g_ffn1, w_ffn1_gate, w_ffn1_up, w_ffn1_down, g_mix, w_in_ab, conv_w, conv_b, ln_a_g, ln_a_b, ln_v_g, ln_v_b, sp_w, sp_b, w_out_ab, w_qkv, w_o, g_ffn2, w_ffn2_gate, w_ffn2_up, w_ffn2_down, g_final, loss_target, m_g_ffn1, m_w_ffn1_gate, m_w_ffn1_up, m_w_ffn1_down, m_g_mix, m_w_in_ab, m_conv_w, m_conv_b, m_ln_a_g, m_ln_a_b, m_ln_v_g, m_ln_v_b, m_sp_w, m_sp_b, m_w_out_ab, m_w_qkv, m_w_o, m_g_ffn2, m_w_ffn2_gate, m_w_ffn2_up, m_w_ffn2_down, m_g_final, v_g_ffn1, v_w_ffn1_gate, v_w_ffn1_up, v_w_ffn1_down, v_g_mix, v_w_in_ab, v_conv_w, v_conv_b, v_ln_a_g, v_ln_a_b, v_ln_v_g, v_ln_v_b, v_sp_w, v_sp_b, v_w_out_ab, v_w_qkv, v_w_o, v_g_ffn2, v_w_ffn2_gate, v_w_ffn2_up, v_w_ffn2_down, v_g_final):
    p = dict(locals())
    n_seq, seq, d = x.shape
    t = n_seq * seq
    depth = g_ffn1.shape[0]
    core = lax.axis_index("c")
    chip = 2 * lax.axis_index("x") + lax.axis_index("y")
    xf = x.reshape(t, d)
    target = loss_target.reshape(t, d)

    items = []
    for name in BIG:
        for layer in range(p[name].shape[0]):
            items.append((name, layer))
    chip1 = chip.reshape(1).astype(jnp.int32)
    placed = {it: place_shard(p[it[0]][it[1]], chip1, BF16, "place_shard") for it in items}
    first = [('w_ffn1_gate', 0), ('w_ffn1_up', 0), ('w_ffn1_down', 0)]
    gathered, (conv_w4,) = allgather_weights([placed[it] for it in first],
                                             [place_shard(conv_w[0], chip1, F32, "place_conv_w")])
    wt = dict(zip(first, gathered))
    waiting = [it for it in _use_order(depth) if it not in wt]

    def riders(name):
        room, take = CARRY_WEIGHTS[name], []
        for it in list(waiting):
            if placed[it].size <= room:
                room -= placed[it].size
                take.append(it)
                waiting.remove(it)
        return (take, gather_carry([placed[it] for it in take])) if take else (take, None)

    def landed(take, carried):
        wt.update(zip(take, carried))

    def weight(it):
        if it not in wt:
            waiting.remove(it)
            (wt[it],), _ = allgather_weights([placed[it]], [])
        return wt[it]

    c_mix = conv_w4.shape[2] * N_CHIPS
    conv_full = jnp.transpose(conv_w4, (1, 0, 2)).reshape(CONV_WIDTH, c_mix)
    vec = lambda a: a.reshape(1, -1)
    sp_bt = sp_b[0].T
    sp_wt = jnp.transpose(sp_w[0], (0, 2, 1))
    d_ff = w_ffn1_gate.shape[2]
    n_in = w_in_ab.shape[2]
    n_qkv = w_qkv.shape[2] // 3

    saved = []
    xc = xf
    for layer in range(depth):
        s = {}
        for half, (gn, wn) in enumerate((('g_ffn1', 'w_ffn1'), ('g_ffn2', 'w_ffn2'))):
            if half == 1:
                s['x_mix'] = xc
                s['h_mix'] = rmsnorm_fwd(xc, vec(g_mix[layer]), "norm_mix")
                if layer % 2 == 0:
                    w_in, w_out = weight(('w_in_ab', layer // 2)), weight(('w_out_ab', layer // 2))
                    take, carry = riders("mm_in")
                    (z,), got = colmm(s['h_mix'], [w_in], n_in, BF16, "mm_in", carry)
                    landed(take, got)
                    cat, a1 = mix_fwd(z, conv_full, conv_b, ln_a_g, ln_a_b, vec(ln_v_g), vec(ln_v_b), sp_w[0], sp_bt, seq)
                    s.update(z=z, cat=cat, a1=a1)
                    take, carry = riders("mm_out")
                    xc, got = rowmm([cat], w_out, xc, 1.0, "mm_out", carry)
                    landed(take, got)
                else:
                    (qkv,), _ = colmm(s['h_mix'], [weight(('w_qkv', layer // 2))], n_qkv, BF16, "mm_qkv")
                    o, tot, cnt = attn_fwd(qkv, n_seq, seq)
                    s.update(qkv=qkv, o=o, tot=tot, cnt=cnt)
                    xc, _ = rowmm([o], weight(('w_o', layer // 2)), xc, 1.0, "mm_o")
            s['x' + wn] = xc
            h = rmsnorm_fwd(xc, vec(p[gn][layer]), "norm_ffn")
            w_gate, w_up, w_down = (weight((wn + part, layer)) for part in ('_gate', '_up', '_down'))
            take, carry = riders("ffn_gateup")
            (gate, up), got = colmm(h, [w_gate, w_up], d_ff, BF16, "ffn_gateup", carry)
            landed(take, got)
            take, carry = riders("ffn_down")
            xc, got = rowmm([gate, up], w_down, xc, 0.5, "ffn_down", carry)
            landed(take, got)
            s.update({'h' + wn: h, 'gate' + wn: gate, 'up' + wn: up})
        saved.append(s)

    loss8, dx, dxb, dg_final = loss_head(xc, vec(g_final), target)
    loss = lax.psum(loss8[0, 0], ("x", "y", "c"))

    gw = {}
    gs = {}
    core1 = core.reshape(1).astype(jnp.int32)
    ready = []
    part, recv = {}, {}

    def leaving():
        its = list(ready)
        ready.clear()
        g4 = [gw[it].reshape(N_CHIPS, 2, gw[it].shape[1] // 2, gw[it].shape[2]) for it in its]
        sums = [rs_add(g, sb, core1, REDUCE_DTYPE, "rs_add") for g, sb in zip(g4, rs_exchange(g4))]
        part.update(zip(its, sums))
        return its, send_carry(sums)

    for layer in reversed(range(depth)):
        s = saved[layer]
        for half, (gn, wn) in reversed(list(enumerate((('g_ffn1', 'w_ffn1'), ('g_ffn2', 'w_ffn2'))))):
            wd = wt[(wn + '_down', layer)]
            dgate, dup, act = rowmm_t(dxb, wd, 0.5, BF16, "ffn_bwd_act", gu=(s['gate' + wn], s['up' + wn]))
            gw[(wn + '_down', layer)] = dw_row(act, dxb, 0.5, "ffn_dw_down")
            gw[(wn + '_gate', layer)], gw[(wn + '_up', layer)] = dw_col(s['h' + wn], [dgate, dup], N_CHIPS, d_ff, "ffn_dw_gateup")
            ready.extend([(wn + '_down', layer), (wn + '_gate', layer), (wn + '_up', layer)])
            its, carry = leaving()
            (dx, dxb, dg), got = colmm_t([dgate, dup], [wt[(wn + '_gate', layer)], wt[(wn + '_up', layer)]], d_ff,
                                         s['x' + wn], vec(p[gn][layer]), dx, "ffn_bwd_in", carry)
            recv.update(zip(its, got))
            gs[(gn, layer)] = dg
            if half == 1:
                if layer % 2 == 0:
                    i = layer // 2
                    w_out = wt[('w_out_ab', i)]
                    dcat = rowmm_t(dxb, w_out, 1.0, F32, "mm_out_t")
                    gw[('w_out_ab', i)] = dw_row(s['cat'], dxb, 1.0, "dw_out")
                    dz, da1, dcb, dlag, dlab, dlvg, dlvb, dspw, dspb = mix_bwd_point(
                        dcat, s['z'], s['a1'], ln_a_g, ln_a_b, vec(ln_v_g), vec(ln_v_b), sp_w[0], sp_wt, sp_bt, seq)
                    dz, dcw = mix_bwd_conv(dz, da1, s['z'], conv_full, seq)
                    gs.update({('conv_b', i): dcb, ('ln_a_g', i): dlag, ('ln_a_b', i): dlab, ('ln_v_g', i): dlvg,
                               ('ln_v_b', i): dlvb, ('sp_w', i): dspw, ('sp_b', i): dspb[:, :, 0], ('conv_w', i): dcw})
                    (gw[('w_in_ab', i)],) = dw_col(s['h_mix'], [dz], N_CHIPS, n_in, "dw_in")
                    ready.extend([('w_out_ab', i), ('w_in_ab', i)])
                    its, carry = leaving()
                    (dx, dxb, dg), got = colmm_t([dz], [wt[('w_in_ab', i)]], n_in, s['x_mix'], vec(g_mix[layer]), dx,
                                                 "mm_in_t", carry)
                    recv.update(zip(its, got))
                else:
                    i = layer // 2
                    w_o4 = wt[('w_o', i)]
                    do = rowmm_t(dxb, w_o4, 1.0, BF16, "mm_o_t")
                    gw[('w_o', i)] = dw_row(s['o'], dxb, 1.0, "dw_o")
                    dq, dk, dv = attn_bwd(s['qkv'], do, s['tot'], s['cnt'], n_seq, seq)
                    dqkv = jnp.concatenate([dq, dk, dv], axis=0)
                    (gw[('w_qkv', i)],) = dw_col(s['h_mix'], [dqkv], N_CHIPS, n_qkv, "dw_qkv")
                    ready.extend([('w_o', i), ('w_qkv', i)])
                    its, carry = leaving()
                    (dx, dxb, dg), got = colmm_t([dqkv], [wt[('w_qkv', i)]], n_qkv, s['x_mix'], vec(g_mix[layer]), dx,
                                                 "mm_qkv_t", carry)
                    recv.update(zip(its, got))
                gs[('g_mix', layer)] = dg
    grad_x = dx.reshape(x.shape)

    assert not ready and set(recv) == set(items)
    where = jnp.stack([chip, core]).astype(jnp.int32)
    fulls = []
    for name in BIG:
        full = None
        n_layers = p[name].shape[0]
        for layer in range(n_layers):
            full = rs_sum(recv[(name, layer)], part[(name, layer)], where, full, layer, n_layers, "rs_sum")
        fulls.append(full)
    shared = rs_share(fulls)
    grads = {name: sh.reshape(p[name].shape) for name, sh in zip(BIG, shared)}

    stack = lambda name: jnp.concatenate([gs[(name, layer)].reshape((1,) + p[name].shape[1:]) for layer in range(p[name].shape[0])], axis=0)
    small_g = [stack(name) if name != 'g_final' else dg_final.reshape(p[name].shape) for name in SMALL]
    packed = _pack(small_g + [gs[('conv_w', 0)]])
    red = allreduce_small(packed)
    outs = _unpack(red, [p[name].shape for name in SMALL] + [(CONV_WIDTH, c_mix)])
    for name, g in zip(SMALL, outs[:-1]):
        grads[name] = g
    conv_g = outs[-1].reshape(CONV_WIDTH, N_CHIPS, c_mix // N_CHIPS)
    grads['conv_w'] = lax.dynamic_index_in_dim(conv_g, chip, axis=1, keepdims=False).reshape(conv_w.shape)

    delta, new_m, new_v = {}, {}, {}
    for name in BIG:
        shp = p[name].shape
        two = lambda a: a.reshape(shp[0] * shp[1], shp[2])
        dl, nm, nv = adamw(two(p[name]), two(grads[name]), two(p['m_' + name]), two(p['v_' + name]), "adamw")
        delta[name], new_m[name], new_v[name] = dl.reshape(shp), nm.reshape(shp), nv.reshape(shp)
    small_names = SMALL + ['conv_w']
    pk = lambda pre: _pack([p[pre + name] for name in small_names])
    dl, nm, nv = adamw(pk(''), _pack([grads[name] for name in small_names]), pk('m_'), pk('v_'), "adamw_small")
    shapes = [p[name].shape for name in small_names]
    for dst, val in ((delta, dl), (new_m, nm), (new_v, nv)):
        for name, a in zip(small_names, _unpack(val, shapes)):
            dst[name] = a

    return (loss, grad_x, *[grads[n] for n in WEIGHTS], *[delta[n] for n in WEIGHTS],
            *[new_m[n] for n in WEIGHTS], *[new_v[n] for n in WEIGHTS])
```

```python
import functools

import jax
import jax.numpy as jnp
from jax import lax
from jax.experimental import pallas as pl
from jax.experimental.pallas import tpu as pltpu

F32 = jnp.float32
BF16 = jnp.bfloat16
EPS = 1e-6
HEAD_DIM = 64
CONV_WIDTH = 31
CHUNK = 128
KBLK = 128
ATT_BLOCK = 256
DW_TOKENS = 2048
STICK_GONE = -110.0
LANES = 128
HALO = 32
ADAM_LR, ADAM_B1, ADAM_B2, ADAM_EPS, ADAM_WD, ADAM_STEP = 0.001, 0.9, 0.999, 1e-08, 0.01, 10
VMEM_LIMIT = 56 * 1024 * 1024
MESH = pl.DeviceIdType.MESH
N_CHIPS = 4
N_DEV = 8
REDUCE_DTYPE = BF16


def _cparams(sem):
    return pltpu.CompilerParams(dimension_semantics=sem, vmem_limit_bytes=VMEM_LIMIT)


def _nt(a, b):
    return lax.dot_general(a, b, (((1,), (1,)), ((), ())), preferred_element_type=F32)


def _tn(a, b):
    return lax.dot_general(a, b, (((0,), (0,)), ((), ())), preferred_element_type=F32)


def _nn(a, b):
    return jnp.dot(a, b, preferred_element_type=F32)


def _sigmoid(x):
    return 0.5 * jnp.tanh(0.5 * x) + 0.5


def _tile(t, want):
    if t <= want:
        return t
    for cand in range(want - want % 8, 7, -8):
        if t % cand == 0:
            return cand
    raise ValueError((t, want))


def rmsnorm_fwd(x, g, name):
    t, d = x.shape
    tm = _tile(t, 512)

    def body(x_ref, g_ref, h_ref):
        xv = x_ref[...]
        r = lax.rsqrt(jnp.mean(xv * xv, axis=-1, keepdims=True) + EPS)
        h_ref[...] = (xv * r * g_ref[...]).astype(BF16)

    return pl.pallas_call(
        body, name=name, grid=(t // tm,),
        in_specs=[pl.BlockSpec((tm, d), lambda i: (i, 0)), pl.BlockSpec((1, d), lambda i: (0, 0))],
        out_specs=pl.BlockSpec((tm, d), lambda i: (i, 0)),
        out_shape=jax.ShapeDtypeStruct((t, d), BF16),
        compiler_params=_cparams(("parallel",)),
    )(x, g)


def colmm(h, ws, nu, out_dtype, name, carry=None, swiglu=False):
    t, k = h.shape
    j, _, nj = ws[0].shape
    per = nj // nu
    units = j * per
    tm = _tile(t, 512)
    nw = len(ws)
    n_out = 4 if swiglu else nw

    def body(*refs):
        h_ref = refs[0]
        hv = h_ref[...]
        if swiglu:
            silu_ref, dsilu_ref, u_ref, act_ref = refs[1 + nw:]
            gv = _nn(hv, refs[1][0])
            uv = _nn(hv, refs[2][0])
            s = _sigmoid(gv)
            silu = gv * s
            silu_ref[0] = silu.astype(out_dtype)
            dsilu_ref[0] = (s + silu * (1.0 - s)).astype(out_dtype)
            u_ref[0] = uv.astype(out_dtype)
            act_ref[0] = (silu * uv).astype(out_dtype)
            return
        for n in range(nw):
            res = _nn(hv, refs[1 + n][0]).astype(out_dtype)
            for u in range(per):
                refs[1 + nw + n][u] = res[:, u * nu:(u + 1) * nu]

    assert not swiglu or (nw == 2 and per == 1)
    w_spec = pl.BlockSpec((1, k, nj), lambda s, i: (s, 0, 0))
    o_spec = pl.BlockSpec((per, tm, nu), lambda s, i: (s, i, 0))
    return _call(
        body, name=name, grid=(j, t // tm),
        in_specs=[pl.BlockSpec((tm, k), lambda s, i: (i, 0))] + [w_spec] * nw,
        out_specs=[o_spec] * n_out,
        out_shape=[jax.ShapeDtypeStruct((units, t, nu), out_dtype)] * n_out,
        args=[h, *ws], sem=("parallel", "parallel"), carry=carry)


def rowmm(a, w, resid, scale, name, carry=None):
    u_n, t, ku = a.shape
    n = w.shape[2]
    tm = _tile(t, 256)

    def body(a_ref, w_ref, r_ref, o_ref):
        acc = jnp.zeros((tm, n), F32)
        for u in range(u_n):
            acc = acc + _nn(a_ref[u], w_ref[u])
        o_ref[...] = r_ref[...] + scale * acc

    (out,), carried = _call(
        body, name=name, grid=(t // tm,),
        in_specs=[pl.BlockSpec((u_n, tm, ku), lambda i: (0, i, 0)), pl.BlockSpec((u_n, ku, n), lambda i: (0, 0, 0)),
                  pl.BlockSpec((tm, n), lambda i: (i, 0))],
        out_specs=[pl.BlockSpec((tm, n), lambda i: (i, 0))],
        out_shape=[jax.ShapeDtypeStruct((t, n), F32)],
        args=[a, w, resid], sem=("parallel",), carry=carry)
    return out, carried


def rowmm_t(dyb, w, scale, out_dtype, name, swiglu=None):
    t, n = dyb.shape
    u_n, ku, _ = w.shape
    tm = _tile(t, 512)

    def body(*refs):
        if swiglu is None:
            dy_ref, w_ref, o_ref = refs
            o_ref[0] = (scale * _nt(dy_ref[...], w_ref[0])).astype(out_dtype)
        else:
            dy_ref, w_ref, silu_ref, dsilu_ref, u_ref, dg_ref, du_ref = refs
            dact = scale * _nt(dy_ref[...], w_ref[0])
            dg_ref[0] = (dact * u_ref[0].astype(F32) * dsilu_ref[0].astype(F32)).astype(BF16)
            du_ref[0] = (dact * silu_ref[0].astype(F32)).astype(BF16)

    blk = pl.BlockSpec((1, tm, ku), lambda u, i: (u, i, 0))
    in_specs = [pl.BlockSpec((tm, n), lambda u, i: (i, 0)), pl.BlockSpec((1, ku, n), lambda u, i: (u, 0, 0))]
    if swiglu is None:
        return pl.pallas_call(
            body, name=name, grid=(u_n, t // tm), in_specs=in_specs, out_specs=blk,
            out_shape=jax.ShapeDtypeStruct((u_n, t, ku), out_dtype),
            compiler_params=_cparams(("parallel", "parallel")),
        )(dyb, w)
    return pl.pallas_call(
        body, name=name, grid=(u_n, t // tm), in_specs=in_specs + [blk] * 3, out_specs=[blk] * 2,
        out_shape=[jax.ShapeDtypeStruct((u_n, t, ku), BF16)] * 2,
        compiler_params=_cparams(("parallel", "parallel")),
    )(dyb, w, *swiglu)


def colmm_t(dzs, ws, nu, x, g, dy_in, name, carry=None):
    t, k = x.shape
    j, _, nj = ws[0].shape
    per = nj // nu
    units = j * per
    nw = len(ws)
    tm = _tile(t, 256)

    def body(*refs):
        dz_refs = refs[:nw]
        w_refs = refs[nw:2 * nw]
        x_ref, g_ref, dy_ref, dx_ref, dxb_ref, dg_ref = refs[2 * nw:]
        i = pl.program_id(0)
        dh = jnp.zeros((tm, k), F32)
        for n in range(nw):
            for u in range(units):
                wv = w_refs[n][u // per, :, (u % per) * nu:(u % per + 1) * nu]
                dh = dh + _nt(dz_refs[n][u], wv)
        xv = x_ref[...]
        gv = g_ref[...]
        r = lax.rsqrt(jnp.mean(xv * xv, axis=-1, keepdims=True) + EPS)
        uu = dh * gv
        dx = dy_ref[...] + r * uu - xv * (r * r * r * jnp.mean(uu * xv, axis=-1, keepdims=True))
        dx_ref[...] = dx
        dxb_ref[...] = dx.astype(BF16)
        part = jnp.sum(dh * (xv * r), axis=0, keepdims=True)

        @pl.when(i == 0)
        def _():
            dg_ref[...] = part

        @pl.when(i > 0)
        def _():
            dg_ref[...] += part

    dz_spec = pl.BlockSpec((units, tm, nu), lambda i: (0, i, 0))
    w_spec = pl.BlockSpec((j, k, nj), lambda i: (0, 0, 0))
    row = pl.BlockSpec((tm, k), lambda i: (i, 0))
    vec = pl.BlockSpec((1, k), lambda i: (0, 0))
    return _call(
        body, name=name, grid=(t // tm,),
        in_specs=[dz_spec] * nw + [w_spec] * nw + [row, vec, row],
        out_specs=[row, row, vec],
        out_shape=[jax.ShapeDtypeStruct((t, k), F32), jax.ShapeDtypeStruct((t, k), BF16),
                   jax.ShapeDtypeStruct((1, k), F32)],
        args=[*dzs, *ws, x, g, dy_in], sem=("arbitrary",), carry=carry)


def dw_col(h, dzs, j, nu, name):
    t, k = h.shape
    units = dzs[0].shape[0]
    per = units // j
    nw = len(dzs)
    tt = _tile(t, DW_TOKENS)

    def body(*refs):
        h_ref = refs[0]
        s = pl.program_id(1)
        hv = h_ref[...]
        @pl.when(s == 0)
        def _():
            for n in range(nw):
                refs[1 + nw + n][...] = jnp.zeros_like(refs[1 + nw + n])

        for n in range(nw):
            for u in range(per):
                refs[1 + nw + n][0, :, u * nu:(u + 1) * nu] += _tn(hv, refs[1 + n][u])

    return pl.pallas_call(
        body, name=name, grid=(j, t // tt),
        in_specs=[pl.BlockSpec((tt, k), lambda u, s: (s, 0))] + [pl.BlockSpec((per, tt, nu), lambda u, s: (u, s, 0))] * nw,
        out_specs=[pl.BlockSpec((1, k, per * nu), lambda u, s: (u, 0, 0))] * nw,
        out_shape=[jax.ShapeDtypeStruct((j, k, per * nu), F32)] * nw,
        compiler_params=_cparams(("parallel", "arbitrary")),
    )(h, *dzs)


def dw_row(a, dyb, scale, name):
    u_n, t, ku = a.shape
    n = dyb.shape[1]
    tt = _tile(t, DW_TOKENS)

    def body(a_ref, dy_ref, o_ref):
        @pl.when(pl.program_id(1) == 0)
        def _():
            o_ref[...] = jnp.zeros_like(o_ref)

        o_ref[0] += scale * _tn(a_ref[0], dy_ref[...])

    return pl.pallas_call(
        body, name=name, grid=(u_n, t // tt),
        in_specs=[pl.BlockSpec((1, tt, ku), lambda u, s: (u, s, 0)), pl.BlockSpec((tt, n), lambda u, s: (s, 0))],
        out_specs=pl.BlockSpec((1, ku, n), lambda u, s: (u, 0, 0)),
        out_shape=jax.ShapeDtypeStruct((u_n, ku, n), F32),
        compiler_params=_cparams(("parallel", "arbitrary")),
    )(a, dyb)


def loss_head(x, g, target):
    t, d = x.shape
    tm = _tile(t, 256)

    def body(x_ref, g_ref, t_ref, loss_ref, dx_ref, dxb_ref, dg_ref):
        i = pl.program_id(0)
        xv = x_ref[...]
        gv = g_ref[...]
        r = lax.rsqrt(jnp.mean(xv * xv, axis=-1, keepdims=True) + EPS)
        xh = xv * r
        err = xh * gv - t_ref[...]
        dy = err * (1.0 / d)
        uu = dy * gv
        dx = r * uu - xv * (r * r * r * jnp.mean(uu * xv, axis=-1, keepdims=True))
        dx_ref[...] = dx
        dxb_ref[...] = dx.astype(BF16)
        dg_part = jnp.sum(dy * xh, axis=0, keepdims=True)
        row = jnp.sum(err * err, axis=-1, keepdims=True) * (0.5 / d)
        l_part = jnp.zeros((8, LANES), F32) + jnp.sum(row, axis=0, keepdims=True)

        @pl.when(i == 0)
        def _():
            dg_ref[...] = dg_part
            loss_ref[...] = l_part

        @pl.when(i > 0)
        def _():
            dg_ref[...] += dg_part
            loss_ref[...] += l_part

    row = pl.BlockSpec((tm, d), lambda i: (i, 0))
    vec = pl.BlockSpec((1, d), lambda i: (0, 0))
    return pl.pallas_call(
        body, name="loss_head", grid=(t // tm,),
        in_specs=[row, vec, row],
        out_specs=[pl.BlockSpec((8, LANES), lambda i: (0, 0)), row, row, vec],
        out_shape=[jax.ShapeDtypeStruct((8, LANES), F32), jax.ShapeDtypeStruct((t, d), F32),
                   jax.ShapeDtypeStruct((t, d), BF16), jax.ShapeDtypeStruct((1, d), F32)],
        compiler_params=_cparams(("arbitrary",)),
    )(x, g, target)


def _split(v):
    hi = v.astype(BF16)
    lo = (v - hi.astype(F32)).astype(BF16)
    return hi, lo


def _keysums(v, m_ext):
    hi, lo = _split(v)
    outs = []
    for j in range(v.shape[1] // KBLK):
        sl = slice(j * KBLK, (j + 1) * KBLK)
        cs = _nn(jnp.concatenate([hi[:, sl], lo[:, sl]], axis=1), m_ext)
        outs.append((cs[:, :KBLK], cs[:, KBLK:]))
    return outs


def _softplus_parts(z):
    sp = jnp.maximum(z, 0.0) + jnp.log(1.0 + jnp.exp(-jnp.abs(z)))
    return sp, z - sp


def _sum_matrices():
    r = lax.broadcasted_iota(jnp.int32, (2 * KBLK, 2 * KBLK), 0) % KBLK
    c = lax.broadcasted_iota(jnp.int32, (2 * KBLK, 2 * KBLK), 1)
    suffix = jnp.where((r > c) | (c >= KBLK), 1.0, 0.0).astype(BF16)
    prefix = jnp.where((r <= c) | (c >= KBLK), 1.0, 0.0).astype(BF16)
    return suffix, prefix


def attn_fwd(qkv, n_seq, seq):
    t = qkv.shape[1]
    n_pairs = (qkv.shape[0] // 3) * 2
    bq = min(ATT_BLOCK, seq)
    nq = seq // bq
    nsub = bq // KBLK
    suffix_m, _ = _sum_matrices()

    def body(q_ref, k_ref, v_ref, m_ref, o_ref, tot_ref, cnt_ref):
        qi = pl.program_id(2)
        step_id = (pl.program_id(0) * n_pairs + pl.program_id(1)) * nq + qi
        lane = lax.broadcasted_iota(jnp.int32, (bq, LANES), 1)
        is_a = lane < HEAD_DIM
        q2 = q_ref[0] * jnp.asarray(HEAD_DIM ** -0.5, BF16)
        qs = (jnp.where(is_a, q2, jnp.zeros_like(q2)), jnp.where(is_a, jnp.zeros_like(q2), q2))
        m_ext = m_ref[...]
        row = lax.broadcasted_iota(jnp.int32, (bq, bq), 0)
        col = lax.broadcasted_iota(jnp.int32, (bq, bq), 1)
        diag_mask = col < row

        def block(kj, carry, mask):
            off = pl.multiple_of(kj * bq, bq)
            k2 = k_ref[0, pl.ds(off, bq), :]
            v2 = v_ref[0, pl.ds(off, bq), :]
            out = []
            for h in range(2):
                rem, acc = carry[h]
                z = _nt(qs[h], k2)
                sp, ls = _softplus_parts(z)
                lk = -sp if mask is None else jnp.where(mask, -sp, 0.0)
                sums = _keysums(lk, m_ext)
                parts = [None] * nsub
                for j in reversed(range(nsub)):
                    suf, total = sums[j]
                    parts[j] = jnp.exp(ls[:, j * KBLK:(j + 1) * KBLK] + suf + rem)
                    rem = rem + total
                a = jnp.concatenate(parts, axis=1)
                if mask is not None:
                    a = jnp.where(mask, a, 0.0)
                out.append((rem, acc + _nn(a.astype(BF16), v2)))
            return tuple(out)

        def most_left(c):
            return jnp.maximum(jnp.max(c[0][0]), jnp.max(c[1][0]))

        def more(s):
            return (s[0] < qi) & (s[1] > STICK_GONE)

        def step(s):
            c = block(qi - 1 - s[0], s[2], None)
            return s[0] + 1, most_left(c), c

        zero = jnp.zeros((bq, LANES), F32)
        carry = block(qi, ((zero, zero), (zero, zero)), diag_mask)
        n_left, _, carry = lax.while_loop(more, step, (jnp.int32(0), most_left(carry), carry))
        o_ref[0] = jnp.where(is_a, carry[0][1], carry[1][1]).astype(BF16)
        tot_ref[...] = jnp.where(is_a, carry[0][0], carry[1][0])
        cnt_ref[step_id] = n_left.astype(F32)

    upp = qkv.shape[0] // 3
    return pl.pallas_call(
        body, name="attn_fwd", grid=(n_seq, n_pairs, nq),
        in_specs=[pl.BlockSpec((1, bq, LANES), lambda b, p, i: (p // 2, b * nq + i, p % 2)),
                  pl.BlockSpec((1, seq, LANES), lambda b, p, i: (upp + p // 2, b, p % 2)),
                  pl.BlockSpec((1, seq, LANES), lambda b, p, i: (2 * upp + p // 2, b, p % 2)),
                  pl.BlockSpec((2 * KBLK, 2 * KBLK), lambda b, p, i: (0, 0))],
        out_specs=[pl.BlockSpec((1, bq, LANES), lambda b, p, i: (p // 2, b * nq + i, p % 2)),
                   pl.BlockSpec((bq, LANES), lambda b, p, i: (b * nq + i, p)),
                   pl.BlockSpec(memory_space=pltpu.SMEM)],
        out_shape=[jax.ShapeDtypeStruct((upp, t, 2 * LANES), BF16), jax.ShapeDtypeStruct((t, n_pairs * LANES), F32),
                   jax.ShapeDtypeStruct((n_seq * n_pairs * nq,), F32)],
        compiler_params=_cparams(("arbitrary", "arbitrary", "arbitrary")),
    )(qkv, qkv, qkv, suffix_m)


def attn_bwd(qkv, do, tot, cnt, n_seq, seq):
    t = qkv.shape[1]
    upp = qkv.shape[0] // 3
    n_pairs = upp * 2
    bq = min(ATT_BLOCK, seq)
    nq = seq // bq
    nsub = bq // KBLK
    _, prefix_m = _sum_matrices()
    scale = HEAD_DIM ** -0.5

    def body(q_ref, k_ref, v_ref, do_ref, tot_ref, m_ref, cnt_ref, dq_ref, dk_ref, dv_ref, dk_acc, dv_acc):
        qi = pl.program_id(2)
        step_id = (pl.program_id(0) * n_pairs + pl.program_id(1)) * nq + qi
        n_left = jnp.clip(cnt_ref[step_id].astype(jnp.int32), 0, qi)
        lane = lax.broadcasted_iota(jnp.int32, (bq, LANES), 1)
        is_a = lane < HEAD_DIM

        def halves(v2):
            z2 = jnp.zeros_like(v2)
            return jnp.where(is_a, v2, z2), jnp.where(is_a, z2, v2)

        qs = halves(q_ref[0] * jnp.asarray(scale, BF16))
        dos = halves(do_ref[0])
        tot2 = tot_ref[...]
        swapped = pltpu.roll(tot2, HEAD_DIM, 1)
        tots = (jnp.where(is_a, tot2, swapped), jnp.where(is_a, swapped, tot2))
        m_ext = m_ref[...]
        row = lax.broadcasted_iota(jnp.int32, (bq, bq), 0)
        col = lax.broadcasted_iota(jnp.int32, (bq, bq), 1)
        diag_mask = col < row

        @pl.when(qi == 0)
        def _():
            dk_acc[...] = jnp.zeros_like(dk_acc)
            dv_acc[...] = jnp.zeros_like(dv_acc)

        def block(kj, carry, mask):
            off = pl.multiple_of(kj * bq, bq)
            k2 = k_ref[0, pl.ds(off, bq), :]
            v2 = v_ref[0, pl.ds(off, bq), :]
            ks = halves(k2)
            dq = carry[2]
            dk_part = jnp.zeros((bq, LANES), F32)
            dv_part = jnp.zeros((bq, LANES), F32)
            out = []
            for h in range(2):
                pre, gpre = carry[h]
                z = _nt(qs[h], k2)
                sp, ls = _softplus_parts(z)
                lk = -sp if mask is None else jnp.where(mask, -sp, 0.0)
                sums = _keysums(lk, m_ext)
                parts = []
                for j in range(nsub):
                    pin, ptot = sums[j]
                    parts.append(jnp.exp(ls[:, j * KBLK:(j + 1) * KBLK] + (tots[h] - (pre + pin))))
                    pre = pre + ptot
                a = jnp.concatenate(parts, axis=1)
                if mask is not None:
                    a = jnp.where(mask, a, 0.0)
                g = a * _nt(dos[h], v2)
                gsums = _keysums(g, m_ext)
                parts = []
                for j in range(nsub):
                    gin, gtot = gsums[j]
                    parts.append(gpre + gin)
                    gpre = gpre + gtot
                dz = g - jnp.exp(ls) * jnp.concatenate(parts, axis=1)
                if mask is not None:
                    dz = jnp.where(mask, dz, 0.0)
                dzb = dz.astype(BF16)
                dq = dq + _nn(dzb, ks[h])
                dk_part = dk_part + _tn(dzb, qs[h])
                dv_part = dv_part + _tn(a.astype(BF16), dos[h])
                out.append((pre, gpre))
            dk_acc[pl.ds(off, bq), :] += dk_part
            dv_acc[pl.ds(off, bq), :] += dv_part
            return (out[0], out[1], dq)

        zero = jnp.zeros((bq, LANES), F32)
        carry = lax.fori_loop(qi - n_left, qi, lambda kj, c: block(kj, c, None), ((zero, zero), (zero, zero), zero))
        carry = block(qi, carry, diag_mask)
        dq_ref[0] = (carry[2] * scale).astype(BF16)

        @pl.when(qi == nq - 1)
        def _():
            dk_ref[0] = dk_acc[...].astype(BF16)
            dv_ref[0] = dv_acc[...].astype(BF16)

    qblk = lambda b, p, i: (p // 2, b * nq + i, p % 2)
    kv_out = pl.BlockSpec((1, seq, LANES), lambda b, p, i: (p // 2, b, p % 2))
    shp = jax.ShapeDtypeStruct((upp, t, 2 * LANES), BF16)
    return pl.pallas_call(
        body, name="attn_bwd", grid=(n_seq, n_pairs, nq),
        in_specs=[pl.BlockSpec((1, bq, LANES), qblk),
                  pl.BlockSpec((1, seq, LANES), lambda b, p, i: (upp + p // 2, b, p % 2)),
                  pl.BlockSpec((1, seq, LANES), lambda b, p, i: (2 * upp + p // 2, b, p % 2)),
                  pl.BlockSpec((1, bq, LANES), qblk),
                  pl.BlockSpec((bq, LANES), lambda b, p, i: (b * nq + i, p)),
                  pl.BlockSpec((2 * KBLK, 2 * KBLK), lambda b, p, i: (0, 0)),
                  pl.BlockSpec(memory_space=pltpu.SMEM)],
        out_specs=[pl.BlockSpec((1, bq, LANES), qblk), kv_out, kv_out],
        out_shape=[shp, shp, shp],
        scratch_shapes=[pltpu.VMEM((seq, LANES), F32), pltpu.VMEM((seq, LANES), F32)],
        compiler_params=_cparams(("parallel", "parallel", "arbitrary")),
    )(qkv, qkv, qkv, do, tot, prefix_m, cnt)


def _ln_stats(v):
    mu = jnp.mean(v, axis=-1, keepdims=True)
    vc = v - mu
    rstd = lax.rsqrt(jnp.mean(vc * vc, axis=-1, keepdims=True) + EPS)
    return vc * rstd, rstd


def _glu_into(a0_ref, av_ref, ag_ref, hv_ref, hg_ref, first):
    hv = hv_ref[0].astype(F32)
    hg = hg_ref[0].astype(F32)
    a0_ref[0:HALO, :] = jnp.where(first, 0.0, hv * _sigmoid(hg))
    av = av_ref[0].astype(F32)
    ag = ag_ref[0].astype(F32)
    a0_ref[HALO:, :] = av * _sigmoid(ag)


def _tril_mask():
    r = lax.broadcasted_iota(jnp.int32, (CHUNK, CHUNK), 0)
    c = lax.broadcasted_iota(jnp.int32, (CHUNK, CHUNK), 1)
    return c <= r


def mix_fwd(z, conv_w, conv_b, ln_a_g, ln_a_b, ln_v_g, ln_v_b, sp_w, sp_bt, seq):
    _, t, c = z.shape
    tm = _tile(seq, 512)
    tiles_per_seq = seq // tm
    groups = c // LANES
    hb = tm // HALO

    def body(av_ref, ag_ref, u_ref, v_ref, hv_ref, hg_ref, cw_ref, cb_ref, lag_ref, lab_ref, lvg_ref, lvb_ref,
             spw_ref, spb_ref, cat_ref, a1_ref, a0_ref):
        i = pl.program_id(0)
        _glu_into(a0_ref, av_ref, ag_ref, hv_ref, hg_ref, i % tiles_per_seq == 0)
        acc = jnp.zeros((tm, c), F32) + cb_ref[...]
        for k in range(CONV_WIDTH):
            acc = acc + cw_ref[k:k + 1, :] * a0_ref[pl.ds(HALO - (CONV_WIDTH - 1) + k, tm), :]
        a1_ref[...] = acc
        xh, _ = _ln_stats(acc)
        a2 = xh * lag_ref[...] + lab_ref[...]
        a3 = (a2 * _sigmoid(a2)).astype(BF16)
        half = c // 2
        cat_ref[0] = a3[:, :half]
        cat_ref[1] = a3[:, half:]
        tril = _tril_mask()
        for g in range(groups):
            sl = slice(g * LANES, (g + 1) * LANES)
            xh, _ = _ln_stats(v_ref[0][:, sl].astype(F32))
            vn = (xh * lvg_ref[:, sl] + lvb_ref[:, sl]).astype(BF16)
            w = jnp.where(tril, spw_ref[g], 0.0).astype(BF16)
            bias = spb_ref[:, g:g + 1]
            for ch in range(tm // CHUNK):
                rows = slice(ch * CHUNK, (ch + 1) * CHUNK)
                vs = _nn(w, vn[rows]) + bias
                bo = (u_ref[0][rows, sl].astype(F32) * vs).astype(BF16)
                cat_ref[2 + (g * LANES) // half, rows, (g * LANES) % half:(g * LANES) % half + LANES] = bo

    unit = lambda u: pl.BlockSpec((1, tm, c), lambda i: (u, i, 0))
    halo = lambda u: pl.BlockSpec((1, HALO, c), lambda i: (u, jnp.maximum(i * hb - 1, 0), 0))
    vec = pl.BlockSpec((1, c), lambda i: (0, 0))
    return pl.pallas_call(
        body, name="mix_fwd", grid=(t // tm,),
        in_specs=[unit(0), unit(1), unit(2), unit(3), halo(0), halo(1),
                  pl.BlockSpec((CONV_WIDTH, c), lambda i: (0, 0)), vec, vec, vec, vec, vec,
                  pl.BlockSpec((groups, CHUNK, CHUNK), lambda i: (0, 0, 0)),
                  pl.BlockSpec((CHUNK, groups), lambda i: (0, 0))],
        out_specs=[pl.BlockSpec((4, tm, c // 2), lambda i: (0, i, 0)), pl.BlockSpec((tm, c), lambda i: (i, 0))],
        out_shape=[jax.ShapeDtypeStruct((4, t, c // 2), BF16), jax.ShapeDtypeStruct((t, c), F32)],
        scratch_shapes=[pltpu.VMEM((HALO + tm, c), F32)],
        compiler_params=_cparams(("parallel",)),
    )(z, z, z, z, z, z, conv_w, conv_b, ln_a_g, ln_a_b, ln_v_g, ln_v_b, sp_w, sp_bt)


def mix_bwd_point(dcat, z, a1, ln_a_g, ln_a_b, ln_v_g, ln_v_b, sp_w, sp_wt, sp_bt, seq):
    _, t, c = z.shape
    tm = _tile(seq, 512)
    groups = c // LANES
    half = c // 2

    def body(dc_ref, u_ref, v_ref, a1_ref, lag_ref, lab_ref, lvg_ref, lvb_ref, spw_ref, spwt_ref, spb_ref,
             dz_ref, da1_ref, dcb_ref, dlag_ref, dlab_ref, dlvg_ref, dlvb_ref, dspw_ref, dspb_ref):
        i = pl.program_id(0)
        last = pl.num_programs(0) - 1

        @pl.when(i == 0)
        def _():
            for r in (dcb_ref, dlag_ref, dlab_ref, dlvg_ref, dlvb_ref, dspw_ref, dspb_ref):
                r[...] = jnp.zeros_like(r)

        da3 = jnp.concatenate([dc_ref[0], dc_ref[1]], axis=-1)
        xh, rstd = _ln_stats(a1_ref[...])
        a2 = xh * lag_ref[...] + lab_ref[...]
        s = _sigmoid(a2)
        da2 = da3 * (s * (1.0 + a2 * (1.0 - s)))
        dlag_ref[...] += jnp.sum(da2 * xh, axis=0, keepdims=True)
        dlab_ref[...] += jnp.sum(da2, axis=0, keepdims=True)
        dxh = da2 * lag_ref[...]
        da1 = rstd * (dxh - jnp.mean(dxh, axis=-1, keepdims=True) - xh * jnp.mean(dxh * xh, axis=-1, keepdims=True))
        da1_ref[...] = da1
        dcb_ref[...] += jnp.sum(da1, axis=0, keepdims=True)

        tril = _tril_mask()
        for g in range(groups):
            sl = slice(g * LANES, (g + 1) * LANES)
            xh, rstd = _ln_stats(v_ref[0][:, sl].astype(F32))
            lg = lvg_ref[:, sl]
            vnb = (xh * lg + lvb_ref[:, sl]).astype(BF16)
            w = jnp.where(tril, spw_ref[g], 0.0).astype(BF16)
            wt = jnp.where(tril.T, spwt_ref[g], 0.0).astype(BF16)
            bias = spb_ref[:, g:g + 1]
            dbo_all = dc_ref[2 + (g * LANES) // half][:, (g * LANES) % half:(g * LANES) % half + LANES]
            dvn_parts = []
            dw_acc = jnp.zeros((CHUNK, CHUNK), F32)
            db_acc = jnp.zeros((CHUNK, LANES), F32)
            for ch in range(tm // CHUNK):
                rows = slice(ch * CHUNK, (ch + 1) * CHUNK)
                vs = _nn(w, vnb[rows]) + bias
                dbo = dbo_all[rows]
                uv = u_ref[0][rows, sl].astype(F32)
                dz_ref[0, rows, sl] = (dbo * vs).astype(BF16)
                dvs = dbo * uv
                dvsb = dvs.astype(BF16)
                dvn_parts.append(_nn(wt, dvsb))
                dw_acc = dw_acc + _nt(dvsb, vnb[rows])
                db_acc = db_acc + dvs
            dvn = jnp.concatenate(dvn_parts, axis=0)
            dspw_ref[g] += jnp.where(tril, dw_acc, 0.0)
            dspb_ref[g] += db_acc
            dlvg_ref[:, sl] += jnp.sum(dvn * xh, axis=0, keepdims=True)
            dlvb_ref[:, sl] += jnp.sum(dvn, axis=0, keepdims=True)
            dxh = dvn * lg
            dv = rstd * (dxh - jnp.mean(dxh, axis=-1, keepdims=True) - xh * jnp.mean(dxh * xh, axis=-1, keepdims=True))
            dz_ref[1, :, sl] = dv.astype(BF16)

        @pl.when(i == last)
        def _():
            for g in range(groups):
                dspb_ref[g] = jnp.zeros((CHUNK, LANES), F32) + jnp.sum(dspb_ref[g], axis=-1, keepdims=True)

    unit = lambda u: pl.BlockSpec((1, tm, c), lambda i: (u, i, 0))
    vec = pl.BlockSpec((1, c), lambda i: (0, 0))
    sq = pl.BlockSpec((groups, CHUNK, CHUNK), lambda i: (0, 0, 0))
    vshape = jax.ShapeDtypeStruct((1, c), F32)
    sshape = jax.ShapeDtypeStruct((groups, CHUNK, CHUNK), F32)
    return pl.pallas_call(
        body, name="mix_bwd_point", grid=(t // tm,),
        in_specs=[pl.BlockSpec((4, tm, half), lambda i: (0, i, 0)), unit(2), unit(3),
                  pl.BlockSpec((tm, c), lambda i: (i, 0)), vec, vec, vec, vec, sq, sq,
                  pl.BlockSpec((CHUNK, groups), lambda i: (0, 0))],
        out_specs=[pl.BlockSpec((2, tm, c), lambda i: (1, i, 0)), pl.BlockSpec((tm, c), lambda i: (i, 0)),
                   vec, vec, vec, vec, vec, sq, sq],
        out_shape=[jax.ShapeDtypeStruct((4, t, c), BF16), jax.ShapeDtypeStruct((t, c), F32),
                   vshape, vshape, vshape, vshape, vshape, sshape, sshape],
        compiler_params=_cparams(("arbitrary",)),
    )(dcat, z, z, a1, ln_a_g, ln_a_b, ln_v_g, ln_v_b, sp_w, sp_wt, sp_bt)


def mix_bwd_conv(dz, da1, z, conv_w, seq):
    _, t, c = z.shape
    tm = _tile(seq, 512)
    tiles_per_seq = seq // tm
    hb = tm // HALO
    n_halo_blocks = t // HALO

    def body(dz_in_ref, d_ref, dh_ref, av_ref, ag_ref, hv_ref, hg_ref, cw_ref, dz_ref, dcw_ref, a0_ref, d1_ref):
        del dz_in_ref
        i = pl.program_id(0)
        _glu_into(a0_ref, av_ref, ag_ref, hv_ref, hg_ref, i % tiles_per_seq == 0)
        d1_ref[0:tm, :] = d_ref[...]
        d1_ref[tm:, :] = jnp.where((i + 1) % tiles_per_seq == 0, 0.0, dh_ref[...])

        @pl.when(i == 0)
        def _():
            dcw_ref[...] = jnp.zeros_like(dcw_ref)

        d1 = d_ref[...]
        da0 = jnp.zeros((tm, c), F32)
        for k in range(CONV_WIDTH):
            back = CONV_WIDTH - 1 - k
            da0 = da0 + cw_ref[k:k + 1, :] * d1_ref[pl.ds(back, tm), :]
            dcw_ref[k:k + 1, :] += jnp.sum(d1 * a0_ref[pl.ds(HALO - back, tm), :], axis=0, keepdims=True)
        av = av_ref[0].astype(F32)
        s = _sigmoid(ag_ref[0].astype(F32))
        dz_ref[0] = (da0 * s).astype(BF16)
        dz_ref[1] = (da0 * av * s * (1.0 - s)).astype(BF16)

    unit = lambda u: pl.BlockSpec((1, tm, c), lambda i: (u, i, 0))
    halo = lambda u: pl.BlockSpec((1, HALO, c), lambda i: (u, jnp.maximum(i * hb - 1, 0), 0))
    return pl.pallas_call(
        body, name="mix_bwd_conv", grid=(t // tm,),
        in_specs=[pl.BlockSpec(memory_space=pl.ANY), pl.BlockSpec((tm, c), lambda i: (i, 0)),
                  pl.BlockSpec((HALO, c), lambda i: (jnp.minimum((i + 1) * hb, n_halo_blocks - 1), 0)),
                  unit(0), unit(1), halo(0), halo(1), pl.BlockSpec((CONV_WIDTH, c), lambda i: (0, 0))],
        out_specs=[pl.BlockSpec((2, tm, c), lambda i: (0, i, 0)), pl.BlockSpec((CONV_WIDTH, c), lambda i: (0, 0))],
        out_shape=[jax.ShapeDtypeStruct(dz.shape, BF16), jax.ShapeDtypeStruct((CONV_WIDTH, c), F32)],
        scratch_shapes=[pltpu.VMEM((HALO + tm, c), F32), pltpu.VMEM((tm + HALO, c), F32)],
        input_output_aliases={0: 0},
        compiler_params=_cparams(("arbitrary",)),
    )(dz, da1, da1, z, z, z, z, conv_w)


CHIP_FLIPS = ((1, 0), (0, 1), (1, 1))
ANY = pl.BlockSpec(memory_space=pl.ANY)


def _place():
    return lax.axis_index("x"), lax.axis_index("y"), lax.axis_index("c")


def _flip(v, f):
    return 1 - v if f else v


def place_shard(w, chip, dtype, name):
    r, cc = w.shape
    rb = _tile(r, 512)

    def body(chip_ref, w_ref, o_ref):
        del chip_ref
        o_ref[0] = w_ref[...].astype(dtype)

    return pl.pallas_call(
        body, name=name,
        grid_spec=pltpu.PrefetchScalarGridSpec(
            num_scalar_prefetch=1, grid=(r // rb,),
            in_specs=[pl.BlockSpec((rb, cc), lambda i, chip_ref: (i, 0))],
            out_specs=pl.BlockSpec((1, rb, cc), lambda i, chip_ref: (chip_ref[0], i, 0))),
        out_shape=jax.ShapeDtypeStruct((N_CHIPS, r, cc), dtype),
        compiler_params=_cparams(("parallel",)),
    )(chip, w)


class Carry:
    def __init__(self, arrays, out_shapes, aliased, sem_shapes, start, finish):
        self.arrays, self.out_shapes, self.aliased, self.sem_shapes = list(arrays), list(out_shapes), aliased, list(sem_shapes)
        self.start, self.finish = start, finish


def _call(body, *, name, grid, in_specs, out_specs, out_shape, args, sem, scratch_shapes=(), carry=None):
    if carry is None:
        res = pl.pallas_call(body, name=name, grid=grid, in_specs=in_specs, out_specs=out_specs, out_shape=out_shape,
                             scratch_shapes=list(scratch_shapes), compiler_params=_cparams(sem))(*args)
        return list(res), []
    n_in, n_out, n_scr, nc = len(args), len(out_shape), len(scratch_shapes), len(carry.arrays)

    def full_body(*refs):
        ins, refs = refs[:n_in], refs[n_in:]
        c_ins, refs = refs[:nc], refs[nc:]
        outs, refs = refs[:n_out], refs[n_out:]
        c_outs, refs = refs[:nc], refs[nc:]
        scr, sems = refs[:n_scr], refs[n_scr:]
        first = functools.reduce(lambda a, b: a & b, [pl.program_id(d) == 0 for d in range(len(grid))])
        last = functools.reduce(lambda a, b: a & b, [pl.program_id(d) == grid[d] - 1 for d in range(len(grid))])

        @pl.when(first)
        def _():
            carry.start(c_ins, c_outs, sems)

        body(*ins, *outs, *scr)

        @pl.when(last)
        def _():
            carry.finish(c_ins, c_outs, sems)

    res = pl.pallas_call(
        full_body, name=name, grid=grid, in_specs=list(in_specs) + [ANY] * nc, out_specs=list(out_specs) + [ANY] * nc,
        out_shape=list(out_shape) + carry.out_shapes, scratch_shapes=list(scratch_shapes) + carry.sem_shapes,
        input_output_aliases={n_in + i: n_out + i for i in range(nc)} if carry.aliased else {},
        compiler_params=pltpu.CompilerParams(dimension_semantics=("arbitrary",) * len(grid), vmem_limit_bytes=VMEM_LIMIT,
                                             has_side_effects=True),
    )(*args, *carry.arrays)
    return list(res[:n_out]), list(res[n_out:])


def _gather_ops(shapes, whole):
    n = len(shapes)

    def rows(a, c):
        hr = shapes[a][1] // 2
        return pl.ds(pl.multiple_of(c * hr, 16), hr)

    def start(ins, outs, sems):
        ici_send, ici_recv = sems[0], sems[1]
        x, y, c = _place()
        k = 2 * x + y
        for a in range(n):
            for o, (fx, fy) in enumerate(CHIP_FLIPS):
                src = ins[a].at[k] if whole[a] else ins[a].at[k, rows(a, c)]
                dst = outs[a].at[k] if whole[a] else outs[a].at[k, rows(a, c)]
                pltpu.make_async_remote_copy(
                    src_ref=src, dst_ref=dst, send_sem=ici_send.at[3 * a + o], recv_sem=ici_recv.at[3 * a + o],
                    device_id=(_flip(x, fx), _flip(y, fy), c), device_id_type=MESH).start()

    def finish(ins, outs, sems):
        ici_send, ici_recv, d2d_send, d2d_recv = sems
        x, y, c = _place()
        k = 2 * x + y
        sibling = (x, y, 1 - c)

        def copy(ref, send, recv, a, o):
            return pltpu.make_async_remote_copy(src_ref=ref, dst_ref=ref, send_sem=send.at[3 * a + o],
                                                recv_sem=recv.at[3 * a + o], device_id=sibling, device_id_type=MESH)

        for a in range(n):
            for o, (fx, fy) in enumerate(CHIP_FLIPS):
                kk = 2 * _flip(x, fx) + _flip(y, fy)
                landed = outs[a].at[kk] if whole[a] else outs[a].at[kk, rows(a, c)]
                copy(landed, ici_send, ici_recv, a, o).wait_recv()
                if not whole[a]:
                    copy(landed, d2d_send, d2d_recv, a, o).start()
        for a in range(n):
            for o, (fx, fy) in enumerate(CHIP_FLIPS):
                kk = 2 * _flip(x, fx) + _flip(y, fy)
                mine = ins[a].at[k] if whole[a] else ins[a].at[k, rows(a, c)]
                copy(mine, ici_send, ici_recv, a, o).wait_send()
                if not whole[a]:
                    copy(outs[a].at[kk, rows(a, 1 - c)], d2d_send, d2d_recv, a, o).wait_recv()
                    copy(outs[a].at[kk, rows(a, c)], d2d_send, d2d_recv, a, o).wait_send()

    dma = pltpu.SemaphoreType.DMA
    return start, finish, [dma((3 * n,))] * 4


def gather_carry(bufs):
    start, finish, sems = _gather_ops([b.shape for b in bufs], [False] * len(bufs))
    return Carry(bufs, [jax.ShapeDtypeStruct(b.shape, b.dtype) for b in bufs], True, sems, start, finish)


def allgather_weights(shards, smalls):
    bufs = list(shards) + list(smalls)
    n = len(bufs)
    start, finish, sems = _gather_ops([b.shape for b in bufs], [False] * len(shards) + [True] * len(smalls))

    def body(*refs):
        start(refs[:n], refs[n:2 * n], refs[2 * n:])
        finish(refs[:n], refs[n:2 * n], refs[2 * n:])

    res = pl.pallas_call(
        body, name="allgather_weights", in_specs=[ANY] * n, out_specs=[ANY] * n,
        out_shape=[jax.ShapeDtypeStruct(b.shape, b.dtype) for b in bufs], scratch_shapes=sems,
        input_output_aliases={i: i for i in range(n)},
        compiler_params=pltpu.CompilerParams(has_side_effects=True),
    )(*bufs)
    return res[:len(shards)], res[len(shards):]


def rs_exchange(grads):
    n = len(grads)

    def body(*refs):
        ins, outs = refs[:n], refs[n:2 * n]
        send, recv = refs[2 * n:]
        x, y, c = _place()
        cps = []
        for a in range(n):
            cp = pltpu.make_async_remote_copy(
                src_ref=ins[a].at[:, 1 - c], dst_ref=outs[a], send_sem=send.at[a], recv_sem=recv.at[a],
                device_id=(x, y, 1 - c), device_id_type=MESH)
            cp.start()
            cps.append(cp)
        for cp in cps:
            cp.wait()

    dma = pltpu.SemaphoreType.DMA
    return pl.pallas_call(
        body, name="rs_exchange", in_specs=[ANY] * n, out_specs=[ANY] * n,
        out_shape=[jax.ShapeDtypeStruct((g.shape[0],) + g.shape[2:], g.dtype) for g in grads],
        scratch_shapes=[dma((n,)), dma((n,))],
        compiler_params=pltpu.CompilerParams(has_side_effects=True),
    )(*grads)


def rs_add(g, sib, core, out_dtype, name):
    nk, _, hr, cc = g.shape
    rb = _tile(hr, 256)

    def body(core_ref, g_ref, s_ref, o_ref):
        del core_ref
        o_ref[0] = (g_ref[0, 0] + s_ref[0]).astype(out_dtype)

    return pl.pallas_call(
        body, name=name,
        grid_spec=pltpu.PrefetchScalarGridSpec(
            num_scalar_prefetch=1, grid=(nk, hr // rb),
            in_specs=[pl.BlockSpec((1, 1, rb, cc), lambda k, i, core_ref: (k, core_ref[0], i, 0)),
                      pl.BlockSpec((1, rb, cc), lambda k, i, core_ref: (k, i, 0))],
            out_specs=pl.BlockSpec((1, rb, cc), lambda k, i, core_ref: (k, i, 0))),
        out_shape=jax.ShapeDtypeStruct((nk, hr, cc), out_dtype),
        compiler_params=_cparams(("parallel", "parallel")),
    )(core, g, sib)


def send_carry(parts):
    n = len(parts)

    def copies(ins, outs, sems):
        x, y, c = _place()
        for a in range(n):
            for o, (fx, fy) in enumerate(CHIP_FLIPS):
                kk = 2 * _flip(x, fx) + _flip(y, fy)
                yield pltpu.make_async_remote_copy(
                    src_ref=ins[a].at[kk], dst_ref=outs[a].at[o], send_sem=sems[0].at[3 * a + o],
                    recv_sem=sems[1].at[3 * a + o], device_id=(_flip(x, fx), _flip(y, fy), c), device_id_type=MESH)

    def start(ins, outs, sems):
        for cp in copies(ins, outs, sems):
            cp.start()

    def finish(ins, outs, sems):
        for cp in copies(ins, outs, sems):
            cp.wait()

    dma = pltpu.SemaphoreType.DMA
    return Carry(parts, [jax.ShapeDtypeStruct((3,) + p.shape[1:], p.dtype) for p in parts], False,
                 [dma((3 * n,)), dma((3 * n,))], start, finish)


def rs_sum(recv, part, where, full, layer, n_layers, name):
    _, hr, cc = recv.shape
    rb = _tile(hr, 256)

    def body(*refs):
        r_ref, p_ref, o_ref = refs[1], refs[2], refs[-1]
        o_ref[0, 0] = ((p_ref[0].astype(F32) + r_ref[0].astype(F32)) + r_ref[1].astype(F32)) + r_ref[2].astype(F32)

    in_specs = [pl.BlockSpec((3, rb, cc), lambda i, w_ref: (0, i, 0)),
                pl.BlockSpec((1, rb, cc), lambda i, w_ref: (w_ref[0], i, 0))]
    args = [where, recv, part]
    aliases = {}
    if full is not None:
        in_specs.append(ANY)
        args.append(full)
        aliases = {3: 0}
    return pl.pallas_call(
        body, name=name,
        grid_spec=pltpu.PrefetchScalarGridSpec(
            num_scalar_prefetch=1, grid=(hr // rb,), in_specs=in_specs,
            out_specs=pl.BlockSpec((1, 1, rb, cc), lambda i, w_ref: (layer, w_ref[1], i, 0))),
        out_shape=jax.ShapeDtypeStruct((n_layers, 2, hr, cc), F32),
        input_output_aliases=aliases,
        compiler_params=_cparams(("parallel",)),
    )(*args)


def rs_share(fulls):
    n = len(fulls)

    def body(*refs):
        ins, outs = refs[:n], refs[n:2 * n]
        send, recv = refs[2 * n:]
        x, y, c = _place()
        cps = []
        for a in range(n):
            cp = pltpu.make_async_remote_copy(
                src_ref=ins[a].at[:, c], dst_ref=outs[a].at[:, c], send_sem=send.at[a], recv_sem=recv.at[a],
                device_id=(x, y, 1 - c), device_id_type=MESH)
            cp.start()
            cps.append(cp)
        for a in range(n):
            got = outs[a].at[:, 1 - c]
            pltpu.make_async_remote_copy(
                src_ref=got, dst_ref=got, send_sem=send.at[a], recv_sem=recv.at[a],
                device_id=(x, y, 1 - c), device_id_type=MESH).wait_recv()
        for cp in cps:
            cp.wait_send()

    dma = pltpu.SemaphoreType.DMA
    return pl.pallas_call(
        body, name="rs_share", in_specs=[ANY] * n, out_specs=[ANY] * n,
        out_shape=[jax.ShapeDtypeStruct(f.shape, f.dtype) for f in fulls],
        scratch_shapes=[dma((n,)), dma((n,))],
        input_output_aliases={i: i for i in range(n)},
        compiler_params=pltpu.CompilerParams(has_side_effects=True),
    )(*fulls)


def allreduce_small(v):
    r, w = v.shape

    def body(v_ref, o_ref, buf, send, recv, loc):
        x, y, c = _place()
        me = 4 * x + 2 * y + c
        mine = pltpu.make_async_copy(v_ref, buf.at[me], loc)
        mine.start()
        cps = []
        for o in range(1, N_DEV):
            fx, fy, fc = (o >> 2) & 1, (o >> 1) & 1, o & 1
            cp = pltpu.make_async_remote_copy(
                src_ref=v_ref, dst_ref=buf.at[me], send_sem=send.at[o - 1], recv_sem=recv.at[o - 1],
                device_id=(_flip(x, fx), _flip(y, fy), _flip(c, fc)), device_id_type=MESH)
            cp.start()
            cps.append(cp)
        for o in range(1, N_DEV):
            fx, fy, fc = (o >> 2) & 1, (o >> 1) & 1, o & 1
            peer = 4 * _flip(x, fx) + 2 * _flip(y, fy) + _flip(c, fc)
            pltpu.make_async_remote_copy(
                src_ref=v_ref, dst_ref=buf.at[peer], send_sem=send.at[o - 1], recv_sem=recv.at[o - 1],
                device_id=(x, y, c), device_id_type=MESH).wait_recv()
        for cp in cps:
            cp.wait_send()
        mine.wait()
        acc = buf[0]
        for d in range(1, N_DEV):
            acc = acc + buf[d]
        o_ref[...] = acc

    dma = pltpu.SemaphoreType.DMA
    vm = pl.BlockSpec(memory_space=pltpu.VMEM)
    return pl.pallas_call(
        body, name="allreduce_small", in_specs=[vm], out_specs=vm,
        out_shape=jax.ShapeDtypeStruct((r, w), F32),
        scratch_shapes=[pltpu.VMEM((N_DEV, r, w), F32), dma((N_DEV - 1,)), dma((N_DEV - 1,)), dma],
        compiler_params=pltpu.CompilerParams(has_side_effects=True, vmem_limit_bytes=VMEM_LIMIT),
    )(v)


def adamw(w, g, m, v, name):
    r, cc = w.shape
    rb = _tile(r, 256)

    def body(w_ref, g_ref, m_ref, v_ref, d_ref, nm_ref, nv_ref):
        gv = g_ref[...]
        nm = ADAM_B1 * m_ref[...] + (1.0 - ADAM_B1) * gv
        nv = ADAM_B2 * v_ref[...] + (1.0 - ADAM_B2) * (gv * gv)
        m_hat = nm / (1.0 - ADAM_B1 ** ADAM_STEP)
        v_hat = nv / (1.0 - ADAM_B2 ** ADAM_STEP)
        d_ref[...] = -ADAM_LR * (m_hat / (jnp.sqrt(v_hat) + ADAM_EPS) + ADAM_WD * w_ref[...])
        nm_ref[...] = nm
        nv_ref[...] = nv

    blk = pl.BlockSpec((rb, cc), lambda i: (i, 0))
    shp = jax.ShapeDtypeStruct((r, cc), F32)
    return pl.pallas_call(
        body, name=name, grid=(r // rb,), in_specs=[blk] * 4, out_specs=[blk] * 3, out_shape=[shp] * 3,
        compiler_params=_cparams(("parallel",)),
    )(w, g, m, v)


WEIGHTS = ['g_ffn1', 'w_ffn1_gate', 'w_ffn1_up', 'w_ffn1_down', 'g_mix', 'w_in_ab', 'conv_w', 'conv_b', 'ln_a_g',
           'ln_a_b', 'ln_v_g', 'ln_v_b', 'sp_w', 'sp_b', 'w_out_ab', 'w_qkv', 'w_o', 'g_ffn2', 'w_ffn2_gate',
           'w_ffn2_up', 'w_ffn2_down', 'g_final']
BIG = ['w_ffn1_gate', 'w_ffn1_up', 'w_ffn1_down', 'w_in_ab', 'w_out_ab', 'w_qkv', 'w_o', 'w_ffn2_gate', 'w_ffn2_up',
       'w_ffn2_down']
SMALL = ['g_ffn1', 'g_mix', 'g_ffn2', 'g_final', 'conv_b', 'ln_a_g', 'ln_a_b', 'ln_v_g', 'ln_v_b', 'sp_b', 'sp_w']


CARRY_WEIGHTS = {"ffn_gateup": 9.2e6, "ffn_down": 6.1e6, "mm_in": 5.9e6, "mm_out": 3.3e6}


def _use_order(depth):
    order = []
    for layer in range(depth):
        order += [('w_ffn1_gate', layer), ('w_ffn1_up', layer), ('w_ffn1_down', layer)]
        order += [('w_in_ab', layer // 2), ('w_out_ab', layer // 2)] if layer % 2 == 0 else [('w_qkv', layer // 2), ('w_o', layer // 2)]
        order += [('w_ffn2_gate', layer), ('w_ffn2_up', layer), ('w_ffn2_down', layer)]
    return order


def _rows(a):
    return a.reshape(-1, LANES)


def _pack(parts):
    v = jnp.concatenate([_rows(p) for p in parts], axis=0)
    pad = (-v.shape[0]) % 8
    return jnp.pad(v, ((0, pad), (0, 0)))


def _unpack(v, shapes):
    out, r = [], 0
    for s in shapes:
        n = 1
        for d in s:
            n *= d
        n //= LANES
        out.append(v[r:r + n].reshape(s))
        r += n
    return out


def kernel(x, g_ffn1, w_ffn1_gate, w_ffn1_up, w_ffn1_down, g_mix, w_in_ab, conv_w, conv_b, ln_a_g, ln_a_b, ln_v_g, ln_v_b, sp_w, sp_b, w_out_ab, w_qkv, w_o, g_ffn2, w_ffn2_gate, w_ffn2_up, w_ffn2_down, g_final, loss_target, m_g_ffn1, m_w_ffn1_gate, m_w_ffn1_up, m_w_ffn1_down, m_g_mix, m_w_in_ab, m_conv_w, m_conv_b, m_ln_a_g, m_ln_a_b, m_ln_v_g, m_ln_v_b, m_sp_w, m_sp_b, m_w_out_ab, m_w_qkv, m_w_o, m_g_ffn2, m_w_ffn2_gate, m_w_ffn2_up, m_w_ffn2_down, m_g_final, v_g_ffn1, v_w_ffn1_gate, v_w_ffn1_up, v_w_ffn1_down, v_g_mix, v_w_in_ab, v_conv_w, v_conv_b, v_ln_a_g, v_ln_a_b, v_ln_v_g, v_ln_v_b, v_sp_w, v_sp_b, v_w_out_ab, v_w_qkv, v_w_o, v_g_ffn2, v_w_ffn2_gate, v_w_ffn2_up, v_w_ffn2_down, v_g_final):
    p = dict(locals())
    n_seq, seq, d = x.shape
    t = n_seq * seq
    depth = g_ffn1.shape[0]
    core = lax.axis_index("c")
    chip = 2 * lax.axis_index("x") + lax.axis_index("y")
    xf = x.reshape(t, d)
    target = loss_target.reshape(t, d)

    items = []
    for name in BIG:
        for layer in range(p[name].shape[0]):
            items.append((name, layer))
    chip1 = chip.reshape(1).astype(jnp.int32)
    placed = {it: place_shard(p[it[0]][it[1]], chip1, BF16, "place_shard") for it in items}
    first = [('w_ffn1_gate', 0), ('w_ffn1_up', 0), ('w_ffn1_down', 0)]
    gathered, (conv_w4,) = allgather_weights([placed[it] for it in first],
                                             [place_shard(conv_w[0], chip1, F32, "place_conv_w")])
    wt = dict(zip(first, gathered))
    waiting = [it for it in _use_order(depth) if it not in wt]

    def riders(name):
        room, take = CARRY_WEIGHTS[name], []
        for it in list(waiting):
            if placed[it].size <= room:
                room -= placed[it].size
                take.append(it)
                waiting.remove(it)
        return (take, gather_carry([placed[it] for it in take])) if take else (take, None)

    def landed(take, carried):
        wt.update(zip(take, carried))

    def weight(it):
        if it not in wt:
            waiting.remove(it)
            (wt[it],), _ = allgather_weights([placed[it]], [])
        return wt[it]

    c_mix = conv_w4.shape[2] * N_CHIPS
    conv_full = jnp.transpose(conv_w4, (1, 0, 2)).reshape(CONV_WIDTH, c_mix)
    vec = lambda a: a.reshape(1, -1)
    sp_bt = sp_b[0].T
    sp_wt = jnp.transpose(sp_w[0], (0, 2, 1))
    d_ff = w_ffn1_gate.shape[2]
    n_in = w_in_ab.shape[2]
    n_qkv = w_qkv.shape[2] // 3

    saved = []
    xc = xf
    for layer in range(depth):
        s = {}
        for half, (gn, wn) in enumerate((('g_ffn1', 'w_ffn1'), ('g_ffn2', 'w_ffn2'))):
            if half == 1:
                s['x_mix'] = xc
                s['h_mix'] = rmsnorm_fwd(xc, vec(g_mix[layer]), "norm_mix")
                if layer % 2 == 0:
                    w_in, w_out = weight(('w_in_ab', layer // 2)), weight(('w_out_ab', layer // 2))
                    take, carry = riders("mm_in")
                    (z,), got = colmm(s['h_mix'], [w_in], n_in, BF16, "mm_in", carry)
                    landed(take, got)
                    cat, a1 = mix_fwd(z, conv_full, conv_b, ln_a_g, ln_a_b, vec(ln_v_g), vec(ln_v_b), sp_w[0], sp_bt, seq)
                    s.update(z=z, cat=cat, a1=a1)
                    take, carry = riders("mm_out")
                    xc, got = rowmm(cat, w_out, xc, 1.0, "mm_out", carry)
                    landed(take, got)
                else:
                    (qkv,), _ = colmm(s['h_mix'], [weight(('w_qkv', layer // 2))], n_qkv, BF16, "mm_qkv")
                    o, tot, cnt = attn_fwd(qkv, n_seq, seq)
                    s.update(qkv=qkv, o=o, tot=tot, cnt=cnt)
                    xc, _ = rowmm(o, weight(('w_o', layer // 2)), xc, 1.0, "mm_o")
            s['x' + wn] = xc
            h = rmsnorm_fwd(xc, vec(p[gn][layer]), "norm_ffn")
            w_gate, w_up, w_down = (weight((wn + part, layer)) for part in ('_gate', '_up', '_down'))
            take, carry = riders("ffn_gateup")
            (silu, dsilu, up, act), got = colmm(h, [w_gate, w_up], d_ff, BF16, "ffn_gateup", carry, swiglu=True)
            landed(take, got)
            take, carry = riders("ffn_down")
            xc, got = rowmm(act, w_down, xc, 0.5, "ffn_down", carry)
            landed(take, got)
            s.update({'h' + wn: h, 'swiglu' + wn: (silu, dsilu, up), 'act' + wn: act})
        saved.append(s)

    loss8, dx, dxb, dg_final = loss_head(xc, vec(g_final), target)
    loss = lax.psum(loss8[0, 0], ("x", "y", "c"))

    gw = {}
    gs = {}
    core1 = core.reshape(1).astype(jnp.int32)
    ready = []
    part, recv = {}, {}

    def leaving():
        its = list(ready)
        ready.clear()
        g4 = [gw[it].reshape(N_CHIPS, 2, gw[it].shape[1] // 2, gw[it].shape[2]) for it in its]
        sums = [rs_add(g, sb, core1, REDUCE_DTYPE, "rs_add") for g, sb in zip(g4, rs_exchange(g4))]
        part.update(zip(its, sums))
        return its, send_carry(sums)

    for layer in reversed(range(depth)):
        s = saved[layer]
        for half, (gn, wn) in reversed(list(enumerate((('g_ffn1', 'w_ffn1'), ('g_ffn2', 'w_ffn2'))))):
            wd = wt[(wn + '_down', layer)]
            dgate, dup = rowmm_t(dxb, wd, 0.5, BF16, "ffn_bwd_act", swiglu=s['swiglu' + wn])
            gw[(wn + '_down', layer)] = dw_row(s['act' + wn], dxb, 0.5, "ffn_dw_down")
            gw[(wn + '_gate', layer)], gw[(wn + '_up', layer)] = dw_col(s['h' + wn], [dgate, dup], N_CHIPS, d_ff, "ffn_dw_gateup")
            ready.extend([(wn + '_down', layer), (wn + '_gate', layer), (wn + '_up', layer)])
            its, carry = leaving()
            (dx, dxb, dg), got = colmm_t([dgate, dup], [wt[(wn + '_gate', layer)], wt[(wn + '_up', layer)]], d_ff,
                                         s['x' + wn], vec(p[gn][layer]), dx, "ffn_bwd_in", carry)
            recv.update(zip(its, got))
            gs[(gn, layer)] = dg
            if half == 1:
                if layer % 2 == 0:
                    i = layer // 2
                    w_out = wt[('w_out_ab', i)]
                    dcat = rowmm_t(dxb, w_out, 1.0, F32, "mm_out_t")
                    gw[('w_out_ab', i)] = dw_row(s['cat'], dxb, 1.0, "dw_out")
                    dz, da1, dcb, dlag, dlab, dlvg, dlvb, dspw, dspb = mix_bwd_point(
                        dcat, s['z'], s['a1'], ln_a_g, ln_a_b, vec(ln_v_g), vec(ln_v_b), sp_w[0], sp_wt, sp_bt, seq)
                    dz, dcw = mix_bwd_conv(dz, da1, s['z'], conv_full, seq)
                    gs.update({('conv_b', i): dcb, ('ln_a_g', i): dlag, ('ln_a_b', i): dlab, ('ln_v_g', i): dlvg,
                               ('ln_v_b', i): dlvb, ('sp_w', i): dspw, ('sp_b', i): dspb[:, :, 0], ('conv_w', i): dcw})
                    (gw[('w_in_ab', i)],) = dw_col(s['h_mix'], [dz], N_CHIPS, n_in, "dw_in")
                    ready.extend([('w_out_ab', i), ('w_in_ab', i)])
                    its, carry = leaving()
                    (dx, dxb, dg), got = colmm_t([dz], [wt[('w_in_ab', i)]], n_in, s['x_mix'], vec(g_mix[layer]), dx,
                                                 "mm_in_t", carry)
                    recv.update(zip(its, got))
                else:
                    i = layer // 2
                    w_o4 = wt[('w_o', i)]
                    do = rowmm_t(dxb, w_o4, 1.0, BF16, "mm_o_t")
                    gw[('w_o', i)] = dw_row(s['o'], dxb, 1.0, "dw_o")
                    dq, dk, dv = attn_bwd(s['qkv'], do, s['tot'], s['cnt'], n_seq, seq)
                    dqkv = jnp.concatenate([dq, dk, dv], axis=0)
                    (gw[('w_qkv', i)],) = dw_col(s['h_mix'], [dqkv], N_CHIPS, n_qkv, "dw_qkv")
                    ready.extend([('w_o', i), ('w_qkv', i)])
                    its, carry = leaving()
                    (dx, dxb, dg), got = colmm_t([dqkv], [wt[('w_qkv', i)]], n_qkv, s['x_mix'], vec(g_mix[layer]), dx,
                                                 "mm_qkv_t", carry)
                    recv.update(zip(its, got))
                gs[('g_mix', layer)] = dg
    grad_x = dx.reshape(x.shape)

    assert not ready and set(recv) == set(items)
    where = jnp.stack([chip, core]).astype(jnp.int32)
    fulls = []
    for name in BIG:
        full = None
        n_layers = p[name].shape[0]
        for layer in range(n_layers):
            full = rs_sum(recv[(name, layer)], part[(name, layer)], where, full, layer, n_layers, "rs_sum")
        fulls.append(full)
    shared = rs_share(fulls)
    grads = {name: sh.reshape(p[name].shape) for name, sh in zip(BIG, shared)}

    stack = lambda name: jnp.concatenate([gs[(name, layer)].reshape((1,) + p[name].shape[1:]) for layer in range(p[name].shape[0])], axis=0)
    small_g = [stack(name) if name != 'g_final' else dg_final.reshape(p[name].shape) for name in SMALL]
    packed = _pack(small_g + [gs[('conv_w', 0)]])
    red = allreduce_small(packed)
    outs = _unpack(red, [p[name].shape for name in SMALL] + [(CONV_WIDTH, c_mix)])
    for name, g in zip(SMALL, outs[:-1]):
        grads[name] = g
    conv_g = outs[-1].reshape(CONV_WIDTH, N_CHIPS, c_mix // N_CHIPS)
    grads['conv_w'] = lax.dynamic_index_in_dim(conv_g, chip, axis=1, keepdims=False).reshape(conv_w.shape)

    delta, new_m, new_v = {}, {}, {}
    for name in BIG:
        shp = p[name].shape
        two = lambda a: a.reshape(shp[0] * shp[1], shp[2])
        dl, nm, nv = adamw(two(p[name]), two(grads[name]), two(p['m_' + name]), two(p['v_' + name]), "adamw")
        delta[name], new_m[name], new_v[name] = dl.reshape(shp), nm.reshape(shp), nv.reshape(shp)
    small_names = SMALL + ['conv_w']
    pk = lambda pre: _pack([p[pre + name] for name in small_names])
    dl, nm, nv = adamw(pk(''), _pack([grads[name] for name in small_names]), pk('m_'), pk('v_'), "adamw_small")
    shapes = [p[name].shape for name in small_names]
    for dst, val in ((delta, dl), (new_m, nm), (new_v, nv)):
        for name, a in zip(small_names, _unpack(val, shapes)):
            dst[name] = a

    return (loss, grad_x, *[grads[n] for n in WEIGHTS], *[delta[n] for n in WEIGHTS],
            *[new_m[n] for n in WEIGHTS], *[new_v[n] for n in WEIGHTS])
```

```python
import functools

import jax
import jax.numpy as jnp
from jax import lax
from jax.experimental import pallas as pl
from jax.experimental.pallas import tpu as pltpu

F32 = jnp.float32
BF16 = jnp.bfloat16
EPS = 1e-6
HEAD_DIM = 64
CONV_WIDTH = 31
CHUNK = 128
KBLK = 128
ATT_BLOCK = 256
DW_TOKENS = 2048
CONV_ROWS = 64
MASKED = -1e30
STICK_GONE = -110.0
LANES = 128
HALO = 32
ADAM_LR, ADAM_B1, ADAM_B2, ADAM_EPS, ADAM_WD, ADAM_STEP = 0.001, 0.9, 0.999, 1e-08, 0.01, 10
VMEM_LIMIT = 56 * 1024 * 1024
MESH = pl.DeviceIdType.MESH
N_CHIPS = 4
N_DEV = 8
REDUCE_DTYPE = BF16


def _cparams(sem):
    return pltpu.CompilerParams(dimension_semantics=sem, vmem_limit_bytes=VMEM_LIMIT)


def _nt(a, b):
    return lax.dot_general(a, b, (((1,), (1,)), ((), ())), preferred_element_type=F32)


def _tn(a, b):
    return lax.dot_general(a, b, (((0,), (0,)), ((), ())), preferred_element_type=F32)


def _nn(a, b):
    return jnp.dot(a, b, preferred_element_type=F32)


def _sigmoid(x):
    return 0.5 * jnp.tanh(0.5 * x) + 0.5


def _tile(t, want):
    if t <= want:
        return t
    for cand in range(want - want % 8, 7, -8):
        if t % cand == 0:
            return cand
    raise ValueError((t, want))


def rmsnorm_fwd(x, g, name):
    t, d = x.shape
    tm = _tile(t, 512)

    def body(x_ref, g_ref, h_ref):
        xv = x_ref[...]
        r = lax.rsqrt(jnp.mean(xv * xv, axis=-1, keepdims=True) + EPS)
        h_ref[...] = (xv * r * g_ref[...]).astype(BF16)

    return pl.pallas_call(
        body, name=name, grid=(t // tm,),
        in_specs=[pl.BlockSpec((tm, d), lambda i: (i, 0)), pl.BlockSpec((1, d), lambda i: (0, 0))],
        out_specs=pl.BlockSpec((tm, d), lambda i: (i, 0)),
        out_shape=jax.ShapeDtypeStruct((t, d), BF16),
        compiler_params=_cparams(("parallel",)),
    )(x, g)


def colmm(h, ws, nu, out_dtype, name, carry=None, swiglu=False):
    t, k = h.shape
    j, _, nj = ws[0].shape
    per = nj // nu
    units = j * per
    tm = _tile(t, 512)
    nw = len(ws)
    n_out = 3 if swiglu else nw

    def body(*refs):
        h_ref = refs[0]
        hv = h_ref[...]
        if swiglu:
            silu_ref, udsilu_ref, act_ref = refs[1 + nw:]
            gv = _nn(hv, refs[1][0])
            uv = _nn(hv, refs[2][0])
            s = _sigmoid(gv)
            silu = gv * s
            silu_ref[0] = silu.astype(out_dtype)
            udsilu_ref[0] = (uv * (s + silu * (1.0 - s))).astype(out_dtype)
            act_ref[0] = (silu * uv).astype(out_dtype)
            return
        for n in range(nw):
            res = _nn(hv, refs[1 + n][0]).astype(out_dtype)
            for u in range(per):
                refs[1 + nw + n][u] = res[:, u * nu:(u + 1) * nu]

    assert not swiglu or (nw == 2 and per == 1)
    w_spec = pl.BlockSpec((1, k, nj), lambda s, i: (s, 0, 0))
    o_spec = pl.BlockSpec((per, tm, nu), lambda s, i: (s, i, 0))
    return _call(
        body, name=name, grid=(j, t // tm),
        in_specs=[pl.BlockSpec((tm, k), lambda s, i: (i, 0))] + [w_spec] * nw,
        out_specs=[o_spec] * n_out,
        out_shape=[jax.ShapeDtypeStruct((units, t, nu), out_dtype)] * n_out,
        args=[h, *ws], sem=("parallel", "parallel"), carry=carry)


def rowmm(a, w, resid, scale, name, carry=None):
    u_n, t, ku = a.shape
    n = w.shape[2]
    tm = _tile(t, 256)

    def body(a_ref, w_ref, r_ref, o_ref):
        acc = jnp.zeros((tm, n), F32)
        for u in range(u_n):
            acc = acc + _nn(a_ref[u], w_ref[u])
        o_ref[...] = r_ref[...] + scale * acc

    (out,), carried = _call(
        body, name=name, grid=(t // tm,),
        in_specs=[pl.BlockSpec((u_n, tm, ku), lambda i: (0, i, 0)), pl.BlockSpec((u_n, ku, n), lambda i: (0, 0, 0)),
                  pl.BlockSpec((tm, n), lambda i: (i, 0))],
        out_specs=[pl.BlockSpec((tm, n), lambda i: (i, 0))],
        out_shape=[jax.ShapeDtypeStruct((t, n), F32)],
        args=[a, w, resid], sem=("parallel",), carry=carry)
    return out, carried


def rowmm_t(dyb, w, scale, out_dtype, name, swiglu=None):
    t, n = dyb.shape
    u_n, ku, _ = w.shape
    tm = _tile(t, 512)

    def body(*refs):
        if swiglu is None:
            dy_ref, w_ref, o_ref = refs
            o_ref[0] = (scale * _nt(dy_ref[...], w_ref[0])).astype(out_dtype)
        else:
            dy_ref, w_ref, silu_ref, udsilu_ref, dg_ref, du_ref = refs
            dact = scale * _nt(dy_ref[...], w_ref[0])
            dg_ref[0] = (dact * udsilu_ref[0].astype(F32)).astype(BF16)
            du_ref[0] = (dact * silu_ref[0].astype(F32)).astype(BF16)

    blk = pl.BlockSpec((1, tm, ku), lambda u, i: (u, i, 0))
    in_specs = [pl.BlockSpec((tm, n), lambda u, i: (i, 0)), pl.BlockSpec((1, ku, n), lambda u, i: (u, 0, 0))]
    if swiglu is None:
        return pl.pallas_call(
            body, name=name, grid=(u_n, t // tm), in_specs=in_specs, out_specs=blk,
            out_shape=jax.ShapeDtypeStruct((u_n, t, ku), out_dtype),
            compiler_params=_cparams(("parallel", "parallel")),
        )(dyb, w)
    return pl.pallas_call(
        body, name=name, grid=(u_n, t // tm), in_specs=in_specs + [blk] * 2, out_specs=[blk] * 2,
        out_shape=[jax.ShapeDtypeStruct((u_n, t, ku), BF16)] * 2,
        compiler_params=_cparams(("parallel", "parallel")),
    )(dyb, w, *swiglu)


def colmm_t(dzs, ws, nu, x, g, dy_in, name, carry=None):
    t, k = x.shape
    j, _, nj = ws[0].shape
    per = nj // nu
    units = j * per
    nw = len(ws)
    tm = _tile(t, 256)

    def body(*refs):
        dz_refs = refs[:nw]
        w_refs = refs[nw:2 * nw]
        x_ref, g_ref, dy_ref, dx_ref, dxb_ref, dg_ref = refs[2 * nw:]
        i = pl.program_id(0)
        dh = jnp.zeros((tm, k), F32)
        for n in range(nw):
            for u in range(units):
                wv = w_refs[n][u // per, :, (u % per) * nu:(u % per + 1) * nu]
                dh = dh + _nt(dz_refs[n][u], wv)
        xv = x_ref[...]
        gv = g_ref[...]
        r = lax.rsqrt(jnp.mean(xv * xv, axis=-1, keepdims=True) + EPS)
        uu = dh * gv
        dx = dy_ref[...] + r * uu - xv * (r * r * r * jnp.mean(uu * xv, axis=-1, keepdims=True))
        dx_ref[...] = dx
        dxb_ref[...] = dx.astype(BF16)
        part = jnp.sum(dh * (xv * r), axis=0, keepdims=True)

        @pl.when(i == 0)
        def _():
            dg_ref[...] = part

        @pl.when(i > 0)
        def _():
            dg_ref[...] += part

    dz_spec = pl.BlockSpec((units, tm, nu), lambda i: (0, i, 0))
    w_spec = pl.BlockSpec((j, k, nj), lambda i: (0, 0, 0))
    row = pl.BlockSpec((tm, k), lambda i: (i, 0))
    vec = pl.BlockSpec((1, k), lambda i: (0, 0))
    return _call(
        body, name=name, grid=(t // tm,),
        in_specs=[dz_spec] * nw + [w_spec] * nw + [row, vec, row],
        out_specs=[row, row, vec],
        out_shape=[jax.ShapeDtypeStruct((t, k), F32), jax.ShapeDtypeStruct((t, k), BF16),
                   jax.ShapeDtypeStruct((1, k), F32)],
        args=[*dzs, *ws, x, g, dy_in], sem=("arbitrary",), carry=carry)


def dw_col(h, dzs, j, nu, name):
    t, k = h.shape
    units = dzs[0].shape[0]
    per = units // j
    nw = len(dzs)
    tt = _tile(t, DW_TOKENS)

    def body(*refs):
        h_ref = refs[0]
        s = pl.program_id(1)
        hv = h_ref[...]
        @pl.when(s == 0)
        def _():
            for n in range(nw):
                refs[1 + nw + n][...] = jnp.zeros_like(refs[1 + nw + n])

        for n in range(nw):
            for u in range(per):
                refs[1 + nw + n][0, :, u * nu:(u + 1) * nu] += _tn(hv, refs[1 + n][u])

    return pl.pallas_call(
        body, name=name, grid=(j, t // tt),
        in_specs=[pl.BlockSpec((tt, k), lambda u, s: (s, 0))] + [pl.BlockSpec((per, tt, nu), lambda u, s: (u, s, 0))] * nw,
        out_specs=[pl.BlockSpec((1, k, per * nu), lambda u, s: (u, 0, 0))] * nw,
        out_shape=[jax.ShapeDtypeStruct((j, k, per * nu), F32)] * nw,
        compiler_params=_cparams(("parallel", "arbitrary")),
    )(h, *dzs)


def dw_row(a, dyb, scale, name):
    u_n, t, ku = a.shape
    n = dyb.shape[1]
    tt = _tile(t, DW_TOKENS)

    def body(a_ref, dy_ref, o_ref):
        @pl.when(pl.program_id(1) == 0)
        def _():
            o_ref[...] = jnp.zeros_like(o_ref)

        o_ref[0] += scale * _tn(a_ref[0], dy_ref[...])

    return pl.pallas_call(
        body, name=name, grid=(u_n, t // tt),
        in_specs=[pl.BlockSpec((1, tt, ku), lambda u, s: (u, s, 0)), pl.BlockSpec((tt, n), lambda u, s: (s, 0))],
        out_specs=pl.BlockSpec((1, ku, n), lambda u, s: (u, 0, 0)),
        out_shape=jax.ShapeDtypeStruct((u_n, ku, n), F32),
        compiler_params=_cparams(("parallel", "arbitrary")),
    )(a, dyb)


def loss_head(x, g, target):
    t, d = x.shape
    tm = _tile(t, 256)

    def body(x_ref, g_ref, t_ref, loss_ref, dx_ref, dxb_ref, dg_ref):
        i = pl.program_id(0)
        xv = x_ref[...]
        gv = g_ref[...]
        r = lax.rsqrt(jnp.mean(xv * xv, axis=-1, keepdims=True) + EPS)
        xh = xv * r
        err = xh * gv - t_ref[...]
        dy = err * (1.0 / d)
        uu = dy * gv
        dx = r * uu - xv * (r * r * r * jnp.mean(uu * xv, axis=-1, keepdims=True))
        dx_ref[...] = dx
        dxb_ref[...] = dx.astype(BF16)
        dg_part = jnp.sum(dy * xh, axis=0, keepdims=True)
        row = jnp.sum(err * err, axis=-1, keepdims=True) * (0.5 / d)
        l_part = jnp.zeros((8, LANES), F32) + jnp.sum(row, axis=0, keepdims=True)

        @pl.when(i == 0)
        def _():
            dg_ref[...] = dg_part
            loss_ref[...] = l_part

        @pl.when(i > 0)
        def _():
            dg_ref[...] += dg_part
            loss_ref[...] += l_part

    row = pl.BlockSpec((tm, d), lambda i: (i, 0))
    vec = pl.BlockSpec((1, d), lambda i: (0, 0))
    return pl.pallas_call(
        body, name="loss_head", grid=(t // tm,),
        in_specs=[row, vec, row],
        out_specs=[pl.BlockSpec((8, LANES), lambda i: (0, 0)), row, row, vec],
        out_shape=[jax.ShapeDtypeStruct((8, LANES), F32), jax.ShapeDtypeStruct((t, d), F32),
                   jax.ShapeDtypeStruct((t, d), BF16), jax.ShapeDtypeStruct((1, d), F32)],
        compiler_params=_cparams(("arbitrary",)),
    )(x, g, target)


def _split(v):
    hi = v.astype(BF16)
    lo = (v - hi.astype(F32)).astype(BF16)
    return hi, lo


def _keysums(v, m_ext):
    hi, lo = _split(v)
    outs = []
    for j in range(v.shape[1] // KBLK):
        sl = slice(j * KBLK, (j + 1) * KBLK)
        cs = _nn(jnp.concatenate([hi[:, sl], lo[:, sl]], axis=1), m_ext)
        outs.append((cs[:, :KBLK], cs[:, KBLK:]))
    return outs


def _softplus_parts(z):
    sp = jnp.maximum(z, 0.0) + jnp.log(1.0 + jnp.exp(-jnp.abs(z)))
    return sp, z - sp


def _sum_matrices():
    r = lax.broadcasted_iota(jnp.int32, (2 * KBLK, 2 * KBLK), 0) % KBLK
    c = lax.broadcasted_iota(jnp.int32, (2 * KBLK, 2 * KBLK), 1)
    suffix = jnp.where((r > c) | (c >= KBLK), 1.0, 0.0).astype(BF16)
    prefix = jnp.where((r <= c) | (c >= KBLK), 1.0, 0.0).astype(BF16)
    return suffix, prefix


def attn_fwd(qkv, n_seq, seq):
    t = qkv.shape[1]
    n_pairs = (qkv.shape[0] // 3) * 2
    bq = min(ATT_BLOCK, seq)
    nq = seq // bq
    nsub = bq // KBLK
    suffix_m, _ = _sum_matrices()

    def body(q_ref, k_ref, v_ref, m_ref, o_ref, tot_ref, cnt_ref):
        qi = pl.program_id(2)
        step_id = (pl.program_id(0) * n_pairs + pl.program_id(1)) * nq + qi
        lane = lax.broadcasted_iota(jnp.int32, (bq, LANES), 1)
        is_a = lane < HEAD_DIM
        q2 = q_ref[0] * jnp.asarray(HEAD_DIM ** -0.5, BF16)
        qs = (jnp.where(is_a, q2, jnp.zeros_like(q2)), jnp.where(is_a, jnp.zeros_like(q2), q2))
        m_ext = m_ref[...]
        row = lax.broadcasted_iota(jnp.int32, (bq, bq), 0)
        col = lax.broadcasted_iota(jnp.int32, (bq, bq), 1)
        diag_mask = col < row

        def block(kj, carry, mask):
            off = pl.multiple_of(kj * bq, bq)
            k2 = k_ref[0, pl.ds(off, bq), :]
            v2 = v_ref[0, pl.ds(off, bq), :]
            out = []
            for h in range(2):
                rem, acc = carry[h]
                z = _nt(qs[h], k2)
                if mask is not None:
                    z = jnp.where(mask, z, MASKED)
                sp, ls = _softplus_parts(z)
                sums = _keysums(-sp, m_ext)
                parts = [None] * nsub
                for j in reversed(range(nsub)):
                    suf, total = sums[j]
                    parts[j] = jnp.exp(ls[:, j * KBLK:(j + 1) * KBLK] + suf + rem)
                    rem = rem + total
                a = jnp.concatenate(parts, axis=1)
                out.append((rem, acc + _nn(a.astype(BF16), v2)))
            return tuple(out)

        def most_left(c):
            return jnp.maximum(jnp.max(c[0][0]), jnp.max(c[1][0]))

        def more(s):
            return (s[0] < qi) & (s[1] > STICK_GONE)

        def step(s):
            c = block(qi - 1 - s[0], s[2], None)
            return s[0] + 1, most_left(c), c

        zero = jnp.zeros((bq, LANES), F32)
        carry = block(qi, ((zero, zero), (zero, zero)), diag_mask)
        n_left, _, carry = lax.while_loop(more, step, (jnp.int32(0), most_left(carry), carry))
        o_ref[0] = jnp.where(is_a, carry[0][1], carry[1][1]).astype(BF16)
        tot_ref[...] = jnp.where(is_a, carry[0][0], carry[1][0])
        cnt_ref[step_id] = n_left.astype(F32)

    upp = qkv.shape[0] // 3
    return pl.pallas_call(
        body, name="attn_fwd", grid=(n_seq, n_pairs, nq),
        in_specs=[pl.BlockSpec((1, bq, LANES), lambda b, p, i: (p // 2, b * nq + i, p % 2)),
                  pl.BlockSpec((1, seq, LANES), lambda b, p, i: (upp + p // 2, b, p % 2)),
                  pl.BlockSpec((1, seq, LANES), lambda b, p, i: (2 * upp + p // 2, b, p % 2)),
                  pl.BlockSpec((2 * KBLK, 2 * KBLK), lambda b, p, i: (0, 0))],
        out_specs=[pl.BlockSpec((1, bq, LANES), lambda b, p, i: (p // 2, b * nq + i, p % 2)),
                   pl.BlockSpec((bq, LANES), lambda b, p, i: (b * nq + i, p)),
                   pl.BlockSpec(memory_space=pltpu.SMEM)],
        out_shape=[jax.ShapeDtypeStruct((upp, t, 2 * LANES), BF16), jax.ShapeDtypeStruct((t, n_pairs * LANES), F32),
                   jax.ShapeDtypeStruct((n_seq * n_pairs * nq,), F32)],
        compiler_params=_cparams(("arbitrary", "arbitrary", "arbitrary")),
    )(qkv, qkv, qkv, suffix_m)


def attn_bwd(qkv, do, tot, cnt, n_seq, seq):
    t = qkv.shape[1]
    upp = qkv.shape[0] // 3
    n_pairs = upp * 2
    bq = min(ATT_BLOCK, seq)
    nq = seq // bq
    nsub = bq // KBLK
    _, prefix_m = _sum_matrices()
    scale = HEAD_DIM ** -0.5

    def body(q_ref, k_ref, v_ref, do_ref, tot_ref, m_ref, cnt_ref, dq_ref, dk_ref, dv_ref, dk_acc, dv_acc):
        qi = pl.program_id(2)
        step_id = (pl.program_id(0) * n_pairs + pl.program_id(1)) * nq + qi
        n_left = jnp.clip(cnt_ref[step_id].astype(jnp.int32), 0, qi)
        lane = lax.broadcasted_iota(jnp.int32, (bq, LANES), 1)
        is_a = lane < HEAD_DIM

        def halves(v2):
            z2 = jnp.zeros_like(v2)
            return jnp.where(is_a, v2, z2), jnp.where(is_a, z2, v2)

        qs = halves(q_ref[0] * jnp.asarray(scale, BF16))
        dos = halves(do_ref[0])
        tot2 = tot_ref[...]
        swapped = pltpu.roll(tot2, HEAD_DIM, 1)
        tots = (jnp.where(is_a, tot2, swapped), jnp.where(is_a, swapped, tot2))
        m_ext = m_ref[...]
        row = lax.broadcasted_iota(jnp.int32, (bq, bq), 0)
        col = lax.broadcasted_iota(jnp.int32, (bq, bq), 1)
        diag_mask = col < row

        @pl.when(qi == 0)
        def _():
            dk_acc[...] = jnp.zeros_like(dk_acc)
            dv_acc[...] = jnp.zeros_like(dv_acc)

        def block(kj, carry, mask):
            off = pl.multiple_of(kj * bq, bq)
            k2 = k_ref[0, pl.ds(off, bq), :]
            v2 = v_ref[0, pl.ds(off, bq), :]
            ks = halves(k2)
            dq = carry[2]
            dk_part = jnp.zeros((bq, LANES), F32)
            dv_part = jnp.zeros((bq, LANES), F32)
            out = []
            for h in range(2):
                pre, gpre = carry[h]
                z = _nt(qs[h], k2)
                if mask is not None:
                    z = jnp.where(mask, z, MASKED)
                sp, ls = _softplus_parts(z)
                sums = _keysums(-sp, m_ext)
                parts = []
                for j in range(nsub):
                    pin, ptot = sums[j]
                    parts.append(jnp.exp(ls[:, j * KBLK:(j + 1) * KBLK] + (tots[h] - (pre + pin))))
                    pre = pre + ptot
                a = jnp.concatenate(parts, axis=1)
                g = a * _nt(dos[h], v2)
                gsums = _keysums(g, m_ext)
                parts = []
                for j in range(nsub):
                    gin, gtot = gsums[j]
                    parts.append(gpre + gin)
                    gpre = gpre + gtot
                dz = g - jnp.exp(ls) * jnp.concatenate(parts, axis=1)
                dzb = dz.astype(BF16)
                dq = dq + _nn(dzb, ks[h])
                dk_part = dk_part + _tn(dzb, qs[h])
                dv_part = dv_part + _tn(a.astype(BF16), dos[h])
                out.append((pre, gpre))
            dk_acc[pl.ds(off, bq), :] += dk_part
            dv_acc[pl.ds(off, bq), :] += dv_part
            return (out[0], out[1], dq)

        zero = jnp.zeros((bq, LANES), F32)
        carry = lax.fori_loop(qi - n_left, qi, lambda kj, c: block(kj, c, None), ((zero, zero), (zero, zero), zero))
        carry = block(qi, carry, diag_mask)
        dq_ref[0] = (carry[2] * scale).astype(BF16)

        @pl.when(qi == nq - 1)
        def _():
            dk_ref[0] = dk_acc[...].astype(BF16)
            dv_ref[0] = dv_acc[...].astype(BF16)

    qblk = lambda b, p, i: (p // 2, b * nq + i, p % 2)
    kv_out = pl.BlockSpec((1, seq, LANES), lambda b, p, i: (p // 2, b, p % 2))
    shp = jax.ShapeDtypeStruct((upp, t, 2 * LANES), BF16)
    return pl.pallas_call(
        body, name="attn_bwd", grid=(n_seq, n_pairs, nq),
        in_specs=[pl.BlockSpec((1, bq, LANES), qblk),
                  pl.BlockSpec((1, seq, LANES), lambda b, p, i: (upp + p // 2, b, p % 2)),
                  pl.BlockSpec((1, seq, LANES), lambda b, p, i: (2 * upp + p // 2, b, p % 2)),
                  pl.BlockSpec((1, bq, LANES), qblk),
                  pl.BlockSpec((bq, LANES), lambda b, p, i: (b * nq + i, p)),
                  pl.BlockSpec((2 * KBLK, 2 * KBLK), lambda b, p, i: (0, 0)),
                  pl.BlockSpec(memory_space=pltpu.SMEM)],
        out_specs=[pl.BlockSpec((1, bq, LANES), qblk), kv_out, kv_out],
        out_shape=[shp, shp, shp],
        scratch_shapes=[pltpu.VMEM((seq, LANES), F32), pltpu.VMEM((seq, LANES), F32)],
        compiler_params=_cparams(("parallel", "parallel", "arbitrary")),
    )(qkv, qkv, qkv, do, tot, prefix_m, cnt)


def _ln_stats(v):
    mu = jnp.mean(v, axis=-1, keepdims=True)
    vc = v - mu
    rstd = lax.rsqrt(jnp.mean(vc * vc, axis=-1, keepdims=True) + EPS)
    return vc * rstd, rstd


def _glu_into(a0_ref, av_ref, ag_ref, hv_ref, hg_ref, first):
    hv = hv_ref[0].astype(F32)
    hg = hg_ref[0].astype(F32)
    a0_ref[0:HALO, :] = jnp.where(first, 0.0, hv * _sigmoid(hg))
    av = av_ref[0].astype(F32)
    ag = ag_ref[0].astype(F32)
    a0_ref[HALO:, :] = av * _sigmoid(ag)


def _conv_taps(ref, tm, first, shifted_ref):
    for b in range(8):
        offs = [o for o in range(first, first + CONV_WIDTH) if o % 8 == b]
        if not offs:
            continue
        n_rows = max(offs) - b + tm
        shifted_ref[b, 0:n_rows, :] = ref[pl.ds(b, n_rows), :]
        for o in offs:
            yield o, o - first, shifted_ref[b, pl.ds(o - b, tm), :]


def _tril_mask():
    r = lax.broadcasted_iota(jnp.int32, (CHUNK, CHUNK), 0)
    c = lax.broadcasted_iota(jnp.int32, (CHUNK, CHUNK), 1)
    return c <= r


def mix_fwd(z, conv_w, conv_b, ln_a_g, ln_a_b, ln_v_g, ln_v_b, sp_w, sp_bt, seq):
    _, t, c = z.shape
    tm = _tile(seq, 512)
    tiles_per_seq = seq // tm
    groups = c // LANES
    hb = tm // HALO

    def body(av_ref, ag_ref, u_ref, v_ref, hv_ref, hg_ref, cw_ref, cb_ref, lag_ref, lab_ref, lvg_ref, lvb_ref,
             spw_ref, spb_ref, cat_ref, a1_ref, a0_ref, sh_ref):
        i = pl.program_id(0)
        _glu_into(a0_ref, av_ref, ag_ref, hv_ref, hg_ref, i % tiles_per_seq == 0)
        acc = jnp.zeros((tm, c), F32) + cb_ref[...]
        for off, k, rows in _conv_taps(a0_ref, tm, HALO - (CONV_WIDTH - 1), sh_ref):
            acc = acc + cw_ref[k:k + 1, :] * rows
        a1_ref[...] = acc
        xh, _ = _ln_stats(acc)
        a2 = xh * lag_ref[...] + lab_ref[...]
        a3 = (a2 * _sigmoid(a2)).astype(BF16)
        half = c // 2
        cat_ref[0] = a3[:, :half]
        cat_ref[1] = a3[:, half:]
        tril = _tril_mask()
        for g in range(groups):
            sl = slice(g * LANES, (g + 1) * LANES)
            xh, _ = _ln_stats(v_ref[0][:, sl].astype(F32))
            vn = (xh * lvg_ref[:, sl] + lvb_ref[:, sl]).astype(BF16)
            w = jnp.where(tril, spw_ref[g], 0.0).astype(BF16)
            bias = spb_ref[:, g:g + 1]
            for ch in range(tm // CHUNK):
                rows = slice(ch * CHUNK, (ch + 1) * CHUNK)
                vs = _nn(w, vn[rows]) + bias
                bo = (u_ref[0][rows, sl].astype(F32) * vs).astype(BF16)
                cat_ref[2 + (g * LANES) // half, rows, (g * LANES) % half:(g * LANES) % half + LANES] = bo

    unit = lambda u: pl.BlockSpec((1, tm, c), lambda i: (u, i, 0))
    halo = lambda u: pl.BlockSpec((1, HALO, c), lambda i: (u, jnp.maximum(i * hb - 1, 0), 0))
    vec = pl.BlockSpec((1, c), lambda i: (0, 0))
    return pl.pallas_call(
        body, name="mix_fwd", grid=(t // tm,),
        in_specs=[unit(0), unit(1), unit(2), unit(3), halo(0), halo(1),
                  pl.BlockSpec((CONV_WIDTH, c), lambda i: (0, 0)), vec, vec, vec, vec, vec,
                  pl.BlockSpec((groups, CHUNK, CHUNK), lambda i: (0, 0, 0)),
                  pl.BlockSpec((CHUNK, groups), lambda i: (0, 0))],
        out_specs=[pl.BlockSpec((4, tm, c // 2), lambda i: (0, i, 0)), pl.BlockSpec((tm, c), lambda i: (i, 0))],
        out_shape=[jax.ShapeDtypeStruct((4, t, c // 2), BF16), jax.ShapeDtypeStruct((t, c), F32)],
        scratch_shapes=[pltpu.VMEM((HALO + tm, c), F32), pltpu.VMEM((8, HALO + tm, c), F32)],
        compiler_params=_cparams(("parallel",)),
    )(z, z, z, z, z, z, conv_w, conv_b, ln_a_g, ln_a_b, ln_v_g, ln_v_b, sp_w, sp_bt)


def mix_bwd_point(dcat, z, a1, ln_a_g, ln_a_b, ln_v_g, ln_v_b, sp_w, sp_wt, sp_bt, seq):
    _, t, c = z.shape
    tm = _tile(seq, 512)
    groups = c // LANES
    half = c // 2

    def body(dc_ref, u_ref, v_ref, a1_ref, lag_ref, lab_ref, lvg_ref, lvb_ref, spw_ref, spwt_ref, spb_ref,
             dz_ref, da1_ref, dcb_ref, dlag_ref, dlab_ref, dlvg_ref, dlvb_ref, dspw_ref, dspb_ref):
        i = pl.program_id(0)
        last = pl.num_programs(0) - 1

        @pl.when(i == 0)
        def _():
            for r in (dcb_ref, dlag_ref, dlab_ref, dlvg_ref, dlvb_ref, dspw_ref, dspb_ref):
                r[...] = jnp.zeros_like(r)

        da3 = jnp.concatenate([dc_ref[0], dc_ref[1]], axis=-1)
        xh, rstd = _ln_stats(a1_ref[...])
        a2 = xh * lag_ref[...] + lab_ref[...]
        s = _sigmoid(a2)
        da2 = da3 * (s * (1.0 + a2 * (1.0 - s)))
        dlag_ref[...] += jnp.sum(da2 * xh, axis=0, keepdims=True)
        dlab_ref[...] += jnp.sum(da2, axis=0, keepdims=True)
        dxh = da2 * lag_ref[...]
        da1 = rstd * (dxh - jnp.mean(dxh, axis=-1, keepdims=True) - xh * jnp.mean(dxh * xh, axis=-1, keepdims=True))
        da1_ref[...] = da1
        dcb_ref[...] += jnp.sum(da1, axis=0, keepdims=True)

        tril = _tril_mask()
        for g in range(groups):
            sl = slice(g * LANES, (g + 1) * LANES)
            xh, rstd = _ln_stats(v_ref[0][:, sl].astype(F32))
            lg = lvg_ref[:, sl]
            vnb = (xh * lg + lvb_ref[:, sl]).astype(BF16)
            w = jnp.where(tril, spw_ref[g], 0.0).astype(BF16)
            wt = jnp.where(tril.T, spwt_ref[g], 0.0).astype(BF16)
            bias = spb_ref[:, g:g + 1]
            dbo_all = dc_ref[2 + (g * LANES) // half][:, (g * LANES) % half:(g * LANES) % half + LANES]
            dvn_parts = []
            dw_acc = jnp.zeros((CHUNK, CHUNK), F32)
            db_acc = jnp.zeros((CHUNK, LANES), F32)
            for ch in range(tm // CHUNK):
                rows = slice(ch * CHUNK, (ch + 1) * CHUNK)
                vs = _nn(w, vnb[rows]) + bias
                dbo = dbo_all[rows]
                uv = u_ref[0][rows, sl].astype(F32)
                dz_ref[0, rows, sl] = (dbo * vs).astype(BF16)
                dvs = dbo * uv
                dvsb = dvs.astype(BF16)
                dvn_parts.append(_nn(wt, dvsb))
                dw_acc = dw_acc + _nt(dvsb, vnb[rows])
                db_acc = db_acc + dvs
            dvn = jnp.concatenate(dvn_parts, axis=0)
            dspw_ref[g] += jnp.where(tril, dw_acc, 0.0)
            dspb_ref[g] += db_acc
            dlvg_ref[:, sl] += jnp.sum(dvn * xh, axis=0, keepdims=True)
            dlvb_ref[:, sl] += jnp.sum(dvn, axis=0, keepdims=True)
            dxh = dvn * lg
            dv = rstd * (dxh - jnp.mean(dxh, axis=-1, keepdims=True) - xh * jnp.mean(dxh * xh, axis=-1, keepdims=True))
            dz_ref[1, :, sl] = dv.astype(BF16)

        @pl.when(i == last)
        def _():
            for g in range(groups):
                dspb_ref[g] = jnp.zeros((CHUNK, LANES), F32) + jnp.sum(dspb_ref[g], axis=-1, keepdims=True)

    unit = lambda u: pl.BlockSpec((1, tm, c), lambda i: (u, i, 0))
    vec = pl.BlockSpec((1, c), lambda i: (0, 0))
    sq = pl.BlockSpec((groups, CHUNK, CHUNK), lambda i: (0, 0, 0))
    vshape = jax.ShapeDtypeStruct((1, c), F32)
    sshape = jax.ShapeDtypeStruct((groups, CHUNK, CHUNK), F32)
    return pl.pallas_call(
        body, name="mix_bwd_point", grid=(t // tm,),
        in_specs=[pl.BlockSpec((4, tm, half), lambda i: (0, i, 0)), unit(2), unit(3),
                  pl.BlockSpec((tm, c), lambda i: (i, 0)), vec, vec, vec, vec, sq, sq,
                  pl.BlockSpec((CHUNK, groups), lambda i: (0, 0))],
        out_specs=[pl.BlockSpec((2, tm, c), lambda i: (1, i, 0)), pl.BlockSpec((tm, c), lambda i: (i, 0)),
                   vec, vec, vec, vec, vec, sq, sq],
        out_shape=[jax.ShapeDtypeStruct((4, t, c), BF16), jax.ShapeDtypeStruct((t, c), F32),
                   vshape, vshape, vshape, vshape, vshape, sshape, sshape],
        compiler_params=_cparams(("arbitrary",)),
    )(dcat, z, z, a1, ln_a_g, ln_a_b, ln_v_g, ln_v_b, sp_w, sp_wt, sp_bt)


def mix_bwd_conv(dz, da1, z, conv_w, seq):
    _, t, c = z.shape
    tm = _tile(seq, 512)
    tiles_per_seq = seq // tm
    hb = tm // HALO
    n_halo_blocks = t // HALO

    rc = _tile(tm, CONV_ROWS)

    def body(dz_in_ref, d_ref, dh_ref, av_ref, ag_ref, cw_ref, dz_ref, dcw_ref, d1_ref, sh_ref, part_ref):
        del dz_in_ref
        i = pl.program_id(0)

        @pl.when(i == 0)
        def _():
            part_ref[...] = jnp.zeros_like(part_ref)

        d1_ref[0:tm, :] = d_ref[...]
        d1_ref[tm:, :] = jnp.where((i + 1) % tiles_per_seq == 0, 0.0, dh_ref[...])
        taps = []
        for b in range(8):
            offs = [o for o in range(CONV_WIDTH) if o % 8 == b]
            n_rows = max(offs) - b + tm
            sh_ref[b, 0:n_rows, :] = d1_ref[pl.ds(b, n_rows), :]
            taps += [(b, o - b, CONV_WIDTH - 1 - o) for o in offs]

        def chunk(ci, carry):
            r0 = pl.multiple_of(ci * rc, rc)
            av = av_ref[0, pl.ds(r0, rc), :].astype(F32)
            s = _sigmoid(ag_ref[0, pl.ds(r0, rc), :].astype(F32))
            a0 = av * s
            da0 = jnp.zeros((rc, c), F32)
            for b, ro, k in taps:
                rows = sh_ref[b, pl.ds(r0 + ro, rc), :]
                da0 = da0 + cw_ref[k:k + 1, :] * rows
                prod = a0 * rows
                part_ref[k] += functools.reduce(lambda p, q: p + q, [prod[8 * r:8 * r + 8] for r in range(rc // 8)])
            dz_ref[0, pl.ds(r0, rc), :] = (da0 * s).astype(BF16)
            dz_ref[1, pl.ds(r0, rc), :] = (da0 * av * s * (1.0 - s)).astype(BF16)
            return carry

        lax.fori_loop(0, tm // rc, chunk, 0)

        @pl.when(i == pl.num_programs(0) - 1)
        def _():
            dcw_ref[...] = jnp.sum(part_ref[...], axis=1)

    unit = lambda u: pl.BlockSpec((1, tm, c), lambda i: (u, i, 0))
    return pl.pallas_call(
        body, name="mix_bwd_conv", grid=(t // tm,),
        in_specs=[pl.BlockSpec(memory_space=pl.ANY), pl.BlockSpec((tm, c), lambda i: (i, 0)),
                  pl.BlockSpec((HALO, c), lambda i: (jnp.minimum((i + 1) * hb, n_halo_blocks - 1), 0)),
                  unit(0), unit(1), pl.BlockSpec((CONV_WIDTH, c), lambda i: (0, 0))],
        out_specs=[pl.BlockSpec((2, tm, c), lambda i: (0, i, 0)), pl.BlockSpec((CONV_WIDTH, c), lambda i: (0, 0))],
        out_shape=[jax.ShapeDtypeStruct(dz.shape, BF16), jax.ShapeDtypeStruct((CONV_WIDTH, c), F32)],
        scratch_shapes=[pltpu.VMEM((tm + HALO, c), F32), pltpu.VMEM((8, tm + HALO, c), F32),
                        pltpu.VMEM((CONV_WIDTH, 8, c), F32)],
        input_output_aliases={0: 0},
        compiler_params=_cparams(("arbitrary",)),
    )(dz, da1, da1, z, z, conv_w)


CHIP_FLIPS = ((1, 0), (0, 1), (1, 1))
ANY = pl.BlockSpec(memory_space=pl.ANY)


def _place():
    return lax.axis_index("x"), lax.axis_index("y"), lax.axis_index("c")


def _flip(v, f):
    return 1 - v if f else v


def place_shard(w, layer, chip, dtype, name):
    _, r, cc = w.shape
    rb = _tile(r, 512)

    def body(chip_ref, w_ref, o_ref):
        del chip_ref
        o_ref[0] = w_ref[0].astype(dtype)

    return pl.pallas_call(
        body, name=name,
        grid_spec=pltpu.PrefetchScalarGridSpec(
            num_scalar_prefetch=1, grid=(r // rb,),
            in_specs=[pl.BlockSpec((1, rb, cc), lambda i, chip_ref: (layer, i, 0))],
            out_specs=pl.BlockSpec((1, rb, cc), lambda i, chip_ref: (chip_ref[0], i, 0))),
        out_shape=jax.ShapeDtypeStruct((N_CHIPS, r, cc), dtype),
        compiler_params=_cparams(("parallel",)),
    )(chip, w)


class Carry:
    def __init__(self, arrays, out_shapes, aliased, sem_shapes, start, finish):
        self.arrays, self.out_shapes, self.aliased, self.sem_shapes = list(arrays), list(out_shapes), aliased, list(sem_shapes)
        self.start, self.finish = start, finish


def _call(body, *, name, grid, in_specs, out_specs, out_shape, args, sem, scratch_shapes=(), carry=None):
    if carry is None:
        res = pl.pallas_call(body, name=name, grid=grid, in_specs=in_specs, out_specs=out_specs, out_shape=out_shape,
                             scratch_shapes=list(scratch_shapes), compiler_params=_cparams(sem))(*args)
        return list(res), []
    n_in, n_out, n_scr, nc = len(args), len(out_shape), len(scratch_shapes), len(carry.arrays)

    def full_body(*refs):
        ins, refs = refs[:n_in], refs[n_in:]
        c_ins, refs = refs[:nc], refs[nc:]
        outs, refs = refs[:n_out], refs[n_out:]
        c_outs, refs = refs[:nc], refs[nc:]
        scr, sems = refs[:n_scr], refs[n_scr:]
        first = functools.reduce(lambda a, b: a & b, [pl.program_id(d) == 0 for d in range(len(grid))])
        last = functools.reduce(lambda a, b: a & b, [pl.program_id(d) == grid[d] - 1 for d in range(len(grid))])

        @pl.when(first)
        def _():
            carry.start(c_ins, c_outs, sems)

        body(*ins, *outs, *scr)

        @pl.when(last)
        def _():
            carry.finish(c_ins, c_outs, sems)

    res = pl.pallas_call(
        full_body, name=name, grid=grid, in_specs=list(in_specs) + [ANY] * nc, out_specs=list(out_specs) + [ANY] * nc,
        out_shape=list(out_shape) + carry.out_shapes, scratch_shapes=list(scratch_shapes) + carry.sem_shapes,
        input_output_aliases={n_in + i: n_out + i for i in range(nc)} if carry.aliased else {},
        compiler_params=pltpu.CompilerParams(dimension_semantics=("arbitrary",) * len(grid), vmem_limit_bytes=VMEM_LIMIT,
                                             has_side_effects=True),
    )(*args, *carry.arrays)
    return list(res[:n_out]), list(res[n_out:])


def _gather_ops(shapes, whole):
    n = len(shapes)

    def rows(a, c):
        hr = shapes[a][1] // 2
        return pl.ds(pl.multiple_of(c * hr, 16), hr)

    def start(ins, outs, sems):
        ici_send, ici_recv = sems[0], sems[1]
        x, y, c = _place()
        k = 2 * x + y
        for a in range(n):
            for o, (fx, fy) in enumerate(CHIP_FLIPS):
                src = ins[a].at[k] if whole[a] else ins[a].at[k, rows(a, c)]
                dst = outs[a].at[k] if whole[a] else outs[a].at[k, rows(a, c)]
                pltpu.make_async_remote_copy(
                    src_ref=src, dst_ref=dst, send_sem=ici_send.at[3 * a + o], recv_sem=ici_recv.at[3 * a + o],
                    device_id=(_flip(x, fx), _flip(y, fy), c), device_id_type=MESH).start()

    def finish(ins, outs, sems):
        ici_send, ici_recv, d2d_send, d2d_recv = sems
        x, y, c = _place()
        k = 2 * x + y
        sibling = (x, y, 1 - c)

        def copy(ref, send, recv, a, o):
            return pltpu.make_async_remote_copy(src_ref=ref, dst_ref=ref, send_sem=send.at[3 * a + o],
                                                recv_sem=recv.at[3 * a + o], device_id=sibling, device_id_type=MESH)

        for a in range(n):
            for o, (fx, fy) in enumerate(CHIP_FLIPS):
                kk = 2 * _flip(x, fx) + _flip(y, fy)
                landed = outs[a].at[kk] if whole[a] else outs[a].at[kk, rows(a, c)]
                copy(landed, ici_send, ici_recv, a, o).wait_recv()
                if not whole[a]:
                    copy(landed, d2d_send, d2d_recv, a, o).start()
        for a in range(n):
            for o, (fx, fy) in enumerate(CHIP_FLIPS):
                kk = 2 * _flip(x, fx) + _flip(y, fy)
                mine = ins[a].at[k] if whole[a] else ins[a].at[k, rows(a, c)]
                copy(mine, ici_send, ici_recv, a, o).wait_send()
                if not whole[a]:
                    copy(outs[a].at[kk, rows(a, 1 - c)], d2d_send, d2d_recv, a, o).wait_recv()
                    copy(outs[a].at[kk, rows(a, c)], d2d_send, d2d_recv, a, o).wait_send()

    dma = pltpu.SemaphoreType.DMA
    return start, finish, [dma((3 * n,))] * 4


def gather_carry(bufs):
    start, finish, sems = _gather_ops([b.shape for b in bufs], [False] * len(bufs))
    return Carry(bufs, [jax.ShapeDtypeStruct(b.shape, b.dtype) for b in bufs], True, sems, start, finish)


def allgather_weights(shards, smalls):
    bufs = list(shards) + list(smalls)
    n = len(bufs)
    start, finish, sems = _gather_ops([b.shape for b in bufs], [False] * len(shards) + [True] * len(smalls))

    def body(*refs):
        start(refs[:n], refs[n:2 * n], refs[2 * n:])
        finish(refs[:n], refs[n:2 * n], refs[2 * n:])

    res = pl.pallas_call(
        body, name="allgather_weights", in_specs=[ANY] * n, out_specs=[ANY] * n,
        out_shape=[jax.ShapeDtypeStruct(b.shape, b.dtype) for b in bufs], scratch_shapes=sems,
        input_output_aliases={i: i for i in range(n)},
        compiler_params=pltpu.CompilerParams(has_side_effects=True),
    )(*bufs)
    return res[:len(shards)], res[len(shards):]


def rs_exchange(grads):
    n = len(grads)

    def body(*refs):
        ins, outs = refs[:n], refs[n:2 * n]
        send, recv = refs[2 * n:]
        x, y, c = _place()
        cps = []
        for a in range(n):
            cp = pltpu.make_async_remote_copy(
                src_ref=ins[a].at[:, 1 - c], dst_ref=outs[a], send_sem=send.at[a], recv_sem=recv.at[a],
                device_id=(x, y, 1 - c), device_id_type=MESH)
            cp.start()
            cps.append(cp)
        for cp in cps:
            cp.wait()

    dma = pltpu.SemaphoreType.DMA
    return pl.pallas_call(
        body, name="rs_exchange", in_specs=[ANY] * n, out_specs=[ANY] * n,
        out_shape=[jax.ShapeDtypeStruct((g.shape[0],) + g.shape[2:], g.dtype) for g in grads],
        scratch_shapes=[dma((n,)), dma((n,))],
        compiler_params=pltpu.CompilerParams(has_side_effects=True),
    )(*grads)


def rs_add(g, sib, core, out_dtype, name):
    nk, _, hr, cc = g.shape
    rb = _tile(hr, 256)

    def body(core_ref, g_ref, s_ref, o_ref):
        del core_ref
        o_ref[0] = (g_ref[0, 0] + s_ref[0]).astype(out_dtype)

    return pl.pallas_call(
        body, name=name,
        grid_spec=pltpu.PrefetchScalarGridSpec(
            num_scalar_prefetch=1, grid=(nk, hr // rb),
            in_specs=[pl.BlockSpec((1, 1, rb, cc), lambda k, i, core_ref: (k, core_ref[0], i, 0)),
                      pl.BlockSpec((1, rb, cc), lambda k, i, core_ref: (k, i, 0))],
            out_specs=pl.BlockSpec((1, rb, cc), lambda k, i, core_ref: (k, i, 0))),
        out_shape=jax.ShapeDtypeStruct((nk, hr, cc), out_dtype),
        compiler_params=_cparams(("parallel", "parallel")),
    )(core, g, sib)


def send_carry(parts):
    n = len(parts)

    def copies(ins, outs, sems):
        x, y, c = _place()
        for a in range(n):
            for o, (fx, fy) in enumerate(CHIP_FLIPS):
                kk = 2 * _flip(x, fx) + _flip(y, fy)
                yield pltpu.make_async_remote_copy(
                    src_ref=ins[a].at[kk], dst_ref=outs[a].at[o], send_sem=sems[0].at[3 * a + o],
                    recv_sem=sems[1].at[3 * a + o], device_id=(_flip(x, fx), _flip(y, fy), c), device_id_type=MESH)

    def start(ins, outs, sems):
        for cp in copies(ins, outs, sems):
            cp.start()

    def finish(ins, outs, sems):
        for cp in copies(ins, outs, sems):
            cp.wait()

    dma = pltpu.SemaphoreType.DMA
    return Carry(parts, [jax.ShapeDtypeStruct((3,) + p.shape[1:], p.dtype) for p in parts], False,
                 [dma((3 * n,)), dma((3 * n,))], start, finish)


def rs_sum(recv, part, where, full, layer, n_layers, name):
    _, hr, cc = recv.shape
    rb = _tile(hr, 256)

    def body(*refs):
        r_ref, p_ref, o_ref = refs[1], refs[2], refs[-1]
        o_ref[0, 0] = ((p_ref[0].astype(F32) + r_ref[0].astype(F32)) + r_ref[1].astype(F32)) + r_ref[2].astype(F32)

    in_specs = [pl.BlockSpec((3, rb, cc), lambda i, w_ref: (0, i, 0)),
                pl.BlockSpec((1, rb, cc), lambda i, w_ref: (w_ref[0], i, 0))]
    args = [where, recv, part]
    aliases = {}
    if full is not None:
        in_specs.append(ANY)
        args.append(full)
        aliases = {3: 0}
    return pl.pallas_call(
        body, name=name,
        grid_spec=pltpu.PrefetchScalarGridSpec(
            num_scalar_prefetch=1, grid=(hr // rb,), in_specs=in_specs,
            out_specs=pl.BlockSpec((1, 1, rb, cc), lambda i, w_ref: (layer, w_ref[1], i, 0))),
        out_shape=jax.ShapeDtypeStruct((n_layers, 2, hr, cc), F32),
        input_output_aliases=aliases,
        compiler_params=_cparams(("parallel",)),
    )(*args)


def rs_share(fulls):
    n = len(fulls)

    def body(*refs):
        ins, outs = refs[:n], refs[n:2 * n]
        send, recv = refs[2 * n:]
        x, y, c = _place()
        cps = []
        for a in range(n):
            cp = pltpu.make_async_remote_copy(
                src_ref=ins[a].at[:, c], dst_ref=outs[a].at[:, c], send_sem=send.at[a], recv_sem=recv.at[a],
                device_id=(x, y, 1 - c), device_id_type=MESH)
            cp.start()
            cps.append(cp)
        for a in range(n):
            got = outs[a].at[:, 1 - c]
            pltpu.make_async_remote_copy(
                src_ref=got, dst_ref=got, send_sem=send.at[a], recv_sem=recv.at[a],
                device_id=(x, y, 1 - c), device_id_type=MESH).wait_recv()
        for cp in cps:
            cp.wait_send()

    dma = pltpu.SemaphoreType.DMA
    return pl.pallas_call(
        body, name="rs_share", in_specs=[ANY] * n, out_specs=[ANY] * n,
        out_shape=[jax.ShapeDtypeStruct(f.shape, f.dtype) for f in fulls],
        scratch_shapes=[dma((n,)), dma((n,))],
        input_output_aliases={i: i for i in range(n)},
        compiler_params=pltpu.CompilerParams(has_side_effects=True),
    )(*fulls)


def allreduce_small(v):
    r, w = v.shape

    def body(v_ref, o_ref, buf, send, recv, loc):
        x, y, c = _place()
        me = 4 * x + 2 * y + c
        mine = pltpu.make_async_copy(v_ref, buf.at[me], loc)
        mine.start()
        cps = []
        for o in range(1, N_DEV):
            fx, fy, fc = (o >> 2) & 1, (o >> 1) & 1, o & 1
            cp = pltpu.make_async_remote_copy(
                src_ref=v_ref, dst_ref=buf.at[me], send_sem=send.at[o - 1], recv_sem=recv.at[o - 1],
                device_id=(_flip(x, fx), _flip(y, fy), _flip(c, fc)), device_id_type=MESH)
            cp.start()
            cps.append(cp)
        for o in range(1, N_DEV):
            fx, fy, fc = (o >> 2) & 1, (o >> 1) & 1, o & 1
            peer = 4 * _flip(x, fx) + 2 * _flip(y, fy) + _flip(c, fc)
            pltpu.make_async_remote_copy(
                src_ref=v_ref, dst_ref=buf.at[peer], send_sem=send.at[o - 1], recv_sem=recv.at[o - 1],
                device_id=(x, y, c), device_id_type=MESH).wait_recv()
        for cp in cps:
            cp.wait_send()
        mine.wait()
        acc = buf[0]
        for d in range(1, N_DEV):
            acc = acc + buf[d]
        o_ref[...] = acc

    dma = pltpu.SemaphoreType.DMA
    vm = pl.BlockSpec(memory_space=pltpu.VMEM)
    return pl.pallas_call(
        body, name="allreduce_small", in_specs=[vm], out_specs=vm,
        out_shape=jax.ShapeDtypeStruct((r, w), F32),
        scratch_shapes=[pltpu.VMEM((N_DEV, r, w), F32), dma((N_DEV - 1,)), dma((N_DEV - 1,)), dma],
        compiler_params=pltpu.CompilerParams(has_side_effects=True, vmem_limit_bytes=VMEM_LIMIT),
    )(v)


def adamw(w, g, m, v, name):
    r, cc = w.shape
    rb = _tile(r, 256)

    def body(w_ref, g_ref, m_ref, v_ref, d_ref, nm_ref, nv_ref):
        gv = g_ref[...]
        nm = ADAM_B1 * m_ref[...] + (1.0 - ADAM_B1) * gv
        nv = ADAM_B2 * v_ref[...] + (1.0 - ADAM_B2) * (gv * gv)
        m_hat = nm / (1.0 - ADAM_B1 ** ADAM_STEP)
        v_hat = nv / (1.0 - ADAM_B2 ** ADAM_STEP)
        d_ref[...] = -ADAM_LR * (m_hat / (jnp.sqrt(v_hat) + ADAM_EPS) + ADAM_WD * w_ref[...])
        nm_ref[...] = nm
        nv_ref[...] = nv

    blk = pl.BlockSpec((rb, cc), lambda i: (i, 0))
    shp = jax.ShapeDtypeStruct((r, cc), F32)
    return pl.pallas_call(
        body, name=name, grid=(r // rb,), in_specs=[blk] * 4, out_specs=[blk] * 3, out_shape=[shp] * 3,
        compiler_params=_cparams(("parallel",)),
    )(w, g, m, v)


WEIGHTS = ['g_ffn1', 'w_ffn1_gate', 'w_ffn1_up', 'w_ffn1_down', 'g_mix', 'w_in_ab', 'conv_w', 'conv_b', 'ln_a_g',
           'ln_a_b', 'ln_v_g', 'ln_v_b', 'sp_w', 'sp_b', 'w_out_ab', 'w_qkv', 'w_o', 'g_ffn2', 'w_ffn2_gate',
           'w_ffn2_up', 'w_ffn2_down', 'g_final']
BIG = ['w_ffn1_gate', 'w_ffn1_up', 'w_ffn1_down', 'w_in_ab', 'w_out_ab', 'w_qkv', 'w_o', 'w_ffn2_gate', 'w_ffn2_up',
       'w_ffn2_down']
SMALL = ['g_ffn1', 'g_mix', 'g_ffn2', 'g_final', 'conv_b', 'ln_a_g', 'ln_a_b', 'ln_v_g', 'ln_v_b', 'sp_b', 'sp_w']


CARRY_WEIGHTS = {"ffn_gateup": 9.2e6, "ffn_down": 6.1e6, "mm_in": 5.9e6, "mm_out": 3.3e6}


def _use_order(depth):
    order = []
    for layer in range(depth):
        order += [('w_ffn1_gate', layer), ('w_ffn1_up', layer), ('w_ffn1_down', layer)]
        order += [('w_in_ab', layer // 2), ('w_out_ab', layer // 2)] if layer % 2 == 0 else [('w_qkv', layer // 2), ('w_o', layer // 2)]
        order += [('w_ffn2_gate', layer), ('w_ffn2_up', layer), ('w_ffn2_down', layer)]
    return order


def _rows(a):
    return a.reshape(-1, LANES)


def _pack(parts):
    v = jnp.concatenate([_rows(p) for p in parts], axis=0)
    pad = (-v.shape[0]) % 8
    return jnp.pad(v, ((0, pad), (0, 0)))


def _unpack(v, shapes):
    out, r = [], 0
    for s in shapes:
        n = 1
        for d in s:
            n *= d
        n //= LANES
        out.append(v[r:r + n].reshape(s))
        r += n
    return out


def kernel(x, g_ffn1, w_ffn1_gate, w_ffn1_up, w_ffn1_down, g_mix, w_in_ab, conv_w, conv_b, ln_a_g, ln_a_b, ln_v_g, ln_v_b, sp_w, sp_b, w_out_ab, w_qkv, w_o, g_ffn2, w_ffn2_gate, w_ffn2_up, w_ffn2_down, g_final, loss_target, m_g_ffn1, m_w_ffn1_gate, m_w_ffn1_up, m_w_ffn1_down, m_g_mix, m_w_in_ab, m_conv_w, m_conv_b, m_ln_a_g, m_ln_a_b, m_ln_v_g, m_ln_v_b, m_sp_w, m_sp_b, m_w_out_ab, m_w_qkv, m_w_o, m_g_ffn2, m_w_ffn2_gate, m_w_ffn2_up, m_w_ffn2_down, m_g_final, v_g_ffn1, v_w_ffn1_gate, v_w_ffn1_up, v_w_ffn1_down, v_g_mix, v_w_in_ab, v_conv_w, v_conv_b, v_ln_a_g, v_ln_a_b, v_ln_v_g, v_ln_v_b, v_sp_w, v_sp_b, v_w_out_ab, v_w_qkv, v_w_o, v_g_ffn2, v_w_ffn2_gate, v_w_ffn2_up, v_w_ffn2_down, v_g_final):
    p = dict(locals())
    n_seq, seq, d = x.shape
    t = n_seq * seq
    depth = g_ffn1.shape[0]
    core = lax.axis_index("c")
    chip = 2 * lax.axis_index("x") + lax.axis_index("y")
    xf = x.reshape(t, d)
    target = loss_target.reshape(t, d)

    items = []
    for name in BIG:
        for layer in range(p[name].shape[0]):
            items.append((name, layer))
    chip1 = chip.reshape(1).astype(jnp.int32)
    placed = {it: place_shard(p[it[0]], it[1], chip1, BF16, "place_shard") for it in items}
    first = [('w_ffn1_gate', 0), ('w_ffn1_up', 0), ('w_ffn1_down', 0)]
    gathered, (conv_w4,) = allgather_weights([placed[it] for it in first],
                                             [place_shard(conv_w, 0, chip1, F32, "place_conv_w")])
    wt = dict(zip(first, gathered))
    waiting = [it for it in _use_order(depth) if it not in wt]

    def riders(name):
        room, take = CARRY_WEIGHTS[name], []
        for it in list(waiting):
            if placed[it].size <= room:
                room -= placed[it].size
                take.append(it)
                waiting.remove(it)
        return (take, gather_carry([placed[it] for it in take])) if take else (take, None)

    def landed(take, carried):
        wt.update(zip(take, carried))

    def weight(it):
        if it not in wt:
            waiting.remove(it)
            (wt[it],), _ = allgather_weights([placed[it]], [])
        return wt[it]

    c_mix = conv_w4.shape[2] * N_CHIPS
    conv_full = jnp.transpose(conv_w4, (1, 0, 2)).reshape(CONV_WIDTH, c_mix)
    vec = lambda a: a.reshape(1, -1)
    sp_bt = sp_b[0].T
    sp_wt = jnp.transpose(sp_w[0], (0, 2, 1))
    d_ff = w_ffn1_gate.shape[2]
    n_in = w_in_ab.shape[2]
    n_qkv = w_qkv.shape[2] // 3

    saved = []
    xc = xf
    for layer in range(depth):
        s = {}
        for half, (gn, wn) in enumerate((('g_ffn1', 'w_ffn1'), ('g_ffn2', 'w_ffn2'))):
            if half == 1:
                s['x_mix'] = xc
                s['h_mix'] = rmsnorm_fwd(xc, vec(g_mix[layer]), "norm_mix")
                if layer % 2 == 0:
                    w_in, w_out = weight(('w_in_ab', layer // 2)), weight(('w_out_ab', layer // 2))
                    take, carry = riders("mm_in")
                    (z,), got = colmm(s['h_mix'], [w_in], n_in, BF16, "mm_in", carry)
                    landed(take, got)
                    cat, a1 = mix_fwd(z, conv_full, conv_b, ln_a_g, ln_a_b, vec(ln_v_g), vec(ln_v_b), sp_w[0], sp_bt, seq)
                    s.update(z=z, cat=cat, a1=a1)
                    take, carry = riders("mm_out")
                    xc, got = rowmm(cat, w_out, xc, 1.0, "mm_out", carry)
                    landed(take, got)
                else:
                    (qkv,), _ = colmm(s['h_mix'], [weight(('w_qkv', layer // 2))], n_qkv, BF16, "mm_qkv")
                    o, tot, cnt = attn_fwd(qkv, n_seq, seq)
                    s.update(qkv=qkv, o=o, tot=tot, cnt=cnt)
                    xc, _ = rowmm(o, weight(('w_o', layer // 2)), xc, 1.0, "mm_o")
            s['x' + wn] = xc
            h = rmsnorm_fwd(xc, vec(p[gn][layer]), "norm_ffn")
            w_gate, w_up, w_down = (weight((wn + part, layer)) for part in ('_gate', '_up', '_down'))
            take, carry = riders("ffn_gateup")
            (silu, udsilu, act), got = colmm(h, [w_gate, w_up], d_ff, BF16, "ffn_gateup", carry, swiglu=True)
            landed(take, got)
            take, carry = riders("ffn_down")
            xc, got = rowmm(act, w_down, xc, 0.5, "ffn_down", carry)
            landed(take, got)
            s.update({'h' + wn: h, 'swiglu' + wn: (silu, udsilu), 'act' + wn: act})
        saved.append(s)

    loss8, dx, dxb, dg_final = loss_head(xc, vec(g_final), target)
    loss = lax.psum(loss8[0, 0], ("x", "y", "c"))

    gw = {}
    gs = {}
    core1 = core.reshape(1).astype(jnp.int32)
    ready = []
    part, recv = {}, {}

    def leaving():
        its = list(ready)
        ready.clear()
        g4 = [gw[it].reshape(N_CHIPS, 2, gw[it].shape[1] // 2, gw[it].shape[2]) for it in its]
        sums = [rs_add(g, sb, core1, REDUCE_DTYPE, "rs_add") for g, sb in zip(g4, rs_exchange(g4))]
        part.update(zip(its, sums))
        return its, send_carry(sums)

    for layer in reversed(range(depth)):
        s = saved[layer]
        for half, (gn, wn) in reversed(list(enumerate((('g_ffn1', 'w_ffn1'), ('g_ffn2', 'w_ffn2'))))):
            wd = wt[(wn + '_down', layer)]
            dgate, dup = rowmm_t(dxb, wd, 0.5, BF16, "ffn_bwd_act", swiglu=s['swiglu' + wn])
            gw[(wn + '_down', layer)] = dw_row(s['act' + wn], dxb, 0.5, "ffn_dw_down")
            gw[(wn + '_gate', layer)], gw[(wn + '_up', layer)] = dw_col(s['h' + wn], [dgate, dup], N_CHIPS, d_ff, "ffn_dw_gateup")
            ready.extend([(wn + '_down', layer), (wn + '_gate', layer), (wn + '_up', layer)])
            its, carry = leaving()
            (dx, dxb, dg), got = colmm_t([dgate, dup], [wt[(wn + '_gate', layer)], wt[(wn + '_up', layer)]], d_ff,
                                         s['x' + wn], vec(p[gn][layer]), dx, "ffn_bwd_in", carry)
            recv.update(zip(its, got))
            gs[(gn, layer)] = dg
            if half == 1:
                if layer % 2 == 0:
                    i = layer // 2
                    w_out = wt[('w_out_ab', i)]
                    dcat = rowmm_t(dxb, w_out, 1.0, F32, "mm_out_t")
                    gw[('w_out_ab', i)] = dw_row(s['cat'], dxb, 1.0, "dw_out")
                    dz, da1, dcb, dlag, dlab, dlvg, dlvb, dspw, dspb = mix_bwd_point(
                        dcat, s['z'], s['a1'], ln_a_g, ln_a_b, vec(ln_v_g), vec(ln_v_b), sp_w[0], sp_wt, sp_bt, seq)
                    dz, dcw = mix_bwd_conv(dz, da1, s['z'], conv_full, seq)
                    gs.update({('conv_b', i): dcb, ('ln_a_g', i): dlag, ('ln_a_b', i): dlab, ('ln_v_g', i): dlvg,
                               ('ln_v_b', i): dlvb, ('sp_w', i): dspw, ('sp_b', i): dspb[:, :, 0], ('conv_w', i): dcw})
                    (gw[('w_in_ab', i)],) = dw_col(s['h_mix'], [dz], N_CHIPS, n_in, "dw_in")
                    ready.extend([('w_out_ab', i), ('w_in_ab', i)])
                    its, carry = leaving()
                    (dx, dxb, dg), got = colmm_t([dz], [wt[('w_in_ab', i)]], n_in, s['x_mix'], vec(g_mix[layer]), dx,
                                                 "mm_in_t", carry)
                    recv.update(zip(its, got))
                else:
                    i = layer // 2
                    w_o4 = wt[('w_o', i)]
                    do = rowmm_t(dxb, w_o4, 1.0, BF16, "mm_o_t")
                    gw[('w_o', i)] = dw_row(s['o'], dxb, 1.0, "dw_o")
                    dq, dk, dv = attn_bwd(s['qkv'], do, s['tot'], s['cnt'], n_seq, seq)
                    dqkv = jnp.concatenate([dq, dk, dv], axis=0)
                    (gw[('w_qkv', i)],) = dw_col(s['h_mix'], [dqkv], N_CHIPS, n_qkv, "dw_qkv")
                    ready.extend([('w_o', i), ('w_qkv', i)])
                    its, carry = leaving()
                    (dx, dxb, dg), got = colmm_t([dqkv], [wt[('w_qkv', i)]], n_qkv, s['x_mix'], vec(g_mix[layer]), dx,
                                                 "mm_qkv_t", carry)
                    recv.update(zip(its, got))
                gs[('g_mix', layer)] = dg
    grad_x = dx.reshape(x.shape)

    assert not ready and set(recv) == set(items)
    where = jnp.stack([chip, core]).astype(jnp.int32)
    fulls = []
    for name in BIG:
        full = None
        n_layers = p[name].shape[0]
        for layer in range(n_layers):
            full = rs_sum(recv[(name, layer)], part[(name, layer)], where, full, layer, n_layers, "rs_sum")
        fulls.append(full)
    shared = rs_share(fulls)
    grads = {name: sh.reshape(p[name].shape) for name, sh in zip(BIG, shared)}

    stack = lambda name: jnp.concatenate([gs[(name, layer)].reshape((1,) + p[name].shape[1:]) for layer in range(p[name].shape[0])], axis=0)
    small_g = [stack(name) if name != 'g_final' else dg_final.reshape(p[name].shape) for name in SMALL]
    packed = _pack(small_g + [gs[('conv_w', 0)]])
    red = allreduce_small(packed)
    outs = _unpack(red, [p[name].shape for name in SMALL] + [(CONV_WIDTH, c_mix)])
    for name, g in zip(SMALL, outs[:-1]):
        grads[name] = g
    conv_g = outs[-1].reshape(CONV_WIDTH, N_CHIPS, c_mix // N_CHIPS)
    grads['conv_w'] = lax.dynamic_index_in_dim(conv_g, chip, axis=1, keepdims=False).reshape(conv_w.shape)

    delta, new_m, new_v = {}, {}, {}
    for name in BIG:
        shp = p[name].shape
        two = lambda a: a.reshape(shp[0] * shp[1], shp[2])
        dl, nm, nv = adamw(two(p[name]), two(grads[name]), two(p['m_' + name]), two(p['v_' + name]), "adamw")
        delta[name], new_m[name], new_v[name] = dl.reshape(shp), nm.reshape(shp), nv.reshape(shp)
    small_names = SMALL + ['conv_w']
    pk = lambda pre: _pack([p[pre + name] for name in small_names])
    dl, nm, nv = adamw(pk(''), _pack([grads[name] for name in small_names]), pk('m_'), pk('v_'), "adamw_small")
    shapes = [p[name].shape for name in small_names]
    for dst, val in ((delta, dl), (new_m, nm), (new_v, nv)):
        for name, a in zip(small_names, _unpack(val, shapes)):
            dst[name] = a

    return (loss, grad_x, *[grads[n] for n in WEIGHTS], *[delta[n] for n in WEIGHTS],
            *[new_m[n] for n in WEIGHTS], *[new_v[n] for n in WEIGHTS])
```

```python
import functools

import jax
import jax.numpy as jnp
from jax import lax
from jax.experimental import pallas as pl
from jax.experimental.pallas import tpu as pltpu

F32 = jnp.float32
BF16 = jnp.bfloat16
EPS = 1e-6
HEAD_DIM = 64
CONV_WIDTH = 31
CHUNK = 128
KBLK = 128
ATT_BLOCK = 256
DW_TOKENS = 2048
CONV_ROWS = 64
MASKED = -1e30
STICK_GONE = -110.0
LANES = 128
HALO = 32
ADAM_LR, ADAM_B1, ADAM_B2, ADAM_EPS, ADAM_WD, ADAM_STEP = 0.001, 0.9, 0.999, 1e-08, 0.01, 10
VMEM_LIMIT = 56 * 1024 * 1024
MESH = pl.DeviceIdType.MESH
N_CHIPS = 4
N_DEV = 8
REDUCE_DTYPE = BF16


def _cparams(sem):
    return pltpu.CompilerParams(dimension_semantics=sem, vmem_limit_bytes=VMEM_LIMIT)


def _nt(a, b):
    return lax.dot_general(a, b, (((1,), (1,)), ((), ())), preferred_element_type=F32)


def _tn(a, b):
    return lax.dot_general(a, b, (((0,), (0,)), ((), ())), preferred_element_type=F32)


def _nn(a, b):
    return jnp.dot(a, b, preferred_element_type=F32)


def _sigmoid(x):
    return 0.5 * jnp.tanh(0.5 * x) + 0.5


def _tile(t, want):
    if t <= want:
        return t
    for cand in range(want - want % 8, 7, -8):
        if t % cand == 0:
            return cand
    raise ValueError((t, want))


def rmsnorm_fwd(x, g, name):
    t, d = x.shape
    tm = _tile(t, 512)

    def body(x_ref, g_ref, h_ref):
        xv = x_ref[...]
        r = lax.rsqrt(jnp.mean(xv * xv, axis=-1, keepdims=True) + EPS)
        h_ref[...] = (xv * r * g_ref[...]).astype(BF16)

    return pl.pallas_call(
        body, name=name, grid=(t // tm,),
        in_specs=[pl.BlockSpec((tm, d), lambda i: (i, 0)), pl.BlockSpec((1, d), lambda i: (0, 0))],
        out_specs=pl.BlockSpec((tm, d), lambda i: (i, 0)),
        out_shape=jax.ShapeDtypeStruct((t, d), BF16),
        compiler_params=_cparams(("parallel",)),
    )(x, g)


def colmm(h, ws, nu, out_dtype, name, carry=None, swiglu=False):
    t, k = h.shape
    j, _, nj = ws[0].shape
    per = nj // nu
    units = j * per
    tm = _tile(t, 512)
    nw = len(ws)
    n_out = 3 if swiglu else nw

    def body(*refs):
        h_ref = refs[0]
        hv = h_ref[...]
        if swiglu:
            silu_ref, udsilu_ref, act_ref = refs[1 + nw:]
            gv = _nn(hv, refs[1][0])
            uv = _nn(hv, refs[2][0])
            s = _sigmoid(gv)
            silu = gv * s
            silu_ref[0] = silu.astype(out_dtype)
            udsilu_ref[0] = (uv * (s + silu * (1.0 - s))).astype(out_dtype)
            act_ref[0] = (silu * uv).astype(out_dtype)
            return
        for n in range(nw):
            res = _nn(hv, refs[1 + n][0]).astype(out_dtype)
            for u in range(per):
                refs[1 + nw + n][u] = res[:, u * nu:(u + 1) * nu]

    assert not swiglu or (nw == 2 and per == 1)
    w_spec = pl.BlockSpec((1, k, nj), lambda s, i: (s, 0, 0))
    o_spec = pl.BlockSpec((per, tm, nu), lambda s, i: (s, i, 0))
    return _call(
        body, name=name, grid=(j, t // tm),
        in_specs=[pl.BlockSpec((tm, k), lambda s, i: (i, 0))] + [w_spec] * nw,
        out_specs=[o_spec] * n_out,
        out_shape=[jax.ShapeDtypeStruct((units, t, nu), out_dtype)] * n_out,
        args=[h, *ws], sem=("parallel", "parallel"), carry=carry)


def rowmm(a, w, resid, scale, name, carry=None, norm_g=None):
    u_n, t, ku = a.shape
    n = w.shape[2]
    tm = _tile(t, 256)

    def body(a_ref, w_ref, r_ref, *rest):
        acc = jnp.zeros((tm, n), F32)
        for u in range(u_n):
            acc = acc + _nn(a_ref[u], w_ref[u])
        out = r_ref[...] + scale * acc
        if norm_g is None:
            (o_ref,) = rest
        else:
            g_ref, o_ref, h_ref = rest
            r = lax.rsqrt(jnp.mean(out * out, axis=-1, keepdims=True) + EPS)
            h_ref[...] = (out * r * g_ref[...]).astype(BF16)
        o_ref[...] = out

    row = pl.BlockSpec((tm, n), lambda i: (i, 0))
    normed = norm_g is not None
    outs, carried = _call(
        body, name=name, grid=(t // tm,),
        in_specs=[pl.BlockSpec((u_n, tm, ku), lambda i: (0, i, 0)), pl.BlockSpec((u_n, ku, n), lambda i: (0, 0, 0)),
                  row] + [pl.BlockSpec((1, n), lambda i: (0, 0))] * normed,
        out_specs=[row] + [row] * normed,
        out_shape=[jax.ShapeDtypeStruct((t, n), F32)] + [jax.ShapeDtypeStruct((t, n), BF16)] * normed,
        args=[a, w, resid] + [norm_g] * normed, sem=("parallel",), carry=carry)
    return (outs[0], outs[1] if normed else None), carried


def rowmm_t(dyb, w, scale, out_dtype, name, swiglu=None):
    t, n = dyb.shape
    u_n, ku, _ = w.shape
    tm = _tile(t, 512)

    if swiglu is None:
        def body(dy_ref, w_ref, o_ref):
            o_ref[0] = (scale * _nt(dy_ref[...], w_ref[0])).astype(out_dtype)

        return pl.pallas_call(
            body, name=name, grid=(u_n, t // tm),
            in_specs=[pl.BlockSpec((tm, n), lambda u, i: (i, 0)), pl.BlockSpec((1, ku, n), lambda u, i: (u, 0, 0))],
            out_specs=pl.BlockSpec((1, tm, ku), lambda u, i: (u, i, 0)),
            out_shape=jax.ShapeDtypeStruct((u_n, t, ku), out_dtype),
            compiler_params=_cparams(("parallel", "parallel")),
        )(dyb, w)

    def body(dy_ref, w_ref, silu_ref, udsilu_ref, dg_ref, du_ref):
        dy = dy_ref[...]
        for u in range(u_n):
            dact = scale * _nt(dy, w_ref[u])
            dg_ref[u] = (dact * udsilu_ref[u].astype(F32)).astype(BF16)
            du_ref[u] = (dact * silu_ref[u].astype(F32)).astype(BF16)

    blk = pl.BlockSpec((u_n, tm, ku), lambda i: (0, i, 0))
    return pl.pallas_call(
        body, name=name, grid=(t // tm,),
        in_specs=[pl.BlockSpec((tm, n), lambda i: (i, 0)), pl.BlockSpec((u_n, ku, n), lambda i: (0, 0, 0)), blk, blk],
        out_specs=[blk] * 2, out_shape=[jax.ShapeDtypeStruct((u_n, t, ku), BF16)] * 2,
        compiler_params=_cparams(("parallel",)),
    )(dyb, w, *swiglu)


def colmm_t(dzs, ws, nu, x, g, dy_in, name, carry=None):
    t, k = x.shape
    j, _, nj = ws[0].shape
    per = nj // nu
    units = j * per
    nw = len(ws)
    tm = _tile(t, 256)

    def body(*refs):
        dz_refs = refs[:nw]
        w_refs = refs[nw:2 * nw]
        x_ref, g_ref, dy_ref, dx_ref, dxb_ref, dg_ref = refs[2 * nw:]
        i = pl.program_id(0)
        dh = jnp.zeros((tm, k), F32)
        for n in range(nw):
            for u in range(units):
                wv = w_refs[n][u // per, :, (u % per) * nu:(u % per + 1) * nu]
                dh = dh + _nt(dz_refs[n][u], wv)
        xv = x_ref[...]
        gv = g_ref[...]
        r = lax.rsqrt(jnp.mean(xv * xv, axis=-1, keepdims=True) + EPS)
        uu = dh * gv
        dx = dy_ref[...] + r * uu - xv * (r * r * r * jnp.mean(uu * xv, axis=-1, keepdims=True))
        dx_ref[...] = dx
        dxb_ref[...] = dx.astype(BF16)
        part = jnp.sum(dh * (xv * r), axis=0, keepdims=True)

        @pl.when(i == 0)
        def _():
            dg_ref[...] = part

        @pl.when(i > 0)
        def _():
            dg_ref[...] += part

    dz_spec = pl.BlockSpec((units, tm, nu), lambda i: (0, i, 0))
    w_spec = pl.BlockSpec((j, k, nj), lambda i: (0, 0, 0))
    row = pl.BlockSpec((tm, k), lambda i: (i, 0))
    vec = pl.BlockSpec((1, k), lambda i: (0, 0))
    return _call(
        body, name=name, grid=(t // tm,),
        in_specs=[dz_spec] * nw + [w_spec] * nw + [row, vec, row],
        out_specs=[row, row, vec],
        out_shape=[jax.ShapeDtypeStruct((t, k), F32), jax.ShapeDtypeStruct((t, k), BF16),
                   jax.ShapeDtypeStruct((1, k), F32)],
        args=[*dzs, *ws, x, g, dy_in], sem=("arbitrary",), carry=carry)


def dw_col(h, dzs, j, nu, name):
    t, k = h.shape
    units = dzs[0].shape[0]
    per = units // j
    nw = len(dzs)
    tt = _tile(t, DW_TOKENS)

    def body(*refs):
        h_ref = refs[0]
        s = pl.program_id(1)
        hv = h_ref[...]
        @pl.when(s == 0)
        def _():
            for n in range(nw):
                refs[1 + nw + n][...] = jnp.zeros_like(refs[1 + nw + n])

        for n in range(nw):
            for u in range(per):
                refs[1 + nw + n][0, :, u * nu:(u + 1) * nu] += _tn(hv, refs[1 + n][u])

    return pl.pallas_call(
        body, name=name, grid=(j, t // tt),
        in_specs=[pl.BlockSpec((tt, k), lambda u, s: (s, 0))] + [pl.BlockSpec((per, tt, nu), lambda u, s: (u, s, 0))] * nw,
        out_specs=[pl.BlockSpec((1, k, per * nu), lambda u, s: (u, 0, 0))] * nw,
        out_shape=[jax.ShapeDtypeStruct((j, k, per * nu), F32)] * nw,
        compiler_params=_cparams(("parallel", "arbitrary")),
    )(h, *dzs)


def dw_row(a, dyb, scale, name):
    u_n, t, ku = a.shape
    n = dyb.shape[1]
    tt = _tile(t, DW_TOKENS)

    def body(a_ref, dy_ref, o_ref):
        @pl.when(pl.program_id(1) == 0)
        def _():
            o_ref[...] = jnp.zeros_like(o_ref)

        o_ref[0] += scale * _tn(a_ref[0], dy_ref[...])

    return pl.pallas_call(
        body, name=name, grid=(u_n, t // tt),
        in_specs=[pl.BlockSpec((1, tt, ku), lambda u, s: (u, s, 0)), pl.BlockSpec((tt, n), lambda u, s: (s, 0))],
        out_specs=pl.BlockSpec((1, ku, n), lambda u, s: (u, 0, 0)),
        out_shape=jax.ShapeDtypeStruct((u_n, ku, n), F32),
        compiler_params=_cparams(("parallel", "arbitrary")),
    )(a, dyb)


def loss_head(x, g, target):
    t, d = x.shape
    tm = _tile(t, 256)

    def body(x_ref, g_ref, t_ref, loss_ref, dx_ref, dxb_ref, dg_ref):
        i = pl.program_id(0)
        xv = x_ref[...]
        gv = g_ref[...]
        r = lax.rsqrt(jnp.mean(xv * xv, axis=-1, keepdims=True) + EPS)
        xh = xv * r
        err = xh * gv - t_ref[...]
        dy = err * (1.0 / d)
        uu = dy * gv
        dx = r * uu - xv * (r * r * r * jnp.mean(uu * xv, axis=-1, keepdims=True))
        dx_ref[...] = dx
        dxb_ref[...] = dx.astype(BF16)
        dg_part = jnp.sum(dy * xh, axis=0, keepdims=True)
        row = jnp.sum(err * err, axis=-1, keepdims=True) * (0.5 / d)
        l_part = jnp.zeros((8, LANES), F32) + jnp.sum(row, axis=0, keepdims=True)

        @pl.when(i == 0)
        def _():
            dg_ref[...] = dg_part
            loss_ref[...] = l_part

        @pl.when(i > 0)
        def _():
            dg_ref[...] += dg_part
            loss_ref[...] += l_part

    row = pl.BlockSpec((tm, d), lambda i: (i, 0))
    vec = pl.BlockSpec((1, d), lambda i: (0, 0))
    return pl.pallas_call(
        body, name="loss_head", grid=(t // tm,),
        in_specs=[row, vec, row],
        out_specs=[pl.BlockSpec((8, LANES), lambda i: (0, 0)), row, row, vec],
        out_shape=[jax.ShapeDtypeStruct((8, LANES), F32), jax.ShapeDtypeStruct((t, d), F32),
                   jax.ShapeDtypeStruct((t, d), BF16), jax.ShapeDtypeStruct((1, d), F32)],
        compiler_params=_cparams(("arbitrary",)),
    )(x, g, target)


def _split(v):
    hi = v.astype(BF16)
    lo = (v - hi.astype(F32)).astype(BF16)
    return hi, lo


def _keysums(v, m_ext):
    hi, lo = _split(v)
    outs = []
    for j in range(v.shape[1] // KBLK):
        sl = slice(j * KBLK, (j + 1) * KBLK)
        cs = _nn(jnp.concatenate([hi[:, sl], lo[:, sl]], axis=1), m_ext)
        outs.append((cs[:, :KBLK], cs[:, KBLK:]))
    return outs


def _softplus_parts(z):
    sp = jnp.maximum(z, 0.0) + jnp.log(1.0 + jnp.exp(-jnp.abs(z)))
    return sp, z - sp


def _sum_matrices():
    r = lax.broadcasted_iota(jnp.int32, (2 * KBLK, 2 * KBLK), 0) % KBLK
    c = lax.broadcasted_iota(jnp.int32, (2 * KBLK, 2 * KBLK), 1)
    suffix = jnp.where((r > c) | (c >= KBLK), 1.0, 0.0).astype(BF16)
    prefix = jnp.where((r <= c) | (c >= KBLK), 1.0, 0.0).astype(BF16)
    return suffix, prefix


def attn_fwd(qkv, n_seq, seq):
    t = qkv.shape[1]
    n_pairs = (qkv.shape[0] // 3) * 2
    bq = min(ATT_BLOCK, seq)
    nq = seq // bq
    nsub = bq // KBLK
    suffix_m, _ = _sum_matrices()

    def body(q_ref, k_ref, v_ref, m_ref, o_ref, tot_ref, cnt_ref):
        qi = pl.program_id(2)
        step_id = (pl.program_id(0) * n_pairs + pl.program_id(1)) * nq + qi
        lane = lax.broadcasted_iota(jnp.int32, (bq, LANES), 1)
        is_a = lane < HEAD_DIM
        q2 = q_ref[0] * jnp.asarray(HEAD_DIM ** -0.5, BF16)
        qs = (jnp.where(is_a, q2, jnp.zeros_like(q2)), jnp.where(is_a, jnp.zeros_like(q2), q2))
        m_ext = m_ref[...]
        row = lax.broadcasted_iota(jnp.int32, (bq, bq), 0)
        col = lax.broadcasted_iota(jnp.int32, (bq, bq), 1)
        diag_mask = col < row

        def block(kj, carry, mask):
            off = pl.multiple_of(kj * bq, bq)
            k2 = k_ref[0, pl.ds(off, bq), :]
            v2 = v_ref[0, pl.ds(off, bq), :]
            out = []
            for h in range(2):
                rem, acc = carry[h]
                z = _nt(qs[h], k2)
                if mask is not None:
                    z = jnp.where(mask, z, MASKED)
                sp, ls = _softplus_parts(z)
                sums = _keysums(-sp, m_ext)
                parts = [None] * nsub
                for j in reversed(range(nsub)):
                    suf, total = sums[j]
                    parts[j] = jnp.exp(ls[:, j * KBLK:(j + 1) * KBLK] + suf + rem)
                    rem = rem + total
                a = jnp.concatenate(parts, axis=1)
                out.append((rem, acc + _nn(a.astype(BF16), v2)))
            return tuple(out)

        def most_left(c):
            return jnp.maximum(jnp.max(c[0][0]), jnp.max(c[1][0]))

        def more(s):
            return (s[0] < qi) & (s[1] > STICK_GONE)

        def step(s):
            c = block(qi - 1 - s[0], s[2], None)
            return s[0] + 1, most_left(c), c

        zero = jnp.zeros((bq, LANES), F32)
        carry = block(qi, ((zero, zero), (zero, zero)), diag_mask)
        n_left, _, carry = lax.while_loop(more, step, (jnp.int32(0), most_left(carry), carry))
        o_ref[0] = jnp.where(is_a, carry[0][1], carry[1][1]).astype(BF16)
        tot_ref[...] = jnp.where(is_a, carry[0][0], carry[1][0])
        cnt_ref[step_id] = n_left.astype(F32)

    upp = qkv.shape[0] // 3
    return pl.pallas_call(
        body, name="attn_fwd", grid=(n_seq, n_pairs, nq),
        in_specs=[pl.BlockSpec((1, bq, LANES), lambda b, p, i: (p // 2, b * nq + i, p % 2)),
                  pl.BlockSpec((1, seq, LANES), lambda b, p, i: (upp + p // 2, b, p % 2)),
                  pl.BlockSpec((1, seq, LANES), lambda b, p, i: (2 * upp + p // 2, b, p % 2)),
                  pl.BlockSpec((2 * KBLK, 2 * KBLK), lambda b, p, i: (0, 0))],
        out_specs=[pl.BlockSpec((1, bq, LANES), lambda b, p, i: (p // 2, b * nq + i, p % 2)),
                   pl.BlockSpec((bq, LANES), lambda b, p, i: (b * nq + i, p)),
                   pl.BlockSpec(memory_space=pltpu.SMEM)],
        out_shape=[jax.ShapeDtypeStruct((upp, t, 2 * LANES), BF16), jax.ShapeDtypeStruct((t, n_pairs * LANES), F32),
                   jax.ShapeDtypeStruct((n_seq * n_pairs * nq,), F32)],
        compiler_params=_cparams(("arbitrary", "arbitrary", "arbitrary")),
    )(qkv, qkv, qkv, suffix_m)


def attn_bwd(qkv, do, tot, cnt, n_seq, seq):
    t = qkv.shape[1]
    upp = qkv.shape[0] // 3
    n_pairs = upp * 2
    bq = min(ATT_BLOCK, seq)
    nq = seq // bq
    nsub = bq // KBLK
    _, prefix_m = _sum_matrices()
    scale = HEAD_DIM ** -0.5

    def body(q_ref, k_ref, v_ref, do_ref, tot_ref, m_ref, cnt_ref, dq_ref, dk_ref, dv_ref, dk_acc, dv_acc):
        qi = pl.program_id(2)
        step_id = (pl.program_id(0) * n_pairs + pl.program_id(1)) * nq + qi
        n_left = jnp.clip(cnt_ref[step_id].astype(jnp.int32), 0, qi)
        lane = lax.broadcasted_iota(jnp.int32, (bq, LANES), 1)
        is_a = lane < HEAD_DIM

        def halves(v2):
            z2 = jnp.zeros_like(v2)
            return jnp.where(is_a, v2, z2), jnp.where(is_a, z2, v2)

        qs = halves(q_ref[0] * jnp.asarray(scale, BF16))
        dos = halves(do_ref[0])
        tot2 = tot_ref[...]
        swapped = pltpu.roll(tot2, HEAD_DIM, 1)
        tots = (jnp.where(is_a, tot2, swapped), jnp.where(is_a, swapped, tot2))
        m_ext = m_ref[...]
        row = lax.broadcasted_iota(jnp.int32, (bq, bq), 0)
        col = lax.broadcasted_iota(jnp.int32, (bq, bq), 1)
        diag_mask = col < row

        @pl.when(qi == 0)
        def _():
            dk_acc[...] = jnp.zeros_like(dk_acc)
            dv_acc[...] = jnp.zeros_like(dv_acc)

        def block(kj, carry, mask):
            off = pl.multiple_of(kj * bq, bq)
            k2 = k_ref[0, pl.ds(off, bq), :]
            v2 = v_ref[0, pl.ds(off, bq), :]
            ks = halves(k2)
            dq = carry[2]
            dk_part = jnp.zeros((bq, LANES), F32)
            dv_part = jnp.zeros((bq, LANES), F32)
            out = []
            for h in range(2):
                pre, gpre = carry[h]
                z = _nt(qs[h], k2)
                if mask is not None:
                    z = jnp.where(mask, z, MASKED)
                sp, ls = _softplus_parts(z)
                sums = _keysums(-sp, m_ext)
                parts = []
                for j in range(nsub):
                    pin, ptot = sums[j]
                    parts.append(jnp.exp(ls[:, j * KBLK:(j + 1) * KBLK] + (tots[h] - (pre + pin))))
                    pre = pre + ptot
                a = jnp.concatenate(parts, axis=1)
                g = a * _nt(dos[h], v2)
                gsums = _keysums(g, m_ext)
                parts = []
                for j in range(nsub):
                    gin, gtot = gsums[j]
                    parts.append(gpre + gin)
                    gpre = gpre + gtot
                dz = g - jnp.exp(ls) * jnp.concatenate(parts, axis=1)
                dzb = dz.astype(BF16)
                dq = dq + _nn(dzb, ks[h])
                dk_part = dk_part + _tn(dzb, qs[h])
                dv_part = dv_part + _tn(a.astype(BF16), dos[h])
                out.append((pre, gpre))
            dk_acc[pl.ds(off, bq), :] += dk_part
            dv_acc[pl.ds(off, bq), :] += dv_part
            return (out[0], out[1], dq)

        zero = jnp.zeros((bq, LANES), F32)
        carry = lax.fori_loop(qi - n_left, qi, lambda kj, c: block(kj, c, None), ((zero, zero), (zero, zero), zero))
        carry = block(qi, carry, diag_mask)
        dq_ref[0] = (carry[2] * scale).astype(BF16)

        @pl.when(qi == nq - 1)
        def _():
            dk_ref[0] = dk_acc[...].astype(BF16)
            dv_ref[0] = dv_acc[...].astype(BF16)

    qblk = lambda b, p, i: (p // 2, b * nq + i, p % 2)
    kv_out = pl.BlockSpec((1, seq, LANES), lambda b, p, i: (p // 2, b, p % 2))
    shp = jax.ShapeDtypeStruct((upp, t, 2 * LANES), BF16)
    return pl.pallas_call(
        body, name="attn_bwd", grid=(n_seq, n_pairs, nq),
        in_specs=[pl.BlockSpec((1, bq, LANES), qblk),
                  pl.BlockSpec((1, seq, LANES), lambda b, p, i: (upp + p // 2, b, p % 2)),
                  pl.BlockSpec((1, seq, LANES), lambda b, p, i: (2 * upp + p // 2, b, p % 2)),
                  pl.BlockSpec((1, bq, LANES), qblk),
                  pl.BlockSpec((bq, LANES), lambda b, p, i: (b * nq + i, p)),
                  pl.BlockSpec((2 * KBLK, 2 * KBLK), lambda b, p, i: (0, 0)),
                  pl.BlockSpec(memory_space=pltpu.SMEM)],
        out_specs=[pl.BlockSpec((1, bq, LANES), qblk), kv_out, kv_out],
        out_shape=[shp, shp, shp],
        scratch_shapes=[pltpu.VMEM((seq, LANES), F32), pltpu.VMEM((seq, LANES), F32)],
        compiler_params=_cparams(("parallel", "parallel", "arbitrary")),
    )(qkv, qkv, qkv, do, tot, prefix_m, cnt)


def _ln_stats(v):
    mu = jnp.mean(v, axis=-1, keepdims=True)
    vc = v - mu
    rstd = lax.rsqrt(jnp.mean(vc * vc, axis=-1, keepdims=True) + EPS)
    return vc * rstd, rstd


def _glu_into(a0_ref, av_ref, ag_ref, hv_ref, hg_ref, first):
    hv = hv_ref[0].astype(F32)
    hg = hg_ref[0].astype(F32)
    a0_ref[0:HALO, :] = jnp.where(first, 0.0, hv * _sigmoid(hg))
    av = av_ref[0].astype(F32)
    ag = ag_ref[0].astype(F32)
    a0_ref[HALO:, :] = av * _sigmoid(ag)


def _conv_taps(ref, tm, first, shifted_ref):
    for b in range(8):
        offs = [o for o in range(first, first + CONV_WIDTH) if o % 8 == b]
        if not offs:
            continue
        n_rows = max(offs) - b + tm
        shifted_ref[b, 0:n_rows, :] = ref[pl.ds(b, n_rows), :]
        for o in offs:
            yield o, o - first, shifted_ref[b, pl.ds(o - b, tm), :]


def _tril_mask():
    r = lax.broadcasted_iota(jnp.int32, (CHUNK, CHUNK), 0)
    c = lax.broadcasted_iota(jnp.int32, (CHUNK, CHUNK), 1)
    return c <= r


def mix_fwd(z, conv_w, conv_b, ln_a_g, ln_a_b, ln_v_g, ln_v_b, sp_w, sp_bt, seq):
    _, t, c = z.shape
    tm = _tile(seq, 512)
    tiles_per_seq = seq // tm
    groups = c // LANES
    hb = tm // HALO

    def body(av_ref, ag_ref, u_ref, v_ref, hv_ref, hg_ref, cw_ref, cb_ref, lag_ref, lab_ref, lvg_ref, lvb_ref,
             spw_ref, spb_ref, cat_ref, a1_ref, a0_ref, sh_ref):
        i = pl.program_id(0)
        _glu_into(a0_ref, av_ref, ag_ref, hv_ref, hg_ref, i % tiles_per_seq == 0)
        acc = jnp.zeros((tm, c), F32) + cb_ref[...]
        for off, k, rows in _conv_taps(a0_ref, tm, HALO - (CONV_WIDTH - 1), sh_ref):
            acc = acc + cw_ref[k:k + 1, :] * rows
        a1_ref[...] = acc
        xh, _ = _ln_stats(acc)
        a2 = xh * lag_ref[...] + lab_ref[...]
        a3 = (a2 * _sigmoid(a2)).astype(BF16)
        half = c // 2
        cat_ref[0] = a3[:, :half]
        cat_ref[1] = a3[:, half:]
        tril = _tril_mask()
        for g in range(groups):
            sl = slice(g * LANES, (g + 1) * LANES)
            xh, _ = _ln_stats(v_ref[0][:, sl].astype(F32))
            vn = (xh * lvg_ref[:, sl] + lvb_ref[:, sl]).astype(BF16)
            w = jnp.where(tril, spw_ref[g], 0.0).astype(BF16)
            bias = spb_ref[:, g:g + 1]
            for ch in range(tm // CHUNK):
                rows = slice(ch * CHUNK, (ch + 1) * CHUNK)
                vs = _nn(w, vn[rows]) + bias
                bo = (u_ref[0][rows, sl].astype(F32) * vs).astype(BF16)
                cat_ref[2 + (g * LANES) // half, rows, (g * LANES) % half:(g * LANES) % half + LANES] = bo

    unit = lambda u: pl.BlockSpec((1, tm, c), lambda i: (u, i, 0))
    halo = lambda u: pl.BlockSpec((1, HALO, c), lambda i: (u, jnp.maximum(i * hb - 1, 0), 0))
    vec = pl.BlockSpec((1, c), lambda i: (0, 0))
    return pl.pallas_call(
        body, name="mix_fwd", grid=(t // tm,),
        in_specs=[unit(0), unit(1), unit(2), unit(3), halo(0), halo(1),
                  pl.BlockSpec((CONV_WIDTH, c), lambda i: (0, 0)), vec, vec, vec, vec, vec,
                  pl.BlockSpec((groups, CHUNK, CHUNK), lambda i: (0, 0, 0)),
                  pl.BlockSpec((CHUNK, groups), lambda i: (0, 0))],
        out_specs=[pl.BlockSpec((4, tm, c // 2), lambda i: (0, i, 0)), pl.BlockSpec((tm, c), lambda i: (i, 0))],
        out_shape=[jax.ShapeDtypeStruct((4, t, c // 2), BF16), jax.ShapeDtypeStruct((t, c), F32)],
        scratch_shapes=[pltpu.VMEM((HALO + tm, c), F32), pltpu.VMEM((8, HALO + tm, c), F32)],
        compiler_params=_cparams(("parallel",)),
    )(z, z, z, z, z, z, conv_w, conv_b, ln_a_g, ln_a_b, ln_v_g, ln_v_b, sp_w, sp_bt)


def mix_bwd_point(dcat, z, a1, ln_a_g, ln_a_b, ln_v_g, ln_v_b, sp_w, sp_wt, sp_bt, seq):
    _, t, c = z.shape
    tm = _tile(seq, 512)
    groups = c // LANES
    half = c // 2

    def body(dc_ref, u_ref, v_ref, a1_ref, lag_ref, lab_ref, lvg_ref, lvb_ref, spw_ref, spwt_ref, spb_ref,
             dz_ref, da1_ref, dcb_ref, dlag_ref, dlab_ref, dlvg_ref, dlvb_ref, dspw_ref, dspb_ref):
        i = pl.program_id(0)
        last = pl.num_programs(0) - 1

        @pl.when(i == 0)
        def _():
            for r in (dcb_ref, dlag_ref, dlab_ref, dlvg_ref, dlvb_ref, dspw_ref, dspb_ref):
                r[...] = jnp.zeros_like(r)

        da3 = jnp.concatenate([dc_ref[0], dc_ref[1]], axis=-1)
        xh, rstd = _ln_stats(a1_ref[...])
        a2 = xh * lag_ref[...] + lab_ref[...]
        s = _sigmoid(a2)
        da2 = da3 * (s * (1.0 + a2 * (1.0 - s)))
        dlag_ref[...] += jnp.sum(da2 * xh, axis=0, keepdims=True)
        dlab_ref[...] += jnp.sum(da2, axis=0, keepdims=True)
        dxh = da2 * lag_ref[...]
        da1 = rstd * (dxh - jnp.mean(dxh, axis=-1, keepdims=True) - xh * jnp.mean(dxh * xh, axis=-1, keepdims=True))
        da1_ref[...] = da1
        dcb_ref[...] += jnp.sum(da1, axis=0, keepdims=True)

        tril = _tril_mask()
        for g in range(groups):
            sl = slice(g * LANES, (g + 1) * LANES)
            xh, rstd = _ln_stats(v_ref[0][:, sl].astype(F32))
            lg = lvg_ref[:, sl]
            vnb = (xh * lg + lvb_ref[:, sl]).astype(BF16)
            w = jnp.where(tril, spw_ref[g], 0.0).astype(BF16)
            wt = jnp.where(tril.T, spwt_ref[g], 0.0).astype(BF16)
            bias = spb_ref[:, g:g + 1]
            dbo_all = dc_ref[2 + (g * LANES) // half][:, (g * LANES) % half:(g * LANES) % half + LANES]
            dvn_parts = []
            dw_acc = jnp.zeros((CHUNK, CHUNK), F32)
            db_acc = jnp.zeros((CHUNK, LANES), F32)
            for ch in range(tm // CHUNK):
                rows = slice(ch * CHUNK, (ch + 1) * CHUNK)
                vs = _nn(w, vnb[rows]) + bias
                dbo = dbo_all[rows]
                uv = u_ref[0][rows, sl].astype(F32)
                dz_ref[0, rows, sl] = (dbo * vs).astype(BF16)
                dvs = dbo * uv
                dvsb = dvs.astype(BF16)
                dvn_parts.append(_nn(wt, dvsb))
                dw_acc = dw_acc + _nt(dvsb, vnb[rows])
                db_acc = db_acc + dvs
            dvn = jnp.concatenate(dvn_parts, axis=0)
            dspw_ref[g] += jnp.where(tril, dw_acc, 0.0)
            dspb_ref[g] += db_acc
            dlvg_ref[:, sl] += jnp.sum(dvn * xh, axis=0, keepdims=True)
            dlvb_ref[:, sl] += jnp.sum(dvn, axis=0, keepdims=True)
            dxh = dvn * lg
            dv = rstd * (dxh - jnp.mean(dxh, axis=-1, keepdims=True) - xh * jnp.mean(dxh * xh, axis=-1, keepdims=True))
            dz_ref[1, :, sl] = dv.astype(BF16)

        @pl.when(i == last)
        def _():
            for g in range(groups):
                dspb_ref[g] = jnp.zeros((CHUNK, LANES), F32) + jnp.sum(dspb_ref[g], axis=-1, keepdims=True)

    unit = lambda u: pl.BlockSpec((1, tm, c), lambda i: (u, i, 0))
    vec = pl.BlockSpec((1, c), lambda i: (0, 0))
    sq = pl.BlockSpec((groups, CHUNK, CHUNK), lambda i: (0, 0, 0))
    vshape = jax.ShapeDtypeStruct((1, c), F32)
    sshape = jax.ShapeDtypeStruct((groups, CHUNK, CHUNK), F32)
    return pl.pallas_call(
        body, name="mix_bwd_point", grid=(t // tm,),
        in_specs=[pl.BlockSpec((4, tm, half), lambda i: (0, i, 0)), unit(2), unit(3),
                  pl.BlockSpec((tm, c), lambda i: (i, 0)), vec, vec, vec, vec, sq, sq,
                  pl.BlockSpec((CHUNK, groups), lambda i: (0, 0))],
        out_specs=[pl.BlockSpec((2, tm, c), lambda i: (1, i, 0)), pl.BlockSpec((tm, c), lambda i: (i, 0)),
                   vec, vec, vec, vec, vec, sq, sq],
        out_shape=[jax.ShapeDtypeStruct((4, t, c), BF16), jax.ShapeDtypeStruct((t, c), F32),
                   vshape, vshape, vshape, vshape, vshape, sshape, sshape],
        compiler_params=_cparams(("arbitrary",)),
    )(dcat, z, z, a1, ln_a_g, ln_a_b, ln_v_g, ln_v_b, sp_w, sp_wt, sp_bt)


def mix_bwd_conv(dz, da1, z, conv_w, seq):
    _, t, c = z.shape
    tm = _tile(seq, 512)
    tiles_per_seq = seq // tm
    hb = tm // HALO
    n_halo_blocks = t // HALO

    rc = _tile(tm, CONV_ROWS)

    def body(dz_in_ref, d_ref, dh_ref, av_ref, ag_ref, cw_ref, dz_ref, dcw_ref, d1_ref, sh_ref, part_ref):
        del dz_in_ref
        i = pl.program_id(0)

        @pl.when(i == 0)
        def _():
            part_ref[...] = jnp.zeros_like(part_ref)

        d1_ref[0:tm, :] = d_ref[...]
        d1_ref[tm:, :] = jnp.where((i + 1) % tiles_per_seq == 0, 0.0, dh_ref[...])
        taps = []
        for b in range(8):
            offs = [o for o in range(CONV_WIDTH) if o % 8 == b]
            n_rows = max(offs) - b + tm
            sh_ref[b, 0:n_rows, :] = d1_ref[pl.ds(b, n_rows), :]
            taps += [(b, o - b, CONV_WIDTH - 1 - o) for o in offs]

        def chunk(ci, carry):
            r0 = pl.multiple_of(ci * rc, rc)
            av = av_ref[0, pl.ds(r0, rc), :].astype(F32)
            s = _sigmoid(ag_ref[0, pl.ds(r0, rc), :].astype(F32))
            a0 = av * s
            da0 = jnp.zeros((rc, c), F32)
            for b, ro, k in taps:
                rows = sh_ref[b, pl.ds(r0 + ro, rc), :]
                da0 = da0 + cw_ref[k:k + 1, :] * rows
                prod = a0 * rows
                part_ref[k] += functools.reduce(lambda p, q: p + q, [prod[8 * r:8 * r + 8] for r in range(rc // 8)])
            dz_ref[0, pl.ds(r0, rc), :] = (da0 * s).astype(BF16)
            dz_ref[1, pl.ds(r0, rc), :] = (da0 * av * s * (1.0 - s)).astype(BF16)
            return carry

        lax.fori_loop(0, tm // rc, chunk, 0)

        @pl.when(i == pl.num_programs(0) - 1)
        def _():
            dcw_ref[...] = jnp.sum(part_ref[...], axis=1)

    unit = lambda u: pl.BlockSpec((1, tm, c), lambda i: (u, i, 0))
    return pl.pallas_call(
        body, name="mix_bwd_conv", grid=(t // tm,),
        in_specs=[pl.BlockSpec(memory_space=pl.ANY), pl.BlockSpec((tm, c), lambda i: (i, 0)),
                  pl.BlockSpec((HALO, c), lambda i: (jnp.minimum((i + 1) * hb, n_halo_blocks - 1), 0)),
                  unit(0), unit(1), pl.BlockSpec((CONV_WIDTH, c), lambda i: (0, 0))],
        out_specs=[pl.BlockSpec((2, tm, c), lambda i: (0, i, 0)), pl.BlockSpec((CONV_WIDTH, c), lambda i: (0, 0))],
        out_shape=[jax.ShapeDtypeStruct(dz.shape, BF16), jax.ShapeDtypeStruct((CONV_WIDTH, c), F32)],
        scratch_shapes=[pltpu.VMEM((tm + HALO, c), F32), pltpu.VMEM((8, tm + HALO, c), F32),
                        pltpu.VMEM((CONV_WIDTH, 8, c), F32)],
        input_output_aliases={0: 0},
        compiler_params=_cparams(("arbitrary",)),
    )(dz, da1, da1, z, z, conv_w)


CHIP_FLIPS = ((1, 0), (0, 1), (1, 1))
ANY = pl.BlockSpec(memory_space=pl.ANY)


def _place():
    return lax.axis_index("x"), lax.axis_index("y"), lax.axis_index("c")


def _flip(v, f):
    return 1 - v if f else v


def place_shard(w, layer, chip, dtype, name):
    _, r, cc = w.shape
    rb = _tile(r, 512)

    def body(chip_ref, w_ref, o_ref):
        del chip_ref
        o_ref[0] = w_ref[0].astype(dtype)

    return pl.pallas_call(
        body, name=name,
        grid_spec=pltpu.PrefetchScalarGridSpec(
            num_scalar_prefetch=1, grid=(r // rb,),
            in_specs=[pl.BlockSpec((1, rb, cc), lambda i, chip_ref: (layer, i, 0))],
            out_specs=pl.BlockSpec((1, rb, cc), lambda i, chip_ref: (chip_ref[0], i, 0))),
        out_shape=jax.ShapeDtypeStruct((N_CHIPS, r, cc), dtype),
        compiler_params=_cparams(("parallel",)),
    )(chip, w)


class Carry:
    def __init__(self, arrays, out_shapes, aliased, sem_shapes, start, finish):
        self.arrays, self.out_shapes, self.aliased, self.sem_shapes = list(arrays), list(out_shapes), aliased, list(sem_shapes)
        self.start, self.finish = start, finish


def _call(body, *, name, grid, in_specs, out_specs, out_shape, args, sem, scratch_shapes=(), carry=None):
    if carry is None:
        res = pl.pallas_call(body, name=name, grid=grid, in_specs=in_specs, out_specs=out_specs, out_shape=out_shape,
                             scratch_shapes=list(scratch_shapes), compiler_params=_cparams(sem))(*args)
        return list(res), []
    n_in, n_out, n_scr, nc = len(args), len(out_shape), len(scratch_shapes), len(carry.arrays)

    def full_body(*refs):
        ins, refs = refs[:n_in], refs[n_in:]
        c_ins, refs = refs[:nc], refs[nc:]
        outs, refs = refs[:n_out], refs[n_out:]
        c_outs, refs = refs[:nc], refs[nc:]
        scr, sems = refs[:n_scr], refs[n_scr:]
        first = functools.reduce(lambda a, b: a & b, [pl.program_id(d) == 0 for d in range(len(grid))])
        last = functools.reduce(lambda a, b: a & b, [pl.program_id(d) == grid[d] - 1 for d in range(len(grid))])

        @pl.when(first)
        def _():
            carry.start(c_ins, c_outs, sems)

        body(*ins, *outs, *scr)

        @pl.when(last)
        def _():
            carry.finish(c_ins, c_outs, sems)

    res = pl.pallas_call(
        full_body, name=name, grid=grid, in_specs=list(in_specs) + [ANY] * nc, out_specs=list(out_specs) + [ANY] * nc,
        out_shape=list(out_shape) + carry.out_shapes, scratch_shapes=list(scratch_shapes) + carry.sem_shapes,
        input_output_aliases={n_in + i: n_out + i for i in range(nc)} if carry.aliased else {},
        compiler_params=pltpu.CompilerParams(dimension_semantics=("arbitrary",) * len(grid), vmem_limit_bytes=VMEM_LIMIT,
                                             has_side_effects=True),
    )(*args, *carry.arrays)
    return list(res[:n_out]), list(res[n_out:])


def _gather_ops(shapes, whole):
    n = len(shapes)

    def rows(a, c):
        hr = shapes[a][1] // 2
        return pl.ds(pl.multiple_of(c * hr, 16), hr)

    def start(ins, outs, sems):
        ici_send, ici_recv = sems[0], sems[1]
        x, y, c = _place()
        k = 2 * x + y
        for a in range(n):
            for o, (fx, fy) in enumerate(CHIP_FLIPS):
                src = ins[a].at[k] if whole[a] else ins[a].at[k, rows(a, c)]
                dst = outs[a].at[k] if whole[a] else outs[a].at[k, rows(a, c)]
                pltpu.make_async_remote_copy(
                    src_ref=src, dst_ref=dst, send_sem=ici_send.at[3 * a + o], recv_sem=ici_recv.at[3 * a + o],
                    device_id=(_flip(x, fx), _flip(y, fy), c), device_id_type=MESH).start()

    def finish(ins, outs, sems):
        ici_send, ici_recv, d2d_send, d2d_recv = sems
        x, y, c = _place()
        k = 2 * x + y
        sibling = (x, y, 1 - c)

        def copy(ref, send, recv, a, o):
            return pltpu.make_async_remote_copy(src_ref=ref, dst_ref=ref, send_sem=send.at[3 * a + o],
                                                recv_sem=recv.at[3 * a + o], device_id=sibling, device_id_type=MESH)

        for a in range(n):
            for o, (fx, fy) in enumerate(CHIP_FLIPS):
                kk = 2 * _flip(x, fx) + _flip(y, fy)
                landed = outs[a].at[kk] if whole[a] else outs[a].at[kk, rows(a, c)]
                copy(landed, ici_send, ici_recv, a, o).wait_recv()
                if not whole[a]:
                    copy(landed, d2d_send, d2d_recv, a, o).start()
        for a in range(n):
            for o, (fx, fy) in enumerate(CHIP_FLIPS):
                kk = 2 * _flip(x, fx) + _flip(y, fy)
                mine = ins[a].at[k] if whole[a] else ins[a].at[k, rows(a, c)]
                copy(mine, ici_send, ici_recv, a, o).wait_send()
                if not whole[a]:
                    copy(outs[a].at[kk, rows(a, 1 - c)], d2d_send, d2d_recv, a, o).wait_recv()
                    copy(outs[a].at[kk, rows(a, c)], d2d_send, d2d_recv, a, o).wait_send()

    dma = pltpu.SemaphoreType.DMA
    return start, finish, [dma((3 * n,))] * 4


def gather_carry(bufs):
    start, finish, sems = _gather_ops([b.shape for b in bufs], [False] * len(bufs))
    return Carry(bufs, [jax.ShapeDtypeStruct(b.shape, b.dtype) for b in bufs], True, sems, start, finish)


def allgather_weights(shards, smalls):
    bufs = list(shards) + list(smalls)
    n = len(bufs)
    start, finish, sems = _gather_ops([b.shape for b in bufs], [False] * len(shards) + [True] * len(smalls))

    def body(*refs):
        start(refs[:n], refs[n:2 * n], refs[2 * n:])
        finish(refs[:n], refs[n:2 * n], refs[2 * n:])

    res = pl.pallas_call(
        body, name="allgather_weights", in_specs=[ANY] * n, out_specs=[ANY] * n,
        out_shape=[jax.ShapeDtypeStruct(b.shape, b.dtype) for b in bufs], scratch_shapes=sems,
        input_output_aliases={i: i for i in range(n)},
        compiler_params=pltpu.CompilerParams(has_side_effects=True),
    )(*bufs)
    return res[:len(shards)], res[len(shards):]


def rs_exchange(grads):
    n = len(grads)

    def body(*refs):
        ins, outs = refs[:n], refs[n:2 * n]
        send, recv = refs[2 * n:]
        x, y, c = _place()
        cps = []
        for a in range(n):
            cp = pltpu.make_async_remote_copy(
                src_ref=ins[a].at[:, 1 - c], dst_ref=outs[a], send_sem=send.at[a], recv_sem=recv.at[a],
                device_id=(x, y, 1 - c), device_id_type=MESH)
            cp.start()
            cps.append(cp)
        for cp in cps:
            cp.wait()

    dma = pltpu.SemaphoreType.DMA
    return pl.pallas_call(
        body, name="rs_exchange", in_specs=[ANY] * n, out_specs=[ANY] * n,
        out_shape=[jax.ShapeDtypeStruct((g.shape[0],) + g.shape[2:], g.dtype) for g in grads],
        scratch_shapes=[dma((n,)), dma((n,))],
        compiler_params=pltpu.CompilerParams(has_side_effects=True),
    )(*grads)


def rs_add(g, sib, core, out_dtype, name):
    nk, _, hr, cc = g.shape
    rb = _tile(hr, 256)

    def body(core_ref, g_ref, s_ref, o_ref):
        del core_ref
        o_ref[0] = (g_ref[0, 0] + s_ref[0]).astype(out_dtype)

    return pl.pallas_call(
        body, name=name,
        grid_spec=pltpu.PrefetchScalarGridSpec(
            num_scalar_prefetch=1, grid=(nk, hr // rb),
            in_specs=[pl.BlockSpec((1, 1, rb, cc), lambda k, i, core_ref: (k, core_ref[0], i, 0)),
                      pl.BlockSpec((1, rb, cc), lambda k, i, core_ref: (k, i, 0))],
            out_specs=pl.BlockSpec((1, rb, cc), lambda k, i, core_ref: (k, i, 0))),
        out_shape=jax.ShapeDtypeStruct((nk, hr, cc), out_dtype),
        compiler_params=_cparams(("parallel", "parallel")),
    )(core, g, sib)


def send_carry(parts):
    n = len(parts)

    def copies(ins, outs, sems):
        x, y, c = _place()
        for a in range(n):
            for o, (fx, fy) in enumerate(CHIP_FLIPS):
                kk = 2 * _flip(x, fx) + _flip(y, fy)
                yield pltpu.make_async_remote_copy(
                    src_ref=ins[a].at[kk], dst_ref=outs[a].at[o], send_sem=sems[0].at[3 * a + o],
                    recv_sem=sems[1].at[3 * a + o], device_id=(_flip(x, fx), _flip(y, fy), c), device_id_type=MESH)

    def start(ins, outs, sems):
        for cp in copies(ins, outs, sems):
            cp.start()

    def finish(ins, outs, sems):
        for cp in copies(ins, outs, sems):
            cp.wait()

    dma = pltpu.SemaphoreType.DMA
    return Carry(parts, [jax.ShapeDtypeStruct((3,) + p.shape[1:], p.dtype) for p in parts], False,
                 [dma((3 * n,)), dma((3 * n,))], start, finish)


def rs_sum(recv, part, where, full, layer, n_layers, name):
    _, hr, cc = recv.shape
    rb = _tile(hr, 256)

    def body(*refs):
        r_ref, p_ref, o_ref = refs[1], refs[2], refs[-1]
        o_ref[0, 0] = ((p_ref[0].astype(F32) + r_ref[0].astype(F32)) + r_ref[1].astype(F32)) + r_ref[2].astype(F32)

    in_specs = [pl.BlockSpec((3, rb, cc), lambda i, w_ref: (0, i, 0)),
                pl.BlockSpec((1, rb, cc), lambda i, w_ref: (w_ref[0], i, 0))]
    args = [where, recv, part]
    aliases = {}
    if full is not None:
        in_specs.append(ANY)
        args.append(full)
        aliases = {3: 0}
    return pl.pallas_call(
        body, name=name,
        grid_spec=pltpu.PrefetchScalarGridSpec(
            num_scalar_prefetch=1, grid=(hr // rb,), in_specs=in_specs,
            out_specs=pl.BlockSpec((1, 1, rb, cc), lambda i, w_ref: (layer, w_ref[1], i, 0))),
        out_shape=jax.ShapeDtypeStruct((n_layers, 2, hr, cc), F32),
        input_output_aliases=aliases,
        compiler_params=_cparams(("parallel",)),
    )(*args)


def rs_share(fulls):
    n = len(fulls)

    def body(*refs):
        ins, outs = refs[:n], refs[n:2 * n]
        send, recv = refs[2 * n:]
        x, y, c = _place()
        cps = []
        for a in range(n):
            cp = pltpu.make_async_remote_copy(
                src_ref=ins[a].at[:, c], dst_ref=outs[a].at[:, c], send_sem=send.at[a], recv_sem=recv.at[a],
                device_id=(x, y, 1 - c), device_id_type=MESH)
            cp.start()
            cps.append(cp)
        for a in range(n):
            got = outs[a].at[:, 1 - c]
            pltpu.make_async_remote_copy(
                src_ref=got, dst_ref=got, send_sem=send.at[a], recv_sem=recv.at[a],
                device_id=(x, y, 1 - c), device_id_type=MESH).wait_recv()
        for cp in cps:
            cp.wait_send()

    dma = pltpu.SemaphoreType.DMA
    return pl.pallas_call(
        body, name="rs_share", in_specs=[ANY] * n, out_specs=[ANY] * n,
        out_shape=[jax.ShapeDtypeStruct(f.shape, f.dtype) for f in fulls],
        scratch_shapes=[dma((n,)), dma((n,))],
        input_output_aliases={i: i for i in range(n)},
        compiler_params=pltpu.CompilerParams(has_side_effects=True),
    )(*fulls)


def allreduce_small(v):
    r, w = v.shape

    def body(v_ref, o_ref, buf, send, recv, loc):
        x, y, c = _place()
        me = 4 * x + 2 * y + c
        mine = pltpu.make_async_copy(v_ref, buf.at[me], loc)
        mine.start()
        cps = []
        for o in range(1, N_DEV):
            fx, fy, fc = (o >> 2) & 1, (o >> 1) & 1, o & 1
            cp = pltpu.make_async_remote_copy(
                src_ref=v_ref, dst_ref=buf.at[me], send_sem=send.at[o - 1], recv_sem=recv.at[o - 1],
                device_id=(_flip(x, fx), _flip(y, fy), _flip(c, fc)), device_id_type=MESH)
            cp.start()
            cps.append(cp)
        for o in range(1, N_DEV):
            fx, fy, fc = (o >> 2) & 1, (o >> 1) & 1, o & 1
            peer = 4 * _flip(x, fx) + 2 * _flip(y, fy) + _flip(c, fc)
            pltpu.make_async_remote_copy(
                src_ref=v_ref, dst_ref=buf.at[peer], send_sem=send.at[o - 1], recv_sem=recv.at[o - 1],
                device_id=(x, y, c), device_id_type=MESH).wait_recv()
        for cp in cps:
            cp.wait_send()
        mine.wait()
        acc = buf[0]
        for d in range(1, N_DEV):
            acc = acc + buf[d]
        o_ref[...] = acc

    dma = pltpu.SemaphoreType.DMA
    vm = pl.BlockSpec(memory_space=pltpu.VMEM)
    return pl.pallas_call(
        body, name="allreduce_small", in_specs=[vm], out_specs=vm,
        out_shape=jax.ShapeDtypeStruct((r, w), F32),
        scratch_shapes=[pltpu.VMEM((N_DEV, r, w), F32), dma((N_DEV - 1,)), dma((N_DEV - 1,)), dma],
        compiler_params=pltpu.CompilerParams(has_side_effects=True, vmem_limit_bytes=VMEM_LIMIT),
    )(v)


def adamw(w, g, m, v, name):
    r, cc = w.shape
    rb = _tile(r, 256)

    def body(w_ref, g_ref, m_ref, v_ref, d_ref, nm_ref, nv_ref):
        gv = g_ref[...]
        nm = ADAM_B1 * m_ref[...] + (1.0 - ADAM_B1) * gv
        nv = ADAM_B2 * v_ref[...] + (1.0 - ADAM_B2) * (gv * gv)
        m_hat = nm / (1.0 - ADAM_B1 ** ADAM_STEP)
        v_hat = nv / (1.0 - ADAM_B2 ** ADAM_STEP)
        d_ref[...] = -ADAM_LR * (m_hat / (jnp.sqrt(v_hat) + ADAM_EPS) + ADAM_WD * w_ref[...])
        nm_ref[...] = nm
        nv_ref[...] = nv

    blk = pl.BlockSpec((rb, cc), lambda i: (i, 0))
    shp = jax.ShapeDtypeStruct((r, cc), F32)
    return pl.pallas_call(
        body, name=name, grid=(r // rb,), in_specs=[blk] * 4, out_specs=[blk] * 3, out_shape=[shp] * 3,
        compiler_params=_cparams(("parallel",)),
    )(w, g, m, v)


WEIGHTS = ['g_ffn1', 'w_ffn1_gate', 'w_ffn1_up', 'w_ffn1_down', 'g_mix', 'w_in_ab', 'conv_w', 'conv_b', 'ln_a_g',
           'ln_a_b', 'ln_v_g', 'ln_v_b', 'sp_w', 'sp_b', 'w_out_ab', 'w_qkv', 'w_o', 'g_ffn2', 'w_ffn2_gate',
           'w_ffn2_up', 'w_ffn2_down', 'g_final']
BIG = ['w_ffn1_gate', 'w_ffn1_up', 'w_ffn1_down', 'w_in_ab', 'w_out_ab', 'w_qkv', 'w_o', 'w_ffn2_gate', 'w_ffn2_up',
       'w_ffn2_down']
SMALL = ['g_ffn1', 'g_mix', 'g_ffn2', 'g_final', 'conv_b', 'ln_a_g', 'ln_a_b', 'ln_v_g', 'ln_v_b', 'sp_b', 'sp_w']


CARRY_WEIGHTS = {"ffn_gateup": 9.2e6, "ffn_down": 6.1e6, "mm_in": 5.9e6, "mm_out": 3.3e6}


def _use_order(depth):
    order = []
    for layer in range(depth):
        order += [('w_ffn1_gate', layer), ('w_ffn1_up', layer), ('w_ffn1_down', layer)]
        order += [('w_in_ab', layer // 2), ('w_out_ab', layer // 2)] if layer % 2 == 0 else [('w_qkv', layer // 2), ('w_o', layer // 2)]
        order += [('w_ffn2_gate', layer), ('w_ffn2_up', layer), ('w_ffn2_down', layer)]
    return order


def _rows(a):
    return a.reshape(-1, LANES)


def _pack(parts):
    v = jnp.concatenate([_rows(p) for p in parts], axis=0)
    pad = (-v.shape[0]) % 8
    return jnp.pad(v, ((0, pad), (0, 0)))


def _unpack(v, shapes):
    out, r = [], 0
    for s in shapes:
        n = 1
        for d in s:
            n *= d
        n //= LANES
        out.append(v[r:r + n].reshape(s))
        r += n
    return out


def kernel(x, g_ffn1, w_ffn1_gate, w_ffn1_up, w_ffn1_down, g_mix, w_in_ab, conv_w, conv_b, ln_a_g, ln_a_b, ln_v_g, ln_v_b, sp_w, sp_b, w_out_ab, w_qkv, w_o, g_ffn2, w_ffn2_gate, w_ffn2_up, w_ffn2_down, g_final, loss_target, m_g_ffn1, m_w_ffn1_gate, m_w_ffn1_up, m_w_ffn1_down, m_g_mix, m_w_in_ab, m_conv_w, m_conv_b, m_ln_a_g, m_ln_a_b, m_ln_v_g, m_ln_v_b, m_sp_w, m_sp_b, m_w_out_ab, m_w_qkv, m_w_o, m_g_ffn2, m_w_ffn2_gate, m_w_ffn2_up, m_w_ffn2_down, m_g_final, v_g_ffn1, v_w_ffn1_gate, v_w_ffn1_up, v_w_ffn1_down, v_g_mix, v_w_in_ab, v_conv_w, v_conv_b, v_ln_a_g, v_ln_a_b, v_ln_v_g, v_ln_v_b, v_sp_w, v_sp_b, v_w_out_ab, v_w_qkv, v_w_o, v_g_ffn2, v_w_ffn2_gate, v_w_ffn2_up, v_w_ffn2_down, v_g_final):
    p = dict(locals())
    n_seq, seq, d = x.shape
    t = n_seq * seq
    depth = g_ffn1.shape[0]
    core = lax.axis_index("c")
    chip = 2 * lax.axis_index("x") + lax.axis_index("y")
    xf = x.reshape(t, d)
    target = loss_target.reshape(t, d)

    items = []
    for name in BIG:
        for layer in range(p[name].shape[0]):
            items.append((name, layer))
    chip1 = chip.reshape(1).astype(jnp.int32)
    placed = {it: place_shard(p[it[0]], it[1], chip1, BF16, "place_shard") for it in items}
    first = [('w_ffn1_gate', 0), ('w_ffn1_up', 0)]
    gathered, (conv_w4,) = allgather_weights([placed[it] for it in first],
                                             [place_shard(conv_w, 0, chip1, F32, "place_conv_w")])
    wt = dict(zip(first, gathered))
    waiting = [it for it in _use_order(depth) if it not in wt]

    def riders(name):
        room, take = CARRY_WEIGHTS[name], []
        for it in list(waiting):
            if placed[it].size <= room:
                room -= placed[it].size
                take.append(it)
                waiting.remove(it)
        return (take, gather_carry([placed[it] for it in take])) if take else (take, None)

    def landed(take, carried):
        wt.update(zip(take, carried))

    def weight(it):
        if it not in wt:
            waiting.remove(it)
            (wt[it],), _ = allgather_weights([placed[it]], [])
        return wt[it]

    c_mix = conv_w4.shape[2] * N_CHIPS
    conv_full = jnp.transpose(conv_w4, (1, 0, 2)).reshape(CONV_WIDTH, c_mix)
    vec = lambda a: a.reshape(1, -1)
    sp_bt = sp_b[0].T
    sp_wt = jnp.transpose(sp_w[0], (0, 2, 1))
    d_ff = w_ffn1_gate.shape[2]
    n_in = w_in_ab.shape[2]
    n_qkv = w_qkv.shape[2] // 3

    saved = []
    xc = xf
    h = rmsnorm_fwd(xc, vec(g_ffn1[0]), "norm_first")
    for layer in range(depth):
        s = {}
        for half, (gn, wn) in enumerate((('g_ffn1', 'w_ffn1'), ('g_ffn2', 'w_ffn2'))):
            if half == 1:
                s['x_mix'], s['h_mix'] = xc, h
                if layer % 2 == 0:
                    w_in = weight(('w_in_ab', layer // 2))
                    take, carry = riders("mm_in")
                    (z,), got = colmm(h, [w_in], n_in, BF16, "mm_in", carry)
                    landed(take, got)
                    cat, a1 = mix_fwd(z, conv_full, conv_b, ln_a_g, ln_a_b, vec(ln_v_g), vec(ln_v_b), sp_w[0], sp_bt, seq)
                    s.update(z=z, cat=cat, a1=a1)
                    w_out = weight(('w_out_ab', layer // 2))
                    take, carry = riders("mm_out")
                    (xc, h), got = rowmm(cat, w_out, xc, 1.0, "mm_out", carry, vec(g_ffn2[layer]))
                    landed(take, got)
                else:
                    (qkv,), _ = colmm(h, [weight(('w_qkv', layer // 2))], n_qkv, BF16, "mm_qkv")
                    o, tot, cnt = attn_fwd(qkv, n_seq, seq)
                    s.update(qkv=qkv, o=o, tot=tot, cnt=cnt)
                    (xc, h), _ = rowmm(o, weight(('w_o', layer // 2)), xc, 1.0, "mm_o", None, vec(g_ffn2[layer]))
            s['x' + wn] = xc
            w_gate, w_up = weight((wn + '_gate', layer)), weight((wn + '_up', layer))
            take, carry = riders("ffn_gateup")
            (silu, udsilu, act), got = colmm(h, [w_gate, w_up], d_ff, BF16, "ffn_gateup", carry, swiglu=True)
            landed(take, got)
            s.update({'h' + wn: h, 'swiglu' + wn: (silu, udsilu), 'act' + wn: act})
            w_down = weight((wn + '_down', layer))
            take, carry = riders("ffn_down")
            following = g_mix[layer] if half == 0 else (g_ffn1[layer + 1] if layer + 1 < depth else None)
            (xc, h), got = rowmm(act, w_down, xc, 0.5, "ffn_down", carry, None if following is None else vec(following))
            landed(take, got)
        saved.append(s)

    loss8, dx, dxb, dg_final = loss_head(xc, vec(g_final), target)
    loss = lax.psum(loss8[0, 0], ("x", "y", "c"))

    gw = {}
    gs = {}
    core1 = core.reshape(1).astype(jnp.int32)
    ready = []
    part, recv = {}, {}

    def leaving():
        its = list(ready)
        ready.clear()
        g4 = [gw[it].reshape(N_CHIPS, 2, gw[it].shape[1] // 2, gw[it].shape[2]) for it in its]
        sums = [rs_add(g, sb, core1, REDUCE_DTYPE, "rs_add") for g, sb in zip(g4, rs_exchange(g4))]
        part.update(zip(its, sums))
        return its, send_carry(sums)

    for layer in reversed(range(depth)):
        s = saved[layer]
        for half, (gn, wn) in reversed(list(enumerate((('g_ffn1', 'w_ffn1'), ('g_ffn2', 'w_ffn2'))))):
            wd = wt[(wn + '_down', layer)]
            dgate, dup = rowmm_t(dxb, wd, 0.5, BF16, "ffn_bwd_act", swiglu=s['swiglu' + wn])
            gw[(wn + '_down', layer)] = dw_row(s['act' + wn], dxb, 0.5, "ffn_dw_down")
            gw[(wn + '_gate', layer)], gw[(wn + '_up', layer)] = dw_col(s['h' + wn], [dgate, dup], N_CHIPS, d_ff, "ffn_dw_gateup")
            ready.extend([(wn + '_down', layer), (wn + '_gate', layer), (wn + '_up', layer)])
            its, carry = leaving()
            (dx, dxb, dg), got = colmm_t([dgate, dup], [wt[(wn + '_gate', layer)], wt[(wn + '_up', layer)]], d_ff,
                                         s['x' + wn], vec(p[gn][layer]), dx, "ffn_bwd_in", carry)
            recv.update(zip(its, got))
            gs[(gn, layer)] = dg
            if half == 1:
                if layer % 2 == 0:
                    i = layer // 2
                    w_out = wt[('w_out_ab', i)]
                    dcat = rowmm_t(dxb, w_out, 1.0, F32, "mm_out_t")
                    gw[('w_out_ab', i)] = dw_row(s['cat'], dxb, 1.0, "dw_out")
                    dz, da1, dcb, dlag, dlab, dlvg, dlvb, dspw, dspb = mix_bwd_point(
                        dcat, s['z'], s['a1'], ln_a_g, ln_a_b, vec(ln_v_g), vec(ln_v_b), sp_w[0], sp_wt, sp_bt, seq)
                    dz, dcw = mix_bwd_conv(dz, da1, s['z'], conv_full, seq)
                    gs.update({('conv_b', i): dcb, ('ln_a_g', i): dlag, ('ln_a_b', i): dlab, ('ln_v_g', i): dlvg,
                               ('ln_v_b', i): dlvb, ('sp_w', i): dspw, ('sp_b', i): dspb[:, :, 0], ('conv_w', i): dcw})
                    (gw[('w_in_ab', i)],) = dw_col(s['h_mix'], [dz], N_CHIPS, n_in, "dw_in")
                    ready.extend([('w_out_ab', i), ('w_in_ab', i)])
                    its, carry = leaving()
                    (dx, dxb, dg), got = colmm_t([dz], [wt[('w_in_ab', i)]], n_in, s['x_mix'], vec(g_mix[layer]), dx,
                                                 "mm_in_t", carry)
                    recv.update(zip(its, got))
                else:
                    i = layer // 2
                    w_o4 = wt[('w_o', i)]
                    do = rowmm_t(dxb, w_o4, 1.0, BF16, "mm_o_t")
                    gw[('w_o', i)] = dw_row(s['o'], dxb, 1.0, "dw_o")
                    dq, dk, dv = attn_bwd(s['qkv'], do, s['tot'], s['cnt'], n_seq, seq)
                    dqkv = jnp.concatenate([dq, dk, dv], axis=0)
                    (gw[('w_qkv', i)],) = dw_col(s['h_mix'], [dqkv], N_CHIPS, n_qkv, "dw_qkv")
                    ready.extend([('w_o', i), ('w_qkv', i)])
                    its, carry = leaving()
                    (dx, dxb, dg), got = colmm_t([dqkv], [wt[('w_qkv', i)]], n_qkv, s['x_mix'], vec(g_mix[layer]), dx,
                                                 "mm_qkv_t", carry)
                    recv.update(zip(its, got))
                gs[('g_mix', layer)] = dg
    grad_x = dx.reshape(x.shape)

    assert not ready and set(recv) == set(items)
    where = jnp.stack([chip, core]).astype(jnp.int32)
    fulls = []
    for name in BIG:
        full = None
        n_layers = p[name].shape[0]
        for layer in range(n_layers):
            full = rs_sum(recv[(name, layer)], part[(name, layer)], where, full, layer, n_layers, "rs_sum")
        fulls.append(full)
    shared = rs_share(fulls)
    grads = {name: sh.reshape(p[name].shape) for name, sh in zip(BIG, shared)}

    stack = lambda name: jnp.concatenate([gs[(name, layer)].reshape((1,) + p[name].shape[1:]) for layer in range(p[name].shape[0])], axis=0)
    small_g = [stack(name) if name != 'g_final' else dg_final.reshape(p[name].shape) for name in SMALL]
    packed = _pack(small_g + [gs[('conv_w', 0)]])
    red = allreduce_small(packed)
    outs = _unpack(red, [p[name].shape for name in SMALL] + [(CONV_WIDTH, c_mix)])
    for name, g in zip(SMALL, outs[:-1]):
        grads[name] = g
    conv_g = outs[-1].reshape(CONV_WIDTH, N_CHIPS, c_mix // N_CHIPS)
    grads['conv_w'] = lax.dynamic_index_in_dim(conv_g, chip, axis=1, keepdims=False).reshape(conv_w.shape)

    delta, new_m, new_v = {}, {}, {}
    for name in BIG:
        shp = p[name].shape
        two = lambda a: a.reshape(shp[0] * shp[1], shp[2])
        dl, nm, nv = adamw(two(p[name]), two(grads[name]), two(p['m_' + name]), two(p['v_' + name]), "adamw")
        delta[name], new_m[name], new_v[name] = dl.reshape(shp), nm.reshape(shp), nv.reshape(shp)
    small_names = SMALL + ['conv_w']
    pk = lambda pre: _pack([p[pre + name] for name in small_names])
    dl, nm, nv = adamw(pk(''), _pack([grads[name] for name in small_names]), pk('m_'), pk('v_'), "adamw_small")
    shapes = [p[name].shape for name in small_names]
    for dst, val in ((delta, dl), (new_m, nm), (new_v, nv)):
        for name, a in zip(small_names, _unpack(val, shapes)):
            dst[name] = a

    return (loss, grad_x, *[grads[n] for n in WEIGHTS], *[delta[n] for n in WEIGHTS],
            *[new_m[n] for n in WEIGHTS], *[new_v[n] for n in WEIGHTS])
```

```python
import functools

import jax
import jax.numpy as jnp
from jax import lax
from jax.experimental import pallas as pl
from jax.experimental.pallas import tpu as pltpu

F32 = jnp.float32
BF16 = jnp.bfloat16
EPS = 1e-6
HEAD_DIM = 64
CONV_WIDTH = 31
CHUNK = 128
KBLK = 128
ATT_BLOCK = 256
ATT_LANES = 256
DW_TOKENS = 2048
CONV_ROWS = 64
MASKED = -1e30
STICK_GONE = -110.0
LANES = 128
HALO = 32
ADAM_LR, ADAM_B1, ADAM_B2, ADAM_EPS, ADAM_WD, ADAM_STEP = 0.001, 0.9, 0.999, 1e-08, 0.01, 10
VMEM_LIMIT = 56 * 1024 * 1024
MESH = pl.DeviceIdType.MESH
N_CHIPS = 4
N_DEV = 8
REDUCE_DTYPE = BF16


def _cparams(sem):
    return pltpu.CompilerParams(dimension_semantics=sem, vmem_limit_bytes=VMEM_LIMIT)


def _nt(a, b):
    return lax.dot_general(a, b, (((1,), (1,)), ((), ())), preferred_element_type=F32)


def _tn(a, b):
    return lax.dot_general(a, b, (((0,), (0,)), ((), ())), preferred_element_type=F32)


def _nn(a, b):
    return jnp.dot(a, b, preferred_element_type=F32)


def _sigmoid(x):
    return 0.5 * jnp.tanh(0.5 * x) + 0.5


def _tile(t, want):
    if t <= want:
        return t
    for cand in range(want - want % 8, 7, -8):
        if t % cand == 0:
            return cand
    raise ValueError((t, want))


def rmsnorm_fwd(x, g, name):
    t, d = x.shape
    tm = _tile(t, 512)

    def body(x_ref, g_ref, h_ref):
        xv = x_ref[...]
        r = lax.rsqrt(jnp.mean(xv * xv, axis=-1, keepdims=True) + EPS)
        h_ref[...] = (xv * r * g_ref[...]).astype(BF16)

    return pl.pallas_call(
        body, name=name, grid=(t // tm,),
        in_specs=[pl.BlockSpec((tm, d), lambda i: (i, 0)), pl.BlockSpec((1, d), lambda i: (0, 0))],
        out_specs=pl.BlockSpec((tm, d), lambda i: (i, 0)),
        out_shape=jax.ShapeDtypeStruct((t, d), BF16),
        compiler_params=_cparams(("parallel",)),
    )(x, g)


def colmm(h, ws, nu, out_dtype, name, carry=None, swiglu=False):
    t, k = h.shape
    j, _, nj = ws[0].shape
    per = nj // nu
    units = j * per
    tm = _tile(t, 512)
    nw = len(ws)
    n_out = 3 if swiglu else nw

    def body(*refs):
        h_ref = refs[0]
        hv = h_ref[...]
        if swiglu:
            silu_ref, udsilu_ref, act_ref = refs[1 + nw:]
            gv = _nn(hv, refs[1][0])
            uv = _nn(hv, refs[2][0])
            s = _sigmoid(gv)
            silu = gv * s
            silu_ref[0] = silu.astype(out_dtype)
            udsilu_ref[0] = (uv * (s + silu * (1.0 - s))).astype(out_dtype)
            act_ref[0] = (silu * uv).astype(out_dtype)
            return
        for n in range(nw):
            res = _nn(hv, refs[1 + n][0]).astype(out_dtype)
            for u in range(per):
                refs[1 + nw + n][u] = res[:, u * nu:(u + 1) * nu]

    assert not swiglu or (nw == 2 and per == 1)
    w_spec = pl.BlockSpec((1, k, nj), lambda s, i: (s, 0, 0))
    o_spec = pl.BlockSpec((per, tm, nu), lambda s, i: (s, i, 0))
    return _call(
        body, name=name, grid=(j, t // tm),
        in_specs=[pl.BlockSpec((tm, k), lambda s, i: (i, 0))] + [w_spec] * nw,
        out_specs=[o_spec] * n_out,
        out_shape=[jax.ShapeDtypeStruct((units, t, nu), out_dtype)] * n_out,
        args=[h, *ws], sem=("parallel", "parallel"), carry=carry)


def rowmm(a, w, resid, scale, name, carry=None, norm_g=None):
    u_n, t, ku = a.shape
    n = w.shape[2]
    tm = _tile(t, 256)

    def body(a_ref, w_ref, r_ref, *rest):
        acc = jnp.zeros((tm, n), F32)
        for u in range(u_n):
            acc = acc + _nn(a_ref[u], w_ref[u])
        out = r_ref[...] + scale * acc
        if norm_g is None:
            (o_ref,) = rest
        else:
            g_ref, o_ref, h_ref = rest
            r = lax.rsqrt(jnp.mean(out * out, axis=-1, keepdims=True) + EPS)
            h_ref[...] = (out * r * g_ref[...]).astype(BF16)
        o_ref[...] = out

    row = pl.BlockSpec((tm, n), lambda i: (i, 0))
    normed = norm_g is not None
    outs, carried = _call(
        body, name=name, grid=(t // tm,),
        in_specs=[pl.BlockSpec((u_n, tm, ku), lambda i: (0, i, 0)), pl.BlockSpec((u_n, ku, n), lambda i: (0, 0, 0)),
                  row] + [pl.BlockSpec((1, n), lambda i: (0, 0))] * normed,
        out_specs=[row] + [row] * normed,
        out_shape=[jax.ShapeDtypeStruct((t, n), F32)] + [jax.ShapeDtypeStruct((t, n), BF16)] * normed,
        args=[a, w, resid] + [norm_g] * normed, sem=("parallel",), carry=carry)
    return (outs[0], outs[1] if normed else None), carried


def rowmm_t(dyb, w, scale, out_dtype, name, swiglu=None):
    t, n = dyb.shape
    u_n, ku, _ = w.shape
    tm = _tile(t, 512)

    if swiglu is None:
        def body(dy_ref, w_ref, o_ref):
            o_ref[0] = (scale * _nt(dy_ref[...], w_ref[0])).astype(out_dtype)

        return pl.pallas_call(
            body, name=name, grid=(u_n, t // tm),
            in_specs=[pl.BlockSpec((tm, n), lambda u, i: (i, 0)), pl.BlockSpec((1, ku, n), lambda u, i: (u, 0, 0))],
            out_specs=pl.BlockSpec((1, tm, ku), lambda u, i: (u, i, 0)),
            out_shape=jax.ShapeDtypeStruct((u_n, t, ku), out_dtype),
            compiler_params=_cparams(("parallel", "parallel")),
        )(dyb, w)

    def body(dy_ref, w_ref, silu_ref, udsilu_ref, dg_ref, du_ref):
        dy = dy_ref[...]
        for u in range(u_n):
            dact = scale * _nt(dy, w_ref[u])
            dg_ref[u] = (dact * udsilu_ref[u].astype(F32)).astype(BF16)
            du_ref[u] = (dact * silu_ref[u].astype(F32)).astype(BF16)

    blk = pl.BlockSpec((u_n, tm, ku), lambda i: (0, i, 0))
    return pl.pallas_call(
        body, name=name, grid=(t // tm,),
        in_specs=[pl.BlockSpec((tm, n), lambda i: (i, 0)), pl.BlockSpec((u_n, ku, n), lambda i: (0, 0, 0)), blk, blk],
        out_specs=[blk] * 2, out_shape=[jax.ShapeDtypeStruct((u_n, t, ku), BF16)] * 2,
        compiler_params=_cparams(("parallel",)),
    )(dyb, w, *swiglu)


def colmm_t(dzs, ws, nu, x, g, dy_in, name, carry=None):
    t, k = x.shape
    j, _, nj = ws[0].shape
    per = nj // nu
    units = j * per
    nw = len(ws)
    tm = _tile(t, 256)

    def body(*refs):
        dz_refs = refs[:nw]
        w_refs = refs[nw:2 * nw]
        x_ref, g_ref, dy_ref, dx_ref, dxb_ref, dg_ref = refs[2 * nw:]
        i = pl.program_id(0)
        dh = jnp.zeros((tm, k), F32)
        for n in range(nw):
            for u in range(units):
                wv = w_refs[n][u // per, :, (u % per) * nu:(u % per + 1) * nu]
                dh = dh + _nt(dz_refs[n][u], wv)
        xv = x_ref[...]
        gv = g_ref[...]
        r = lax.rsqrt(jnp.mean(xv * xv, axis=-1, keepdims=True) + EPS)
        uu = dh * gv
        dx = dy_ref[...] + r * uu - xv * (r * r * r * jnp.mean(uu * xv, axis=-1, keepdims=True))
        dx_ref[...] = dx
        dxb_ref[...] = dx.astype(BF16)
        part = jnp.sum(dh * (xv * r), axis=0, keepdims=True)

        @pl.when(i == 0)
        def _():
            dg_ref[...] = part

        @pl.when(i > 0)
        def _():
            dg_ref[...] += part

    dz_spec = pl.BlockSpec((units, tm, nu), lambda i: (0, i, 0))
    w_spec = pl.BlockSpec((j, k, nj), lambda i: (0, 0, 0))
    row = pl.BlockSpec((tm, k), lambda i: (i, 0))
    vec = pl.BlockSpec((1, k), lambda i: (0, 0))
    return _call(
        body, name=name, grid=(t // tm,),
        in_specs=[dz_spec] * nw + [w_spec] * nw + [row, vec, row],
        out_specs=[row, row, vec],
        out_shape=[jax.ShapeDtypeStruct((t, k), F32), jax.ShapeDtypeStruct((t, k), BF16),
                   jax.ShapeDtypeStruct((1, k), F32)],
        args=[*dzs, *ws, x, g, dy_in], sem=("arbitrary",), carry=carry)


def dw_col(h, dzs, j, nu, name):
    t, k = h.shape
    units = dzs[0].shape[0]
    per = units // j
    nw = len(dzs)
    tt = _tile(t, DW_TOKENS)

    def body(*refs):
        h_ref = refs[0]
        s = pl.program_id(1)
        hv = h_ref[...]
        @pl.when(s == 0)
        def _():
            for n in range(nw):
                refs[1 + nw + n][...] = jnp.zeros_like(refs[1 + nw + n])

        for n in range(nw):
            for u in range(per):
                refs[1 + nw + n][0, :, u * nu:(u + 1) * nu] += _tn(hv, refs[1 + n][u])

    return pl.pallas_call(
        body, name=name, grid=(j, t // tt),
        in_specs=[pl.BlockSpec((tt, k), lambda u, s: (s, 0))] + [pl.BlockSpec((per, tt, nu), lambda u, s: (u, s, 0))] * nw,
        out_specs=[pl.BlockSpec((1, k, per * nu), lambda u, s: (u, 0, 0))] * nw,
        out_shape=[jax.ShapeDtypeStruct((j, k, per * nu), F32)] * nw,
        compiler_params=_cparams(("parallel", "arbitrary")),
    )(h, *dzs)


def dw_row(a, dyb, scale, name):
    u_n, t, ku = a.shape
    n = dyb.shape[1]
    tt = _tile(t, DW_TOKENS)

    def body(a_ref, dy_ref, o_ref):
        @pl.when(pl.program_id(1) == 0)
        def _():
            o_ref[...] = jnp.zeros_like(o_ref)

        o_ref[0] += scale * _tn(a_ref[0], dy_ref[...])

    return pl.pallas_call(
        body, name=name, grid=(u_n, t // tt),
        in_specs=[pl.BlockSpec((1, tt, ku), lambda u, s: (u, s, 0)), pl.BlockSpec((tt, n), lambda u, s: (s, 0))],
        out_specs=pl.BlockSpec((1, ku, n), lambda u, s: (u, 0, 0)),
        out_shape=jax.ShapeDtypeStruct((u_n, ku, n), F32),
        compiler_params=_cparams(("parallel", "arbitrary")),
    )(a, dyb)


def loss_head(x, g, target):
    t, d = x.shape
    tm = _tile(t, 256)

    def body(x_ref, g_ref, t_ref, loss_ref, dx_ref, dxb_ref, dg_ref):
        i = pl.program_id(0)
        xv = x_ref[...]
        gv = g_ref[...]
        r = lax.rsqrt(jnp.mean(xv * xv, axis=-1, keepdims=True) + EPS)
        xh = xv * r
        err = xh * gv - t_ref[...]
        dy = err * (1.0 / d)
        uu = dy * gv
        dx = r * uu - xv * (r * r * r * jnp.mean(uu * xv, axis=-1, keepdims=True))
        dx_ref[...] = dx
        dxb_ref[...] = dx.astype(BF16)
        dg_part = jnp.sum(dy * xh, axis=0, keepdims=True)
        row = jnp.sum(err * err, axis=-1, keepdims=True) * (0.5 / d)
        l_part = jnp.zeros((8, LANES), F32) + jnp.sum(row, axis=0, keepdims=True)

        @pl.when(i == 0)
        def _():
            dg_ref[...] = dg_part
            loss_ref[...] = l_part

        @pl.when(i > 0)
        def _():
            dg_ref[...] += dg_part
            loss_ref[...] += l_part

    row = pl.BlockSpec((tm, d), lambda i: (i, 0))
    vec = pl.BlockSpec((1, d), lambda i: (0, 0))
    return pl.pallas_call(
        body, name="loss_head", grid=(t // tm,),
        in_specs=[row, vec, row],
        out_specs=[pl.BlockSpec((8, LANES), lambda i: (0, 0)), row, row, vec],
        out_shape=[jax.ShapeDtypeStruct((8, LANES), F32), jax.ShapeDtypeStruct((t, d), F32),
                   jax.ShapeDtypeStruct((t, d), BF16), jax.ShapeDtypeStruct((1, d), F32)],
        compiler_params=_cparams(("arbitrary",)),
    )(x, g, target)


def _split(v):
    hi = v.astype(BF16)
    lo = (v - hi.astype(F32)).astype(BF16)
    return hi, lo


def _keysums(v, m_ext):
    hi, lo = _split(v)
    outs = []
    for j in range(v.shape[1] // KBLK):
        sl = slice(j * KBLK, (j + 1) * KBLK)
        cs = _nn(jnp.concatenate([hi[:, sl], lo[:, sl]], axis=1), m_ext)
        outs.append((cs[:, :KBLK], cs[:, KBLK:]))
    return outs


def _softplus_parts(z):
    sp = jnp.maximum(z, 0.0) + jnp.log(1.0 + jnp.exp(-jnp.abs(z)))
    return sp, z - sp


def _sum_matrices():
    r = lax.broadcasted_iota(jnp.int32, (2 * KBLK, 2 * KBLK), 0) % KBLK
    c = lax.broadcasted_iota(jnp.int32, (2 * KBLK, 2 * KBLK), 1)
    suffix = jnp.where((r > c) | (c >= KBLK), 1.0, 0.0).astype(BF16)
    prefix = jnp.where((r <= c) | (c >= KBLK), 1.0, 0.0).astype(BF16)
    return suffix, prefix


def _att_geometry(qkv, seq):
    upp = qkv.shape[0] // 3
    bq = min(ATT_BLOCK, seq)
    per_unit = (2 * LANES) // ATT_LANES
    return upp, bq, seq // bq, bq // KBLK, per_unit, upp * per_unit, ATT_LANES // HEAD_DIM


def _head_lanes(rows, heads):
    lane = lax.broadcasted_iota(jnp.int32, (rows, ATT_LANES), 1)
    return [(lane >= HEAD_DIM * h) & (lane < HEAD_DIM * (h + 1)) for h in range(heads)]


def attn_fwd(qkv, n_seq, seq):
    t = qkv.shape[1]
    upp, bq, nq, nsub, per_unit, groups, heads = _att_geometry(qkv, seq)
    suffix_m, _ = _sum_matrices()

    def body(q_ref, k_ref, v_ref, m_ref, o_ref, tot_ref, cnt_ref):
        qi = pl.program_id(2)
        step_id = (pl.program_id(0) * groups + pl.program_id(1)) * nq + qi
        in_head = _head_lanes(bq, heads)
        only = lambda v, h: jnp.where(in_head[h], v, jnp.zeros_like(v))
        q_all = q_ref[0] * jnp.asarray(HEAD_DIM ** -0.5, BF16)
        qs = [only(q_all, h) for h in range(heads)]
        m_ext = m_ref[...]
        row = lax.broadcasted_iota(jnp.int32, (bq, bq), 0)
        col = lax.broadcasted_iota(jnp.int32, (bq, bq), 1)
        diag_mask = col < row

        def block(kj, carry, mask):
            off = pl.multiple_of(kj * bq, bq)
            k_all = k_ref[0, pl.ds(off, bq), :]
            v_all = v_ref[0, pl.ds(off, bq), :]
            rems, acc = carry
            out = []
            for h in range(heads):
                rem = rems[h]
                z = _nt(qs[h], k_all)
                if mask is not None:
                    z = jnp.where(mask, z, MASKED)
                sp, ls = _softplus_parts(z)
                sums = _keysums(-sp, m_ext)
                parts = [None] * nsub
                for j in reversed(range(nsub)):
                    suf, total = sums[j]
                    parts[j] = jnp.exp(ls[:, j * KBLK:(j + 1) * KBLK] + suf + rem)
                    rem = rem + total
                a = jnp.concatenate(parts, axis=1)
                acc = acc + _nn(a.astype(BF16), only(v_all, h))
                out.append(rem)
            return tuple(out), acc

        def most_left(c):
            return functools.reduce(jnp.maximum, [jnp.max(r) for r in c[0]])

        def more(s):
            return (s[0] < qi) & (s[1] > STICK_GONE)

        def step(s):
            c = block(qi - 1 - s[0], s[2], None)
            return s[0] + 1, most_left(c), c

        zero = jnp.zeros((bq, LANES), F32)
        carry = block(qi, ((zero,) * heads, jnp.zeros((bq, ATT_LANES), F32)), diag_mask)
        n_left, _, (rems, acc) = lax.while_loop(more, step, (jnp.int32(0), most_left(carry), carry))
        o_ref[0] = acc.astype(BF16)
        first = lax.broadcasted_iota(jnp.int32, (bq, LANES), 1) < HEAD_DIM
        tot_ref[...] = jnp.concatenate([jnp.where(first, rems[h], rems[h + 1]) for h in range(0, heads, 2)], axis=1)
        cnt_ref[step_id] = n_left.astype(F32)

    qblk = lambda b, g, i: (g // per_unit, b * nq + i, g % per_unit)
    return pl.pallas_call(
        body, name="attn_fwd", grid=(n_seq, groups, nq),
        in_specs=[pl.BlockSpec((1, bq, ATT_LANES), qblk),
                  pl.BlockSpec((1, seq, ATT_LANES), lambda b, g, i: (upp + g // per_unit, b, g % per_unit)),
                  pl.BlockSpec((1, seq, ATT_LANES), lambda b, g, i: (2 * upp + g // per_unit, b, g % per_unit)),
                  pl.BlockSpec((2 * KBLK, 2 * KBLK), lambda b, g, i: (0, 0))],
        out_specs=[pl.BlockSpec((1, bq, ATT_LANES), qblk),
                   pl.BlockSpec((bq, ATT_LANES), lambda b, g, i: (b * nq + i, g)),
                   pl.BlockSpec(memory_space=pltpu.SMEM)],
        out_shape=[jax.ShapeDtypeStruct((upp, t, 2 * LANES), BF16), jax.ShapeDtypeStruct((t, upp * 2 * LANES), F32),
                   jax.ShapeDtypeStruct((n_seq * groups * nq,), F32)],
        compiler_params=_cparams(("arbitrary", "arbitrary", "arbitrary")),
    )(qkv, qkv, qkv, suffix_m)


def attn_bwd(qkv, do, tot, cnt, n_seq, seq):
    t = qkv.shape[1]
    upp, bq, nq, nsub, per_unit, groups, heads = _att_geometry(qkv, seq)
    _, prefix_m = _sum_matrices()
    scale = HEAD_DIM ** -0.5

    def body(q_ref, k_ref, v_ref, do_ref, tot_ref, m_ref, cnt_ref, dq_ref, dk_ref, dv_ref, dk_acc, dv_acc):
        qi = pl.program_id(2)
        step_id = (pl.program_id(0) * groups + pl.program_id(1)) * nq + qi
        n_left = jnp.clip(cnt_ref[step_id].astype(jnp.int32), 0, qi)
        in_head = _head_lanes(bq, heads)
        only = lambda v, h: jnp.where(in_head[h], v, jnp.zeros_like(v))
        q_all = q_ref[0] * jnp.asarray(scale, BF16)
        do_all = do_ref[0]
        qs = [only(q_all, h) for h in range(heads)]
        dos = [only(do_all, h) for h in range(heads)]
        first = lax.broadcasted_iota(jnp.int32, (bq, LANES), 1) < HEAD_DIM
        tots = []
        for h in range(0, heads, 2):
            both = tot_ref[:, h // 2 * LANES:(h // 2 + 1) * LANES]
            swapped = pltpu.roll(both, HEAD_DIM, 1)
            tots += [jnp.where(first, both, swapped), jnp.where(first, swapped, both)]
        m_ext = m_ref[...]
        row = lax.broadcasted_iota(jnp.int32, (bq, bq), 0)
        col = lax.broadcasted_iota(jnp.int32, (bq, bq), 1)
        diag_mask = col < row

        @pl.when(qi == 0)
        def _():
            dk_acc[...] = jnp.zeros_like(dk_acc)
            dv_acc[...] = jnp.zeros_like(dv_acc)

        def block(kj, carry, mask):
            off = pl.multiple_of(kj * bq, bq)
            k_all = k_ref[0, pl.ds(off, bq), :]
            v_all = v_ref[0, pl.ds(off, bq), :]
            pres, gpres, dq = carry
            dk_part = jnp.zeros((bq, ATT_LANES), F32)
            dv_part = jnp.zeros((bq, ATT_LANES), F32)
            pres_out, gpres_out = [], []
            for h in range(heads):
                pre, gpre = pres[h], gpres[h]
                z = _nt(qs[h], k_all)
                if mask is not None:
                    z = jnp.where(mask, z, MASKED)
                sp, ls = _softplus_parts(z)
                sums = _keysums(-sp, m_ext)
                parts = []
                for j in range(nsub):
                    pin, ptot = sums[j]
                    parts.append(jnp.exp(ls[:, j * KBLK:(j + 1) * KBLK] + (tots[h] - (pre + pin))))
                    pre = pre + ptot
                a = jnp.concatenate(parts, axis=1)
                g = a * _nt(dos[h], v_all)
                gsums = _keysums(g, m_ext)
                parts = []
                for j in range(nsub):
                    gin, gtot = gsums[j]
                    parts.append(gpre + gin)
                    gpre = gpre + gtot
                dz = g - jnp.exp(ls) * jnp.concatenate(parts, axis=1)
                dzb = dz.astype(BF16)
                dq = dq + _nn(dzb, only(k_all, h))
                dk_part = dk_part + _tn(dzb, qs[h])
                dv_part = dv_part + _tn(a.astype(BF16), dos[h])
                pres_out.append(pre)
                gpres_out.append(gpre)
            dk_acc[pl.ds(off, bq), :] += dk_part
            dv_acc[pl.ds(off, bq), :] += dv_part
            return tuple(pres_out), tuple(gpres_out), dq

        zero = jnp.zeros((bq, LANES), F32)
        carry = ((zero,) * heads, (zero,) * heads, jnp.zeros((bq, ATT_LANES), F32))
        carry = lax.fori_loop(qi - n_left, qi, lambda kj, c: block(kj, c, None), carry)
        carry = block(qi, carry, diag_mask)
        dq_ref[0] = (carry[2] * scale).astype(BF16)

        @pl.when(qi == nq - 1)
        def _():
            dk_ref[0] = dk_acc[...].astype(BF16)
            dv_ref[0] = dv_acc[...].astype(BF16)

    qblk = lambda b, g, i: (g // per_unit, b * nq + i, g % per_unit)
    kv_out = pl.BlockSpec((1, seq, ATT_LANES), lambda b, g, i: (g // per_unit, b, g % per_unit))
    shp = jax.ShapeDtypeStruct((upp, t, 2 * LANES), BF16)
    return pl.pallas_call(
        body, name="attn_bwd", grid=(n_seq, groups, nq),
        in_specs=[pl.BlockSpec((1, bq, ATT_LANES), qblk),
                  pl.BlockSpec((1, seq, ATT_LANES), lambda b, g, i: (upp + g // per_unit, b, g % per_unit)),
                  pl.BlockSpec((1, seq, ATT_LANES), lambda b, g, i: (2 * upp + g // per_unit, b, g % per_unit)),
                  pl.BlockSpec((1, bq, ATT_LANES), qblk),
                  pl.BlockSpec((bq, ATT_LANES), lambda b, g, i: (b * nq + i, g)),
                  pl.BlockSpec((2 * KBLK, 2 * KBLK), lambda b, g, i: (0, 0)),
                  pl.BlockSpec(memory_space=pltpu.SMEM)],
        out_specs=[pl.BlockSpec((1, bq, ATT_LANES), qblk), kv_out, kv_out],
        out_shape=[shp, shp, shp],
        scratch_shapes=[pltpu.VMEM((seq, ATT_LANES), F32), pltpu.VMEM((seq, ATT_LANES), F32)],
        compiler_params=_cparams(("parallel", "parallel", "arbitrary")),
    )(qkv, qkv, qkv, do, tot, prefix_m, cnt)


def _ln_stats(v):
    mu = jnp.mean(v, axis=-1, keepdims=True)
    vc = v - mu
    rstd = lax.rsqrt(jnp.mean(vc * vc, axis=-1, keepdims=True) + EPS)
    return vc * rstd, rstd


def _glu_into(a0_ref, av_ref, ag_ref, hv_ref, hg_ref, first):
    hv = hv_ref[0].astype(F32)
    hg = hg_ref[0].astype(F32)
    a0_ref[0:HALO, :] = jnp.where(first, 0.0, hv * _sigmoid(hg))
    av = av_ref[0].astype(F32)
    ag = ag_ref[0].astype(F32)
    a0_ref[HALO:, :] = av * _sigmoid(ag)


def _conv_taps(ref, tm, first, shifted_ref):
    for b in range(8):
        offs = [o for o in range(first, first + CONV_WIDTH) if o % 8 == b]
        if not offs:
            continue
        n_rows = max(offs) - b + tm
        shifted_ref[b, 0:n_rows, :] = ref[pl.ds(b, n_rows), :]
        for o in offs:
            yield o, o - first, shifted_ref[b, pl.ds(o - b, tm), :]


def _tril_mask():
    r = lax.broadcasted_iota(jnp.int32, (CHUNK, CHUNK), 0)
    c = lax.broadcasted_iota(jnp.int32, (CHUNK, CHUNK), 1)
    return c <= r


def mix_fwd(z, conv_w, conv_b, ln_a_g, ln_a_b, ln_v_g, ln_v_b, sp_w, sp_bt, seq):
    _, t, c = z.shape
    tm = _tile(seq, 512)
    tiles_per_seq = seq // tm
    groups = c // LANES
    hb = tm // HALO

    def body(av_ref, ag_ref, u_ref, v_ref, hv_ref, hg_ref, cw_ref, cb_ref, lag_ref, lab_ref, lvg_ref, lvb_ref,
             spw_ref, spb_ref, cat_ref, a1_ref, a0_ref, sh_ref):
        i = pl.program_id(0)
        _glu_into(a0_ref, av_ref, ag_ref, hv_ref, hg_ref, i % tiles_per_seq == 0)
        acc = jnp.zeros((tm, c), F32) + cb_ref[...]
        for off, k, rows in _conv_taps(a0_ref, tm, HALO - (CONV_WIDTH - 1), sh_ref):
            acc = acc + cw_ref[k:k + 1, :] * rows
        a1_ref[...] = acc
        xh, _ = _ln_stats(acc)
        a2 = xh * lag_ref[...] + lab_ref[...]
        a3 = (a2 * _sigmoid(a2)).astype(BF16)
        half = c // 2
        cat_ref[0] = a3[:, :half]
        cat_ref[1] = a3[:, half:]
        tril = _tril_mask()
        for g in range(groups):
            sl = slice(g * LANES, (g + 1) * LANES)
            xh, _ = _ln_stats(v_ref[0][:, sl].astype(F32))
            vn = (xh * lvg_ref[:, sl] + lvb_ref[:, sl]).astype(BF16)
            w = jnp.where(tril, spw_ref[g], 0.0).astype(BF16)
            bias = spb_ref[:, g:g + 1]
            for ch in range(tm // CHUNK):
                rows = slice(ch * CHUNK, (ch + 1) * CHUNK)
                vs = _nn(w, vn[rows]) + bias
                bo = (u_ref[0][rows, sl].astype(F32) * vs).astype(BF16)
                cat_ref[2 + (g * LANES) // half, rows, (g * LANES) % half:(g * LANES) % half + LANES] = bo

    unit = lambda u: pl.BlockSpec((1, tm, c), lambda i: (u, i, 0))
    halo = lambda u: pl.BlockSpec((1, HALO, c), lambda i: (u, jnp.maximum(i * hb - 1, 0), 0))
    vec = pl.BlockSpec((1, c), lambda i: (0, 0))
    return pl.pallas_call(
        body, name="mix_fwd", grid=(t // tm,),
        in_specs=[unit(0), unit(1), unit(2), unit(3), halo(0), halo(1),
                  pl.BlockSpec((CONV_WIDTH, c), lambda i: (0, 0)), vec, vec, vec, vec, vec,
                  pl.BlockSpec((groups, CHUNK, CHUNK), lambda i: (0, 0, 0)),
                  pl.BlockSpec((CHUNK, groups), lambda i: (0, 0))],
        out_specs=[pl.BlockSpec((4, tm, c // 2), lambda i: (0, i, 0)), pl.BlockSpec((tm, c), lambda i: (i, 0))],
        out_shape=[jax.ShapeDtypeStruct((4, t, c // 2), BF16), jax.ShapeDtypeStruct((t, c), F32)],
        scratch_shapes=[pltpu.VMEM((HALO + tm, c), F32), pltpu.VMEM((8, HALO + tm, c), F32)],
        compiler_params=_cparams(("parallel",)),
    )(z, z, z, z, z, z, conv_w, conv_b, ln_a_g, ln_a_b, ln_v_g, ln_v_b, sp_w, sp_bt)


def mix_bwd_point(dcat, z, a1, ln_a_g, ln_a_b, ln_v_g, ln_v_b, sp_w, sp_wt, sp_bt, seq):
    _, t, c = z.shape
    tm = _tile(seq, 512)
    groups = c // LANES
    half = c // 2

    def body(dc_ref, u_ref, v_ref, a1_ref, lag_ref, lab_ref, lvg_ref, lvb_ref, spw_ref, spwt_ref, spb_ref,
             dz_ref, da1_ref, dcb_ref, dlag_ref, dlab_ref, dlvg_ref, dlvb_ref, dspw_ref, dspb_ref):
        i = pl.program_id(0)
        last = pl.num_programs(0) - 1

        @pl.when(i == 0)
        def _():
            for r in (dcb_ref, dlag_ref, dlab_ref, dlvg_ref, dlvb_ref, dspw_ref, dspb_ref):
                r[...] = jnp.zeros_like(r)

        da3 = jnp.concatenate([dc_ref[0], dc_ref[1]], axis=-1)
        xh, rstd = _ln_stats(a1_ref[...])
        a2 = xh * lag_ref[...] + lab_ref[...]
        s = _sigmoid(a2)
        da2 = da3 * (s * (1.0 + a2 * (1.0 - s)))
        dlag_ref[...] += jnp.sum(da2 * xh, axis=0, keepdims=True)
        dlab_ref[...] += jnp.sum(da2, axis=0, keepdims=True)
        dxh = da2 * lag_ref[...]
        da1 = rstd * (dxh - jnp.mean(dxh, axis=-1, keepdims=True) - xh * jnp.mean(dxh * xh, axis=-1, keepdims=True))
        da1_ref[...] = da1
        dcb_ref[...] += jnp.sum(da1, axis=0, keepdims=True)

        tril = _tril_mask()
        for g in range(groups):
            sl = slice(g * LANES, (g + 1) * LANES)
            xh, rstd = _ln_stats(v_ref[0][:, sl].astype(F32))
            lg = lvg_ref[:, sl]
            vnb = (xh * lg + lvb_ref[:, sl]).astype(BF16)
            w = jnp.where(tril, spw_ref[g], 0.0).astype(BF16)
            wt = jnp.where(tril.T, spwt_ref[g], 0.0).astype(BF16)
            bias = spb_ref[:, g:g + 1]
            dbo_all = dc_ref[2 + (g * LANES) // half][:, (g * LANES) % half:(g * LANES) % half + LANES]
            dvn_parts = []
            dw_acc = jnp.zeros((CHUNK, CHUNK), F32)
            db_acc = jnp.zeros((CHUNK, LANES), F32)
            for ch in range(tm // CHUNK):
                rows = slice(ch * CHUNK, (ch + 1) * CHUNK)
                vs = _nn(w, vnb[rows]) + bias
                dbo = dbo_all[rows]
                uv = u_ref[0][rows, sl].astype(F32)
                dz_ref[0, rows, sl] = (dbo * vs).astype(BF16)
                dvs = dbo * uv
                dvsb = dvs.astype(BF16)
                dvn_parts.append(_nn(wt, dvsb))
                dw_acc = dw_acc + _nt(dvsb, vnb[rows])
                db_acc = db_acc + dvs
            dvn = jnp.concatenate(dvn_parts, axis=0)
            dspw_ref[g] += jnp.where(tril, dw_acc, 0.0)
            dspb_ref[g] += db_acc
            dlvg_ref[:, sl] += jnp.sum(dvn * xh, axis=0, keepdims=True)
            dlvb_ref[:, sl] += jnp.sum(dvn, axis=0, keepdims=True)
            dxh = dvn * lg
            dv = rstd * (dxh - jnp.mean(dxh, axis=-1, keepdims=True) - xh * jnp.mean(dxh * xh, axis=-1, keepdims=True))
            dz_ref[1, :, sl] = dv.astype(BF16)

        @pl.when(i == last)
        def _():
            for g in range(groups):
                dspb_ref[g] = jnp.zeros((CHUNK, LANES), F32) + jnp.sum(dspb_ref[g], axis=-1, keepdims=True)

    unit = lambda u: pl.BlockSpec((1, tm, c), lambda i: (u, i, 0))
    vec = pl.BlockSpec((1, c), lambda i: (0, 0))
    sq = pl.BlockSpec((groups, CHUNK, CHUNK), lambda i: (0, 0, 0))
    vshape = jax.ShapeDtypeStruct((1, c), F32)
    sshape = jax.ShapeDtypeStruct((groups, CHUNK, CHUNK), F32)
    return pl.pallas_call(
        body, name="mix_bwd_point", grid=(t // tm,),
        in_specs=[pl.BlockSpec((4, tm, half), lambda i: (0, i, 0)), unit(2), unit(3),
                  pl.BlockSpec((tm, c), lambda i: (i, 0)), vec, vec, vec, vec, sq, sq,
                  pl.BlockSpec((CHUNK, groups), lambda i: (0, 0))],
        out_specs=[pl.BlockSpec((2, tm, c), lambda i: (1, i, 0)), pl.BlockSpec((tm, c), lambda i: (i, 0)),
                   vec, vec, vec, vec, vec, sq, sq],
        out_shape=[jax.ShapeDtypeStruct((4, t, c), BF16), jax.ShapeDtypeStruct((t, c), F32),
                   vshape, vshape, vshape, vshape, vshape, sshape, sshape],
        compiler_params=_cparams(("arbitrary",)),
    )(dcat, z, z, a1, ln_a_g, ln_a_b, ln_v_g, ln_v_b, sp_w, sp_wt, sp_bt)


def mix_bwd_conv(dz, da1, z, conv_w, seq):
    _, t, c = z.shape
    tm = _tile(seq, 512)
    tiles_per_seq = seq // tm
    hb = tm // HALO
    n_halo_blocks = t // HALO

    rc = _tile(tm, CONV_ROWS)

    def body(dz_in_ref, d_ref, dh_ref, av_ref, ag_ref, cw_ref, dz_ref, dcw_ref, d1_ref, sh_ref, part_ref):
        del dz_in_ref
        i = pl.program_id(0)

        @pl.when(i == 0)
        def _():
            part_ref[...] = jnp.zeros_like(part_ref)

        d1_ref[0:tm, :] = d_ref[...]
        d1_ref[tm:, :] = jnp.where((i + 1) % tiles_per_seq == 0, 0.0, dh_ref[...])
        taps = []
        for b in range(8):
            offs = [o for o in range(CONV_WIDTH) if o % 8 == b]
            n_rows = max(offs) - b + tm
            sh_ref[b, 0:n_rows, :] = d1_ref[pl.ds(b, n_rows), :]
            taps += [(b, o - b, CONV_WIDTH - 1 - o) for o in offs]

        def chunk(ci, carry):
            r0 = pl.multiple_of(ci * rc, rc)
            av = av_ref[0, pl.ds(r0, rc), :].astype(F32)
            s = _sigmoid(ag_ref[0, pl.ds(r0, rc), :].astype(F32))
            a0 = av * s
            da0 = jnp.zeros((rc, c), F32)
            for b, ro, k in taps:
                rows = sh_ref[b, pl.ds(r0 + ro, rc), :]
                da0 = da0 + cw_ref[k:k + 1, :] * rows
                prod = a0 * rows
                part_ref[k] += functools.reduce(lambda p, q: p + q, [prod[8 * r:8 * r + 8] for r in range(rc // 8)])
            dz_ref[0, pl.ds(r0, rc), :] = (da0 * s).astype(BF16)
            dz_ref[1, pl.ds(r0, rc), :] = (da0 * av * s * (1.0 - s)).astype(BF16)
            return carry

        lax.fori_loop(0, tm // rc, chunk, 0)

        @pl.when(i == pl.num_programs(0) - 1)
        def _():
            dcw_ref[...] = jnp.sum(part_ref[...], axis=1)

    unit = lambda u: pl.BlockSpec((1, tm, c), lambda i: (u, i, 0))
    return pl.pallas_call(
        body, name="mix_bwd_conv", grid=(t // tm,),
        in_specs=[pl.BlockSpec(memory_space=pl.ANY), pl.BlockSpec((tm, c), lambda i: (i, 0)),
                  pl.BlockSpec((HALO, c), lambda i: (jnp.minimum((i + 1) * hb, n_halo_blocks - 1), 0)),
                  unit(0), unit(1), pl.BlockSpec((CONV_WIDTH, c), lambda i: (0, 0))],
        out_specs=[pl.BlockSpec((2, tm, c), lambda i: (0, i, 0)), pl.BlockSpec((CONV_WIDTH, c), lambda i: (0, 0))],
        out_shape=[jax.ShapeDtypeStruct(dz.shape, BF16), jax.ShapeDtypeStruct((CONV_WIDTH, c), F32)],
        scratch_shapes=[pltpu.VMEM((tm + HALO, c), F32), pltpu.VMEM((8, tm + HALO, c), F32),
                        pltpu.VMEM((CONV_WIDTH, 8, c), F32)],
        input_output_aliases={0: 0},
        compiler_params=_cparams(("arbitrary",)),
    )(dz, da1, da1, z, z, conv_w)


CHIP_FLIPS = ((1, 0), (0, 1), (1, 1))
ANY = pl.BlockSpec(memory_space=pl.ANY)


def _place():
    return lax.axis_index("x"), lax.axis_index("y"), lax.axis_index("c")


def _flip(v, f):
    return 1 - v if f else v


def place_shard(w, layer, chip, dtype, name):
    _, r, cc = w.shape
    rb = _tile(r, 512)

    def body(chip_ref, w_ref, o_ref):
        del chip_ref
        o_ref[0] = w_ref[0].astype(dtype)

    return pl.pallas_call(
        body, name=name,
        grid_spec=pltpu.PrefetchScalarGridSpec(
            num_scalar_prefetch=1, grid=(r // rb,),
            in_specs=[pl.BlockSpec((1, rb, cc), lambda i, chip_ref: (layer, i, 0))],
            out_specs=pl.BlockSpec((1, rb, cc), lambda i, chip_ref: (chip_ref[0], i, 0))),
        out_shape=jax.ShapeDtypeStruct((N_CHIPS, r, cc), dtype),
        compiler_params=_cparams(("parallel",)),
    )(chip, w)


class Carry:
    def __init__(self, arrays, out_shapes, aliased, sem_shapes, start, finish):
        self.arrays, self.out_shapes, self.aliased, self.sem_shapes = list(arrays), list(out_shapes), aliased, list(sem_shapes)
        self.start, self.finish = start, finish


def _call(body, *, name, grid, in_specs, out_specs, out_shape, args, sem, scratch_shapes=(), carry=None):
    if carry is None:
        res = pl.pallas_call(body, name=name, grid=grid, in_specs=in_specs, out_specs=out_specs, out_shape=out_shape,
                             scratch_shapes=list(scratch_shapes), compiler_params=_cparams(sem))(*args)
        return list(res), []
    n_in, n_out, n_scr, nc = len(args), len(out_shape), len(scratch_shapes), len(carry.arrays)

    def full_body(*refs):
        ins, refs = refs[:n_in], refs[n_in:]
        c_ins, refs = refs[:nc], refs[nc:]
        outs, refs = refs[:n_out], refs[n_out:]
        c_outs, refs = refs[:nc], refs[nc:]
        scr, sems = refs[:n_scr], refs[n_scr:]
        first = functools.reduce(lambda a, b: a & b, [pl.program_id(d) == 0 for d in range(len(grid))])
        last = functools.reduce(lambda a, b: a & b, [pl.program_id(d) == grid[d] - 1 for d in range(len(grid))])

        @pl.when(first)
        def _():
            carry.start(c_ins, c_outs, sems)

        body(*ins, *outs, *scr)

        @pl.when(last)
        def _():
            carry.finish(c_ins, c_outs, sems)

    res = pl.pallas_call(
        full_body, name=name, grid=grid, in_specs=list(in_specs) + [ANY] * nc, out_specs=list(out_specs) + [ANY] * nc,
        out_shape=list(out_shape) + carry.out_shapes, scratch_shapes=list(scratch_shapes) + carry.sem_shapes,
        input_output_aliases={n_in + i: n_out + i for i in range(nc)} if carry.aliased else {},
        compiler_params=pltpu.CompilerParams(dimension_semantics=("arbitrary",) * len(grid), vmem_limit_bytes=VMEM_LIMIT,
                                             has_side_effects=True),
    )(*args, *carry.arrays)
    return list(res[:n_out]), list(res[n_out:])


def _gather_ops(shapes, whole):
    n = len(shapes)

    def rows(a, c):
        hr = shapes[a][1] // 2
        return pl.ds(pl.multiple_of(c * hr, 16), hr)

    def start(ins, outs, sems):
        ici_send, ici_recv = sems[0], sems[1]
        x, y, c = _place()
        k = 2 * x + y
        for a in range(n):
            for o, (fx, fy) in enumerate(CHIP_FLIPS):
                src = ins[a].at[k] if whole[a] else ins[a].at[k, rows(a, c)]
                dst = outs[a].at[k] if whole[a] else outs[a].at[k, rows(a, c)]
                pltpu.make_async_remote_copy(
                    src_ref=src, dst_ref=dst, send_sem=ici_send.at[3 * a + o], recv_sem=ici_recv.at[3 * a + o],
                    device_id=(_flip(x, fx), _flip(y, fy), c), device_id_type=MESH).start()

    def finish(ins, outs, sems):
        ici_send, ici_recv, d2d_send, d2d_recv = sems
        x, y, c = _place()
        k = 2 * x + y
        sibling = (x, y, 1 - c)

        def copy(ref, send, recv, a, o):
            return pltpu.make_async_remote_copy(src_ref=ref, dst_ref=ref, send_sem=send.at[3 * a + o],
                                                recv_sem=recv.at[3 * a + o], device_id=sibling, device_id_type=MESH)

        for a in range(n):
            for o, (fx, fy) in enumerate(CHIP_FLIPS):
                kk = 2 * _flip(x, fx) + _flip(y, fy)
                landed = outs[a].at[kk] if whole[a] else outs[a].at[kk, rows(a, c)]
                copy(landed, ici_send, ici_recv, a, o).wait_recv()
                if not whole[a]:
                    copy(landed, d2d_send, d2d_recv, a, o).start()
        for a in range(n):
            for o, (fx, fy) in enumerate(CHIP_FLIPS):
                kk = 2 * _flip(x, fx) + _flip(y, fy)
                mine = ins[a].at[k] if whole[a] else ins[a].at[k, rows(a, c)]
                copy(mine, ici_send, ici_recv, a, o).wait_send()
                if not whole[a]:
                    copy(outs[a].at[kk, rows(a, 1 - c)], d2d_send, d2d_recv, a, o).wait_recv()
                    copy(outs[a].at[kk, rows(a, c)], d2d_send, d2d_recv, a, o).wait_send()

    dma = pltpu.SemaphoreType.DMA
    return start, finish, [dma((3 * n,))] * 4


def gather_carry(bufs):
    start, finish, sems = _gather_ops([b.shape for b in bufs], [False] * len(bufs))
    return Carry(bufs, [jax.ShapeDtypeStruct(b.shape, b.dtype) for b in bufs], True, sems, start, finish)


def allgather_weights(shards, smalls):
    bufs = list(shards) + list(smalls)
    n = len(bufs)
    start, finish, sems = _gather_ops([b.shape for b in bufs], [False] * len(shards) + [True] * len(smalls))

    def body(*refs):
        start(refs[:n], refs[n:2 * n], refs[2 * n:])
        finish(refs[:n], refs[n:2 * n], refs[2 * n:])

    res = pl.pallas_call(
        body, name="allgather_weights", in_specs=[ANY] * n, out_specs=[ANY] * n,
        out_shape=[jax.ShapeDtypeStruct(b.shape, b.dtype) for b in bufs], scratch_shapes=sems,
        input_output_aliases={i: i for i in range(n)},
        compiler_params=pltpu.CompilerParams(has_side_effects=True),
    )(*bufs)
    return res[:len(shards)], res[len(shards):]


def rs_exchange(grads):
    n = len(grads)

    def body(*refs):
        ins, outs = refs[:n], refs[n:2 * n]
        send, recv = refs[2 * n:]
        x, y, c = _place()
        cps = []
        for a in range(n):
            cp = pltpu.make_async_remote_copy(
                src_ref=ins[a].at[:, 1 - c], dst_ref=outs[a], send_sem=send.at[a], recv_sem=recv.at[a],
                device_id=(x, y, 1 - c), device_id_type=MESH)
            cp.start()
            cps.append(cp)
        for cp in cps:
            cp.wait()

    dma = pltpu.SemaphoreType.DMA
    return pl.pallas_call(
        body, name="rs_exchange", in_specs=[ANY] * n, out_specs=[ANY] * n,
        out_shape=[jax.ShapeDtypeStruct((g.shape[0],) + g.shape[2:], g.dtype) for g in grads],
        scratch_shapes=[dma((n,)), dma((n,))],
        compiler_params=pltpu.CompilerParams(has_side_effects=True),
    )(*grads)


def rs_add(g, sib, core, out_dtype, name):
    nk, _, hr, cc = g.shape
    rb = _tile(hr, 256)

    def body(core_ref, g_ref, s_ref, o_ref):
        del core_ref
        o_ref[0] = (g_ref[0, 0] + s_ref[0]).astype(out_dtype)

    return pl.pallas_call(
        body, name=name,
        grid_spec=pltpu.PrefetchScalarGridSpec(
            num_scalar_prefetch=1, grid=(nk, hr // rb),
            in_specs=[pl.BlockSpec((1, 1, rb, cc), lambda k, i, core_ref: (k, core_ref[0], i, 0)),
                      pl.BlockSpec((1, rb, cc), lambda k, i, core_ref: (k, i, 0))],
            out_specs=pl.BlockSpec((1, rb, cc), lambda k, i, core_ref: (k, i, 0))),
        out_shape=jax.ShapeDtypeStruct((nk, hr, cc), out_dtype),
        compiler_params=_cparams(("parallel", "parallel")),
    )(core, g, sib)


def send_carry(parts):
    n = len(parts)

    def copies(ins, outs, sems):
        x, y, c = _place()
        for a in range(n):
            for o, (fx, fy) in enumerate(CHIP_FLIPS):
                kk = 2 * _flip(x, fx) + _flip(y, fy)
                yield pltpu.make_async_remote_copy(
                    src_ref=ins[a].at[kk], dst_ref=outs[a].at[o], send_sem=sems[0].at[3 * a + o],
                    recv_sem=sems[1].at[3 * a + o], device_id=(_flip(x, fx), _flip(y, fy), c), device_id_type=MESH)

    def start(ins, outs, sems):
        for cp in copies(ins, outs, sems):
            cp.start()

    def finish(ins, outs, sems):
        for cp in copies(ins, outs, sems):
            cp.wait()

    dma = pltpu.SemaphoreType.DMA
    return Carry(parts, [jax.ShapeDtypeStruct((3,) + p.shape[1:], p.dtype) for p in parts], False,
                 [dma((3 * n,)), dma((3 * n,))], start, finish)


def rs_sum(recv, part, where, full, layer, n_layers, name):
    _, hr, cc = recv.shape
    rb = _tile(hr, 256)

    def body(*refs):
        r_ref, p_ref, o_ref = refs[1], refs[2], refs[-1]
        o_ref[0, 0] = ((p_ref[0].astype(F32) + r_ref[0].astype(F32)) + r_ref[1].astype(F32)) + r_ref[2].astype(F32)

    in_specs = [pl.BlockSpec((3, rb, cc), lambda i, w_ref: (0, i, 0)),
                pl.BlockSpec((1, rb, cc), lambda i, w_ref: (w_ref[0], i, 0))]
    args = [where, recv, part]
    aliases = {}
    if full is not None:
        in_specs.append(ANY)
        args.append(full)
        aliases = {3: 0}
    return pl.pallas_call(
        body, name=name,
        grid_spec=pltpu.PrefetchScalarGridSpec(
            num_scalar_prefetch=1, grid=(hr // rb,), in_specs=in_specs,
            out_specs=pl.BlockSpec((1, 1, rb, cc), lambda i, w_ref: (layer, w_ref[1], i, 0))),
        out_shape=jax.ShapeDtypeStruct((n_layers, 2, hr, cc), F32),
        input_output_aliases=aliases,
        compiler_params=_cparams(("parallel",)),
    )(*args)


def rs_share(fulls):
    n = len(fulls)

    def body(*refs):
        ins, outs = refs[:n], refs[n:2 * n]
        send, recv = refs[2 * n:]
        x, y, c = _place()
        cps = []
        for a in range(n):
            cp = pltpu.make_async_remote_copy(
                src_ref=ins[a].at[:, c], dst_ref=outs[a].at[:, c], send_sem=send.at[a], recv_sem=recv.at[a],
                device_id=(x, y, 1 - c), device_id_type=MESH)
            cp.start()
            cps.append(cp)
        for a in range(n):
            got = outs[a].at[:, 1 - c]
            pltpu.make_async_remote_copy(
                src_ref=got, dst_ref=got, send_sem=send.at[a], recv_sem=recv.at[a],
                device_id=(x, y, 1 - c), device_id_type=MESH).wait_recv()
        for cp in cps:
            cp.wait_send()

    dma = pltpu.SemaphoreType.DMA
    return pl.pallas_call(
        body, name="rs_share", in_specs=[ANY] * n, out_specs=[ANY] * n,
        out_shape=[jax.ShapeDtypeStruct(f.shape, f.dtype) for f in fulls],
        scratch_shapes=[dma((n,)), dma((n,))],
        input_output_aliases={i: i for i in range(n)},
        compiler_params=pltpu.CompilerParams(has_side_effects=True),
    )(*fulls)


def allreduce_small(v):
    r, w = v.shape

    def body(v_ref, o_ref, buf, send, recv, loc):
        x, y, c = _place()
        me = 4 * x + 2 * y + c
        mine = pltpu.make_async_copy(v_ref, buf.at[me], loc)
        mine.start()
        cps = []
        for o in range(1, N_DEV):
            fx, fy, fc = (o >> 2) & 1, (o >> 1) & 1, o & 1
            cp = pltpu.make_async_remote_copy(
                src_ref=v_ref, dst_ref=buf.at[me], send_sem=send.at[o - 1], recv_sem=recv.at[o - 1],
                device_id=(_flip(x, fx), _flip(y, fy), _flip(c, fc)), device_id_type=MESH)
            cp.start()
            cps.append(cp)
        for o in range(1, N_DEV):
            fx, fy, fc = (o >> 2) & 1, (o >> 1) & 1, o & 1
            peer = 4 * _flip(x, fx) + 2 * _flip(y, fy) + _flip(c, fc)
            pltpu.make_async_remote_copy(
                src_ref=v_ref, dst_ref=buf.at[peer], send_sem=send.at[o - 1], recv_sem=recv.at[o - 1],
                device_id=(x, y, c), device_id_type=MESH).wait_recv()
        for cp in cps:
            cp.wait_send()
        mine.wait()
        acc = buf[0]
        for d in range(1, N_DEV):
            acc = acc + buf[d]
        o_ref[...] = acc

    dma = pltpu.SemaphoreType.DMA
    vm = pl.BlockSpec(memory_space=pltpu.VMEM)
    return pl.pallas_call(
        body, name="allreduce_small", in_specs=[vm], out_specs=vm,
        out_shape=jax.ShapeDtypeStruct((r, w), F32),
        scratch_shapes=[pltpu.VMEM((N_DEV, r, w), F32), dma((N_DEV - 1,)), dma((N_DEV - 1,)), dma],
        compiler_params=pltpu.CompilerParams(has_side_effects=True, vmem_limit_bytes=VMEM_LIMIT),
    )(v)


def adamw(w, g, m, v, name):
    r, cc = w.shape
    rb = _tile(r, 256)

    def body(w_ref, g_ref, m_ref, v_ref, d_ref, nm_ref, nv_ref):
        gv = g_ref[...]
        nm = ADAM_B1 * m_ref[...] + (1.0 - ADAM_B1) * gv
        nv = ADAM_B2 * v_ref[...] + (1.0 - ADAM_B2) * (gv * gv)
        m_hat = nm / (1.0 - ADAM_B1 ** ADAM_STEP)
        v_hat = nv / (1.0 - ADAM_B2 ** ADAM_STEP)
        d_ref[...] = -ADAM_LR * (m_hat / (jnp.sqrt(v_hat) + ADAM_EPS) + ADAM_WD * w_ref[...])
        nm_ref[...] = nm
        nv_ref[...] = nv

    blk = pl.BlockSpec((rb, cc), lambda i: (i, 0))
    shp = jax.ShapeDtypeStruct((r, cc), F32)
    return pl.pallas_call(
        body, name=name, grid=(r // rb,), in_specs=[blk] * 4, out_specs=[blk] * 3, out_shape=[shp] * 3,
        compiler_params=_cparams(("parallel",)),
    )(w, g, m, v)


WEIGHTS = ['g_ffn1', 'w_ffn1_gate', 'w_ffn1_up', 'w_ffn1_down', 'g_mix', 'w_in_ab', 'conv_w', 'conv_b', 'ln_a_g',
           'ln_a_b', 'ln_v_g', 'ln_v_b', 'sp_w', 'sp_b', 'w_out_ab', 'w_qkv', 'w_o', 'g_ffn2', 'w_ffn2_gate',
           'w_ffn2_up', 'w_ffn2_down', 'g_final']
BIG = ['w_ffn1_gate', 'w_ffn1_up', 'w_ffn1_down', 'w_in_ab', 'w_out_ab', 'w_qkv', 'w_o', 'w_ffn2_gate', 'w_ffn2_up',
       'w_ffn2_down']
SMALL = ['g_ffn1', 'g_mix', 'g_ffn2', 'g_final', 'conv_b', 'ln_a_g', 'ln_a_b', 'ln_v_g', 'ln_v_b', 'sp_b', 'sp_w']


CARRY_WEIGHTS = {"ffn_gateup": 9.2e6, "ffn_down": 6.1e6, "mm_in": 5.9e6, "mm_out": 3.3e6}


def _use_order(depth):
    order = []
    for layer in range(depth):
        order += [('w_ffn1_gate', layer), ('w_ffn1_up', layer), ('w_ffn1_down', layer)]
        order += [('w_in_ab', layer // 2), ('w_out_ab', layer // 2)] if layer % 2 == 0 else [('w_qkv', layer // 2), ('w_o', layer // 2)]
        order += [('w_ffn2_gate', layer), ('w_ffn2_up', layer), ('w_ffn2_down', layer)]
    return order


def _rows(a):
    return a.reshape(-1, LANES)


def _pack(parts):
    v = jnp.concatenate([_rows(p) for p in parts], axis=0)
    pad = (-v.shape[0]) % 8
    return jnp.pad(v, ((0, pad), (0, 0)))


def _unpack(v, shapes):
    out, r = [], 0
    for s in shapes:
        n = 1
        for d in s:
            n *= d
        n //= LANES
        out.append(v[r:r + n].reshape(s))
        r += n
    return out


def kernel(x, g_ffn1, w_ffn1_gate, w_ffn1_up, w_ffn1_down, g_mix, w_in_ab, conv_w, conv_b, ln_a_g, ln_a_b, ln_v_g, ln_v_b, sp_w, sp_b, w_out_ab, w_qkv, w_o, g_ffn2, w_ffn2_gate, w_ffn2_up, w_ffn2_down, g_final, loss_target, m_g_ffn1, m_w_ffn1_gate, m_w_ffn1_up, m_w_ffn1_down, m_g_mix, m_w_in_ab, m_conv_w, m_conv_b, m_ln_a_g, m_ln_a_b, m_ln_v_g, m_ln_v_b, m_sp_w, m_sp_b, m_w_out_ab, m_w_qkv, m_w_o, m_g_ffn2, m_w_ffn2_gate, m_w_ffn2_up, m_w_ffn2_down, m_g_final, v_g_ffn1, v_w_ffn1_gate, v_w_ffn1_up, v_w_ffn1_down, v_g_mix, v_w_in_ab, v_conv_w, v_conv_b, v_ln_a_g, v_ln_a_b, v_ln_v_g, v_ln_v_b, v_sp_w, v_sp_b, v_w_out_ab, v_w_qkv, v_w_o, v_g_ffn2, v_w_ffn2_gate, v_w_ffn2_up, v_w_ffn2_down, v_g_final):
    p = dict(locals())
    n_seq, seq, d = x.shape
    t = n_seq * seq
    depth = g_ffn1.shape[0]
    core = lax.axis_index("c")
    chip = 2 * lax.axis_index("x") + lax.axis_index("y")
    xf = x.reshape(t, d)
    target = loss_target.reshape(t, d)

    items = []
    for name in BIG:
        for layer in range(p[name].shape[0]):
            items.append((name, layer))
    chip1 = chip.reshape(1).astype(jnp.int32)
    placed = {it: place_shard(p[it[0]], it[1], chip1, BF16, "place_shard") for it in items}
    first = [('w_ffn1_gate', 0), ('w_ffn1_up', 0)]
    gathered, (conv_w4,) = allgather_weights([placed[it] for it in first],
                                             [place_shard(conv_w, 0, chip1, F32, "place_conv_w")])
    wt = dict(zip(first, gathered))
    waiting = [it for it in _use_order(depth) if it not in wt]

    def riders(name):
        room, take = CARRY_WEIGHTS[name], []
        for it in list(waiting):
            if placed[it].size <= room:
                room -= placed[it].size
                take.append(it)
                waiting.remove(it)
        return (take, gather_carry([placed[it] for it in take])) if take else (take, None)

    def landed(take, carried):
        wt.update(zip(take, carried))

    def weight(it):
        if it not in wt:
            waiting.remove(it)
            (wt[it],), _ = allgather_weights([placed[it]], [])
        return wt[it]

    c_mix = conv_w4.shape[2] * N_CHIPS
    conv_full = jnp.transpose(conv_w4, (1, 0, 2)).reshape(CONV_WIDTH, c_mix)
    vec = lambda a: a.reshape(1, -1)
    sp_bt = sp_b[0].T
    sp_wt = jnp.transpose(sp_w[0], (0, 2, 1))
    d_ff = w_ffn1_gate.shape[2]
    n_in = w_in_ab.shape[2]
    n_qkv = w_qkv.shape[2] // 3

    saved = []
    xc = xf
    h = rmsnorm_fwd(xc, vec(g_ffn1[0]), "norm_first")
    for layer in range(depth):
        s = {}
        for half, (gn, wn) in enumerate((('g_ffn1', 'w_ffn1'), ('g_ffn2', 'w_ffn2'))):
            if half == 1:
                s['x_mix'], s['h_mix'] = xc, h
                if layer % 2 == 0:
                    w_in = weight(('w_in_ab', layer // 2))
                    take, carry = riders("mm_in")
                    (z,), got = colmm(h, [w_in], n_in, BF16, "mm_in", carry)
                    landed(take, got)
                    cat, a1 = mix_fwd(z, conv_full, conv_b, ln_a_g, ln_a_b, vec(ln_v_g), vec(ln_v_b), sp_w[0], sp_bt, seq)
                    s.update(z=z, cat=cat, a1=a1)
                    w_out = weight(('w_out_ab', layer // 2))
                    take, carry = riders("mm_out")
                    (xc, h), got = rowmm(cat, w_out, xc, 1.0, "mm_out", carry, vec(g_ffn2[layer]))
                    landed(take, got)
                else:
                    (qkv,), _ = colmm(h, [weight(('w_qkv', layer // 2))], n_qkv, BF16, "mm_qkv")
                    o, tot, cnt = attn_fwd(qkv, n_seq, seq)
                    s.update(qkv=qkv, o=o, tot=tot, cnt=cnt)
                    (xc, h), _ = rowmm(o, weight(('w_o', layer // 2)), xc, 1.0, "mm_o", None, vec(g_ffn2[layer]))
            s['x' + wn] = xc
            w_gate, w_up = weight((wn + '_gate', layer)), weight((wn + '_up', layer))
            take, carry = riders("ffn_gateup")
            (silu, udsilu, act), got = colmm(h, [w_gate, w_up], d_ff, BF16, "ffn_gateup", carry, swiglu=True)
            landed(take, got)
            s.update({'h' + wn: h, 'swiglu' + wn: (silu, udsilu), 'act' + wn: act})
            w_down = weight((wn + '_down', layer))
            take, carry = riders("ffn_down")
            following = g_mix[layer] if half == 0 else (g_ffn1[layer + 1] if layer + 1 < depth else None)
            (xc, h), got = rowmm(act, w_down, xc, 0.5, "ffn_down", carry, None if following is None else vec(following))
            landed(take, got)
        saved.append(s)

    loss8, dx, dxb, dg_final = loss_head(xc, vec(g_final), target)
    loss = lax.psum(loss8[0, 0], ("x", "y", "c"))

    gw = {}
    gs = {}
    core1 = core.reshape(1).astype(jnp.int32)
    ready = []
    part, recv = {}, {}

    def leaving():
        its = list(ready)
        ready.clear()
        g4 = [gw[it].reshape(N_CHIPS, 2, gw[it].shape[1] // 2, gw[it].shape[2]) for it in its]
        sums = [rs_add(g, sb, core1, REDUCE_DTYPE, "rs_add") for g, sb in zip(g4, rs_exchange(g4))]
        part.update(zip(its, sums))
        return its, send_carry(sums)

    for layer in reversed(range(depth)):
        s = saved[layer]
        for half, (gn, wn) in reversed(list(enumerate((('g_ffn1', 'w_ffn1'), ('g_ffn2', 'w_ffn2'))))):
            wd = wt[(wn + '_down', layer)]
            dgate, dup = rowmm_t(dxb, wd, 0.5, BF16, "ffn_bwd_act", swiglu=s['swiglu' + wn])
            gw[(wn + '_down', layer)] = dw_row(s['act' + wn], dxb, 0.5, "ffn_dw_down")
            gw[(wn + '_gate', layer)], gw[(wn + '_up', layer)] = dw_col(s['h' + wn], [dgate, dup], N_CHIPS, d_ff, "ffn_dw_gateup")
            ready.extend([(wn + '_down', layer), (wn + '_gate', layer), (wn + '_up', layer)])
            its, carry = leaving()
            (dx, dxb, dg), got = colmm_t([dgate, dup], [wt[(wn + '_gate', layer)], wt[(wn + '_up', layer)]], d_ff,
                                         s['x' + wn], vec(p[gn][layer]), dx, "ffn_bwd_in", carry)
            recv.update(zip(its, got))
            gs[(gn, layer)] = dg
            if half == 1:
                if layer % 2 == 0:
                    i = layer // 2
                    w_out = wt[('w_out_ab', i)]
                    dcat = rowmm_t(dxb, w_out, 1.0, F32, "mm_out_t")
                    gw[('w_out_ab', i)] = dw_row(s['cat'], dxb, 1.0, "dw_out")
                    dz, da1, dcb, dlag, dlab, dlvg, dlvb, dspw, dspb = mix_bwd_point(
                        dcat, s['z'], s['a1'], ln_a_g, ln_a_b, vec(ln_v_g), vec(ln_v_b), sp_w[0], sp_wt, sp_bt, seq)
                    dz, dcw = mix_bwd_conv(dz, da1, s['z'], conv_full, seq)
                    gs.update({('conv_b', i): dcb, ('ln_a_g', i): dlag, ('ln_a_b', i): dlab, ('ln_v_g', i): dlvg,
                               ('ln_v_b', i): dlvb, ('sp_w', i): dspw, ('sp_b', i): dspb[:, :, 0], ('conv_w', i): dcw})
                    (gw[('w_in_ab', i)],) = dw_col(s['h_mix'], [dz], N_CHIPS, n_in, "dw_in")
                    ready.extend([('w_out_ab', i), ('w_in_ab', i)])
                    its, carry = leaving()
                    (dx, dxb, dg), got = colmm_t([dz], [wt[('w_in_ab', i)]], n_in, s['x_mix'], vec(g_mix[layer]), dx,
                                                 "mm_in_t", carry)
                    recv.update(zip(its, got))
                else:
                    i = layer // 2
                    w_o4 = wt[('w_o', i)]
                    do = rowmm_t(dxb, w_o4, 1.0, BF16, "mm_o_t")
                    gw[('w_o', i)] = dw_row(s['o'], dxb, 1.0, "dw_o")
                    dq, dk, dv = attn_bwd(s['qkv'], do, s['tot'], s['cnt'], n_seq, seq)
                    dqkv = jnp.concatenate([dq, dk, dv], axis=0)
                    (gw[('w_qkv', i)],) = dw_col(s['h_mix'], [dqkv], N_CHIPS, n_qkv, "dw_qkv")
                    ready.extend([('w_o', i), ('w_qkv', i)])
                    its, carry = leaving()
                    (dx, dxb, dg), got = colmm_t([dqkv], [wt[('w_qkv', i)]], n_qkv, s['x_mix'], vec(g_mix[layer]), dx,
                                                 "mm_qkv_t", carry)
                    recv.update(zip(its, got))
                gs[('g_mix', layer)] = dg
    grad_x = dx.reshape(x.shape)

    assert not ready and set(recv) == set(items)
    where = jnp.stack([chip, core]).astype(jnp.int32)
    fulls = []
    for name in BIG:
        full = None
        n_layers = p[name].shape[0]
        for layer in range(n_layers):
            full = rs_sum(recv[(name, layer)], part[(name, layer)], where, full, layer, n_layers, "rs_sum")
        fulls.append(full)
    shared = rs_share(fulls)
    grads = {name: sh.reshape(p[name].shape) for name, sh in zip(BIG, shared)}

    stack = lambda name: jnp.concatenate([gs[(name, layer)].reshape((1,) + p[name].shape[1:]) for layer in range(p[name].shape[0])], axis=0)
    small_g = [stack(name) if name != 'g_final' else dg_final.reshape(p[name].shape) for name in SMALL]
    packed = _pack(small_g + [gs[('conv_w', 0)]])
    red = allreduce_small(packed)
    outs = _unpack(red, [p[name].shape for name in SMALL] + [(CONV_WIDTH, c_mix)])
    for name, g in zip(SMALL, outs[:-1]):
        grads[name] = g
    conv_g = outs[-1].reshape(CONV_WIDTH, N_CHIPS, c_mix // N_CHIPS)
    grads['conv_w'] = lax.dynamic_index_in_dim(conv_g, chip, axis=1, keepdims=False).reshape(conv_w.shape)

    delta, new_m, new_v = {}, {}, {}
    for name in BIG:
        shp = p[name].shape
        two = lambda a: a.reshape(shp[0] * shp[1], shp[2])
        dl, nm, nv = adamw(two(p[name]), two(grads[name]), two(p['m_' + name]), two(p['v_' + name]), "adamw")
        delta[name], new_m[name], new_v[name] = dl.reshape(shp), nm.reshape(shp), nv.reshape(shp)
    small_names = SMALL + ['conv_w']
    pk = lambda pre: _pack([p[pre + name] for name in small_names])
    dl, nm, nv = adamw(pk(''), _pack([grads[name] for name in small_names]), pk('m_'), pk('v_'), "adamw_small")
    shapes = [p[name].shape for name in small_names]
    for dst, val in ((delta, dl), (new_m, nm), (new_v, nv)):
        for name, a in zip(small_names, _unpack(val, shapes)):
            dst[name] = a

    return (loss, grad_x, *[grads[n] for n in WEIGHTS], *[delta[n] for n in WEIGHTS],
            *[new_m[n] for n in WEIGHTS], *[new_v[n] for n in WEIGHTS])
```

```python
import functools

import jax
import jax.numpy as jnp
from jax import lax
from jax.experimental import pallas as pl
from jax.experimental.pallas import tpu as pltpu

F32 = jnp.float32
BF16 = jnp.bfloat16
EPS = 1e-6
HEAD_DIM = 64
CONV_WIDTH = 31
CHUNK = 128
KBLK = 128
ATT_BLOCK = 256
ATT_LANES = 256
DW_TOKENS = 2048
CONV_ROWS = 64
MASKED = -1e30
STICK_GONE = -110.0
LANES = 128
HALO = 32
ADAM_LR, ADAM_B1, ADAM_B2, ADAM_EPS, ADAM_WD, ADAM_STEP = 0.001, 0.9, 0.999, 1e-08, 0.01, 10
VMEM_LIMIT = 56 * 1024 * 1024
MESH = pl.DeviceIdType.MESH
N_CHIPS = 4
N_DEV = 8
REDUCE_DTYPE = BF16


def _cparams(sem):
    return pltpu.CompilerParams(dimension_semantics=sem, vmem_limit_bytes=VMEM_LIMIT)


def _nt(a, b):
    return lax.dot_general(a, b, (((1,), (1,)), ((), ())), preferred_element_type=F32)


def _tn(a, b):
    return lax.dot_general(a, b, (((0,), (0,)), ((), ())), preferred_element_type=F32)


def _nn(a, b):
    return jnp.dot(a, b, preferred_element_type=F32)


def _sigmoid(x):
    return 0.5 * jnp.tanh(0.5 * x) + 0.5


def _tile(t, want):
    if t <= want:
        return t
    for cand in range(want - want % 8, 7, -8):
        if t % cand == 0:
            return cand
    raise ValueError((t, want))


def rmsnorm_fwd(x, g, name):
    t, d = x.shape
    tm = _tile(t, 512)

    def body(x_ref, g_ref, h_ref):
        xv = x_ref[...]
        r = lax.rsqrt(jnp.mean(xv * xv, axis=-1, keepdims=True) + EPS)
        h_ref[...] = (xv * r * g_ref[...]).astype(BF16)

    return pl.pallas_call(
        body, name=name, grid=(t // tm,),
        in_specs=[pl.BlockSpec((tm, d), lambda i: (i, 0)), pl.BlockSpec((1, d), lambda i: (0, 0))],
        out_specs=pl.BlockSpec((tm, d), lambda i: (i, 0)),
        out_shape=jax.ShapeDtypeStruct((t, d), BF16),
        compiler_params=_cparams(("parallel",)),
    )(x, g)


def colmm(h, ws, nu, out_dtype, name, carry=None, swiglu=False):
    t, k = h.shape
    j, _, nj = ws[0].shape
    per = nj // nu
    units = j * per
    tm = _tile(t, 512)
    nw = len(ws)
    n_out = 3 if swiglu else nw

    def body(*refs):
        h_ref = refs[0]
        hv = h_ref[...]
        if swiglu:
            silu_ref, udsilu_ref, act_ref = refs[1 + nw:]
            gv = _nn(hv, refs[1][0])
            uv = _nn(hv, refs[2][0])
            s = _sigmoid(gv)
            silu = gv * s
            silu_ref[0] = silu.astype(out_dtype)
            udsilu_ref[0] = (uv * (s + silu * (1.0 - s))).astype(out_dtype)
            act_ref[0] = (silu * uv).astype(out_dtype)
            return
        for n in range(nw):
            res = _nn(hv, refs[1 + n][0]).astype(out_dtype)
            for u in range(per):
                refs[1 + nw + n][u] = res[:, u * nu:(u + 1) * nu]

    assert not swiglu or (nw == 2 and per == 1)
    w_spec = pl.BlockSpec((1, k, nj), lambda s, i: (s, 0, 0))
    o_spec = pl.BlockSpec((per, tm, nu), lambda s, i: (s, i, 0))
    return _call(
        body, name=name, grid=(j, t // tm),
        in_specs=[pl.BlockSpec((tm, k), lambda s, i: (i, 0))] + [w_spec] * nw,
        out_specs=[o_spec] * n_out,
        out_shape=[jax.ShapeDtypeStruct((units, t, nu), out_dtype)] * n_out,
        args=[h, *ws], sem=("parallel", "parallel"), carry=carry)


def rowmm(a, w, resid, scale, name, carry=None, norm_g=None):
    u_n, t, ku = a.shape
    n = w.shape[2]
    tm = _tile(t, 256)

    def body(a_ref, w_ref, r_ref, *rest):
        acc = jnp.zeros((tm, n), F32)
        for u in range(u_n):
            acc = acc + _nn(a_ref[u], w_ref[u])
        out = r_ref[...] + scale * acc
        if norm_g is None:
            (o_ref,) = rest
        else:
            g_ref, o_ref, h_ref = rest
            r = lax.rsqrt(jnp.mean(out * out, axis=-1, keepdims=True) + EPS)
            h_ref[...] = (out * r * g_ref[...]).astype(BF16)
        o_ref[...] = out

    row = pl.BlockSpec((tm, n), lambda i: (i, 0))
    normed = norm_g is not None
    outs, carried = _call(
        body, name=name, grid=(t // tm,),
        in_specs=[pl.BlockSpec((u_n, tm, ku), lambda i: (0, i, 0)), pl.BlockSpec((u_n, ku, n), lambda i: (0, 0, 0)),
                  row] + [pl.BlockSpec((1, n), lambda i: (0, 0))] * normed,
        out_specs=[row] + [row] * normed,
        out_shape=[jax.ShapeDtypeStruct((t, n), F32)] + [jax.ShapeDtypeStruct((t, n), BF16)] * normed,
        args=[a, w, resid] + [norm_g] * normed, sem=("parallel",), carry=carry)
    return (outs[0], outs[1] if normed else None), carried


def rowmm_t(dyb, w, scale, out_dtype, name, swiglu=None):
    t, n = dyb.shape
    u_n, ku, _ = w.shape
    tm = _tile(t, 512)

    if swiglu is None:
        def body(dy_ref, w_ref, o_ref):
            o_ref[0] = (scale * _nt(dy_ref[...], w_ref[0])).astype(out_dtype)

        return pl.pallas_call(
            body, name=name, grid=(u_n, t // tm),
            in_specs=[pl.BlockSpec((tm, n), lambda u, i: (i, 0)), pl.BlockSpec((1, ku, n), lambda u, i: (u, 0, 0))],
            out_specs=pl.BlockSpec((1, tm, ku), lambda u, i: (u, i, 0)),
            out_shape=jax.ShapeDtypeStruct((u_n, t, ku), out_dtype),
            compiler_params=_cparams(("parallel", "parallel")),
        )(dyb, w)

    def body(dy_ref, w_ref, silu_ref, udsilu_ref, dg_ref, du_ref):
        dy = dy_ref[...]
        for u in range(u_n):
            dact = scale * _nt(dy, w_ref[u])
            dg_ref[u] = (dact * udsilu_ref[u].astype(F32)).astype(BF16)
            du_ref[u] = (dact * silu_ref[u].astype(F32)).astype(BF16)

    blk = pl.BlockSpec((u_n, tm, ku), lambda i: (0, i, 0))
    return pl.pallas_call(
        body, name=name, grid=(t // tm,),
        in_specs=[pl.BlockSpec((tm, n), lambda i: (i, 0)), pl.BlockSpec((u_n, ku, n), lambda i: (0, 0, 0)), blk, blk],
        out_specs=[blk] * 2, out_shape=[jax.ShapeDtypeStruct((u_n, t, ku), BF16)] * 2,
        compiler_params=_cparams(("parallel",)),
    )(dyb, w, *swiglu)


def colmm_t(dzs, ws, nu, x, g, dy_in, name, carry=None):
    t, k = x.shape
    j, _, nj = ws[0].shape
    per = nj // nu
    units = j * per
    nw = len(ws)
    tm = _tile(t, 256)

    def body(*refs):
        dz_refs = refs[:nw]
        w_refs = refs[nw:2 * nw]
        x_ref, g_ref, dy_ref, dx_ref, dxb_ref, dg_ref = refs[2 * nw:]
        i = pl.program_id(0)
        dh = jnp.zeros((tm, k), F32)
        for n in range(nw):
            for u in range(units):
                wv = w_refs[n][u // per, :, (u % per) * nu:(u % per + 1) * nu]
                dh = dh + _nt(dz_refs[n][u], wv)
        xv = x_ref[...]
        gv = g_ref[...]
        r = lax.rsqrt(jnp.mean(xv * xv, axis=-1, keepdims=True) + EPS)
        uu = dh * gv
        dx = dy_ref[...] + r * uu - xv * (r * r * r * jnp.mean(uu * xv, axis=-1, keepdims=True))
        dx_ref[...] = dx
        dxb_ref[...] = dx.astype(BF16)
        part = jnp.sum(dh * (xv * r), axis=0, keepdims=True)

        @pl.when(i == 0)
        def _():
            dg_ref[...] = part

        @pl.when(i > 0)
        def _():
            dg_ref[...] += part

    dz_spec = pl.BlockSpec((units, tm, nu), lambda i: (0, i, 0))
    w_spec = pl.BlockSpec((j, k, nj), lambda i: (0, 0, 0))
    row = pl.BlockSpec((tm, k), lambda i: (i, 0))
    vec = pl.BlockSpec((1, k), lambda i: (0, 0))
    return _call(
        body, name=name, grid=(t // tm,),
        in_specs=[dz_spec] * nw + [w_spec] * nw + [row, vec, row],
        out_specs=[row, row, vec],
        out_shape=[jax.ShapeDtypeStruct((t, k), F32), jax.ShapeDtypeStruct((t, k), BF16),
                   jax.ShapeDtypeStruct((1, k), F32)],
        args=[*dzs, *ws, x, g, dy_in], sem=("arbitrary",), carry=carry)


def dw_col(h, dzs, j, nu, name):
    t, k = h.shape
    units = dzs[0].shape[0]
    per = units // j
    nw = len(dzs)
    tt = _tile(t, DW_TOKENS)

    def body(*refs):
        h_ref = refs[0]
        s = pl.program_id(1)
        hv = h_ref[...]
        outs, copies = refs[1 + nw:1 + 2 * nw], refs[1 + 2 * nw:]

        @pl.when(s == 0)
        def _():
            for o_ref in outs:
                o_ref[...] = jnp.zeros_like(o_ref)

        for n in range(nw):
            for u in range(per):
                outs[n][0, :, u * nu:(u + 1) * nu] += _tn(hv, refs[1 + n][u])

        @pl.when(s == pl.num_programs(1) - 1)
        def _():
            for o_ref, c_ref in zip(outs, copies):
                c_ref[...] = o_ref[...].astype(REDUCE_DTYPE)

    o_spec = pl.BlockSpec((1, k, per * nu), lambda u, s: (u, 0, 0))
    res = pl.pallas_call(
        body, name=name, grid=(j, t // tt),
        in_specs=[pl.BlockSpec((tt, k), lambda u, s: (s, 0))] + [pl.BlockSpec((per, tt, nu), lambda u, s: (u, s, 0))] * nw,
        out_specs=[o_spec] * (2 * nw),
        out_shape=[jax.ShapeDtypeStruct((j, k, per * nu), F32)] * nw
        + [jax.ShapeDtypeStruct((j, k, per * nu), REDUCE_DTYPE)] * nw,
        compiler_params=_cparams(("parallel", "arbitrary")),
    )(h, *dzs)
    return list(zip(res[:nw], res[nw:]))


def dw_row(a, dyb, scale, name):
    u_n, t, ku = a.shape
    n = dyb.shape[1]
    tt = _tile(t, DW_TOKENS)

    def body(a_ref, dy_ref, o_ref, c_ref):
        @pl.when(pl.program_id(1) == 0)
        def _():
            o_ref[...] = jnp.zeros_like(o_ref)

        o_ref[0] += scale * _tn(a_ref[0], dy_ref[...])

        @pl.when(pl.program_id(1) == pl.num_programs(1) - 1)
        def _():
            c_ref[...] = o_ref[...].astype(REDUCE_DTYPE)

    o_spec = pl.BlockSpec((1, ku, n), lambda u, s: (u, 0, 0))
    return tuple(pl.pallas_call(
        body, name=name, grid=(u_n, t // tt),
        in_specs=[pl.BlockSpec((1, tt, ku), lambda u, s: (u, s, 0)), pl.BlockSpec((tt, n), lambda u, s: (s, 0))],
        out_specs=[o_spec, o_spec],
        out_shape=[jax.ShapeDtypeStruct((u_n, ku, n), F32), jax.ShapeDtypeStruct((u_n, ku, n), REDUCE_DTYPE)],
        compiler_params=_cparams(("parallel", "arbitrary")),
    )(a, dyb))


def loss_head(x, g, target):
    t, d = x.shape
    tm = _tile(t, 256)

    def body(x_ref, g_ref, t_ref, loss_ref, dx_ref, dxb_ref, dg_ref):
        i = pl.program_id(0)
        xv = x_ref[...]
        gv = g_ref[...]
        r = lax.rsqrt(jnp.mean(xv * xv, axis=-1, keepdims=True) + EPS)
        xh = xv * r
        err = xh * gv - t_ref[...]
        dy = err * (1.0 / d)
        uu = dy * gv
        dx = r * uu - xv * (r * r * r * jnp.mean(uu * xv, axis=-1, keepdims=True))
        dx_ref[...] = dx
        dxb_ref[...] = dx.astype(BF16)
        dg_part = jnp.sum(dy * xh, axis=0, keepdims=True)
        row = jnp.sum(err * err, axis=-1, keepdims=True) * (0.5 / d)
        l_part = jnp.zeros((8, LANES), F32) + jnp.sum(row, axis=0, keepdims=True)

        @pl.when(i == 0)
        def _():
            dg_ref[...] = dg_part
            loss_ref[...] = l_part

        @pl.when(i > 0)
        def _():
            dg_ref[...] += dg_part
            loss_ref[...] += l_part

    row = pl.BlockSpec((tm, d), lambda i: (i, 0))
    vec = pl.BlockSpec((1, d), lambda i: (0, 0))
    return pl.pallas_call(
        body, name="loss_head", grid=(t // tm,),
        in_specs=[row, vec, row],
        out_specs=[pl.BlockSpec((8, LANES), lambda i: (0, 0)), row, row, vec],
        out_shape=[jax.ShapeDtypeStruct((8, LANES), F32), jax.ShapeDtypeStruct((t, d), F32),
                   jax.ShapeDtypeStruct((t, d), BF16), jax.ShapeDtypeStruct((1, d), F32)],
        compiler_params=_cparams(("arbitrary",)),
    )(x, g, target)


def _split(v):
    hi = v.astype(BF16)
    lo = (v - hi.astype(F32)).astype(BF16)
    return hi, lo


def _keysums(v, m_ext):
    hi, lo = _split(v)
    outs = []
    for j in range(v.shape[1] // KBLK):
        sl = slice(j * KBLK, (j + 1) * KBLK)
        cs = _nn(jnp.concatenate([hi[:, sl], lo[:, sl]], axis=1), m_ext)
        outs.append((cs[:, :KBLK], cs[:, KBLK:]))
    return outs


def _softplus_parts(z):
    sp = jnp.maximum(z, 0.0) + jnp.log(1.0 + jnp.exp(-jnp.abs(z)))
    return sp, z - sp


def _sum_matrices():
    r = lax.broadcasted_iota(jnp.int32, (2 * KBLK, 2 * KBLK), 0) % KBLK
    c = lax.broadcasted_iota(jnp.int32, (2 * KBLK, 2 * KBLK), 1)
    suffix = jnp.where((r > c) | (c >= KBLK), 1.0, 0.0).astype(BF16)
    prefix = jnp.where((r <= c) | (c >= KBLK), 1.0, 0.0).astype(BF16)
    return suffix, prefix


def _att_geometry(qkv, seq):
    upp = qkv.shape[0] // 3
    bq = min(ATT_BLOCK, seq)
    per_unit = (2 * LANES) // ATT_LANES
    return upp, bq, seq // bq, bq // KBLK, per_unit, upp * per_unit, ATT_LANES // HEAD_DIM


def _head_lanes(rows, heads):
    lane = lax.broadcasted_iota(jnp.int32, (rows, ATT_LANES), 1)
    return [(lane >= HEAD_DIM * h) & (lane < HEAD_DIM * (h + 1)) for h in range(heads)]


def attn_fwd(qkv, n_seq, seq):
    t = qkv.shape[1]
    upp, bq, nq, nsub, per_unit, groups, heads = _att_geometry(qkv, seq)
    suffix_m, _ = _sum_matrices()

    def body(q_ref, k_ref, v_ref, m_ref, o_ref, tot_ref, cnt_ref):
        qi = pl.program_id(2)
        step_id = (pl.program_id(0) * groups + pl.program_id(1)) * nq + qi
        in_head = _head_lanes(bq, heads)
        only = lambda v, h: jnp.where(in_head[h], v, jnp.zeros_like(v))
        q_all = q_ref[0] * jnp.asarray(HEAD_DIM ** -0.5, BF16)
        qs = [only(q_all, h) for h in range(heads)]
        m_ext = m_ref[...]
        row = lax.broadcasted_iota(jnp.int32, (bq, bq), 0)
        col = lax.broadcasted_iota(jnp.int32, (bq, bq), 1)
        diag_mask = col < row

        def block(kj, carry, mask):
            off = pl.multiple_of(kj * bq, bq)
            k_all = k_ref[0, pl.ds(off, bq), :]
            v_all = v_ref[0, pl.ds(off, bq), :]
            rems, acc = carry
            out = []
            for h in range(heads):
                rem = rems[h]
                z = _nt(qs[h], k_all)
                if mask is not None:
                    z = jnp.where(mask, z, MASKED)
                sp, ls = _softplus_parts(z)
                sums = _keysums(-sp, m_ext)
                parts = [None] * nsub
                for j in reversed(range(nsub)):
                    suf, total = sums[j]
                    parts[j] = jnp.exp(ls[:, j * KBLK:(j + 1) * KBLK] + suf + rem)
                    rem = rem + total
                a = jnp.concatenate(parts, axis=1)
                acc = acc + _nn(a.astype(BF16), only(v_all, h))
                out.append(rem)
            return tuple(out), acc

        def most_left(c):
            return functools.reduce(jnp.maximum, [jnp.max(r) for r in c[0]])

        def more(s):
            return (s[0] < qi) & (s[1] > STICK_GONE)

        def step(s):
            c = block(qi - 1 - s[0], s[2], None)
            return s[0] + 1, most_left(c), c

        zero = jnp.zeros((bq, LANES), F32)
        carry = block(qi, ((zero,) * heads, jnp.zeros((bq, ATT_LANES), F32)), diag_mask)
        n_left, _, (rems, acc) = lax.while_loop(more, step, (jnp.int32(0), most_left(carry), carry))
        o_ref[0] = acc.astype(BF16)
        first = lax.broadcasted_iota(jnp.int32, (bq, LANES), 1) < HEAD_DIM
        tot_ref[...] = jnp.concatenate([jnp.where(first, rems[h], rems[h + 1]) for h in range(0, heads, 2)], axis=1)
        cnt_ref[step_id] = n_left.astype(F32)

    qblk = lambda b, g, i: (g // per_unit, b * nq + i, g % per_unit)
    return pl.pallas_call(
        body, name="attn_fwd", grid=(n_seq, groups, nq),
        in_specs=[pl.BlockSpec((1, bq, ATT_LANES), qblk),
                  pl.BlockSpec((1, seq, ATT_LANES), lambda b, g, i: (upp + g // per_unit, b, g % per_unit)),
                  pl.BlockSpec((1, seq, ATT_LANES), lambda b, g, i: (2 * upp + g // per_unit, b, g % per_unit)),
                  pl.BlockSpec((2 * KBLK, 2 * KBLK), lambda b, g, i: (0, 0))],
        out_specs=[pl.BlockSpec((1, bq, ATT_LANES), qblk),
                   pl.BlockSpec((bq, ATT_LANES), lambda b, g, i: (b * nq + i, g)),
                   pl.BlockSpec(memory_space=pltpu.SMEM)],
        out_shape=[jax.ShapeDtypeStruct((upp, t, 2 * LANES), BF16), jax.ShapeDtypeStruct((t, upp * 2 * LANES), F32),
                   jax.ShapeDtypeStruct((n_seq * groups * nq,), F32)],
        compiler_params=_cparams(("arbitrary", "arbitrary", "arbitrary")),
    )(qkv, qkv, qkv, suffix_m)


def attn_bwd(qkv, do, tot, cnt, n_seq, seq):
    t = qkv.shape[1]
    upp, bq, nq, nsub, per_unit, groups, heads = _att_geometry(qkv, seq)
    _, prefix_m = _sum_matrices()
    scale = HEAD_DIM ** -0.5

    def body(q_ref, k_ref, v_ref, do_ref, tot_ref, m_ref, cnt_ref, dq_ref, dk_ref, dv_ref, dk_acc, dv_acc):
        qi = pl.program_id(2)
        step_id = (pl.program_id(0) * groups + pl.program_id(1)) * nq + qi
        n_left = jnp.clip(cnt_ref[step_id].astype(jnp.int32), 0, qi)
        in_head = _head_lanes(bq, heads)
        only = lambda v, h: jnp.where(in_head[h], v, jnp.zeros_like(v))
        q_all = q_ref[0] * jnp.asarray(scale, BF16)
        do_all = do_ref[0]
        qs = [only(q_all, h) for h in range(heads)]
        dos = [only(do_all, h) for h in range(heads)]
        first = lax.broadcasted_iota(jnp.int32, (bq, LANES), 1) < HEAD_DIM
        tots = []
        for h in range(0, heads, 2):
            both = tot_ref[:, h // 2 * LANES:(h // 2 + 1) * LANES]
            swapped = pltpu.roll(both, HEAD_DIM, 1)
            tots += [jnp.where(first, both, swapped), jnp.where(first, swapped, both)]
        m_ext = m_ref[...]
        row = lax.broadcasted_iota(jnp.int32, (bq, bq), 0)
        col = lax.broadcasted_iota(jnp.int32, (bq, bq), 1)
        diag_mask = col < row

        @pl.when(qi == 0)
        def _():
            dk_acc[...] = jnp.zeros_like(dk_acc)
            dv_acc[...] = jnp.zeros_like(dv_acc)

        def block(kj, carry, mask):
            off = pl.multiple_of(kj * bq, bq)
            k_all = k_ref[0, pl.ds(off, bq), :]
            v_all = v_ref[0, pl.ds(off, bq), :]
            pres, gpres, dq = carry
            dk_part = jnp.zeros((bq, ATT_LANES), F32)
            dv_part = jnp.zeros((bq, ATT_LANES), F32)
            pres_out, gpres_out = [], []
            for h in range(heads):
                pre, gpre = pres[h], gpres[h]
                z = _nt(qs[h], k_all)
                if mask is not None:
                    z = jnp.where(mask, z, MASKED)
                sp, ls = _softplus_parts(z)
                sums = _keysums(-sp, m_ext)
                parts = []
                for j in range(nsub):
                    pin, ptot = sums[j]
                    parts.append(jnp.exp(ls[:, j * KBLK:(j + 1) * KBLK] + (tots[h] - (pre + pin))))
                    pre = pre + ptot
                a = jnp.concatenate(parts, axis=1)
                g = a * _nt(dos[h], v_all)
                gsums = _keysums(g, m_ext)
                parts = []
                for j in range(nsub):
                    gin, gtot = gsums[j]
                    parts.append(gpre + gin)
                    gpre = gpre + gtot
                dz = g - jnp.exp(ls) * jnp.concatenate(parts, axis=1)
                dzb = dz.astype(BF16)
                dq = dq + _nn(dzb, only(k_all, h))
                dk_part = dk_part + _tn(dzb, qs[h])
                dv_part = dv_part + _tn(a.astype(BF16), dos[h])
                pres_out.append(pre)
                gpres_out.append(gpre)
            dk_acc[pl.ds(off, bq), :] += dk_part
            dv_acc[pl.ds(off, bq), :] += dv_part
            return tuple(pres_out), tuple(gpres_out), dq

        zero = jnp.zeros((bq, LANES), F32)
        carry = ((zero,) * heads, (zero,) * heads, jnp.zeros((bq, ATT_LANES), F32))
        carry = lax.fori_loop(qi - n_left, qi, lambda kj, c: block(kj, c, None), carry)
        carry = block(qi, carry, diag_mask)
        dq_ref[0] = (carry[2] * scale).astype(BF16)

        @pl.when(qi == nq - 1)
        def _():
            dk_ref[0] = dk_acc[...].astype(BF16)
            dv_ref[0] = dv_acc[...].astype(BF16)

    qblk = lambda b, g, i: (g // per_unit, b * nq + i, g % per_unit)
    kv_out = pl.BlockSpec((1, seq, ATT_LANES), lambda b, g, i: (g // per_unit, b, g % per_unit))
    shp = jax.ShapeDtypeStruct((upp, t, 2 * LANES), BF16)
    return pl.pallas_call(
        body, name="attn_bwd", grid=(n_seq, groups, nq),
        in_specs=[pl.BlockSpec((1, bq, ATT_LANES), qblk),
                  pl.BlockSpec((1, seq, ATT_LANES), lambda b, g, i: (upp + g // per_unit, b, g % per_unit)),
                  pl.BlockSpec((1, seq, ATT_LANES), lambda b, g, i: (2 * upp + g // per_unit, b, g % per_unit)),
                  pl.BlockSpec((1, bq, ATT_LANES), qblk),
                  pl.BlockSpec((bq, ATT_LANES), lambda b, g, i: (b * nq + i, g)),
                  pl.BlockSpec((2 * KBLK, 2 * KBLK), lambda b, g, i: (0, 0)),
                  pl.BlockSpec(memory_space=pltpu.SMEM)],
        out_specs=[pl.BlockSpec((1, bq, ATT_LANES), qblk), kv_out, kv_out],
        out_shape=[shp, shp, shp],
        scratch_shapes=[pltpu.VMEM((seq, ATT_LANES), F32), pltpu.VMEM((seq, ATT_LANES), F32)],
        compiler_params=_cparams(("parallel", "parallel", "arbitrary")),
    )(qkv, qkv, qkv, do, tot, prefix_m, cnt)


def _ln_stats(v):
    mu = jnp.mean(v, axis=-1, keepdims=True)
    vc = v - mu
    rstd = lax.rsqrt(jnp.mean(vc * vc, axis=-1, keepdims=True) + EPS)
    return vc * rstd, rstd


def _glu_into(a0_ref, av_ref, ag_ref, hv_ref, hg_ref, first):
    hv = hv_ref[0].astype(F32)
    hg = hg_ref[0].astype(F32)
    a0_ref[0:HALO, :] = jnp.where(first, 0.0, hv * _sigmoid(hg))
    av = av_ref[0].astype(F32)
    ag = ag_ref[0].astype(F32)
    a0_ref[HALO:, :] = av * _sigmoid(ag)


def _conv_taps(ref, tm, first, shifted_ref):
    for b in range(8):
        offs = [o for o in range(first, first + CONV_WIDTH) if o % 8 == b]
        if not offs:
            continue
        n_rows = max(offs) - b + tm
        shifted_ref[b, 0:n_rows, :] = ref[pl.ds(b, n_rows), :]
        for o in offs:
            yield o, o - first, shifted_ref[b, pl.ds(o - b, tm), :]


def _tril_mask():
    r = lax.broadcasted_iota(jnp.int32, (CHUNK, CHUNK), 0)
    c = lax.broadcasted_iota(jnp.int32, (CHUNK, CHUNK), 1)
    return c <= r


def mix_fwd(z, conv_w, conv_b, ln_a_g, ln_a_b, ln_v_g, ln_v_b, sp_w, sp_bt, seq):
    _, t, c = z.shape
    tm = _tile(seq, 512)
    tiles_per_seq = seq // tm
    groups = c // LANES
    hb = tm // HALO

    def body(av_ref, ag_ref, u_ref, v_ref, hv_ref, hg_ref, cw_ref, cb_ref, lag_ref, lab_ref, lvg_ref, lvb_ref,
             spw_ref, spb_ref, cat_ref, a1_ref, a0_ref, sh_ref):
        i = pl.program_id(0)
        _glu_into(a0_ref, av_ref, ag_ref, hv_ref, hg_ref, i % tiles_per_seq == 0)
        acc = jnp.zeros((tm, c), F32) + cb_ref[...]
        for off, k, rows in _conv_taps(a0_ref, tm, HALO - (CONV_WIDTH - 1), sh_ref):
            acc = acc + cw_ref[k:k + 1, :] * rows
        a1_ref[...] = acc
        xh, _ = _ln_stats(acc)
        a2 = xh * lag_ref[...] + lab_ref[...]
        a3 = (a2 * _sigmoid(a2)).astype(BF16)
        half = c // 2
        cat_ref[0] = a3[:, :half]
        cat_ref[1] = a3[:, half:]
        tril = _tril_mask()
        for g in range(groups):
            sl = slice(g * LANES, (g + 1) * LANES)
            xh, _ = _ln_stats(v_ref[0][:, sl].astype(F32))
            vn = (xh * lvg_ref[:, sl] + lvb_ref[:, sl]).astype(BF16)
            w = jnp.where(tril, spw_ref[g], 0.0).astype(BF16)
            bias = spb_ref[:, g:g + 1]
            for ch in range(tm // CHUNK):
                rows = slice(ch * CHUNK, (ch + 1) * CHUNK)
                vs = _nn(w, vn[rows]) + bias
                bo = (u_ref[0][rows, sl].astype(F32) * vs).astype(BF16)
                cat_ref[2 + (g * LANES) // half, rows, (g * LANES) % half:(g * LANES) % half + LANES] = bo

    unit = lambda u: pl.BlockSpec((1, tm, c), lambda i: (u, i, 0))
    halo = lambda u: pl.BlockSpec((1, HALO, c), lambda i: (u, jnp.maximum(i * hb - 1, 0), 0))
    vec = pl.BlockSpec((1, c), lambda i: (0, 0))
    return pl.pallas_call(
        body, name="mix_fwd", grid=(t // tm,),
        in_specs=[unit(0), unit(1), unit(2), unit(3), halo(0), halo(1),
                  pl.BlockSpec((CONV_WIDTH, c), lambda i: (0, 0)), vec, vec, vec, vec, vec,
                  pl.BlockSpec((groups, CHUNK, CHUNK), lambda i: (0, 0, 0)),
                  pl.BlockSpec((CHUNK, groups), lambda i: (0, 0))],
        out_specs=[pl.BlockSpec((4, tm, c // 2), lambda i: (0, i, 0)), pl.BlockSpec((tm, c), lambda i: (i, 0))],
        out_shape=[jax.ShapeDtypeStruct((4, t, c // 2), BF16), jax.ShapeDtypeStruct((t, c), F32)],
        scratch_shapes=[pltpu.VMEM((HALO + tm, c), F32), pltpu.VMEM((8, HALO + tm, c), F32)],
        compiler_params=_cparams(("parallel",)),
    )(z, z, z, z, z, z, conv_w, conv_b, ln_a_g, ln_a_b, ln_v_g, ln_v_b, sp_w, sp_bt)


def mix_bwd_point(dcat, z, a1, ln_a_g, ln_a_b, ln_v_g, ln_v_b, sp_w, sp_wt, sp_bt, seq):
    _, t, c = z.shape
    tm = _tile(seq, 512)
    groups = c // LANES
    half = c // 2

    def body(dc_ref, u_ref, v_ref, a1_ref, lag_ref, lab_ref, lvg_ref, lvb_ref, spw_ref, spwt_ref, spb_ref,
             dz_ref, da1_ref, dcb_ref, dlag_ref, dlab_ref, dlvg_ref, dlvb_ref, dspw_ref, dspb_ref):
        i = pl.program_id(0)
        last = pl.num_programs(0) - 1

        @pl.when(i == 0)
        def _():
            for r in (dcb_ref, dlag_ref, dlab_ref, dlvg_ref, dlvb_ref, dspw_ref, dspb_ref):
                r[...] = jnp.zeros_like(r)

        da3 = jnp.concatenate([dc_ref[0], dc_ref[1]], axis=-1)
        xh, rstd = _ln_stats(a1_ref[...])
        a2 = xh * lag_ref[...] + lab_ref[...]
        s = _sigmoid(a2)
        da2 = da3 * (s * (1.0 + a2 * (1.0 - s)))
        dlag_ref[...] += jnp.sum(da2 * xh, axis=0, keepdims=True)
        dlab_ref[...] += jnp.sum(da2, axis=0, keepdims=True)
        dxh = da2 * lag_ref[...]
        da1 = rstd * (dxh - jnp.mean(dxh, axis=-1, keepdims=True) - xh * jnp.mean(dxh * xh, axis=-1, keepdims=True))
        da1_ref[...] = da1
        dcb_ref[...] += jnp.sum(da1, axis=0, keepdims=True)

        tril = _tril_mask()
        for g in range(groups):
            sl = slice(g * LANES, (g + 1) * LANES)
            xh, rstd = _ln_stats(v_ref[0][:, sl].astype(F32))
            lg = lvg_ref[:, sl]
            vnb = (xh * lg + lvb_ref[:, sl]).astype(BF16)
            w = jnp.where(tril, spw_ref[g], 0.0).astype(BF16)
            wt = jnp.where(tril.T, spwt_ref[g], 0.0).astype(BF16)
            bias = spb_ref[:, g:g + 1]
            dbo_all = dc_ref[2 + (g * LANES) // half][:, (g * LANES) % half:(g * LANES) % half + LANES]
            dvn_parts = []
            dw_acc = jnp.zeros((CHUNK, CHUNK), F32)
            db_acc = jnp.zeros((CHUNK, LANES), F32)
            for ch in range(tm // CHUNK):
                rows = slice(ch * CHUNK, (ch + 1) * CHUNK)
                vs = _nn(w, vnb[rows]) + bias
                dbo = dbo_all[rows]
                uv = u_ref[0][rows, sl].astype(F32)
                dz_ref[0, rows, sl] = (dbo * vs).astype(BF16)
                dvs = dbo * uv
                dvsb = dvs.astype(BF16)
                dvn_parts.append(_nn(wt, dvsb))
                dw_acc = dw_acc + _nt(dvsb, vnb[rows])
                db_acc = db_acc + dvs
            dvn = jnp.concatenate(dvn_parts, axis=0)
            dspw_ref[g] += jnp.where(tril, dw_acc, 0.0)
            dspb_ref[g] += db_acc
            dlvg_ref[:, sl] += jnp.sum(dvn * xh, axis=0, keepdims=True)
            dlvb_ref[:, sl] += jnp.sum(dvn, axis=0, keepdims=True)
            dxh = dvn * lg
            dv = rstd * (dxh - jnp.mean(dxh, axis=-1, keepdims=True) - xh * jnp.mean(dxh * xh, axis=-1, keepdims=True))
            dz_ref[1, :, sl] = dv.astype(BF16)

        @pl.when(i == last)
        def _():
            for g in range(groups):
                dspb_ref[g] = jnp.zeros((CHUNK, LANES), F32) + jnp.sum(dspb_ref[g], axis=-1, keepdims=True)

    unit = lambda u: pl.BlockSpec((1, tm, c), lambda i: (u, i, 0))
    vec = pl.BlockSpec((1, c), lambda i: (0, 0))
    sq = pl.BlockSpec((groups, CHUNK, CHUNK), lambda i: (0, 0, 0))
    vshape = jax.ShapeDtypeStruct((1, c), F32)
    sshape = jax.ShapeDtypeStruct((groups, CHUNK, CHUNK), F32)
    return pl.pallas_call(
        body, name="mix_bwd_point", grid=(t // tm,),
        in_specs=[pl.BlockSpec((4, tm, half), lambda i: (0, i, 0)), unit(2), unit(3),
                  pl.BlockSpec((tm, c), lambda i: (i, 0)), vec, vec, vec, vec, sq, sq,
                  pl.BlockSpec((CHUNK, groups), lambda i: (0, 0))],
        out_specs=[pl.BlockSpec((2, tm, c), lambda i: (1, i, 0)), pl.BlockSpec((tm, c), lambda i: (i, 0)),
                   vec, vec, vec, vec, vec, sq, sq],
        out_shape=[jax.ShapeDtypeStruct((4, t, c), BF16), jax.ShapeDtypeStruct((t, c), F32),
                   vshape, vshape, vshape, vshape, vshape, sshape, sshape],
        compiler_params=_cparams(("arbitrary",)),
    )(dcat, z, z, a1, ln_a_g, ln_a_b, ln_v_g, ln_v_b, sp_w, sp_wt, sp_bt)


def mix_bwd_conv(dz, da1, z, conv_w, seq):
    _, t, c = z.shape
    tm = _tile(seq, 512)
    tiles_per_seq = seq // tm
    hb = tm // HALO
    n_halo_blocks = t // HALO

    rc = _tile(tm, CONV_ROWS)

    def body(dz_in_ref, d_ref, dh_ref, av_ref, ag_ref, cw_ref, dz_ref, dcw_ref, d1_ref, sh_ref, part_ref):
        del dz_in_ref
        i = pl.program_id(0)

        @pl.when(i == 0)
        def _():
            part_ref[...] = jnp.zeros_like(part_ref)

        d1_ref[0:tm, :] = d_ref[...]
        d1_ref[tm:, :] = jnp.where((i + 1) % tiles_per_seq == 0, 0.0, dh_ref[...])
        taps = []
        for b in range(8):
            offs = [o for o in range(CONV_WIDTH) if o % 8 == b]
            n_rows = max(offs) - b + tm
            sh_ref[b, 0:n_rows, :] = d1_ref[pl.ds(b, n_rows), :]
            taps += [(b, o - b, CONV_WIDTH - 1 - o) for o in offs]

        def chunk(ci, carry):
            r0 = pl.multiple_of(ci * rc, rc)
            av = av_ref[0, pl.ds(r0, rc), :].astype(F32)
            s = _sigmoid(ag_ref[0, pl.ds(r0, rc), :].astype(F32))
            a0 = av * s
            da0 = jnp.zeros((rc, c), F32)
            for b, ro, k in taps:
                rows = sh_ref[b, pl.ds(r0 + ro, rc), :]
                da0 = da0 + cw_ref[k:k + 1, :] * rows
                prod = a0 * rows
                part_ref[k] += functools.reduce(lambda p, q: p + q, [prod[8 * r:8 * r + 8] for r in range(rc // 8)])
            dz_ref[0, pl.ds(r0, rc), :] = (da0 * s).astype(BF16)
            dz_ref[1, pl.ds(r0, rc), :] = (da0 * av * s * (1.0 - s)).astype(BF16)
            return carry

        lax.fori_loop(0, tm // rc, chunk, 0)

        @pl.when(i == pl.num_programs(0) - 1)
        def _():
            dcw_ref[...] = jnp.sum(part_ref[...], axis=1)

    unit = lambda u: pl.BlockSpec((1, tm, c), lambda i: (u, i, 0))
    return pl.pallas_call(
        body, name="mix_bwd_conv", grid=(t // tm,),
        in_specs=[pl.BlockSpec(memory_space=pl.ANY), pl.BlockSpec((tm, c), lambda i: (i, 0)),
                  pl.BlockSpec((HALO, c), lambda i: (jnp.minimum((i + 1) * hb, n_halo_blocks - 1), 0)),
                  unit(0), unit(1), pl.BlockSpec((CONV_WIDTH, c), lambda i: (0, 0))],
        out_specs=[pl.BlockSpec((2, tm, c), lambda i: (0, i, 0)), pl.BlockSpec((CONV_WIDTH, c), lambda i: (0, 0))],
        out_shape=[jax.ShapeDtypeStruct(dz.shape, BF16), jax.ShapeDtypeStruct((CONV_WIDTH, c), F32)],
        scratch_shapes=[pltpu.VMEM((tm + HALO, c), F32), pltpu.VMEM((8, tm + HALO, c), F32),
                        pltpu.VMEM((CONV_WIDTH, 8, c), F32)],
        input_output_aliases={0: 0},
        compiler_params=_cparams(("arbitrary",)),
    )(dz, da1, da1, z, z, conv_w)


CHIP_FLIPS = ((1, 0), (0, 1), (1, 1))
ANY = pl.BlockSpec(memory_space=pl.ANY)


def _place():
    return lax.axis_index("x"), lax.axis_index("y"), lax.axis_index("c")


def _flip(v, f):
    return 1 - v if f else v


def place_shard(w, layer, chip, dtype, name):
    _, r, cc = w.shape
    rb = _tile(r, 512)

    def body(chip_ref, w_ref, o_ref):
        del chip_ref
        o_ref[0] = w_ref[0].astype(dtype)

    return pl.pallas_call(
        body, name=name,
        grid_spec=pltpu.PrefetchScalarGridSpec(
            num_scalar_prefetch=1, grid=(r // rb,),
            in_specs=[pl.BlockSpec((1, rb, cc), lambda i, chip_ref: (layer, i, 0))],
            out_specs=pl.BlockSpec((1, rb, cc), lambda i, chip_ref: (chip_ref[0], i, 0))),
        out_shape=jax.ShapeDtypeStruct((N_CHIPS, r, cc), dtype),
        compiler_params=_cparams(("parallel",)),
    )(chip, w)


class Carry:
    def __init__(self, arrays, out_shapes, aliased, sem_shapes, start, finish):
        self.arrays, self.out_shapes, self.aliased, self.sem_shapes = list(arrays), list(out_shapes), aliased, list(sem_shapes)
        self.start, self.finish = start, finish


def _call(body, *, name, grid, in_specs, out_specs, out_shape, args, sem, scratch_shapes=(), carry=None):
    if carry is None:
        res = pl.pallas_call(body, name=name, grid=grid, in_specs=in_specs, out_specs=out_specs, out_shape=out_shape,
                             scratch_shapes=list(scratch_shapes), compiler_params=_cparams(sem))(*args)
        return list(res), []
    n_in, n_out, n_scr, nc = len(args), len(out_shape), len(scratch_shapes), len(carry.arrays)

    def full_body(*refs):
        ins, refs = refs[:n_in], refs[n_in:]
        c_ins, refs = refs[:nc], refs[nc:]
        outs, refs = refs[:n_out], refs[n_out:]
        c_outs, refs = refs[:nc], refs[nc:]
        scr, sems = refs[:n_scr], refs[n_scr:]
        first = functools.reduce(lambda a, b: a & b, [pl.program_id(d) == 0 for d in range(len(grid))])
        last = functools.reduce(lambda a, b: a & b, [pl.program_id(d) == grid[d] - 1 for d in range(len(grid))])

        @pl.when(first)
        def _():
            carry.start(c_ins, c_outs, sems)

        body(*ins, *outs, *scr)

        @pl.when(last)
        def _():
            carry.finish(c_ins, c_outs, sems)

    res = pl.pallas_call(
        full_body, name=name, grid=grid, in_specs=list(in_specs) + [ANY] * nc, out_specs=list(out_specs) + [ANY] * nc,
        out_shape=list(out_shape) + carry.out_shapes, scratch_shapes=list(scratch_shapes) + carry.sem_shapes,
        input_output_aliases={n_in + i: n_out + i for i in range(nc)} if carry.aliased else {},
        compiler_params=pltpu.CompilerParams(dimension_semantics=("arbitrary",) * len(grid), vmem_limit_bytes=VMEM_LIMIT,
                                             has_side_effects=True),
    )(*args, *carry.arrays)
    return list(res[:n_out]), list(res[n_out:])


def _gather_ops(shapes, whole):
    n = len(shapes)

    def rows(a, c):
        hr = shapes[a][1] // 2
        return pl.ds(pl.multiple_of(c * hr, 16), hr)

    def start(ins, outs, sems):
        ici_send, ici_recv = sems[0], sems[1]
        x, y, c = _place()
        k = 2 * x + y
        for a in range(n):
            for o, (fx, fy) in enumerate(CHIP_FLIPS):
                src = ins[a].at[k] if whole[a] else ins[a].at[k, rows(a, c)]
                dst = outs[a].at[k] if whole[a] else outs[a].at[k, rows(a, c)]
                pltpu.make_async_remote_copy(
                    src_ref=src, dst_ref=dst, send_sem=ici_send.at[3 * a + o], recv_sem=ici_recv.at[3 * a + o],
                    device_id=(_flip(x, fx), _flip(y, fy), c), device_id_type=MESH).start()

    def finish(ins, outs, sems):
        ici_send, ici_recv, d2d_send, d2d_recv = sems
        x, y, c = _place()
        k = 2 * x + y
        sibling = (x, y, 1 - c)

        def copy(ref, send, recv, a, o):
            return pltpu.make_async_remote_copy(src_ref=ref, dst_ref=ref, send_sem=send.at[3 * a + o],
                                                recv_sem=recv.at[3 * a + o], device_id=sibling, device_id_type=MESH)

        for a in range(n):
            for o, (fx, fy) in enumerate(CHIP_FLIPS):
                kk = 2 * _flip(x, fx) + _flip(y, fy)
                landed = outs[a].at[kk] if whole[a] else outs[a].at[kk, rows(a, c)]
                copy(landed, ici_send, ici_recv, a, o).wait_recv()
                if not whole[a]:
                    copy(landed, d2d_send, d2d_recv, a, o).start()
        for a in range(n):
            for o, (fx, fy) in enumerate(CHIP_FLIPS):
                kk = 2 * _flip(x, fx) + _flip(y, fy)
                mine = ins[a].at[k] if whole[a] else ins[a].at[k, rows(a, c)]
                copy(mine, ici_send, ici_recv, a, o).wait_send()
                if not whole[a]:
                    copy(outs[a].at[kk, rows(a, 1 - c)], d2d_send, d2d_recv, a, o).wait_recv()
                    copy(outs[a].at[kk, rows(a, c)], d2d_send, d2d_recv, a, o).wait_send()

    dma = pltpu.SemaphoreType.DMA
    return start, finish, [dma((3 * n,))] * 4


def gather_carry(bufs):
    start, finish, sems = _gather_ops([b.shape for b in bufs], [False] * len(bufs))
    return Carry(bufs, [jax.ShapeDtypeStruct(b.shape, b.dtype) for b in bufs], True, sems, start, finish)


def allgather_weights(shards, smalls):
    bufs = list(shards) + list(smalls)
    n = len(bufs)
    start, finish, sems = _gather_ops([b.shape for b in bufs], [False] * len(shards) + [True] * len(smalls))

    def body(*refs):
        start(refs[:n], refs[n:2 * n], refs[2 * n:])
        finish(refs[:n], refs[n:2 * n], refs[2 * n:])

    res = pl.pallas_call(
        body, name="allgather_weights", in_specs=[ANY] * n, out_specs=[ANY] * n,
        out_shape=[jax.ShapeDtypeStruct(b.shape, b.dtype) for b in bufs], scratch_shapes=sems,
        input_output_aliases={i: i for i in range(n)},
        compiler_params=pltpu.CompilerParams(has_side_effects=True),
    )(*bufs)
    return res[:len(shards)], res[len(shards):]


def rs_exchange(grads):
    n = len(grads)

    def body(*refs):
        ins, outs = refs[:n], refs[n:2 * n]
        send, recv = refs[2 * n:]
        x, y, c = _place()
        cps = []
        for a in range(n):
            cp = pltpu.make_async_remote_copy(
                src_ref=ins[a].at[:, 1 - c], dst_ref=outs[a], send_sem=send.at[a], recv_sem=recv.at[a],
                device_id=(x, y, 1 - c), device_id_type=MESH)
            cp.start()
            cps.append(cp)
        for cp in cps:
            cp.wait()

    dma = pltpu.SemaphoreType.DMA
    return pl.pallas_call(
        body, name="rs_exchange", in_specs=[ANY] * n, out_specs=[ANY] * n,
        out_shape=[jax.ShapeDtypeStruct((g.shape[0],) + g.shape[2:], g.dtype) for g in grads],
        scratch_shapes=[dma((n,)), dma((n,))],
        compiler_params=pltpu.CompilerParams(has_side_effects=True),
    )(*grads)


def rs_add(g, sib, core, out_dtype, name):
    nk, _, hr, cc = g.shape
    rb = _tile(hr, 256)

    def body(core_ref, g_ref, s_ref, o_ref):
        del core_ref
        o_ref[0] = (g_ref[0, 0] + s_ref[0]).astype(out_dtype)

    return pl.pallas_call(
        body, name=name,
        grid_spec=pltpu.PrefetchScalarGridSpec(
            num_scalar_prefetch=1, grid=(nk, hr // rb),
            in_specs=[pl.BlockSpec((1, 1, rb, cc), lambda k, i, core_ref: (k, core_ref[0], i, 0)),
                      pl.BlockSpec((1, rb, cc), lambda k, i, core_ref: (k, i, 0))],
            out_specs=pl.BlockSpec((1, rb, cc), lambda k, i, core_ref: (k, i, 0))),
        out_shape=jax.ShapeDtypeStruct((nk, hr, cc), out_dtype),
        compiler_params=_cparams(("parallel", "parallel")),
    )(core, g, sib)


def send_carry(parts):
    n = len(parts)

    def copies(ins, outs, sems):
        x, y, c = _place()
        for a in range(n):
            for o, (fx, fy) in enumerate(CHIP_FLIPS):
                kk = 2 * _flip(x, fx) + _flip(y, fy)
                yield pltpu.make_async_remote_copy(
                    src_ref=ins[a].at[kk], dst_ref=outs[a].at[o], send_sem=sems[0].at[3 * a + o],
                    recv_sem=sems[1].at[3 * a + o], device_id=(_flip(x, fx), _flip(y, fy), c), device_id_type=MESH)

    def start(ins, outs, sems):
        for cp in copies(ins, outs, sems):
            cp.start()

    def finish(ins, outs, sems):
        for cp in copies(ins, outs, sems):
            cp.wait()

    dma = pltpu.SemaphoreType.DMA
    return Carry(parts, [jax.ShapeDtypeStruct((3,) + p.shape[1:], p.dtype) for p in parts], False,
                 [dma((3 * n,)), dma((3 * n,))], start, finish)


def rs_sum(recv, part, where, full, layer, n_layers, name):
    _, hr, cc = recv.shape
    rb = _tile(hr, 256)

    def body(*refs):
        r_ref, p_ref, o_ref = refs[1], refs[2], refs[-1]
        o_ref[0, 0] = ((p_ref[0].astype(F32) + r_ref[0].astype(F32)) + r_ref[1].astype(F32)) + r_ref[2].astype(F32)

    in_specs = [pl.BlockSpec((3, rb, cc), lambda i, w_ref: (0, i, 0)),
                pl.BlockSpec((1, rb, cc), lambda i, w_ref: (w_ref[0], i, 0))]
    args = [where, recv, part]
    aliases = {}
    if full is not None:
        in_specs.append(ANY)
        args.append(full)
        aliases = {3: 0}
    return pl.pallas_call(
        body, name=name,
        grid_spec=pltpu.PrefetchScalarGridSpec(
            num_scalar_prefetch=1, grid=(hr // rb,), in_specs=in_specs,
            out_specs=pl.BlockSpec((1, 1, rb, cc), lambda i, w_ref: (layer, w_ref[1], i, 0))),
        out_shape=jax.ShapeDtypeStruct((n_layers, 2, hr, cc), F32),
        input_output_aliases=aliases,
        compiler_params=_cparams(("parallel",)),
    )(*args)


def rs_share(fulls):
    n = len(fulls)

    def body(*refs):
        ins, outs = refs[:n], refs[n:2 * n]
        send, recv = refs[2 * n:]
        x, y, c = _place()
        cps = []
        for a in range(n):
            cp = pltpu.make_async_remote_copy(
                src_ref=ins[a].at[:, c], dst_ref=outs[a].at[:, c], send_sem=send.at[a], recv_sem=recv.at[a],
                device_id=(x, y, 1 - c), device_id_type=MESH)
            cp.start()
            cps.append(cp)
        for a in range(n):
            got = outs[a].at[:, 1 - c]
            pltpu.make_async_remote_copy(
                src_ref=got, dst_ref=got, send_sem=send.at[a], recv_sem=recv.at[a],
                device_id=(x, y, 1 - c), device_id_type=MESH).wait_recv()
        for cp in cps:
            cp.wait_send()

    dma = pltpu.SemaphoreType.DMA
    return pl.pallas_call(
        body, name="rs_share", in_specs=[ANY] * n, out_specs=[ANY] * n,
        out_shape=[jax.ShapeDtypeStruct(f.shape, f.dtype) for f in fulls],
        scratch_shapes=[dma((n,)), dma((n,))],
        input_output_aliases={i: i for i in range(n)},
        compiler_params=pltpu.CompilerParams(has_side_effects=True),
    )(*fulls)


def allreduce_small(v):
    r, w = v.shape

    def body(v_ref, o_ref, buf, send, recv, loc):
        x, y, c = _place()
        me = 4 * x + 2 * y + c
        mine = pltpu.make_async_copy(v_ref, buf.at[me], loc)
        mine.start()
        cps = []
        for o in range(1, N_DEV):
            fx, fy, fc = (o >> 2) & 1, (o >> 1) & 1, o & 1
            cp = pltpu.make_async_remote_copy(
                src_ref=v_ref, dst_ref=buf.at[me], send_sem=send.at[o - 1], recv_sem=recv.at[o - 1],
                device_id=(_flip(x, fx), _flip(y, fy), _flip(c, fc)), device_id_type=MESH)
            cp.start()
            cps.append(cp)
        for o in range(1, N_DEV):
            fx, fy, fc = (o >> 2) & 1, (o >> 1) & 1, o & 1
            peer = 4 * _flip(x, fx) + 2 * _flip(y, fy) + _flip(c, fc)
            pltpu.make_async_remote_copy(
                src_ref=v_ref, dst_ref=buf.at[peer], send_sem=send.at[o - 1], recv_sem=recv.at[o - 1],
                device_id=(x, y, c), device_id_type=MESH).wait_recv()
        for cp in cps:
            cp.wait_send()
        mine.wait()
        acc = buf[0]
        for d in range(1, N_DEV):
            acc = acc + buf[d]
        o_ref[...] = acc

    dma = pltpu.SemaphoreType.DMA
    vm = pl.BlockSpec(memory_space=pltpu.VMEM)
    return pl.pallas_call(
        body, name="allreduce_small", in_specs=[vm], out_specs=vm,
        out_shape=jax.ShapeDtypeStruct((r, w), F32),
        scratch_shapes=[pltpu.VMEM((N_DEV, r, w), F32), dma((N_DEV - 1,)), dma((N_DEV - 1,)), dma],
        compiler_params=pltpu.CompilerParams(has_side_effects=True, vmem_limit_bytes=VMEM_LIMIT),
    )(v)


def adamw(w, g, m, v, name):
    r, cc = w.shape
    rb = _tile(r, 256)

    def body(w_ref, g_ref, m_ref, v_ref, d_ref, nm_ref, nv_ref):
        gv = g_ref[...]
        nm = ADAM_B1 * m_ref[...] + (1.0 - ADAM_B1) * gv
        nv = ADAM_B2 * v_ref[...] + (1.0 - ADAM_B2) * (gv * gv)
        m_hat = nm / (1.0 - ADAM_B1 ** ADAM_STEP)
        v_hat = nv / (1.0 - ADAM_B2 ** ADAM_STEP)
        d_ref[...] = -ADAM_LR * (m_hat / (jnp.sqrt(v_hat) + ADAM_EPS) + ADAM_WD * w_ref[...])
        nm_ref[...] = nm
        nv_ref[...] = nv

    blk = pl.BlockSpec((rb, cc), lambda i: (i, 0))
    shp = jax.ShapeDtypeStruct((r, cc), F32)
    return pl.pallas_call(
        body, name=name, grid=(r // rb,), in_specs=[blk] * 4, out_specs=[blk] * 3, out_shape=[shp] * 3,
        compiler_params=_cparams(("parallel",)),
    )(w, g, m, v)


WEIGHTS = ['g_ffn1', 'w_ffn1_gate', 'w_ffn1_up', 'w_ffn1_down', 'g_mix', 'w_in_ab', 'conv_w', 'conv_b', 'ln_a_g',
           'ln_a_b', 'ln_v_g', 'ln_v_b', 'sp_w', 'sp_b', 'w_out_ab', 'w_qkv', 'w_o', 'g_ffn2', 'w_ffn2_gate',
           'w_ffn2_up', 'w_ffn2_down', 'g_final']
BIG = ['w_ffn1_gate', 'w_ffn1_up', 'w_ffn1_down', 'w_in_ab', 'w_out_ab', 'w_qkv', 'w_o', 'w_ffn2_gate', 'w_ffn2_up',
       'w_ffn2_down']
SMALL = ['g_ffn1', 'g_mix', 'g_ffn2', 'g_final', 'conv_b', 'ln_a_g', 'ln_a_b', 'ln_v_g', 'ln_v_b', 'sp_b', 'sp_w']


CARRY_WEIGHTS = {"ffn_gateup": 9.2e6, "ffn_down": 6.1e6, "mm_in": 5.9e6, "mm_out": 3.3e6}


def _use_order(depth):
    order = []
    for layer in range(depth):
        order += [('w_ffn1_gate', layer), ('w_ffn1_up', layer), ('w_ffn1_down', layer)]
        order += [('w_in_ab', layer // 2), ('w_out_ab', layer // 2)] if layer % 2 == 0 else [('w_qkv', layer // 2), ('w_o', layer // 2)]
        order += [('w_ffn2_gate', layer), ('w_ffn2_up', layer), ('w_ffn2_down', layer)]
    return order


def _rows(a):
    return a.reshape(-1, LANES)


def _pack(parts):
    v = jnp.concatenate([_rows(p) for p in parts], axis=0)
    pad = (-v.shape[0]) % 8
    return jnp.pad(v, ((0, pad), (0, 0)))


def _unpack(v, shapes):
    out, r = [], 0
    for s in shapes:
        n = 1
        for d in s:
            n *= d
        n //= LANES
        out.append(v[r:r + n].reshape(s))
        r += n
    return out


def kernel(x, g_ffn1, w_ffn1_gate, w_ffn1_up, w_ffn1_down, g_mix, w_in_ab, conv_w, conv_b, ln_a_g, ln_a_b, ln_v_g, ln_v_b, sp_w, sp_b, w_out_ab, w_qkv, w_o, g_ffn2, w_ffn2_gate, w_ffn2_up, w_ffn2_down, g_final, loss_target, m_g_ffn1, m_w_ffn1_gate, m_w_ffn1_up, m_w_ffn1_down, m_g_mix, m_w_in_ab, m_conv_w, m_conv_b, m_ln_a_g, m_ln_a_b, m_ln_v_g, m_ln_v_b, m_sp_w, m_sp_b, m_w_out_ab, m_w_qkv, m_w_o, m_g_ffn2, m_w_ffn2_gate, m_w_ffn2_up, m_w_ffn2_down, m_g_final, v_g_ffn1, v_w_ffn1_gate, v_w_ffn1_up, v_w_ffn1_down, v_g_mix, v_w_in_ab, v_conv_w, v_conv_b, v_ln_a_g, v_ln_a_b, v_ln_v_g, v_ln_v_b, v_sp_w, v_sp_b, v_w_out_ab, v_w_qkv, v_w_o, v_g_ffn2, v_w_ffn2_gate, v_w_ffn2_up, v_w_ffn2_down, v_g_final):
    p = dict(locals())
    n_seq, seq, d = x.shape
    t = n_seq * seq
    depth = g_ffn1.shape[0]
    core = lax.axis_index("c")
    chip = 2 * lax.axis_index("x") + lax.axis_index("y")
    xf = x.reshape(t, d)
    target = loss_target.reshape(t, d)

    items = []
    for name in BIG:
        for layer in range(p[name].shape[0]):
            items.append((name, layer))
    chip1 = chip.reshape(1).astype(jnp.int32)
    placed = {it: place_shard(p[it[0]], it[1], chip1, BF16, "place_shard") for it in items}
    first = [('w_ffn1_gate', 0), ('w_ffn1_up', 0)]
    gathered, (conv_w4,) = allgather_weights([placed[it] for it in first],
                                             [place_shard(conv_w, 0, chip1, F32, "place_conv_w")])
    wt = dict(zip(first, gathered))
    waiting = [it for it in _use_order(depth) if it not in wt]

    def riders(name):
        room, take = CARRY_WEIGHTS[name], []
        for it in list(waiting):
            if placed[it].size <= room:
                room -= placed[it].size
                take.append(it)
                waiting.remove(it)
        return (take, gather_carry([placed[it] for it in take])) if take else (take, None)

    def landed(take, carried):
        wt.update(zip(take, carried))

    def weight(it):
        if it not in wt:
            waiting.remove(it)
            (wt[it],), _ = allgather_weights([placed[it]], [])
        return wt[it]

    c_mix = conv_w4.shape[2] * N_CHIPS
    conv_full = jnp.transpose(conv_w4, (1, 0, 2)).reshape(CONV_WIDTH, c_mix)
    vec = lambda a: a.reshape(1, -1)
    sp_bt = sp_b[0].T
    sp_wt = jnp.transpose(sp_w[0], (0, 2, 1))
    d_ff = w_ffn1_gate.shape[2]
    n_in = w_in_ab.shape[2]
    n_qkv = w_qkv.shape[2] // 3

    saved = []
    xc = xf
    h = rmsnorm_fwd(xc, vec(g_ffn1[0]), "norm_first")
    for layer in range(depth):
        s = {}
        for half, (gn, wn) in enumerate((('g_ffn1', 'w_ffn1'), ('g_ffn2', 'w_ffn2'))):
            if half == 1:
                s['x_mix'], s['h_mix'] = xc, h
                if layer % 2 == 0:
                    w_in = weight(('w_in_ab', layer // 2))
                    take, carry = riders("mm_in")
                    (z,), got = colmm(h, [w_in], n_in, BF16, "mm_in", carry)
                    landed(take, got)
                    cat, a1 = mix_fwd(z, conv_full, conv_b, ln_a_g, ln_a_b, vec(ln_v_g), vec(ln_v_b), sp_w[0], sp_bt, seq)
                    s.update(z=z, cat=cat, a1=a1)
                    w_out = weight(('w_out_ab', layer // 2))
                    take, carry = riders("mm_out")
                    (xc, h), got = rowmm(cat, w_out, xc, 1.0, "mm_out", carry, vec(g_ffn2[layer]))
                    landed(take, got)
                else:
                    (qkv,), _ = colmm(h, [weight(('w_qkv', layer // 2))], n_qkv, BF16, "mm_qkv")
                    o, tot, cnt = attn_fwd(qkv, n_seq, seq)
                    s.update(qkv=qkv, o=o, tot=tot, cnt=cnt)
                    (xc, h), _ = rowmm(o, weight(('w_o', layer // 2)), xc, 1.0, "mm_o", None, vec(g_ffn2[layer]))
            s['x' + wn] = xc
            w_gate, w_up = weight((wn + '_gate', layer)), weight((wn + '_up', layer))
            take, carry = riders("ffn_gateup")
            (silu, udsilu, act), got = colmm(h, [w_gate, w_up], d_ff, BF16, "ffn_gateup", carry, swiglu=True)
            landed(take, got)
            s.update({'h' + wn: h, 'swiglu' + wn: (silu, udsilu), 'act' + wn: act})
            w_down = weight((wn + '_down', layer))
            take, carry = riders("ffn_down")
            following = g_mix[layer] if half == 0 else (g_ffn1[layer + 1] if layer + 1 < depth else None)
            (xc, h), got = rowmm(act, w_down, xc, 0.5, "ffn_down", carry, None if following is None else vec(following))
            landed(take, got)
        saved.append(s)

    loss8, dx, dxb, dg_final = loss_head(xc, vec(g_final), target)
    loss = lax.psum(loss8[0, 0], ("x", "y", "c"))

    gw = {}
    gs = {}
    core1 = core.reshape(1).astype(jnp.int32)
    ready = []
    part, recv = {}, {}

    def leaving():
        its = list(ready)
        ready.clear()
        halves = lambda a: a.reshape(N_CHIPS, 2, a.shape[1] // 2, a.shape[2])
        theirs = rs_exchange([halves(gw[it][1]) for it in its])
        sums = [rs_add(halves(gw[it][0]), sb, core1, REDUCE_DTYPE, "rs_add") for it, sb in zip(its, theirs)]
        part.update(zip(its, sums))
        return its, send_carry(sums)

    for layer in reversed(range(depth)):
        s = saved[layer]
        for half, (gn, wn) in reversed(list(enumerate((('g_ffn1', 'w_ffn1'), ('g_ffn2', 'w_ffn2'))))):
            wd = wt[(wn + '_down', layer)]
            dgate, dup = rowmm_t(dxb, wd, 0.5, BF16, "ffn_bwd_act", swiglu=s['swiglu' + wn])
            gw[(wn + '_down', layer)] = dw_row(s['act' + wn], dxb, 0.5, "ffn_dw_down")
            gw[(wn + '_gate', layer)], gw[(wn + '_up', layer)] = dw_col(s['h' + wn], [dgate, dup], N_CHIPS, d_ff, "ffn_dw_gateup")
            ready.extend([(wn + '_down', layer), (wn + '_gate', layer), (wn + '_up', layer)])
            its, carry = leaving()
            (dx, dxb, dg), got = colmm_t([dgate, dup], [wt[(wn + '_gate', layer)], wt[(wn + '_up', layer)]], d_ff,
                                         s['x' + wn], vec(p[gn][layer]), dx, "ffn_bwd_in", carry)
            recv.update(zip(its, got))
            gs[(gn, layer)] = dg
            if half == 1:
                if layer % 2 == 0:
                    i = layer // 2
                    w_out = wt[('w_out_ab', i)]
                    dcat = rowmm_t(dxb, w_out, 1.0, F32, "mm_out_t")
                    gw[('w_out_ab', i)] = dw_row(s['cat'], dxb, 1.0, "dw_out")
                    dz, da1, dcb, dlag, dlab, dlvg, dlvb, dspw, dspb = mix_bwd_point(
                        dcat, s['z'], s['a1'], ln_a_g, ln_a_b, vec(ln_v_g), vec(ln_v_b), sp_w[0], sp_wt, sp_bt, seq)
                    dz, dcw = mix_bwd_conv(dz, da1, s['z'], conv_full, seq)
                    gs.update({('conv_b', i): dcb, ('ln_a_g', i): dlag, ('ln_a_b', i): dlab, ('ln_v_g', i): dlvg,
                               ('ln_v_b', i): dlvb, ('sp_w', i): dspw, ('sp_b', i): dspb[:, :, 0], ('conv_w', i): dcw})
                    (gw[('w_in_ab', i)],) = dw_col(s['h_mix'], [dz], N_CHIPS, n_in, "dw_in")
                    ready.extend([('w_out_ab', i), ('w_in_ab', i)])
                    its, carry = leaving()
                    (dx, dxb, dg), got = colmm_t([dz], [wt[('w_in_ab', i)]], n_in, s['x_mix'], vec(g_mix[layer]), dx,
                                                 "mm_in_t", carry)
                    recv.update(zip(its, got))
                else:
                    i = layer // 2
                    w_o4 = wt[('w_o', i)]
                    do = rowmm_t(dxb, w_o4, 1.0, BF16, "mm_o_t")
                    gw[('w_o', i)] = dw_row(s['o'], dxb, 1.0, "dw_o")
                    dq, dk, dv = attn_bwd(s['qkv'], do, s['tot'], s['cnt'], n_seq, seq)
                    dqkv = jnp.concatenate([dq, dk, dv], axis=0)
                    (gw[('w_qkv', i)],) = dw_col(s['h_mix'], [dqkv], N_CHIPS, n_qkv, "dw_qkv")
                    ready.extend([('w_o', i), ('w_qkv', i)])
                    its, carry = leaving()
                    (dx, dxb, dg), got = colmm_t([dqkv], [wt[('w_qkv', i)]], n_qkv, s['x_mix'], vec(g_mix[layer]), dx,
                                                 "mm_qkv_t", carry)
                    recv.update(zip(its, got))
                gs[('g_mix', layer)] = dg
    grad_x = dx.reshape(x.shape)

    assert not ready and set(recv) == set(items)
    where = jnp.stack([chip, core]).astype(jnp.int32)
    fulls = []
    for name in BIG:
        full = None
        n_layers = p[name].shape[0]
        for layer in range(n_layers):
            full = rs_sum(recv[(name, layer)], part[(name, layer)], where, full, layer, n_layers, "rs_sum")
        fulls.append(full)
    shared = rs_share(fulls)
    grads = {name: sh.reshape(p[name].shape) for name, sh in zip(BIG, shared)}

    stack = lambda name: jnp.concatenate([gs[(name, layer)].reshape((1,) + p[name].shape[1:]) for layer in range(p[name].shape[0])], axis=0)
    small_g = [stack(name) if name != 'g_final' else dg_final.reshape(p[name].shape) for name in SMALL]
    packed = _pack(small_g + [gs[('conv_w', 0)]])
    red = allreduce_small(packed)
    outs = _unpack(red, [p[name].shape for name in SMALL] + [(CONV_WIDTH, c_mix)])
    for name, g in zip(SMALL, outs[:-1]):
        grads[name] = g
    conv_g = outs[-1].reshape(CONV_WIDTH, N_CHIPS, c_mix // N_CHIPS)
    grads['conv_w'] = lax.dynamic_index_in_dim(conv_g, chip, axis=1, keepdims=False).reshape(conv_w.shape)

    delta, new_m, new_v = {}, {}, {}
    for name in BIG:
        shp = p[name].shape
        two = lambda a: a.reshape(shp[0] * shp[1], shp[2])
        dl, nm, nv = adamw(two(p[name]), two(grads[name]), two(p['m_' + name]), two(p['v_' + name]), "adamw")
        delta[name], new_m[name], new_v[name] = dl.reshape(shp), nm.reshape(shp), nv.reshape(shp)
    small_names = SMALL + ['conv_w']
    pk = lambda pre: _pack([p[pre + name] for name in small_names])
    dl, nm, nv = adamw(pk(''), _pack([grads[name] for name in small_names]), pk('m_'), pk('v_'), "adamw_small")
    shapes = [p[name].shape for name in small_names]
    for dst, val in ((delta, dl), (new_m, nm), (new_v, nv)):
        for name, a in zip(small_names, _unpack(val, shapes)):
            dst[name] = a

    return (loss, grad_x, *[grads[n] for n in WEIGHTS], *[delta[n] for n in WEIGHTS],
            *[new_m[n] for n in WEIGHTS], *[new_v[n] for n in WEIGHTS])
```

```python
import functools

import jax
import jax.numpy as jnp
from jax import lax
from jax.experimental import pallas as pl
from jax.experimental.pallas import tpu as pltpu

F32 = jnp.float32
BF16 = jnp.bfloat16
EPS = 1e-6
HEAD_DIM = 64
CONV_WIDTH = 31
CHUNK = 128
KBLK = 128
ATT_BLOCK = 256
ATT_LANES = 256
DW_TOKENS = 2048
CONV_ROWS = 64
MASKED = -1e30
STICK_GONE = -110.0
LANES = 128
HALO = 32
ADAM_LR, ADAM_B1, ADAM_B2, ADAM_EPS, ADAM_WD, ADAM_STEP = 0.001, 0.9, 0.999, 1e-08, 0.01, 10
VMEM_LIMIT = 56 * 1024 * 1024
MESH = pl.DeviceIdType.MESH
N_CHIPS = 4
N_DEV = 8
REDUCE_DTYPE = BF16


def _cparams(sem):
    return pltpu.CompilerParams(dimension_semantics=sem, vmem_limit_bytes=VMEM_LIMIT)


def _nt(a, b):
    return lax.dot_general(a, b, (((1,), (1,)), ((), ())), preferred_element_type=F32)


def _tn(a, b):
    return lax.dot_general(a, b, (((0,), (0,)), ((), ())), preferred_element_type=F32)


def _nn(a, b):
    return jnp.dot(a, b, preferred_element_type=F32)


def _sigmoid(x):
    return 0.5 * jnp.tanh(0.5 * x) + 0.5


def _tile(t, want):
    if t <= want:
        return t
    for cand in range(want - want % 8, 7, -8):
        if t % cand == 0:
            return cand
    raise ValueError((t, want))


def rmsnorm_fwd(x, g, name):
    t, d = x.shape
    tm = _tile(t, 512)

    def body(x_ref, g_ref, h_ref):
        xv = x_ref[...]
        r = lax.rsqrt(jnp.mean(xv * xv, axis=-1, keepdims=True) + EPS)
        h_ref[...] = (xv * r * g_ref[...]).astype(BF16)

    return pl.pallas_call(
        body, name=name, grid=(t // tm,),
        in_specs=[pl.BlockSpec((tm, d), lambda i: (i, 0)), pl.BlockSpec((1, d), lambda i: (0, 0))],
        out_specs=pl.BlockSpec((tm, d), lambda i: (i, 0)),
        out_shape=jax.ShapeDtypeStruct((t, d), BF16),
        compiler_params=_cparams(("parallel",)),
    )(x, g)


def colmm(h, ws, nu, out_dtype, name, carry=None, swiglu=False):
    t, k = h.shape
    j, _, nj = ws[0].shape
    per = nj // nu
    units = j * per
    tm = _tile(t, 512)
    nw = len(ws)
    n_out = 3 if swiglu else nw

    def body(*refs):
        h_ref = refs[0]
        hv = h_ref[...]
        if swiglu:
            silu_ref, udsilu_ref, act_ref = refs[1 + nw:]
            gv = _nn(hv, refs[1][0])
            uv = _nn(hv, refs[2][0])
            s = _sigmoid(gv)
            silu = gv * s
            silu_ref[0] = silu.astype(out_dtype)
            udsilu_ref[0] = (uv * (s + silu * (1.0 - s))).astype(out_dtype)
            act_ref[0] = (silu * uv).astype(out_dtype)
            return
        for n in range(nw):
            res = _nn(hv, refs[1 + n][0]).astype(out_dtype)
            for u in range(per):
                refs[1 + nw + n][u] = res[:, u * nu:(u + 1) * nu]

    assert not swiglu or (nw == 2 and per == 1)
    w_spec = pl.BlockSpec((1, k, nj), lambda s, i: (s, 0, 0))
    o_spec = pl.BlockSpec((per, tm, nu), lambda s, i: (s, i, 0))
    return _call(
        body, name=name, grid=(j, t // tm),
        in_specs=[pl.BlockSpec((tm, k), lambda s, i: (i, 0))] + [w_spec] * nw,
        out_specs=[o_spec] * n_out,
        out_shape=[jax.ShapeDtypeStruct((units, t, nu), out_dtype)] * n_out,
        args=[h, *ws], sem=("parallel", "parallel"), carry=carry)


def rowmm(a, w, resid, scale, name, carry=None, norm_g=None):
    u_n, t, ku = a.shape
    n = w.shape[2]
    tm = _tile(t, 256)

    def body(a_ref, w_ref, r_ref, *rest):
        acc = jnp.zeros((tm, n), F32)
        for u in range(u_n):
            acc = acc + _nn(a_ref[u], w_ref[u])
        out = r_ref[...] + scale * acc
        if norm_g is None:
            (o_ref,) = rest
        else:
            g_ref, o_ref, h_ref = rest
            r = lax.rsqrt(jnp.mean(out * out, axis=-1, keepdims=True) + EPS)
            h_ref[...] = (out * r * g_ref[...]).astype(BF16)
        o_ref[...] = out

    row = pl.BlockSpec((tm, n), lambda i: (i, 0))
    normed = norm_g is not None
    outs, carried = _call(
        body, name=name, grid=(t // tm,),
        in_specs=[pl.BlockSpec((u_n, tm, ku), lambda i: (0, i, 0)), pl.BlockSpec((u_n, ku, n), lambda i: (0, 0, 0)),
                  row] + [pl.BlockSpec((1, n), lambda i: (0, 0))] * normed,
        out_specs=[row] + [row] * normed,
        out_shape=[jax.ShapeDtypeStruct((t, n), F32)] + [jax.ShapeDtypeStruct((t, n), BF16)] * normed,
        args=[a, w, resid] + [norm_g] * normed, sem=("parallel",), carry=carry)
    return (outs[0], outs[1] if normed else None), carried


def rowmm_t(dyb, w, scale, out_dtype, name, swiglu=None):
    t, n = dyb.shape
    u_n, ku, _ = w.shape
    tm = _tile(t, 512)

    if swiglu is None:
        def body(dy_ref, w_ref, o_ref):
            o_ref[0] = (scale * _nt(dy_ref[...], w_ref[0])).astype(out_dtype)

        return pl.pallas_call(
            body, name=name, grid=(u_n, t // tm),
            in_specs=[pl.BlockSpec((tm, n), lambda u, i: (i, 0)), pl.BlockSpec((1, ku, n), lambda u, i: (u, 0, 0))],
            out_specs=pl.BlockSpec((1, tm, ku), lambda u, i: (u, i, 0)),
            out_shape=jax.ShapeDtypeStruct((u_n, t, ku), out_dtype),
            compiler_params=_cparams(("parallel", "parallel")),
        )(dyb, w)

    def body(dy_ref, w_ref, silu_ref, udsilu_ref, dg_ref, du_ref):
        dy = dy_ref[...]
        for u in range(u_n):
            dact = scale * _nt(dy, w_ref[u])
            dg_ref[u] = (dact * udsilu_ref[u].astype(F32)).astype(BF16)
            du_ref[u] = (dact * silu_ref[u].astype(F32)).astype(BF16)

    blk = pl.BlockSpec((u_n, tm, ku), lambda i: (0, i, 0))
    return pl.pallas_call(
        body, name=name, grid=(t // tm,),
        in_specs=[pl.BlockSpec((tm, n), lambda i: (i, 0)), pl.BlockSpec((u_n, ku, n), lambda i: (0, 0, 0)), blk, blk],
        out_specs=[blk] * 2, out_shape=[jax.ShapeDtypeStruct((u_n, t, ku), BF16)] * 2,
        compiler_params=_cparams(("parallel",)),
    )(dyb, w, *swiglu)


def colmm_t(dzs, ws, nu, x, g, dy_in, name, carry=None):
    t, k = x.shape
    j, _, nj = ws[0].shape
    per = nj // nu
    units = j * per
    nw = len(ws)
    tm = _tile(t, 256)

    def body(*refs):
        dz_refs = refs[:nw]
        w_refs = refs[nw:2 * nw]
        x_ref, g_ref, dy_ref, dx_ref, dxb_ref, dg_ref = refs[2 * nw:]
        i = pl.program_id(0)
        dh = jnp.zeros((tm, k), F32)
        for n in range(nw):
            for u in range(units):
                wv = w_refs[n][u // per, :, (u % per) * nu:(u % per + 1) * nu]
                dh = dh + _nt(dz_refs[n][u], wv)
        xv = x_ref[...]
        gv = g_ref[...]
        r = lax.rsqrt(jnp.mean(xv * xv, axis=-1, keepdims=True) + EPS)
        uu = dh * gv
        dx = dy_ref[...] + r * uu - xv * (r * r * r * jnp.mean(uu * xv, axis=-1, keepdims=True))
        dx_ref[...] = dx
        dxb_ref[...] = dx.astype(BF16)
        part = jnp.sum(dh * (xv * r), axis=0, keepdims=True)

        @pl.when(i == 0)
        def _():
            dg_ref[...] = part

        @pl.when(i > 0)
        def _():
            dg_ref[...] += part

    dz_spec = pl.BlockSpec((units, tm, nu), lambda i: (0, i, 0))
    w_spec = pl.BlockSpec((j, k, nj), lambda i: (0, 0, 0))
    row = pl.BlockSpec((tm, k), lambda i: (i, 0))
    vec = pl.BlockSpec((1, k), lambda i: (0, 0))
    return _call(
        body, name=name, grid=(t // tm,),
        in_specs=[dz_spec] * nw + [w_spec] * nw + [row, vec, row],
        out_specs=[row, row, vec],
        out_shape=[jax.ShapeDtypeStruct((t, k), F32), jax.ShapeDtypeStruct((t, k), BF16),
                   jax.ShapeDtypeStruct((1, k), F32)],
        args=[*dzs, *ws, x, g, dy_in], sem=("arbitrary",), carry=carry)


def dw_col(h, dzs, j, nu, name):
    t, k = h.shape
    units = dzs[0].shape[0]
    per = units // j
    nw = len(dzs)
    tt = _tile(t, DW_TOKENS)

    def body(*refs):
        h_ref = refs[0]
        s = pl.program_id(1)
        hv = h_ref[...]
        outs, copies = refs[1 + nw:1 + 2 * nw], refs[1 + 2 * nw:]

        @pl.when(s == 0)
        def _():
            for o_ref in outs:
                o_ref[...] = jnp.zeros_like(o_ref)

        for n in range(nw):
            for u in range(per):
                outs[n][0, :, u * nu:(u + 1) * nu] += _tn(hv, refs[1 + n][u])

        @pl.when(s == pl.num_programs(1) - 1)
        def _():
            for o_ref, c_ref in zip(outs, copies):
                c_ref[...] = o_ref[...].astype(REDUCE_DTYPE)

    o_spec = pl.BlockSpec((1, k, per * nu), lambda u, s: (u, 0, 0))
    res = pl.pallas_call(
        body, name=name, grid=(j, t // tt),
        in_specs=[pl.BlockSpec((tt, k), lambda u, s: (s, 0))] + [pl.BlockSpec((per, tt, nu), lambda u, s: (u, s, 0))] * nw,
        out_specs=[o_spec] * (2 * nw),
        out_shape=[jax.ShapeDtypeStruct((j, k, per * nu), F32)] * nw
        + [jax.ShapeDtypeStruct((j, k, per * nu), REDUCE_DTYPE)] * nw,
        compiler_params=_cparams(("parallel", "arbitrary")),
    )(h, *dzs)
    return list(zip(res[:nw], res[nw:]))


def dw_row(a, dyb, scale, name):
    u_n, t, ku = a.shape
    n = dyb.shape[1]
    tt = _tile(t, DW_TOKENS)

    def body(a_ref, dy_ref, o_ref, c_ref):
        @pl.when(pl.program_id(1) == 0)
        def _():
            o_ref[...] = jnp.zeros_like(o_ref)

        o_ref[0] += scale * _tn(a_ref[0], dy_ref[...])

        @pl.when(pl.program_id(1) == pl.num_programs(1) - 1)
        def _():
            c_ref[...] = o_ref[...].astype(REDUCE_DTYPE)

    o_spec = pl.BlockSpec((1, ku, n), lambda u, s: (u, 0, 0))
    return tuple(pl.pallas_call(
        body, name=name, grid=(u_n, t // tt),
        in_specs=[pl.BlockSpec((1, tt, ku), lambda u, s: (u, s, 0)), pl.BlockSpec((tt, n), lambda u, s: (s, 0))],
        out_specs=[o_spec, o_spec],
        out_shape=[jax.ShapeDtypeStruct((u_n, ku, n), F32), jax.ShapeDtypeStruct((u_n, ku, n), REDUCE_DTYPE)],
        compiler_params=_cparams(("parallel", "arbitrary")),
    )(a, dyb))


def loss_head(x, g, target):
    t, d = x.shape
    tm = _tile(t, 256)

    def body(x_ref, g_ref, t_ref, loss_ref, dx_ref, dxb_ref, dg_ref):
        i = pl.program_id(0)
        xv = x_ref[...]
        gv = g_ref[...]
        r = lax.rsqrt(jnp.mean(xv * xv, axis=-1, keepdims=True) + EPS)
        xh = xv * r
        err = xh * gv - t_ref[...]
        dy = err * (1.0 / d)
        uu = dy * gv
        dx = r * uu - xv * (r * r * r * jnp.mean(uu * xv, axis=-1, keepdims=True))
        dx_ref[...] = dx
        dxb_ref[...] = dx.astype(BF16)
        dg_part = jnp.sum(dy * xh, axis=0, keepdims=True)
        row = jnp.sum(err * err, axis=-1, keepdims=True) * (0.5 / d)
        l_part = jnp.zeros((8, LANES), F32) + jnp.sum(row, axis=0, keepdims=True)

        @pl.when(i == 0)
        def _():
            dg_ref[...] = dg_part
            loss_ref[...] = l_part

        @pl.when(i > 0)
        def _():
            dg_ref[...] += dg_part
            loss_ref[...] += l_part

    row = pl.BlockSpec((tm, d), lambda i: (i, 0))
    vec = pl.BlockSpec((1, d), lambda i: (0, 0))
    return pl.pallas_call(
        body, name="loss_head", grid=(t // tm,),
        in_specs=[row, vec, row],
        out_specs=[pl.BlockSpec((8, LANES), lambda i: (0, 0)), row, row, vec],
        out_shape=[jax.ShapeDtypeStruct((8, LANES), F32), jax.ShapeDtypeStruct((t, d), F32),
                   jax.ShapeDtypeStruct((t, d), BF16), jax.ShapeDtypeStruct((1, d), F32)],
        compiler_params=_cparams(("arbitrary",)),
    )(x, g, target)


def _split(v):
    hi = v.astype(BF16)
    lo = (v - hi.astype(F32)).astype(BF16)
    return hi, lo


def _keysums(v, m_ext):
    hi, lo = _split(v)
    outs = []
    for j in range(v.shape[1] // KBLK):
        sl = slice(j * KBLK, (j + 1) * KBLK)
        cs = _nn(jnp.concatenate([hi[:, sl], lo[:, sl]], axis=1), m_ext)
        outs.append((cs[:, :KBLK], cs[:, KBLK:]))
    return outs


def _softplus_parts(z):
    sp = jnp.maximum(z, 0.0) + jnp.log(1.0 + jnp.exp(-jnp.abs(z)))
    return sp, z - sp


def _sum_matrices():
    r = lax.broadcasted_iota(jnp.int32, (2 * KBLK, 2 * KBLK), 0) % KBLK
    c = lax.broadcasted_iota(jnp.int32, (2 * KBLK, 2 * KBLK), 1)
    suffix = jnp.where((r > c) | (c >= KBLK), 1.0, 0.0).astype(BF16)
    prefix = jnp.where((r <= c) | (c >= KBLK), 1.0, 0.0).astype(BF16)
    return suffix, prefix


def _att_geometry(qkv, seq):
    upp = qkv.shape[0] // 3
    bq = min(ATT_BLOCK, seq)
    per_unit = (2 * LANES) // ATT_LANES
    return upp, bq, seq // bq, bq // KBLK, per_unit, upp * per_unit, ATT_LANES // HEAD_DIM


def _head_lanes(rows, heads):
    lane = lax.broadcasted_iota(jnp.int32, (rows, ATT_LANES), 1)
    return [(lane >= HEAD_DIM * h) & (lane < HEAD_DIM * (h + 1)) for h in range(heads)]


def attn_fwd(qkv, n_seq, seq):
    t = qkv.shape[1]
    upp, bq, nq, nsub, per_unit, groups, heads = _att_geometry(qkv, seq)
    suffix_m, _ = _sum_matrices()

    def body(q_ref, k_ref, v_ref, m_ref, o_ref, tot_ref, cnt_ref):
        qi = pl.program_id(2)
        step_id = (pl.program_id(0) * groups + pl.program_id(1)) * nq + qi
        in_head = _head_lanes(bq, heads)
        only = lambda v, h: jnp.where(in_head[h], v, jnp.zeros_like(v))
        q_all = q_ref[0] * jnp.asarray(HEAD_DIM ** -0.5, BF16)
        qs = [only(q_all, h) for h in range(heads)]
        m_ext = m_ref[...]
        row = lax.broadcasted_iota(jnp.int32, (bq, bq), 0)
        col = lax.broadcasted_iota(jnp.int32, (bq, bq), 1)
        diag_mask = col < row

        def block(kj, carry, mask):
            off = pl.multiple_of(kj * bq, bq)
            k_all = k_ref[0, pl.ds(off, bq), :]
            v_all = v_ref[0, pl.ds(off, bq), :]
            rems, acc = carry
            out = []
            for h in range(heads):
                rem = rems[h]
                z = _nt(qs[h], k_all)
                if mask is not None:
                    z = jnp.where(mask, z, MASKED)
                sp, ls = _softplus_parts(z)
                sums = _keysums(-sp, m_ext)
                parts = [None] * nsub
                for j in reversed(range(nsub)):
                    suf, total = sums[j]
                    parts[j] = jnp.exp(ls[:, j * KBLK:(j + 1) * KBLK] + suf + rem)
                    rem = rem + total
                a = jnp.concatenate(parts, axis=1)
                acc = acc + _nn(a.astype(BF16), only(v_all, h))
                out.append(rem)
            return tuple(out), acc

        def most_left(c):
            return functools.reduce(jnp.maximum, [jnp.max(r) for r in c[0]])

        def more(s):
            return (s[0] < qi) & (s[1] > STICK_GONE)

        def step(s):
            c = block(qi - 1 - s[0], s[2], None)
            return s[0] + 1, most_left(c), c

        zero = jnp.zeros((bq, LANES), F32)
        carry = block(qi, ((zero,) * heads, jnp.zeros((bq, ATT_LANES), F32)), diag_mask)
        n_left, _, (rems, acc) = lax.while_loop(more, step, (jnp.int32(0), most_left(carry), carry))
        o_ref[0] = acc.astype(BF16)
        first = lax.broadcasted_iota(jnp.int32, (bq, LANES), 1) < HEAD_DIM
        tot_ref[...] = jnp.concatenate([jnp.where(first, rems[h], rems[h + 1]) for h in range(0, heads, 2)], axis=1)
        cnt_ref[step_id] = n_left.astype(F32)

    qblk = lambda b, g, i: (g // per_unit, b * nq + i, g % per_unit)
    return pl.pallas_call(
        body, name="attn_fwd", grid=(n_seq, groups, nq),
        in_specs=[pl.BlockSpec((1, bq, ATT_LANES), qblk),
                  pl.BlockSpec((1, seq, ATT_LANES), lambda b, g, i: (upp + g // per_unit, b, g % per_unit)),
                  pl.BlockSpec((1, seq, ATT_LANES), lambda b, g, i: (2 * upp + g // per_unit, b, g % per_unit)),
                  pl.BlockSpec((2 * KBLK, 2 * KBLK), lambda b, g, i: (0, 0))],
        out_specs=[pl.BlockSpec((1, bq, ATT_LANES), qblk),
                   pl.BlockSpec((bq, ATT_LANES), lambda b, g, i: (b * nq + i, g)),
                   pl.BlockSpec(memory_space=pltpu.SMEM)],
        out_shape=[jax.ShapeDtypeStruct((upp, t, 2 * LANES), BF16), jax.ShapeDtypeStruct((t, upp * 2 * LANES), F32),
                   jax.ShapeDtypeStruct((n_seq * groups * nq,), F32)],
        compiler_params=_cparams(("arbitrary", "arbitrary", "arbitrary")),
    )(qkv, qkv, qkv, suffix_m)


def attn_bwd(qkv, do, tot, cnt, n_seq, seq):
    t = qkv.shape[1]
    upp, bq, nq, nsub, per_unit, groups, heads = _att_geometry(qkv, seq)
    _, prefix_m = _sum_matrices()
    scale = HEAD_DIM ** -0.5

    def body(q_ref, k_ref, v_ref, do_ref, tot_ref, m_ref, cnt_ref, dq_ref, dk_ref, dv_ref, dk_acc, dv_acc):
        qi = pl.program_id(2)
        step_id = (pl.program_id(0) * groups + pl.program_id(1)) * nq + qi
        n_left = jnp.clip(cnt_ref[step_id].astype(jnp.int32), 0, qi)
        in_head = _head_lanes(bq, heads)
        only = lambda v, h: jnp.where(in_head[h], v, jnp.zeros_like(v))
        q_all = q_ref[0] * jnp.asarray(scale, BF16)
        do_all = do_ref[0]
        qs = [only(q_all, h) for h in range(heads)]
        dos = [only(do_all, h) for h in range(heads)]
        first = lax.broadcasted_iota(jnp.int32, (bq, LANES), 1) < HEAD_DIM
        tots = []
        for h in range(0, heads, 2):
            both = tot_ref[:, h // 2 * LANES:(h // 2 + 1) * LANES]
            swapped = pltpu.roll(both, HEAD_DIM, 1)
            tots += [jnp.where(first, both, swapped), jnp.where(first, swapped, both)]
        m_ext = m_ref[...]
        row = lax.broadcasted_iota(jnp.int32, (bq, bq), 0)
        col = lax.broadcasted_iota(jnp.int32, (bq, bq), 1)
        diag_mask = col < row

        @pl.when(qi == 0)
        def _():
            dk_acc[...] = jnp.zeros_like(dk_acc)
            dv_acc[...] = jnp.zeros_like(dv_acc)

        def block(kj, carry, mask):
            off = pl.multiple_of(kj * bq, bq)
            k_all = k_ref[0, pl.ds(off, bq), :]
            v_all = v_ref[0, pl.ds(off, bq), :]
            pres, gpres, dq = carry
            dk_part = jnp.zeros((bq, ATT_LANES), F32)
            dv_part = jnp.zeros((bq, ATT_LANES), F32)
            pres_out, gpres_out = [], []
            for h in range(heads):
                pre, gpre = pres[h], gpres[h]
                z = _nt(qs[h], k_all)
                if mask is not None:
                    z = jnp.where(mask, z, MASKED)
                sp, ls = _softplus_parts(z)
                sums = _keysums(-sp, m_ext)
                parts = []
                for j in range(nsub):
                    pin, ptot = sums[j]
                    parts.append(jnp.exp(ls[:, j * KBLK:(j + 1) * KBLK] + (tots[h] - (pre + pin))))
                    pre = pre + ptot
                a = jnp.concatenate(parts, axis=1)
                g = a * _nt(dos[h], v_all)
                gsums = _keysums(g, m_ext)
                parts = []
                for j in range(nsub):
                    gin, gtot = gsums[j]
                    parts.append(gpre + gin)
                    gpre = gpre + gtot
                dz = g - jnp.exp(ls) * jnp.concatenate(parts, axis=1)
                dzb = dz.astype(BF16)
                dq = dq + _nn(dzb, only(k_all, h))
                dk_part = dk_part + _tn(dzb, qs[h])
                dv_part = dv_part + _tn(a.astype(BF16), dos[h])
                pres_out.append(pre)
                gpres_out.append(gpre)
            dk_acc[pl.ds(off, bq), :] += dk_part
            dv_acc[pl.ds(off, bq), :] += dv_part
            return tuple(pres_out), tuple(gpres_out), dq

        zero = jnp.zeros((bq, LANES), F32)
        carry = ((zero,) * heads, (zero,) * heads, jnp.zeros((bq, ATT_LANES), F32))
        carry = lax.fori_loop(qi - n_left, qi, lambda kj, c: block(kj, c, None), carry)
        carry = block(qi, carry, diag_mask)
        dq_ref[0] = (carry[2] * scale).astype(BF16)

        @pl.when(qi == nq - 1)
        def _():
            dk_ref[0] = dk_acc[...].astype(BF16)
            dv_ref[0] = dv_acc[...].astype(BF16)

    qblk = lambda b, g, i: (g // per_unit, b * nq + i, g % per_unit)
    kv_out = pl.BlockSpec((1, seq, ATT_LANES), lambda b, g, i: (g // per_unit, b, g % per_unit))
    shp = jax.ShapeDtypeStruct((upp, t, 2 * LANES), BF16)
    return pl.pallas_call(
        body, name="attn_bwd", grid=(n_seq, groups, nq),
        in_specs=[pl.BlockSpec((1, bq, ATT_LANES), qblk),
                  pl.BlockSpec((1, seq, ATT_LANES), lambda b, g, i: (upp + g // per_unit, b, g % per_unit)),
                  pl.BlockSpec((1, seq, ATT_LANES), lambda b, g, i: (2 * upp + g // per_unit, b, g % per_unit)),
                  pl.BlockSpec((1, bq, ATT_LANES), qblk),
                  pl.BlockSpec((bq, ATT_LANES), lambda b, g, i: (b * nq + i, g)),
                  pl.BlockSpec((2 * KBLK, 2 * KBLK), lambda b, g, i: (0, 0)),
                  pl.BlockSpec(memory_space=pltpu.SMEM)],
        out_specs=[pl.BlockSpec((1, bq, ATT_LANES), qblk), kv_out, kv_out],
        out_shape=[shp, shp, shp],
        scratch_shapes=[pltpu.VMEM((seq, ATT_LANES), F32), pltpu.VMEM((seq, ATT_LANES), F32)],
        compiler_params=_cparams(("parallel", "parallel", "arbitrary")),
    )(qkv, qkv, qkv, do, tot, prefix_m, cnt)


def _ln_stats(v):
    mu = jnp.mean(v, axis=-1, keepdims=True)
    vc = v - mu
    rstd = lax.rsqrt(jnp.mean(vc * vc, axis=-1, keepdims=True) + EPS)
    return vc * rstd, rstd


def _glu_into(a0_ref, av_ref, ag_ref, hv_ref, hg_ref, first):
    hv = hv_ref[0].astype(F32)
    hg = hg_ref[0].astype(F32)
    a0_ref[0:HALO, :] = jnp.where(first, 0.0, hv * _sigmoid(hg))
    av = av_ref[0].astype(F32)
    ag = ag_ref[0].astype(F32)
    a0_ref[HALO:, :] = av * _sigmoid(ag)


def _shifted_taps(ref, shifted_ref, tm, first):
    taps = []
    for b in range(8):
        offs = [o for o in range(first, first + CONV_WIDTH) if o % 8 == b]
        n_rows = max(offs) - b + tm
        shifted_ref[b, 0:n_rows, :] = ref[pl.ds(b, n_rows), :]
        taps += [(b, o - b, o - first) for o in offs]
    return taps


def _tril_mask():
    r = lax.broadcasted_iota(jnp.int32, (CHUNK, CHUNK), 0)
    c = lax.broadcasted_iota(jnp.int32, (CHUNK, CHUNK), 1)
    return c <= r


def mix_fwd(z, conv_w, conv_b, ln_a_g, ln_a_b, ln_v_g, ln_v_b, sp_w, sp_bt, seq):
    _, t, c = z.shape
    tm = _tile(seq, 512)
    tiles_per_seq = seq // tm
    groups = c // LANES
    hb = tm // HALO

    def body(av_ref, ag_ref, u_ref, v_ref, hv_ref, hg_ref, cw_ref, cb_ref, lag_ref, lab_ref, lvg_ref, lvb_ref,
             spw_ref, spb_ref, cat_ref, a1_ref, a0_ref, sh_ref):
        i = pl.program_id(0)
        _glu_into(a0_ref, av_ref, ag_ref, hv_ref, hg_ref, i % tiles_per_seq == 0)
        acc = jnp.zeros((tm, c), F32) + cb_ref[...]
        for b, ro, k in _shifted_taps(a0_ref, sh_ref, tm, HALO - (CONV_WIDTH - 1)):
            acc = acc + cw_ref[k:k + 1, :] * sh_ref[b, pl.ds(ro, tm), :]
        a1_ref[...] = acc
        xh, _ = _ln_stats(acc)
        a2 = xh * lag_ref[...] + lab_ref[...]
        a3 = (a2 * _sigmoid(a2)).astype(BF16)
        half = c // 2
        cat_ref[0] = a3[:, :half]
        cat_ref[1] = a3[:, half:]
        tril = _tril_mask()
        for g in range(groups):
            sl = slice(g * LANES, (g + 1) * LANES)
            xh, _ = _ln_stats(v_ref[0][:, sl].astype(F32))
            vn = (xh * lvg_ref[:, sl] + lvb_ref[:, sl]).astype(BF16)
            w = jnp.where(tril, spw_ref[g], 0.0).astype(BF16)
            bias = spb_ref[:, g:g + 1]
            for ch in range(tm // CHUNK):
                rows = slice(ch * CHUNK, (ch + 1) * CHUNK)
                vs = _nn(w, vn[rows]) + bias
                bo = (u_ref[0][rows, sl].astype(F32) * vs).astype(BF16)
                cat_ref[2 + (g * LANES) // half, rows, (g * LANES) % half:(g * LANES) % half + LANES] = bo

    unit = lambda u: pl.BlockSpec((1, tm, c), lambda i: (u, i, 0))
    halo = lambda u: pl.BlockSpec((1, HALO, c), lambda i: (u, jnp.maximum(i * hb - 1, 0), 0))
    vec = pl.BlockSpec((1, c), lambda i: (0, 0))
    return pl.pallas_call(
        body, name="mix_fwd", grid=(t // tm,),
        in_specs=[unit(0), unit(1), unit(2), unit(3), halo(0), halo(1),
                  pl.BlockSpec((CONV_WIDTH, c), lambda i: (0, 0)), vec, vec, vec, vec, vec,
                  pl.BlockSpec((groups, CHUNK, CHUNK), lambda i: (0, 0, 0)),
                  pl.BlockSpec((CHUNK, groups), lambda i: (0, 0))],
        out_specs=[pl.BlockSpec((4, tm, c // 2), lambda i: (0, i, 0)), pl.BlockSpec((tm, c), lambda i: (i, 0))],
        out_shape=[jax.ShapeDtypeStruct((4, t, c // 2), BF16), jax.ShapeDtypeStruct((t, c), F32)],
        scratch_shapes=[pltpu.VMEM((HALO + tm, c), F32), pltpu.VMEM((8, HALO + tm, c), F32)],
        compiler_params=_cparams(("parallel",)),
    )(z, z, z, z, z, z, conv_w, conv_b, ln_a_g, ln_a_b, ln_v_g, ln_v_b, sp_w, sp_bt)


def mix_bwd_point(dcat, z, a1, ln_a_g, ln_a_b, ln_v_g, ln_v_b, sp_w, sp_wt, sp_bt, seq):
    _, t, c = z.shape
    tm = _tile(seq, 512)
    groups = c // LANES
    half = c // 2

    def body(dc_ref, u_ref, v_ref, a1_ref, lag_ref, lab_ref, lvg_ref, lvb_ref, spw_ref, spwt_ref, spb_ref,
             dz_ref, da1_ref, dcb_ref, dlag_ref, dlab_ref, dlvg_ref, dlvb_ref, dspw_ref, dspb_ref):
        i = pl.program_id(0)
        last = pl.num_programs(0) - 1

        @pl.when(i == 0)
        def _():
            for r in (dcb_ref, dlag_ref, dlab_ref, dlvg_ref, dlvb_ref, dspw_ref, dspb_ref):
                r[...] = jnp.zeros_like(r)

        da3 = jnp.concatenate([dc_ref[0], dc_ref[1]], axis=-1)
        xh, rstd = _ln_stats(a1_ref[...])
        a2 = xh * lag_ref[...] + lab_ref[...]
        s = _sigmoid(a2)
        da2 = da3 * (s * (1.0 + a2 * (1.0 - s)))
        dlag_ref[...] += jnp.sum(da2 * xh, axis=0, keepdims=True)
        dlab_ref[...] += jnp.sum(da2, axis=0, keepdims=True)
        dxh = da2 * lag_ref[...]
        da1 = rstd * (dxh - jnp.mean(dxh, axis=-1, keepdims=True) - xh * jnp.mean(dxh * xh, axis=-1, keepdims=True))
        da1_ref[...] = da1
        dcb_ref[...] += jnp.sum(da1, axis=0, keepdims=True)

        tril = _tril_mask()
        for g in range(groups):
            sl = slice(g * LANES, (g + 1) * LANES)
            xh, rstd = _ln_stats(v_ref[0][:, sl].astype(F32))
            lg = lvg_ref[:, sl]
            vnb = (xh * lg + lvb_ref[:, sl]).astype(BF16)
            w = jnp.where(tril, spw_ref[g], 0.0).astype(BF16)
            wt = jnp.where(tril.T, spwt_ref[g], 0.0).astype(BF16)
            bias = spb_ref[:, g:g + 1]
            dbo_all = dc_ref[2 + (g * LANES) // half][:, (g * LANES) % half:(g * LANES) % half + LANES]
            dvn_parts = []
            dw_acc = jnp.zeros((CHUNK, CHUNK), F32)
            db_acc = jnp.zeros((CHUNK, LANES), F32)
            for ch in range(tm // CHUNK):
                rows = slice(ch * CHUNK, (ch + 1) * CHUNK)
                vs = _nn(w, vnb[rows]) + bias
                dbo = dbo_all[rows]
                uv = u_ref[0][rows, sl].astype(F32)
                dz_ref[0, rows, sl] = (dbo * vs).astype(BF16)
                dvs = dbo * uv
                dvsb = dvs.astype(BF16)
                dvn_parts.append(_nn(wt, dvsb))
                dw_acc = dw_acc + _nt(dvsb, vnb[rows])
                db_acc = db_acc + dvs
            dvn = jnp.concatenate(dvn_parts, axis=0)
            dspw_ref[g] += jnp.where(tril, dw_acc, 0.0)
            dspb_ref[g] += db_acc
            dlvg_ref[:, sl] += jnp.sum(dvn * xh, axis=0, keepdims=True)
            dlvb_ref[:, sl] += jnp.sum(dvn, axis=0, keepdims=True)
            dxh = dvn * lg
            dv = rstd * (dxh - jnp.mean(dxh, axis=-1, keepdims=True) - xh * jnp.mean(dxh * xh, axis=-1, keepdims=True))
            dz_ref[1, :, sl] = dv.astype(BF16)

        @pl.when(i == last)
        def _():
            for g in range(groups):
                dspb_ref[g] = jnp.zeros((CHUNK, LANES), F32) + jnp.sum(dspb_ref[g], axis=-1, keepdims=True)

    unit = lambda u: pl.BlockSpec((1, tm, c), lambda i: (u, i, 0))
    vec = pl.BlockSpec((1, c), lambda i: (0, 0))
    sq = pl.BlockSpec((groups, CHUNK, CHUNK), lambda i: (0, 0, 0))
    vshape = jax.ShapeDtypeStruct((1, c), F32)
    sshape = jax.ShapeDtypeStruct((groups, CHUNK, CHUNK), F32)
    return pl.pallas_call(
        body, name="mix_bwd_point", grid=(t // tm,),
        in_specs=[pl.BlockSpec((4, tm, half), lambda i: (0, i, 0)), unit(2), unit(3),
                  pl.BlockSpec((tm, c), lambda i: (i, 0)), vec, vec, vec, vec, sq, sq,
                  pl.BlockSpec((CHUNK, groups), lambda i: (0, 0))],
        out_specs=[pl.BlockSpec((2, tm, c), lambda i: (1, i, 0)), pl.BlockSpec((tm, c), lambda i: (i, 0)),
                   vec, vec, vec, vec, vec, sq, sq],
        out_shape=[jax.ShapeDtypeStruct((4, t, c), BF16), jax.ShapeDtypeStruct((t, c), F32),
                   vshape, vshape, vshape, vshape, vshape, sshape, sshape],
        compiler_params=_cparams(("arbitrary",)),
    )(dcat, z, z, a1, ln_a_g, ln_a_b, ln_v_g, ln_v_b, sp_w, sp_wt, sp_bt)


def mix_bwd_conv(dz, da1, z, conv_w, seq):
    _, t, c = z.shape
    tm = _tile(seq, 512)
    tiles_per_seq = seq // tm
    hb = tm // HALO
    n_halo_blocks = t // HALO

    rc = _tile(tm, CONV_ROWS)

    def body(dz_in_ref, d_ref, dh_ref, av_ref, ag_ref, cw_ref, dz_ref, dcw_ref, d1_ref, sh_ref, part_ref):
        del dz_in_ref
        i = pl.program_id(0)

        @pl.when(i == 0)
        def _():
            part_ref[...] = jnp.zeros_like(part_ref)

        d1_ref[0:tm, :] = d_ref[...]
        d1_ref[tm:, :] = jnp.where((i + 1) % tiles_per_seq == 0, 0.0, dh_ref[...])
        taps = _shifted_taps(d1_ref, sh_ref, tm, 0)

        def chunk(ci, carry):
            r0 = pl.multiple_of(ci * rc, rc)
            av = av_ref[0, pl.ds(r0, rc), :].astype(F32)
            s = _sigmoid(ag_ref[0, pl.ds(r0, rc), :].astype(F32))
            a0 = av * s
            da0 = jnp.zeros((rc, c), F32)
            for b, ro, back in taps:
                k = CONV_WIDTH - 1 - back
                rows = sh_ref[b, pl.ds(r0 + ro, rc), :]
                da0 = da0 + cw_ref[k:k + 1, :] * rows
                prod = a0 * rows
                part_ref[k] += functools.reduce(lambda p, q: p + q, [prod[8 * r:8 * r + 8] for r in range(rc // 8)])
            dz_ref[0, pl.ds(r0, rc), :] = (da0 * s).astype(BF16)
            dz_ref[1, pl.ds(r0, rc), :] = (da0 * av * s * (1.0 - s)).astype(BF16)
            return carry

        lax.fori_loop(0, tm // rc, chunk, 0)

        @pl.when(i == pl.num_programs(0) - 1)
        def _():
            dcw_ref[...] = jnp.sum(part_ref[...], axis=1)

    unit = lambda u: pl.BlockSpec((1, tm, c), lambda i: (u, i, 0))
    return pl.pallas_call(
        body, name="mix_bwd_conv", grid=(t // tm,),
        in_specs=[pl.BlockSpec(memory_space=pl.ANY), pl.BlockSpec((tm, c), lambda i: (i, 0)),
                  pl.BlockSpec((HALO, c), lambda i: (jnp.minimum((i + 1) * hb, n_halo_blocks - 1), 0)),
                  unit(0), unit(1), pl.BlockSpec((CONV_WIDTH, c), lambda i: (0, 0))],
        out_specs=[pl.BlockSpec((2, tm, c), lambda i: (0, i, 0)), pl.BlockSpec((CONV_WIDTH, c), lambda i: (0, 0))],
        out_shape=[jax.ShapeDtypeStruct(dz.shape, BF16), jax.ShapeDtypeStruct((CONV_WIDTH, c), F32)],
        scratch_shapes=[pltpu.VMEM((tm + HALO, c), F32), pltpu.VMEM((8, tm + HALO, c), F32),
                        pltpu.VMEM((CONV_WIDTH, 8, c), F32)],
        input_output_aliases={0: 0},
        compiler_params=_cparams(("arbitrary",)),
    )(dz, da1, da1, z, z, conv_w)


CHIP_FLIPS = ((1, 0), (0, 1), (1, 1))
ANY = pl.BlockSpec(memory_space=pl.ANY)


def _place():
    return lax.axis_index("x"), lax.axis_index("y"), lax.axis_index("c")


def _flip(v, f):
    return 1 - v if f else v


def place_shard(w, chip, dtype, name):
    n_layers, r, cc = w.shape
    rb = _tile(r, 512)

    def body(chip_ref, w_ref, *o_refs):
        del chip_ref
        for layer, o_ref in enumerate(o_refs):
            o_ref[0] = w_ref[layer].astype(dtype)

    return pl.pallas_call(
        body, name=name,
        grid_spec=pltpu.PrefetchScalarGridSpec(
            num_scalar_prefetch=1, grid=(r // rb,),
            in_specs=[pl.BlockSpec((n_layers, rb, cc), lambda i, chip_ref: (0, i, 0))],
            out_specs=[pl.BlockSpec((1, rb, cc), lambda i, chip_ref: (chip_ref[0], i, 0))] * n_layers),
        out_shape=[jax.ShapeDtypeStruct((N_CHIPS, r, cc), dtype)] * n_layers,
        compiler_params=_cparams(("parallel",)),
    )(chip, w)


class Carry:
    def __init__(self, arrays, out_shapes, aliased, sem_shapes, start, finish):
        self.arrays, self.out_shapes, self.aliased, self.sem_shapes = list(arrays), list(out_shapes), aliased, list(sem_shapes)
        self.start, self.finish = start, finish


def _call(body, *, name, grid, in_specs, out_specs, out_shape, args, sem, scratch_shapes=(), carry=None):
    if carry is None:
        res = pl.pallas_call(body, name=name, grid=grid, in_specs=in_specs, out_specs=out_specs, out_shape=out_shape,
                             scratch_shapes=list(scratch_shapes), compiler_params=_cparams(sem))(*args)
        return list(res), []
    n_in, n_out, n_scr, nc = len(args), len(out_shape), len(scratch_shapes), len(carry.arrays)

    def full_body(*refs):
        ins, refs = refs[:n_in], refs[n_in:]
        c_ins, refs = refs[:nc], refs[nc:]
        outs, refs = refs[:n_out], refs[n_out:]
        c_outs, refs = refs[:nc], refs[nc:]
        scr, sems = refs[:n_scr], refs[n_scr:]
        first = functools.reduce(lambda a, b: a & b, [pl.program_id(d) == 0 for d in range(len(grid))])
        last = functools.reduce(lambda a, b: a & b, [pl.program_id(d) == grid[d] - 1 for d in range(len(grid))])

        @pl.when(first)
        def _():
            carry.start(c_ins, c_outs, sems)

        body(*ins, *outs, *scr)

        @pl.when(last)
        def _():
            carry.finish(c_ins, c_outs, sems)

    res = pl.pallas_call(
        full_body, name=name, grid=grid, in_specs=list(in_specs) + [ANY] * nc, out_specs=list(out_specs) + [ANY] * nc,
        out_shape=list(out_shape) + carry.out_shapes, scratch_shapes=list(scratch_shapes) + carry.sem_shapes,
        input_output_aliases={n_in + i: n_out + i for i in range(nc)} if carry.aliased else {},
        compiler_params=pltpu.CompilerParams(dimension_semantics=("arbitrary",) * len(grid), vmem_limit_bytes=VMEM_LIMIT,
                                             has_side_effects=True),
    )(*args, *carry.arrays)
    return list(res[:n_out]), list(res[n_out:])


def _gather_ops(shapes, whole):
    n = len(shapes)

    def rows(a, c):
        hr = shapes[a][1] // 2
        return pl.ds(pl.multiple_of(c * hr, 16), hr)

    def start(ins, outs, sems):
        ici_send, ici_recv = sems[0], sems[1]
        x, y, c = _place()
        k = 2 * x + y
        for a in range(n):
            for o, (fx, fy) in enumerate(CHIP_FLIPS):
                src = ins[a].at[k] if whole[a] else ins[a].at[k, rows(a, c)]
                dst = outs[a].at[k] if whole[a] else outs[a].at[k, rows(a, c)]
                pltpu.make_async_remote_copy(
                    src_ref=src, dst_ref=dst, send_sem=ici_send.at[3 * a + o], recv_sem=ici_recv.at[3 * a + o],
                    device_id=(_flip(x, fx), _flip(y, fy), c), device_id_type=MESH).start()

    def finish(ins, outs, sems):
        ici_send, ici_recv, d2d_send, d2d_recv = sems
        x, y, c = _place()
        k = 2 * x + y
        sibling = (x, y, 1 - c)

        def copy(ref, send, recv, a, o):
            return pltpu.make_async_remote_copy(src_ref=ref, dst_ref=ref, send_sem=send.at[3 * a + o],
                                                recv_sem=recv.at[3 * a + o], device_id=sibling, device_id_type=MESH)

        for a in range(n):
            for o, (fx, fy) in enumerate(CHIP_FLIPS):
                kk = 2 * _flip(x, fx) + _flip(y, fy)
                landed = outs[a].at[kk] if whole[a] else outs[a].at[kk, rows(a, c)]
                copy(landed, ici_send, ici_recv, a, o).wait_recv()
                if not whole[a]:
                    copy(landed, d2d_send, d2d_recv, a, o).start()
        for a in range(n):
            for o, (fx, fy) in enumerate(CHIP_FLIPS):
                kk = 2 * _flip(x, fx) + _flip(y, fy)
                mine = ins[a].at[k] if whole[a] else ins[a].at[k, rows(a, c)]
                copy(mine, ici_send, ici_recv, a, o).wait_send()
                if not whole[a]:
                    copy(outs[a].at[kk, rows(a, 1 - c)], d2d_send, d2d_recv, a, o).wait_recv()
                    copy(outs[a].at[kk, rows(a, c)], d2d_send, d2d_recv, a, o).wait_send()

    dma = pltpu.SemaphoreType.DMA
    return start, finish, [dma((3 * n,))] * 4


def gather_carry(bufs):
    start, finish, sems = _gather_ops([b.shape for b in bufs], [False] * len(bufs))
    return Carry(bufs, [jax.ShapeDtypeStruct(b.shape, b.dtype) for b in bufs], True, sems, start, finish)


def allgather_weights(shards, smalls):
    bufs = list(shards) + list(smalls)
    n = len(bufs)
    start, finish, sems = _gather_ops([b.shape for b in bufs], [False] * len(shards) + [True] * len(smalls))

    def body(*refs):
        start(refs[:n], refs[n:2 * n], refs[2 * n:])
        finish(refs[:n], refs[n:2 * n], refs[2 * n:])

    res = pl.pallas_call(
        body, name="allgather_weights", in_specs=[ANY] * n, out_specs=[ANY] * n,
        out_shape=[jax.ShapeDtypeStruct(b.shape, b.dtype) for b in bufs], scratch_shapes=sems,
        input_output_aliases={i: i for i in range(n)},
        compiler_params=pltpu.CompilerParams(has_side_effects=True),
    )(*bufs)
    return res[:len(shards)], res[len(shards):]


def rs_exchange(grads):
    n = len(grads)

    def body(*refs):
        ins, outs = refs[:n], refs[n:2 * n]
        send, recv = refs[2 * n:]
        x, y, c = _place()
        cps = []
        for a in range(n):
            cp = pltpu.make_async_remote_copy(
                src_ref=ins[a].at[:, 1 - c], dst_ref=outs[a], send_sem=send.at[a], recv_sem=recv.at[a],
                device_id=(x, y, 1 - c), device_id_type=MESH)
            cp.start()
            cps.append(cp)
        for cp in cps:
            cp.wait()

    dma = pltpu.SemaphoreType.DMA
    return pl.pallas_call(
        body, name="rs_exchange", in_specs=[ANY] * n, out_specs=[ANY] * n,
        out_shape=[jax.ShapeDtypeStruct((g.shape[0],) + g.shape[2:], g.dtype) for g in grads],
        scratch_shapes=[dma((n,)), dma((n,))],
        compiler_params=pltpu.CompilerParams(has_side_effects=True),
    )(*grads)


def rs_add(gs, sibs, core, out_dtype, name):
    n = len(gs)
    nk = gs[0].shape[0]

    def body(core_ref, *refs):
        del core_ref
        for a in range(n):
            refs[2 * n + a][0] = (refs[a][0, 0] + refs[n + a][0]).astype(out_dtype)

    halves = [g.shape[2:] for g in gs]
    return pl.pallas_call(
        body, name=name,
        grid_spec=pltpu.PrefetchScalarGridSpec(
            num_scalar_prefetch=1, grid=(nk,),
            in_specs=[pl.BlockSpec((1, 1) + h, lambda k, core_ref: (k, core_ref[0], 0, 0)) for h in halves]
            + [pl.BlockSpec((1,) + h, lambda k, core_ref: (k, 0, 0)) for h in halves],
            out_specs=[pl.BlockSpec((1,) + h, lambda k, core_ref: (k, 0, 0)) for h in halves]),
        out_shape=[jax.ShapeDtypeStruct((nk,) + h, out_dtype) for h in halves],
        compiler_params=_cparams(("parallel",)),
    )(core, *gs, *sibs)


def send_carry(parts):
    n = len(parts)

    def copies(ins, outs, sems):
        x, y, c = _place()
        for a in range(n):
            for o, (fx, fy) in enumerate(CHIP_FLIPS):
                kk = 2 * _flip(x, fx) + _flip(y, fy)
                yield pltpu.make_async_remote_copy(
                    src_ref=ins[a].at[kk], dst_ref=outs[a].at[o], send_sem=sems[0].at[3 * a + o],
                    recv_sem=sems[1].at[3 * a + o], device_id=(_flip(x, fx), _flip(y, fy), c), device_id_type=MESH)

    def start(ins, outs, sems):
        for cp in copies(ins, outs, sems):
            cp.start()

    def finish(ins, outs, sems):
        for cp in copies(ins, outs, sems):
            cp.wait()

    dma = pltpu.SemaphoreType.DMA
    return Carry(parts, [jax.ShapeDtypeStruct((3,) + p.shape[1:], p.dtype) for p in parts], False,
                 [dma((3 * n,)), dma((3 * n,))], start, finish)


def rs_sum(recvs, parts, where, name):
    n_layers = len(recvs)
    _, hr, cc = recvs[0].shape
    rb = _tile(hr, 256)

    def body(where_ref, *refs):
        del where_ref
        o_ref = refs[-1]
        for layer in range(n_layers):
            r_ref, p_ref = refs[layer], refs[n_layers + layer]
            o_ref[layer, 0] = ((p_ref[0].astype(F32) + r_ref[0].astype(F32)) + r_ref[1].astype(F32)) + r_ref[2].astype(F32)

    return pl.pallas_call(
        body, name=name,
        grid_spec=pltpu.PrefetchScalarGridSpec(
            num_scalar_prefetch=1, grid=(hr // rb,),
            in_specs=[pl.BlockSpec((3, rb, cc), lambda i, w_ref: (0, i, 0))] * n_layers
            + [pl.BlockSpec((1, rb, cc), lambda i, w_ref: (w_ref[0], i, 0))] * n_layers,
            out_specs=pl.BlockSpec((n_layers, 1, rb, cc), lambda i, w_ref: (0, w_ref[1], i, 0))),
        out_shape=jax.ShapeDtypeStruct((n_layers, 2, hr, cc), F32),
        compiler_params=_cparams(("parallel",)),
    )(where, *recvs, *parts)


def rs_share(fulls):
    n = len(fulls)

    def body(*refs):
        ins, outs = refs[:n], refs[n:2 * n]
        send, recv = refs[2 * n:]
        x, y, c = _place()
        cps = []
        for a in range(n):
            cp = pltpu.make_async_remote_copy(
                src_ref=ins[a].at[:, c], dst_ref=outs[a].at[:, c], send_sem=send.at[a], recv_sem=recv.at[a],
                device_id=(x, y, 1 - c), device_id_type=MESH)
            cp.start()
            cps.append(cp)
        for a in range(n):
            got = outs[a].at[:, 1 - c]
            pltpu.make_async_remote_copy(
                src_ref=got, dst_ref=got, send_sem=send.at[a], recv_sem=recv.at[a],
                device_id=(x, y, 1 - c), device_id_type=MESH).wait_recv()
        for cp in cps:
            cp.wait_send()

    dma = pltpu.SemaphoreType.DMA
    return pl.pallas_call(
        body, name="rs_share", in_specs=[ANY] * n, out_specs=[ANY] * n,
        out_shape=[jax.ShapeDtypeStruct(f.shape, f.dtype) for f in fulls],
        scratch_shapes=[dma((n,)), dma((n,))],
        input_output_aliases={i: i for i in range(n)},
        compiler_params=pltpu.CompilerParams(has_side_effects=True),
    )(*fulls)


def allreduce_small(v):
    r, w = v.shape

    def body(v_ref, o_ref, buf, send, recv, loc):
        x, y, c = _place()
        me = 4 * x + 2 * y + c
        mine = pltpu.make_async_copy(v_ref, buf.at[me], loc)
        mine.start()
        cps = []
        for o in range(1, N_DEV):
            fx, fy, fc = (o >> 2) & 1, (o >> 1) & 1, o & 1
            cp = pltpu.make_async_remote_copy(
                src_ref=v_ref, dst_ref=buf.at[me], send_sem=send.at[o - 1], recv_sem=recv.at[o - 1],
                device_id=(_flip(x, fx), _flip(y, fy), _flip(c, fc)), device_id_type=MESH)
            cp.start()
            cps.append(cp)
        for o in range(1, N_DEV):
            fx, fy, fc = (o >> 2) & 1, (o >> 1) & 1, o & 1
            peer = 4 * _flip(x, fx) + 2 * _flip(y, fy) + _flip(c, fc)
            pltpu.make_async_remote_copy(
                src_ref=v_ref, dst_ref=buf.at[peer], send_sem=send.at[o - 1], recv_sem=recv.at[o - 1],
                device_id=(x, y, c), device_id_type=MESH).wait_recv()
        for cp in cps:
            cp.wait_send()
        mine.wait()
        acc = buf[0]
        for d in range(1, N_DEV):
            acc = acc + buf[d]
        o_ref[...] = acc

    dma = pltpu.SemaphoreType.DMA
    vm = pl.BlockSpec(memory_space=pltpu.VMEM)
    return pl.pallas_call(
        body, name="allreduce_small", in_specs=[vm], out_specs=vm,
        out_shape=jax.ShapeDtypeStruct((r, w), F32),
        scratch_shapes=[pltpu.VMEM((N_DEV, r, w), F32), dma((N_DEV - 1,)), dma((N_DEV - 1,)), dma],
        compiler_params=pltpu.CompilerParams(has_side_effects=True, vmem_limit_bytes=VMEM_LIMIT),
    )(v)


def adamw(w, g, m, v, name):
    r, cc = w.shape
    rb = _tile(r, 256)

    def body(w_ref, g_ref, m_ref, v_ref, d_ref, nm_ref, nv_ref):
        gv = g_ref[...]
        nm = ADAM_B1 * m_ref[...] + (1.0 - ADAM_B1) * gv
        nv = ADAM_B2 * v_ref[...] + (1.0 - ADAM_B2) * (gv * gv)
        m_hat = nm / (1.0 - ADAM_B1 ** ADAM_STEP)
        v_hat = nv / (1.0 - ADAM_B2 ** ADAM_STEP)
        d_ref[...] = -ADAM_LR * (m_hat / (jnp.sqrt(v_hat) + ADAM_EPS) + ADAM_WD * w_ref[...])
        nm_ref[...] = nm
        nv_ref[...] = nv

    blk = pl.BlockSpec((rb, cc), lambda i: (i, 0))
    shp = jax.ShapeDtypeStruct((r, cc), F32)
    return pl.pallas_call(
        body, name=name, grid=(r // rb,), in_specs=[blk] * 4, out_specs=[blk] * 3, out_shape=[shp] * 3,
        compiler_params=_cparams(("parallel",)),
    )(w, g, m, v)


WEIGHTS = ['g_ffn1', 'w_ffn1_gate', 'w_ffn1_up', 'w_ffn1_down', 'g_mix', 'w_in_ab', 'conv_w', 'conv_b', 'ln_a_g',
           'ln_a_b', 'ln_v_g', 'ln_v_b', 'sp_w', 'sp_b', 'w_out_ab', 'w_qkv', 'w_o', 'g_ffn2', 'w_ffn2_gate',
           'w_ffn2_up', 'w_ffn2_down', 'g_final']
BIG = ['w_ffn1_gate', 'w_ffn1_up', 'w_ffn1_down', 'w_in_ab', 'w_out_ab', 'w_qkv', 'w_o', 'w_ffn2_gate', 'w_ffn2_up',
       'w_ffn2_down']
SMALL = ['g_ffn1', 'g_mix', 'g_ffn2', 'g_final', 'conv_b', 'ln_a_g', 'ln_a_b', 'ln_v_g', 'ln_v_b', 'sp_b', 'sp_w']


CARRY_WEIGHTS = {"ffn_gateup": 9.2e6, "ffn_down": 6.1e6, "mm_in": 5.9e6, "mm_out": 3.3e6}


def _use_order(depth):
    order = []
    for layer in range(depth):
        order += [('w_ffn1_gate', layer), ('w_ffn1_up', layer), ('w_ffn1_down', layer)]
        order += [('w_in_ab', layer // 2), ('w_out_ab', layer // 2)] if layer % 2 == 0 else [('w_qkv', layer // 2), ('w_o', layer // 2)]
        order += [('w_ffn2_gate', layer), ('w_ffn2_up', layer), ('w_ffn2_down', layer)]
    return order


def _rows(a):
    return a.reshape(-1, LANES)


def _pack(parts):
    v = jnp.concatenate([_rows(p) for p in parts], axis=0)
    pad = (-v.shape[0]) % 8
    return jnp.pad(v, ((0, pad), (0, 0)))


def _unpack(v, shapes):
    out, r = [], 0
    for s in shapes:
        n = 1
        for d in s:
            n *= d
        n //= LANES
        out.append(v[r:r + n].reshape(s))
        r += n
    return out


def kernel(x, g_ffn1, w_ffn1_gate, w_ffn1_up, w_ffn1_down, g_mix, w_in_ab, conv_w, conv_b, ln_a_g, ln_a_b, ln_v_g, ln_v_b, sp_w, sp_b, w_out_ab, w_qkv, w_o, g_ffn2, w_ffn2_gate, w_ffn2_up, w_ffn2_down, g_final, loss_target, m_g_ffn1, m_w_ffn1_gate, m_w_ffn1_up, m_w_ffn1_down, m_g_mix, m_w_in_ab, m_conv_w, m_conv_b, m_ln_a_g, m_ln_a_b, m_ln_v_g, m_ln_v_b, m_sp_w, m_sp_b, m_w_out_ab, m_w_qkv, m_w_o, m_g_ffn2, m_w_ffn2_gate, m_w_ffn2_up, m_w_ffn2_down, m_g_final, v_g_ffn1, v_w_ffn1_gate, v_w_ffn1_up, v_w_ffn1_down, v_g_mix, v_w_in_ab, v_conv_w, v_conv_b, v_ln_a_g, v_ln_a_b, v_ln_v_g, v_ln_v_b, v_sp_w, v_sp_b, v_w_out_ab, v_w_qkv, v_w_o, v_g_ffn2, v_w_ffn2_gate, v_w_ffn2_up, v_w_ffn2_down, v_g_final):
    p = dict(locals())
    n_seq, seq, d = x.shape
    t = n_seq * seq
    depth = g_ffn1.shape[0]
    core = lax.axis_index("c")
    chip = 2 * lax.axis_index("x") + lax.axis_index("y")
    xf = x.reshape(t, d)
    target = loss_target.reshape(t, d)

    items = []
    for name in BIG:
        for layer in range(p[name].shape[0]):
            items.append((name, layer))
    chip1 = chip.reshape(1).astype(jnp.int32)
    placed = {}
    for name in BIG:
        for layer, buf in enumerate(place_shard(p[name], chip1, BF16, "place_shard")):
            placed[(name, layer)] = buf
    first = [('w_ffn1_gate', 0), ('w_ffn1_up', 0)]
    gathered, (conv_w4,) = allgather_weights([placed[it] for it in first],
                                             place_shard(conv_w, chip1, F32, "place_conv_w"))
    wt = dict(zip(first, gathered))
    waiting = [it for it in _use_order(depth) if it not in wt]

    def riders(name):
        room, take = CARRY_WEIGHTS[name], []
        for it in list(waiting):
            if placed[it].size <= room:
                room -= placed[it].size
                take.append(it)
                waiting.remove(it)
        return (take, gather_carry([placed[it] for it in take])) if take else (take, None)

    def landed(take, carried):
        wt.update(zip(take, carried))

    def weight(it):
        if it not in wt:
            waiting.remove(it)
            (wt[it],), _ = allgather_weights([placed[it]], [])
        return wt[it]

    c_mix = conv_w4.shape[2] * N_CHIPS
    conv_full = jnp.transpose(conv_w4, (1, 0, 2)).reshape(CONV_WIDTH, c_mix)
    vec = lambda a: a.reshape(1, -1)
    sp_bt = sp_b[0].T
    sp_wt = jnp.transpose(sp_w[0], (0, 2, 1))
    d_ff = w_ffn1_gate.shape[2]
    n_in = w_in_ab.shape[2]
    n_qkv = w_qkv.shape[2] // 3

    saved = []
    xc = xf
    h = rmsnorm_fwd(xc, vec(g_ffn1[0]), "norm_first")
    for layer in range(depth):
        s = {}
        for half, (gn, wn) in enumerate((('g_ffn1', 'w_ffn1'), ('g_ffn2', 'w_ffn2'))):
            if half == 1:
                s['x_mix'], s['h_mix'] = xc, h
                if layer % 2 == 0:
                    w_in = weight(('w_in_ab', layer // 2))
                    take, carry = riders("mm_in")
                    (z,), got = colmm(h, [w_in], n_in, BF16, "mm_in", carry)
                    landed(take, got)
                    cat, a1 = mix_fwd(z, conv_full, conv_b, ln_a_g, ln_a_b, vec(ln_v_g), vec(ln_v_b), sp_w[0], sp_bt, seq)
                    s.update(z=z, cat=cat, a1=a1)
                    w_out = weight(('w_out_ab', layer // 2))
                    take, carry = riders("mm_out")
                    (xc, h), got = rowmm(cat, w_out, xc, 1.0, "mm_out", carry, vec(g_ffn2[layer]))
                    landed(take, got)
                else:
                    (qkv,), _ = colmm(h, [weight(('w_qkv', layer // 2))], n_qkv, BF16, "mm_qkv")
                    o, tot, cnt = attn_fwd(qkv, n_seq, seq)
                    s.update(qkv=qkv, o=o, tot=tot, cnt=cnt)
                    (xc, h), _ = rowmm(o, weight(('w_o', layer // 2)), xc, 1.0, "mm_o", None, vec(g_ffn2[layer]))
            s['x' + wn] = xc
            w_gate, w_up = weight((wn + '_gate', layer)), weight((wn + '_up', layer))
            take, carry = riders("ffn_gateup")
            (silu, udsilu, act), got = colmm(h, [w_gate, w_up], d_ff, BF16, "ffn_gateup", carry, swiglu=True)
            landed(take, got)
            s.update({'h' + wn: h, 'swiglu' + wn: (silu, udsilu), 'act' + wn: act})
            w_down = weight((wn + '_down', layer))
            take, carry = riders("ffn_down")
            following = g_mix[layer] if half == 0 else (g_ffn1[layer + 1] if layer + 1 < depth else None)
            (xc, h), got = rowmm(act, w_down, xc, 0.5, "ffn_down", carry, None if following is None else vec(following))
            landed(take, got)
        saved.append(s)

    loss8, dx, dxb, dg_final = loss_head(xc, vec(g_final), target)
    loss = lax.psum(loss8[0, 0], ("x", "y", "c"))

    gw = {}
    gs = {}
    core1 = core.reshape(1).astype(jnp.int32)
    ready = []
    part, recv = {}, {}

    def leaving():
        its = list(ready)
        ready.clear()
        halves = lambda a: a.reshape(N_CHIPS, 2, a.shape[1] // 2, a.shape[2])
        theirs = rs_exchange([halves(gw[it][1]) for it in its])
        sums = rs_add([halves(gw[it][0]) for it in its], theirs, core1, REDUCE_DTYPE, "rs_add")
        part.update(zip(its, sums))
        return its, send_carry(sums)

    for layer in reversed(range(depth)):
        s = saved[layer]
        for half, (gn, wn) in reversed(list(enumerate((('g_ffn1', 'w_ffn1'), ('g_ffn2', 'w_ffn2'))))):
            wd = wt[(wn + '_down', layer)]
            dgate, dup = rowmm_t(dxb, wd, 0.5, BF16, "ffn_bwd_act", swiglu=s['swiglu' + wn])
            gw[(wn + '_down', layer)] = dw_row(s['act' + wn], dxb, 0.5, "ffn_dw_down")
            gw[(wn + '_gate', layer)], gw[(wn + '_up', layer)] = dw_col(s['h' + wn], [dgate, dup], N_CHIPS, d_ff, "ffn_dw_gateup")
            ready.extend([(wn + '_down', layer), (wn + '_gate', layer), (wn + '_up', layer)])
            its, carry = leaving()
            (dx, dxb, dg), got = colmm_t([dgate, dup], [wt[(wn + '_gate', layer)], wt[(wn + '_up', layer)]], d_ff,
                                         s['x' + wn], vec(p[gn][layer]), dx, "ffn_bwd_in", carry)
            recv.update(zip(its, got))
            gs[(gn, layer)] = dg
            if half == 1:
                if layer % 2 == 0:
                    i = layer // 2
                    w_out = wt[('w_out_ab', i)]
                    dcat = rowmm_t(dxb, w_out, 1.0, F32, "mm_out_t")
                    gw[('w_out_ab', i)] = dw_row(s['cat'], dxb, 1.0, "dw_out")
                    dz, da1, dcb, dlag, dlab, dlvg, dlvb, dspw, dspb = mix_bwd_point(
                        dcat, s['z'], s['a1'], ln_a_g, ln_a_b, vec(ln_v_g), vec(ln_v_b), sp_w[0], sp_wt, sp_bt, seq)
                    dz, dcw = mix_bwd_conv(dz, da1, s['z'], conv_full, seq)
                    gs.update({('conv_b', i): dcb, ('ln_a_g', i): dlag, ('ln_a_b', i): dlab, ('ln_v_g', i): dlvg,
                               ('ln_v_b', i): dlvb, ('sp_w', i): dspw, ('sp_b', i): dspb[:, :, 0], ('conv_w', i): dcw})
                    (gw[('w_in_ab', i)],) = dw_col(s['h_mix'], [dz], N_CHIPS, n_in, "dw_in")
                    ready.extend([('w_out_ab', i), ('w_in_ab', i)])
                    its, carry = leaving()
                    (dx, dxb, dg), got = colmm_t([dz], [wt[('w_in_ab', i)]], n_in, s['x_mix'], vec(g_mix[layer]), dx,
                                                 "mm_in_t", carry)
                    recv.update(zip(its, got))
                else:
                    i = layer // 2
                    w_o4 = wt[('w_o', i)]
                    do = rowmm_t(dxb, w_o4, 1.0, BF16, "mm_o_t")
                    gw[('w_o', i)] = dw_row(s['o'], dxb, 1.0, "dw_o")
                    dq, dk, dv = attn_bwd(s['qkv'], do, s['tot'], s['cnt'], n_seq, seq)
                    dqkv = jnp.concatenate([dq, dk, dv], axis=0)
                    (gw[('w_qkv', i)],) = dw_col(s['h_mix'], [dqkv], N_CHIPS, n_qkv, "dw_qkv")
                    ready.extend([('w_o', i), ('w_qkv', i)])
                    its, carry = leaving()
                    (dx, dxb, dg), got = colmm_t([dqkv], [wt[('w_qkv', i)]], n_qkv, s['x_mix'], vec(g_mix[layer]), dx,
                                                 "mm_qkv_t", carry)
                    recv.update(zip(its, got))
                gs[('g_mix', layer)] = dg
    grad_x = dx.reshape(x.shape)

    assert not ready and set(recv) == set(items)
    where = jnp.stack([chip, core]).astype(jnp.int32)
    fulls = []
    for name in BIG:
        its = [(name, layer) for layer in range(p[name].shape[0])]
        fulls.append(rs_sum([recv[it] for it in its], [part[it] for it in its], where, "rs_sum"))
    shared = rs_share(fulls)
    grads = {name: sh.reshape(p[name].shape) for name, sh in zip(BIG, shared)}

    stack = lambda name: jnp.concatenate([gs[(name, layer)].reshape((1,) + p[name].shape[1:]) for layer in range(p[name].shape[0])], axis=0)
    small_g = [stack(name) if name != 'g_final' else dg_final.reshape(p[name].shape) for name in SMALL]
    packed = _pack(small_g + [gs[('conv_w', 0)]])
    red = allreduce_small(packed)
    outs = _unpack(red, [p[name].shape for name in SMALL] + [(CONV_WIDTH, c_mix)])
    for name, g in zip(SMALL, outs[:-1]):
        grads[name] = g
    conv_g = outs[-1].reshape(CONV_WIDTH, N_CHIPS, c_mix // N_CHIPS)
    grads['conv_w'] = lax.dynamic_index_in_dim(conv_g, chip, axis=1, keepdims=False).reshape(conv_w.shape)

    delta, new_m, new_v = {}, {}, {}
    for name in BIG:
        shp = p[name].shape
        two = lambda a: a.reshape(shp[0] * shp[1], shp[2])
        dl, nm, nv = adamw(two(p[name]), two(grads[name]), two(p['m_' + name]), two(p['v_' + name]), "adamw")
        delta[name], new_m[name], new_v[name] = dl.reshape(shp), nm.reshape(shp), nv.reshape(shp)
    small_names = SMALL + ['conv_w']
    pk = lambda pre: _pack([p[pre + name] for name in small_names])
    dl, nm, nv = adamw(pk(''), _pack([grads[name] for name in small_names]), pk('m_'), pk('v_'), "adamw_small")
    shapes = [p[name].shape for name in small_names]
    for dst, val in ((delta, dl), (new_m, nm), (new_v, nv)):
        for name, a in zip(small_names, _unpack(val, shapes)):
            dst[name] = a

    return (loss, grad_x, *[grads[n] for n in WEIGHTS], *[delta[n] for n in WEIGHTS],
            *[new_m[n] for n in WEIGHTS], *[new_v[n] for n in WEIGHTS])
```

```python
import functools

import jax
import jax.numpy as jnp
from jax import lax
from jax.experimental import pallas as pl
from jax.experimental.pallas import tpu as pltpu

F32 = jnp.float32
BF16 = jnp.bfloat16
EPS = 1e-6
HEAD_DIM = 64
CONV_WIDTH = 31
CHUNK = 128
KBLK = 128
ATT_BLOCK = 256
ATT_LANES = 256
DW_TOKENS = 2048
CONV_ROWS = 64
MASKED = -1e30
STICK_GONE = -110.0
LANES = 128
HALO = 32
ADAM_LR, ADAM_B1, ADAM_B2, ADAM_EPS, ADAM_WD, ADAM_STEP = 0.001, 0.9, 0.999, 1e-08, 0.01, 10
VMEM_LIMIT = 56 * 1024 * 1024
MESH = pl.DeviceIdType.MESH
N_CHIPS = 4
N_DEV = 8
REDUCE_DTYPE = BF16


def _cparams(sem):
    return pltpu.CompilerParams(dimension_semantics=sem, vmem_limit_bytes=VMEM_LIMIT)


def _nt(a, b):
    return lax.dot_general(a, b, (((1,), (1,)), ((), ())), preferred_element_type=F32)


def _tn(a, b):
    return lax.dot_general(a, b, (((0,), (0,)), ((), ())), preferred_element_type=F32)


def _nn(a, b):
    return jnp.dot(a, b, preferred_element_type=F32)


def _sigmoid(x):
    return 0.5 * jnp.tanh(0.5 * x) + 0.5


def _tile(t, want):
    if t <= want:
        return t
    for cand in range(want - want % 8, 7, -8):
        if t % cand == 0:
            return cand
    raise ValueError((t, want))


def rmsnorm_fwd(x, g, name):
    t, d = x.shape
    tm = _tile(t, 512)

    def body(x_ref, g_ref, h_ref):
        xv = x_ref[...]
        r = lax.rsqrt(jnp.mean(xv * xv, axis=-1, keepdims=True) + EPS)
        h_ref[...] = (xv * r * g_ref[...]).astype(BF16)

    return pl.pallas_call(
        body, name=name, grid=(t // tm,),
        in_specs=[pl.BlockSpec((tm, d), lambda i: (i, 0)), pl.BlockSpec((1, d), lambda i: (0, 0))],
        out_specs=pl.BlockSpec((tm, d), lambda i: (i, 0)),
        out_shape=jax.ShapeDtypeStruct((t, d), BF16),
        compiler_params=_cparams(("parallel",)),
    )(x, g)


def colmm(h, ws, nu, out_dtype, name, carry=None, swiglu=False):
    t, k = h.shape
    j, nj = (ws[0].shape[0], ws[0].shape[1]) if swiglu else (ws[0].shape[0], ws[0].shape[2])
    per = nj // nu
    units = j * per
    tm = _tile(t, 512)
    nw = len(ws)
    n_out = 3 if swiglu else nw

    def body(*refs):
        h_ref = refs[0]
        hv = h_ref[...]
        if swiglu:
            silu_ref, udsilu_ref, act_ref = refs[1 + nw:]
            gv = _nt(hv, refs[1][0])
            uv = _nt(hv, refs[2][0])
            s = _sigmoid(gv)
            silu = gv * s
            silu_ref[0] = silu.astype(out_dtype)
            udsilu_ref[0] = (uv * (s + silu * (1.0 - s))).astype(out_dtype)
            act_ref[0] = (silu * uv).astype(out_dtype)
            return
        for n in range(nw):
            res = _nn(hv, refs[1 + n][0]).astype(out_dtype)
            for u in range(per):
                refs[1 + nw + n][u] = res[:, u * nu:(u + 1) * nu]

    assert not swiglu or (nw == 2 and per == 1)
    w_spec = pl.BlockSpec((1, nj, k) if swiglu else (1, k, nj), lambda s, i: (s, 0, 0))
    o_spec = pl.BlockSpec((per, tm, nu), lambda s, i: (s, i, 0))
    return _call(
        body, name=name, grid=(j, t // tm),
        in_specs=[pl.BlockSpec((tm, k), lambda s, i: (i, 0))] + [w_spec] * nw,
        out_specs=[o_spec] * n_out,
        out_shape=[jax.ShapeDtypeStruct((units, t, nu), out_dtype)] * n_out,
        args=[h, *ws], sem=("parallel", "parallel"), carry=carry)


def rowmm(a, w, resid, scale, name, carry=None, norm_g=None):
    u_n, t, ku = a.shape
    n = w.shape[2]
    tm = _tile(t, 256)

    def body(a_ref, w_ref, r_ref, *rest):
        acc = jnp.zeros((tm, n), F32)
        for u in range(u_n):
            acc = acc + _nn(a_ref[u], w_ref[u])
        out = r_ref[...] + scale * acc
        if norm_g is None:
            (o_ref,) = rest
        else:
            g_ref, o_ref, h_ref = rest
            r = lax.rsqrt(jnp.mean(out * out, axis=-1, keepdims=True) + EPS)
            h_ref[...] = (out * r * g_ref[...]).astype(BF16)
        o_ref[...] = out

    row = pl.BlockSpec((tm, n), lambda i: (i, 0))
    normed = norm_g is not None
    outs, carried = _call(
        body, name=name, grid=(t // tm,),
        in_specs=[pl.BlockSpec((u_n, tm, ku), lambda i: (0, i, 0)), pl.BlockSpec((u_n, ku, n), lambda i: (0, 0, 0)),
                  row] + [pl.BlockSpec((1, n), lambda i: (0, 0))] * normed,
        out_specs=[row] + [row] * normed,
        out_shape=[jax.ShapeDtypeStruct((t, n), F32)] + [jax.ShapeDtypeStruct((t, n), BF16)] * normed,
        args=[a, w, resid] + [norm_g] * normed, sem=("parallel",), carry=carry)
    return (outs[0], outs[1] if normed else None), carried


def rowmm_t(dyb, w, scale, out_dtype, name, swiglu=None):
    t, n = dyb.shape
    u_n, ku, _ = w.shape
    tm = _tile(t, 512)

    if swiglu is None:
        def body(dy_ref, w_ref, o_ref):
            o_ref[0] = (scale * _nt(dy_ref[...], w_ref[0])).astype(out_dtype)

        return pl.pallas_call(
            body, name=name, grid=(u_n, t // tm),
            in_specs=[pl.BlockSpec((tm, n), lambda u, i: (i, 0)), pl.BlockSpec((1, ku, n), lambda u, i: (u, 0, 0))],
            out_specs=pl.BlockSpec((1, tm, ku), lambda u, i: (u, i, 0)),
            out_shape=jax.ShapeDtypeStruct((u_n, t, ku), out_dtype),
            compiler_params=_cparams(("parallel", "parallel")),
        )(dyb, w)

    def body(dy_ref, w_ref, silu_ref, udsilu_ref, dg_ref, du_ref):
        dy = dy_ref[...]
        for u in range(u_n):
            dact = scale * _nt(dy, w_ref[u])
            dg_ref[u] = (dact * udsilu_ref[u].astype(F32)).astype(BF16)
            du_ref[u] = (dact * silu_ref[u].astype(F32)).astype(BF16)

    blk = pl.BlockSpec((u_n, tm, ku), lambda i: (0, i, 0))
    return pl.pallas_call(
        body, name=name, grid=(t // tm,),
        in_specs=[pl.BlockSpec((tm, n), lambda i: (i, 0)), pl.BlockSpec((u_n, ku, n), lambda i: (0, 0, 0)), blk, blk],
        out_specs=[blk] * 2, out_shape=[jax.ShapeDtypeStruct((u_n, t, ku), BF16)] * 2,
        compiler_params=_cparams(("parallel",)),
    )(dyb, w, *swiglu)


def colmm_t(dzs, ws, nu, x, g, dy_in, name, carry=None, transposed=False):
    t, k = x.shape
    j, nj = (ws[0].shape[0], ws[0].shape[1]) if transposed else (ws[0].shape[0], ws[0].shape[2])
    per = nj // nu
    units = j * per
    nw = len(ws)
    tm = _tile(t, 256)
    assert not transposed or per == 1

    def body(*refs):
        dz_refs = refs[:nw]
        w_refs = refs[nw:2 * nw]
        x_ref, g_ref, dy_ref, dx_ref, dxb_ref, dg_ref = refs[2 * nw:]
        i = pl.program_id(0)
        dh = jnp.zeros((tm, k), F32)
        for n in range(nw):
            for u in range(units):
                if transposed:
                    dh = dh + _nn(dz_refs[n][u], w_refs[n][u])
                else:
                    wv = w_refs[n][u // per, :, (u % per) * nu:(u % per + 1) * nu]
                    dh = dh + _nt(dz_refs[n][u], wv)
        xv = x_ref[...]
        gv = g_ref[...]
        r = lax.rsqrt(jnp.mean(xv * xv, axis=-1, keepdims=True) + EPS)
        uu = dh * gv
        dx = dy_ref[...] + r * uu - xv * (r * r * r * jnp.mean(uu * xv, axis=-1, keepdims=True))
        dx_ref[...] = dx
        dxb_ref[...] = dx.astype(BF16)
        part = jnp.sum(dh * (xv * r), axis=0, keepdims=True)

        @pl.when(i == 0)
        def _():
            dg_ref[...] = part

        @pl.when(i > 0)
        def _():
            dg_ref[...] += part

    dz_spec = pl.BlockSpec((units, tm, nu), lambda i: (0, i, 0))
    w_spec = pl.BlockSpec((j, nj, k) if transposed else (j, k, nj), lambda i: (0, 0, 0))
    row = pl.BlockSpec((tm, k), lambda i: (i, 0))
    vec = pl.BlockSpec((1, k), lambda i: (0, 0))
    return _call(
        body, name=name, grid=(t // tm,),
        in_specs=[dz_spec] * nw + [w_spec] * nw + [row, vec, row],
        out_specs=[row, row, vec],
        out_shape=[jax.ShapeDtypeStruct((t, k), F32), jax.ShapeDtypeStruct((t, k), BF16),
                   jax.ShapeDtypeStruct((1, k), F32)],
        args=[*dzs, *ws, x, g, dy_in], sem=("arbitrary",), carry=carry)


def dw_col(h, dzs, j, nu, name, transposed=False):
    t, k = h.shape
    units = dzs[0].shape[0]
    per = units // j
    nw = len(dzs)
    tt = _tile(t, DW_TOKENS)
    assert not transposed or per == 1

    def body(*refs):
        h_ref = refs[0]
        s = pl.program_id(1)
        hv = h_ref[...]
        outs, copies = refs[1 + nw:1 + 2 * nw], refs[1 + 2 * nw:]

        @pl.when(s == 0)
        def _():
            for o_ref in outs:
                o_ref[...] = jnp.zeros_like(o_ref)

        for n in range(nw):
            if transposed:
                outs[n][0] += _tn(refs[1 + n][0], hv)
                continue
            for u in range(per):
                outs[n][0, :, u * nu:(u + 1) * nu] += _tn(hv, refs[1 + n][u])

        @pl.when(s == pl.num_programs(1) - 1)
        def _():
            for o_ref, c_ref in zip(outs, copies):
                c_ref[...] = o_ref[...].astype(REDUCE_DTYPE)

    shard = (nu, k) if transposed else (k, per * nu)
    o_spec = pl.BlockSpec((1,) + shard, lambda u, s: (u, 0, 0))
    res = pl.pallas_call(
        body, name=name, grid=(j, t // tt),
        in_specs=[pl.BlockSpec((tt, k), lambda u, s: (s, 0))] + [pl.BlockSpec((per, tt, nu), lambda u, s: (u, s, 0))] * nw,
        out_specs=[o_spec] * (2 * nw),
        out_shape=[jax.ShapeDtypeStruct((j,) + shard, F32)] * nw + [jax.ShapeDtypeStruct((j,) + shard, REDUCE_DTYPE)] * nw,
        compiler_params=_cparams(("parallel", "arbitrary")),
    )(h, *dzs)
    return list(zip(res[:nw], res[nw:]))


def dw_row(a, dyb, scale, name):
    u_n, t, ku = a.shape
    n = dyb.shape[1]
    tt = _tile(t, DW_TOKENS)

    def body(a_ref, dy_ref, o_ref, c_ref):
        @pl.when(pl.program_id(1) == 0)
        def _():
            o_ref[...] = jnp.zeros_like(o_ref)

        o_ref[0] += scale * _tn(a_ref[0], dy_ref[...])

        @pl.when(pl.program_id(1) == pl.num_programs(1) - 1)
        def _():
            c_ref[...] = o_ref[...].astype(REDUCE_DTYPE)

    o_spec = pl.BlockSpec((1, ku, n), lambda u, s: (u, 0, 0))
    return tuple(pl.pallas_call(
        body, name=name, grid=(u_n, t // tt),
        in_specs=[pl.BlockSpec((1, tt, ku), lambda u, s: (u, s, 0)), pl.BlockSpec((tt, n), lambda u, s: (s, 0))],
        out_specs=[o_spec, o_spec],
        out_shape=[jax.ShapeDtypeStruct((u_n, ku, n), F32), jax.ShapeDtypeStruct((u_n, ku, n), REDUCE_DTYPE)],
        compiler_params=_cparams(("parallel", "arbitrary")),
    )(a, dyb))


def loss_head(x, g, target):
    t, d = x.shape
    tm = _tile(t, 256)

    def body(x_ref, g_ref, t_ref, loss_ref, dx_ref, dxb_ref, dg_ref):
        i = pl.program_id(0)
        xv = x_ref[...]
        gv = g_ref[...]
        r = lax.rsqrt(jnp.mean(xv * xv, axis=-1, keepdims=True) + EPS)
        xh = xv * r
        err = xh * gv - t_ref[...]
        dy = err * (1.0 / d)
        uu = dy * gv
        dx = r * uu - xv * (r * r * r * jnp.mean(uu * xv, axis=-1, keepdims=True))
        dx_ref[...] = dx
        dxb_ref[...] = dx.astype(BF16)
        dg_part = jnp.sum(dy * xh, axis=0, keepdims=True)
        row = jnp.sum(err * err, axis=-1, keepdims=True) * (0.5 / d)
        l_part = jnp.zeros((8, LANES), F32) + jnp.sum(row, axis=0, keepdims=True)

        @pl.when(i == 0)
        def _():
            dg_ref[...] = dg_part
            loss_ref[...] = l_part

        @pl.when(i > 0)
        def _():
            dg_ref[...] += dg_part
            loss_ref[...] += l_part

    row = pl.BlockSpec((tm, d), lambda i: (i, 0))
    vec = pl.BlockSpec((1, d), lambda i: (0, 0))
    return pl.pallas_call(
        body, name="loss_head", grid=(t // tm,),
        in_specs=[row, vec, row],
        out_specs=[pl.BlockSpec((8, LANES), lambda i: (0, 0)), row, row, vec],
        out_shape=[jax.ShapeDtypeStruct((8, LANES), F32), jax.ShapeDtypeStruct((t, d), F32),
                   jax.ShapeDtypeStruct((t, d), BF16), jax.ShapeDtypeStruct((1, d), F32)],
        compiler_params=_cparams(("arbitrary",)),
    )(x, g, target)


def _split(v):
    hi = v.astype(BF16)
    lo = (v - hi.astype(F32)).astype(BF16)
    return hi, lo


def _keysums(v, m_ext):
    hi, lo = _split(v)
    outs = []
    for j in range(v.shape[1] // KBLK):
        sl = slice(j * KBLK, (j + 1) * KBLK)
        cs = _nn(jnp.concatenate([hi[:, sl], lo[:, sl]], axis=1), m_ext)
        outs.append((cs[:, :KBLK], cs[:, KBLK:]))
    return outs


def _softplus_parts(z):
    sp = jnp.maximum(z, 0.0) + jnp.log(1.0 + jnp.exp(-jnp.abs(z)))
    return sp, z - sp


def _sum_matrices():
    r = lax.broadcasted_iota(jnp.int32, (2 * KBLK, 2 * KBLK), 0) % KBLK
    c = lax.broadcasted_iota(jnp.int32, (2 * KBLK, 2 * KBLK), 1)
    suffix = jnp.where((r > c) | (c >= KBLK), 1.0, 0.0).astype(BF16)
    prefix = jnp.where((r <= c) | (c >= KBLK), 1.0, 0.0).astype(BF16)
    return suffix, prefix


def _att_geometry(qkv, seq):
    upp = qkv.shape[0] // 3
    bq = min(ATT_BLOCK, seq)
    per_unit = (2 * LANES) // ATT_LANES
    return upp, bq, seq // bq, bq // KBLK, per_unit, upp * per_unit, ATT_LANES // HEAD_DIM


def _head_lanes(rows, heads):
    lane = lax.broadcasted_iota(jnp.int32, (rows, ATT_LANES), 1)
    return [(lane >= HEAD_DIM * h) & (lane < HEAD_DIM * (h + 1)) for h in range(heads)]


def attn_fwd(qkv, n_seq, seq):
    t = qkv.shape[1]
    upp, bq, nq, nsub, per_unit, groups, heads = _att_geometry(qkv, seq)
    suffix_m, _ = _sum_matrices()

    def body(q_ref, k_ref, v_ref, m_ref, o_ref, tot_ref, cnt_ref):
        qi = pl.program_id(2)
        step_id = (pl.program_id(0) * groups + pl.program_id(1)) * nq + qi
        in_head = _head_lanes(bq, heads)
        only = lambda v, h: jnp.where(in_head[h], v, jnp.zeros_like(v))
        q_all = q_ref[0] * jnp.asarray(HEAD_DIM ** -0.5, BF16)
        qs = [only(q_all, h) for h in range(heads)]
        m_ext = m_ref[...]
        row = lax.broadcasted_iota(jnp.int32, (bq, bq), 0)
        col = lax.broadcasted_iota(jnp.int32, (bq, bq), 1)
        diag_mask = col < row

        def block(kj, carry, mask):
            off = pl.multiple_of(kj * bq, bq)
            k_all = k_ref[0, pl.ds(off, bq), :]
            v_all = v_ref[0, pl.ds(off, bq), :]
            rems, acc = carry
            out = []
            for h in range(heads):
                rem = rems[h]
                z = _nt(qs[h], k_all)
                if mask is not None:
                    z = jnp.where(mask, z, MASKED)
                sp, ls = _softplus_parts(z)
                sums = _keysums(-sp, m_ext)
                parts = [None] * nsub
                for j in reversed(range(nsub)):
                    suf, total = sums[j]
                    parts[j] = jnp.exp(ls[:, j * KBLK:(j + 1) * KBLK] + suf + rem)
                    rem = rem + total
                a = jnp.concatenate(parts, axis=1)
                acc = acc + _nn(a.astype(BF16), only(v_all, h))
                out.append(rem)
            return tuple(out), acc

        def most_left(c):
            return functools.reduce(jnp.maximum, [jnp.max(r) for r in c[0]])

        def more(s):
            return (s[0] < qi) & (s[1] > STICK_GONE)

        def step(s):
            c = block(qi - 1 - s[0], s[2], None)
            return s[0] + 1, most_left(c), c

        zero = jnp.zeros((bq, LANES), F32)
        carry = block(qi, ((zero,) * heads, jnp.zeros((bq, ATT_LANES), F32)), diag_mask)
        n_left, _, (rems, acc) = lax.while_loop(more, step, (jnp.int32(0), most_left(carry), carry))
        o_ref[0] = acc.astype(BF16)
        first = lax.broadcasted_iota(jnp.int32, (bq, LANES), 1) < HEAD_DIM
        tot_ref[...] = jnp.concatenate([jnp.where(first, rems[h], rems[h + 1]) for h in range(0, heads, 2)], axis=1)
        cnt_ref[step_id] = n_left.astype(F32)

    qblk = lambda b, g, i: (g // per_unit, b * nq + i, g % per_unit)
    return pl.pallas_call(
        body, name="attn_fwd", grid=(n_seq, groups, nq),
        in_specs=[pl.BlockSpec((1, bq, ATT_LANES), qblk),
                  pl.BlockSpec((1, seq, ATT_LANES), lambda b, g, i: (upp + g // per_unit, b, g % per_unit)),
                  pl.BlockSpec((1, seq, ATT_LANES), lambda b, g, i: (2 * upp + g // per_unit, b, g % per_unit)),
                  pl.BlockSpec((2 * KBLK, 2 * KBLK), lambda b, g, i: (0, 0))],
        out_specs=[pl.BlockSpec((1, bq, ATT_LANES), qblk),
                   pl.BlockSpec((bq, ATT_LANES), lambda b, g, i: (b * nq + i, g)),
                   pl.BlockSpec(memory_space=pltpu.SMEM)],
        out_shape=[jax.ShapeDtypeStruct((upp, t, 2 * LANES), BF16), jax.ShapeDtypeStruct((t, upp * 2 * LANES), F32),
                   jax.ShapeDtypeStruct((n_seq * groups * nq,), F32)],
        compiler_params=_cparams(("arbitrary", "arbitrary", "arbitrary")),
    )(qkv, qkv, qkv, suffix_m)


def attn_bwd(qkv, do, tot, cnt, n_seq, seq):
    t = qkv.shape[1]
    upp, bq, nq, nsub, per_unit, groups, heads = _att_geometry(qkv, seq)
    _, prefix_m = _sum_matrices()
    scale = HEAD_DIM ** -0.5

    def body(q_ref, k_ref, v_ref, do_ref, tot_ref, m_ref, cnt_ref, dq_ref, dk_ref, dv_ref, dk_acc, dv_acc):
        qi = pl.program_id(2)
        step_id = (pl.program_id(0) * groups + pl.program_id(1)) * nq + qi
        n_left = jnp.clip(cnt_ref[step_id].astype(jnp.int32), 0, qi)
        in_head = _head_lanes(bq, heads)
        only = lambda v, h: jnp.where(in_head[h], v, jnp.zeros_like(v))
        q_all = q_ref[0] * jnp.asarray(scale, BF16)
        do_all = do_ref[0]
        qs = [only(q_all, h) for h in range(heads)]
        dos = [only(do_all, h) for h in range(heads)]
        first = lax.broadcasted_iota(jnp.int32, (bq, LANES), 1) < HEAD_DIM
        tots = []
        for h in range(0, heads, 2):
            both = tot_ref[:, h // 2 * LANES:(h // 2 + 1) * LANES]
            swapped = pltpu.roll(both, HEAD_DIM, 1)
            tots += [jnp.where(first, both, swapped), jnp.where(first, swapped, both)]
        m_ext = m_ref[...]
        row = lax.broadcasted_iota(jnp.int32, (bq, bq), 0)
        col = lax.broadcasted_iota(jnp.int32, (bq, bq), 1)
        diag_mask = col < row

        @pl.when(qi == 0)
        def _():
            dk_acc[...] = jnp.zeros_like(dk_acc)
            dv_acc[...] = jnp.zeros_like(dv_acc)

        def block(kj, carry, mask):
            off = pl.multiple_of(kj * bq, bq)
            k_all = k_ref[0, pl.ds(off, bq), :]
            v_all = v_ref[0, pl.ds(off, bq), :]
            pres, gpres, dq = carry
            dk_part = jnp.zeros((bq, ATT_LANES), F32)
            dv_part = jnp.zeros((bq, ATT_LANES), F32)
            pres_out, gpres_out = [], []
            for h in range(heads):
                pre, gpre = pres[h], gpres[h]
                z = _nt(qs[h], k_all)
                if mask is not None:
                    z = jnp.where(mask, z, MASKED)
                sp, ls = _softplus_parts(z)
                sums = _keysums(-sp, m_ext)
                parts = []
                for j in range(nsub):
                    pin, ptot = sums[j]
                    parts.append(jnp.exp(ls[:, j * KBLK:(j + 1) * KBLK] + (tots[h] - (pre + pin))))
                    pre = pre + ptot
                a = jnp.concatenate(parts, axis=1)
                g = a * _nt(dos[h], v_all)
                gsums = _keysums(g, m_ext)
                parts = []
                for j in range(nsub):
                    gin, gtot = gsums[j]
                    parts.append(gpre + gin)
                    gpre = gpre + gtot
                dz = g - jnp.exp(ls) * jnp.concatenate(parts, axis=1)
                dzb = dz.astype(BF16)
                dq = dq + _nn(dzb, only(k_all, h))
                dk_part = dk_part + _tn(dzb, qs[h])
                dv_part = dv_part + _tn(a.astype(BF16), dos[h])
                pres_out.append(pre)
                gpres_out.append(gpre)
            dk_acc[pl.ds(off, bq), :] += dk_part
            dv_acc[pl.ds(off, bq), :] += dv_part
            return tuple(pres_out), tuple(gpres_out), dq

        zero = jnp.zeros((bq, LANES), F32)
        carry = ((zero,) * heads, (zero,) * heads, jnp.zeros((bq, ATT_LANES), F32))
        carry = lax.fori_loop(qi - n_left, qi, lambda kj, c: block(kj, c, None), carry)
        carry = block(qi, carry, diag_mask)
        dq_ref[0] = (carry[2] * scale).astype(BF16)

        @pl.when(qi == nq - 1)
        def _():
            dk_ref[0] = dk_acc[...].astype(BF16)
            dv_ref[0] = dv_acc[...].astype(BF16)

    qblk = lambda b, g, i: (g // per_unit, b * nq + i, g % per_unit)
    kv_out = pl.BlockSpec((1, seq, ATT_LANES), lambda b, g, i: (g // per_unit, b, g % per_unit))
    shp = jax.ShapeDtypeStruct((upp, t, 2 * LANES), BF16)
    return pl.pallas_call(
        body, name="attn_bwd", grid=(n_seq, groups, nq),
        in_specs=[pl.BlockSpec((1, bq, ATT_LANES), qblk),
                  pl.BlockSpec((1, seq, ATT_LANES), lambda b, g, i: (upp + g // per_unit, b, g % per_unit)),
                  pl.BlockSpec((1, seq, ATT_LANES), lambda b, g, i: (2 * upp + g // per_unit, b, g % per_unit)),
                  pl.BlockSpec((1, bq, ATT_LANES), qblk),
                  pl.BlockSpec((bq, ATT_LANES), lambda b, g, i: (b * nq + i, g)),
                  pl.BlockSpec((2 * KBLK, 2 * KBLK), lambda b, g, i: (0, 0)),
                  pl.BlockSpec(memory_space=pltpu.SMEM)],
        out_specs=[pl.BlockSpec((1, bq, ATT_LANES), qblk), kv_out, kv_out],
        out_shape=[shp, shp, shp],
        scratch_shapes=[pltpu.VMEM((seq, ATT_LANES), F32), pltpu.VMEM((seq, ATT_LANES), F32)],
        compiler_params=_cparams(("parallel", "parallel", "arbitrary")),
    )(qkv, qkv, qkv, do, tot, prefix_m, cnt)


def _ln_stats(v):
    mu = jnp.mean(v, axis=-1, keepdims=True)
    vc = v - mu
    rstd = lax.rsqrt(jnp.mean(vc * vc, axis=-1, keepdims=True) + EPS)
    return vc * rstd, rstd


def _glu_into(a0_ref, av_ref, ag_ref, hv_ref, hg_ref, first):
    hv = hv_ref[0].astype(F32)
    hg = hg_ref[0].astype(F32)
    a0_ref[0:HALO, :] = jnp.where(first, 0.0, hv * _sigmoid(hg))
    av = av_ref[0].astype(F32)
    ag = ag_ref[0].astype(F32)
    a0_ref[HALO:, :] = av * _sigmoid(ag)


def _shifted_taps(ref, shifted_ref, tm, first):
    taps = []
    for b in range(8):
        offs = [o for o in range(first, first + CONV_WIDTH) if o % 8 == b]
        n_rows = max(offs) - b + tm
        shifted_ref[b, 0:n_rows, :] = ref[pl.ds(b, n_rows), :]
        taps += [(b, o - b, o - first) for o in offs]
    return taps


def _tril_mask():
    r = lax.broadcasted_iota(jnp.int32, (CHUNK, CHUNK), 0)
    c = lax.broadcasted_iota(jnp.int32, (CHUNK, CHUNK), 1)
    return c <= r


def mix_fwd(z, conv_w, conv_b, ln_a_g, ln_a_b, ln_v_g, ln_v_b, sp_w, sp_bt, seq):
    _, t, c = z.shape
    tm = _tile(seq, 512)
    tiles_per_seq = seq // tm
    groups = c // LANES
    hb = tm // HALO

    def body(av_ref, ag_ref, u_ref, v_ref, hv_ref, hg_ref, cw_ref, cb_ref, lag_ref, lab_ref, lvg_ref, lvb_ref,
             spw_ref, spb_ref, cat_ref, a1_ref, a0_ref, sh_ref):
        i = pl.program_id(0)
        _glu_into(a0_ref, av_ref, ag_ref, hv_ref, hg_ref, i % tiles_per_seq == 0)
        acc = jnp.zeros((tm, c), F32) + cb_ref[...]
        for b, ro, k in _shifted_taps(a0_ref, sh_ref, tm, HALO - (CONV_WIDTH - 1)):
            acc = acc + cw_ref[k:k + 1, :] * sh_ref[b, pl.ds(ro, tm), :]
        a1_ref[...] = acc
        xh, _ = _ln_stats(acc)
        a2 = xh * lag_ref[...] + lab_ref[...]
        a3 = (a2 * _sigmoid(a2)).astype(BF16)
        half = c // 2
        cat_ref[0] = a3[:, :half]
        cat_ref[1] = a3[:, half:]
        tril = _tril_mask()
        for g in range(groups):
            sl = slice(g * LANES, (g + 1) * LANES)
            xh, _ = _ln_stats(v_ref[0][:, sl].astype(F32))
            vn = (xh * lvg_ref[:, sl] + lvb_ref[:, sl]).astype(BF16)
            w = jnp.where(tril, spw_ref[g], 0.0).astype(BF16)
            bias = spb_ref[:, g:g + 1]
            for ch in range(tm // CHUNK):
                rows = slice(ch * CHUNK, (ch + 1) * CHUNK)
                vs = _nn(w, vn[rows]) + bias
                bo = (u_ref[0][rows, sl].astype(F32) * vs).astype(BF16)
                cat_ref[2 + (g * LANES) // half, rows, (g * LANES) % half:(g * LANES) % half + LANES] = bo

    unit = lambda u: pl.BlockSpec((1, tm, c), lambda i: (u, i, 0))
    halo = lambda u: pl.BlockSpec((1, HALO, c), lambda i: (u, jnp.maximum(i * hb - 1, 0), 0))
    vec = pl.BlockSpec((1, c), lambda i: (0, 0))
    return pl.pallas_call(
        body, name="mix_fwd", grid=(t // tm,),
        in_specs=[unit(0), unit(1), unit(2), unit(3), halo(0), halo(1),
                  pl.BlockSpec((CONV_WIDTH, c), lambda i: (0, 0)), vec, vec, vec, vec, vec,
                  pl.BlockSpec((groups, CHUNK, CHUNK), lambda i: (0, 0, 0)),
                  pl.BlockSpec((CHUNK, groups), lambda i: (0, 0))],
        out_specs=[pl.BlockSpec((4, tm, c // 2), lambda i: (0, i, 0)), pl.BlockSpec((tm, c), lambda i: (i, 0))],
        out_shape=[jax.ShapeDtypeStruct((4, t, c // 2), BF16), jax.ShapeDtypeStruct((t, c), F32)],
        scratch_shapes=[pltpu.VMEM((HALO + tm, c), F32), pltpu.VMEM((8, HALO + tm, c), F32)],
        compiler_params=_cparams(("parallel",)),
    )(z, z, z, z, z, z, conv_w, conv_b, ln_a_g, ln_a_b, ln_v_g, ln_v_b, sp_w, sp_bt)


def mix_bwd_point(dcat, z, a1, ln_a_g, ln_a_b, ln_v_g, ln_v_b, sp_w, sp_wt, sp_bt, seq):
    _, t, c = z.shape
    tm = _tile(seq, 512)
    groups = c // LANES
    half = c // 2

    def body(dc_ref, u_ref, v_ref, a1_ref, lag_ref, lab_ref, lvg_ref, lvb_ref, spw_ref, spwt_ref, spb_ref,
             dz_ref, da1_ref, dcb_ref, dlag_ref, dlab_ref, dlvg_ref, dlvb_ref, dspw_ref, dspb_ref):
        i = pl.program_id(0)
        last = pl.num_programs(0) - 1

        @pl.when(i == 0)
        def _():
            for r in (dcb_ref, dlag_ref, dlab_ref, dlvg_ref, dlvb_ref, dspw_ref, dspb_ref):
                r[...] = jnp.zeros_like(r)

        da3 = jnp.concatenate([dc_ref[0], dc_ref[1]], axis=-1)
        xh, rstd = _ln_stats(a1_ref[...])
        a2 = xh * lag_ref[...] + lab_ref[...]
        s = _sigmoid(a2)
        da2 = da3 * (s * (1.0 + a2 * (1.0 - s)))
        dlag_ref[...] += jnp.sum(da2 * xh, axis=0, keepdims=True)
        dlab_ref[...] += jnp.sum(da2, axis=0, keepdims=True)
        dxh = da2 * lag_ref[...]
        da1 = rstd * (dxh - jnp.mean(dxh, axis=-1, keepdims=True) - xh * jnp.mean(dxh * xh, axis=-1, keepdims=True))
        da1_ref[...] = da1
        dcb_ref[...] += jnp.sum(da1, axis=0, keepdims=True)

        tril = _tril_mask()
        for g in range(groups):
            sl = slice(g * LANES, (g + 1) * LANES)
            xh, rstd = _ln_stats(v_ref[0][:, sl].astype(F32))
            lg = lvg_ref[:, sl]
            vnb = (xh * lg + lvb_ref[:, sl]).astype(BF16)
            w = jnp.where(tril, spw_ref[g], 0.0).astype(BF16)
            wt = jnp.where(tril.T, spwt_ref[g], 0.0).astype(BF16)
            bias = spb_ref[:, g:g + 1]
            dbo_all = dc_ref[2 + (g * LANES) // half][:, (g * LANES) % half:(g * LANES) % half + LANES]
            dvn_parts = []
            dw_acc = jnp.zeros((CHUNK, CHUNK), F32)
            db_acc = jnp.zeros((CHUNK, LANES), F32)
            for ch in range(tm // CHUNK):
                rows = slice(ch * CHUNK, (ch + 1) * CHUNK)
                vs = _nn(w, vnb[rows]) + bias
                dbo = dbo_all[rows]
                uv = u_ref[0][rows, sl].astype(F32)
                dz_ref[0, rows, sl] = (dbo * vs).astype(BF16)
                dvs = dbo * uv
                dvsb = dvs.astype(BF16)
                dvn_parts.append(_nn(wt, dvsb))
                dw_acc = dw_acc + _nt(dvsb, vnb[rows])
                db_acc = db_acc + dvs
            dvn = jnp.concatenate(dvn_parts, axis=0)
            dspw_ref[g] += jnp.where(tril, dw_acc, 0.0)
            dspb_ref[g] += db_acc
            dlvg_ref[:, sl] += jnp.sum(dvn * xh, axis=0, keepdims=True)
            dlvb_ref[:, sl] += jnp.sum(dvn, axis=0, keepdims=True)
            dxh = dvn * lg
            dv = rstd * (dxh - jnp.mean(dxh, axis=-1, keepdims=True) - xh * jnp.mean(dxh * xh, axis=-1, keepdims=True))
            dz_ref[1, :, sl] = dv.astype(BF16)

        @pl.when(i == last)
        def _():
            for g in range(groups):
                dspb_ref[g] = jnp.zeros((CHUNK, LANES), F32) + jnp.sum(dspb_ref[g], axis=-1, keepdims=True)

    unit = lambda u: pl.BlockSpec((1, tm, c), lambda i: (u, i, 0))
    vec = pl.BlockSpec((1, c), lambda i: (0, 0))
    sq = pl.BlockSpec((groups, CHUNK, CHUNK), lambda i: (0, 0, 0))
    vshape = jax.ShapeDtypeStruct((1, c), F32)
    sshape = jax.ShapeDtypeStruct((groups, CHUNK, CHUNK), F32)
    return pl.pallas_call(
        body, name="mix_bwd_point", grid=(t // tm,),
        in_specs=[pl.BlockSpec((4, tm, half), lambda i: (0, i, 0)), unit(2), unit(3),
                  pl.BlockSpec((tm, c), lambda i: (i, 0)), vec, vec, vec, vec, sq, sq,
                  pl.BlockSpec((CHUNK, groups), lambda i: (0, 0))],
        out_specs=[pl.BlockSpec((2, tm, c), lambda i: (1, i, 0)), pl.BlockSpec((tm, c), lambda i: (i, 0)),
                   vec, vec, vec, vec, vec, sq, sq],
        out_shape=[jax.ShapeDtypeStruct((4, t, c), BF16), jax.ShapeDtypeStruct((t, c), F32),
                   vshape, vshape, vshape, vshape, vshape, sshape, sshape],
        compiler_params=_cparams(("arbitrary",)),
    )(dcat, z, z, a1, ln_a_g, ln_a_b, ln_v_g, ln_v_b, sp_w, sp_wt, sp_bt)


def mix_bwd_conv(dz, da1, z, conv_w, seq):
    _, t, c = z.shape
    tm = _tile(seq, 512)
    tiles_per_seq = seq // tm
    hb = tm // HALO
    n_halo_blocks = t // HALO

    rc = _tile(tm, CONV_ROWS)

    def body(dz_in_ref, d_ref, dh_ref, av_ref, ag_ref, cw_ref, dz_ref, dcw_ref, d1_ref, sh_ref, part_ref):
        del dz_in_ref
        i = pl.program_id(0)

        @pl.when(i == 0)
        def _():
            part_ref[...] = jnp.zeros_like(part_ref)

        d1_ref[0:tm, :] = d_ref[...]
        d1_ref[tm:, :] = jnp.where((i + 1) % tiles_per_seq == 0, 0.0, dh_ref[...])
        taps = _shifted_taps(d1_ref, sh_ref, tm, 0)

        def chunk(ci, carry):
            r0 = pl.multiple_of(ci * rc, rc)
            av = av_ref[0, pl.ds(r0, rc), :].astype(F32)
            s = _sigmoid(ag_ref[0, pl.ds(r0, rc), :].astype(F32))
            a0 = av * s
            da0 = jnp.zeros((rc, c), F32)
            for b, ro, back in taps:
                k = CONV_WIDTH - 1 - back
                rows = sh_ref[b, pl.ds(r0 + ro, rc), :]
                da0 = da0 + cw_ref[k:k + 1, :] * rows
                prod = a0 * rows
                part_ref[k] += functools.reduce(lambda p, q: p + q, [prod[8 * r:8 * r + 8] for r in range(rc // 8)])
            dz_ref[0, pl.ds(r0, rc), :] = (da0 * s).astype(BF16)
            dz_ref[1, pl.ds(r0, rc), :] = (da0 * av * s * (1.0 - s)).astype(BF16)
            return carry

        lax.fori_loop(0, tm // rc, chunk, 0)

        @pl.when(i == pl.num_programs(0) - 1)
        def _():
            dcw_ref[...] = jnp.sum(part_ref[...], axis=1)

    unit = lambda u: pl.BlockSpec((1, tm, c), lambda i: (u, i, 0))
    return pl.pallas_call(
        body, name="mix_bwd_conv", grid=(t // tm,),
        in_specs=[pl.BlockSpec(memory_space=pl.ANY), pl.BlockSpec((tm, c), lambda i: (i, 0)),
                  pl.BlockSpec((HALO, c), lambda i: (jnp.minimum((i + 1) * hb, n_halo_blocks - 1), 0)),
                  unit(0), unit(1), pl.BlockSpec((CONV_WIDTH, c), lambda i: (0, 0))],
        out_specs=[pl.BlockSpec((2, tm, c), lambda i: (0, i, 0)), pl.BlockSpec((CONV_WIDTH, c), lambda i: (0, 0))],
        out_shape=[jax.ShapeDtypeStruct(dz.shape, BF16), jax.ShapeDtypeStruct((CONV_WIDTH, c), F32)],
        scratch_shapes=[pltpu.VMEM((tm + HALO, c), F32), pltpu.VMEM((8, tm + HALO, c), F32),
                        pltpu.VMEM((CONV_WIDTH, 8, c), F32)],
        input_output_aliases={0: 0},
        compiler_params=_cparams(("arbitrary",)),
    )(dz, da1, da1, z, z, conv_w)


CHIP_FLIPS = ((1, 0), (0, 1), (1, 1))
ANY = pl.BlockSpec(memory_space=pl.ANY)


def _place():
    return lax.axis_index("x"), lax.axis_index("y"), lax.axis_index("c")


def _flip(v, f):
    return 1 - v if f else v


def place_shard(w, chip, dtype, name):
    n_layers, r, cc = w.shape
    rb = _tile(r, 512)

    def body(chip_ref, w_ref, *o_refs):
        del chip_ref
        for layer, o_ref in enumerate(o_refs):
            o_ref[0] = w_ref[layer].astype(dtype)

    return pl.pallas_call(
        body, name=name,
        grid_spec=pltpu.PrefetchScalarGridSpec(
            num_scalar_prefetch=1, grid=(r // rb,),
            in_specs=[pl.BlockSpec((n_layers, rb, cc), lambda i, chip_ref: (0, i, 0))],
            out_specs=[pl.BlockSpec((1, rb, cc), lambda i, chip_ref: (chip_ref[0], i, 0))] * n_layers),
        out_shape=[jax.ShapeDtypeStruct((N_CHIPS, r, cc), dtype)] * n_layers,
        compiler_params=_cparams(("parallel",)),
    )(chip, w)


class Carry:
    def __init__(self, arrays, out_shapes, aliased, sem_shapes, start, finish):
        self.arrays, self.out_shapes, self.aliased, self.sem_shapes = list(arrays), list(out_shapes), aliased, list(sem_shapes)
        self.start, self.finish = start, finish


def _call(body, *, name, grid, in_specs, out_specs, out_shape, args, sem, scratch_shapes=(), carry=None):
    if carry is None:
        res = pl.pallas_call(body, name=name, grid=grid, in_specs=in_specs, out_specs=out_specs, out_shape=out_shape,
                             scratch_shapes=list(scratch_shapes), compiler_params=_cparams(sem))(*args)
        return list(res), []
    n_in, n_out, n_scr, nc = len(args), len(out_shape), len(scratch_shapes), len(carry.arrays)

    def full_body(*refs):
        ins, refs = refs[:n_in], refs[n_in:]
        c_ins, refs = refs[:nc], refs[nc:]
        outs, refs = refs[:n_out], refs[n_out:]
        c_outs, refs = refs[:nc], refs[nc:]
        scr, sems = refs[:n_scr], refs[n_scr:]
        first = functools.reduce(lambda a, b: a & b, [pl.program_id(d) == 0 for d in range(len(grid))])
        last = functools.reduce(lambda a, b: a & b, [pl.program_id(d) == grid[d] - 1 for d in range(len(grid))])

        @pl.when(first)
        def _():
            carry.start(c_ins, c_outs, sems)

        body(*ins, *outs, *scr)

        @pl.when(last)
        def _():
            carry.finish(c_ins, c_outs, sems)

    res = pl.pallas_call(
        full_body, name=name, grid=grid, in_specs=list(in_specs) + [ANY] * nc, out_specs=list(out_specs) + [ANY] * nc,
        out_shape=list(out_shape) + carry.out_shapes, scratch_shapes=list(scratch_shapes) + carry.sem_shapes,
        input_output_aliases={n_in + i: n_out + i for i in range(nc)} if carry.aliased else {},
        compiler_params=pltpu.CompilerParams(dimension_semantics=("arbitrary",) * len(grid), vmem_limit_bytes=VMEM_LIMIT,
                                             has_side_effects=True),
    )(*args, *carry.arrays)
    return list(res[:n_out]), list(res[n_out:])


def _gather_ops(shapes, whole):
    n = len(shapes)

    def rows(a, c):
        hr = shapes[a][1] // 2
        return pl.ds(pl.multiple_of(c * hr, 16), hr)

    def start(ins, outs, sems):
        ici_send, ici_recv = sems[0], sems[1]
        x, y, c = _place()
        k = 2 * x + y
        for a in range(n):
            for o, (fx, fy) in enumerate(CHIP_FLIPS):
                src = ins[a].at[k] if whole[a] else ins[a].at[k, rows(a, c)]
                dst = outs[a].at[k] if whole[a] else outs[a].at[k, rows(a, c)]
                pltpu.make_async_remote_copy(
                    src_ref=src, dst_ref=dst, send_sem=ici_send.at[3 * a + o], recv_sem=ici_recv.at[3 * a + o],
                    device_id=(_flip(x, fx), _flip(y, fy), c), device_id_type=MESH).start()

    def finish(ins, outs, sems):
        ici_send, ici_recv, d2d_send, d2d_recv = sems
        x, y, c = _place()
        k = 2 * x + y
        sibling = (x, y, 1 - c)

        def copy(ref, send, recv, a, o):
            return pltpu.make_async_remote_copy(src_ref=ref, dst_ref=ref, send_sem=send.at[3 * a + o],
                                                recv_sem=recv.at[3 * a + o], device_id=sibling, device_id_type=MESH)

        for a in range(n):
            for o, (fx, fy) in enumerate(CHIP_FLIPS):
                kk = 2 * _flip(x, fx) + _flip(y, fy)
                landed = outs[a].at[kk] if whole[a] else outs[a].at[kk, rows(a, c)]
                copy(landed, ici_send, ici_recv, a, o).wait_recv()
                if not whole[a]:
                    copy(landed, d2d_send, d2d_recv, a, o).start()
        for a in range(n):
            for o, (fx, fy) in enumerate(CHIP_FLIPS):
                kk = 2 * _flip(x, fx) + _flip(y, fy)
                mine = ins[a].at[k] if whole[a] else ins[a].at[k, rows(a, c)]
                copy(mine, ici_send, ici_recv, a, o).wait_send()
                if not whole[a]:
                    copy(outs[a].at[kk, rows(a, 1 - c)], d2d_send, d2d_recv, a, o).wait_recv()
                    copy(outs[a].at[kk, rows(a, c)], d2d_send, d2d_recv, a, o).wait_send()

    dma = pltpu.SemaphoreType.DMA
    return start, finish, [dma((3 * n,))] * 4


def gather_carry(bufs):
    start, finish, sems = _gather_ops([b.shape for b in bufs], [False] * len(bufs))
    return Carry(bufs, [jax.ShapeDtypeStruct(b.shape, b.dtype) for b in bufs], True, sems, start, finish)


def allgather_weights(shards, smalls):
    bufs = list(shards) + list(smalls)
    n = len(bufs)
    start, finish, sems = _gather_ops([b.shape for b in bufs], [False] * len(shards) + [True] * len(smalls))

    def body(*refs):
        start(refs[:n], refs[n:2 * n], refs[2 * n:])
        finish(refs[:n], refs[n:2 * n], refs[2 * n:])

    res = pl.pallas_call(
        body, name="allgather_weights", in_specs=[ANY] * n, out_specs=[ANY] * n,
        out_shape=[jax.ShapeDtypeStruct(b.shape, b.dtype) for b in bufs], scratch_shapes=sems,
        input_output_aliases={i: i for i in range(n)},
        compiler_params=pltpu.CompilerParams(has_side_effects=True),
    )(*bufs)
    return res[:len(shards)], res[len(shards):]


def rs_exchange(grads):
    n = len(grads)

    def body(*refs):
        ins, outs = refs[:n], refs[n:2 * n]
        send, recv = refs[2 * n:]
        x, y, c = _place()
        cps = []
        for a in range(n):
            cp = pltpu.make_async_remote_copy(
                src_ref=ins[a].at[:, 1 - c], dst_ref=outs[a], send_sem=send.at[a], recv_sem=recv.at[a],
                device_id=(x, y, 1 - c), device_id_type=MESH)
            cp.start()
            cps.append(cp)
        for cp in cps:
            cp.wait()

    dma = pltpu.SemaphoreType.DMA
    return pl.pallas_call(
        body, name="rs_exchange", in_specs=[ANY] * n, out_specs=[ANY] * n,
        out_shape=[jax.ShapeDtypeStruct((g.shape[0],) + g.shape[2:], g.dtype) for g in grads],
        scratch_shapes=[dma((n,)), dma((n,))],
        compiler_params=pltpu.CompilerParams(has_side_effects=True),
    )(*grads)


def rs_add(gs, sibs, core, out_dtype, name):
    n = len(gs)
    nk = gs[0].shape[0]

    def body(core_ref, *refs):
        del core_ref
        for a in range(n):
            refs[2 * n + a][0] = (refs[a][0, 0] + refs[n + a][0]).astype(out_dtype)

    halves = [g.shape[2:] for g in gs]
    return pl.pallas_call(
        body, name=name,
        grid_spec=pltpu.PrefetchScalarGridSpec(
            num_scalar_prefetch=1, grid=(nk,),
            in_specs=[pl.BlockSpec((1, 1) + h, lambda k, core_ref: (k, core_ref[0], 0, 0)) for h in halves]
            + [pl.BlockSpec((1,) + h, lambda k, core_ref: (k, 0, 0)) for h in halves],
            out_specs=[pl.BlockSpec((1,) + h, lambda k, core_ref: (k, 0, 0)) for h in halves]),
        out_shape=[jax.ShapeDtypeStruct((nk,) + h, out_dtype) for h in halves],
        compiler_params=_cparams(("parallel",)),
    )(core, *gs, *sibs)


def send_carry(parts):
    n = len(parts)

    def copies(ins, outs, sems):
        x, y, c = _place()
        for a in range(n):
            for o, (fx, fy) in enumerate(CHIP_FLIPS):
                kk = 2 * _flip(x, fx) + _flip(y, fy)
                yield pltpu.make_async_remote_copy(
                    src_ref=ins[a].at[kk], dst_ref=outs[a].at[o], send_sem=sems[0].at[3 * a + o],
                    recv_sem=sems[1].at[3 * a + o], device_id=(_flip(x, fx), _flip(y, fy), c), device_id_type=MESH)

    def start(ins, outs, sems):
        for cp in copies(ins, outs, sems):
            cp.start()

    def finish(ins, outs, sems):
        for cp in copies(ins, outs, sems):
            cp.wait()

    dma = pltpu.SemaphoreType.DMA
    return Carry(parts, [jax.ShapeDtypeStruct((3,) + p.shape[1:], p.dtype) for p in parts], False,
                 [dma((3 * n,)), dma((3 * n,))], start, finish)


def rs_sum(recvs, parts, where, name):
    n_layers = len(recvs)
    _, hr, cc = recvs[0].shape
    rb = _tile(hr, 256)

    def body(where_ref, *refs):
        del where_ref
        o_ref = refs[-1]
        for layer in range(n_layers):
            r_ref, p_ref = refs[layer], refs[n_layers + layer]
            o_ref[layer, 0] = ((p_ref[0].astype(F32) + r_ref[0].astype(F32)) + r_ref[1].astype(F32)) + r_ref[2].astype(F32)

    return pl.pallas_call(
        body, name=name,
        grid_spec=pltpu.PrefetchScalarGridSpec(
            num_scalar_prefetch=1, grid=(hr // rb,),
            in_specs=[pl.BlockSpec((3, rb, cc), lambda i, w_ref: (0, i, 0))] * n_layers
            + [pl.BlockSpec((1, rb, cc), lambda i, w_ref: (w_ref[0], i, 0))] * n_layers,
            out_specs=pl.BlockSpec((n_layers, 1, rb, cc), lambda i, w_ref: (0, w_ref[1], i, 0))),
        out_shape=jax.ShapeDtypeStruct((n_layers, 2, hr, cc), F32),
        compiler_params=_cparams(("parallel",)),
    )(where, *recvs, *parts)


def rs_share(fulls):
    n = len(fulls)

    def body(*refs):
        ins, outs = refs[:n], refs[n:2 * n]
        send, recv = refs[2 * n:]
        x, y, c = _place()
        cps = []
        for a in range(n):
            cp = pltpu.make_async_remote_copy(
                src_ref=ins[a].at[:, c], dst_ref=outs[a].at[:, c], send_sem=send.at[a], recv_sem=recv.at[a],
                device_id=(x, y, 1 - c), device_id_type=MESH)
            cp.start()
            cps.append(cp)
        for a in range(n):
            got = outs[a].at[:, 1 - c]
            pltpu.make_async_remote_copy(
                src_ref=got, dst_ref=got, send_sem=send.at[a], recv_sem=recv.at[a],
                device_id=(x, y, 1 - c), device_id_type=MESH).wait_recv()
        for cp in cps:
            cp.wait_send()

    dma = pltpu.SemaphoreType.DMA
    return pl.pallas_call(
        body, name="rs_share", in_specs=[ANY] * n, out_specs=[ANY] * n,
        out_shape=[jax.ShapeDtypeStruct(f.shape, f.dtype) for f in fulls],
        scratch_shapes=[dma((n,)), dma((n,))],
        input_output_aliases={i: i for i in range(n)},
        compiler_params=pltpu.CompilerParams(has_side_effects=True),
    )(*fulls)


def allreduce_small(v):
    r, w = v.shape

    def body(v_ref, o_ref, buf, send, recv, loc):
        x, y, c = _place()
        me = 4 * x + 2 * y + c
        mine = pltpu.make_async_copy(v_ref, buf.at[me], loc)
        mine.start()
        cps = []
        for o in range(1, N_DEV):
            fx, fy, fc = (o >> 2) & 1, (o >> 1) & 1, o & 1
            cp = pltpu.make_async_remote_copy(
                src_ref=v_ref, dst_ref=buf.at[me], send_sem=send.at[o - 1], recv_sem=recv.at[o - 1],
                device_id=(_flip(x, fx), _flip(y, fy), _flip(c, fc)), device_id_type=MESH)
            cp.start()
            cps.append(cp)
        for o in range(1, N_DEV):
            fx, fy, fc = (o >> 2) & 1, (o >> 1) & 1, o & 1
            peer = 4 * _flip(x, fx) + 2 * _flip(y, fy) + _flip(c, fc)
            pltpu.make_async_remote_copy(
                src_ref=v_ref, dst_ref=buf.at[peer], send_sem=send.at[o - 1], recv_sem=recv.at[o - 1],
                device_id=(x, y, c), device_id_type=MESH).wait_recv()
        for cp in cps:
            cp.wait_send()
        mine.wait()
        acc = buf[0]
        for d in range(1, N_DEV):
            acc = acc + buf[d]
        o_ref[...] = acc

    dma = pltpu.SemaphoreType.DMA
    vm = pl.BlockSpec(memory_space=pltpu.VMEM)
    return pl.pallas_call(
        body, name="allreduce_small", in_specs=[vm], out_specs=vm,
        out_shape=jax.ShapeDtypeStruct((r, w), F32),
        scratch_shapes=[pltpu.VMEM((N_DEV, r, w), F32), dma((N_DEV - 1,)), dma((N_DEV - 1,)), dma],
        compiler_params=pltpu.CompilerParams(has_side_effects=True, vmem_limit_bytes=VMEM_LIMIT),
    )(v)


def adamw(w, g, m, v, name):
    r, cc = w.shape
    rb = _tile(r, 256)

    def body(w_ref, g_ref, m_ref, v_ref, d_ref, nm_ref, nv_ref):
        gv = g_ref[...]
        nm = ADAM_B1 * m_ref[...] + (1.0 - ADAM_B1) * gv
        nv = ADAM_B2 * v_ref[...] + (1.0 - ADAM_B2) * (gv * gv)
        m_hat = nm / (1.0 - ADAM_B1 ** ADAM_STEP)
        v_hat = nv / (1.0 - ADAM_B2 ** ADAM_STEP)
        d_ref[...] = -ADAM_LR * (m_hat / (jnp.sqrt(v_hat) + ADAM_EPS) + ADAM_WD * w_ref[...])
        nm_ref[...] = nm
        nv_ref[...] = nv

    blk = pl.BlockSpec((rb, cc), lambda i: (i, 0))
    shp = jax.ShapeDtypeStruct((r, cc), F32)
    return pl.pallas_call(
        body, name=name, grid=(r // rb,), in_specs=[blk] * 4, out_specs=[blk] * 3, out_shape=[shp] * 3,
        compiler_params=_cparams(("parallel",)),
    )(w, g, m, v)


WEIGHTS = ['g_ffn1', 'w_ffn1_gate', 'w_ffn1_up', 'w_ffn1_down', 'g_mix', 'w_in_ab', 'conv_w', 'conv_b', 'ln_a_g',
           'ln_a_b', 'ln_v_g', 'ln_v_b', 'sp_w', 'sp_b', 'w_out_ab', 'w_qkv', 'w_o', 'g_ffn2', 'w_ffn2_gate',
           'w_ffn2_up', 'w_ffn2_down', 'g_final']
BIG = ['w_ffn1_gate', 'w_ffn1_up', 'w_ffn1_down', 'w_in_ab', 'w_out_ab', 'w_qkv', 'w_o', 'w_ffn2_gate', 'w_ffn2_up',
       'w_ffn2_down']
SMALL = ['g_ffn1', 'g_mix', 'g_ffn2', 'g_final', 'conv_b', 'ln_a_g', 'ln_a_b', 'ln_v_g', 'ln_v_b', 'sp_b', 'sp_w']
HIDDEN_MAJOR = ['w_ffn1_gate', 'w_ffn1_up', 'w_ffn2_gate', 'w_ffn2_up']


CARRY_WEIGHTS = {"ffn_gateup": 9.2e6, "ffn_down": 6.1e6, "mm_in": 5.9e6, "mm_out": 3.3e6}


def _use_order(depth):
    order = []
    for layer in range(depth):
        order += [('w_ffn1_gate', layer), ('w_ffn1_up', layer), ('w_ffn1_down', layer)]
        order += [('w_in_ab', layer // 2), ('w_out_ab', layer // 2)] if layer % 2 == 0 else [('w_qkv', layer // 2), ('w_o', layer // 2)]
        order += [('w_ffn2_gate', layer), ('w_ffn2_up', layer), ('w_ffn2_down', layer)]
    return order


def _rows(a):
    return a.reshape(-1, LANES)


def _pack(parts):
    v = jnp.concatenate([_rows(p) for p in parts], axis=0)
    pad = (-v.shape[0]) % 8
    return jnp.pad(v, ((0, pad), (0, 0)))


def _unpack(v, shapes):
    out, r = [], 0
    for s in shapes:
        n = 1
        for d in s:
            n *= d
        n //= LANES
        out.append(v[r:r + n].reshape(s))
        r += n
    return out


def kernel(x, g_ffn1, w_ffn1_gate, w_ffn1_up, w_ffn1_down, g_mix, w_in_ab, conv_w, conv_b, ln_a_g, ln_a_b, ln_v_g, ln_v_b, sp_w, sp_b, w_out_ab, w_qkv, w_o, g_ffn2, w_ffn2_gate, w_ffn2_up, w_ffn2_down, g_final, loss_target, m_g_ffn1, m_w_ffn1_gate, m_w_ffn1_up, m_w_ffn1_down, m_g_mix, m_w_in_ab, m_conv_w, m_conv_b, m_ln_a_g, m_ln_a_b, m_ln_v_g, m_ln_v_b, m_sp_w, m_sp_b, m_w_out_ab, m_w_qkv, m_w_o, m_g_ffn2, m_w_ffn2_gate, m_w_ffn2_up, m_w_ffn2_down, m_g_final, v_g_ffn1, v_w_ffn1_gate, v_w_ffn1_up, v_w_ffn1_down, v_g_mix, v_w_in_ab, v_conv_w, v_conv_b, v_ln_a_g, v_ln_a_b, v_ln_v_g, v_ln_v_b, v_sp_w, v_sp_b, v_w_out_ab, v_w_qkv, v_w_o, v_g_ffn2, v_w_ffn2_gate, v_w_ffn2_up, v_w_ffn2_down, v_g_final):
    p = dict(locals())
    for name in HIDDEN_MAJOR:
        for pre in ('', 'm_', 'v_'):
            p[pre + name] = jnp.swapaxes(p[pre + name], 1, 2)
    back = lambda name, a: jnp.swapaxes(a, 1, 2) if name in HIDDEN_MAJOR else a
    n_seq, seq, d = x.shape
    t = n_seq * seq
    depth = g_ffn1.shape[0]
    core = lax.axis_index("c")
    chip = 2 * lax.axis_index("x") + lax.axis_index("y")
    xf = x.reshape(t, d)
    target = loss_target.reshape(t, d)

    items = []
    for name in BIG:
        for layer in range(p[name].shape[0]):
            items.append((name, layer))
    chip1 = chip.reshape(1).astype(jnp.int32)
    placed = {}
    for name in BIG:
        for layer, buf in enumerate(place_shard(p[name], chip1, BF16, "place_shard")):
            placed[(name, layer)] = buf
    first = [('w_ffn1_gate', 0), ('w_ffn1_up', 0)]
    gathered, (conv_w4,) = allgather_weights([placed[it] for it in first],
                                             place_shard(conv_w, chip1, F32, "place_conv_w"))
    wt = dict(zip(first, gathered))
    waiting = [it for it in _use_order(depth) if it not in wt]

    def riders(name):
        room, take = CARRY_WEIGHTS[name], []
        for it in list(waiting):
            if placed[it].size <= room:
                room -= placed[it].size
                take.append(it)
                waiting.remove(it)
        return (take, gather_carry([placed[it] for it in take])) if take else (take, None)

    def landed(take, carried):
        wt.update(zip(take, carried))

    def weight(it):
        if it not in wt:
            waiting.remove(it)
            (wt[it],), _ = allgather_weights([placed[it]], [])
        return wt[it]

    c_mix = conv_w4.shape[2] * N_CHIPS
    conv_full = jnp.transpose(conv_w4, (1, 0, 2)).reshape(CONV_WIDTH, c_mix)
    vec = lambda a: a.reshape(1, -1)
    sp_bt = sp_b[0].T
    sp_wt = jnp.transpose(sp_w[0], (0, 2, 1))
    d_ff = w_ffn1_gate.shape[2]
    n_in = w_in_ab.shape[2]
    n_qkv = w_qkv.shape[2] // 3

    saved = []
    xc = xf
    h = rmsnorm_fwd(xc, vec(g_ffn1[0]), "norm_first")
    for layer in range(depth):
        s = {}
        for half, (gn, wn) in enumerate((('g_ffn1', 'w_ffn1'), ('g_ffn2', 'w_ffn2'))):
            if half == 1:
                s['x_mix'], s['h_mix'] = xc, h
                if layer % 2 == 0:
                    w_in = weight(('w_in_ab', layer // 2))
                    take, carry = riders("mm_in")
                    (z,), got = colmm(h, [w_in], n_in, BF16, "mm_in", carry)
                    landed(take, got)
                    cat, a1 = mix_fwd(z, conv_full, conv_b, ln_a_g, ln_a_b, vec(ln_v_g), vec(ln_v_b), sp_w[0], sp_bt, seq)
                    s.update(z=z, cat=cat, a1=a1)
                    w_out = weight(('w_out_ab', layer // 2))
                    take, carry = riders("mm_out")
                    (xc, h), got = rowmm(cat, w_out, xc, 1.0, "mm_out", carry, vec(g_ffn2[layer]))
                    landed(take, got)
                else:
                    (qkv,), _ = colmm(h, [weight(('w_qkv', layer // 2))], n_qkv, BF16, "mm_qkv")
                    o, tot, cnt = attn_fwd(qkv, n_seq, seq)
                    s.update(qkv=qkv, o=o, tot=tot, cnt=cnt)
                    (xc, h), _ = rowmm(o, weight(('w_o', layer // 2)), xc, 1.0, "mm_o", None, vec(g_ffn2[layer]))
            s['x' + wn] = xc
            w_gate, w_up = weight((wn + '_gate', layer)), weight((wn + '_up', layer))
            take, carry = riders("ffn_gateup")
            (silu, udsilu, act), got = colmm(h, [w_gate, w_up], d_ff, BF16, "ffn_gateup", carry, swiglu=True)
            landed(take, got)
            s.update({'h' + wn: h, 'swiglu' + wn: (silu, udsilu), 'act' + wn: act})
            w_down = weight((wn + '_down', layer))
            take, carry = riders("ffn_down")
            following = g_mix[layer] if half == 0 else (g_ffn1[layer + 1] if layer + 1 < depth else None)
            (xc, h), got = rowmm(act, w_down, xc, 0.5, "ffn_down", carry, None if following is None else vec(following))
            landed(take, got)
        saved.append(s)

    loss8, dx, dxb, dg_final = loss_head(xc, vec(g_final), target)
    loss = lax.psum(loss8[0, 0], ("x", "y", "c"))

    gw = {}
    gs = {}
    core1 = core.reshape(1).astype(jnp.int32)
    ready = []
    part, recv = {}, {}

    def leaving():
        its = list(ready)
        ready.clear()
        halves = lambda a: a.reshape(N_CHIPS, 2, a.shape[1] // 2, a.shape[2])
        theirs = rs_exchange([halves(gw[it][1]) for it in its])
        sums = rs_add([halves(gw[it][0]) for it in its], theirs, core1, REDUCE_DTYPE, "rs_add")
        part.update(zip(its, sums))
        return its, send_carry(sums)

    for layer in reversed(range(depth)):
        s = saved[layer]
        for half, (gn, wn) in reversed(list(enumerate((('g_ffn1', 'w_ffn1'), ('g_ffn2', 'w_ffn2'))))):
            wd = wt[(wn + '_down', layer)]
            dgate, dup = rowmm_t(dxb, wd, 0.5, BF16, "ffn_bwd_act", swiglu=s['swiglu' + wn])
            gw[(wn + '_down', layer)] = dw_row(s['act' + wn], dxb, 0.5, "ffn_dw_down")
            gw[(wn + '_gate', layer)], gw[(wn + '_up', layer)] = dw_col(s['h' + wn], [dgate, dup], N_CHIPS, d_ff,
                                                                        "ffn_dw_gateup", transposed=True)
            ready.extend([(wn + '_down', layer), (wn + '_gate', layer), (wn + '_up', layer)])
            its, carry = leaving()
            (dx, dxb, dg), got = colmm_t([dgate, dup], [wt[(wn + '_gate', layer)], wt[(wn + '_up', layer)]], d_ff,
                                         s['x' + wn], vec(p[gn][layer]), dx, "ffn_bwd_in", carry, transposed=True)
            recv.update(zip(its, got))
            gs[(gn, layer)] = dg
            if half == 1:
                if layer % 2 == 0:
                    i = layer // 2
                    w_out = wt[('w_out_ab', i)]
                    dcat = rowmm_t(dxb, w_out, 1.0, F32, "mm_out_t")
                    gw[('w_out_ab', i)] = dw_row(s['cat'], dxb, 1.0, "dw_out")
                    dz, da1, dcb, dlag, dlab, dlvg, dlvb, dspw, dspb = mix_bwd_point(
                        dcat, s['z'], s['a1'], ln_a_g, ln_a_b, vec(ln_v_g), vec(ln_v_b), sp_w[0], sp_wt, sp_bt, seq)
                    dz, dcw = mix_bwd_conv(dz, da1, s['z'], conv_full, seq)
                    gs.update({('conv_b', i): dcb, ('ln_a_g', i): dlag, ('ln_a_b', i): dlab, ('ln_v_g', i): dlvg,
                               ('ln_v_b', i): dlvb, ('sp_w', i): dspw, ('sp_b', i): dspb[:, :, 0], ('conv_w', i): dcw})
                    (gw[('w_in_ab', i)],) = dw_col(s['h_mix'], [dz], N_CHIPS, n_in, "dw_in")
                    ready.extend([('w_out_ab', i), ('w_in_ab', i)])
                    its, carry = leaving()
                    (dx, dxb, dg), got = colmm_t([dz], [wt[('w_in_ab', i)]], n_in, s['x_mix'], vec(g_mix[layer]), dx,
                                                 "mm_in_t", carry)
                    recv.update(zip(its, got))
                else:
                    i = layer // 2
                    w_o4 = wt[('w_o', i)]
                    do = rowmm_t(dxb, w_o4, 1.0, BF16, "mm_o_t")
                    gw[('w_o', i)] = dw_row(s['o'], dxb, 1.0, "dw_o")
                    dq, dk, dv = attn_bwd(s['qkv'], do, s['tot'], s['cnt'], n_seq, seq)
                    dqkv = jnp.concatenate([dq, dk, dv], axis=0)
                    (gw[('w_qkv', i)],) = dw_col(s['h_mix'], [dqkv], N_CHIPS, n_qkv, "dw_qkv")
                    ready.extend([('w_o', i), ('w_qkv', i)])
                    its, carry = leaving()
                    (dx, dxb, dg), got = colmm_t([dqkv], [wt[('w_qkv', i)]], n_qkv, s['x_mix'], vec(g_mix[layer]), dx,
                                                 "mm_qkv_t", carry)
                    recv.update(zip(its, got))
                gs[('g_mix', layer)] = dg
    grad_x = dx.reshape(x.shape)

    assert not ready and set(recv) == set(items)
    where = jnp.stack([chip, core]).astype(jnp.int32)
    fulls = []
    for name in BIG:
        its = [(name, layer) for layer in range(p[name].shape[0])]
        fulls.append(rs_sum([recv[it] for it in its], [part[it] for it in its], where, "rs_sum"))
    shared = rs_share(fulls)
    grads = {name: sh.reshape(p[name].shape) for name, sh in zip(BIG, shared)}

    stack = lambda name: jnp.concatenate([gs[(name, layer)].reshape((1,) + p[name].shape[1:]) for layer in range(p[name].shape[0])], axis=0)
    small_g = [stack(name) if name != 'g_final' else dg_final.reshape(p[name].shape) for name in SMALL]
    packed = _pack(small_g + [gs[('conv_w', 0)]])
    red = allreduce_small(packed)
    outs = _unpack(red, [p[name].shape for name in SMALL] + [(CONV_WIDTH, c_mix)])
    for name, g in zip(SMALL, outs[:-1]):
        grads[name] = g
    conv_g = outs[-1].reshape(CONV_WIDTH, N_CHIPS, c_mix // N_CHIPS)
    grads['conv_w'] = lax.dynamic_index_in_dim(conv_g, chip, axis=1, keepdims=False).reshape(conv_w.shape)

    delta, new_m, new_v = {}, {}, {}
    for name in BIG:
        shp = p[name].shape
        two = lambda a: a.reshape(shp[0] * shp[1], shp[2])
        dl, nm, nv = adamw(two(p[name]), two(grads[name]), two(p['m_' + name]), two(p['v_' + name]), "adamw")
        delta[name], new_m[name], new_v[name] = dl.reshape(shp), nm.reshape(shp), nv.reshape(shp)
    small_names = SMALL + ['conv_w']
    pk = lambda pre: _pack([p[pre + name] for name in small_names])
    dl, nm, nv = adamw(pk(''), _pack([grads[name] for name in small_names]), pk('m_'), pk('v_'), "adamw_small")
    shapes = [p[name].shape for name in small_names]
    for dst, val in ((delta, dl), (new_m, nm), (new_v, nv)):
        for name, a in zip(small_names, _unpack(val, shapes)):
            dst[name] = a

    return (loss, grad_x, *[back(n, d[n]) for d in (grads, delta, new_m, new_v) for n in WEIGHTS])
```

```python
import functools

import jax
import jax.numpy as jnp
from jax import lax
from jax.experimental import pallas as pl
from jax.experimental.pallas import tpu as pltpu

F32 = jnp.float32
BF16 = jnp.bfloat16
EPS = 1e-6
HEAD_DIM = 64
CONV_WIDTH = 31
CHUNK = 128
KBLK = 128
ATT_BLOCK = 256
ATT_LANES = 256
DW_TOKENS = 2048
CONV_ROWS = 64
MASKED = -1e30
STICK_GONE = -110.0
LANES = 128
HALO = 32
ADAM_LR, ADAM_B1, ADAM_B2, ADAM_EPS, ADAM_WD, ADAM_STEP = 0.001, 0.9, 0.999, 1e-08, 0.01, 10
VMEM_LIMIT = 56 * 1024 * 1024
MESH = pl.DeviceIdType.MESH
N_CHIPS = 4
N_DEV = 8
REDUCE_DTYPE = BF16


def _cparams(sem):
    return pltpu.CompilerParams(dimension_semantics=sem, vmem_limit_bytes=VMEM_LIMIT)


def _nt(a, b):
    return lax.dot_general(a, b, (((1,), (1,)), ((), ())), preferred_element_type=F32)


def _tn(a, b):
    return lax.dot_general(a, b, (((0,), (0,)), ((), ())), preferred_element_type=F32)


def _nn(a, b):
    return jnp.dot(a, b, preferred_element_type=F32)


def _sigmoid(x):
    return 0.5 * jnp.tanh(0.5 * x) + 0.5


def _tile(t, want):
    if t <= want:
        return t
    for cand in range(want - want % 8, 7, -8):
        if t % cand == 0:
            return cand
    raise ValueError((t, want))


def rmsnorm_fwd(x, g, name):
    t, d = x.shape
    tm = _tile(t, 512)

    def body(x_ref, g_ref, h_ref):
        xv = x_ref[...]
        r = lax.rsqrt(jnp.mean(xv * xv, axis=-1, keepdims=True) + EPS)
        h_ref[...] = (xv * r * g_ref[...]).astype(BF16)

    return pl.pallas_call(
        body, name=name, grid=(t // tm,),
        in_specs=[pl.BlockSpec((tm, d), lambda i: (i, 0)), pl.BlockSpec((1, d), lambda i: (0, 0))],
        out_specs=pl.BlockSpec((tm, d), lambda i: (i, 0)),
        out_shape=jax.ShapeDtypeStruct((t, d), BF16),
        compiler_params=_cparams(("parallel",)),
    )(x, g)


def colmm(h, ws, nu, out_dtype, name, carry=None, swiglu=False):
    t, k = h.shape
    j, nj = (ws[0].shape[0], ws[0].shape[1]) if swiglu else (ws[0].shape[0], ws[0].shape[2])
    per = nj // nu
    units = j * per
    tm = _tile(t, 1024)
    nw = len(ws)
    n_out = 3 if swiglu else nw

    def body(*refs):
        h_ref = refs[0]
        hv = h_ref[...]
        if swiglu:
            silu_ref, udsilu_ref, act_ref = refs[1 + nw:]
            gv = _nt(hv, refs[1][0])
            uv = _nt(hv, refs[2][0])
            s = _sigmoid(gv)
            silu = gv * s
            silu_ref[0] = silu.astype(out_dtype)
            udsilu_ref[0] = (uv * (s + silu * (1.0 - s))).astype(out_dtype)
            act_ref[0] = (silu * uv).astype(out_dtype)
            return
        for n in range(nw):
            res = _nn(hv, refs[1 + n][0]).astype(out_dtype)
            for u in range(per):
                refs[1 + nw + n][u] = res[:, u * nu:(u + 1) * nu]

    assert not swiglu or (nw == 2 and per == 1)
    w_spec = pl.BlockSpec((1, nj, k) if swiglu else (1, k, nj), lambda s, i: (s, 0, 0))
    o_spec = pl.BlockSpec((per, tm, nu), lambda s, i: (s, i, 0))
    return _call(
        body, name=name, grid=(j, t // tm),
        in_specs=[pl.BlockSpec((tm, k), lambda s, i: (i, 0))] + [w_spec] * nw,
        out_specs=[o_spec] * n_out,
        out_shape=[jax.ShapeDtypeStruct((units, t, nu), out_dtype)] * n_out,
        args=[h, *ws], sem=("parallel", "parallel"), carry=carry)


def rowmm(a, w, resid, scale, name, carry=None, norm_g=None):
    u_n, t, ku = a.shape
    n = w.shape[2]
    tm = _tile(t, 512)

    def body(a_ref, w_ref, r_ref, *rest):
        acc = jnp.zeros((tm, n), F32)
        for u in range(u_n):
            acc = acc + _nn(a_ref[u], w_ref[u])
        out = r_ref[...] + scale * acc
        if norm_g is None:
            (o_ref,) = rest
        else:
            g_ref, o_ref, h_ref = rest
            r = lax.rsqrt(jnp.mean(out * out, axis=-1, keepdims=True) + EPS)
            h_ref[...] = (out * r * g_ref[...]).astype(BF16)
        o_ref[...] = out

    row = pl.BlockSpec((tm, n), lambda i: (i, 0))
    normed = norm_g is not None
    outs, carried = _call(
        body, name=name, grid=(t // tm,),
        in_specs=[pl.BlockSpec((u_n, tm, ku), lambda i: (0, i, 0)), pl.BlockSpec((u_n, ku, n), lambda i: (0, 0, 0)),
                  row] + [pl.BlockSpec((1, n), lambda i: (0, 0))] * normed,
        out_specs=[row] + [row] * normed,
        out_shape=[jax.ShapeDtypeStruct((t, n), F32)] + [jax.ShapeDtypeStruct((t, n), BF16)] * normed,
        args=[a, w, resid] + [norm_g] * normed, sem=("parallel",), carry=carry)
    return (outs[0], outs[1] if normed else None), carried


def rowmm_t(dyb, w, scale, out_dtype, name, swiglu=None):
    t, n = dyb.shape
    u_n, ku, _ = w.shape
    tm = _tile(t, 512)

    if swiglu is None:
        def body(dy_ref, w_ref, o_ref):
            o_ref[0] = (scale * _nt(dy_ref[...], w_ref[0])).astype(out_dtype)

        return pl.pallas_call(
            body, name=name, grid=(u_n, t // tm),
            in_specs=[pl.BlockSpec((tm, n), lambda u, i: (i, 0)), pl.BlockSpec((1, ku, n), lambda u, i: (u, 0, 0))],
            out_specs=pl.BlockSpec((1, tm, ku), lambda u, i: (u, i, 0)),
            out_shape=jax.ShapeDtypeStruct((u_n, t, ku), out_dtype),
            compiler_params=_cparams(("parallel", "parallel")),
        )(dyb, w)

    def body(dy_ref, w_ref, silu_ref, udsilu_ref, dg_ref, du_ref):
        dy = dy_ref[...]
        for u in range(u_n):
            dact = scale * _nt(dy, w_ref[u])
            dg_ref[u] = (dact * udsilu_ref[u].astype(F32)).astype(BF16)
            du_ref[u] = (dact * silu_ref[u].astype(F32)).astype(BF16)

    blk = pl.BlockSpec((u_n, tm, ku), lambda i: (0, i, 0))
    return pl.pallas_call(
        body, name=name, grid=(t // tm,),
        in_specs=[pl.BlockSpec((tm, n), lambda i: (i, 0)), pl.BlockSpec((u_n, ku, n), lambda i: (0, 0, 0)), blk, blk],
        out_specs=[blk] * 2, out_shape=[jax.ShapeDtypeStruct((u_n, t, ku), BF16)] * 2,
        compiler_params=_cparams(("parallel",)),
    )(dyb, w, *swiglu)


def colmm_t(dzs, ws, nu, x, g, dy_in, name, carry=None, transposed=False):
    t, k = x.shape
    j, nj = (ws[0].shape[0], ws[0].shape[1]) if transposed else (ws[0].shape[0], ws[0].shape[2])
    per = nj // nu
    units = j * per
    nw = len(ws)
    tm = _tile(t, 512)
    assert not transposed or per == 1

    def body(*refs):
        dz_refs = refs[:nw]
        w_refs = refs[nw:2 * nw]
        x_ref, g_ref, dy_ref, dx_ref, dxb_ref, dg_ref = refs[2 * nw:]
        i = pl.program_id(0)
        dh = jnp.zeros((tm, k), F32)
        for n in range(nw):
            for u in range(units):
                if transposed:
                    dh = dh + _nn(dz_refs[n][u], w_refs[n][u])
                else:
                    wv = w_refs[n][u // per, :, (u % per) * nu:(u % per + 1) * nu]
                    dh = dh + _nt(dz_refs[n][u], wv)
        xv = x_ref[...]
        gv = g_ref[...]
        r = lax.rsqrt(jnp.mean(xv * xv, axis=-1, keepdims=True) + EPS)
        uu = dh * gv
        dx = dy_ref[...] + r * uu - xv * (r * r * r * jnp.mean(uu * xv, axis=-1, keepdims=True))
        dx_ref[...] = dx
        dxb_ref[...] = dx.astype(BF16)
        part = jnp.sum(dh * (xv * r), axis=0, keepdims=True)

        @pl.when(i == 0)
        def _():
            dg_ref[...] = part

        @pl.when(i > 0)
        def _():
            dg_ref[...] += part

    dz_spec = pl.BlockSpec((units, tm, nu), lambda i: (0, i, 0))
    w_spec = pl.BlockSpec((j, nj, k) if transposed else (j, k, nj), lambda i: (0, 0, 0))
    row = pl.BlockSpec((tm, k), lambda i: (i, 0))
    vec = pl.BlockSpec((1, k), lambda i: (0, 0))
    return _call(
        body, name=name, grid=(t // tm,),
        in_specs=[dz_spec] * nw + [w_spec] * nw + [row, vec, row],
        out_specs=[row, row, vec],
        out_shape=[jax.ShapeDtypeStruct((t, k), F32), jax.ShapeDtypeStruct((t, k), BF16),
                   jax.ShapeDtypeStruct((1, k), F32)],
        args=[*dzs, *ws, x, g, dy_in], sem=("arbitrary",), carry=carry)


def dw_col(h, dzs, j, nu, name, transposed=False):
    t, k = h.shape
    units = dzs[0].shape[0]
    per = units // j
    nw = len(dzs)
    tt = _tile(t, DW_TOKENS)
    assert not transposed or per == 1

    def body(*refs):
        h_ref = refs[0]
        s = pl.program_id(1)
        hv = h_ref[...]
        outs, copies = refs[1 + nw:1 + 2 * nw], refs[1 + 2 * nw:]

        @pl.when(s == 0)
        def _():
            for o_ref in outs:
                o_ref[...] = jnp.zeros_like(o_ref)

        for n in range(nw):
            if transposed:
                outs[n][0] += _tn(refs[1 + n][0], hv)
                continue
            for u in range(per):
                outs[n][0, :, u * nu:(u + 1) * nu] += _tn(hv, refs[1 + n][u])

        @pl.when(s == pl.num_programs(1) - 1)
        def _():
            for o_ref, c_ref in zip(outs, copies):
                c_ref[...] = o_ref[...].astype(REDUCE_DTYPE)

    shard = (nu, k) if transposed else (k, per * nu)
    o_spec = pl.BlockSpec((1,) + shard, lambda u, s: (u, 0, 0))
    res = pl.pallas_call(
        body, name=name, grid=(j, t // tt),
        in_specs=[pl.BlockSpec((tt, k), lambda u, s: (s, 0))] + [pl.BlockSpec((per, tt, nu), lambda u, s: (u, s, 0))] * nw,
        out_specs=[o_spec] * (2 * nw),
        out_shape=[jax.ShapeDtypeStruct((j,) + shard, F32)] * nw + [jax.ShapeDtypeStruct((j,) + shard, REDUCE_DTYPE)] * nw,
        compiler_params=_cparams(("parallel", "arbitrary")),
    )(h, *dzs)
    return list(zip(res[:nw], res[nw:]))


def dw_row(a, dyb, scale, name):
    u_n, t, ku = a.shape
    n = dyb.shape[1]
    tt = _tile(t, DW_TOKENS)

    def body(a_ref, dy_ref, o_ref, c_ref):
        @pl.when(pl.program_id(1) == 0)
        def _():
            o_ref[...] = jnp.zeros_like(o_ref)

        o_ref[0] += scale * _tn(a_ref[0], dy_ref[...])

        @pl.when(pl.program_id(1) == pl.num_programs(1) - 1)
        def _():
            c_ref[...] = o_ref[...].astype(REDUCE_DTYPE)

    o_spec = pl.BlockSpec((1, ku, n), lambda u, s: (u, 0, 0))
    return tuple(pl.pallas_call(
        body, name=name, grid=(u_n, t // tt),
        in_specs=[pl.BlockSpec((1, tt, ku), lambda u, s: (u, s, 0)), pl.BlockSpec((tt, n), lambda u, s: (s, 0))],
        out_specs=[o_spec, o_spec],
        out_shape=[jax.ShapeDtypeStruct((u_n, ku, n), F32), jax.ShapeDtypeStruct((u_n, ku, n), REDUCE_DTYPE)],
        compiler_params=_cparams(("parallel", "arbitrary")),
    )(a, dyb))


def loss_head(x, g, target):
    t, d = x.shape
    tm = _tile(t, 256)

    def body(x_ref, g_ref, t_ref, loss_ref, dx_ref, dxb_ref, dg_ref):
        i = pl.program_id(0)
        xv = x_ref[...]
        gv = g_ref[...]
        r = lax.rsqrt(jnp.mean(xv * xv, axis=-1, keepdims=True) + EPS)
        xh = xv * r
        err = xh * gv - t_ref[...]
        dy = err * (1.0 / d)
        uu = dy * gv
        dx = r * uu - xv * (r * r * r * jnp.mean(uu * xv, axis=-1, keepdims=True))
        dx_ref[...] = dx
        dxb_ref[...] = dx.astype(BF16)
        dg_part = jnp.sum(dy * xh, axis=0, keepdims=True)
        row = jnp.sum(err * err, axis=-1, keepdims=True) * (0.5 / d)
        l_part = jnp.zeros((8, LANES), F32) + jnp.sum(row, axis=0, keepdims=True)

        @pl.when(i == 0)
        def _():
            dg_ref[...] = dg_part
            loss_ref[...] = l_part

        @pl.when(i > 0)
        def _():
            dg_ref[...] += dg_part
            loss_ref[...] += l_part

    row = pl.BlockSpec((tm, d), lambda i: (i, 0))
    vec = pl.BlockSpec((1, d), lambda i: (0, 0))
    return pl.pallas_call(
        body, name="loss_head", grid=(t // tm,),
        in_specs=[row, vec, row],
        out_specs=[pl.BlockSpec((8, LANES), lambda i: (0, 0)), row, row, vec],
        out_shape=[jax.ShapeDtypeStruct((8, LANES), F32), jax.ShapeDtypeStruct((t, d), F32),
                   jax.ShapeDtypeStruct((t, d), BF16), jax.ShapeDtypeStruct((1, d), F32)],
        compiler_params=_cparams(("arbitrary",)),
    )(x, g, target)


def _split(v):
    hi = v.astype(BF16)
    lo = (v - hi.astype(F32)).astype(BF16)
    return hi, lo


def _keysums(v, m_ext):
    hi, lo = _split(v)
    outs = []
    for j in range(v.shape[1] // KBLK):
        sl = slice(j * KBLK, (j + 1) * KBLK)
        cs = _nn(jnp.concatenate([hi[:, sl], lo[:, sl]], axis=1), m_ext)
        outs.append((cs[:, :KBLK], cs[:, KBLK:]))
    return outs


def _softplus_parts(z):
    sp = jnp.maximum(z, 0.0) + jnp.log(1.0 + jnp.exp(-jnp.abs(z)))
    return sp, z - sp


def _sum_matrices():
    r = lax.broadcasted_iota(jnp.int32, (2 * KBLK, 2 * KBLK), 0) % KBLK
    c = lax.broadcasted_iota(jnp.int32, (2 * KBLK, 2 * KBLK), 1)
    suffix = jnp.where((r > c) | (c >= KBLK), 1.0, 0.0).astype(BF16)
    prefix = jnp.where((r <= c) | (c >= KBLK), 1.0, 0.0).astype(BF16)
    return suffix, prefix


def _att_geometry(qkv, seq):
    upp = qkv.shape[0] // 3
    bq = min(ATT_BLOCK, seq)
    per_unit = (2 * LANES) // ATT_LANES
    return upp, bq, seq // bq, bq // KBLK, per_unit, upp * per_unit, ATT_LANES // HEAD_DIM


def _head_lanes(rows, heads):
    lane = lax.broadcasted_iota(jnp.int32, (rows, ATT_LANES), 1)
    return [(lane >= HEAD_DIM * h) & (lane < HEAD_DIM * (h + 1)) for h in range(heads)]


def attn_fwd(qkv, n_seq, seq):
    t = qkv.shape[1]
    upp, bq, nq, nsub, per_unit, groups, heads = _att_geometry(qkv, seq)
    suffix_m, _ = _sum_matrices()

    def body(q_ref, k_ref, v_ref, m_ref, o_ref, tot_ref, cnt_ref):
        qi = pl.program_id(2)
        step_id = (pl.program_id(0) * groups + pl.program_id(1)) * nq + qi
        in_head = _head_lanes(bq, heads)
        only = lambda v, h: jnp.where(in_head[h], v, jnp.zeros_like(v))
        q_all = q_ref[0] * jnp.asarray(HEAD_DIM ** -0.5, BF16)
        qs = [only(q_all, h) for h in range(heads)]
        m_ext = m_ref[...]
        row = lax.broadcasted_iota(jnp.int32, (bq, bq), 0)
        col = lax.broadcasted_iota(jnp.int32, (bq, bq), 1)
        diag_mask = col < row

        def block(kj, carry, mask):
            off = pl.multiple_of(kj * bq, bq)
            k_all = k_ref[0, pl.ds(off, bq), :]
            v_all = v_ref[0, pl.ds(off, bq), :]
            rems, acc = carry
            out = []
            for h in range(heads):
                rem = rems[h]
                z = _nt(qs[h], k_all)
                if mask is not None:
                    z = jnp.where(mask, z, MASKED)
                sp, ls = _softplus_parts(z)
                sums = _keysums(-sp, m_ext)
                parts = [None] * nsub
                for j in reversed(range(nsub)):
                    suf, total = sums[j]
                    parts[j] = jnp.exp(ls[:, j * KBLK:(j + 1) * KBLK] + suf + rem)
                    rem = rem + total
                a = jnp.concatenate(parts, axis=1)
                acc = acc + _nn(a.astype(BF16), only(v_all, h))
                out.append(rem)
            return tuple(out), acc

        def most_left(c):
            return functools.reduce(jnp.maximum, [jnp.max(r) for r in c[0]])

        def more(s):
            return (s[0] < qi) & (s[1] > STICK_GONE)

        def step(s):
            c = block(qi - 1 - s[0], s[2], None)
            return s[0] + 1, most_left(c), c

        zero = jnp.zeros((bq, LANES), F32)
        carry = block(qi, ((zero,) * heads, jnp.zeros((bq, ATT_LANES), F32)), diag_mask)
        n_left, _, (rems, acc) = lax.while_loop(more, step, (jnp.int32(0), most_left(carry), carry))
        o_ref[0] = acc.astype(BF16)
        first = lax.broadcasted_iota(jnp.int32, (bq, LANES), 1) < HEAD_DIM
        tot_ref[...] = jnp.concatenate([jnp.where(first, rems[h], rems[h + 1]) for h in range(0, heads, 2)], axis=1)
        cnt_ref[step_id] = n_left.astype(F32)

    qblk = lambda b, g, i: (g // per_unit, b * nq + i, g % per_unit)
    return pl.pallas_call(
        body, name="attn_fwd", grid=(n_seq, groups, nq),
        in_specs=[pl.BlockSpec((1, bq, ATT_LANES), qblk),
                  pl.BlockSpec((1, seq, ATT_LANES), lambda b, g, i: (upp + g // per_unit, b, g % per_unit)),
                  pl.BlockSpec((1, seq, ATT_LANES), lambda b, g, i: (2 * upp + g // per_unit, b, g % per_unit)),
                  pl.BlockSpec((2 * KBLK, 2 * KBLK), lambda b, g, i: (0, 0))],
        out_specs=[pl.BlockSpec((1, bq, ATT_LANES), qblk),
                   pl.BlockSpec((bq, ATT_LANES), lambda b, g, i: (b * nq + i, g)),
                   pl.BlockSpec(memory_space=pltpu.SMEM)],
        out_shape=[jax.ShapeDtypeStruct((upp, t, 2 * LANES), BF16), jax.ShapeDtypeStruct((t, upp * 2 * LANES), F32),
                   jax.ShapeDtypeStruct((n_seq * groups * nq,), F32)],
        compiler_params=_cparams(("arbitrary", "arbitrary", "arbitrary")),
    )(qkv, qkv, qkv, suffix_m)


def attn_bwd(qkv, do, tot, cnt, n_seq, seq):
    t = qkv.shape[1]
    upp, bq, nq, nsub, per_unit, groups, heads = _att_geometry(qkv, seq)
    _, prefix_m = _sum_matrices()
    scale = HEAD_DIM ** -0.5

    def body(q_ref, k_ref, v_ref, do_ref, tot_ref, m_ref, cnt_ref, dq_ref, dk_ref, dv_ref, dk_acc, dv_acc):
        qi = pl.program_id(2)
        step_id = (pl.program_id(0) * groups + pl.program_id(1)) * nq + qi
        n_left = jnp.clip(cnt_ref[step_id].astype(jnp.int32), 0, qi)
        in_head = _head_lanes(bq, heads)
        only = lambda v, h: jnp.where(in_head[h], v, jnp.zeros_like(v))
        q_all = q_ref[0] * jnp.asarray(scale, BF16)
        do_all = do_ref[0]
        qs = [only(q_all, h) for h in range(heads)]
        dos = [only(do_all, h) for h in range(heads)]
        first = lax.broadcasted_iota(jnp.int32, (bq, LANES), 1) < HEAD_DIM
        tots = []
        for h in range(0, heads, 2):
            both = tot_ref[:, h // 2 * LANES:(h // 2 + 1) * LANES]
            swapped = pltpu.roll(both, HEAD_DIM, 1)
            tots += [jnp.where(first, both, swapped), jnp.where(first, swapped, both)]
        m_ext = m_ref[...]
        row = lax.broadcasted_iota(jnp.int32, (bq, bq), 0)
        col = lax.broadcasted_iota(jnp.int32, (bq, bq), 1)
        diag_mask = col < row

        @pl.when(qi == 0)
        def _():
            dk_acc[...] = jnp.zeros_like(dk_acc)
            dv_acc[...] = jnp.zeros_like(dv_acc)

        def block(kj, carry, mask):
            off = pl.multiple_of(kj * bq, bq)
            k_all = k_ref[0, pl.ds(off, bq), :]
            v_all = v_ref[0, pl.ds(off, bq), :]
            pres, gpres, dq = carry
            dk_part = jnp.zeros((bq, ATT_LANES), F32)
            dv_part = jnp.zeros((bq, ATT_LANES), F32)
            pres_out, gpres_out = [], []
            for h in range(heads):
                pre, gpre = pres[h], gpres[h]
                z = _nt(qs[h], k_all)
                if mask is not None:
                    z = jnp.where(mask, z, MASKED)
                sp, ls = _softplus_parts(z)
                sums = _keysums(-sp, m_ext)
                parts = []
                for j in range(nsub):
                    pin, ptot = sums[j]
                    parts.append(jnp.exp(ls[:, j * KBLK:(j + 1) * KBLK] + (tots[h] - (pre + pin))))
                    pre = pre + ptot
                a = jnp.concatenate(parts, axis=1)
                g = a * _nt(dos[h], v_all)
                gsums = _keysums(g, m_ext)
                parts = []
                for j in range(nsub):
                    gin, gtot = gsums[j]
                    parts.append(gpre + gin)
                    gpre = gpre + gtot
                dz = g - jnp.exp(ls) * jnp.concatenate(parts, axis=1)
                dzb = dz.astype(BF16)
                dq = dq + _nn(dzb, only(k_all, h))
                dk_part = dk_part + _tn(dzb, qs[h])
                dv_part = dv_part + _tn(a.astype(BF16), dos[h])
                pres_out.append(pre)
                gpres_out.append(gpre)
            dk_acc[pl.ds(off, bq), :] += dk_part
            dv_acc[pl.ds(off, bq), :] += dv_part
            return tuple(pres_out), tuple(gpres_out), dq

        zero = jnp.zeros((bq, LANES), F32)
        carry = ((zero,) * heads, (zero,) * heads, jnp.zeros((bq, ATT_LANES), F32))
        carry = lax.fori_loop(qi - n_left, qi, lambda kj, c: block(kj, c, None), carry)
        carry = block(qi, carry, diag_mask)
        dq_ref[0] = (carry[2] * scale).astype(BF16)

        @pl.when(qi == nq - 1)
        def _():
            dk_ref[0] = dk_acc[...].astype(BF16)
            dv_ref[0] = dv_acc[...].astype(BF16)

    qblk = lambda b, g, i: (g // per_unit, b * nq + i, g % per_unit)
    kv_out = pl.BlockSpec((1, seq, ATT_LANES), lambda b, g, i: (g // per_unit, b, g % per_unit))
    shp = jax.ShapeDtypeStruct((upp, t, 2 * LANES), BF16)
    return pl.pallas_call(
        body, name="attn_bwd", grid=(n_seq, groups, nq),
        in_specs=[pl.BlockSpec((1, bq, ATT_LANES), qblk),
                  pl.BlockSpec((1, seq, ATT_LANES), lambda b, g, i: (upp + g // per_unit, b, g % per_unit)),
                  pl.BlockSpec((1, seq, ATT_LANES), lambda b, g, i: (2 * upp + g // per_unit, b, g % per_unit)),
                  pl.BlockSpec((1, bq, ATT_LANES), qblk),
                  pl.BlockSpec((bq, ATT_LANES), lambda b, g, i: (b * nq + i, g)),
                  pl.BlockSpec((2 * KBLK, 2 * KBLK), lambda b, g, i: (0, 0)),
                  pl.BlockSpec(memory_space=pltpu.SMEM)],
        out_specs=[pl.BlockSpec((1, bq, ATT_LANES), qblk), kv_out, kv_out],
        out_shape=[shp, shp, shp],
        scratch_shapes=[pltpu.VMEM((seq, ATT_LANES), F32), pltpu.VMEM((seq, ATT_LANES), F32)],
        compiler_params=_cparams(("parallel", "parallel", "arbitrary")),
    )(qkv, qkv, qkv, do, tot, prefix_m, cnt)


def _ln_stats(v):
    mu = jnp.mean(v, axis=-1, keepdims=True)
    vc = v - mu
    rstd = lax.rsqrt(jnp.mean(vc * vc, axis=-1, keepdims=True) + EPS)
    return vc * rstd, rstd


def _glu_into(a0_ref, av_ref, ag_ref, hv_ref, hg_ref, first):
    hv = hv_ref[0].astype(F32)
    hg = hg_ref[0].astype(F32)
    a0_ref[0:HALO, :] = jnp.where(first, 0.0, hv * _sigmoid(hg))
    av = av_ref[0].astype(F32)
    ag = ag_ref[0].astype(F32)
    a0_ref[HALO:, :] = av * _sigmoid(ag)


def _shifted_taps(ref, shifted_ref, tm, first):
    taps = []
    for b in range(8):
        offs = [o for o in range(first, first + CONV_WIDTH) if o % 8 == b]
        n_rows = max(offs) - b + tm
        shifted_ref[b, 0:n_rows, :] = ref[pl.ds(b, n_rows), :]
        taps += [(b, o - b, o - first) for o in offs]
    return taps


def _tril_mask():
    r = lax.broadcasted_iota(jnp.int32, (CHUNK, CHUNK), 0)
    c = lax.broadcasted_iota(jnp.int32, (CHUNK, CHUNK), 1)
    return c <= r


def mix_fwd(z, conv_w, conv_b, ln_a_g, ln_a_b, ln_v_g, ln_v_b, sp_w, sp_bt, seq):
    _, t, c = z.shape
    tm = _tile(seq, 512)
    tiles_per_seq = seq // tm
    groups = c // LANES
    hb = tm // HALO

    def body(av_ref, ag_ref, u_ref, v_ref, hv_ref, hg_ref, cw_ref, cb_ref, lag_ref, lab_ref, lvg_ref, lvb_ref,
             spw_ref, spb_ref, cat_ref, a1_ref, a0_ref, sh_ref):
        i = pl.program_id(0)
        _glu_into(a0_ref, av_ref, ag_ref, hv_ref, hg_ref, i % tiles_per_seq == 0)
        acc = jnp.zeros((tm, c), F32) + cb_ref[...]
        for b, ro, k in _shifted_taps(a0_ref, sh_ref, tm, HALO - (CONV_WIDTH - 1)):
            acc = acc + cw_ref[k:k + 1, :] * sh_ref[b, pl.ds(ro, tm), :]
        a1_ref[...] = acc
        xh, _ = _ln_stats(acc)
        a2 = xh * lag_ref[...] + lab_ref[...]
        a3 = (a2 * _sigmoid(a2)).astype(BF16)
        half = c // 2
        cat_ref[0] = a3[:, :half]
        cat_ref[1] = a3[:, half:]
        tril = _tril_mask()
        for g in range(groups):
            sl = slice(g * LANES, (g + 1) * LANES)
            xh, _ = _ln_stats(v_ref[0][:, sl].astype(F32))
            vn = (xh * lvg_ref[:, sl] + lvb_ref[:, sl]).astype(BF16)
            w = jnp.where(tril, spw_ref[g], 0.0).astype(BF16)
            bias = spb_ref[:, g:g + 1]
            for ch in range(tm // CHUNK):
                rows = slice(ch * CHUNK, (ch + 1) * CHUNK)
                vs = _nn(w, vn[rows]) + bias
                bo = (u_ref[0][rows, sl].astype(F32) * vs).astype(BF16)
                cat_ref[2 + (g * LANES) // half, rows, (g * LANES) % half:(g * LANES) % half + LANES] = bo

    unit = lambda u: pl.BlockSpec((1, tm, c), lambda i: (u, i, 0))
    halo = lambda u: pl.BlockSpec((1, HALO, c), lambda i: (u, jnp.maximum(i * hb - 1, 0), 0))
    vec = pl.BlockSpec((1, c), lambda i: (0, 0))
    return pl.pallas_call(
        body, name="mix_fwd", grid=(t // tm,),
        in_specs=[unit(0), unit(1), unit(2), unit(3), halo(0), halo(1),
                  pl.BlockSpec((CONV_WIDTH, c), lambda i: (0, 0)), vec, vec, vec, vec, vec,
                  pl.BlockSpec((groups, CHUNK, CHUNK), lambda i: (0, 0, 0)),
                  pl.BlockSpec((CHUNK, groups), lambda i: (0, 0))],
        out_specs=[pl.BlockSpec((4, tm, c // 2), lambda i: (0, i, 0)), pl.BlockSpec((tm, c), lambda i: (i, 0))],
        out_shape=[jax.ShapeDtypeStruct((4, t, c // 2), BF16), jax.ShapeDtypeStruct((t, c), F32)],
        scratch_shapes=[pltpu.VMEM((HALO + tm, c), F32), pltpu.VMEM((8, HALO + tm, c), F32)],
        compiler_params=_cparams(("parallel",)),
    )(z, z, z, z, z, z, conv_w, conv_b, ln_a_g, ln_a_b, ln_v_g, ln_v_b, sp_w, sp_bt)


def mix_bwd_point(dcat, z, a1, ln_a_g, ln_a_b, ln_v_g, ln_v_b, sp_w, sp_wt, sp_bt, seq):
    _, t, c = z.shape
    tm = _tile(seq, 512)
    groups = c // LANES
    half = c // 2

    def body(dc_ref, u_ref, v_ref, a1_ref, lag_ref, lab_ref, lvg_ref, lvb_ref, spw_ref, spwt_ref, spb_ref,
             dz_ref, da1_ref, dcb_ref, dlag_ref, dlab_ref, dlvg_ref, dlvb_ref, dspw_ref, dspb_ref):
        i = pl.program_id(0)
        last = pl.num_programs(0) - 1

        @pl.when(i == 0)
        def _():
            for r in (dcb_ref, dlag_ref, dlab_ref, dlvg_ref, dlvb_ref, dspw_ref, dspb_ref):
                r[...] = jnp.zeros_like(r)

        da3 = jnp.concatenate([dc_ref[0], dc_ref[1]], axis=-1)
        xh, rstd = _ln_stats(a1_ref[...])
        a2 = xh * lag_ref[...] + lab_ref[...]
        s = _sigmoid(a2)
        da2 = da3 * (s * (1.0 + a2 * (1.0 - s)))
        dlag_ref[...] += jnp.sum(da2 * xh, axis=0, keepdims=True)
        dlab_ref[...] += jnp.sum(da2, axis=0, keepdims=True)
        dxh = da2 * lag_ref[...]
        da1 = rstd * (dxh - jnp.mean(dxh, axis=-1, keepdims=True) - xh * jnp.mean(dxh * xh, axis=-1, keepdims=True))
        da1_ref[...] = da1
        dcb_ref[...] += jnp.sum(da1, axis=0, keepdims=True)

        tril = _tril_mask()
        for g in range(groups):
            sl = slice(g * LANES, (g + 1) * LANES)
            xh, rstd = _ln_stats(v_ref[0][:, sl].astype(F32))
            lg = lvg_ref[:, sl]
            vnb = (xh * lg + lvb_ref[:, sl]).astype(BF16)
            w = jnp.where(tril, spw_ref[g], 0.0).astype(BF16)
            wt = jnp.where(tril.T, spwt_ref[g], 0.0).astype(BF16)
            bias = spb_ref[:, g:g + 1]
            dbo_all = dc_ref[2 + (g * LANES) // half][:, (g * LANES) % half:(g * LANES) % half + LANES]
            dvn_parts = []
            dw_acc = jnp.zeros((CHUNK, CHUNK), F32)
            db_acc = jnp.zeros((CHUNK, LANES), F32)
            for ch in range(tm // CHUNK):
                rows = slice(ch * CHUNK, (ch + 1) * CHUNK)
                vs = _nn(w, vnb[rows]) + bias
                dbo = dbo_all[rows]
                uv = u_ref[0][rows, sl].astype(F32)
                dz_ref[0, rows, sl] = (dbo * vs).astype(BF16)
                dvs = dbo * uv
                dvsb = dvs.astype(BF16)
                dvn_parts.append(_nn(wt, dvsb))
                dw_acc = dw_acc + _nt(dvsb, vnb[rows])
                db_acc = db_acc + dvs
            dvn = jnp.concatenate(dvn_parts, axis=0)
            dspw_ref[g] += jnp.where(tril, dw_acc, 0.0)
            dspb_ref[g] += db_acc
            dlvg_ref[:, sl] += jnp.sum(dvn * xh, axis=0, keepdims=True)
            dlvb_ref[:, sl] += jnp.sum(dvn, axis=0, keepdims=True)
            dxh = dvn * lg
            dv = rstd * (dxh - jnp.mean(dxh, axis=-1, keepdims=True) - xh * jnp.mean(dxh * xh, axis=-1, keepdims=True))
            dz_ref[1, :, sl] = dv.astype(BF16)

        @pl.when(i == last)
        def _():
            for g in range(groups):
                dspb_ref[g] = jnp.zeros((CHUNK, LANES), F32) + jnp.sum(dspb_ref[g], axis=-1, keepdims=True)

    unit = lambda u: pl.BlockSpec((1, tm, c), lambda i: (u, i, 0))
    vec = pl.BlockSpec((1, c), lambda i: (0, 0))
    sq = pl.BlockSpec((groups, CHUNK, CHUNK), lambda i: (0, 0, 0))
    vshape = jax.ShapeDtypeStruct((1, c), F32)
    sshape = jax.ShapeDtypeStruct((groups, CHUNK, CHUNK), F32)
    return pl.pallas_call(
        body, name="mix_bwd_point", grid=(t // tm,),
        in_specs=[pl.BlockSpec((4, tm, half), lambda i: (0, i, 0)), unit(2), unit(3),
                  pl.BlockSpec((tm, c), lambda i: (i, 0)), vec, vec, vec, vec, sq, sq,
                  pl.BlockSpec((CHUNK, groups), lambda i: (0, 0))],
        out_specs=[pl.BlockSpec((2, tm, c), lambda i: (1, i, 0)), pl.BlockSpec((tm, c), lambda i: (i, 0)),
                   vec, vec, vec, vec, vec, sq, sq],
        out_shape=[jax.ShapeDtypeStruct((4, t, c), BF16), jax.ShapeDtypeStruct((t, c), F32),
                   vshape, vshape, vshape, vshape, vshape, sshape, sshape],
        compiler_params=_cparams(("arbitrary",)),
    )(dcat, z, z, a1, ln_a_g, ln_a_b, ln_v_g, ln_v_b, sp_w, sp_wt, sp_bt)


def mix_bwd_conv(dz, da1, z, conv_w, seq):
    _, t, c = z.shape
    tm = _tile(seq, 512)
    tiles_per_seq = seq // tm
    hb = tm // HALO
    n_halo_blocks = t // HALO

    rc = _tile(tm, CONV_ROWS)

    def body(dz_in_ref, d_ref, dh_ref, av_ref, ag_ref, cw_ref, dz_ref, dcw_ref, d1_ref, sh_ref, part_ref):
        del dz_in_ref
        i = pl.program_id(0)

        @pl.when(i == 0)
        def _():
            part_ref[...] = jnp.zeros_like(part_ref)

        d1_ref[0:tm, :] = d_ref[...]
        d1_ref[tm:, :] = jnp.where((i + 1) % tiles_per_seq == 0, 0.0, dh_ref[...])
        taps = _shifted_taps(d1_ref, sh_ref, tm, 0)

        def chunk(ci, carry):
            r0 = pl.multiple_of(ci * rc, rc)
            av = av_ref[0, pl.ds(r0, rc), :].astype(F32)
            s = _sigmoid(ag_ref[0, pl.ds(r0, rc), :].astype(F32))
            a0 = av * s
            da0 = jnp.zeros((rc, c), F32)
            for b, ro, back in taps:
                k = CONV_WIDTH - 1 - back
                rows = sh_ref[b, pl.ds(r0 + ro, rc), :]
                da0 = da0 + cw_ref[k:k + 1, :] * rows
                prod = a0 * rows
                part_ref[k] += functools.reduce(lambda p, q: p + q, [prod[8 * r:8 * r + 8] for r in range(rc // 8)])
            dz_ref[0, pl.ds(r0, rc), :] = (da0 * s).astype(BF16)
            dz_ref[1, pl.ds(r0, rc), :] = (da0 * av * s * (1.0 - s)).astype(BF16)
            return carry

        lax.fori_loop(0, tm // rc, chunk, 0)

        @pl.when(i == pl.num_programs(0) - 1)
        def _():
            dcw_ref[...] = jnp.sum(part_ref[...], axis=1)

    unit = lambda u: pl.BlockSpec((1, tm, c), lambda i: (u, i, 0))
    return pl.pallas_call(
        body, name="mix_bwd_conv", grid=(t // tm,),
        in_specs=[pl.BlockSpec(memory_space=pl.ANY), pl.BlockSpec((tm, c), lambda i: (i, 0)),
                  pl.BlockSpec((HALO, c), lambda i: (jnp.minimum((i + 1) * hb, n_halo_blocks - 1), 0)),
                  unit(0), unit(1), pl.BlockSpec((CONV_WIDTH, c), lambda i: (0, 0))],
        out_specs=[pl.BlockSpec((2, tm, c), lambda i: (0, i, 0)), pl.BlockSpec((CONV_WIDTH, c), lambda i: (0, 0))],
        out_shape=[jax.ShapeDtypeStruct(dz.shape, BF16), jax.ShapeDtypeStruct((CONV_WIDTH, c), F32)],
        scratch_shapes=[pltpu.VMEM((tm + HALO, c), F32), pltpu.VMEM((8, tm + HALO, c), F32),
                        pltpu.VMEM((CONV_WIDTH, 8, c), F32)],
        input_output_aliases={0: 0},
        compiler_params=_cparams(("arbitrary",)),
    )(dz, da1, da1, z, z, conv_w)


CHIP_FLIPS = ((1, 0), (0, 1), (1, 1))
ANY = pl.BlockSpec(memory_space=pl.ANY)


def _place():
    return lax.axis_index("x"), lax.axis_index("y"), lax.axis_index("c")


def _flip(v, f):
    return 1 - v if f else v


def place_shard(w, chip, dtype, name):
    n_layers, r, cc = w.shape
    rb = _tile(r, 512)

    def body(chip_ref, w_ref, *o_refs):
        del chip_ref
        for layer, o_ref in enumerate(o_refs):
            o_ref[0] = w_ref[layer].astype(dtype)

    return pl.pallas_call(
        body, name=name,
        grid_spec=pltpu.PrefetchScalarGridSpec(
            num_scalar_prefetch=1, grid=(r // rb,),
            in_specs=[pl.BlockSpec((n_layers, rb, cc), lambda i, chip_ref: (0, i, 0))],
            out_specs=[pl.BlockSpec((1, rb, cc), lambda i, chip_ref: (chip_ref[0], i, 0))] * n_layers),
        out_shape=[jax.ShapeDtypeStruct((N_CHIPS, r, cc), dtype)] * n_layers,
        compiler_params=_cparams(("parallel",)),
    )(chip, w)


class Carry:
    def __init__(self, arrays, out_shapes, aliased, sem_shapes, start, finish):
        self.arrays, self.out_shapes, self.aliased, self.sem_shapes = list(arrays), list(out_shapes), aliased, list(sem_shapes)
        self.start, self.finish = start, finish


def _call(body, *, name, grid, in_specs, out_specs, out_shape, args, sem, scratch_shapes=(), carry=None):
    if carry is None:
        res = pl.pallas_call(body, name=name, grid=grid, in_specs=in_specs, out_specs=out_specs, out_shape=out_shape,
                             scratch_shapes=list(scratch_shapes), compiler_params=_cparams(sem))(*args)
        return list(res), []
    n_in, n_out, n_scr, nc = len(args), len(out_shape), len(scratch_shapes), len(carry.arrays)

    def full_body(*refs):
        ins, refs = refs[:n_in], refs[n_in:]
        c_ins, refs = refs[:nc], refs[nc:]
        outs, refs = refs[:n_out], refs[n_out:]
        c_outs, refs = refs[:nc], refs[nc:]
        scr, sems = refs[:n_scr], refs[n_scr:]
        first = functools.reduce(lambda a, b: a & b, [pl.program_id(d) == 0 for d in range(len(grid))])
        last = functools.reduce(lambda a, b: a & b, [pl.program_id(d) == grid[d] - 1 for d in range(len(grid))])

        @pl.when(first)
        def _():
            carry.start(c_ins, c_outs, sems)

        body(*ins, *outs, *scr)

        @pl.when(last)
        def _():
            carry.finish(c_ins, c_outs, sems)

    res = pl.pallas_call(
        full_body, name=name, grid=grid, in_specs=list(in_specs) + [ANY] * nc, out_specs=list(out_specs) + [ANY] * nc,
        out_shape=list(out_shape) + carry.out_shapes, scratch_shapes=list(scratch_shapes) + carry.sem_shapes,
        input_output_aliases={n_in + i: n_out + i for i in range(nc)} if carry.aliased else {},
        compiler_params=pltpu.CompilerParams(dimension_semantics=("arbitrary",) * len(grid), vmem_limit_bytes=VMEM_LIMIT,
                                             has_side_effects=True),
    )(*args, *carry.arrays)
    return list(res[:n_out]), list(res[n_out:])


def _gather_ops(shapes, whole):
    n = len(shapes)

    def rows(a, c):
        hr = shapes[a][1] // 2
        return pl.ds(pl.multiple_of(c * hr, 16), hr)

    def start(ins, outs, sems):
        ici_send, ici_recv = sems[0], sems[1]
        x, y, c = _place()
        k = 2 * x + y
        for a in range(n):
            for o, (fx, fy) in enumerate(CHIP_FLIPS):
                src = ins[a].at[k] if whole[a] else ins[a].at[k, rows(a, c)]
                dst = outs[a].at[k] if whole[a] else outs[a].at[k, rows(a, c)]
                pltpu.make_async_remote_copy(
                    src_ref=src, dst_ref=dst, send_sem=ici_send.at[3 * a + o], recv_sem=ici_recv.at[3 * a + o],
                    device_id=(_flip(x, fx), _flip(y, fy), c), device_id_type=MESH).start()

    def finish(ins, outs, sems):
        ici_send, ici_recv, d2d_send, d2d_recv = sems
        x, y, c = _place()
        k = 2 * x + y
        sibling = (x, y, 1 - c)

        def copy(ref, send, recv, a, o):
            return pltpu.make_async_remote_copy(src_ref=ref, dst_ref=ref, send_sem=send.at[3 * a + o],
                                                recv_sem=recv.at[3 * a + o], device_id=sibling, device_id_type=MESH)

        for a in range(n):
            for o, (fx, fy) in enumerate(CHIP_FLIPS):
                kk = 2 * _flip(x, fx) + _flip(y, fy)
                landed = outs[a].at[kk] if whole[a] else outs[a].at[kk, rows(a, c)]
                copy(landed, ici_send, ici_recv, a, o).wait_recv()
                if not whole[a]:
                    copy(landed, d2d_send, d2d_recv, a, o).start()
        for a in range(n):
            for o, (fx, fy) in enumerate(CHIP_FLIPS):
                kk = 2 * _flip(x, fx) + _flip(y, fy)
                mine = ins[a].at[k] if whole[a] else ins[a].at[k, rows(a, c)]
                copy(mine, ici_send, ici_recv, a, o).wait_send()
                if not whole[a]:
                    copy(outs[a].at[kk, rows(a, 1 - c)], d2d_send, d2d_recv, a, o).wait_recv()
                    copy(outs[a].at[kk, rows(a, c)], d2d_send, d2d_recv, a, o).wait_send()

    dma = pltpu.SemaphoreType.DMA
    return start, finish, [dma((3 * n,))] * 4


def gather_carry(bufs):
    start, finish, sems = _gather_ops([b.shape for b in bufs], [False] * len(bufs))
    return Carry(bufs, [jax.ShapeDtypeStruct(b.shape, b.dtype) for b in bufs], True, sems, start, finish)


def allgather_weights(shards, smalls):
    bufs = list(shards) + list(smalls)
    n = len(bufs)
    start, finish, sems = _gather_ops([b.shape for b in bufs], [False] * len(shards) + [True] * len(smalls))

    def body(*refs):
        start(refs[:n], refs[n:2 * n], refs[2 * n:])
        finish(refs[:n], refs[n:2 * n], refs[2 * n:])

    res = pl.pallas_call(
        body, name="allgather_weights", in_specs=[ANY] * n, out_specs=[ANY] * n,
        out_shape=[jax.ShapeDtypeStruct(b.shape, b.dtype) for b in bufs], scratch_shapes=sems,
        input_output_aliases={i: i for i in range(n)},
        compiler_params=pltpu.CompilerParams(has_side_effects=True),
    )(*bufs)
    return res[:len(shards)], res[len(shards):]


def rs_exchange(grads):
    n = len(grads)

    def body(*refs):
        ins, outs = refs[:n], refs[n:2 * n]
        send, recv = refs[2 * n:]
        x, y, c = _place()
        cps = []
        for a in range(n):
            cp = pltpu.make_async_remote_copy(
                src_ref=ins[a].at[:, 1 - c], dst_ref=outs[a], send_sem=send.at[a], recv_sem=recv.at[a],
                device_id=(x, y, 1 - c), device_id_type=MESH)
            cp.start()
            cps.append(cp)
        for cp in cps:
            cp.wait()

    dma = pltpu.SemaphoreType.DMA
    return pl.pallas_call(
        body, name="rs_exchange", in_specs=[ANY] * n, out_specs=[ANY] * n,
        out_shape=[jax.ShapeDtypeStruct((g.shape[0],) + g.shape[2:], g.dtype) for g in grads],
        scratch_shapes=[dma((n,)), dma((n,))],
        compiler_params=pltpu.CompilerParams(has_side_effects=True),
    )(*grads)


def rs_add(gs, sibs, core, out_dtype, name):
    n = len(gs)
    nk = gs[0].shape[0]

    def body(core_ref, *refs):
        del core_ref
        for a in range(n):
            refs[2 * n + a][0] = (refs[a][0, 0] + refs[n + a][0]).astype(out_dtype)

    halves = [g.shape[2:] for g in gs]
    return pl.pallas_call(
        body, name=name,
        grid_spec=pltpu.PrefetchScalarGridSpec(
            num_scalar_prefetch=1, grid=(nk,),
            in_specs=[pl.BlockSpec((1, 1) + h, lambda k, core_ref: (k, core_ref[0], 0, 0)) for h in halves]
            + [pl.BlockSpec((1,) + h, lambda k, core_ref: (k, 0, 0)) for h in halves],
            out_specs=[pl.BlockSpec((1,) + h, lambda k, core_ref: (k, 0, 0)) for h in halves]),
        out_shape=[jax.ShapeDtypeStruct((nk,) + h, out_dtype) for h in halves],
        compiler_params=_cparams(("parallel",)),
    )(core, *gs, *sibs)


def send_carry(parts):
    n = len(parts)

    def copies(ins, outs, sems):
        x, y, c = _place()
        for a in range(n):
            for o, (fx, fy) in enumerate(CHIP_FLIPS):
                kk = 2 * _flip(x, fx) + _flip(y, fy)
                yield pltpu.make_async_remote_copy(
                    src_ref=ins[a].at[kk], dst_ref=outs[a].at[o], send_sem=sems[0].at[3 * a + o],
                    recv_sem=sems[1].at[3 * a + o], device_id=(_flip(x, fx), _flip(y, fy), c), device_id_type=MESH)

    def start(ins, outs, sems):
        for cp in copies(ins, outs, sems):
            cp.start()

    def finish(ins, outs, sems):
        for cp in copies(ins, outs, sems):
            cp.wait()

    dma = pltpu.SemaphoreType.DMA
    return Carry(parts, [jax.ShapeDtypeStruct((3,) + p.shape[1:], p.dtype) for p in parts], False,
                 [dma((3 * n,)), dma((3 * n,))], start, finish)


def rs_sum(recvs, parts, where, name):
    n_layers = len(recvs)
    _, hr, cc = recvs[0].shape
    rb = _tile(hr, 256)

    def body(where_ref, *refs):
        del where_ref
        o_ref = refs[-1]
        for layer in range(n_layers):
            r_ref, p_ref = refs[layer], refs[n_layers + layer]
            o_ref[layer, 0] = ((p_ref[0].astype(F32) + r_ref[0].astype(F32)) + r_ref[1].astype(F32)) + r_ref[2].astype(F32)

    return pl.pallas_call(
        body, name=name,
        grid_spec=pltpu.PrefetchScalarGridSpec(
            num_scalar_prefetch=1, grid=(hr // rb,),
            in_specs=[pl.BlockSpec((3, rb, cc), lambda i, w_ref: (0, i, 0))] * n_layers
            + [pl.BlockSpec((1, rb, cc), lambda i, w_ref: (w_ref[0], i, 0))] * n_layers,
            out_specs=pl.BlockSpec((n_layers, 1, rb, cc), lambda i, w_ref: (0, w_ref[1], i, 0))),
        out_shape=jax.ShapeDtypeStruct((n_layers, 2, hr, cc), F32),
        compiler_params=_cparams(("parallel",)),
    )(where, *recvs, *parts)


def rs_share(fulls):
    n = len(fulls)

    def body(*refs):
        ins, outs = refs[:n], refs[n:2 * n]
        send, recv = refs[2 * n:]
        x, y, c = _place()
        cps = []
        for a in range(n):
            cp = pltpu.make_async_remote_copy(
                src_ref=ins[a].at[:, c], dst_ref=outs[a].at[:, c], send_sem=send.at[a], recv_sem=recv.at[a],
                device_id=(x, y, 1 - c), device_id_type=MESH)
            cp.start()
            cps.append(cp)
        for a in range(n):
            got = outs[a].at[:, 1 - c]
            pltpu.make_async_remote_copy(
                src_ref=got, dst_ref=got, send_sem=send.at[a], recv_sem=recv.at[a],
                device_id=(x, y, 1 - c), device_id_type=MESH).wait_recv()
        for cp in cps:
            cp.wait_send()

    dma = pltpu.SemaphoreType.DMA
    return pl.pallas_call(
        body, name="rs_share", in_specs=[ANY] * n, out_specs=[ANY] * n,
        out_shape=[jax.ShapeDtypeStruct(f.shape, f.dtype) for f in fulls],
        scratch_shapes=[dma((n,)), dma((n,))],
        input_output_aliases={i: i for i in range(n)},
        compiler_params=pltpu.CompilerParams(has_side_effects=True),
    )(*fulls)


def allreduce_small(v):
    r, w = v.shape

    def body(v_ref, o_ref, buf, send, recv, loc):
        x, y, c = _place()
        me = 4 * x + 2 * y + c
        mine = pltpu.make_async_copy(v_ref, buf.at[me], loc)
        mine.start()
        cps = []
        for o in range(1, N_DEV):
            fx, fy, fc = (o >> 2) & 1, (o >> 1) & 1, o & 1
            cp = pltpu.make_async_remote_copy(
                src_ref=v_ref, dst_ref=buf.at[me], send_sem=send.at[o - 1], recv_sem=recv.at[o - 1],
                device_id=(_flip(x, fx), _flip(y, fy), _flip(c, fc)), device_id_type=MESH)
            cp.start()
            cps.append(cp)
        for o in range(1, N_DEV):
            fx, fy, fc = (o >> 2) & 1, (o >> 1) & 1, o & 1
            peer = 4 * _flip(x, fx) + 2 * _flip(y, fy) + _flip(c, fc)
            pltpu.make_async_remote_copy(
                src_ref=v_ref, dst_ref=buf.at[peer], send_sem=send.at[o - 1], recv_sem=recv.at[o - 1],
                device_id=(x, y, c), device_id_type=MESH).wait_recv()
        for cp in cps:
            cp.wait_send()
        mine.wait()
        acc = buf[0]
        for d in range(1, N_DEV):
            acc = acc + buf[d]
        o_ref[...] = acc

    dma = pltpu.SemaphoreType.DMA
    vm = pl.BlockSpec(memory_space=pltpu.VMEM)
    return pl.pallas_call(
        body, name="allreduce_small", in_specs=[vm], out_specs=vm,
        out_shape=jax.ShapeDtypeStruct((r, w), F32),
        scratch_shapes=[pltpu.VMEM((N_DEV, r, w), F32), dma((N_DEV - 1,)), dma((N_DEV - 1,)), dma],
        compiler_params=pltpu.CompilerParams(has_side_effects=True, vmem_limit_bytes=VMEM_LIMIT),
    )(v)


def adamw(w, g, m, v, name):
    r, cc = w.shape
    rb = _tile(r, 256)

    def body(w_ref, g_ref, m_ref, v_ref, d_ref, nm_ref, nv_ref):
        gv = g_ref[...]
        nm = ADAM_B1 * m_ref[...] + (1.0 - ADAM_B1) * gv
        nv = ADAM_B2 * v_ref[...] + (1.0 - ADAM_B2) * (gv * gv)
        m_hat = nm / (1.0 - ADAM_B1 ** ADAM_STEP)
        v_hat = nv / (1.0 - ADAM_B2 ** ADAM_STEP)
        d_ref[...] = -ADAM_LR * (m_hat / (jnp.sqrt(v_hat) + ADAM_EPS) + ADAM_WD * w_ref[...])
        nm_ref[...] = nm
        nv_ref[...] = nv

    blk = pl.BlockSpec((rb, cc), lambda i: (i, 0))
    shp = jax.ShapeDtypeStruct((r, cc), F32)
    return pl.pallas_call(
        body, name=name, grid=(r // rb,), in_specs=[blk] * 4, out_specs=[blk] * 3, out_shape=[shp] * 3,
        compiler_params=_cparams(("parallel",)),
    )(w, g, m, v)


WEIGHTS = ['g_ffn1', 'w_ffn1_gate', 'w_ffn1_up', 'w_ffn1_down', 'g_mix', 'w_in_ab', 'conv_w', 'conv_b', 'ln_a_g',
           'ln_a_b', 'ln_v_g', 'ln_v_b', 'sp_w', 'sp_b', 'w_out_ab', 'w_qkv', 'w_o', 'g_ffn2', 'w_ffn2_gate',
           'w_ffn2_up', 'w_ffn2_down', 'g_final']
BIG = ['w_ffn1_gate', 'w_ffn1_up', 'w_ffn1_down', 'w_in_ab', 'w_out_ab', 'w_qkv', 'w_o', 'w_ffn2_gate', 'w_ffn2_up',
       'w_ffn2_down']
SMALL = ['g_ffn1', 'g_mix', 'g_ffn2', 'g_final', 'conv_b', 'ln_a_g', 'ln_a_b', 'ln_v_g', 'ln_v_b', 'sp_b', 'sp_w']
HIDDEN_MAJOR = ['w_ffn1_gate', 'w_ffn1_up', 'w_ffn2_gate', 'w_ffn2_up']


CARRY_WEIGHTS = {"ffn_gateup": 9.2e6, "ffn_down": 6.1e6, "mm_in": 5.9e6, "mm_out": 3.3e6}


def _use_order(depth):
    order = []
    for layer in range(depth):
        order += [('w_ffn1_gate', layer), ('w_ffn1_up', layer), ('w_ffn1_down', layer)]
        order += [('w_in_ab', layer // 2), ('w_out_ab', layer // 2)] if layer % 2 == 0 else [('w_qkv', layer // 2), ('w_o', layer // 2)]
        order += [('w_ffn2_gate', layer), ('w_ffn2_up', layer), ('w_ffn2_down', layer)]
    return order


def _rows(a):
    return a.reshape(-1, LANES)


def _pack(parts):
    v = jnp.concatenate([_rows(p) for p in parts], axis=0)
    pad = (-v.shape[0]) % 8
    return jnp.pad(v, ((0, pad), (0, 0)))


def _unpack(v, shapes):
    out, r = [], 0
    for s in shapes:
        n = 1
        for d in s:
            n *= d
        n //= LANES
        out.append(v[r:r + n].reshape(s))
        r += n
    return out


def kernel(x, g_ffn1, w_ffn1_gate, w_ffn1_up, w_ffn1_down, g_mix, w_in_ab, conv_w, conv_b, ln_a_g, ln_a_b, ln_v_g, ln_v_b, sp_w, sp_b, w_out_ab, w_qkv, w_o, g_ffn2, w_ffn2_gate, w_ffn2_up, w_ffn2_down, g_final, loss_target, m_g_ffn1, m_w_ffn1_gate, m_w_ffn1_up, m_w_ffn1_down, m_g_mix, m_w_in_ab, m_conv_w, m_conv_b, m_ln_a_g, m_ln_a_b, m_ln_v_g, m_ln_v_b, m_sp_w, m_sp_b, m_w_out_ab, m_w_qkv, m_w_o, m_g_ffn2, m_w_ffn2_gate, m_w_ffn2_up, m_w_ffn2_down, m_g_final, v_g_ffn1, v_w_ffn1_gate, v_w_ffn1_up, v_w_ffn1_down, v_g_mix, v_w_in_ab, v_conv_w, v_conv_b, v_ln_a_g, v_ln_a_b, v_ln_v_g, v_ln_v_b, v_sp_w, v_sp_b, v_w_out_ab, v_w_qkv, v_w_o, v_g_ffn2, v_w_ffn2_gate, v_w_ffn2_up, v_w_ffn2_down, v_g_final):
    p = dict(locals())
    for name in HIDDEN_MAJOR:
        for pre in ('', 'm_', 'v_'):
            p[pre + name] = jnp.swapaxes(p[pre + name], 1, 2)
    back = lambda name, a: jnp.swapaxes(a, 1, 2) if name in HIDDEN_MAJOR else a
    n_seq, seq, d = x.shape
    t = n_seq * seq
    depth = g_ffn1.shape[0]
    core = lax.axis_index("c")
    chip = 2 * lax.axis_index("x") + lax.axis_index("y")
    xf = x.reshape(t, d)
    target = loss_target.reshape(t, d)

    items = []
    for name in BIG:
        for layer in range(p[name].shape[0]):
            items.append((name, layer))
    chip1 = chip.reshape(1).astype(jnp.int32)
    placed = {}
    for name in BIG:
        for layer, buf in enumerate(place_shard(p[name], chip1, BF16, "place_shard")):
            placed[(name, layer)] = buf
    first = [('w_ffn1_gate', 0), ('w_ffn1_up', 0)]
    gathered, (conv_w4,) = allgather_weights([placed[it] for it in first],
                                             place_shard(conv_w, chip1, F32, "place_conv_w"))
    wt = dict(zip(first, gathered))
    waiting = [it for it in _use_order(depth) if it not in wt]

    def riders(name):
        room, take = CARRY_WEIGHTS[name], []
        for it in list(waiting):
            if placed[it].size <= room:
                room -= placed[it].size
                take.append(it)
                waiting.remove(it)
        return (take, gather_carry([placed[it] for it in take])) if take else (take, None)

    def landed(take, carried):
        wt.update(zip(take, carried))

    def weight(it):
        if it not in wt:
            waiting.remove(it)
            (wt[it],), _ = allgather_weights([placed[it]], [])
        return wt[it]

    c_mix = conv_w4.shape[2] * N_CHIPS
    conv_full = jnp.transpose(conv_w4, (1, 0, 2)).reshape(CONV_WIDTH, c_mix)
    vec = lambda a: a.reshape(1, -1)
    sp_bt = sp_b[0].T
    sp_wt = jnp.transpose(sp_w[0], (0, 2, 1))
    d_ff = w_ffn1_gate.shape[2]
    n_in = w_in_ab.shape[2]
    n_qkv = w_qkv.shape[2] // 3

    saved = []
    xc = xf
    h = rmsnorm_fwd(xc, vec(g_ffn1[0]), "norm_first")
    for layer in range(depth):
        s = {}
        for half, (gn, wn) in enumerate((('g_ffn1', 'w_ffn1'), ('g_ffn2', 'w_ffn2'))):
            if half == 1:
                s['x_mix'], s['h_mix'] = xc, h
                if layer % 2 == 0:
                    w_in = weight(('w_in_ab', layer // 2))
                    take, carry = riders("mm_in")
                    (z,), got = colmm(h, [w_in], n_in, BF16, "mm_in", carry)
                    landed(take, got)
                    cat, a1 = mix_fwd(z, conv_full, conv_b, ln_a_g, ln_a_b, vec(ln_v_g), vec(ln_v_b), sp_w[0], sp_bt, seq)
                    s.update(z=z, cat=cat, a1=a1)
                    w_out = weight(('w_out_ab', layer // 2))
                    take, carry = riders("mm_out")
                    (xc, h), got = rowmm(cat, w_out, xc, 1.0, "mm_out", carry, vec(g_ffn2[layer]))
                    landed(take, got)
                else:
                    (qkv,), _ = colmm(h, [weight(('w_qkv', layer // 2))], n_qkv, BF16, "mm_qkv")
                    o, tot, cnt = attn_fwd(qkv, n_seq, seq)
                    s.update(qkv=qkv, o=o, tot=tot, cnt=cnt)
                    (xc, h), _ = rowmm(o, weight(('w_o', layer // 2)), xc, 1.0, "mm_o", None, vec(g_ffn2[layer]))
            s['x' + wn] = xc
            w_gate, w_up = weight((wn + '_gate', layer)), weight((wn + '_up', layer))
            take, carry = riders("ffn_gateup")
            (silu, udsilu, act), got = colmm(h, [w_gate, w_up], d_ff, BF16, "ffn_gateup", carry, swiglu=True)
            landed(take, got)
            s.update({'h' + wn: h, 'swiglu' + wn: (silu, udsilu), 'act' + wn: act})
            w_down = weight((wn + '_down', layer))
            take, carry = riders("ffn_down")
            following = g_mix[layer] if half == 0 else (g_ffn1[layer + 1] if layer + 1 < depth else None)
            (xc, h), got = rowmm(act, w_down, xc, 0.5, "ffn_down", carry, None if following is None else vec(following))
            landed(take, got)
        saved.append(s)

    loss8, dx, dxb, dg_final = loss_head(xc, vec(g_final), target)
    loss = lax.psum(loss8[0, 0], ("x", "y", "c"))

    gw = {}
    gs = {}
    core1 = core.reshape(1).astype(jnp.int32)
    ready = []
    part, recv = {}, {}

    def leaving():
        its = list(ready)
        ready.clear()
        halves = lambda a: a.reshape(N_CHIPS, 2, a.shape[1] // 2, a.shape[2])
        theirs = rs_exchange([halves(gw[it][1]) for it in its])
        sums = rs_add([halves(gw[it][0]) for it in its], theirs, core1, REDUCE_DTYPE, "rs_add")
        part.update(zip(its, sums))
        return its, send_carry(sums)

    for layer in reversed(range(depth)):
        s = saved[layer]
        for half, (gn, wn) in reversed(list(enumerate((('g_ffn1', 'w_ffn1'), ('g_ffn2', 'w_ffn2'))))):
            wd = wt[(wn + '_down', layer)]
            dgate, dup = rowmm_t(dxb, wd, 0.5, BF16, "ffn_bwd_act", swiglu=s['swiglu' + wn])
            gw[(wn + '_down', layer)] = dw_row(s['act' + wn], dxb, 0.5, "ffn_dw_down")
            gw[(wn + '_gate', layer)], gw[(wn + '_up', layer)] = dw_col(s['h' + wn], [dgate, dup], N_CHIPS, d_ff,
                                                                        "ffn_dw_gateup", transposed=True)
            ready.extend([(wn + '_down', layer), (wn + '_gate', layer), (wn + '_up', layer)])
            its, carry = leaving()
            (dx, dxb, dg), got = colmm_t([dgate, dup], [wt[(wn + '_gate', layer)], wt[(wn + '_up', layer)]], d_ff,
                                         s['x' + wn], vec(p[gn][layer]), dx, "ffn_bwd_in", carry, transposed=True)
            recv.update(zip(its, got))
            gs[(gn, layer)] = dg
            if half == 1:
                if layer % 2 == 0:
                    i = layer // 2
                    w_out = wt[('w_out_ab', i)]
                    dcat = rowmm_t(dxb, w_out, 1.0, F32, "mm_out_t")
                    gw[('w_out_ab', i)] = dw_row(s['cat'], dxb, 1.0, "dw_out")
                    dz, da1, dcb, dlag, dlab, dlvg, dlvb, dspw, dspb = mix_bwd_point(
                        dcat, s['z'], s['a1'], ln_a_g, ln_a_b, vec(ln_v_g), vec(ln_v_b), sp_w[0], sp_wt, sp_bt, seq)
                    dz, dcw = mix_bwd_conv(dz, da1, s['z'], conv_full, seq)
                    gs.update({('conv_b', i): dcb, ('ln_a_g', i): dlag, ('ln_a_b', i): dlab, ('ln_v_g', i): dlvg,
                               ('ln_v_b', i): dlvb, ('sp_w', i): dspw, ('sp_b', i): dspb[:, :, 0], ('conv_w', i): dcw})
                    (gw[('w_in_ab', i)],) = dw_col(s['h_mix'], [dz], N_CHIPS, n_in, "dw_in")
                    ready.extend([('w_out_ab', i), ('w_in_ab', i)])
                    its, carry = leaving()
                    (dx, dxb, dg), got = colmm_t([dz], [wt[('w_in_ab', i)]], n_in, s['x_mix'], vec(g_mix[layer]), dx,
                                                 "mm_in_t", carry)
                    recv.update(zip(its, got))
                else:
                    i = layer // 2
                    w_o4 = wt[('w_o', i)]
                    do = rowmm_t(dxb, w_o4, 1.0, BF16, "mm_o_t")
                    gw[('w_o', i)] = dw_row(s['o'], dxb, 1.0, "dw_o")
                    dq, dk, dv = attn_bwd(s['qkv'], do, s['tot'], s['cnt'], n_seq, seq)
                    dqkv = jnp.concatenate([dq, dk, dv], axis=0)
                    (gw[('w_qkv', i)],) = dw_col(s['h_mix'], [dqkv], N_CHIPS, n_qkv, "dw_qkv")
                    ready.extend([('w_o', i), ('w_qkv', i)])
                    its, carry = leaving()
                    (dx, dxb, dg), got = colmm_t([dqkv], [wt[('w_qkv', i)]], n_qkv, s['x_mix'], vec(g_mix[layer]), dx,
                                                 "mm_qkv_t", carry)
                    recv.update(zip(its, got))
                gs[('g_mix', layer)] = dg
    grad_x = dx.reshape(x.shape)

    assert not ready and set(recv) == set(items)
    where = jnp.stack([chip, core]).astype(jnp.int32)
    fulls = []
    for name in BIG:
        its = [(name, layer) for layer in range(p[name].shape[0])]
        fulls.append(rs_sum([recv[it] for it in its], [part[it] for it in its], where, "rs_sum"))
    shared = rs_share(fulls)
    grads = {name: sh.reshape(p[name].shape) for name, sh in zip(BIG, shared)}

    stack = lambda name: jnp.concatenate([gs[(name, layer)].reshape((1,) + p[name].shape[1:]) for layer in range(p[name].shape[0])], axis=0)
    small_g = [stack(name) if name != 'g_final' else dg_final.reshape(p[name].shape) for name in SMALL]
    packed = _pack(small_g + [gs[('conv_w', 0)]])
    red = allreduce_small(packed)
    outs = _unpack(red, [p[name].shape for name in SMALL] + [(CONV_WIDTH, c_mix)])
    for name, g in zip(SMALL, outs[:-1]):
        grads[name] = g
    conv_g = outs[-1].reshape(CONV_WIDTH, N_CHIPS, c_mix // N_CHIPS)
    grads['conv_w'] = lax.dynamic_index_in_dim(conv_g, chip, axis=1, keepdims=False).reshape(conv_w.shape)

    delta, new_m, new_v = {}, {}, {}
    for name in BIG:
        shp = p[name].shape
        two = lambda a: a.reshape(shp[0] * shp[1], shp[2])
        dl, nm, nv = adamw(two(p[name]), two(grads[name]), two(p['m_' + name]), two(p['v_' + name]), "adamw")
        delta[name], new_m[name], new_v[name] = dl.reshape(shp), nm.reshape(shp), nv.reshape(shp)
    small_names = SMALL + ['conv_w']
    pk = lambda pre: _pack([p[pre + name] for name in small_names])
    dl, nm, nv = adamw(pk(''), _pack([grads[name] for name in small_names]), pk('m_'), pk('v_'), "adamw_small")
    shapes = [p[name].shape for name in small_names]
    for dst, val in ((delta, dl), (new_m, nm), (new_v, nv)):
        for name, a in zip(small_names, _unpack(val, shapes)):
            dst[name] = a

    return (loss, grad_x, *[back(n, d[n]) for d in (grads, delta, new_m, new_v) for n in WEIGHTS])
```

```python
import functools

import jax
import jax.numpy as jnp
from jax import lax
from jax.experimental import pallas as pl
from jax.experimental.pallas import tpu as pltpu

F32 = jnp.float32
BF16 = jnp.bfloat16
EPS = 1e-6
HEAD_DIM = 64
CONV_WIDTH = 31
CHUNK = 128
KBLK = 128
ATT_BLOCK = 256
ATT_LANES = 256
DW_TOKENS = 2048
CONV_ROWS = 64
MASKED = -1e30
STICK_GONE = -110.0
LANES = 128
HALO = 32
ADAM_LR, ADAM_B1, ADAM_B2, ADAM_EPS, ADAM_WD, ADAM_STEP = 0.001, 0.9, 0.999, 1e-08, 0.01, 10
VMEM_LIMIT = 56 * 1024 * 1024
MESH = pl.DeviceIdType.MESH
N_CHIPS = 4
N_DEV = 8
REDUCE_DTYPE = BF16


def _cparams(sem):
    return pltpu.CompilerParams(dimension_semantics=sem, vmem_limit_bytes=VMEM_LIMIT)


def _nt(a, b):
    return lax.dot_general(a, b, (((1,), (1,)), ((), ())), preferred_element_type=F32)


def _tn(a, b):
    return lax.dot_general(a, b, (((0,), (0,)), ((), ())), preferred_element_type=F32)


def _nn(a, b):
    return jnp.dot(a, b, preferred_element_type=F32)


def _sigmoid(x):
    return 0.5 * jnp.tanh(0.5 * x) + 0.5


def _tile(t, want):
    if t <= want:
        return t
    for cand in range(want - want % 8, 7, -8):
        if t % cand == 0:
            return cand
    raise ValueError((t, want))


def rmsnorm_fwd(x, g, name):
    t, d = x.shape
    tm = _tile(t, 512)

    def body(x_ref, g_ref, h_ref):
        xv = x_ref[...]
        r = lax.rsqrt(jnp.mean(xv * xv, axis=-1, keepdims=True) + EPS)
        h_ref[...] = (xv * r * g_ref[...]).astype(BF16)

    return pl.pallas_call(
        body, name=name, grid=(t // tm,),
        in_specs=[pl.BlockSpec((tm, d), lambda i: (i, 0)), pl.BlockSpec((1, d), lambda i: (0, 0))],
        out_specs=pl.BlockSpec((tm, d), lambda i: (i, 0)),
        out_shape=jax.ShapeDtypeStruct((t, d), BF16),
        compiler_params=_cparams(("parallel",)),
    )(x, g)


def colmm(h, ws, nu, out_dtype, name, carry=None, swiglu=False):
    t, k = h.shape
    j, nj = (ws[0].shape[0], ws[0].shape[1]) if swiglu else (ws[0].shape[0], ws[0].shape[2])
    per = nj // nu
    units = j * per
    tm = _tile(t, 1024)
    nw = len(ws)
    n_out = 3 if swiglu else nw

    def body(*refs):
        h_ref = refs[0]
        hv = h_ref[...]
        if swiglu:
            silu_ref, udsilu_ref, act_ref = refs[1 + nw:]
            gv = _nt(hv, refs[1][0])
            uv = _nt(hv, refs[2][0])
            s = _sigmoid(gv)
            silu = gv * s
            silu_ref[0] = silu.astype(out_dtype)
            udsilu_ref[0] = (uv * (s + silu * (1.0 - s))).astype(out_dtype)
            act_ref[0] = (silu * uv).astype(out_dtype)
            return
        for n in range(nw):
            res = _nn(hv, refs[1 + n][0]).astype(out_dtype)
            for u in range(per):
                refs[1 + nw + n][u] = res[:, u * nu:(u + 1) * nu]

    assert not swiglu or (nw == 2 and per == 1)
    w_spec = pl.BlockSpec((1, nj, k) if swiglu else (1, k, nj), lambda s, i: (s, 0, 0))
    o_spec = pl.BlockSpec((per, tm, nu), lambda s, i: (s, i, 0))
    return _call(
        body, name=name, grid=(j, t // tm),
        in_specs=[pl.BlockSpec((tm, k), lambda s, i: (i, 0))] + [w_spec] * nw,
        out_specs=[o_spec] * n_out,
        out_shape=[jax.ShapeDtypeStruct((units, t, nu), out_dtype)] * n_out,
        args=[h, *ws], sem=("parallel", "parallel"), carry=carry)


def rowmm(a, w, resid, scale, name, carry=None, norm_g=None):
    u_n, t, ku = a.shape
    n = w.shape[2]
    tm = _tile(t, 512)

    def body(a_ref, w_ref, r_ref, *rest):
        acc = jnp.zeros((tm, n), F32)
        for u in range(u_n):
            acc = acc + _nn(a_ref[u], w_ref[u])
        out = r_ref[...] + scale * acc
        if norm_g is None:
            (o_ref,) = rest
        else:
            g_ref, o_ref, h_ref = rest
            r = lax.rsqrt(jnp.mean(out * out, axis=-1, keepdims=True) + EPS)
            h_ref[...] = (out * r * g_ref[...]).astype(BF16)
        o_ref[...] = out

    row = pl.BlockSpec((tm, n), lambda i: (i, 0))
    normed = norm_g is not None
    outs, carried = _call(
        body, name=name, grid=(t // tm,),
        in_specs=[pl.BlockSpec((u_n, tm, ku), lambda i: (0, i, 0)), pl.BlockSpec((u_n, ku, n), lambda i: (0, 0, 0)),
                  row] + [pl.BlockSpec((1, n), lambda i: (0, 0))] * normed,
        out_specs=[row] + [row] * normed,
        out_shape=[jax.ShapeDtypeStruct((t, n), F32)] + [jax.ShapeDtypeStruct((t, n), BF16)] * normed,
        args=[a, w, resid] + [norm_g] * normed, sem=("parallel",), carry=carry)
    return (outs[0], outs[1] if normed else None), carried


def rowmm_t(dyb, w, scale, out_dtype, name, swiglu=None):
    t, n = dyb.shape
    u_n, ku, _ = w.shape
    tm = _tile(t, 512)

    if swiglu is None:
        def body(dy_ref, w_ref, o_ref):
            o_ref[0] = (scale * _nt(dy_ref[...], w_ref[0])).astype(out_dtype)

        return pl.pallas_call(
            body, name=name, grid=(u_n, t // tm),
            in_specs=[pl.BlockSpec((tm, n), lambda u, i: (i, 0)), pl.BlockSpec((1, ku, n), lambda u, i: (u, 0, 0))],
            out_specs=pl.BlockSpec((1, tm, ku), lambda u, i: (u, i, 0)),
            out_shape=jax.ShapeDtypeStruct((u_n, t, ku), out_dtype),
            compiler_params=_cparams(("parallel", "parallel")),
        )(dyb, w)

    def body(dy_ref, w_ref, silu_ref, udsilu_ref, dg_ref, du_ref):
        dy = dy_ref[...]
        for u in range(u_n):
            dact = scale * _nt(dy, w_ref[u])
            dg_ref[u] = (dact * udsilu_ref[u].astype(F32)).astype(BF16)
            du_ref[u] = (dact * silu_ref[u].astype(F32)).astype(BF16)

    blk = pl.BlockSpec((u_n, tm, ku), lambda i: (0, i, 0))
    return pl.pallas_call(
        body, name=name, grid=(t // tm,),
        in_specs=[pl.BlockSpec((tm, n), lambda i: (i, 0)), pl.BlockSpec((u_n, ku, n), lambda i: (0, 0, 0)), blk, blk],
        out_specs=[blk] * 2, out_shape=[jax.ShapeDtypeStruct((u_n, t, ku), BF16)] * 2,
        compiler_params=_cparams(("parallel",)),
    )(dyb, w, *swiglu)


def colmm_t(dzs, ws, nu, x, g, dy_in, name, carry=None, transposed=False):
    t, k = x.shape
    j, nj = (ws[0].shape[0], ws[0].shape[1]) if transposed else (ws[0].shape[0], ws[0].shape[2])
    per = nj // nu
    units = j * per
    nw = len(ws)
    tm = _tile(t, 512)
    assert not transposed or per == 1

    def body(*refs):
        dz_refs = refs[:nw]
        w_refs = refs[nw:2 * nw]
        x_ref, g_ref, dy_ref, dx_ref, dxb_ref, dg_ref = refs[2 * nw:]
        i = pl.program_id(0)
        dh = jnp.zeros((tm, k), F32)
        for n in range(nw):
            for u in range(units):
                if transposed:
                    dh = dh + _nn(dz_refs[n][u], w_refs[n][u])
                else:
                    wv = w_refs[n][u // per, :, (u % per) * nu:(u % per + 1) * nu]
                    dh = dh + _nt(dz_refs[n][u], wv)
        xv = x_ref[...]
        gv = g_ref[...]
        r = lax.rsqrt(jnp.mean(xv * xv, axis=-1, keepdims=True) + EPS)
        uu = dh * gv
        dx = dy_ref[...] + r * uu - xv * (r * r * r * jnp.mean(uu * xv, axis=-1, keepdims=True))
        dx_ref[...] = dx
        dxb_ref[...] = dx.astype(BF16)
        part = jnp.sum(dh * (xv * r), axis=0, keepdims=True)

        @pl.when(i == 0)
        def _():
            dg_ref[...] = part

        @pl.when(i > 0)
        def _():
            dg_ref[...] += part

    dz_spec = pl.BlockSpec((units, tm, nu), lambda i: (0, i, 0))
    w_spec = pl.BlockSpec((j, nj, k) if transposed else (j, k, nj), lambda i: (0, 0, 0))
    row = pl.BlockSpec((tm, k), lambda i: (i, 0))
    vec = pl.BlockSpec((1, k), lambda i: (0, 0))
    return _call(
        body, name=name, grid=(t // tm,),
        in_specs=[dz_spec] * nw + [w_spec] * nw + [row, vec, row],
        out_specs=[row, row, vec],
        out_shape=[jax.ShapeDtypeStruct((t, k), F32), jax.ShapeDtypeStruct((t, k), BF16),
                   jax.ShapeDtypeStruct((1, k), F32)],
        args=[*dzs, *ws, x, g, dy_in], sem=("arbitrary",), carry=carry)


def dw_col(h, dzs, j, nu, name, transposed=False):
    t, k = h.shape
    units = dzs[0].shape[0]
    per = units // j
    nw = len(dzs)
    tt = _tile(t, DW_TOKENS)
    assert not transposed or per == 1

    def body(*refs):
        h_ref = refs[0]
        s = pl.program_id(1)
        hv = h_ref[...]
        outs, copies = refs[1 + nw:1 + 2 * nw], refs[1 + 2 * nw:]

        @pl.when(s == 0)
        def _():
            for o_ref in outs:
                o_ref[...] = jnp.zeros_like(o_ref)

        for n in range(nw):
            if transposed:
                outs[n][0] += _tn(refs[1 + n][0], hv)
                continue
            for u in range(per):
                outs[n][0, :, u * nu:(u + 1) * nu] += _tn(hv, refs[1 + n][u])

        @pl.when(s == pl.num_programs(1) - 1)
        def _():
            for o_ref, c_ref in zip(outs, copies):
                c_ref[...] = o_ref[...].astype(REDUCE_DTYPE)

    shard = (nu, k) if transposed else (k, per * nu)
    o_spec = pl.BlockSpec((1,) + shard, lambda u, s: (u, 0, 0))
    res = pl.pallas_call(
        body, name=name, grid=(j, t // tt),
        in_specs=[pl.BlockSpec((tt, k), lambda u, s: (s, 0))] + [pl.BlockSpec((per, tt, nu), lambda u, s: (u, s, 0))] * nw,
        out_specs=[o_spec] * (2 * nw),
        out_shape=[jax.ShapeDtypeStruct((j,) + shard, F32)] * nw + [jax.ShapeDtypeStruct((j,) + shard, REDUCE_DTYPE)] * nw,
        compiler_params=_cparams(("parallel", "arbitrary")),
    )(h, *dzs)
    return list(zip(res[:nw], res[nw:]))


def dw_row(a, dyb, scale, name):
    u_n, t, ku = a.shape
    n = dyb.shape[1]
    tt = _tile(t, DW_TOKENS)

    def body(a_ref, dy_ref, o_ref, c_ref):
        @pl.when(pl.program_id(1) == 0)
        def _():
            o_ref[...] = jnp.zeros_like(o_ref)

        o_ref[0] += scale * _tn(a_ref[0], dy_ref[...])

        @pl.when(pl.program_id(1) == pl.num_programs(1) - 1)
        def _():
            c_ref[...] = o_ref[...].astype(REDUCE_DTYPE)

    o_spec = pl.BlockSpec((1, ku, n), lambda u, s: (u, 0, 0))
    return tuple(pl.pallas_call(
        body, name=name, grid=(u_n, t // tt),
        in_specs=[pl.BlockSpec((1, tt, ku), lambda u, s: (u, s, 0)), pl.BlockSpec((tt, n), lambda u, s: (s, 0))],
        out_specs=[o_spec, o_spec],
        out_shape=[jax.ShapeDtypeStruct((u_n, ku, n), F32), jax.ShapeDtypeStruct((u_n, ku, n), REDUCE_DTYPE)],
        compiler_params=_cparams(("parallel", "arbitrary")),
    )(a, dyb))


def loss_head(x, g, target):
    t, d = x.shape
    tm = _tile(t, 256)

    def body(x_ref, g_ref, t_ref, loss_ref, dx_ref, dxb_ref, dg_ref):
        i = pl.program_id(0)
        xv = x_ref[...]
        gv = g_ref[...]
        r = lax.rsqrt(jnp.mean(xv * xv, axis=-1, keepdims=True) + EPS)
        xh = xv * r
        err = xh * gv - t_ref[...]
        dy = err * (1.0 / d)
        uu = dy * gv
        dx = r * uu - xv * (r * r * r * jnp.mean(uu * xv, axis=-1, keepdims=True))
        dx_ref[...] = dx
        dxb_ref[...] = dx.astype(BF16)
        dg_part = jnp.sum(dy * xh, axis=0, keepdims=True)
        row = jnp.sum(err * err, axis=-1, keepdims=True) * (0.5 / d)
        l_part = jnp.zeros((8, LANES), F32) + jnp.sum(row, axis=0, keepdims=True)

        @pl.when(i == 0)
        def _():
            dg_ref[...] = dg_part
            loss_ref[...] = l_part

        @pl.when(i > 0)
        def _():
            dg_ref[...] += dg_part
            loss_ref[...] += l_part

    row = pl.BlockSpec((tm, d), lambda i: (i, 0))
    vec = pl.BlockSpec((1, d), lambda i: (0, 0))
    return pl.pallas_call(
        body, name="loss_head", grid=(t // tm,),
        in_specs=[row, vec, row],
        out_specs=[pl.BlockSpec((8, LANES), lambda i: (0, 0)), row, row, vec],
        out_shape=[jax.ShapeDtypeStruct((8, LANES), F32), jax.ShapeDtypeStruct((t, d), F32),
                   jax.ShapeDtypeStruct((t, d), BF16), jax.ShapeDtypeStruct((1, d), F32)],
        compiler_params=_cparams(("arbitrary",)),
    )(x, g, target)


def _split(v):
    hi = v.astype(BF16)
    lo = (v - hi.astype(F32)).astype(BF16)
    return hi, lo


def _keysums(v, m_ext):
    hi, lo = _split(v)
    outs = []
    for j in range(v.shape[1] // KBLK):
        sl = slice(j * KBLK, (j + 1) * KBLK)
        cs = _nn(jnp.concatenate([hi[:, sl], lo[:, sl]], axis=1), m_ext)
        outs.append((cs[:, :KBLK], cs[:, KBLK:]))
    return outs


def _softplus_parts(z):
    sp = jnp.maximum(z, 0.0) + jnp.log(1.0 + jnp.exp(-jnp.abs(z)))
    return sp, z - sp


def _sum_matrices():
    r = lax.broadcasted_iota(jnp.int32, (2 * KBLK, 2 * KBLK), 0) % KBLK
    c = lax.broadcasted_iota(jnp.int32, (2 * KBLK, 2 * KBLK), 1)
    suffix = jnp.where((r > c) | (c >= KBLK), 1.0, 0.0).astype(BF16)
    prefix = jnp.where((r <= c) | (c >= KBLK), 1.0, 0.0).astype(BF16)
    return suffix, prefix


def _att_geometry(qkv, seq):
    upp = qkv.shape[0] // 3
    bq = min(ATT_BLOCK, seq)
    per_unit = (2 * LANES) // ATT_LANES
    return upp, bq, seq // bq, bq // KBLK, per_unit, upp * per_unit, ATT_LANES // HEAD_DIM


def _head_lanes(rows, heads):
    lane = lax.broadcasted_iota(jnp.int32, (rows, ATT_LANES), 1)
    return [(lane >= HEAD_DIM * h) & (lane < HEAD_DIM * (h + 1)) for h in range(heads)]


def attn_fwd(qkv, n_seq, seq):
    t = qkv.shape[1]
    upp, bq, nq, nsub, per_unit, groups, heads = _att_geometry(qkv, seq)
    suffix_m, _ = _sum_matrices()

    def body(q_ref, k_ref, v_ref, m_ref, o_ref, tot_ref, cnt_ref):
        qi = pl.program_id(2)
        step_id = (pl.program_id(0) * groups + pl.program_id(1)) * nq + qi
        in_head = _head_lanes(bq, heads)
        only = lambda v, h: jnp.where(in_head[h], v, jnp.zeros_like(v))
        q_all = q_ref[0] * jnp.asarray(HEAD_DIM ** -0.5, BF16)
        qs = [only(q_all, h) for h in range(heads)]
        m_ext = m_ref[...]
        row = lax.broadcasted_iota(jnp.int32, (bq, bq), 0)
        col = lax.broadcasted_iota(jnp.int32, (bq, bq), 1)
        diag_mask = col < row

        def block(kj, carry, mask):
            off = pl.multiple_of(kj * bq, bq)
            k_all = k_ref[0, pl.ds(off, bq), :]
            v_all = v_ref[0, pl.ds(off, bq), :]
            rems, acc = carry
            out = []
            for h in range(heads):
                rem = rems[h]
                z = _nt(qs[h], k_all)
                if mask is not None:
                    z = jnp.where(mask, z, MASKED)
                sp, ls = _softplus_parts(z)
                sums = _keysums(-sp, m_ext)
                parts = [None] * nsub
                for j in reversed(range(nsub)):
                    suf, total = sums[j]
                    parts[j] = jnp.exp(ls[:, j * KBLK:(j + 1) * KBLK] + suf + rem)
                    rem = rem + total
                a = jnp.concatenate(parts, axis=1)
                acc = acc + _nn(a.astype(BF16), only(v_all, h))
                out.append(rem)
            return tuple(out), acc

        def most_left(c):
            return functools.reduce(jnp.maximum, [jnp.max(r) for r in c[0]])

        def more(s):
            return (s[0] < qi) & (s[1] > STICK_GONE)

        def step(s):
            c = block(qi - 1 - s[0], s[2], None)
            return s[0] + 1, most_left(c), c

        zero = jnp.zeros((bq, LANES), F32)
        carry = block(qi, ((zero,) * heads, jnp.zeros((bq, ATT_LANES), F32)), diag_mask)
        n_left, _, (rems, acc) = lax.while_loop(more, step, (jnp.int32(0), most_left(carry), carry))
        o_ref[0] = acc.astype(BF16)
        first = lax.broadcasted_iota(jnp.int32, (bq, LANES), 1) < HEAD_DIM
        tot_ref[...] = jnp.concatenate([jnp.where(first, rems[h], rems[h + 1]) for h in range(0, heads, 2)], axis=1)
        cnt_ref[step_id] = n_left.astype(F32)

    qblk = lambda b, g, i: (g // per_unit, b * nq + i, g % per_unit)
    return pl.pallas_call(
        body, name="attn_fwd", grid=(n_seq, groups, nq),
        in_specs=[pl.BlockSpec((1, bq, ATT_LANES), qblk),
                  pl.BlockSpec((1, seq, ATT_LANES), lambda b, g, i: (upp + g // per_unit, b, g % per_unit)),
                  pl.BlockSpec((1, seq, ATT_LANES), lambda b, g, i: (2 * upp + g // per_unit, b, g % per_unit)),
                  pl.BlockSpec((2 * KBLK, 2 * KBLK), lambda b, g, i: (0, 0))],
        out_specs=[pl.BlockSpec((1, bq, ATT_LANES), qblk),
                   pl.BlockSpec((bq, ATT_LANES), lambda b, g, i: (b * nq + i, g)),
                   pl.BlockSpec(memory_space=pltpu.SMEM)],
        out_shape=[jax.ShapeDtypeStruct((upp, t, 2 * LANES), BF16), jax.ShapeDtypeStruct((t, upp * 2 * LANES), F32),
                   jax.ShapeDtypeStruct((n_seq * groups * nq,), F32)],
        compiler_params=_cparams(("arbitrary", "arbitrary", "arbitrary")),
    )(qkv, qkv, qkv, suffix_m)


def attn_bwd(qkv, do, tot, cnt, n_seq, seq):
    t = qkv.shape[1]
    upp, bq, nq, nsub, per_unit, groups, heads = _att_geometry(qkv, seq)
    _, prefix_m = _sum_matrices()
    scale = HEAD_DIM ** -0.5

    def body(q_ref, k_ref, v_ref, do_ref, tot_ref, m_ref, cnt_ref, dq_ref, dk_ref, dv_ref, dk_acc, dv_acc):
        qi = pl.program_id(2)
        step_id = (pl.program_id(0) * groups + pl.program_id(1)) * nq + qi
        n_left = jnp.clip(cnt_ref[step_id].astype(jnp.int32), 0, qi)
        in_head = _head_lanes(bq, heads)
        only = lambda v, h: jnp.where(in_head[h], v, jnp.zeros_like(v))
        q_all = q_ref[0] * jnp.asarray(scale, BF16)
        do_all = do_ref[0]
        qs = [only(q_all, h) for h in range(heads)]
        dos = [only(do_all, h) for h in range(heads)]
        first = lax.broadcasted_iota(jnp.int32, (bq, LANES), 1) < HEAD_DIM
        tots = []
        for h in range(0, heads, 2):
            both = tot_ref[:, h // 2 * LANES:(h // 2 + 1) * LANES]
            swapped = pltpu.roll(both, HEAD_DIM, 1)
            tots += [jnp.where(first, both, swapped), jnp.where(first, swapped, both)]
        m_ext = m_ref[...]
        row = lax.broadcasted_iota(jnp.int32, (bq, bq), 0)
        col = lax.broadcasted_iota(jnp.int32, (bq, bq), 1)
        diag_mask = col < row

        @pl.when(qi == 0)
        def _():
            dk_acc[...] = jnp.zeros_like(dk_acc)
            dv_acc[...] = jnp.zeros_like(dv_acc)

        def block(kj, carry, mask):
            off = pl.multiple_of(kj * bq, bq)
            k_all = k_ref[0, pl.ds(off, bq), :]
            v_all = v_ref[0, pl.ds(off, bq), :]
            pres, gpres, dq = carry
            dk_part = jnp.zeros((bq, ATT_LANES), F32)
            dv_part = jnp.zeros((bq, ATT_LANES), F32)
            pres_out, gpres_out = [], []
            for h in range(heads):
                pre, gpre = pres[h], gpres[h]
                z = _nt(qs[h], k_all)
                if mask is not None:
                    z = jnp.where(mask, z, MASKED)
                sp, ls = _softplus_parts(z)
                sums = _keysums(-sp, m_ext)
                parts = []
                for j in range(nsub):
                    pin, ptot = sums[j]
                    parts.append(jnp.exp(ls[:, j * KBLK:(j + 1) * KBLK] + (tots[h] - (pre + pin))))
                    pre = pre + ptot
                a = jnp.concatenate(parts, axis=1)
                g = a * _nt(dos[h], v_all)
                gsums = _keysums(g, m_ext)
                parts = []
                for j in range(nsub):
                    gin, gtot = gsums[j]
                    parts.append(gpre + gin)
                    gpre = gpre + gtot
                dz = g - jnp.exp(ls) * jnp.concatenate(parts, axis=1)
                dzb = dz.astype(BF16)
                dq = dq + _nn(dzb, only(k_all, h))
                dk_part = dk_part + _tn(dzb, qs[h])
                dv_part = dv_part + _tn(a.astype(BF16), dos[h])
                pres_out.append(pre)
                gpres_out.append(gpre)
            dk_acc[pl.ds(off, bq), :] += dk_part
            dv_acc[pl.ds(off, bq), :] += dv_part
            return tuple(pres_out), tuple(gpres_out), dq

        zero = jnp.zeros((bq, LANES), F32)
        carry = ((zero,) * heads, (zero,) * heads, jnp.zeros((bq, ATT_LANES), F32))
        carry = lax.fori_loop(qi - n_left, qi, lambda kj, c: block(kj, c, None), carry)
        carry = block(qi, carry, diag_mask)
        dq_ref[0] = (carry[2] * scale).astype(BF16)

        @pl.when(qi == nq - 1)
        def _():
            dk_ref[0] = dk_acc[...].astype(BF16)
            dv_ref[0] = dv_acc[...].astype(BF16)

    qblk = lambda b, g, i: (g // per_unit, b * nq + i, g % per_unit)
    kv_out = pl.BlockSpec((1, seq, ATT_LANES), lambda b, g, i: (g // per_unit, b, g % per_unit))
    shp = jax.ShapeDtypeStruct((upp, t, 2 * LANES), BF16)
    return pl.pallas_call(
        body, name="attn_bwd", grid=(n_seq, groups, nq),
        in_specs=[pl.BlockSpec((1, bq, ATT_LANES), qblk),
                  pl.BlockSpec((1, seq, ATT_LANES), lambda b, g, i: (upp + g // per_unit, b, g % per_unit)),
                  pl.BlockSpec((1, seq, ATT_LANES), lambda b, g, i: (2 * upp + g // per_unit, b, g % per_unit)),
                  pl.BlockSpec((1, bq, ATT_LANES), qblk),
                  pl.BlockSpec((bq, ATT_LANES), lambda b, g, i: (b * nq + i, g)),
                  pl.BlockSpec((2 * KBLK, 2 * KBLK), lambda b, g, i: (0, 0)),
                  pl.BlockSpec(memory_space=pltpu.SMEM)],
        out_specs=[pl.BlockSpec((1, bq, ATT_LANES), qblk), kv_out, kv_out],
        out_shape=[shp, shp, shp],
        scratch_shapes=[pltpu.VMEM((seq, ATT_LANES), F32), pltpu.VMEM((seq, ATT_LANES), F32)],
        compiler_params=_cparams(("parallel", "parallel", "arbitrary")),
    )(qkv, qkv, qkv, do, tot, prefix_m, cnt)


def _ln_stats(v):
    mu = jnp.mean(v, axis=-1, keepdims=True)
    vc = v - mu
    rstd = lax.rsqrt(jnp.mean(vc * vc, axis=-1, keepdims=True) + EPS)
    return vc * rstd, rstd


def _glu_into(a0_ref, av_ref, ag_ref, hv_ref, hg_ref, first):
    hv = hv_ref[0].astype(F32)
    hg = hg_ref[0].astype(F32)
    a0_ref[0:HALO, :] = jnp.where(first, 0.0, hv * _sigmoid(hg))
    av = av_ref[0].astype(F32)
    ag = ag_ref[0].astype(F32)
    a0_ref[HALO:, :] = av * _sigmoid(ag)


def _shifted_taps(ref, shifted_ref, tm, first):
    taps = []
    for b in range(8):
        offs = [o for o in range(first, first + CONV_WIDTH) if o % 8 == b]
        n_rows = max(offs) - b + tm
        shifted_ref[b, 0:n_rows, :] = ref[pl.ds(b, n_rows), :]
        taps += [(b, o - b, o - first) for o in offs]
    return taps


def _tril_mask():
    r = lax.broadcasted_iota(jnp.int32, (CHUNK, CHUNK), 0)
    c = lax.broadcasted_iota(jnp.int32, (CHUNK, CHUNK), 1)
    return c <= r


def mix_fwd(z, conv_w, conv_b, ln_a_g, ln_a_b, ln_v_g, ln_v_b, sp_w, sp_bt, seq):
    _, t, c = z.shape
    tm = _tile(seq, 512)
    tiles_per_seq = seq // tm
    groups = c // LANES
    hb = tm // HALO

    def body(av_ref, ag_ref, u_ref, v_ref, hv_ref, hg_ref, cw_ref, cb_ref, lag_ref, lab_ref, lvg_ref, lvb_ref,
             spw_ref, spb_ref, cat_ref, a1_ref, a0_ref, sh_ref):
        i = pl.program_id(0)
        _glu_into(a0_ref, av_ref, ag_ref, hv_ref, hg_ref, i % tiles_per_seq == 0)
        acc = jnp.zeros((tm, c), F32) + cb_ref[...]
        for b, ro, k in _shifted_taps(a0_ref, sh_ref, tm, HALO - (CONV_WIDTH - 1)):
            acc = acc + cw_ref[k:k + 1, :] * sh_ref[b, pl.ds(ro, tm), :]
        a1_ref[...] = acc
        xh, _ = _ln_stats(acc)
        a2 = xh * lag_ref[...] + lab_ref[...]
        a3 = (a2 * _sigmoid(a2)).astype(BF16)
        half = c // 2
        cat_ref[0] = a3[:, :half]
        cat_ref[1] = a3[:, half:]
        tril = _tril_mask()
        for g in range(groups):
            sl = slice(g * LANES, (g + 1) * LANES)
            xh, _ = _ln_stats(v_ref[0][:, sl].astype(F32))
            vn = (xh * lvg_ref[:, sl] + lvb_ref[:, sl]).astype(BF16)
            w = jnp.where(tril, spw_ref[g], 0.0).astype(BF16)
            bias = spb_ref[:, g:g + 1]
            for ch in range(tm // CHUNK):
                rows = slice(ch * CHUNK, (ch + 1) * CHUNK)
                vs = _nn(w, vn[rows]) + bias
                bo = (u_ref[0][rows, sl].astype(F32) * vs).astype(BF16)
                cat_ref[2 + (g * LANES) // half, rows, (g * LANES) % half:(g * LANES) % half + LANES] = bo

    unit = lambda u: pl.BlockSpec((1, tm, c), lambda i: (u, i, 0))
    halo = lambda u: pl.BlockSpec((1, HALO, c), lambda i: (u, jnp.maximum(i * hb - 1, 0), 0))
    vec = pl.BlockSpec((1, c), lambda i: (0, 0))
    return pl.pallas_call(
        body, name="mix_fwd", grid=(t // tm,),
        in_specs=[unit(0), unit(1), unit(2), unit(3), halo(0), halo(1),
                  pl.BlockSpec((CONV_WIDTH, c), lambda i: (0, 0)), vec, vec, vec, vec, vec,
                  pl.BlockSpec((groups, CHUNK, CHUNK), lambda i: (0, 0, 0)),
                  pl.BlockSpec((CHUNK, groups), lambda i: (0, 0))],
        out_specs=[pl.BlockSpec((4, tm, c // 2), lambda i: (0, i, 0)), pl.BlockSpec((tm, c), lambda i: (i, 0))],
        out_shape=[jax.ShapeDtypeStruct((4, t, c // 2), BF16), jax.ShapeDtypeStruct((t, c), F32)],
        scratch_shapes=[pltpu.VMEM((HALO + tm, c), F32), pltpu.VMEM((8, HALO + tm, c), F32)],
        compiler_params=_cparams(("parallel",)),
    )(z, z, z, z, z, z, conv_w, conv_b, ln_a_g, ln_a_b, ln_v_g, ln_v_b, sp_w, sp_bt)


def mix_bwd_point(dcat, z, a1, ln_a_g, ln_a_b, ln_v_g, ln_v_b, sp_w, sp_wt, sp_bt, seq):
    _, t, c = z.shape
    tm = _tile(seq, 512)
    groups = c // LANES
    half = c // 2

    def body(dc_ref, u_ref, v_ref, a1_ref, lag_ref, lab_ref, lvg_ref, lvb_ref, spw_ref, spwt_ref, spb_ref,
             dz_ref, da1_ref, dcb_ref, dlag_ref, dlab_ref, dlvg_ref, dlvb_ref, dspw_ref, dspb_ref):
        i = pl.program_id(0)
        last = pl.num_programs(0) - 1

        @pl.when(i == 0)
        def _():
            for r in (dcb_ref, dlag_ref, dlab_ref, dlvg_ref, dlvb_ref, dspw_ref, dspb_ref):
                r[...] = jnp.zeros_like(r)

        da3 = jnp.concatenate([dc_ref[0], dc_ref[1]], axis=-1)
        xh, rstd = _ln_stats(a1_ref[...])
        a2 = xh * lag_ref[...] + lab_ref[...]
        s = _sigmoid(a2)
        da2 = da3 * (s * (1.0 + a2 * (1.0 - s)))
        dlag_ref[...] += jnp.sum(da2 * xh, axis=0, keepdims=True)
        dlab_ref[...] += jnp.sum(da2, axis=0, keepdims=True)
        dxh = da2 * lag_ref[...]
        da1 = rstd * (dxh - jnp.mean(dxh, axis=-1, keepdims=True) - xh * jnp.mean(dxh * xh, axis=-1, keepdims=True))
        da1_ref[...] = da1
        dcb_ref[...] += jnp.sum(da1, axis=0, keepdims=True)

        tril = _tril_mask()
        for g in range(groups):
            sl = slice(g * LANES, (g + 1) * LANES)
            xh, rstd = _ln_stats(v_ref[0][:, sl].astype(F32))
            lg = lvg_ref[:, sl]
            vnb = (xh * lg + lvb_ref[:, sl]).astype(BF16)
            w = jnp.where(tril, spw_ref[g], 0.0).astype(BF16)
            wt = jnp.where(tril.T, spwt_ref[g], 0.0).astype(BF16)
            bias = spb_ref[:, g:g + 1]
            dbo_all = dc_ref[2 + (g * LANES) // half][:, (g * LANES) % half:(g * LANES) % half + LANES]
            dvn_parts = []
            dw_acc = jnp.zeros((CHUNK, CHUNK), F32)
            db_acc = jnp.zeros((CHUNK, LANES), F32)
            for ch in range(tm // CHUNK):
                rows = slice(ch * CHUNK, (ch + 1) * CHUNK)
                vs = _nn(w, vnb[rows]) + bias
                dbo = dbo_all[rows]
                uv = u_ref[0][rows, sl].astype(F32)
                dz_ref[0, rows, sl] = (dbo * vs).astype(BF16)
                dvs = dbo * uv
                dvsb = dvs.astype(BF16)
                dvn_parts.append(_nn(wt, dvsb))
                dw_acc = dw_acc + _nt(dvsb, vnb[rows])
                db_acc = db_acc + dvs
            dvn = jnp.concatenate(dvn_parts, axis=0)
            dspw_ref[g] += jnp.where(tril, dw_acc, 0.0)
            dspb_ref[g] += db_acc
            dlvg_ref[:, sl] += jnp.sum(dvn * xh, axis=0, keepdims=True)
            dlvb_ref[:, sl] += jnp.sum(dvn, axis=0, keepdims=True)
            dxh = dvn * lg
            dv = rstd * (dxh - jnp.mean(dxh, axis=-1, keepdims=True) - xh * jnp.mean(dxh * xh, axis=-1, keepdims=True))
            dz_ref[1, :, sl] = dv.astype(BF16)

        @pl.when(i == last)
        def _():
            for g in range(groups):
                dspb_ref[g] = jnp.zeros((CHUNK, LANES), F32) + jnp.sum(dspb_ref[g], axis=-1, keepdims=True)

    unit = lambda u: pl.BlockSpec((1, tm, c), lambda i: (u, i, 0))
    vec = pl.BlockSpec((1, c), lambda i: (0, 0))
    sq = pl.BlockSpec((groups, CHUNK, CHUNK), lambda i: (0, 0, 0))
    vshape = jax.ShapeDtypeStruct((1, c), F32)
    sshape = jax.ShapeDtypeStruct((groups, CHUNK, CHUNK), F32)
    return pl.pallas_call(
        body, name="mix_bwd_point", grid=(t // tm,),
        in_specs=[pl.BlockSpec((4, tm, half), lambda i: (0, i, 0)), unit(2), unit(3),
                  pl.BlockSpec((tm, c), lambda i: (i, 0)), vec, vec, vec, vec, sq, sq,
                  pl.BlockSpec((CHUNK, groups), lambda i: (0, 0))],
        out_specs=[pl.BlockSpec((2, tm, c), lambda i: (1, i, 0)), pl.BlockSpec((tm, c), lambda i: (i, 0)),
                   vec, vec, vec, vec, vec, sq, sq],
        out_shape=[jax.ShapeDtypeStruct((4, t, c), BF16), jax.ShapeDtypeStruct((t, c), F32),
                   vshape, vshape, vshape, vshape, vshape, sshape, sshape],
        compiler_params=_cparams(("arbitrary",)),
    )(dcat, z, z, a1, ln_a_g, ln_a_b, ln_v_g, ln_v_b, sp_w, sp_wt, sp_bt)


def mix_bwd_conv(dz, da1, z, conv_w, seq):
    _, t, c = z.shape
    tm = _tile(seq, 512)
    tiles_per_seq = seq // tm
    hb = tm // HALO
    n_halo_blocks = t // HALO

    rc = _tile(tm, CONV_ROWS)

    def body(dz_in_ref, d_ref, dh_ref, av_ref, ag_ref, cw_ref, dz_ref, dcw_ref, d1_ref, sh_ref, part_ref):
        del dz_in_ref
        i = pl.program_id(0)

        @pl.when(i == 0)
        def _():
            part_ref[...] = jnp.zeros_like(part_ref)

        d1_ref[0:tm, :] = d_ref[...]
        d1_ref[tm:, :] = jnp.where((i + 1) % tiles_per_seq == 0, 0.0, dh_ref[...])
        taps = _shifted_taps(d1_ref, sh_ref, tm, 0)

        def chunk(ci, carry):
            r0 = pl.multiple_of(ci * rc, rc)
            av = av_ref[0, pl.ds(r0, rc), :].astype(F32)
            s = _sigmoid(ag_ref[0, pl.ds(r0, rc), :].astype(F32))
            a0 = av * s
            da0 = jnp.zeros((rc, c), F32)
            for b, ro, back in taps:
                k = CONV_WIDTH - 1 - back
                rows = sh_ref[b, pl.ds(r0 + ro, rc), :]
                da0 = da0 + cw_ref[k:k + 1, :] * rows
                prod = a0 * rows
                part_ref[k] += functools.reduce(lambda p, q: p + q, [prod[8 * r:8 * r + 8] for r in range(rc // 8)])
            dz_ref[0, pl.ds(r0, rc), :] = (da0 * s).astype(BF16)
            dz_ref[1, pl.ds(r0, rc), :] = (da0 * av * s * (1.0 - s)).astype(BF16)
            return carry

        lax.fori_loop(0, tm // rc, chunk, 0)

        @pl.when(i == pl.num_programs(0) - 1)
        def _():
            dcw_ref[...] = jnp.sum(part_ref[...], axis=1)

    unit = lambda u: pl.BlockSpec((1, tm, c), lambda i: (u, i, 0))
    return pl.pallas_call(
        body, name="mix_bwd_conv", grid=(t // tm,),
        in_specs=[pl.BlockSpec(memory_space=pl.ANY), pl.BlockSpec((tm, c), lambda i: (i, 0)),
                  pl.BlockSpec((HALO, c), lambda i: (jnp.minimum((i + 1) * hb, n_halo_blocks - 1), 0)),
                  unit(0), unit(1), pl.BlockSpec((CONV_WIDTH, c), lambda i: (0, 0))],
        out_specs=[pl.BlockSpec((2, tm, c), lambda i: (0, i, 0)), pl.BlockSpec((CONV_WIDTH, c), lambda i: (0, 0))],
        out_shape=[jax.ShapeDtypeStruct(dz.shape, BF16), jax.ShapeDtypeStruct((CONV_WIDTH, c), F32)],
        scratch_shapes=[pltpu.VMEM((tm + HALO, c), F32), pltpu.VMEM((8, tm + HALO, c), F32),
                        pltpu.VMEM((CONV_WIDTH, 8, c), F32)],
        input_output_aliases={0: 0},
        compiler_params=_cparams(("arbitrary",)),
    )(dz, da1, da1, z, z, conv_w)


CHIP_FLIPS = ((1, 0), (0, 1), (1, 1))
ANY = pl.BlockSpec(memory_space=pl.ANY)


def _place():
    return lax.axis_index("x"), lax.axis_index("y"), lax.axis_index("c")


def _flip(v, f):
    return 1 - v if f else v


def place_shard(w, chip, dtype, name):
    n_layers, r, cc = w.shape
    rb = _tile(r, 512)

    def body(chip_ref, w_ref, *o_refs):
        del chip_ref
        for layer, o_ref in enumerate(o_refs):
            o_ref[0] = w_ref[layer].astype(dtype)

    return pl.pallas_call(
        body, name=name,
        grid_spec=pltpu.PrefetchScalarGridSpec(
            num_scalar_prefetch=1, grid=(r // rb,),
            in_specs=[pl.BlockSpec((n_layers, rb, cc), lambda i, chip_ref: (0, i, 0))],
            out_specs=[pl.BlockSpec((1, rb, cc), lambda i, chip_ref: (chip_ref[0], i, 0))] * n_layers),
        out_shape=[jax.ShapeDtypeStruct((N_CHIPS, r, cc), dtype)] * n_layers,
        compiler_params=_cparams(("parallel",)),
    )(chip, w)


class Carry:
    def __init__(self, arrays, out_shapes, aliased, sem_shapes, start, finish):
        self.arrays, self.out_shapes, self.aliased, self.sem_shapes = list(arrays), list(out_shapes), aliased, list(sem_shapes)
        self.start, self.finish = start, finish


def _call(body, *, name, grid, in_specs, out_specs, out_shape, args, sem, scratch_shapes=(), carry=None):
    if carry is None:
        res = pl.pallas_call(body, name=name, grid=grid, in_specs=in_specs, out_specs=out_specs, out_shape=out_shape,
                             scratch_shapes=list(scratch_shapes), compiler_params=_cparams(sem))(*args)
        return list(res), []
    n_in, n_out, n_scr, nc = len(args), len(out_shape), len(scratch_shapes), len(carry.arrays)

    def full_body(*refs):
        ins, refs = refs[:n_in], refs[n_in:]
        c_ins, refs = refs[:nc], refs[nc:]
        outs, refs = refs[:n_out], refs[n_out:]
        c_outs, refs = refs[:nc], refs[nc:]
        scr, sems = refs[:n_scr], refs[n_scr:]
        first = functools.reduce(lambda a, b: a & b, [pl.program_id(d) == 0 for d in range(len(grid))])
        last = functools.reduce(lambda a, b: a & b, [pl.program_id(d) == grid[d] - 1 for d in range(len(grid))])

        @pl.when(first)
        def _():
            carry.start(c_ins, c_outs, sems)

        body(*ins, *outs, *scr)

        @pl.when(last)
        def _():
            carry.finish(c_ins, c_outs, sems)

    res = pl.pallas_call(
        full_body, name=name, grid=grid, in_specs=list(in_specs) + [ANY] * nc, out_specs=list(out_specs) + [ANY] * nc,
        out_shape=list(out_shape) + carry.out_shapes, scratch_shapes=list(scratch_shapes) + carry.sem_shapes,
        input_output_aliases={n_in + i: n_out + i for i in range(nc)} if carry.aliased else {},
        compiler_params=pltpu.CompilerParams(dimension_semantics=("arbitrary",) * len(grid), vmem_limit_bytes=VMEM_LIMIT,
                                             has_side_effects=True),
    )(*args, *carry.arrays)
    return list(res[:n_out]), list(res[n_out:])


def _gather_ops(shapes, whole):
    n = len(shapes)

    def rows(a, c):
        hr = shapes[a][1] // 2
        return pl.ds(pl.multiple_of(c * hr, 16), hr)

    def start(ins, outs, sems):
        ici_send, ici_recv = sems[0], sems[1]
        x, y, c = _place()
        k = 2 * x + y
        for a in range(n):
            for o, (fx, fy) in enumerate(CHIP_FLIPS):
                src = ins[a].at[k] if whole[a] else ins[a].at[k, rows(a, c)]
                dst = outs[a].at[k] if whole[a] else outs[a].at[k, rows(a, c)]
                pltpu.make_async_remote_copy(
                    src_ref=src, dst_ref=dst, send_sem=ici_send.at[3 * a + o], recv_sem=ici_recv.at[3 * a + o],
                    device_id=(_flip(x, fx), _flip(y, fy), c), device_id_type=MESH).start()

    def finish(ins, outs, sems):
        ici_send, ici_recv, d2d_send, d2d_recv = sems
        x, y, c = _place()
        k = 2 * x + y
        sibling = (x, y, 1 - c)

        def copy(ref, send, recv, a, o):
            return pltpu.make_async_remote_copy(src_ref=ref, dst_ref=ref, send_sem=send.at[3 * a + o],
                                                recv_sem=recv.at[3 * a + o], device_id=sibling, device_id_type=MESH)

        for a in range(n):
            for o, (fx, fy) in enumerate(CHIP_FLIPS):
                kk = 2 * _flip(x, fx) + _flip(y, fy)
                landed = outs[a].at[kk] if whole[a] else outs[a].at[kk, rows(a, c)]
                copy(landed, ici_send, ici_recv, a, o).wait_recv()
                if not whole[a]:
                    copy(landed, d2d_send, d2d_recv, a, o).start()
        for a in range(n):
            for o, (fx, fy) in enumerate(CHIP_FLIPS):
                kk = 2 * _flip(x, fx) + _flip(y, fy)
                mine = ins[a].at[k] if whole[a] else ins[a].at[k, rows(a, c)]
                copy(mine, ici_send, ici_recv, a, o).wait_send()
                if not whole[a]:
                    copy(outs[a].at[kk, rows(a, 1 - c)], d2d_send, d2d_recv, a, o).wait_recv()
                    copy(outs[a].at[kk, rows(a, c)], d2d_send, d2d_recv, a, o).wait_send()

    dma = pltpu.SemaphoreType.DMA
    return start, finish, [dma((3 * n,))] * 4


def gather_carry(bufs):
    start, finish, sems = _gather_ops([b.shape for b in bufs], [False] * len(bufs))
    return Carry(bufs, [jax.ShapeDtypeStruct(b.shape, b.dtype) for b in bufs], True, sems, start, finish)


def allgather_weights(shards, smalls):
    bufs = list(shards) + list(smalls)
    n = len(bufs)
    start, finish, sems = _gather_ops([b.shape for b in bufs], [False] * len(shards) + [True] * len(smalls))

    def body(*refs):
        start(refs[:n], refs[n:2 * n], refs[2 * n:])
        finish(refs[:n], refs[n:2 * n], refs[2 * n:])

    res = pl.pallas_call(
        body, name="allgather_weights", in_specs=[ANY] * n, out_specs=[ANY] * n,
        out_shape=[jax.ShapeDtypeStruct(b.shape, b.dtype) for b in bufs], scratch_shapes=sems,
        input_output_aliases={i: i for i in range(n)},
        compiler_params=pltpu.CompilerParams(has_side_effects=True),
    )(*bufs)
    return res[:len(shards)], res[len(shards):]


def rs_exchange(grads):
    n = len(grads)

    def body(*refs):
        ins, outs = refs[:n], refs[n:2 * n]
        send, recv = refs[2 * n:]
        x, y, c = _place()
        cps = []
        for a in range(n):
            cp = pltpu.make_async_remote_copy(
                src_ref=ins[a].at[:, 1 - c], dst_ref=outs[a], send_sem=send.at[a], recv_sem=recv.at[a],
                device_id=(x, y, 1 - c), device_id_type=MESH)
            cp.start()
            cps.append(cp)
        for cp in cps:
            cp.wait()

    dma = pltpu.SemaphoreType.DMA
    return pl.pallas_call(
        body, name="rs_exchange", in_specs=[ANY] * n, out_specs=[ANY] * n,
        out_shape=[jax.ShapeDtypeStruct((g.shape[0],) + g.shape[2:], g.dtype) for g in grads],
        scratch_shapes=[dma((n,)), dma((n,))],
        compiler_params=pltpu.CompilerParams(has_side_effects=True),
    )(*grads)


def rs_add(gs, sibs, core, out_dtype, name):
    n = len(gs)
    nk = gs[0].shape[0]

    def body(core_ref, *refs):
        del core_ref
        for a in range(n):
            refs[2 * n + a][0] = (refs[a][0, 0] + refs[n + a][0]).astype(out_dtype)

    halves = [g.shape[2:] for g in gs]
    return pl.pallas_call(
        body, name=name,
        grid_spec=pltpu.PrefetchScalarGridSpec(
            num_scalar_prefetch=1, grid=(nk,),
            in_specs=[pl.BlockSpec((1, 1) + h, lambda k, core_ref: (k, core_ref[0], 0, 0)) for h in halves]
            + [pl.BlockSpec((1,) + h, lambda k, core_ref: (k, 0, 0)) for h in halves],
            out_specs=[pl.BlockSpec((1,) + h, lambda k, core_ref: (k, 0, 0)) for h in halves]),
        out_shape=[jax.ShapeDtypeStruct((nk,) + h, out_dtype) for h in halves],
        compiler_params=_cparams(("parallel",)),
    )(core, *gs, *sibs)


def send_carry(parts):
    n = len(parts)

    def copies(ins, outs, sems):
        x, y, c = _place()
        for a in range(n):
            for o, (fx, fy) in enumerate(CHIP_FLIPS):
                kk = 2 * _flip(x, fx) + _flip(y, fy)
                yield pltpu.make_async_remote_copy(
                    src_ref=ins[a].at[kk], dst_ref=outs[a].at[o], send_sem=sems[0].at[3 * a + o],
                    recv_sem=sems[1].at[3 * a + o], device_id=(_flip(x, fx), _flip(y, fy), c), device_id_type=MESH)

    def start(ins, outs, sems):
        for cp in copies(ins, outs, sems):
            cp.start()

    def finish(ins, outs, sems):
        for cp in copies(ins, outs, sems):
            cp.wait()

    dma = pltpu.SemaphoreType.DMA
    return Carry(parts, [jax.ShapeDtypeStruct((3,) + p.shape[1:], p.dtype) for p in parts], False,
                 [dma((3 * n,)), dma((3 * n,))], start, finish)


def rs_sum(recvs, parts, where, name):
    n_layers = len(recvs)
    _, hr, cc = recvs[0].shape
    rb = _tile(hr, 256)

    def body(where_ref, *refs):
        del where_ref
        o_ref = refs[-1]
        for layer in range(n_layers):
            r_ref, p_ref = refs[layer], refs[n_layers + layer]
            o_ref[layer, 0] = ((p_ref[0].astype(F32) + r_ref[0].astype(F32)) + r_ref[1].astype(F32)) + r_ref[2].astype(F32)

    return pl.pallas_call(
        body, name=name,
        grid_spec=pltpu.PrefetchScalarGridSpec(
            num_scalar_prefetch=1, grid=(hr // rb,),
            in_specs=[pl.BlockSpec((3, rb, cc), lambda i, w_ref: (0, i, 0))] * n_layers
            + [pl.BlockSpec((1, rb, cc), lambda i, w_ref: (w_ref[0], i, 0))] * n_layers,
            out_specs=pl.BlockSpec((n_layers, 1, rb, cc), lambda i, w_ref: (0, w_ref[1], i, 0))),
        out_shape=jax.ShapeDtypeStruct((n_layers, 2, hr, cc), F32),
        compiler_params=_cparams(("parallel",)),
    )(where, *recvs, *parts)


def rs_share(fulls):
    n = len(fulls)

    def body(*refs):
        ins, outs = refs[:n], refs[n:2 * n]
        send, recv = refs[2 * n:]
        x, y, c = _place()
        cps = []
        for a in range(n):
            cp = pltpu.make_async_remote_copy(
                src_ref=ins[a].at[:, c], dst_ref=outs[a].at[:, c], send_sem=send.at[a], recv_sem=recv.at[a],
                device_id=(x, y, 1 - c), device_id_type=MESH)
            cp.start()
            cps.append(cp)
        for a in range(n):
            got = outs[a].at[:, 1 - c]
            pltpu.make_async_remote_copy(
                src_ref=got, dst_ref=got, send_sem=send.at[a], recv_sem=recv.at[a],
                device_id=(x, y, 1 - c), device_id_type=MESH).wait_recv()
        for cp in cps:
            cp.wait_send()

    dma = pltpu.SemaphoreType.DMA
    return pl.pallas_call(
        body, name="rs_share", in_specs=[ANY] * n, out_specs=[ANY] * n,
        out_shape=[jax.ShapeDtypeStruct(f.shape, f.dtype) for f in fulls],
        scratch_shapes=[dma((n,)), dma((n,))],
        input_output_aliases={i: i for i in range(n)},
        compiler_params=pltpu.CompilerParams(has_side_effects=True),
    )(*fulls)


def allreduce_small(v):
    r, w = v.shape

    def body(v_ref, o_ref, buf, send, recv, loc):
        x, y, c = _place()
        me = 4 * x + 2 * y + c
        mine = pltpu.make_async_copy(v_ref, buf.at[me], loc)
        mine.start()
        cps = []
        for o in range(1, N_DEV):
            fx, fy, fc = (o >> 2) & 1, (o >> 1) & 1, o & 1
            cp = pltpu.make_async_remote_copy(
                src_ref=v_ref, dst_ref=buf.at[me], send_sem=send.at[o - 1], recv_sem=recv.at[o - 1],
                device_id=(_flip(x, fx), _flip(y, fy), _flip(c, fc)), device_id_type=MESH)
            cp.start()
            cps.append(cp)
        for o in range(1, N_DEV):
            fx, fy, fc = (o >> 2) & 1, (o >> 1) & 1, o & 1
            peer = 4 * _flip(x, fx) + 2 * _flip(y, fy) + _flip(c, fc)
            pltpu.make_async_remote_copy(
                src_ref=v_ref, dst_ref=buf.at[peer], send_sem=send.at[o - 1], recv_sem=recv.at[o - 1],
                device_id=(x, y, c), device_id_type=MESH).wait_recv()
        for cp in cps:
            cp.wait_send()
        mine.wait()
        acc = buf[0]
        for d in range(1, N_DEV):
            acc = acc + buf[d]
        o_ref[...] = acc

    dma = pltpu.SemaphoreType.DMA
    vm = pl.BlockSpec(memory_space=pltpu.VMEM)
    return pl.pallas_call(
        body, name="allreduce_small", in_specs=[vm], out_specs=vm,
        out_shape=jax.ShapeDtypeStruct((r, w), F32),
        scratch_shapes=[pltpu.VMEM((N_DEV, r, w), F32), dma((N_DEV - 1,)), dma((N_DEV - 1,)), dma],
        compiler_params=pltpu.CompilerParams(has_side_effects=True, vmem_limit_bytes=VMEM_LIMIT),
    )(v)


def adamw(w, g, m, v, name):
    r, cc = w.shape
    rb = _tile(r, 256)

    def body(w_ref, g_ref, m_ref, v_ref, d_ref, nm_ref, nv_ref):
        gv = g_ref[...]
        nm = ADAM_B1 * m_ref[...] + (1.0 - ADAM_B1) * gv
        nv = ADAM_B2 * v_ref[...] + (1.0 - ADAM_B2) * (gv * gv)
        m_hat = nm / (1.0 - ADAM_B1 ** ADAM_STEP)
        v_hat = nv / (1.0 - ADAM_B2 ** ADAM_STEP)
        d_ref[...] = -ADAM_LR * (m_hat / (jnp.sqrt(v_hat) + ADAM_EPS) + ADAM_WD * w_ref[...])
        nm_ref[...] = nm
        nv_ref[...] = nv

    blk = pl.BlockSpec((rb, cc), lambda i: (i, 0))
    shp = jax.ShapeDtypeStruct((r, cc), F32)
    return pl.pallas_call(
        body, name=name, grid=(r // rb,), in_specs=[blk] * 4, out_specs=[blk] * 3, out_shape=[shp] * 3,
        compiler_params=_cparams(("parallel",)),
    )(w, g, m, v)


WEIGHTS = ['g_ffn1', 'w_ffn1_gate', 'w_ffn1_up', 'w_ffn1_down', 'g_mix', 'w_in_ab', 'conv_w', 'conv_b', 'ln_a_g',
           'ln_a_b', 'ln_v_g', 'ln_v_b', 'sp_w', 'sp_b', 'w_out_ab', 'w_qkv', 'w_o', 'g_ffn2', 'w_ffn2_gate',
           'w_ffn2_up', 'w_ffn2_down', 'g_final']
BIG = ['w_ffn1_gate', 'w_ffn1_up', 'w_ffn1_down', 'w_in_ab', 'w_out_ab', 'w_qkv', 'w_o', 'w_ffn2_gate', 'w_ffn2_up',
       'w_ffn2_down']
SMALL = ['g_ffn1', 'g_mix', 'g_ffn2', 'g_final', 'conv_b', 'ln_a_g', 'ln_a_b', 'ln_v_g', 'ln_v_b', 'sp_b', 'sp_w']
HIDDEN_MAJOR = ['w_ffn1_gate', 'w_ffn1_up', 'w_ffn2_gate', 'w_ffn2_up']


CARRY_WEIGHTS = {"ffn_gateup": 9.2e6, "ffn_down": 6.1e6, "mm_in": 3.0e6, "mm_out": 3.3e6}


def _use_order(depth):
    order = []
    for layer in range(depth):
        order += [('w_ffn1_gate', layer), ('w_ffn1_up', layer), ('w_ffn1_down', layer)]
        order += [('w_in_ab', layer // 2), ('w_out_ab', layer // 2)] if layer % 2 == 0 else [('w_qkv', layer // 2), ('w_o', layer // 2)]
        order += [('w_ffn2_gate', layer), ('w_ffn2_up', layer), ('w_ffn2_down', layer)]
    return order


def _rows(a):
    return a.reshape(-1, LANES)


def _pack(parts):
    v = jnp.concatenate([_rows(p) for p in parts], axis=0)
    pad = (-v.shape[0]) % 8
    return jnp.pad(v, ((0, pad), (0, 0)))


def _unpack(v, shapes):
    out, r = [], 0
    for s in shapes:
        n = 1
        for d in s:
            n *= d
        n //= LANES
        out.append(v[r:r + n].reshape(s))
        r += n
    return out


def kernel(x, g_ffn1, w_ffn1_gate, w_ffn1_up, w_ffn1_down, g_mix, w_in_ab, conv_w, conv_b, ln_a_g, ln_a_b, ln_v_g, ln_v_b, sp_w, sp_b, w_out_ab, w_qkv, w_o, g_ffn2, w_ffn2_gate, w_ffn2_up, w_ffn2_down, g_final, loss_target, m_g_ffn1, m_w_ffn1_gate, m_w_ffn1_up, m_w_ffn1_down, m_g_mix, m_w_in_ab, m_conv_w, m_conv_b, m_ln_a_g, m_ln_a_b, m_ln_v_g, m_ln_v_b, m_sp_w, m_sp_b, m_w_out_ab, m_w_qkv, m_w_o, m_g_ffn2, m_w_ffn2_gate, m_w_ffn2_up, m_w_ffn2_down, m_g_final, v_g_ffn1, v_w_ffn1_gate, v_w_ffn1_up, v_w_ffn1_down, v_g_mix, v_w_in_ab, v_conv_w, v_conv_b, v_ln_a_g, v_ln_a_b, v_ln_v_g, v_ln_v_b, v_sp_w, v_sp_b, v_w_out_ab, v_w_qkv, v_w_o, v_g_ffn2, v_w_ffn2_gate, v_w_ffn2_up, v_w_ffn2_down, v_g_final):
    p = dict(locals())
    for name in HIDDEN_MAJOR:
        for pre in ('', 'm_', 'v_'):
            p[pre + name] = jnp.swapaxes(p[pre + name], 1, 2)
    back = lambda name, a: jnp.swapaxes(a, 1, 2) if name in HIDDEN_MAJOR else a
    n_seq, seq, d = x.shape
    t = n_seq * seq
    depth = g_ffn1.shape[0]
    core = lax.axis_index("c")
    chip = 2 * lax.axis_index("x") + lax.axis_index("y")
    xf = x.reshape(t, d)
    target = loss_target.reshape(t, d)

    items = []
    for name in BIG:
        for layer in range(p[name].shape[0]):
            items.append((name, layer))
    chip1 = chip.reshape(1).astype(jnp.int32)
    placed = {}
    for name in BIG:
        for layer, buf in enumerate(place_shard(p[name], chip1, BF16, "place_shard")):
            placed[(name, layer)] = buf
    first = [('w_ffn1_gate', 0), ('w_ffn1_up', 0)]
    gathered, (conv_w4,) = allgather_weights([placed[it] for it in first],
                                             place_shard(conv_w, chip1, F32, "place_conv_w"))
    wt = dict(zip(first, gathered))
    waiting = [it for it in _use_order(depth) if it not in wt]

    def riders(name):
        room, take = CARRY_WEIGHTS[name], []
        for it in list(waiting):
            if placed[it].size <= room:
                room -= placed[it].size
                take.append(it)
                waiting.remove(it)
        return (take, gather_carry([placed[it] for it in take])) if take else (take, None)

    def landed(take, carried):
        wt.update(zip(take, carried))

    def weight(it):
        if it not in wt:
            waiting.remove(it)
            (wt[it],), _ = allgather_weights([placed[it]], [])
        return wt[it]

    c_mix = conv_w4.shape[2] * N_CHIPS
    conv_full = jnp.transpose(conv_w4, (1, 0, 2)).reshape(CONV_WIDTH, c_mix)
    vec = lambda a: a.reshape(1, -1)
    sp_bt = sp_b[0].T
    sp_wt = jnp.transpose(sp_w[0], (0, 2, 1))
    d_ff = w_ffn1_gate.shape[2]
    n_in = w_in_ab.shape[2]
    n_qkv = w_qkv.shape[2] // 3

    saved = []
    xc = xf
    h = rmsnorm_fwd(xc, vec(g_ffn1[0]), "norm_first")
    for layer in range(depth):
        s = {}
        for half, (gn, wn) in enumerate((('g_ffn1', 'w_ffn1'), ('g_ffn2', 'w_ffn2'))):
            if half == 1:
                s['x_mix'], s['h_mix'] = xc, h
                if layer % 2 == 0:
                    w_in = weight(('w_in_ab', layer // 2))
                    take, carry = riders("mm_in")
                    (z,), got = colmm(h, [w_in], n_in, BF16, "mm_in", carry)
                    landed(take, got)
                    cat, a1 = mix_fwd(z, conv_full, conv_b, ln_a_g, ln_a_b, vec(ln_v_g), vec(ln_v_b), sp_w[0], sp_bt, seq)
                    s.update(z=z, cat=cat, a1=a1)
                    w_out = weight(('w_out_ab', layer // 2))
                    take, carry = riders("mm_out")
                    (xc, h), got = rowmm(cat, w_out, xc, 1.0, "mm_out", carry, vec(g_ffn2[layer]))
                    landed(take, got)
                else:
                    (qkv,), _ = colmm(h, [weight(('w_qkv', layer // 2))], n_qkv, BF16, "mm_qkv")
                    o, tot, cnt = attn_fwd(qkv, n_seq, seq)
                    s.update(qkv=qkv, o=o, tot=tot, cnt=cnt)
                    (xc, h), _ = rowmm(o, weight(('w_o', layer // 2)), xc, 1.0, "mm_o", None, vec(g_ffn2[layer]))
            s['x' + wn] = xc
            w_gate, w_up = weight((wn + '_gate', layer)), weight((wn + '_up', layer))
            take, carry = riders("ffn_gateup")
            (silu, udsilu, act), got = colmm(h, [w_gate, w_up], d_ff, BF16, "ffn_gateup", carry, swiglu=True)
            landed(take, got)
            s.update({'h' + wn: h, 'swiglu' + wn: (silu, udsilu), 'act' + wn: act})
            w_down = weight((wn + '_down', layer))
            take, carry = riders("ffn_down")
            following = g_mix[layer] if half == 0 else (g_ffn1[layer + 1] if layer + 1 < depth else None)
            (xc, h), got = rowmm(act, w_down, xc, 0.5, "ffn_down", carry, None if following is None else vec(following))
            landed(take, got)
        saved.append(s)

    loss8, dx, dxb, dg_final = loss_head(xc, vec(g_final), target)
    loss = lax.psum(loss8[0, 0], ("x", "y", "c"))

    gw = {}
    gs = {}
    core1 = core.reshape(1).astype(jnp.int32)
    ready = []
    part, recv = {}, {}

    def leaving():
        its = list(ready)
        ready.clear()
        halves = lambda a: a.reshape(N_CHIPS, 2, a.shape[1] // 2, a.shape[2])
        theirs = rs_exchange([halves(gw[it][1]) for it in its])
        sums = rs_add([halves(gw[it][0]) for it in its], theirs, core1, REDUCE_DTYPE, "rs_add")
        part.update(zip(its, sums))
        return its, send_carry(sums)

    for layer in reversed(range(depth)):
        s = saved[layer]
        for half, (gn, wn) in reversed(list(enumerate((('g_ffn1', 'w_ffn1'), ('g_ffn2', 'w_ffn2'))))):
            wd = wt[(wn + '_down', layer)]
            dgate, dup = rowmm_t(dxb, wd, 0.5, BF16, "ffn_bwd_act", swiglu=s['swiglu' + wn])
            gw[(wn + '_down', layer)] = dw_row(s['act' + wn], dxb, 0.5, "ffn_dw_down")
            gw[(wn + '_gate', layer)], gw[(wn + '_up', layer)] = dw_col(s['h' + wn], [dgate, dup], N_CHIPS, d_ff,
                                                                        "ffn_dw_gateup", transposed=True)
            ready.extend([(wn + '_down', layer), (wn + '_gate', layer), (wn + '_up', layer)])
            its, carry = leaving()
            (dx, dxb, dg), got = colmm_t([dgate, dup], [wt[(wn + '_gate', layer)], wt[(wn + '_up', layer)]], d_ff,
                                         s['x' + wn], vec(p[gn][layer]), dx, "ffn_bwd_in", carry, transposed=True)
            recv.update(zip(its, got))
            gs[(gn, layer)] = dg
            if half == 1:
                if layer % 2 == 0:
                    i = layer // 2
                    w_out = wt[('w_out_ab', i)]
                    dcat = rowmm_t(dxb, w_out, 1.0, F32, "mm_out_t")
                    gw[('w_out_ab', i)] = dw_row(s['cat'], dxb, 1.0, "dw_out")
                    dz, da1, dcb, dlag, dlab, dlvg, dlvb, dspw, dspb = mix_bwd_point(
                        dcat, s['z'], s['a1'], ln_a_g, ln_a_b, vec(ln_v_g), vec(ln_v_b), sp_w[0], sp_wt, sp_bt, seq)
                    dz, dcw = mix_bwd_conv(dz, da1, s['z'], conv_full, seq)
                    gs.update({('conv_b', i): dcb, ('ln_a_g', i): dlag, ('ln_a_b', i): dlab, ('ln_v_g', i): dlvg,
                               ('ln_v_b', i): dlvb, ('sp_w', i): dspw, ('sp_b', i): dspb[:, :, 0], ('conv_w', i): dcw})
                    (gw[('w_in_ab', i)],) = dw_col(s['h_mix'], [dz], N_CHIPS, n_in, "dw_in")
                    ready.extend([('w_out_ab', i), ('w_in_ab', i)])
                    its, carry = leaving()
                    (dx, dxb, dg), got = colmm_t([dz], [wt[('w_in_ab', i)]], n_in, s['x_mix'], vec(g_mix[layer]), dx,
                                                 "mm_in_t", carry)
                    recv.update(zip(its, got))
                else:
                    i = layer // 2
                    w_o4 = wt[('w_o', i)]
                    do = rowmm_t(dxb, w_o4, 1.0, BF16, "mm_o_t")
                    gw[('w_o', i)] = dw_row(s['o'], dxb, 1.0, "dw_o")
                    dq, dk, dv = attn_bwd(s['qkv'], do, s['tot'], s['cnt'], n_seq, seq)
                    dqkv = jnp.concatenate([dq, dk, dv], axis=0)
                    (gw[('w_qkv', i)],) = dw_col(s['h_mix'], [dqkv], N_CHIPS, n_qkv, "dw_qkv")
                    ready.extend([('w_o', i), ('w_qkv', i)])
                    its, carry = leaving()
                    (dx, dxb, dg), got = colmm_t([dqkv], [wt[('w_qkv', i)]], n_qkv, s['x_mix'], vec(g_mix[layer]), dx,
                                                 "mm_qkv_t", carry)
                    recv.update(zip(its, got))
                gs[('g_mix', layer)] = dg
    grad_x = dx.reshape(x.shape)

    assert not ready and set(recv) == set(items)
    where = jnp.stack([chip, core]).astype(jnp.int32)
    fulls = []
    for name in BIG:
        its = [(name, layer) for layer in range(p[name].shape[0])]
        fulls.append(rs_sum([recv[it] for it in its], [part[it] for it in its], where, "rs_sum"))
    shared = rs_share(fulls)
    grads = {name: sh.reshape(p[name].shape) for name, sh in zip(BIG, shared)}

    stack = lambda name: jnp.concatenate([gs[(name, layer)].reshape((1,) + p[name].shape[1:]) for layer in range(p[name].shape[0])], axis=0)
    small_g = [stack(name) if name != 'g_final' else dg_final.reshape(p[name].shape) for name in SMALL]
    packed = _pack(small_g + [gs[('conv_w', 0)]])
    red = allreduce_small(packed)
    outs = _unpack(red, [p[name].shape for name in SMALL] + [(CONV_WIDTH, c_mix)])
    for name, g in zip(SMALL, outs[:-1]):
        grads[name] = g
    conv_g = outs[-1].reshape(CONV_WIDTH, N_CHIPS, c_mix // N_CHIPS)
    grads['conv_w'] = lax.dynamic_index_in_dim(conv_g, chip, axis=1, keepdims=False).reshape(conv_w.shape)

    delta, new_m, new_v = {}, {}, {}
    for name in BIG:
        shp = p[name].shape
        two = lambda a: a.reshape(shp[0] * shp[1], shp[2])
        dl, nm, nv = adamw(two(p[name]), two(grads[name]), two(p['m_' + name]), two(p['v_' + name]), "adamw")
        delta[name], new_m[name], new_v[name] = dl.reshape(shp), nm.reshape(shp), nv.reshape(shp)
    small_names = SMALL + ['conv_w']
    pk = lambda pre: _pack([p[pre + name] for name in small_names])
    dl, nm, nv = adamw(pk(''), _pack([grads[name] for name in small_names]), pk('m_'), pk('v_'), "adamw_small")
    shapes = [p[name].shape for name in small_names]
    for dst, val in ((delta, dl), (new_m, nm), (new_v, nv)):
        for name, a in zip(small_names, _unpack(val, shapes)):
            dst[name] = a

    return (loss, grad_x, *[back(n, d[n]) for d in (grads, delta, new_m, new_v) for n in WEIGHTS])
```

```python
import functools

import jax
import jax.numpy as jnp
from jax import lax
from jax.experimental import pallas as pl
from jax.experimental.pallas import tpu as pltpu

F32 = jnp.float32
BF16 = jnp.bfloat16
EPS = 1e-6
HEAD_DIM = 64
CONV_WIDTH = 31
CHUNK = 128
KBLK = 128
ATT_BLOCK = 256
ATT_LANES = 256
DW_TOKENS = 2048
CONV_ROWS = 64
MASKED = -1e30
STICK_GONE = -110.0
LANES = 128
HALO = 32
ADAM_LR, ADAM_B1, ADAM_B2, ADAM_EPS, ADAM_WD, ADAM_STEP = 0.001, 0.9, 0.999, 1e-08, 0.01, 10
VMEM_LIMIT = 56 * 1024 * 1024
MESH = pl.DeviceIdType.MESH
N_CHIPS = 4
N_DEV = 8
REDUCE_DTYPE = BF16


def _cparams(sem):
    return pltpu.CompilerParams(dimension_semantics=sem, vmem_limit_bytes=VMEM_LIMIT)


def _nt(a, b):
    return lax.dot_general(a, b, (((1,), (1,)), ((), ())), preferred_element_type=F32)


def _tn(a, b):
    return lax.dot_general(a, b, (((0,), (0,)), ((), ())), preferred_element_type=F32)


def _nn(a, b):
    return jnp.dot(a, b, preferred_element_type=F32)


def _sigmoid(x):
    return 0.5 * jnp.tanh(0.5 * x) + 0.5


def _tile(t, want):
    if t <= want:
        return t
    for cand in range(want - want % 8, 7, -8):
        if t % cand == 0:
            return cand
    raise ValueError((t, want))


def rmsnorm_fwd(x, g, name):
    t, d = x.shape
    tm = _tile(t, 512)

    def body(x_ref, g_ref, h_ref):
        xv = x_ref[...]
        r = lax.rsqrt(jnp.mean(xv * xv, axis=-1, keepdims=True) + EPS)
        h_ref[...] = (xv * r * g_ref[...]).astype(BF16)

    return pl.pallas_call(
        body, name=name, grid=(t // tm,),
        in_specs=[pl.BlockSpec((tm, d), lambda i: (i, 0)), pl.BlockSpec((1, d), lambda i: (0, 0))],
        out_specs=pl.BlockSpec((tm, d), lambda i: (i, 0)),
        out_shape=jax.ShapeDtypeStruct((t, d), BF16),
        compiler_params=_cparams(("parallel",)),
    )(x, g)


def colmm(h, ws, nu, out_dtype, name, carry=None, swiglu=False):
    t, k = h.shape
    j, nj = (ws[0].shape[0], ws[0].shape[1]) if swiglu else (ws[0].shape[0], ws[0].shape[2])
    per = nj // nu
    units = j * per
    tm = _tile(t, 1024)
    nw = len(ws)
    n_out = 3 if swiglu else nw

    def body(*refs):
        h_ref = refs[0]
        hv = h_ref[...]
        if swiglu:
            silu_ref, udsilu_ref, act_ref = refs[1 + nw:]
            gv = _nt(hv, refs[1][0])
            uv = _nt(hv, refs[2][0])
            s = _sigmoid(gv)
            silu = gv * s
            silu_ref[0] = silu.astype(out_dtype)
            udsilu_ref[0] = (uv * (s + silu * (1.0 - s))).astype(out_dtype)
            act_ref[0] = (silu * uv).astype(out_dtype)
            return
        for n in range(nw):
            res = _nn(hv, refs[1 + n][0]).astype(out_dtype)
            for u in range(per):
                refs[1 + nw + n][u] = res[:, u * nu:(u + 1) * nu]

    assert not swiglu or (nw == 2 and per == 1)
    w_spec = pl.BlockSpec((1, nj, k) if swiglu else (1, k, nj), lambda s, i: (s, 0, 0))
    o_spec = pl.BlockSpec((per, tm, nu), lambda s, i: (s, i, 0))
    return _call(
        body, name=name, grid=(j, t // tm),
        in_specs=[pl.BlockSpec((tm, k), lambda s, i: (i, 0))] + [w_spec] * nw,
        out_specs=[o_spec] * n_out,
        out_shape=[jax.ShapeDtypeStruct((units, t, nu), out_dtype)] * n_out,
        args=[h, *ws], sem=("parallel", "parallel"), carry=carry)


def rowmm(a, w, resid, scale, name, carry=None, norm_g=None):
    u_n, t, ku = a.shape
    n = w.shape[2]
    tm = _tile(t, 512)

    def body(a_ref, w_ref, r_ref, *rest):
        acc = jnp.zeros((tm, n), F32)
        for u in range(u_n):
            acc = acc + _nn(a_ref[u], w_ref[u])
        out = r_ref[...] + scale * acc
        if norm_g is None:
            (o_ref,) = rest
        else:
            g_ref, o_ref, h_ref = rest
            r = lax.rsqrt(jnp.mean(out * out, axis=-1, keepdims=True) + EPS)
            h_ref[...] = (out * r * g_ref[...]).astype(BF16)
        o_ref[...] = out

    row = pl.BlockSpec((tm, n), lambda i: (i, 0))
    normed = norm_g is not None
    outs, carried = _call(
        body, name=name, grid=(t // tm,),
        in_specs=[pl.BlockSpec((u_n, tm, ku), lambda i: (0, i, 0)), pl.BlockSpec((u_n, ku, n), lambda i: (0, 0, 0)),
                  row] + [pl.BlockSpec((1, n), lambda i: (0, 0))] * normed,
        out_specs=[row] + [row] * normed,
        out_shape=[jax.ShapeDtypeStruct((t, n), F32)] + [jax.ShapeDtypeStruct((t, n), BF16)] * normed,
        args=[a, w, resid] + [norm_g] * normed, sem=("parallel",), carry=carry)
    return (outs[0], outs[1] if normed else None), carried


def rowmm_t(dyb, w, scale, out_dtype, name, swiglu=None):
    t, n = dyb.shape
    u_n, ku, _ = w.shape
    tm = _tile(t, 512)

    if swiglu is None:
        def body(dy_ref, w_ref, o_ref):
            dy = dy_ref[...]
            for u in range(u_n):
                o_ref[u] = (scale * _nt(dy, w_ref[u])).astype(out_dtype)

        return pl.pallas_call(
            body, name=name, grid=(t // tm,),
            in_specs=[pl.BlockSpec((tm, n), lambda i: (i, 0)), pl.BlockSpec((u_n, ku, n), lambda i: (0, 0, 0))],
            out_specs=pl.BlockSpec((u_n, tm, ku), lambda i: (0, i, 0)),
            out_shape=jax.ShapeDtypeStruct((u_n, t, ku), out_dtype),
            compiler_params=_cparams(("parallel",)),
        )(dyb, w)

    def body(dy_ref, w_ref, silu_ref, udsilu_ref, dg_ref, du_ref):
        dy = dy_ref[...]
        for u in range(u_n):
            dact = scale * _nt(dy, w_ref[u])
            dg_ref[u] = (dact * udsilu_ref[u].astype(F32)).astype(BF16)
            du_ref[u] = (dact * silu_ref[u].astype(F32)).astype(BF16)

    blk = pl.BlockSpec((u_n, tm, ku), lambda i: (0, i, 0))
    return pl.pallas_call(
        body, name=name, grid=(t // tm,),
        in_specs=[pl.BlockSpec((tm, n), lambda i: (i, 0)), pl.BlockSpec((u_n, ku, n), lambda i: (0, 0, 0)), blk, blk],
        out_specs=[blk] * 2, out_shape=[jax.ShapeDtypeStruct((u_n, t, ku), BF16)] * 2,
        compiler_params=_cparams(("parallel",)),
    )(dyb, w, *swiglu)


def colmm_t(dzs, ws, nu, x, g, dy_in, name, carry=None, transposed=False):
    t, k = x.shape
    j, nj = (ws[0].shape[0], ws[0].shape[1]) if transposed else (ws[0].shape[0], ws[0].shape[2])
    per = nj // nu
    units = j * per
    nw = len(ws)
    tm = _tile(t, 512)
    assert not transposed or per == 1

    def body(*refs):
        dz_refs = refs[:nw]
        w_refs = refs[nw:2 * nw]
        x_ref, g_ref, dy_ref, dx_ref, dxb_ref, dg_ref = refs[2 * nw:]
        i = pl.program_id(0)
        dh = jnp.zeros((tm, k), F32)
        for n in range(nw):
            for u in range(units):
                if transposed:
                    dh = dh + _nn(dz_refs[n][u], w_refs[n][u])
                else:
                    wv = w_refs[n][u // per, :, (u % per) * nu:(u % per + 1) * nu]
                    dh = dh + _nt(dz_refs[n][u], wv)
        xv = x_ref[...]
        gv = g_ref[...]
        r = lax.rsqrt(jnp.mean(xv * xv, axis=-1, keepdims=True) + EPS)
        uu = dh * gv
        dx = dy_ref[...] + r * uu - xv * (r * r * r * jnp.mean(uu * xv, axis=-1, keepdims=True))
        dx_ref[...] = dx
        dxb_ref[...] = dx.astype(BF16)
        part = jnp.sum(dh * (xv * r), axis=0, keepdims=True)

        @pl.when(i == 0)
        def _():
            dg_ref[...] = part

        @pl.when(i > 0)
        def _():
            dg_ref[...] += part

    dz_spec = pl.BlockSpec((units, tm, nu), lambda i: (0, i, 0))
    w_spec = pl.BlockSpec((j, nj, k) if transposed else (j, k, nj), lambda i: (0, 0, 0))
    row = pl.BlockSpec((tm, k), lambda i: (i, 0))
    vec = pl.BlockSpec((1, k), lambda i: (0, 0))
    return _call(
        body, name=name, grid=(t // tm,),
        in_specs=[dz_spec] * nw + [w_spec] * nw + [row, vec, row],
        out_specs=[row, row, vec],
        out_shape=[jax.ShapeDtypeStruct((t, k), F32), jax.ShapeDtypeStruct((t, k), BF16),
                   jax.ShapeDtypeStruct((1, k), F32)],
        args=[*dzs, *ws, x, g, dy_in], sem=("arbitrary",), carry=carry)


def dw_col(h, dzs, j, nu, name, transposed=False):
    t, k = h.shape
    units = dzs[0].shape[0]
    per = units // j
    nw = len(dzs)
    tt = _tile(t, DW_TOKENS)
    assert not transposed or per == 1

    def body(*refs):
        h_ref = refs[0]
        s = pl.program_id(1)
        hv = h_ref[...]
        outs, copies = refs[1 + nw:1 + 2 * nw], refs[1 + 2 * nw:]

        @pl.when(s == 0)
        def _():
            for o_ref in outs:
                o_ref[...] = jnp.zeros_like(o_ref)

        for n in range(nw):
            if transposed:
                outs[n][0] += _tn(refs[1 + n][0], hv)
                continue
            for u in range(per):
                outs[n][0, :, u * nu:(u + 1) * nu] += _tn(hv, refs[1 + n][u])

        @pl.when(s == pl.num_programs(1) - 1)
        def _():
            for o_ref, c_ref in zip(outs, copies):
                c_ref[...] = o_ref[...].astype(REDUCE_DTYPE)

    shard = (nu, k) if transposed else (k, per * nu)
    o_spec = pl.BlockSpec((1,) + shard, lambda u, s: (u, 0, 0))
    res = pl.pallas_call(
        body, name=name, grid=(j, t // tt),
        in_specs=[pl.BlockSpec((tt, k), lambda u, s: (s, 0))] + [pl.BlockSpec((per, tt, nu), lambda u, s: (u, s, 0))] * nw,
        out_specs=[o_spec] * (2 * nw),
        out_shape=[jax.ShapeDtypeStruct((j,) + shard, F32)] * nw + [jax.ShapeDtypeStruct((j,) + shard, REDUCE_DTYPE)] * nw,
        compiler_params=_cparams(("parallel", "arbitrary")),
    )(h, *dzs)
    return list(zip(res[:nw], res[nw:]))


def dw_row(a, dyb, scale, name):
    u_n, t, ku = a.shape
    n = dyb.shape[1]
    tt = _tile(t, DW_TOKENS)

    def body(a_ref, dy_ref, o_ref, c_ref):
        @pl.when(pl.program_id(1) == 0)
        def _():
            o_ref[...] = jnp.zeros_like(o_ref)

        o_ref[0] += scale * _tn(a_ref[0], dy_ref[...])

        @pl.when(pl.program_id(1) == pl.num_programs(1) - 1)
        def _():
            c_ref[...] = o_ref[...].astype(REDUCE_DTYPE)

    o_spec = pl.BlockSpec((1, ku, n), lambda u, s: (u, 0, 0))
    return tuple(pl.pallas_call(
        body, name=name, grid=(u_n, t // tt),
        in_specs=[pl.BlockSpec((1, tt, ku), lambda u, s: (u, s, 0)), pl.BlockSpec((tt, n), lambda u, s: (s, 0))],
        out_specs=[o_spec, o_spec],
        out_shape=[jax.ShapeDtypeStruct((u_n, ku, n), F32), jax.ShapeDtypeStruct((u_n, ku, n), REDUCE_DTYPE)],
        compiler_params=_cparams(("parallel", "arbitrary")),
    )(a, dyb))


def loss_head(x, g, target):
    t, d = x.shape
    tm = _tile(t, 256)

    def body(x_ref, g_ref, t_ref, loss_ref, dx_ref, dxb_ref, dg_ref):
        i = pl.program_id(0)
        xv = x_ref[...]
        gv = g_ref[...]
        r = lax.rsqrt(jnp.mean(xv * xv, axis=-1, keepdims=True) + EPS)
        xh = xv * r
        err = xh * gv - t_ref[...]
        dy = err * (1.0 / d)
        uu = dy * gv
        dx = r * uu - xv * (r * r * r * jnp.mean(uu * xv, axis=-1, keepdims=True))
        dx_ref[...] = dx
        dxb_ref[...] = dx.astype(BF16)
        dg_part = jnp.sum(dy * xh, axis=0, keepdims=True)
        row = jnp.sum(err * err, axis=-1, keepdims=True) * (0.5 / d)
        l_part = jnp.zeros((8, LANES), F32) + jnp.sum(row, axis=0, keepdims=True)

        @pl.when(i == 0)
        def _():
            dg_ref[...] = dg_part
            loss_ref[...] = l_part

        @pl.when(i > 0)
        def _():
            dg_ref[...] += dg_part
            loss_ref[...] += l_part

    row = pl.BlockSpec((tm, d), lambda i: (i, 0))
    vec = pl.BlockSpec((1, d), lambda i: (0, 0))
    return pl.pallas_call(
        body, name="loss_head", grid=(t // tm,),
        in_specs=[row, vec, row],
        out_specs=[pl.BlockSpec((8, LANES), lambda i: (0, 0)), row, row, vec],
        out_shape=[jax.ShapeDtypeStruct((8, LANES), F32), jax.ShapeDtypeStruct((t, d), F32),
                   jax.ShapeDtypeStruct((t, d), BF16), jax.ShapeDtypeStruct((1, d), F32)],
        compiler_params=_cparams(("arbitrary",)),
    )(x, g, target)


def _split(v):
    hi = v.astype(BF16)
    lo = (v - hi.astype(F32)).astype(BF16)
    return hi, lo


def _keysums(v, m_ext):
    hi, lo = _split(v)
    outs = []
    for j in range(v.shape[1] // KBLK):
        sl = slice(j * KBLK, (j + 1) * KBLK)
        cs = _nn(jnp.concatenate([hi[:, sl], lo[:, sl]], axis=1), m_ext)
        outs.append((cs[:, :KBLK], cs[:, KBLK:]))
    return outs


def _softplus_parts(z):
    sp = jnp.maximum(z, 0.0) + jnp.log(1.0 + jnp.exp(-jnp.abs(z)))
    return sp, z - sp


def _sum_matrices():
    r = lax.broadcasted_iota(jnp.int32, (2 * KBLK, 2 * KBLK), 0) % KBLK
    c = lax.broadcasted_iota(jnp.int32, (2 * KBLK, 2 * KBLK), 1)
    suffix = jnp.where((r > c) | (c >= KBLK), 1.0, 0.0).astype(BF16)
    prefix = jnp.where((r <= c) | (c >= KBLK), 1.0, 0.0).astype(BF16)
    return suffix, prefix


def _att_geometry(qkv, seq):
    upp = qkv.shape[0] // 3
    bq = min(ATT_BLOCK, seq)
    per_unit = (2 * LANES) // ATT_LANES
    return upp, bq, seq // bq, bq // KBLK, per_unit, upp * per_unit, ATT_LANES // HEAD_DIM


def _head_lanes(rows, heads):
    lane = lax.broadcasted_iota(jnp.int32, (rows, ATT_LANES), 1)
    return [(lane >= HEAD_DIM * h) & (lane < HEAD_DIM * (h + 1)) for h in range(heads)]


def attn_fwd(qkv, n_seq, seq):
    t = qkv.shape[1]
    upp, bq, nq, nsub, per_unit, groups, heads = _att_geometry(qkv, seq)
    suffix_m, _ = _sum_matrices()

    def body(q_ref, k_ref, v_ref, m_ref, o_ref, tot_ref, cnt_ref):
        qi = pl.program_id(2)
        step_id = (pl.program_id(0) * groups + pl.program_id(1)) * nq + qi
        in_head = _head_lanes(bq, heads)
        only = lambda v, h: jnp.where(in_head[h], v, jnp.zeros_like(v))
        q_all = q_ref[0] * jnp.asarray(HEAD_DIM ** -0.5, BF16)
        qs = [only(q_all, h) for h in range(heads)]
        m_ext = m_ref[...]
        row = lax.broadcasted_iota(jnp.int32, (bq, bq), 0)
        col = lax.broadcasted_iota(jnp.int32, (bq, bq), 1)
        diag_mask = col < row

        def block(kj, carry, mask):
            off = pl.multiple_of(kj * bq, bq)
            k_all = k_ref[0, pl.ds(off, bq), :]
            v_all = v_ref[0, pl.ds(off, bq), :]
            rems, acc = carry
            out = []
            for h in range(heads):
                rem = rems[h]
                z = _nt(qs[h], k_all)
                if mask is not None:
                    z = jnp.where(mask, z, MASKED)
                sp, ls = _softplus_parts(z)
                sums = _keysums(-sp, m_ext)
                parts = [None] * nsub
                for j in reversed(range(nsub)):
                    suf, total = sums[j]
                    parts[j] = jnp.exp(ls[:, j * KBLK:(j + 1) * KBLK] + suf + rem)
                    rem = rem + total
                a = jnp.concatenate(parts, axis=1)
                acc = acc + _nn(a.astype(BF16), only(v_all, h))
                out.append(rem)
            return tuple(out), acc

        def most_left(c):
            return functools.reduce(jnp.maximum, [jnp.max(r) for r in c[0]])

        def more(s):
            return (s[0] < qi) & (s[1] > STICK_GONE)

        def step(s):
            c = block(qi - 1 - s[0], s[2], None)
            return s[0] + 1, most_left(c), c

        zero = jnp.zeros((bq, LANES), F32)
        carry = block(qi, ((zero,) * heads, jnp.zeros((bq, ATT_LANES), F32)), diag_mask)
        n_left, _, (rems, acc) = lax.while_loop(more, step, (jnp.int32(0), most_left(carry), carry))
        o_ref[0] = acc.astype(BF16)
        first = lax.broadcasted_iota(jnp.int32, (bq, LANES), 1) < HEAD_DIM
        tot_ref[...] = jnp.concatenate([jnp.where(first, rems[h], rems[h + 1]) for h in range(0, heads, 2)], axis=1)
        cnt_ref[step_id] = n_left.astype(F32)

    qblk = lambda b, g, i: (g // per_unit, b * nq + i, g % per_unit)
    return pl.pallas_call(
        body, name="attn_fwd", grid=(n_seq, groups, nq),
        in_specs=[pl.BlockSpec((1, bq, ATT_LANES), qblk),
                  pl.BlockSpec((1, seq, ATT_LANES), lambda b, g, i: (upp + g // per_unit, b, g % per_unit)),
                  pl.BlockSpec((1, seq, ATT_LANES), lambda b, g, i: (2 * upp + g // per_unit, b, g % per_unit)),
                  pl.BlockSpec((2 * KBLK, 2 * KBLK), lambda b, g, i: (0, 0))],
        out_specs=[pl.BlockSpec((1, bq, ATT_LANES), qblk),
                   pl.BlockSpec((bq, ATT_LANES), lambda b, g, i: (b * nq + i, g)),
                   pl.BlockSpec(memory_space=pltpu.SMEM)],
        out_shape=[jax.ShapeDtypeStruct((upp, t, 2 * LANES), BF16), jax.ShapeDtypeStruct((t, upp * 2 * LANES), F32),
                   jax.ShapeDtypeStruct((n_seq * groups * nq,), F32)],
        compiler_params=_cparams(("arbitrary", "arbitrary", "arbitrary")),
    )(qkv, qkv, qkv, suffix_m)


def attn_bwd(qkv, do, tot, cnt, n_seq, seq):
    t = qkv.shape[1]
    upp, bq, nq, nsub, per_unit, groups, heads = _att_geometry(qkv, seq)
    _, prefix_m = _sum_matrices()
    scale = HEAD_DIM ** -0.5

    def body(q_ref, k_ref, v_ref, do_ref, tot_ref, m_ref, cnt_ref, dq_ref, dk_ref, dv_ref, dk_acc, dv_acc):
        qi = pl.program_id(2)
        step_id = (pl.program_id(0) * groups + pl.program_id(1)) * nq + qi
        n_left = jnp.clip(cnt_ref[step_id].astype(jnp.int32), 0, qi)
        in_head = _head_lanes(bq, heads)
        only = lambda v, h: jnp.where(in_head[h], v, jnp.zeros_like(v))
        q_all = q_ref[0] * jnp.asarray(scale, BF16)
        do_all = do_ref[0]
        qs = [only(q_all, h) for h in range(heads)]
        dos = [only(do_all, h) for h in range(heads)]
        first = lax.broadcasted_iota(jnp.int32, (bq, LANES), 1) < HEAD_DIM
        tots = []
        for h in range(0, heads, 2):
            both = tot_ref[:, h // 2 * LANES:(h // 2 + 1) * LANES]
            swapped = pltpu.roll(both, HEAD_DIM, 1)
            tots += [jnp.where(first, both, swapped), jnp.where(first, swapped, both)]
        m_ext = m_ref[...]
        row = lax.broadcasted_iota(jnp.int32, (bq, bq), 0)
        col = lax.broadcasted_iota(jnp.int32, (bq, bq), 1)
        diag_mask = col < row

        @pl.when(qi == 0)
        def _():
            dk_acc[...] = jnp.zeros_like(dk_acc)
            dv_acc[...] = jnp.zeros_like(dv_acc)

        def block(kj, carry, mask):
            off = pl.multiple_of(kj * bq, bq)
            k_all = k_ref[0, pl.ds(off, bq), :]
            v_all = v_ref[0, pl.ds(off, bq), :]
            pres, gpres, dq = carry
            dk_part = jnp.zeros((bq, ATT_LANES), F32)
            dv_part = jnp.zeros((bq, ATT_LANES), F32)
            pres_out, gpres_out = [], []
            for h in range(heads):
                pre, gpre = pres[h], gpres[h]
                z = _nt(qs[h], k_all)
                if mask is not None:
                    z = jnp.where(mask, z, MASKED)
                sp, ls = _softplus_parts(z)
                sums = _keysums(-sp, m_ext)
                parts = []
                for j in range(nsub):
                    pin, ptot = sums[j]
                    parts.append(jnp.exp(ls[:, j * KBLK:(j + 1) * KBLK] + (tots[h] - (pre + pin))))
                    pre = pre + ptot
                a = jnp.concatenate(parts, axis=1)
                g = a * _nt(dos[h], v_all)
                gsums = _keysums(g, m_ext)
                parts = []
                for j in range(nsub):
                    gin, gtot = gsums[j]
                    parts.append(gpre + gin)
                    gpre = gpre + gtot
                dz = g - jnp.exp(ls) * jnp.concatenate(parts, axis=1)
                dzb = dz.astype(BF16)
                dq = dq + _nn(dzb, only(k_all, h))
                dk_part = dk_part + _tn(dzb, qs[h])
                dv_part = dv_part + _tn(a.astype(BF16), dos[h])
                pres_out.append(pre)
                gpres_out.append(gpre)
            dk_acc[pl.ds(off, bq), :] += dk_part
            dv_acc[pl.ds(off, bq), :] += dv_part
            return tuple(pres_out), tuple(gpres_out), dq

        zero = jnp.zeros((bq, LANES), F32)
        carry = ((zero,) * heads, (zero,) * heads, jnp.zeros((bq, ATT_LANES), F32))
        carry = lax.fori_loop(qi - n_left, qi, lambda kj, c: block(kj, c, None), carry)
        carry = block(qi, carry, diag_mask)
        dq_ref[0] = (carry[2] * scale).astype(BF16)

        @pl.when(qi == nq - 1)
        def _():
            dk_ref[0] = dk_acc[...].astype(BF16)
            dv_ref[0] = dv_acc[...].astype(BF16)

    qblk = lambda b, g, i: (g // per_unit, b * nq + i, g % per_unit)
    kv_out = pl.BlockSpec((1, seq, ATT_LANES), lambda b, g, i: (g // per_unit, b, g % per_unit))
    shp = jax.ShapeDtypeStruct((upp, t, 2 * LANES), BF16)
    return pl.pallas_call(
        body, name="attn_bwd", grid=(n_seq, groups, nq),
        in_specs=[pl.BlockSpec((1, bq, ATT_LANES), qblk),
                  pl.BlockSpec((1, seq, ATT_LANES), lambda b, g, i: (upp + g // per_unit, b, g % per_unit)),
                  pl.BlockSpec((1, seq, ATT_LANES), lambda b, g, i: (2 * upp + g // per_unit, b, g % per_unit)),
                  pl.BlockSpec((1, bq, ATT_LANES), qblk),
                  pl.BlockSpec((bq, ATT_LANES), lambda b, g, i: (b * nq + i, g)),
                  pl.BlockSpec((2 * KBLK, 2 * KBLK), lambda b, g, i: (0, 0)),
                  pl.BlockSpec(memory_space=pltpu.SMEM)],
        out_specs=[pl.BlockSpec((1, bq, ATT_LANES), qblk), kv_out, kv_out],
        out_shape=[shp, shp, shp],
        scratch_shapes=[pltpu.VMEM((seq, ATT_LANES), F32), pltpu.VMEM((seq, ATT_LANES), F32)],
        compiler_params=_cparams(("parallel", "parallel", "arbitrary")),
    )(qkv, qkv, qkv, do, tot, prefix_m, cnt)


def _ln_stats(v):
    mu = jnp.mean(v, axis=-1, keepdims=True)
    vc = v - mu
    rstd = lax.rsqrt(jnp.mean(vc * vc, axis=-1, keepdims=True) + EPS)
    return vc * rstd, rstd


def _glu_into(a0_ref, av_ref, ag_ref, hv_ref, hg_ref, first):
    hv = hv_ref[0].astype(F32)
    hg = hg_ref[0].astype(F32)
    a0_ref[0:HALO, :] = jnp.where(first, 0.0, hv * _sigmoid(hg))
    av = av_ref[0].astype(F32)
    ag = ag_ref[0].astype(F32)
    a0_ref[HALO:, :] = av * _sigmoid(ag)


def _shifted_taps(ref, shifted_ref, tm, first):
    taps = []
    for b in range(8):
        offs = [o for o in range(first, first + CONV_WIDTH) if o % 8 == b]
        n_rows = max(offs) - b + tm
        shifted_ref[b, 0:n_rows, :] = ref[pl.ds(b, n_rows), :]
        taps += [(b, o - b, o - first) for o in offs]
    return taps


def _tril_mask():
    r = lax.broadcasted_iota(jnp.int32, (CHUNK, CHUNK), 0)
    c = lax.broadcasted_iota(jnp.int32, (CHUNK, CHUNK), 1)
    return c <= r


def mix_fwd(z, conv_w, conv_b, ln_a_g, ln_a_b, ln_v_g, ln_v_b, sp_w, sp_bt, seq):
    _, t, c = z.shape
    tm = _tile(seq, 512)
    tiles_per_seq = seq // tm
    groups = c // LANES
    hb = tm // HALO

    def body(av_ref, ag_ref, u_ref, v_ref, hv_ref, hg_ref, cw_ref, cb_ref, lag_ref, lab_ref, lvg_ref, lvb_ref,
             spw_ref, spb_ref, cat_ref, a1_ref, a0_ref, sh_ref):
        i = pl.program_id(0)
        _glu_into(a0_ref, av_ref, ag_ref, hv_ref, hg_ref, i % tiles_per_seq == 0)
        acc = jnp.zeros((tm, c), F32) + cb_ref[...]
        for b, ro, k in _shifted_taps(a0_ref, sh_ref, tm, HALO - (CONV_WIDTH - 1)):
            acc = acc + cw_ref[k:k + 1, :] * sh_ref[b, pl.ds(ro, tm), :]
        a1_ref[...] = acc
        xh, _ = _ln_stats(acc)
        a2 = xh * lag_ref[...] + lab_ref[...]
        a3 = (a2 * _sigmoid(a2)).astype(BF16)
        half = c // 2
        cat_ref[0] = a3[:, :half]
        cat_ref[1] = a3[:, half:]
        tril = _tril_mask()
        for g in range(groups):
            sl = slice(g * LANES, (g + 1) * LANES)
            xh, _ = _ln_stats(v_ref[0][:, sl].astype(F32))
            vn = (xh * lvg_ref[:, sl] + lvb_ref[:, sl]).astype(BF16)
            w = jnp.where(tril, spw_ref[g], 0.0).astype(BF16)
            bias = spb_ref[:, g:g + 1]
            for ch in range(tm // CHUNK):
                rows = slice(ch * CHUNK, (ch + 1) * CHUNK)
                vs = _nn(w, vn[rows]) + bias
                bo = (u_ref[0][rows, sl].astype(F32) * vs).astype(BF16)
                cat_ref[2 + (g * LANES) // half, rows, (g * LANES) % half:(g * LANES) % half + LANES] = bo

    unit = lambda u: pl.BlockSpec((1, tm, c), lambda i: (u, i, 0))
    halo = lambda u: pl.BlockSpec((1, HALO, c), lambda i: (u, jnp.maximum(i * hb - 1, 0), 0))
    vec = pl.BlockSpec((1, c), lambda i: (0, 0))
    return pl.pallas_call(
        body, name="mix_fwd", grid=(t // tm,),
        in_specs=[unit(0), unit(1), unit(2), unit(3), halo(0), halo(1),
                  pl.BlockSpec((CONV_WIDTH, c), lambda i: (0, 0)), vec, vec, vec, vec, vec,
                  pl.BlockSpec((groups, CHUNK, CHUNK), lambda i: (0, 0, 0)),
                  pl.BlockSpec((CHUNK, groups), lambda i: (0, 0))],
        out_specs=[pl.BlockSpec((4, tm, c // 2), lambda i: (0, i, 0)), pl.BlockSpec((tm, c), lambda i: (i, 0))],
        out_shape=[jax.ShapeDtypeStruct((4, t, c // 2), BF16), jax.ShapeDtypeStruct((t, c), F32)],
        scratch_shapes=[pltpu.VMEM((HALO + tm, c), F32), pltpu.VMEM((8, HALO + tm, c), F32)],
        compiler_params=_cparams(("parallel",)),
    )(z, z, z, z, z, z, conv_w, conv_b, ln_a_g, ln_a_b, ln_v_g, ln_v_b, sp_w, sp_bt)


def mix_bwd_point(dcat, z, a1, ln_a_g, ln_a_b, ln_v_g, ln_v_b, sp_w, sp_wt, sp_bt, seq):
    _, t, c = z.shape
    tm = _tile(seq, 512)
    groups = c // LANES
    half = c // 2

    def body(dc_ref, u_ref, v_ref, a1_ref, lag_ref, lab_ref, lvg_ref, lvb_ref, spw_ref, spwt_ref, spb_ref,
             dz_ref, da1_ref, dcb_ref, dlag_ref, dlab_ref, dlvg_ref, dlvb_ref, dspw_ref, dspb_ref):
        i = pl.program_id(0)
        last = pl.num_programs(0) - 1

        @pl.when(i == 0)
        def _():
            for r in (dcb_ref, dlag_ref, dlab_ref, dlvg_ref, dlvb_ref, dspw_ref, dspb_ref):
                r[...] = jnp.zeros_like(r)

        da3 = jnp.concatenate([dc_ref[0], dc_ref[1]], axis=-1)
        xh, rstd = _ln_stats(a1_ref[...])
        a2 = xh * lag_ref[...] + lab_ref[...]
        s = _sigmoid(a2)
        da2 = da3 * (s * (1.0 + a2 * (1.0 - s)))
        dlag_ref[...] += jnp.sum(da2 * xh, axis=0, keepdims=True)
        dlab_ref[...] += jnp.sum(da2, axis=0, keepdims=True)
        dxh = da2 * lag_ref[...]
        da1 = rstd * (dxh - jnp.mean(dxh, axis=-1, keepdims=True) - xh * jnp.mean(dxh * xh, axis=-1, keepdims=True))
        da1_ref[...] = da1
        dcb_ref[...] += jnp.sum(da1, axis=0, keepdims=True)

        tril = _tril_mask()
        for g in range(groups):
            sl = slice(g * LANES, (g + 1) * LANES)
            xh, rstd = _ln_stats(v_ref[0][:, sl].astype(F32))
            lg = lvg_ref[:, sl]
            vnb = (xh * lg + lvb_ref[:, sl]).astype(BF16)
            w = jnp.where(tril, spw_ref[g], 0.0).astype(BF16)
            wt = jnp.where(tril.T, spwt_ref[g], 0.0).astype(BF16)
            bias = spb_ref[:, g:g + 1]
            dbo_all = dc_ref[2 + (g * LANES) // half][:, (g * LANES) % half:(g * LANES) % half + LANES]
            dvn_parts = []
            dw_acc = jnp.zeros((CHUNK, CHUNK), F32)
            db_acc = jnp.zeros((CHUNK, LANES), F32)
            for ch in range(tm // CHUNK):
                rows = slice(ch * CHUNK, (ch + 1) * CHUNK)
                vs = _nn(w, vnb[rows]) + bias
                dbo = dbo_all[rows]
                uv = u_ref[0][rows, sl].astype(F32)
                dz_ref[0, rows, sl] = (dbo * vs).astype(BF16)
                dvs = dbo * uv
                dvsb = dvs.astype(BF16)
                dvn_parts.append(_nn(wt, dvsb))
                dw_acc = dw_acc + _nt(dvsb, vnb[rows])
                db_acc = db_acc + dvs
            dvn = jnp.concatenate(dvn_parts, axis=0)
            dspw_ref[g] += jnp.where(tril, dw_acc, 0.0)
            dspb_ref[g] += db_acc
            dlvg_ref[:, sl] += jnp.sum(dvn * xh, axis=0, keepdims=True)
            dlvb_ref[:, sl] += jnp.sum(dvn, axis=0, keepdims=True)
            dxh = dvn * lg
            dv = rstd * (dxh - jnp.mean(dxh, axis=-1, keepdims=True) - xh * jnp.mean(dxh * xh, axis=-1, keepdims=True))
            dz_ref[1, :, sl] = dv.astype(BF16)

        @pl.when(i == last)
        def _():
            for g in range(groups):
                dspb_ref[g] = jnp.zeros((CHUNK, LANES), F32) + jnp.sum(dspb_ref[g], axis=-1, keepdims=True)

    unit = lambda u: pl.BlockSpec((1, tm, c), lambda i: (u, i, 0))
    vec = pl.BlockSpec((1, c), lambda i: (0, 0))
    sq = pl.BlockSpec((groups, CHUNK, CHUNK), lambda i: (0, 0, 0))
    vshape = jax.ShapeDtypeStruct((1, c), F32)
    sshape = jax.ShapeDtypeStruct((groups, CHUNK, CHUNK), F32)
    return pl.pallas_call(
        body, name="mix_bwd_point", grid=(t // tm,),
        in_specs=[pl.BlockSpec((4, tm, half), lambda i: (0, i, 0)), unit(2), unit(3),
                  pl.BlockSpec((tm, c), lambda i: (i, 0)), vec, vec, vec, vec, sq, sq,
                  pl.BlockSpec((CHUNK, groups), lambda i: (0, 0))],
        out_specs=[pl.BlockSpec((2, tm, c), lambda i: (1, i, 0)), pl.BlockSpec((tm, c), lambda i: (i, 0)),
                   vec, vec, vec, vec, vec, sq, sq],
        out_shape=[jax.ShapeDtypeStruct((4, t, c), BF16), jax.ShapeDtypeStruct((t, c), F32),
                   vshape, vshape, vshape, vshape, vshape, sshape, sshape],
        compiler_params=_cparams(("arbitrary",)),
    )(dcat, z, z, a1, ln_a_g, ln_a_b, ln_v_g, ln_v_b, sp_w, sp_wt, sp_bt)


def mix_bwd_conv(dz, da1, z, conv_w, seq):
    _, t, c = z.shape
    tm = _tile(seq, 512)
    tiles_per_seq = seq // tm
    hb = tm // HALO
    n_halo_blocks = t // HALO

    rc = _tile(tm, CONV_ROWS)

    def body(dz_in_ref, d_ref, dh_ref, av_ref, ag_ref, cw_ref, dz_ref, dcw_ref, d1_ref, sh_ref, part_ref):
        del dz_in_ref
        i = pl.program_id(0)

        @pl.when(i == 0)
        def _():
            part_ref[...] = jnp.zeros_like(part_ref)

        d1_ref[0:tm, :] = d_ref[...]
        d1_ref[tm:, :] = jnp.where((i + 1) % tiles_per_seq == 0, 0.0, dh_ref[...])
        taps = _shifted_taps(d1_ref, sh_ref, tm, 0)

        def chunk(ci, carry):
            r0 = pl.multiple_of(ci * rc, rc)
            av = av_ref[0, pl.ds(r0, rc), :].astype(F32)
            s = _sigmoid(ag_ref[0, pl.ds(r0, rc), :].astype(F32))
            a0 = av * s
            da0 = jnp.zeros((rc, c), F32)
            for b, ro, back in taps:
                k = CONV_WIDTH - 1 - back
                rows = sh_ref[b, pl.ds(r0 + ro, rc), :]
                da0 = da0 + cw_ref[k:k + 1, :] * rows
                prod = a0 * rows
                part_ref[k] += functools.reduce(lambda p, q: p + q, [prod[8 * r:8 * r + 8] for r in range(rc // 8)])
            dz_ref[0, pl.ds(r0, rc), :] = (da0 * s).astype(BF16)
            dz_ref[1, pl.ds(r0, rc), :] = (da0 * av * s * (1.0 - s)).astype(BF16)
            return carry

        lax.fori_loop(0, tm // rc, chunk, 0)

        @pl.when(i == pl.num_programs(0) - 1)
        def _():
            dcw_ref[...] = jnp.sum(part_ref[...], axis=1)

    unit = lambda u: pl.BlockSpec((1, tm, c), lambda i: (u, i, 0))
    return pl.pallas_call(
        body, name="mix_bwd_conv", grid=(t // tm,),
        in_specs=[pl.BlockSpec(memory_space=pl.ANY), pl.BlockSpec((tm, c), lambda i: (i, 0)),
                  pl.BlockSpec((HALO, c), lambda i: (jnp.minimum((i + 1) * hb, n_halo_blocks - 1), 0)),
                  unit(0), unit(1), pl.BlockSpec((CONV_WIDTH, c), lambda i: (0, 0))],
        out_specs=[pl.BlockSpec((2, tm, c), lambda i: (0, i, 0)), pl.BlockSpec((CONV_WIDTH, c), lambda i: (0, 0))],
        out_shape=[jax.ShapeDtypeStruct(dz.shape, BF16), jax.ShapeDtypeStruct((CONV_WIDTH, c), F32)],
        scratch_shapes=[pltpu.VMEM((tm + HALO, c), F32), pltpu.VMEM((8, tm + HALO, c), F32),
                        pltpu.VMEM((CONV_WIDTH, 8, c), F32)],
        input_output_aliases={0: 0},
        compiler_params=_cparams(("arbitrary",)),
    )(dz, da1, da1, z, z, conv_w)


CHIP_FLIPS = ((1, 0), (0, 1), (1, 1))
ANY = pl.BlockSpec(memory_space=pl.ANY)


def _place():
    return lax.axis_index("x"), lax.axis_index("y"), lax.axis_index("c")


def _flip(v, f):
    return 1 - v if f else v


def place_shard(w, chip, dtype, name):
    n_layers, r, cc = w.shape
    rb = _tile(r, 512)

    def body(chip_ref, w_ref, *o_refs):
        del chip_ref
        for layer, o_ref in enumerate(o_refs):
            o_ref[0] = w_ref[layer].astype(dtype)

    return pl.pallas_call(
        body, name=name,
        grid_spec=pltpu.PrefetchScalarGridSpec(
            num_scalar_prefetch=1, grid=(r // rb,),
            in_specs=[pl.BlockSpec((n_layers, rb, cc), lambda i, chip_ref: (0, i, 0))],
            out_specs=[pl.BlockSpec((1, rb, cc), lambda i, chip_ref: (chip_ref[0], i, 0))] * n_layers),
        out_shape=[jax.ShapeDtypeStruct((N_CHIPS, r, cc), dtype)] * n_layers,
        compiler_params=_cparams(("parallel",)),
    )(chip, w)


class Carry:
    def __init__(self, arrays, out_shapes, aliased, sem_shapes, start, finish):
        self.arrays, self.out_shapes, self.aliased, self.sem_shapes = list(arrays), list(out_shapes), aliased, list(sem_shapes)
        self.start, self.finish = start, finish


def _call(body, *, name, grid, in_specs, out_specs, out_shape, args, sem, scratch_shapes=(), carry=None):
    if carry is None:
        res = pl.pallas_call(body, name=name, grid=grid, in_specs=in_specs, out_specs=out_specs, out_shape=out_shape,
                             scratch_shapes=list(scratch_shapes), compiler_params=_cparams(sem))(*args)
        return list(res), []
    n_in, n_out, n_scr, nc = len(args), len(out_shape), len(scratch_shapes), len(carry.arrays)

    def full_body(*refs):
        ins, refs = refs[:n_in], refs[n_in:]
        c_ins, refs = refs[:nc], refs[nc:]
        outs, refs = refs[:n_out], refs[n_out:]
        c_outs, refs = refs[:nc], refs[nc:]
        scr, sems = refs[:n_scr], refs[n_scr:]
        first = functools.reduce(lambda a, b: a & b, [pl.program_id(d) == 0 for d in range(len(grid))])
        last = functools.reduce(lambda a, b: a & b, [pl.program_id(d) == grid[d] - 1 for d in range(len(grid))])

        @pl.when(first)
        def _():
            carry.start(c_ins, c_outs, sems)

        body(*ins, *outs, *scr)

        @pl.when(last)
        def _():
            carry.finish(c_ins, c_outs, sems)

    res = pl.pallas_call(
        full_body, name=name, grid=grid, in_specs=list(in_specs) + [ANY] * nc, out_specs=list(out_specs) + [ANY] * nc,
        out_shape=list(out_shape) + carry.out_shapes, scratch_shapes=list(scratch_shapes) + carry.sem_shapes,
        input_output_aliases={n_in + i: n_out + i for i in range(nc)} if carry.aliased else {},
        compiler_params=pltpu.CompilerParams(dimension_semantics=("arbitrary",) * len(grid), vmem_limit_bytes=VMEM_LIMIT,
                                             has_side_effects=True),
    )(*args, *carry.arrays)
    return list(res[:n_out]), list(res[n_out:])


def _gather_ops(shapes, whole):
    n = len(shapes)

    def rows(a, c):
        hr = shapes[a][1] // 2
        return pl.ds(pl.multiple_of(c * hr, 16), hr)

    def start(ins, outs, sems):
        ici_send, ici_recv = sems[0], sems[1]
        x, y, c = _place()
        k = 2 * x + y
        for a in range(n):
            for o, (fx, fy) in enumerate(CHIP_FLIPS):
                src = ins[a].at[k] if whole[a] else ins[a].at[k, rows(a, c)]
                dst = outs[a].at[k] if whole[a] else outs[a].at[k, rows(a, c)]
                pltpu.make_async_remote_copy(
                    src_ref=src, dst_ref=dst, send_sem=ici_send.at[3 * a + o], recv_sem=ici_recv.at[3 * a + o],
                    device_id=(_flip(x, fx), _flip(y, fy), c), device_id_type=MESH).start()

    def finish(ins, outs, sems):
        ici_send, ici_recv, d2d_send, d2d_recv = sems
        x, y, c = _place()
        k = 2 * x + y
        sibling = (x, y, 1 - c)

        def copy(ref, send, recv, a, o):
            return pltpu.make_async_remote_copy(src_ref=ref, dst_ref=ref, send_sem=send.at[3 * a + o],
                                                recv_sem=recv.at[3 * a + o], device_id=sibling, device_id_type=MESH)

        for a in range(n):
            for o, (fx, fy) in enumerate(CHIP_FLIPS):
                kk = 2 * _flip(x, fx) + _flip(y, fy)
                landed = outs[a].at[kk] if whole[a] else outs[a].at[kk, rows(a, c)]
                copy(landed, ici_send, ici_recv, a, o).wait_recv()
                if not whole[a]:
                    copy(landed, d2d_send, d2d_recv, a, o).start()
        for a in range(n):
            for o, (fx, fy) in enumerate(CHIP_FLIPS):
                kk = 2 * _flip(x, fx) + _flip(y, fy)
                mine = ins[a].at[k] if whole[a] else ins[a].at[k, rows(a, c)]
                copy(mine, ici_send, ici_recv, a, o).wait_send()
                if not whole[a]:
                    copy(outs[a].at[kk, rows(a, 1 - c)], d2d_send, d2d_recv, a, o).wait_recv()
                    copy(outs[a].at[kk, rows(a, c)], d2d_send, d2d_recv, a, o).wait_send()

    dma = pltpu.SemaphoreType.DMA
    return start, finish, [dma((3 * n,))] * 4


def gather_carry(bufs):
    start, finish, sems = _gather_ops([b.shape for b in bufs], [False] * len(bufs))
    return Carry(bufs, [jax.ShapeDtypeStruct(b.shape, b.dtype) for b in bufs], True, sems, start, finish)


def allgather_weights(shards, smalls):
    bufs = list(shards) + list(smalls)
    n = len(bufs)
    start, finish, sems = _gather_ops([b.shape for b in bufs], [False] * len(shards) + [True] * len(smalls))

    def body(*refs):
        start(refs[:n], refs[n:2 * n], refs[2 * n:])
        finish(refs[:n], refs[n:2 * n], refs[2 * n:])

    res = pl.pallas_call(
        body, name="allgather_weights", in_specs=[ANY] * n, out_specs=[ANY] * n,
        out_shape=[jax.ShapeDtypeStruct(b.shape, b.dtype) for b in bufs], scratch_shapes=sems,
        input_output_aliases={i: i for i in range(n)},
        compiler_params=pltpu.CompilerParams(has_side_effects=True),
    )(*bufs)
    return res[:len(shards)], res[len(shards):]


def rs_exchange(grads):
    n = len(grads)

    def body(*refs):
        ins, outs = refs[:n], refs[n:2 * n]
        send, recv = refs[2 * n:]
        x, y, c = _place()
        cps = []
        for a in range(n):
            cp = pltpu.make_async_remote_copy(
                src_ref=ins[a].at[:, 1 - c], dst_ref=outs[a], send_sem=send.at[a], recv_sem=recv.at[a],
                device_id=(x, y, 1 - c), device_id_type=MESH)
            cp.start()
            cps.append(cp)
        for cp in cps:
            cp.wait()

    dma = pltpu.SemaphoreType.DMA
    return pl.pallas_call(
        body, name="rs_exchange", in_specs=[ANY] * n, out_specs=[ANY] * n,
        out_shape=[jax.ShapeDtypeStruct((g.shape[0],) + g.shape[2:], g.dtype) for g in grads],
        scratch_shapes=[dma((n,)), dma((n,))],
        compiler_params=pltpu.CompilerParams(has_side_effects=True),
    )(*grads)


def rs_add(gs, sibs, core, out_dtype, name):
    n = len(gs)
    nk = gs[0].shape[0]

    def body(core_ref, *refs):
        del core_ref
        for a in range(n):
            refs[2 * n + a][0] = (refs[a][0, 0] + refs[n + a][0]).astype(out_dtype)

    halves = [g.shape[2:] for g in gs]
    return pl.pallas_call(
        body, name=name,
        grid_spec=pltpu.PrefetchScalarGridSpec(
            num_scalar_prefetch=1, grid=(nk,),
            in_specs=[pl.BlockSpec((1, 1) + h, lambda k, core_ref: (k, core_ref[0], 0, 0)) for h in halves]
            + [pl.BlockSpec((1,) + h, lambda k, core_ref: (k, 0, 0)) for h in halves],
            out_specs=[pl.BlockSpec((1,) + h, lambda k, core_ref: (k, 0, 0)) for h in halves]),
        out_shape=[jax.ShapeDtypeStruct((nk,) + h, out_dtype) for h in halves],
        compiler_params=_cparams(("parallel",)),
    )(core, *gs, *sibs)


def send_carry(parts):
    n = len(parts)

    def copies(ins, outs, sems):
        x, y, c = _place()
        for a in range(n):
            for o, (fx, fy) in enumerate(CHIP_FLIPS):
                kk = 2 * _flip(x, fx) + _flip(y, fy)
                yield pltpu.make_async_remote_copy(
                    src_ref=ins[a].at[kk], dst_ref=outs[a].at[o], send_sem=sems[0].at[3 * a + o],
                    recv_sem=sems[1].at[3 * a + o], device_id=(_flip(x, fx), _flip(y, fy), c), device_id_type=MESH)

    def start(ins, outs, sems):
        for cp in copies(ins, outs, sems):
            cp.start()

    def finish(ins, outs, sems):
        for cp in copies(ins, outs, sems):
            cp.wait()

    dma = pltpu.SemaphoreType.DMA
    return Carry(parts, [jax.ShapeDtypeStruct((3,) + p.shape[1:], p.dtype) for p in parts], False,
                 [dma((3 * n,)), dma((3 * n,))], start, finish)


def rs_sum(recvs, parts, where, name):
    n_layers = len(recvs)
    _, hr, cc = recvs[0].shape
    rb = _tile(hr, 256)

    def body(where_ref, *refs):
        del where_ref
        o_ref = refs[-1]
        for layer in range(n_layers):
            r_ref, p_ref = refs[layer], refs[n_layers + layer]
            o_ref[layer, 0] = ((p_ref[0].astype(F32) + r_ref[0].astype(F32)) + r_ref[1].astype(F32)) + r_ref[2].astype(F32)

    return pl.pallas_call(
        body, name=name,
        grid_spec=pltpu.PrefetchScalarGridSpec(
            num_scalar_prefetch=1, grid=(hr // rb,),
            in_specs=[pl.BlockSpec((3, rb, cc), lambda i, w_ref: (0, i, 0))] * n_layers
            + [pl.BlockSpec((1, rb, cc), lambda i, w_ref: (w_ref[0], i, 0))] * n_layers,
            out_specs=pl.BlockSpec((n_layers, 1, rb, cc), lambda i, w_ref: (0, w_ref[1], i, 0))),
        out_shape=jax.ShapeDtypeStruct((n_layers, 2, hr, cc), F32),
        compiler_params=_cparams(("parallel",)),
    )(where, *recvs, *parts)


def rs_share(fulls):
    n = len(fulls)

    def body(*refs):
        ins, outs = refs[:n], refs[n:2 * n]
        send, recv = refs[2 * n:]
        x, y, c = _place()
        cps = []
        for a in range(n):
            cp = pltpu.make_async_remote_copy(
                src_ref=ins[a].at[:, c], dst_ref=outs[a].at[:, c], send_sem=send.at[a], recv_sem=recv.at[a],
                device_id=(x, y, 1 - c), device_id_type=MESH)
            cp.start()
            cps.append(cp)
        for a in range(n):
            got = outs[a].at[:, 1 - c]
            pltpu.make_async_remote_copy(
                src_ref=got, dst_ref=got, send_sem=send.at[a], recv_sem=recv.at[a],
                device_id=(x, y, 1 - c), device_id_type=MESH).wait_recv()
        for cp in cps:
            cp.wait_send()

    dma = pltpu.SemaphoreType.DMA
    return pl.pallas_call(
        body, name="rs_share", in_specs=[ANY] * n, out_specs=[ANY] * n,
        out_shape=[jax.ShapeDtypeStruct(f.shape, f.dtype) for f in fulls],
        scratch_shapes=[dma((n,)), dma((n,))],
        input_output_aliases={i: i for i in range(n)},
        compiler_params=pltpu.CompilerParams(has_side_effects=True),
    )(*fulls)


def allreduce_small(v):
    r, w = v.shape

    def body(v_ref, o_ref, buf, send, recv, loc):
        x, y, c = _place()
        me = 4 * x + 2 * y + c
        mine = pltpu.make_async_copy(v_ref, buf.at[me], loc)
        mine.start()
        cps = []
        for o in range(1, N_DEV):
            fx, fy, fc = (o >> 2) & 1, (o >> 1) & 1, o & 1
            cp = pltpu.make_async_remote_copy(
                src_ref=v_ref, dst_ref=buf.at[me], send_sem=send.at[o - 1], recv_sem=recv.at[o - 1],
                device_id=(_flip(x, fx), _flip(y, fy), _flip(c, fc)), device_id_type=MESH)
            cp.start()
            cps.append(cp)
        for o in range(1, N_DEV):
            fx, fy, fc = (o >> 2) & 1, (o >> 1) & 1, o & 1
            peer = 4 * _flip(x, fx) + 2 * _flip(y, fy) + _flip(c, fc)
            pltpu.make_async_remote_copy(
                src_ref=v_ref, dst_ref=buf.at[peer], send_sem=send.at[o - 1], recv_sem=recv.at[o - 1],
                device_id=(x, y, c), device_id_type=MESH).wait_recv()
        for cp in cps:
            cp.wait_send()
        mine.wait()
        acc = buf[0]
        for d in range(1, N_DEV):
            acc = acc + buf[d]
        o_ref[...] = acc

    dma = pltpu.SemaphoreType.DMA
    vm = pl.BlockSpec(memory_space=pltpu.VMEM)
    return pl.pallas_call(
        body, name="allreduce_small", in_specs=[vm], out_specs=vm,
        out_shape=jax.ShapeDtypeStruct((r, w), F32),
        scratch_shapes=[pltpu.VMEM((N_DEV, r, w), F32), dma((N_DEV - 1,)), dma((N_DEV - 1,)), dma],
        compiler_params=pltpu.CompilerParams(has_side_effects=True, vmem_limit_bytes=VMEM_LIMIT),
    )(v)


def adamw(w, g, m, v, name):
    r, cc = w.shape
    rb = _tile(r, 256)

    def body(w_ref, g_ref, m_ref, v_ref, d_ref, nm_ref, nv_ref):
        gv = g_ref[...]
        nm = ADAM_B1 * m_ref[...] + (1.0 - ADAM_B1) * gv
        nv = ADAM_B2 * v_ref[...] + (1.0 - ADAM_B2) * (gv * gv)
        m_hat = nm / (1.0 - ADAM_B1 ** ADAM_STEP)
        v_hat = nv / (1.0 - ADAM_B2 ** ADAM_STEP)
        d_ref[...] = -ADAM_LR * (m_hat / (jnp.sqrt(v_hat) + ADAM_EPS) + ADAM_WD * w_ref[...])
        nm_ref[...] = nm
        nv_ref[...] = nv

    blk = pl.BlockSpec((rb, cc), lambda i: (i, 0))
    shp = jax.ShapeDtypeStruct((r, cc), F32)
    return pl.pallas_call(
        body, name=name, grid=(r // rb,), in_specs=[blk] * 4, out_specs=[blk] * 3, out_shape=[shp] * 3,
        compiler_params=_cparams(("parallel",)),
    )(w, g, m, v)


WEIGHTS = ['g_ffn1', 'w_ffn1_gate', 'w_ffn1_up', 'w_ffn1_down', 'g_mix', 'w_in_ab', 'conv_w', 'conv_b', 'ln_a_g',
           'ln_a_b', 'ln_v_g', 'ln_v_b', 'sp_w', 'sp_b', 'w_out_ab', 'w_qkv', 'w_o', 'g_ffn2', 'w_ffn2_gate',
           'w_ffn2_up', 'w_ffn2_down', 'g_final']
BIG = ['w_ffn1_gate', 'w_ffn1_up', 'w_ffn1_down', 'w_in_ab', 'w_out_ab', 'w_qkv', 'w_o', 'w_ffn2_gate', 'w_ffn2_up',
       'w_ffn2_down']
SMALL = ['g_ffn1', 'g_mix', 'g_ffn2', 'g_final', 'conv_b', 'ln_a_g', 'ln_a_b', 'ln_v_g', 'ln_v_b', 'sp_b', 'sp_w']
HIDDEN_MAJOR = ['w_ffn1_gate', 'w_ffn1_up', 'w_ffn2_gate', 'w_ffn2_up']


CARRY_WEIGHTS = {"ffn_gateup": 9.2e6, "ffn_down": 6.1e6, "mm_in": 5.9e6, "mm_out": 3.3e6}


def _use_order(depth):
    order = []
    for layer in range(depth):
        order += [('w_ffn1_gate', layer), ('w_ffn1_up', layer), ('w_ffn1_down', layer)]
        order += [('w_in_ab', layer // 2), ('w_out_ab', layer // 2)] if layer % 2 == 0 else [('w_qkv', layer // 2), ('w_o', layer // 2)]
        order += [('w_ffn2_gate', layer), ('w_ffn2_up', layer), ('w_ffn2_down', layer)]
    return order


def _rows(a):
    return a.reshape(-1, LANES)


def _pack(parts):
    v = jnp.concatenate([_rows(p) for p in parts], axis=0)
    pad = (-v.shape[0]) % 8
    return jnp.pad(v, ((0, pad), (0, 0)))


def _unpack(v, shapes):
    out, r = [], 0
    for s in shapes:
        n = 1
        for d in s:
            n *= d
        n //= LANES
        out.append(v[r:r + n].reshape(s))
        r += n
    return out


def kernel(x, g_ffn1, w_ffn1_gate, w_ffn1_up, w_ffn1_down, g_mix, w_in_ab, conv_w, conv_b, ln_a_g, ln_a_b, ln_v_g, ln_v_b, sp_w, sp_b, w_out_ab, w_qkv, w_o, g_ffn2, w_ffn2_gate, w_ffn2_up, w_ffn2_down, g_final, loss_target, m_g_ffn1, m_w_ffn1_gate, m_w_ffn1_up, m_w_ffn1_down, m_g_mix, m_w_in_ab, m_conv_w, m_conv_b, m_ln_a_g, m_ln_a_b, m_ln_v_g, m_ln_v_b, m_sp_w, m_sp_b, m_w_out_ab, m_w_qkv, m_w_o, m_g_ffn2, m_w_ffn2_gate, m_w_ffn2_up, m_w_ffn2_down, m_g_final, v_g_ffn1, v_w_ffn1_gate, v_w_ffn1_up, v_w_ffn1_down, v_g_mix, v_w_in_ab, v_conv_w, v_conv_b, v_ln_a_g, v_ln_a_b, v_ln_v_g, v_ln_v_b, v_sp_w, v_sp_b, v_w_out_ab, v_w_qkv, v_w_o, v_g_ffn2, v_w_ffn2_gate, v_w_ffn2_up, v_w_ffn2_down, v_g_final):
    p = dict(locals())
    for name in HIDDEN_MAJOR:
        for pre in ('', 'm_', 'v_'):
            p[pre + name] = jnp.swapaxes(p[pre + name], 1, 2)
    back = lambda name, a: jnp.swapaxes(a, 1, 2) if name in HIDDEN_MAJOR else a
    n_seq, seq, d = x.shape
    t = n_seq * seq
    depth = g_ffn1.shape[0]
    core = lax.axis_index("c")
    chip = 2 * lax.axis_index("x") + lax.axis_index("y")
    xf = x.reshape(t, d)
    target = loss_target.reshape(t, d)

    items = []
    for name in BIG:
        for layer in range(p[name].shape[0]):
            items.append((name, layer))
    chip1 = chip.reshape(1).astype(jnp.int32)
    placed = {}
    for name in BIG:
        for layer, buf in enumerate(place_shard(p[name], chip1, BF16, "place_shard")):
            placed[(name, layer)] = buf
    first = [('w_ffn1_gate', 0), ('w_ffn1_up', 0)]
    gathered, (conv_w4,) = allgather_weights([placed[it] for it in first],
                                             place_shard(conv_w, chip1, F32, "place_conv_w"))
    wt = dict(zip(first, gathered))
    waiting = [it for it in _use_order(depth) if it not in wt]

    def riders(name):
        room, take = CARRY_WEIGHTS[name], []
        for it in list(waiting):
            if placed[it].size <= room:
                room -= placed[it].size
                take.append(it)
                waiting.remove(it)
        return (take, gather_carry([placed[it] for it in take])) if take else (take, None)

    def landed(take, carried):
        wt.update(zip(take, carried))

    def weight(it):
        if it not in wt:
            waiting.remove(it)
            (wt[it],), _ = allgather_weights([placed[it]], [])
        return wt[it]

    c_mix = conv_w4.shape[2] * N_CHIPS
    conv_full = jnp.transpose(conv_w4, (1, 0, 2)).reshape(CONV_WIDTH, c_mix)
    vec = lambda a: a.reshape(1, -1)
    sp_bt = sp_b[0].T
    sp_wt = jnp.transpose(sp_w[0], (0, 2, 1))
    d_ff = w_ffn1_gate.shape[2]
    n_in = w_in_ab.shape[2]
    n_qkv = w_qkv.shape[2] // 3

    saved = []
    xc = xf
    h = rmsnorm_fwd(xc, vec(g_ffn1[0]), "norm_first")
    for layer in range(depth):
        s = {}
        for half, (gn, wn) in enumerate((('g_ffn1', 'w_ffn1'), ('g_ffn2', 'w_ffn2'))):
            if half == 1:
                s['x_mix'], s['h_mix'] = xc, h
                if layer % 2 == 0:
                    w_in = weight(('w_in_ab', layer // 2))
                    take, carry = riders("mm_in")
                    (z,), got = colmm(h, [w_in], n_in, BF16, "mm_in", carry)
                    landed(take, got)
                    cat, a1 = mix_fwd(z, conv_full, conv_b, ln_a_g, ln_a_b, vec(ln_v_g), vec(ln_v_b), sp_w[0], sp_bt, seq)
                    s.update(z=z, cat=cat, a1=a1)
                    w_out = weight(('w_out_ab', layer // 2))
                    take, carry = riders("mm_out")
                    (xc, h), got = rowmm(cat, w_out, xc, 1.0, "mm_out", carry, vec(g_ffn2[layer]))
                    landed(take, got)
                else:
                    (qkv,), _ = colmm(h, [weight(('w_qkv', layer // 2))], n_qkv, BF16, "mm_qkv")
                    o, tot, cnt = attn_fwd(qkv, n_seq, seq)
                    s.update(qkv=qkv, o=o, tot=tot, cnt=cnt)
                    (xc, h), _ = rowmm(o, weight(('w_o', layer // 2)), xc, 1.0, "mm_o", None, vec(g_ffn2[layer]))
            s['x' + wn] = xc
            w_gate, w_up = weight((wn + '_gate', layer)), weight((wn + '_up', layer))
            take, carry = riders("ffn_gateup")
            (silu, udsilu, act), got = colmm(h, [w_gate, w_up], d_ff, BF16, "ffn_gateup", carry, swiglu=True)
            landed(take, got)
            s.update({'h' + wn: h, 'swiglu' + wn: (silu, udsilu), 'act' + wn: act})
            w_down = weight((wn + '_down', layer))
            take, carry = riders("ffn_down")
            following = g_mix[layer] if half == 0 else (g_ffn1[layer + 1] if layer + 1 < depth else None)
            (xc, h), got = rowmm(act, w_down, xc, 0.5, "ffn_down", carry, None if following is None else vec(following))
            landed(take, got)
        saved.append(s)

    loss8, dx, dxb, dg_final = loss_head(xc, vec(g_final), target)
    loss = lax.psum(loss8[0, 0], ("x", "y", "c"))

    gw = {}
    gs = {}
    core1 = core.reshape(1).astype(jnp.int32)
    ready = []
    part, recv = {}, {}

    def leaving():
        its = list(ready)
        ready.clear()
        halves = lambda a: a.reshape(N_CHIPS, 2, a.shape[1] // 2, a.shape[2])
        theirs = rs_exchange([halves(gw[it][1]) for it in its])
        sums = rs_add([halves(gw[it][0]) for it in its], theirs, core1, REDUCE_DTYPE, "rs_add")
        part.update(zip(its, sums))
        return its, send_carry(sums)

    for layer in reversed(range(depth)):
        s = saved[layer]
        for half, (gn, wn) in reversed(list(enumerate((('g_ffn1', 'w_ffn1'), ('g_ffn2', 'w_ffn2'))))):
            wd = wt[(wn + '_down', layer)]
            dgate, dup = rowmm_t(dxb, wd, 0.5, BF16, "ffn_bwd_act", swiglu=s['swiglu' + wn])
            gw[(wn + '_down', layer)] = dw_row(s['act' + wn], dxb, 0.5, "ffn_dw_down")
            gw[(wn + '_gate', layer)], gw[(wn + '_up', layer)] = dw_col(s['h' + wn], [dgate, dup], N_CHIPS, d_ff,
                                                                        "ffn_dw_gateup", transposed=True)
            ready.extend([(wn + '_down', layer), (wn + '_gate', layer), (wn + '_up', layer)])
            its, carry = leaving()
            (dx, dxb, dg), got = colmm_t([dgate, dup], [wt[(wn + '_gate', layer)], wt[(wn + '_up', layer)]], d_ff,
                                         s['x' + wn], vec(p[gn][layer]), dx, "ffn_bwd_in", carry, transposed=True)
            recv.update(zip(its, got))
            gs[(gn, layer)] = dg
            if half == 1:
                if layer % 2 == 0:
                    i = layer // 2
                    w_out = wt[('w_out_ab', i)]
                    dcat = rowmm_t(dxb, w_out, 1.0, F32, "mm_out_t")
                    gw[('w_out_ab', i)] = dw_row(s['cat'], dxb, 1.0, "dw_out")
                    dz, da1, dcb, dlag, dlab, dlvg, dlvb, dspw, dspb = mix_bwd_point(
                        dcat, s['z'], s['a1'], ln_a_g, ln_a_b, vec(ln_v_g), vec(ln_v_b), sp_w[0], sp_wt, sp_bt, seq)
                    dz, dcw = mix_bwd_conv(dz, da1, s['z'], conv_full, seq)
                    gs.update({('conv_b', i): dcb, ('ln_a_g', i): dlag, ('ln_a_b', i): dlab, ('ln_v_g', i): dlvg,
                               ('ln_v_b', i): dlvb, ('sp_w', i): dspw, ('sp_b', i): dspb[:, :, 0], ('conv_w', i): dcw})
                    (gw[('w_in_ab', i)],) = dw_col(s['h_mix'], [dz], N_CHIPS, n_in, "dw_in")
                    ready.extend([('w_out_ab', i), ('w_in_ab', i)])
                    its, carry = leaving()
                    (dx, dxb, dg), got = colmm_t([dz], [wt[('w_in_ab', i)]], n_in, s['x_mix'], vec(g_mix[layer]), dx,
                                                 "mm_in_t", carry)
                    recv.update(zip(its, got))
                else:
                    i = layer // 2
                    w_o4 = wt[('w_o', i)]
                    do = rowmm_t(dxb, w_o4, 1.0, BF16, "mm_o_t")
                    gw[('w_o', i)] = dw_row(s['o'], dxb, 1.0, "dw_o")
                    dq, dk, dv = attn_bwd(s['qkv'], do, s['tot'], s['cnt'], n_seq, seq)
                    dqkv = jnp.concatenate([dq, dk, dv], axis=0)
                    (gw[('w_qkv', i)],) = dw_col(s['h_mix'], [dqkv], N_CHIPS, n_qkv, "dw_qkv")
                    ready.extend([('w_o', i), ('w_qkv', i)])
                    its, carry = leaving()
                    (dx, dxb, dg), got = colmm_t([dqkv], [wt[('w_qkv', i)]], n_qkv, s['x_mix'], vec(g_mix[layer]), dx,
                                                 "mm_qkv_t", carry)
                    recv.update(zip(its, got))
                gs[('g_mix', layer)] = dg
    grad_x = dx.reshape(x.shape)

    assert not ready and set(recv) == set(items)
    where = jnp.stack([chip, core]).astype(jnp.int32)
    fulls = []
    for name in BIG:
        its = [(name, layer) for layer in range(p[name].shape[0])]
        fulls.append(rs_sum([recv[it] for it in its], [part[it] for it in its], where, "rs_sum"))
    shared = rs_share(fulls)
    grads = {name: sh.reshape(p[name].shape) for name, sh in zip(BIG, shared)}

    stack = lambda name: jnp.concatenate([gs[(name, layer)].reshape((1,) + p[name].shape[1:]) for layer in range(p[name].shape[0])], axis=0)
    small_g = [stack(name) if name != 'g_final' else dg_final.reshape(p[name].shape) for name in SMALL]
    packed = _pack(small_g + [gs[('conv_w', 0)]])
    red = allreduce_small(packed)
    outs = _unpack(red, [p[name].shape for name in SMALL] + [(CONV_WIDTH, c_mix)])
    for name, g in zip(SMALL, outs[:-1]):
        grads[name] = g
    conv_g = outs[-1].reshape(CONV_WIDTH, N_CHIPS, c_mix // N_CHIPS)
    grads['conv_w'] = lax.dynamic_index_in_dim(conv_g, chip, axis=1, keepdims=False).reshape(conv_w.shape)

    delta, new_m, new_v = {}, {}, {}
    for name in BIG:
        shp = p[name].shape
        two = lambda a: a.reshape(shp[0] * shp[1], shp[2])
        dl, nm, nv = adamw(two(p[name]), two(grads[name]), two(p['m_' + name]), two(p['v_' + name]), "adamw")
        delta[name], new_m[name], new_v[name] = dl.reshape(shp), nm.reshape(shp), nv.reshape(shp)
    small_names = SMALL + ['conv_w']
    pk = lambda pre: _pack([p[pre + name] for name in small_names])
    dl, nm, nv = adamw(pk(''), _pack([grads[name] for name in small_names]), pk('m_'), pk('v_'), "adamw_small")
    shapes = [p[name].shape for name in small_names]
    for dst, val in ((delta, dl), (new_m, nm), (new_v, nv)):
        for name, a in zip(small_names, _unpack(val, shapes)):
            dst[name] = a

    return (loss, grad_x, *[back(n, d[n]) for d in (grads, delta, new_m, new_v) for n in WEIGHTS])
```

```python
import functools

import jax
import jax.numpy as jnp
from jax import lax
from jax.experimental import pallas as pl
from jax.experimental.pallas import tpu as pltpu

F32 = jnp.float32
BF16 = jnp.bfloat16
EPS = 1e-6
HEAD_DIM = 64
CONV_WIDTH = 31
CHUNK = 128
KBLK = 128
ATT_BLOCK = 256
ATT_LANES = 256
DW_TOKENS = 2048
CONV_ROWS = 64
MASKED = -1e30
STICK_GONE = -110.0
LANES = 128
HALO = 32
ADAM_LR, ADAM_B1, ADAM_B2, ADAM_EPS, ADAM_WD, ADAM_STEP = 0.001, 0.9, 0.999, 1e-08, 0.01, 10
VMEM_LIMIT = 56 * 1024 * 1024
MESH = pl.DeviceIdType.MESH
N_CHIPS = 4
N_DEV = 8
REDUCE_DTYPE = BF16


def _cparams(sem):
    return pltpu.CompilerParams(dimension_semantics=sem, vmem_limit_bytes=VMEM_LIMIT)


def _nt(a, b):
    return lax.dot_general(a, b, (((1,), (1,)), ((), ())), preferred_element_type=F32)


def _tn(a, b):
    return lax.dot_general(a, b, (((0,), (0,)), ((), ())), preferred_element_type=F32)


def _nn(a, b):
    return jnp.dot(a, b, preferred_element_type=F32)


def _sigmoid(x):
    return 0.5 * jnp.tanh(0.5 * x) + 0.5


def _tile(t, want):
    if t <= want:
        return t
    for cand in range(want - want % 8, 7, -8):
        if t % cand == 0:
            return cand
    raise ValueError((t, want))


def rmsnorm_fwd(x, g, name):
    t, d = x.shape
    tm = _tile(t, 512)

    def body(x_ref, g_ref, h_ref):
        xv = x_ref[...]
        r = lax.rsqrt(jnp.mean(xv * xv, axis=-1, keepdims=True) + EPS)
        h_ref[...] = (xv * r * g_ref[...]).astype(BF16)

    return pl.pallas_call(
        body, name=name, grid=(t // tm,),
        in_specs=[pl.BlockSpec((tm, d), lambda i: (i, 0)), pl.BlockSpec((1, d), lambda i: (0, 0))],
        out_specs=pl.BlockSpec((tm, d), lambda i: (i, 0)),
        out_shape=jax.ShapeDtypeStruct((t, d), BF16),
        compiler_params=_cparams(("parallel",)),
    )(x, g)


def colmm(h, ws, nu, out_dtype, name, carry=None, swiglu=False):
    t, k = h.shape
    j, nj = (ws[0].shape[0], ws[0].shape[1]) if swiglu else (ws[0].shape[0], ws[0].shape[2])
    per = nj // nu
    units = j * per
    tm = _tile(t, 1024)
    nw = len(ws)
    n_out = 3 if swiglu else nw

    def body(*refs):
        h_ref = refs[0]
        hv = h_ref[...]
        if swiglu:
            silu_ref, udsilu_ref, act_ref = refs[1 + nw:]
            gv = _nt(hv, refs[1][0])
            uv = _nt(hv, refs[2][0])
            s = _sigmoid(gv)
            silu = gv * s
            silu_ref[0] = silu.astype(out_dtype)
            udsilu_ref[0] = (uv * (s + silu * (1.0 - s))).astype(out_dtype)
            act_ref[0] = (silu * uv).astype(out_dtype)
            return
        for n in range(nw):
            for s in range(j):
                res = _nn(hv, refs[1 + n][s]).astype(out_dtype)
                for u in range(per):
                    refs[1 + nw + n][s * per + u] = res[:, u * nu:(u + 1) * nu]

    out_shape = [jax.ShapeDtypeStruct((units, t, nu), out_dtype)] * n_out
    if swiglu:
        assert nw == 2 and per == 1
        return _call(
            body, name=name, grid=(j, t // tm),
            in_specs=[pl.BlockSpec((tm, k), lambda s, i: (i, 0))] + [pl.BlockSpec((1, nj, k), lambda s, i: (s, 0, 0))] * nw,
            out_specs=[pl.BlockSpec((1, tm, nu), lambda s, i: (s, i, 0))] * n_out, out_shape=out_shape,
            args=[h, *ws], sem=("parallel", "parallel"), carry=carry)
    return _call(
        body, name=name, grid=(t // tm,),
        in_specs=[pl.BlockSpec((tm, k), lambda i: (i, 0))] + [pl.BlockSpec((j, k, nj), lambda i: (0, 0, 0))] * nw,
        out_specs=[pl.BlockSpec((units, tm, nu), lambda i: (0, i, 0))] * n_out, out_shape=out_shape,
        args=[h, *ws], sem=("parallel",), carry=carry)


def rowmm(a, w, resid, scale, name, carry=None, norm_g=None):
    u_n, t, ku = a.shape
    n = w.shape[2]
    tm = _tile(t, 512)

    def body(a_ref, w_ref, r_ref, *rest):
        acc = jnp.zeros((tm, n), F32)
        for u in range(u_n):
            acc = acc + _nn(a_ref[u], w_ref[u])
        out = r_ref[...] + scale * acc
        if norm_g is None:
            (o_ref,) = rest
        else:
            g_ref, o_ref, h_ref = rest
            r = lax.rsqrt(jnp.mean(out * out, axis=-1, keepdims=True) + EPS)
            h_ref[...] = (out * r * g_ref[...]).astype(BF16)
        o_ref[...] = out

    row = pl.BlockSpec((tm, n), lambda i: (i, 0))
    normed = norm_g is not None
    outs, carried = _call(
        body, name=name, grid=(t // tm,),
        in_specs=[pl.BlockSpec((u_n, tm, ku), lambda i: (0, i, 0)), pl.BlockSpec((u_n, ku, n), lambda i: (0, 0, 0)),
                  row] + [pl.BlockSpec((1, n), lambda i: (0, 0))] * normed,
        out_specs=[row] + [row] * normed,
        out_shape=[jax.ShapeDtypeStruct((t, n), F32)] + [jax.ShapeDtypeStruct((t, n), BF16)] * normed,
        args=[a, w, resid] + [norm_g] * normed, sem=("parallel",), carry=carry)
    return (outs[0], outs[1] if normed else None), carried


def rowmm_t(dyb, w, scale, out_dtype, name, swiglu=None):
    t, n = dyb.shape
    u_n, ku, _ = w.shape
    tm = _tile(t, 512)

    if swiglu is None:
        def body(dy_ref, w_ref, o_ref):
            dy = dy_ref[...]
            for u in range(u_n):
                o_ref[u] = (scale * _nt(dy, w_ref[u])).astype(out_dtype)

        return pl.pallas_call(
            body, name=name, grid=(t // tm,),
            in_specs=[pl.BlockSpec((tm, n), lambda i: (i, 0)), pl.BlockSpec((u_n, ku, n), lambda i: (0, 0, 0))],
            out_specs=pl.BlockSpec((u_n, tm, ku), lambda i: (0, i, 0)),
            out_shape=jax.ShapeDtypeStruct((u_n, t, ku), out_dtype),
            compiler_params=_cparams(("parallel",)),
        )(dyb, w)

    def body(dy_ref, w_ref, silu_ref, udsilu_ref, dg_ref, du_ref):
        dy = dy_ref[...]
        for u in range(u_n):
            dact = scale * _nt(dy, w_ref[u])
            dg_ref[u] = (dact * udsilu_ref[u].astype(F32)).astype(BF16)
            du_ref[u] = (dact * silu_ref[u].astype(F32)).astype(BF16)

    blk = pl.BlockSpec((u_n, tm, ku), lambda i: (0, i, 0))
    return pl.pallas_call(
        body, name=name, grid=(t // tm,),
        in_specs=[pl.BlockSpec((tm, n), lambda i: (i, 0)), pl.BlockSpec((u_n, ku, n), lambda i: (0, 0, 0)), blk, blk],
        out_specs=[blk] * 2, out_shape=[jax.ShapeDtypeStruct((u_n, t, ku), BF16)] * 2,
        compiler_params=_cparams(("parallel",)),
    )(dyb, w, *swiglu)


def colmm_t(dzs, ws, nu, x, g, dy_in, name, carry=None, transposed=False):
    t, k = x.shape
    j, nj = (ws[0].shape[0], ws[0].shape[1]) if transposed else (ws[0].shape[0], ws[0].shape[2])
    per = nj // nu
    units = j * per
    nw = len(ws)
    tm = _tile(t, 512)
    assert not transposed or per == 1

    def body(*refs):
        dz_refs = refs[:nw]
        w_refs = refs[nw:2 * nw]
        x_ref, g_ref, dy_ref, dx_ref, dxb_ref, dg_ref = refs[2 * nw:]
        i = pl.program_id(0)
        dh = jnp.zeros((tm, k), F32)
        for n in range(nw):
            for u in range(units):
                if transposed:
                    dh = dh + _nn(dz_refs[n][u], w_refs[n][u])
                else:
                    wv = w_refs[n][u // per, :, (u % per) * nu:(u % per + 1) * nu]
                    dh = dh + _nt(dz_refs[n][u], wv)
        xv = x_ref[...]
        gv = g_ref[...]
        r = lax.rsqrt(jnp.mean(xv * xv, axis=-1, keepdims=True) + EPS)
        uu = dh * gv
        dx = dy_ref[...] + r * uu - xv * (r * r * r * jnp.mean(uu * xv, axis=-1, keepdims=True))
        dx_ref[...] = dx
        dxb_ref[...] = dx.astype(BF16)
        part = jnp.sum(dh * (xv * r), axis=0, keepdims=True)

        @pl.when(i == 0)
        def _():
            dg_ref[...] = part

        @pl.when(i > 0)
        def _():
            dg_ref[...] += part

    dz_spec = pl.BlockSpec((units, tm, nu), lambda i: (0, i, 0))
    w_spec = pl.BlockSpec((j, nj, k) if transposed else (j, k, nj), lambda i: (0, 0, 0))
    row = pl.BlockSpec((tm, k), lambda i: (i, 0))
    vec = pl.BlockSpec((1, k), lambda i: (0, 0))
    return _call(
        body, name=name, grid=(t // tm,),
        in_specs=[dz_spec] * nw + [w_spec] * nw + [row, vec, row],
        out_specs=[row, row, vec],
        out_shape=[jax.ShapeDtypeStruct((t, k), F32), jax.ShapeDtypeStruct((t, k), BF16),
                   jax.ShapeDtypeStruct((1, k), F32)],
        args=[*dzs, *ws, x, g, dy_in], sem=("arbitrary",), carry=carry)


def dw_col(h, dzs, j, nu, name, transposed=False):
    t, k = h.shape
    units = dzs[0].shape[0]
    per = units // j
    nw = len(dzs)
    tt = _tile(t, DW_TOKENS)
    assert not transposed or per == 1

    def body(*refs):
        h_ref = refs[0]
        s = pl.program_id(1)
        hv = h_ref[...]
        outs, copies = refs[1 + nw:1 + 2 * nw], refs[1 + 2 * nw:]

        @pl.when(s == 0)
        def _():
            for o_ref in outs:
                o_ref[...] = jnp.zeros_like(o_ref)

        for n in range(nw):
            if transposed:
                outs[n][0] += _tn(refs[1 + n][0], hv)
                continue
            for u in range(per):
                outs[n][0, :, u * nu:(u + 1) * nu] += _tn(hv, refs[1 + n][u])

        @pl.when(s == pl.num_programs(1) - 1)
        def _():
            for o_ref, c_ref in zip(outs, copies):
                c_ref[...] = o_ref[...].astype(REDUCE_DTYPE)

    shard = (nu, k) if transposed else (k, per * nu)
    o_spec = pl.BlockSpec((1,) + shard, lambda u, s: (u, 0, 0))
    res = pl.pallas_call(
        body, name=name, grid=(j, t // tt),
        in_specs=[pl.BlockSpec((tt, k), lambda u, s: (s, 0))] + [pl.BlockSpec((per, tt, nu), lambda u, s: (u, s, 0))] * nw,
        out_specs=[o_spec] * (2 * nw),
        out_shape=[jax.ShapeDtypeStruct((j,) + shard, F32)] * nw + [jax.ShapeDtypeStruct((j,) + shard, REDUCE_DTYPE)] * nw,
        compiler_params=_cparams(("parallel", "arbitrary")),
    )(h, *dzs)
    return list(zip(res[:nw], res[nw:]))


def dw_row(a, dyb, scale, name):
    u_n, t, ku = a.shape
    n = dyb.shape[1]
    tt = _tile(t, DW_TOKENS)

    def body(a_ref, dy_ref, o_ref, c_ref):
        @pl.when(pl.program_id(1) == 0)
        def _():
            o_ref[...] = jnp.zeros_like(o_ref)

        o_ref[0] += scale * _tn(a_ref[0], dy_ref[...])

        @pl.when(pl.program_id(1) == pl.num_programs(1) - 1)
        def _():
            c_ref[...] = o_ref[...].astype(REDUCE_DTYPE)

    o_spec = pl.BlockSpec((1, ku, n), lambda u, s: (u, 0, 0))
    return tuple(pl.pallas_call(
        body, name=name, grid=(u_n, t // tt),
        in_specs=[pl.BlockSpec((1, tt, ku), lambda u, s: (u, s, 0)), pl.BlockSpec((tt, n), lambda u, s: (s, 0))],
        out_specs=[o_spec, o_spec],
        out_shape=[jax.ShapeDtypeStruct((u_n, ku, n), F32), jax.ShapeDtypeStruct((u_n, ku, n), REDUCE_DTYPE)],
        compiler_params=_cparams(("parallel", "arbitrary")),
    )(a, dyb))


def loss_head(x, g, target):
    t, d = x.shape
    tm = _tile(t, 256)

    def body(x_ref, g_ref, t_ref, loss_ref, dx_ref, dxb_ref, dg_ref):
        i = pl.program_id(0)
        xv = x_ref[...]
        gv = g_ref[...]
        r = lax.rsqrt(jnp.mean(xv * xv, axis=-1, keepdims=True) + EPS)
        xh = xv * r
        err = xh * gv - t_ref[...]
        dy = err * (1.0 / d)
        uu = dy * gv
        dx = r * uu - xv * (r * r * r * jnp.mean(uu * xv, axis=-1, keepdims=True))
        dx_ref[...] = dx
        dxb_ref[...] = dx.astype(BF16)
        dg_part = jnp.sum(dy * xh, axis=0, keepdims=True)
        row = jnp.sum(err * err, axis=-1, keepdims=True) * (0.5 / d)
        l_part = jnp.zeros((8, LANES), F32) + jnp.sum(row, axis=0, keepdims=True)

        @pl.when(i == 0)
        def _():
            dg_ref[...] = dg_part
            loss_ref[...] = l_part

        @pl.when(i > 0)
        def _():
            dg_ref[...] += dg_part
            loss_ref[...] += l_part

    row = pl.BlockSpec((tm, d), lambda i: (i, 0))
    vec = pl.BlockSpec((1, d), lambda i: (0, 0))
    return pl.pallas_call(
        body, name="loss_head", grid=(t // tm,),
        in_specs=[row, vec, row],
        out_specs=[pl.BlockSpec((8, LANES), lambda i: (0, 0)), row, row, vec],
        out_shape=[jax.ShapeDtypeStruct((8, LANES), F32), jax.ShapeDtypeStruct((t, d), F32),
                   jax.ShapeDtypeStruct((t, d), BF16), jax.ShapeDtypeStruct((1, d), F32)],
        compiler_params=_cparams(("arbitrary",)),
    )(x, g, target)


def _split(v):
    hi = v.astype(BF16)
    lo = (v - hi.astype(F32)).astype(BF16)
    return hi, lo


def _keysums(v, m_ext):
    hi, lo = _split(v)
    outs = []
    for j in range(v.shape[1] // KBLK):
        sl = slice(j * KBLK, (j + 1) * KBLK)
        cs = _nn(jnp.concatenate([hi[:, sl], lo[:, sl]], axis=1), m_ext)
        outs.append((cs[:, :KBLK], cs[:, KBLK:]))
    return outs


def _softplus_parts(z):
    sp = jnp.maximum(z, 0.0) + jnp.log(1.0 + jnp.exp(-jnp.abs(z)))
    return sp, z - sp


def _sum_matrices():
    r = lax.broadcasted_iota(jnp.int32, (2 * KBLK, 2 * KBLK), 0) % KBLK
    c = lax.broadcasted_iota(jnp.int32, (2 * KBLK, 2 * KBLK), 1)
    suffix = jnp.where((r > c) | (c >= KBLK), 1.0, 0.0).astype(BF16)
    prefix = jnp.where((r <= c) | (c >= KBLK), 1.0, 0.0).astype(BF16)
    return suffix, prefix


def _att_geometry(qkv, seq):
    upp = qkv.shape[0] // 3
    bq = min(ATT_BLOCK, seq)
    per_unit = (2 * LANES) // ATT_LANES
    return upp, bq, seq // bq, bq // KBLK, per_unit, upp * per_unit, ATT_LANES // HEAD_DIM


def _head_lanes(rows, heads):
    lane = lax.broadcasted_iota(jnp.int32, (rows, ATT_LANES), 1)
    return [(lane >= HEAD_DIM * h) & (lane < HEAD_DIM * (h + 1)) for h in range(heads)]


def attn_fwd(qkv, n_seq, seq):
    t = qkv.shape[1]
    upp, bq, nq, nsub, per_unit, groups, heads = _att_geometry(qkv, seq)
    suffix_m, _ = _sum_matrices()

    def body(q_ref, k_ref, v_ref, m_ref, o_ref, tot_ref, cnt_ref):
        qi = pl.program_id(2)
        step_id = (pl.program_id(0) * groups + pl.program_id(1)) * nq + qi
        in_head = _head_lanes(bq, heads)
        only = lambda v, h: jnp.where(in_head[h], v, jnp.zeros_like(v))
        q_all = q_ref[0] * jnp.asarray(HEAD_DIM ** -0.5, BF16)
        qs = [only(q_all, h) for h in range(heads)]
        m_ext = m_ref[...]
        row = lax.broadcasted_iota(jnp.int32, (bq, bq), 0)
        col = lax.broadcasted_iota(jnp.int32, (bq, bq), 1)
        diag_mask = col < row

        def block(kj, carry, mask):
            off = pl.multiple_of(kj * bq, bq)
            k_all = k_ref[0, pl.ds(off, bq), :]
            v_all = v_ref[0, pl.ds(off, bq), :]
            rems, acc = carry
            out = []
            for h in range(heads):
                rem = rems[h]
                z = _nt(qs[h], k_all)
                if mask is not None:
                    z = jnp.where(mask, z, MASKED)
                sp, ls = _softplus_parts(z)
                sums = _keysums(-sp, m_ext)
                parts = [None] * nsub
                for j in reversed(range(nsub)):
                    suf, total = sums[j]
                    parts[j] = jnp.exp(ls[:, j * KBLK:(j + 1) * KBLK] + suf + rem)
                    rem = rem + total
                a = jnp.concatenate(parts, axis=1)
                acc = acc + _nn(a.astype(BF16), only(v_all, h))
                out.append(rem)
            return tuple(out), acc

        def most_left(c):
            return functools.reduce(jnp.maximum, [jnp.max(r) for r in c[0]])

        def more(s):
            return (s[0] < qi) & (s[1] > STICK_GONE)

        def step(s):
            c = block(qi - 1 - s[0], s[2], None)
            return s[0] + 1, most_left(c), c

        zero = jnp.zeros((bq, LANES), F32)
        carry = block(qi, ((zero,) * heads, jnp.zeros((bq, ATT_LANES), F32)), diag_mask)
        n_left, _, (rems, acc) = lax.while_loop(more, step, (jnp.int32(0), most_left(carry), carry))
        o_ref[0] = acc.astype(BF16)
        first = lax.broadcasted_iota(jnp.int32, (bq, LANES), 1) < HEAD_DIM
        tot_ref[...] = jnp.concatenate([jnp.where(first, rems[h], rems[h + 1]) for h in range(0, heads, 2)], axis=1)
        cnt_ref[step_id] = n_left.astype(F32)

    qblk = lambda b, g, i: (g // per_unit, b * nq + i, g % per_unit)
    return pl.pallas_call(
        body, name="attn_fwd", grid=(n_seq, groups, nq),
        in_specs=[pl.BlockSpec((1, bq, ATT_LANES), qblk),
                  pl.BlockSpec((1, seq, ATT_LANES), lambda b, g, i: (upp + g // per_unit, b, g % per_unit)),
                  pl.BlockSpec((1, seq, ATT_LANES), lambda b, g, i: (2 * upp + g // per_unit, b, g % per_unit)),
                  pl.BlockSpec((2 * KBLK, 2 * KBLK), lambda b, g, i: (0, 0))],
        out_specs=[pl.BlockSpec((1, bq, ATT_LANES), qblk),
                   pl.BlockSpec((bq, ATT_LANES), lambda b, g, i: (b * nq + i, g)),
                   pl.BlockSpec(memory_space=pltpu.SMEM)],
        out_shape=[jax.ShapeDtypeStruct((upp, t, 2 * LANES), BF16), jax.ShapeDtypeStruct((t, upp * 2 * LANES), F32),
                   jax.ShapeDtypeStruct((n_seq * groups * nq,), F32)],
        compiler_params=_cparams(("arbitrary", "arbitrary", "arbitrary")),
    )(qkv, qkv, qkv, suffix_m)


def attn_bwd(qkv, do, tot, cnt, n_seq, seq):
    t = qkv.shape[1]
    upp, bq, nq, nsub, per_unit, groups, heads = _att_geometry(qkv, seq)
    _, prefix_m = _sum_matrices()
    scale = HEAD_DIM ** -0.5

    def body(q_ref, k_ref, v_ref, do_ref, tot_ref, m_ref, cnt_ref, dq_ref, dk_ref, dv_ref, dk_acc, dv_acc):
        qi = pl.program_id(2)
        step_id = (pl.program_id(0) * groups + pl.program_id(1)) * nq + qi
        n_left = jnp.clip(cnt_ref[step_id].astype(jnp.int32), 0, qi)
        in_head = _head_lanes(bq, heads)
        only = lambda v, h: jnp.where(in_head[h], v, jnp.zeros_like(v))
        q_all = q_ref[0] * jnp.asarray(scale, BF16)
        do_all = do_ref[0]
        qs = [only(q_all, h) for h in range(heads)]
        dos = [only(do_all, h) for h in range(heads)]
        first = lax.broadcasted_iota(jnp.int32, (bq, LANES), 1) < HEAD_DIM
        tots = []
        for h in range(0, heads, 2):
            both = tot_ref[:, h // 2 * LANES:(h // 2 + 1) * LANES]
            swapped = pltpu.roll(both, HEAD_DIM, 1)
            tots += [jnp.where(first, both, swapped), jnp.where(first, swapped, both)]
        m_ext = m_ref[...]
        row = lax.broadcasted_iota(jnp.int32, (bq, bq), 0)
        col = lax.broadcasted_iota(jnp.int32, (bq, bq), 1)
        diag_mask = col < row

        @pl.when(qi == 0)
        def _():
            dk_acc[...] = jnp.zeros_like(dk_acc)
            dv_acc[...] = jnp.zeros_like(dv_acc)

        def block(kj, carry, mask):
            off = pl.multiple_of(kj * bq, bq)
            k_all = k_ref[0, pl.ds(off, bq), :]
            v_all = v_ref[0, pl.ds(off, bq), :]
            pres, gpres, dq = carry
            dk_part = jnp.zeros((bq, ATT_LANES), F32)
            dv_part = jnp.zeros((bq, ATT_LANES), F32)
            pres_out, gpres_out = [], []
            for h in range(heads):
                pre, gpre = pres[h], gpres[h]
                z = _nt(qs[h], k_all)
                if mask is not None:
                    z = jnp.where(mask, z, MASKED)
                sp, ls = _softplus_parts(z)
                sums = _keysums(-sp, m_ext)
                parts = []
                for j in range(nsub):
                    pin, ptot = sums[j]
                    parts.append(jnp.exp(ls[:, j * KBLK:(j + 1) * KBLK] + (tots[h] - (pre + pin))))
                    pre = pre + ptot
                a = jnp.concatenate(parts, axis=1)
                g = a * _nt(dos[h], v_all)
                gsums = _keysums(g, m_ext)
                parts = []
                for j in range(nsub):
                    gin, gtot = gsums[j]
                    parts.append(gpre + gin)
                    gpre = gpre + gtot
                dz = g - jnp.exp(ls) * jnp.concatenate(parts, axis=1)
                dzb = dz.astype(BF16)
                dq = dq + _nn(dzb, only(k_all, h))
                dk_part = dk_part + _tn(dzb, qs[h])
                dv_part = dv_part + _tn(a.astype(BF16), dos[h])
                pres_out.append(pre)
                gpres_out.append(gpre)
            dk_acc[pl.ds(off, bq), :] += dk_part
            dv_acc[pl.ds(off, bq), :] += dv_part
            return tuple(pres_out), tuple(gpres_out), dq

        zero = jnp.zeros((bq, LANES), F32)
        carry = ((zero,) * heads, (zero,) * heads, jnp.zeros((bq, ATT_LANES), F32))
        carry = lax.fori_loop(qi - n_left, qi, lambda kj, c: block(kj, c, None), carry)
        carry = block(qi, carry, diag_mask)
        dq_ref[0] = (carry[2] * scale).astype(BF16)

        @pl.when(qi == nq - 1)
        def _():
            dk_ref[0] = dk_acc[...].astype(BF16)
            dv_ref[0] = dv_acc[...].astype(BF16)

    qblk = lambda b, g, i: (g // per_unit, b * nq + i, g % per_unit)
    kv_out = pl.BlockSpec((1, seq, ATT_LANES), lambda b, g, i: (g // per_unit, b, g % per_unit))
    shp = jax.ShapeDtypeStruct((upp, t, 2 * LANES), BF16)
    return pl.pallas_call(
        body, name="attn_bwd", grid=(n_seq, groups, nq),
        in_specs=[pl.BlockSpec((1, bq, ATT_LANES), qblk),
                  pl.BlockSpec((1, seq, ATT_LANES), lambda b, g, i: (upp + g // per_unit, b, g % per_unit)),
                  pl.BlockSpec((1, seq, ATT_LANES), lambda b, g, i: (2 * upp + g // per_unit, b, g % per_unit)),
                  pl.BlockSpec((1, bq, ATT_LANES), qblk),
                  pl.BlockSpec((bq, ATT_LANES), lambda b, g, i: (b * nq + i, g)),
                  pl.BlockSpec((2 * KBLK, 2 * KBLK), lambda b, g, i: (0, 0)),
                  pl.BlockSpec(memory_space=pltpu.SMEM)],
        out_specs=[pl.BlockSpec((1, bq, ATT_LANES), qblk), kv_out, kv_out],
        out_shape=[shp, shp, shp],
        scratch_shapes=[pltpu.VMEM((seq, ATT_LANES), F32), pltpu.VMEM((seq, ATT_LANES), F32)],
        compiler_params=_cparams(("parallel", "parallel", "arbitrary")),
    )(qkv, qkv, qkv, do, tot, prefix_m, cnt)


def _ln_stats(v):
    mu = jnp.mean(v, axis=-1, keepdims=True)
    vc = v - mu
    rstd = lax.rsqrt(jnp.mean(vc * vc, axis=-1, keepdims=True) + EPS)
    return vc * rstd, rstd


def _glu_into(a0_ref, av_ref, ag_ref, hv_ref, hg_ref, first):
    hv = hv_ref[0].astype(F32)
    hg = hg_ref[0].astype(F32)
    a0_ref[0:HALO, :] = jnp.where(first, 0.0, hv * _sigmoid(hg))
    av = av_ref[0].astype(F32)
    ag = ag_ref[0].astype(F32)
    a0_ref[HALO:, :] = av * _sigmoid(ag)


def _shifted_taps(ref, shifted_ref, tm, first):
    taps = []
    for b in range(8):
        offs = [o for o in range(first, first + CONV_WIDTH) if o % 8 == b]
        n_rows = max(offs) - b + tm
        shifted_ref[b, 0:n_rows, :] = ref[pl.ds(b, n_rows), :]
        taps += [(b, o - b, o - first) for o in offs]
    return taps


def _tril_mask():
    r = lax.broadcasted_iota(jnp.int32, (CHUNK, CHUNK), 0)
    c = lax.broadcasted_iota(jnp.int32, (CHUNK, CHUNK), 1)
    return c <= r


def mix_fwd(z, conv_w, conv_b, ln_a_g, ln_a_b, ln_v_g, ln_v_b, sp_w, sp_bt, seq):
    _, t, c = z.shape
    tm = _tile(seq, 512)
    tiles_per_seq = seq // tm
    groups = c // LANES
    hb = tm // HALO

    def body(av_ref, ag_ref, u_ref, v_ref, hv_ref, hg_ref, cw_ref, cb_ref, lag_ref, lab_ref, lvg_ref, lvb_ref,
             spw_ref, spb_ref, cat_ref, a1_ref, a0_ref, sh_ref):
        i = pl.program_id(0)
        _glu_into(a0_ref, av_ref, ag_ref, hv_ref, hg_ref, i % tiles_per_seq == 0)
        acc = jnp.zeros((tm, c), F32) + cb_ref[...]
        for b, ro, k in _shifted_taps(a0_ref, sh_ref, tm, HALO - (CONV_WIDTH - 1)):
            acc = acc + cw_ref[k:k + 1, :] * sh_ref[b, pl.ds(ro, tm), :]
        a1_ref[...] = acc
        xh, _ = _ln_stats(acc)
        a2 = xh * lag_ref[...] + lab_ref[...]
        a3 = (a2 * _sigmoid(a2)).astype(BF16)
        half = c // 2
        cat_ref[0] = a3[:, :half]
        cat_ref[1] = a3[:, half:]
        tril = _tril_mask()
        for g in range(groups):
            sl = slice(g * LANES, (g + 1) * LANES)
            xh, _ = _ln_stats(v_ref[0][:, sl].astype(F32))
            vn = (xh * lvg_ref[:, sl] + lvb_ref[:, sl]).astype(BF16)
            w = jnp.where(tril, spw_ref[g], 0.0).astype(BF16)
            bias = spb_ref[:, g:g + 1]
            for ch in range(tm // CHUNK):
                rows = slice(ch * CHUNK, (ch + 1) * CHUNK)
                vs = _nn(w, vn[rows]) + bias
                bo = (u_ref[0][rows, sl].astype(F32) * vs).astype(BF16)
                cat_ref[2 + (g * LANES) // half, rows, (g * LANES) % half:(g * LANES) % half + LANES] = bo

    unit = lambda u: pl.BlockSpec((1, tm, c), lambda i: (u, i, 0))
    halo = lambda u: pl.BlockSpec((1, HALO, c), lambda i: (u, jnp.maximum(i * hb - 1, 0), 0))
    vec = pl.BlockSpec((1, c), lambda i: (0, 0))
    return pl.pallas_call(
        body, name="mix_fwd", grid=(t // tm,),
        in_specs=[unit(0), unit(1), unit(2), unit(3), halo(0), halo(1),
                  pl.BlockSpec((CONV_WIDTH, c), lambda i: (0, 0)), vec, vec, vec, vec, vec,
                  pl.BlockSpec((groups, CHUNK, CHUNK), lambda i: (0, 0, 0)),
                  pl.BlockSpec((CHUNK, groups), lambda i: (0, 0))],
        out_specs=[pl.BlockSpec((4, tm, c // 2), lambda i: (0, i, 0)), pl.BlockSpec((tm, c), lambda i: (i, 0))],
        out_shape=[jax.ShapeDtypeStruct((4, t, c // 2), BF16), jax.ShapeDtypeStruct((t, c), F32)],
        scratch_shapes=[pltpu.VMEM((HALO + tm, c), F32), pltpu.VMEM((8, HALO + tm, c), F32)],
        compiler_params=_cparams(("parallel",)),
    )(z, z, z, z, z, z, conv_w, conv_b, ln_a_g, ln_a_b, ln_v_g, ln_v_b, sp_w, sp_bt)


def mix_bwd_point(dcat, z, a1, ln_a_g, ln_a_b, ln_v_g, ln_v_b, sp_w, sp_wt, sp_bt, seq):
    _, t, c = z.shape
    tm = _tile(seq, 512)
    groups = c // LANES
    half = c // 2

    def body(dc_ref, u_ref, v_ref, a1_ref, lag_ref, lab_ref, lvg_ref, lvb_ref, spw_ref, spwt_ref, spb_ref,
             dz_ref, da1_ref, dcb_ref, dlag_ref, dlab_ref, dlvg_ref, dlvb_ref, dspw_ref, dspb_ref):
        i = pl.program_id(0)
        last = pl.num_programs(0) - 1

        @pl.when(i == 0)
        def _():
            for r in (dcb_ref, dlag_ref, dlab_ref, dlvg_ref, dlvb_ref, dspw_ref, dspb_ref):
                r[...] = jnp.zeros_like(r)

        da3 = jnp.concatenate([dc_ref[0], dc_ref[1]], axis=-1)
        xh, rstd = _ln_stats(a1_ref[...])
        a2 = xh * lag_ref[...] + lab_ref[...]
        s = _sigmoid(a2)
        da2 = da3 * (s * (1.0 + a2 * (1.0 - s)))
        dlag_ref[...] += jnp.sum(da2 * xh, axis=0, keepdims=True)
        dlab_ref[...] += jnp.sum(da2, axis=0, keepdims=True)
        dxh = da2 * lag_ref[...]
        da1 = rstd * (dxh - jnp.mean(dxh, axis=-1, keepdims=True) - xh * jnp.mean(dxh * xh, axis=-1, keepdims=True))
        da1_ref[...] = da1
        dcb_ref[...] += jnp.sum(da1, axis=0, keepdims=True)

        tril = _tril_mask()
        for g in range(groups):
            sl = slice(g * LANES, (g + 1) * LANES)
            xh, rstd = _ln_stats(v_ref[0][:, sl].astype(F32))
            lg = lvg_ref[:, sl]
            vnb = (xh * lg + lvb_ref[:, sl]).astype(BF16)
            w = jnp.where(tril, spw_ref[g], 0.0).astype(BF16)
            wt = jnp.where(tril.T, spwt_ref[g], 0.0).astype(BF16)
            bias = spb_ref[:, g:g + 1]
            dbo_all = dc_ref[2 + (g * LANES) // half][:, (g * LANES) % half:(g * LANES) % half + LANES]
            dvn_parts = []
            dw_acc = jnp.zeros((CHUNK, CHUNK), F32)
            db_acc = jnp.zeros((CHUNK, LANES), F32)
            for ch in range(tm // CHUNK):
                rows = slice(ch * CHUNK, (ch + 1) * CHUNK)
                vs = _nn(w, vnb[rows]) + bias
                dbo = dbo_all[rows]
                uv = u_ref[0][rows, sl].astype(F32)
                dz_ref[0, rows, sl] = (dbo * vs).astype(BF16)
                dvs = dbo * uv
                dvsb = dvs.astype(BF16)
                dvn_parts.append(_nn(wt, dvsb))
                dw_acc = dw_acc + _nt(dvsb, vnb[rows])
                db_acc = db_acc + dvs
            dvn = jnp.concatenate(dvn_parts, axis=0)
            dspw_ref[g] += jnp.where(tril, dw_acc, 0.0)
            dspb_ref[g] += db_acc
            dlvg_ref[:, sl] += jnp.sum(dvn * xh, axis=0, keepdims=True)
            dlvb_ref[:, sl] += jnp.sum(dvn, axis=0, keepdims=True)
            dxh = dvn * lg
            dv = rstd * (dxh - jnp.mean(dxh, axis=-1, keepdims=True) - xh * jnp.mean(dxh * xh, axis=-1, keepdims=True))
            dz_ref[1, :, sl] = dv.astype(BF16)

        @pl.when(i == last)
        def _():
            for g in range(groups):
                dspb_ref[g] = jnp.zeros((CHUNK, LANES), F32) + jnp.sum(dspb_ref[g], axis=-1, keepdims=True)

    unit = lambda u: pl.BlockSpec((1, tm, c), lambda i: (u, i, 0))
    vec = pl.BlockSpec((1, c), lambda i: (0, 0))
    sq = pl.BlockSpec((groups, CHUNK, CHUNK), lambda i: (0, 0, 0))
    vshape = jax.ShapeDtypeStruct((1, c), F32)
    sshape = jax.ShapeDtypeStruct((groups, CHUNK, CHUNK), F32)
    return pl.pallas_call(
        body, name="mix_bwd_point", grid=(t // tm,),
        in_specs=[pl.BlockSpec((4, tm, half), lambda i: (0, i, 0)), unit(2), unit(3),
                  pl.BlockSpec((tm, c), lambda i: (i, 0)), vec, vec, vec, vec, sq, sq,
                  pl.BlockSpec((CHUNK, groups), lambda i: (0, 0))],
        out_specs=[pl.BlockSpec((2, tm, c), lambda i: (1, i, 0)), pl.BlockSpec((tm, c), lambda i: (i, 0)),
                   vec, vec, vec, vec, vec, sq, sq],
        out_shape=[jax.ShapeDtypeStruct((4, t, c), BF16), jax.ShapeDtypeStruct((t, c), F32),
                   vshape, vshape, vshape, vshape, vshape, sshape, sshape],
        compiler_params=_cparams(("arbitrary",)),
    )(dcat, z, z, a1, ln_a_g, ln_a_b, ln_v_g, ln_v_b, sp_w, sp_wt, sp_bt)


def mix_bwd_conv(dz, da1, z, conv_w, seq):
    _, t, c = z.shape
    tm = _tile(seq, 512)
    tiles_per_seq = seq // tm
    hb = tm // HALO
    n_halo_blocks = t // HALO

    rc = _tile(tm, CONV_ROWS)

    def body(dz_in_ref, d_ref, dh_ref, av_ref, ag_ref, cw_ref, dz_ref, dcw_ref, d1_ref, sh_ref, part_ref):
        del dz_in_ref
        i = pl.program_id(0)

        @pl.when(i == 0)
        def _():
            part_ref[...] = jnp.zeros_like(part_ref)

        d1_ref[0:tm, :] = d_ref[...]
        d1_ref[tm:, :] = jnp.where((i + 1) % tiles_per_seq == 0, 0.0, dh_ref[...])
        taps = _shifted_taps(d1_ref, sh_ref, tm, 0)

        def chunk(ci, carry):
            r0 = pl.multiple_of(ci * rc, rc)
            av = av_ref[0, pl.ds(r0, rc), :].astype(F32)
            s = _sigmoid(ag_ref[0, pl.ds(r0, rc), :].astype(F32))
            a0 = av * s
            da0 = jnp.zeros((rc, c), F32)
            for b, ro, back in taps:
                k = CONV_WIDTH - 1 - back
                rows = sh_ref[b, pl.ds(r0 + ro, rc), :]
                da0 = da0 + cw_ref[k:k + 1, :] * rows
                prod = a0 * rows
                part_ref[k] += functools.reduce(lambda p, q: p + q, [prod[8 * r:8 * r + 8] for r in range(rc // 8)])
            dz_ref[0, pl.ds(r0, rc), :] = (da0 * s).astype(BF16)
            dz_ref[1, pl.ds(r0, rc), :] = (da0 * av * s * (1.0 - s)).astype(BF16)
            return carry

        lax.fori_loop(0, tm // rc, chunk, 0)

        @pl.when(i == pl.num_programs(0) - 1)
        def _():
            dcw_ref[...] = jnp.sum(part_ref[...], axis=1)

    unit = lambda u: pl.BlockSpec((1, tm, c), lambda i: (u, i, 0))
    return pl.pallas_call(
        body, name="mix_bwd_conv", grid=(t // tm,),
        in_specs=[pl.BlockSpec(memory_space=pl.ANY), pl.BlockSpec((tm, c), lambda i: (i, 0)),
                  pl.BlockSpec((HALO, c), lambda i: (jnp.minimum((i + 1) * hb, n_halo_blocks - 1), 0)),
                  unit(0), unit(1), pl.BlockSpec((CONV_WIDTH, c), lambda i: (0, 0))],
        out_specs=[pl.BlockSpec((2, tm, c), lambda i: (0, i, 0)), pl.BlockSpec((CONV_WIDTH, c), lambda i: (0, 0))],
        out_shape=[jax.ShapeDtypeStruct(dz.shape, BF16), jax.ShapeDtypeStruct((CONV_WIDTH, c), F32)],
        scratch_shapes=[pltpu.VMEM((tm + HALO, c), F32), pltpu.VMEM((8, tm + HALO, c), F32),
                        pltpu.VMEM((CONV_WIDTH, 8, c), F32)],
        input_output_aliases={0: 0},
        compiler_params=_cparams(("arbitrary",)),
    )(dz, da1, da1, z, z, conv_w)


CHIP_FLIPS = ((1, 0), (0, 1), (1, 1))
ANY = pl.BlockSpec(memory_space=pl.ANY)


def _place():
    return lax.axis_index("x"), lax.axis_index("y"), lax.axis_index("c")


def _flip(v, f):
    return 1 - v if f else v


def place_shard(w, chip, dtype, name):
    n_layers, r, cc = w.shape
    rb = _tile(r, 512)

    def body(chip_ref, w_ref, *o_refs):
        del chip_ref
        for layer, o_ref in enumerate(o_refs):
            o_ref[0] = w_ref[layer].astype(dtype)

    return pl.pallas_call(
        body, name=name,
        grid_spec=pltpu.PrefetchScalarGridSpec(
            num_scalar_prefetch=1, grid=(r // rb,),
            in_specs=[pl.BlockSpec((n_layers, rb, cc), lambda i, chip_ref: (0, i, 0))],
            out_specs=[pl.BlockSpec((1, rb, cc), lambda i, chip_ref: (chip_ref[0], i, 0))] * n_layers),
        out_shape=[jax.ShapeDtypeStruct((N_CHIPS, r, cc), dtype)] * n_layers,
        compiler_params=_cparams(("parallel",)),
    )(chip, w)


class Carry:
    def __init__(self, arrays, out_shapes, aliased, sem_shapes, start, finish):
        self.arrays, self.out_shapes, self.aliased, self.sem_shapes = list(arrays), list(out_shapes), aliased, list(sem_shapes)
        self.start, self.finish = start, finish


def _call(body, *, name, grid, in_specs, out_specs, out_shape, args, sem, scratch_shapes=(), carry=None):
    if carry is None:
        res = pl.pallas_call(body, name=name, grid=grid, in_specs=in_specs, out_specs=out_specs, out_shape=out_shape,
                             scratch_shapes=list(scratch_shapes), compiler_params=_cparams(sem))(*args)
        return list(res), []
    n_in, n_out, n_scr, nc = len(args), len(out_shape), len(scratch_shapes), len(carry.arrays)

    def full_body(*refs):
        ins, refs = refs[:n_in], refs[n_in:]
        c_ins, refs = refs[:nc], refs[nc:]
        outs, refs = refs[:n_out], refs[n_out:]
        c_outs, refs = refs[:nc], refs[nc:]
        scr, sems = refs[:n_scr], refs[n_scr:]
        first = functools.reduce(lambda a, b: a & b, [pl.program_id(d) == 0 for d in range(len(grid))])
        last = functools.reduce(lambda a, b: a & b, [pl.program_id(d) == grid[d] - 1 for d in range(len(grid))])

        @pl.when(first)
        def _():
            carry.start(c_ins, c_outs, sems)

        body(*ins, *outs, *scr)

        @pl.when(last)
        def _():
            carry.finish(c_ins, c_outs, sems)

    res = pl.pallas_call(
        full_body, name=name, grid=grid, in_specs=list(in_specs) + [ANY] * nc, out_specs=list(out_specs) + [ANY] * nc,
        out_shape=list(out_shape) + carry.out_shapes, scratch_shapes=list(scratch_shapes) + carry.sem_shapes,
        input_output_aliases={n_in + i: n_out + i for i in range(nc)} if carry.aliased else {},
        compiler_params=pltpu.CompilerParams(dimension_semantics=("arbitrary",) * len(grid), vmem_limit_bytes=VMEM_LIMIT,
                                             has_side_effects=True),
    )(*args, *carry.arrays)
    return list(res[:n_out]), list(res[n_out:])


def _gather_ops(shapes, whole):
    n = len(shapes)

    def rows(a, c):
        hr = shapes[a][1] // 2
        return pl.ds(pl.multiple_of(c * hr, 16), hr)

    def start(ins, outs, sems):
        ici_send, ici_recv = sems[0], sems[1]
        x, y, c = _place()
        k = 2 * x + y
        for a in range(n):
            for o, (fx, fy) in enumerate(CHIP_FLIPS):
                src = ins[a].at[k] if whole[a] else ins[a].at[k, rows(a, c)]
                dst = outs[a].at[k] if whole[a] else outs[a].at[k, rows(a, c)]
                pltpu.make_async_remote_copy(
                    src_ref=src, dst_ref=dst, send_sem=ici_send.at[3 * a + o], recv_sem=ici_recv.at[3 * a + o],
                    device_id=(_flip(x, fx), _flip(y, fy), c), device_id_type=MESH).start()

    def finish(ins, outs, sems):
        ici_send, ici_recv, d2d_send, d2d_recv = sems
        x, y, c = _place()
        k = 2 * x + y
        sibling = (x, y, 1 - c)

        def copy(ref, send, recv, a, o):
            return pltpu.make_async_remote_copy(src_ref=ref, dst_ref=ref, send_sem=send.at[3 * a + o],
                                                recv_sem=recv.at[3 * a + o], device_id=sibling, device_id_type=MESH)

        for a in range(n):
            for o, (fx, fy) in enumerate(CHIP_FLIPS):
                kk = 2 * _flip(x, fx) + _flip(y, fy)
                landed = outs[a].at[kk] if whole[a] else outs[a].at[kk, rows(a, c)]
                copy(landed, ici_send, ici_recv, a, o).wait_recv()
                if not whole[a]:
                    copy(landed, d2d_send, d2d_recv, a, o).start()
        for a in range(n):
            for o, (fx, fy) in enumerate(CHIP_FLIPS):
                kk = 2 * _flip(x, fx) + _flip(y, fy)
                mine = ins[a].at[k] if whole[a] else ins[a].at[k, rows(a, c)]
                copy(mine, ici_send, ici_recv, a, o).wait_send()
                if not whole[a]:
                    copy(outs[a].at[kk, rows(a, 1 - c)], d2d_send, d2d_recv, a, o).wait_recv()
                    copy(outs[a].at[kk, rows(a, c)], d2d_send, d2d_recv, a, o).wait_send()

    dma = pltpu.SemaphoreType.DMA
    return start, finish, [dma((3 * n,))] * 4


def gather_carry(bufs):
    start, finish, sems = _gather_ops([b.shape for b in bufs], [False] * len(bufs))
    return Carry(bufs, [jax.ShapeDtypeStruct(b.shape, b.dtype) for b in bufs], True, sems, start, finish)


def allgather_weights(shards, smalls):
    bufs = list(shards) + list(smalls)
    n = len(bufs)
    start, finish, sems = _gather_ops([b.shape for b in bufs], [False] * len(shards) + [True] * len(smalls))

    def body(*refs):
        start(refs[:n], refs[n:2 * n], refs[2 * n:])
        finish(refs[:n], refs[n:2 * n], refs[2 * n:])

    res = pl.pallas_call(
        body, name="allgather_weights", in_specs=[ANY] * n, out_specs=[ANY] * n,
        out_shape=[jax.ShapeDtypeStruct(b.shape, b.dtype) for b in bufs], scratch_shapes=sems,
        input_output_aliases={i: i for i in range(n)},
        compiler_params=pltpu.CompilerParams(has_side_effects=True),
    )(*bufs)
    return res[:len(shards)], res[len(shards):]


def rs_exchange(grads):
    n = len(grads)

    def body(*refs):
        ins, outs = refs[:n], refs[n:2 * n]
        send, recv = refs[2 * n:]
        x, y, c = _place()
        cps = []
        for a in range(n):
            cp = pltpu.make_async_remote_copy(
                src_ref=ins[a].at[:, 1 - c], dst_ref=outs[a], send_sem=send.at[a], recv_sem=recv.at[a],
                device_id=(x, y, 1 - c), device_id_type=MESH)
            cp.start()
            cps.append(cp)
        for cp in cps:
            cp.wait()

    dma = pltpu.SemaphoreType.DMA
    return pl.pallas_call(
        body, name="rs_exchange", in_specs=[ANY] * n, out_specs=[ANY] * n,
        out_shape=[jax.ShapeDtypeStruct((g.shape[0],) + g.shape[2:], g.dtype) for g in grads],
        scratch_shapes=[dma((n,)), dma((n,))],
        compiler_params=pltpu.CompilerParams(has_side_effects=True),
    )(*grads)


def rs_add(gs, sibs, core, out_dtype, name):
    n = len(gs)
    nk = gs[0].shape[0]

    def body(core_ref, *refs):
        del core_ref
        for a in range(n):
            refs[2 * n + a][0] = (refs[a][0, 0] + refs[n + a][0]).astype(out_dtype)

    halves = [g.shape[2:] for g in gs]
    return pl.pallas_call(
        body, name=name,
        grid_spec=pltpu.PrefetchScalarGridSpec(
            num_scalar_prefetch=1, grid=(nk,),
            in_specs=[pl.BlockSpec((1, 1) + h, lambda k, core_ref: (k, core_ref[0], 0, 0)) for h in halves]
            + [pl.BlockSpec((1,) + h, lambda k, core_ref: (k, 0, 0)) for h in halves],
            out_specs=[pl.BlockSpec((1,) + h, lambda k, core_ref: (k, 0, 0)) for h in halves]),
        out_shape=[jax.ShapeDtypeStruct((nk,) + h, out_dtype) for h in halves],
        compiler_params=_cparams(("parallel",)),
    )(core, *gs, *sibs)


def send_carry(parts):
    n = len(parts)

    def copies(ins, outs, sems):
        x, y, c = _place()
        for a in range(n):
            for o, (fx, fy) in enumerate(CHIP_FLIPS):
                kk = 2 * _flip(x, fx) + _flip(y, fy)
                yield pltpu.make_async_remote_copy(
                    src_ref=ins[a].at[kk], dst_ref=outs[a].at[o], send_sem=sems[0].at[3 * a + o],
                    recv_sem=sems[1].at[3 * a + o], device_id=(_flip(x, fx), _flip(y, fy), c), device_id_type=MESH)

    def start(ins, outs, sems):
        for cp in copies(ins, outs, sems):
            cp.start()

    def finish(ins, outs, sems):
        for cp in copies(ins, outs, sems):
            cp.wait()

    dma = pltpu.SemaphoreType.DMA
    return Carry(parts, [jax.ShapeDtypeStruct((3,) + p.shape[1:], p.dtype) for p in parts], False,
                 [dma((3 * n,)), dma((3 * n,))], start, finish)


def rs_sum(recvs, parts, where, name):
    n_layers = len(recvs)
    _, hr, cc = recvs[0].shape
    rb = _tile(hr, 256)

    def body(where_ref, *refs):
        del where_ref
        o_ref = refs[-1]
        for layer in range(n_layers):
            r_ref, p_ref = refs[layer], refs[n_layers + layer]
            o_ref[layer, 0] = ((p_ref[0].astype(F32) + r_ref[0].astype(F32)) + r_ref[1].astype(F32)) + r_ref[2].astype(F32)

    return pl.pallas_call(
        body, name=name,
        grid_spec=pltpu.PrefetchScalarGridSpec(
            num_scalar_prefetch=1, grid=(hr // rb,),
            in_specs=[pl.BlockSpec((3, rb, cc), lambda i, w_ref: (0, i, 0))] * n_layers
            + [pl.BlockSpec((1, rb, cc), lambda i, w_ref: (w_ref[0], i, 0))] * n_layers,
            out_specs=pl.BlockSpec((n_layers, 1, rb, cc), lambda i, w_ref: (0, w_ref[1], i, 0))),
        out_shape=jax.ShapeDtypeStruct((n_layers, 2, hr, cc), F32),
        compiler_params=_cparams(("parallel",)),
    )(where, *recvs, *parts)


def rs_share(fulls):
    n = len(fulls)

    def body(*refs):
        ins, outs = refs[:n], refs[n:2 * n]
        send, recv = refs[2 * n:]
        x, y, c = _place()
        cps = []
        for a in range(n):
            cp = pltpu.make_async_remote_copy(
                src_ref=ins[a].at[:, c], dst_ref=outs[a].at[:, c], send_sem=send.at[a], recv_sem=recv.at[a],
                device_id=(x, y, 1 - c), device_id_type=MESH)
            cp.start()
            cps.append(cp)
        for a in range(n):
            got = outs[a].at[:, 1 - c]
            pltpu.make_async_remote_copy(
                src_ref=got, dst_ref=got, send_sem=send.at[a], recv_sem=recv.at[a],
                device_id=(x, y, 1 - c), device_id_type=MESH).wait_recv()
        for cp in cps:
            cp.wait_send()

    dma = pltpu.SemaphoreType.DMA
    return pl.pallas_call(
        body, name="rs_share", in_specs=[ANY] * n, out_specs=[ANY] * n,
        out_shape=[jax.ShapeDtypeStruct(f.shape, f.dtype) for f in fulls],
        scratch_shapes=[dma((n,)), dma((n,))],
        input_output_aliases={i: i for i in range(n)},
        compiler_params=pltpu.CompilerParams(has_side_effects=True),
    )(*fulls)


def allreduce_small(v):
    r, w = v.shape

    def body(v_ref, o_ref, buf, send, recv, loc):
        x, y, c = _place()
        me = 4 * x + 2 * y + c
        mine = pltpu.make_async_copy(v_ref, buf.at[me], loc)
        mine.start()
        cps = []
        for o in range(1, N_DEV):
            fx, fy, fc = (o >> 2) & 1, (o >> 1) & 1, o & 1
            cp = pltpu.make_async_remote_copy(
                src_ref=v_ref, dst_ref=buf.at[me], send_sem=send.at[o - 1], recv_sem=recv.at[o - 1],
                device_id=(_flip(x, fx), _flip(y, fy), _flip(c, fc)), device_id_type=MESH)
            cp.start()
            cps.append(cp)
        for o in range(1, N_DEV):
            fx, fy, fc = (o >> 2) & 1, (o >> 1) & 1, o & 1
            peer = 4 * _flip(x, fx) + 2 * _flip(y, fy) + _flip(c, fc)
            pltpu.make_async_remote_copy(
                src_ref=v_ref, dst_ref=buf.at[peer], send_sem=send.at[o - 1], recv_sem=recv.at[o - 1],
                device_id=(x, y, c), device_id_type=MESH).wait_recv()
        for cp in cps:
            cp.wait_send()
        mine.wait()
        acc = buf[0]
        for d in range(1, N_DEV):
            acc = acc + buf[d]
        o_ref[...] = acc

    dma = pltpu.SemaphoreType.DMA
    vm = pl.BlockSpec(memory_space=pltpu.VMEM)
    return pl.pallas_call(
        body, name="allreduce_small", in_specs=[vm], out_specs=vm,
        out_shape=jax.ShapeDtypeStruct((r, w), F32),
        scratch_shapes=[pltpu.VMEM((N_DEV, r, w), F32), dma((N_DEV - 1,)), dma((N_DEV - 1,)), dma],
        compiler_params=pltpu.CompilerParams(has_side_effects=True, vmem_limit_bytes=VMEM_LIMIT),
    )(v)


def adamw(w, g, m, v, name):
    r, cc = w.shape
    rb = _tile(r, 256)

    def body(w_ref, g_ref, m_ref, v_ref, d_ref, nm_ref, nv_ref):
        gv = g_ref[...]
        nm = ADAM_B1 * m_ref[...] + (1.0 - ADAM_B1) * gv
        nv = ADAM_B2 * v_ref[...] + (1.0 - ADAM_B2) * (gv * gv)
        m_hat = nm / (1.0 - ADAM_B1 ** ADAM_STEP)
        v_hat = nv / (1.0 - ADAM_B2 ** ADAM_STEP)
        d_ref[...] = -ADAM_LR * (m_hat / (jnp.sqrt(v_hat) + ADAM_EPS) + ADAM_WD * w_ref[...])
        nm_ref[...] = nm
        nv_ref[...] = nv

    blk = pl.BlockSpec((rb, cc), lambda i: (i, 0))
    shp = jax.ShapeDtypeStruct((r, cc), F32)
    return pl.pallas_call(
        body, name=name, grid=(r // rb,), in_specs=[blk] * 4, out_specs=[blk] * 3, out_shape=[shp] * 3,
        compiler_params=_cparams(("parallel",)),
    )(w, g, m, v)


WEIGHTS = ['g_ffn1', 'w_ffn1_gate', 'w_ffn1_up', 'w_ffn1_down', 'g_mix', 'w_in_ab', 'conv_w', 'conv_b', 'ln_a_g',
           'ln_a_b', 'ln_v_g', 'ln_v_b', 'sp_w', 'sp_b', 'w_out_ab', 'w_qkv', 'w_o', 'g_ffn2', 'w_ffn2_gate',
           'w_ffn2_up', 'w_ffn2_down', 'g_final']
BIG = ['w_ffn1_gate', 'w_ffn1_up', 'w_ffn1_down', 'w_in_ab', 'w_out_ab', 'w_qkv', 'w_o', 'w_ffn2_gate', 'w_ffn2_up',
       'w_ffn2_down']
SMALL = ['g_ffn1', 'g_mix', 'g_ffn2', 'g_final', 'conv_b', 'ln_a_g', 'ln_a_b', 'ln_v_g', 'ln_v_b', 'sp_b', 'sp_w']
HIDDEN_MAJOR = ['w_ffn1_gate', 'w_ffn1_up', 'w_ffn2_gate', 'w_ffn2_up']


CARRY_WEIGHTS = {"ffn_gateup": 9.2e6, "ffn_down": 6.1e6, "mm_in": 5.9e6, "mm_out": 3.3e6}


def _use_order(depth):
    order = []
    for layer in range(depth):
        order += [('w_ffn1_gate', layer), ('w_ffn1_up', layer), ('w_ffn1_down', layer)]
        order += [('w_in_ab', layer // 2), ('w_out_ab', layer // 2)] if layer % 2 == 0 else [('w_qkv', layer // 2), ('w_o', layer // 2)]
        order += [('w_ffn2_gate', layer), ('w_ffn2_up', layer), ('w_ffn2_down', layer)]
    return order


def _rows(a):
    return a.reshape(-1, LANES)


def _pack(parts):
    v = jnp.concatenate([_rows(p) for p in parts], axis=0)
    pad = (-v.shape[0]) % 8
    return jnp.pad(v, ((0, pad), (0, 0)))


def _unpack(v, shapes):
    out, r = [], 0
    for s in shapes:
        n = 1
        for d in s:
            n *= d
        n //= LANES
        out.append(v[r:r + n].reshape(s))
        r += n
    return out


def kernel(x, g_ffn1, w_ffn1_gate, w_ffn1_up, w_ffn1_down, g_mix, w_in_ab, conv_w, conv_b, ln_a_g, ln_a_b, ln_v_g, ln_v_b, sp_w, sp_b, w_out_ab, w_qkv, w_o, g_ffn2, w_ffn2_gate, w_ffn2_up, w_ffn2_down, g_final, loss_target, m_g_ffn1, m_w_ffn1_gate, m_w_ffn1_up, m_w_ffn1_down, m_g_mix, m_w_in_ab, m_conv_w, m_conv_b, m_ln_a_g, m_ln_a_b, m_ln_v_g, m_ln_v_b, m_sp_w, m_sp_b, m_w_out_ab, m_w_qkv, m_w_o, m_g_ffn2, m_w_ffn2_gate, m_w_ffn2_up, m_w_ffn2_down, m_g_final, v_g_ffn1, v_w_ffn1_gate, v_w_ffn1_up, v_w_ffn1_down, v_g_mix, v_w_in_ab, v_conv_w, v_conv_b, v_ln_a_g, v_ln_a_b, v_ln_v_g, v_ln_v_b, v_sp_w, v_sp_b, v_w_out_ab, v_w_qkv, v_w_o, v_g_ffn2, v_w_ffn2_gate, v_w_ffn2_up, v_w_ffn2_down, v_g_final):
    p = dict(locals())
    for name in HIDDEN_MAJOR:
        for pre in ('', 'm_', 'v_'):
            p[pre + name] = jnp.swapaxes(p[pre + name], 1, 2)
    back = lambda name, a: jnp.swapaxes(a, 1, 2) if name in HIDDEN_MAJOR else a
    n_seq, seq, d = x.shape
    t = n_seq * seq
    depth = g_ffn1.shape[0]
    core = lax.axis_index("c")
    chip = 2 * lax.axis_index("x") + lax.axis_index("y")
    xf = x.reshape(t, d)
    target = loss_target.reshape(t, d)

    items = []
    for name in BIG:
        for layer in range(p[name].shape[0]):
            items.append((name, layer))
    chip1 = chip.reshape(1).astype(jnp.int32)
    placed = {}
    for name in BIG:
        for layer, buf in enumerate(place_shard(p[name], chip1, BF16, "place_shard")):
            placed[(name, layer)] = buf
    first = [('w_ffn1_gate', 0), ('w_ffn1_up', 0)]
    gathered, (conv_w4,) = allgather_weights([placed[it] for it in first],
                                             place_shard(conv_w, chip1, F32, "place_conv_w"))
    wt = dict(zip(first, gathered))
    waiting = [it for it in _use_order(depth) if it not in wt]

    def riders(name):
        room, take = CARRY_WEIGHTS[name], []
        for it in list(waiting):
            if placed[it].size <= room:
                room -= placed[it].size
                take.append(it)
                waiting.remove(it)
        return (take, gather_carry([placed[it] for it in take])) if take else (take, None)

    def landed(take, carried):
        wt.update(zip(take, carried))

    def weight(it):
        if it not in wt:
            waiting.remove(it)
            (wt[it],), _ = allgather_weights([placed[it]], [])
        return wt[it]

    c_mix = conv_w4.shape[2] * N_CHIPS
    conv_full = jnp.transpose(conv_w4, (1, 0, 2)).reshape(CONV_WIDTH, c_mix)
    vec = lambda a: a.reshape(1, -1)
    sp_bt = sp_b[0].T
    sp_wt = jnp.transpose(sp_w[0], (0, 2, 1))
    d_ff = w_ffn1_gate.shape[2]
    n_in = w_in_ab.shape[2]
    n_qkv = w_qkv.shape[2] // 3

    saved = []
    xc = xf
    h = rmsnorm_fwd(xc, vec(g_ffn1[0]), "norm_first")
    for layer in range(depth):
        s = {}
        for half, (gn, wn) in enumerate((('g_ffn1', 'w_ffn1'), ('g_ffn2', 'w_ffn2'))):
            if half == 1:
                s['x_mix'], s['h_mix'] = xc, h
                if layer % 2 == 0:
                    w_in = weight(('w_in_ab', layer // 2))
                    take, carry = riders("mm_in")
                    (z,), got = colmm(h, [w_in], n_in, BF16, "mm_in", carry)
                    landed(take, got)
                    cat, a1 = mix_fwd(z, conv_full, conv_b, ln_a_g, ln_a_b, vec(ln_v_g), vec(ln_v_b), sp_w[0], sp_bt, seq)
                    s.update(z=z, cat=cat, a1=a1)
                    w_out = weight(('w_out_ab', layer // 2))
                    take, carry = riders("mm_out")
                    (xc, h), got = rowmm(cat, w_out, xc, 1.0, "mm_out", carry, vec(g_ffn2[layer]))
                    landed(take, got)
                else:
                    (qkv,), _ = colmm(h, [weight(('w_qkv', layer // 2))], n_qkv, BF16, "mm_qkv")
                    o, tot, cnt = attn_fwd(qkv, n_seq, seq)
                    s.update(qkv=qkv, o=o, tot=tot, cnt=cnt)
                    (xc, h), _ = rowmm(o, weight(('w_o', layer // 2)), xc, 1.0, "mm_o", None, vec(g_ffn2[layer]))
            s['x' + wn] = xc
            w_gate, w_up = weight((wn + '_gate', layer)), weight((wn + '_up', layer))
            take, carry = riders("ffn_gateup")
            (silu, udsilu, act), got = colmm(h, [w_gate, w_up], d_ff, BF16, "ffn_gateup", carry, swiglu=True)
            landed(take, got)
            s.update({'h' + wn: h, 'swiglu' + wn: (silu, udsilu), 'act' + wn: act})
            w_down = weight((wn + '_down', layer))
            take, carry = riders("ffn_down")
            following = g_mix[layer] if half == 0 else (g_ffn1[layer + 1] if layer + 1 < depth else None)
            (xc, h), got = rowmm(act, w_down, xc, 0.5, "ffn_down", carry, None if following is None else vec(following))
            landed(take, got)
        saved.append(s)

    loss8, dx, dxb, dg_final = loss_head(xc, vec(g_final), target)
    loss = lax.psum(loss8[0, 0], ("x", "y", "c"))

    gw = {}
    gs = {}
    core1 = core.reshape(1).astype(jnp.int32)
    ready = []
    part, recv = {}, {}

    def leaving():
        its = list(ready)
        ready.clear()
        halves = lambda a: a.reshape(N_CHIPS, 2, a.shape[1] // 2, a.shape[2])
        theirs = rs_exchange([halves(gw[it][1]) for it in its])
        sums = rs_add([halves(gw[it][0]) for it in its], theirs, core1, REDUCE_DTYPE, "rs_add")
        part.update(zip(its, sums))
        return its, send_carry(sums)

    for layer in reversed(range(depth)):
        s = saved[layer]
        for half, (gn, wn) in reversed(list(enumerate((('g_ffn1', 'w_ffn1'), ('g_ffn2', 'w_ffn2'))))):
            wd = wt[(wn + '_down', layer)]
            dgate, dup = rowmm_t(dxb, wd, 0.5, BF16, "ffn_bwd_act", swiglu=s['swiglu' + wn])
            gw[(wn + '_down', layer)] = dw_row(s['act' + wn], dxb, 0.5, "ffn_dw_down")
            gw[(wn + '_gate', layer)], gw[(wn + '_up', layer)] = dw_col(s['h' + wn], [dgate, dup], N_CHIPS, d_ff,
                                                                        "ffn_dw_gateup", transposed=True)
            ready.extend([(wn + '_down', layer), (wn + '_gate', layer), (wn + '_up', layer)])
            its, carry = leaving()
            (dx, dxb, dg), got = colmm_t([dgate, dup], [wt[(wn + '_gate', layer)], wt[(wn + '_up', layer)]], d_ff,
                                         s['x' + wn], vec(p[gn][layer]), dx, "ffn_bwd_in", carry, transposed=True)
            recv.update(zip(its, got))
            gs[(gn, layer)] = dg
            if half == 1:
                if layer % 2 == 0:
                    i = layer // 2
                    w_out = wt[('w_out_ab', i)]
                    dcat = rowmm_t(dxb, w_out, 1.0, F32, "mm_out_t")
                    gw[('w_out_ab', i)] = dw_row(s['cat'], dxb, 1.0, "dw_out")
                    dz, da1, dcb, dlag, dlab, dlvg, dlvb, dspw, dspb = mix_bwd_point(
                        dcat, s['z'], s['a1'], ln_a_g, ln_a_b, vec(ln_v_g), vec(ln_v_b), sp_w[0], sp_wt, sp_bt, seq)
                    dz, dcw = mix_bwd_conv(dz, da1, s['z'], conv_full, seq)
                    gs.update({('conv_b', i): dcb, ('ln_a_g', i): dlag, ('ln_a_b', i): dlab, ('ln_v_g', i): dlvg,
                               ('ln_v_b', i): dlvb, ('sp_w', i): dspw, ('sp_b', i): dspb[:, :, 0], ('conv_w', i): dcw})
                    (gw[('w_in_ab', i)],) = dw_col(s['h_mix'], [dz], N_CHIPS, n_in, "dw_in")
                    ready.extend([('w_out_ab', i), ('w_in_ab', i)])
                    its, carry = leaving()
                    (dx, dxb, dg), got = colmm_t([dz], [wt[('w_in_ab', i)]], n_in, s['x_mix'], vec(g_mix[layer]), dx,
                                                 "mm_in_t", carry)
                    recv.update(zip(its, got))
                else:
                    i = layer // 2
                    w_o4 = wt[('w_o', i)]
                    do = rowmm_t(dxb, w_o4, 1.0, BF16, "mm_o_t")
                    gw[('w_o', i)] = dw_row(s['o'], dxb, 1.0, "dw_o")
                    dq, dk, dv = attn_bwd(s['qkv'], do, s['tot'], s['cnt'], n_seq, seq)
                    dqkv = jnp.concatenate([dq, dk, dv], axis=0)
                    (gw[('w_qkv', i)],) = dw_col(s['h_mix'], [dqkv], N_CHIPS, n_qkv, "dw_qkv")
                    ready.extend([('w_o', i), ('w_qkv', i)])
                    its, carry = leaving()
                    (dx, dxb, dg), got = colmm_t([dqkv], [wt[('w_qkv', i)]], n_qkv, s['x_mix'], vec(g_mix[layer]), dx,
                                                 "mm_qkv_t", carry)
                    recv.update(zip(its, got))
                gs[('g_mix', layer)] = dg
    grad_x = dx.reshape(x.shape)

    assert not ready and set(recv) == set(items)
    where = jnp.stack([chip, core]).astype(jnp.int32)
    fulls = []
    for name in BIG:
        its = [(name, layer) for layer in range(p[name].shape[0])]
        fulls.append(rs_sum([recv[it] for it in its], [part[it] for it in its], where, "rs_sum"))
    shared = rs_share(fulls)
    grads = {name: sh.reshape(p[name].shape) for name, sh in zip(BIG, shared)}

    stack = lambda name: jnp.concatenate([gs[(name, layer)].reshape((1,) + p[name].shape[1:]) for layer in range(p[name].shape[0])], axis=0)
    small_g = [stack(name) if name != 'g_final' else dg_final.reshape(p[name].shape) for name in SMALL]
    packed = _pack(small_g + [gs[('conv_w', 0)]])
    red = allreduce_small(packed)
    outs = _unpack(red, [p[name].shape for name in SMALL] + [(CONV_WIDTH, c_mix)])
    for name, g in zip(SMALL, outs[:-1]):
        grads[name] = g
    conv_g = outs[-1].reshape(CONV_WIDTH, N_CHIPS, c_mix // N_CHIPS)
    grads['conv_w'] = lax.dynamic_index_in_dim(conv_g, chip, axis=1, keepdims=False).reshape(conv_w.shape)

    delta, new_m, new_v = {}, {}, {}
    for name in BIG:
        shp = p[name].shape
        two = lambda a: a.reshape(shp[0] * shp[1], shp[2])
        dl, nm, nv = adamw(two(p[name]), two(grads[name]), two(p['m_' + name]), two(p['v_' + name]), "adamw")
        delta[name], new_m[name], new_v[name] = dl.reshape(shp), nm.reshape(shp), nv.reshape(shp)
    small_names = SMALL + ['conv_w']
    pk = lambda pre: _pack([p[pre + name] for name in small_names])
    dl, nm, nv = adamw(pk(''), _pack([grads[name] for name in small_names]), pk('m_'), pk('v_'), "adamw_small")
    shapes = [p[name].shape for name in small_names]
    for dst, val in ((delta, dl), (new_m, nm), (new_v, nv)):
        for name, a in zip(small_names, _unpack(val, shapes)):
            dst[name] = a

    return (loss, grad_x, *[back(n, d[n]) for d in (grads, delta, new_m, new_v) for n in WEIGHTS])
```

```python
import functools

import jax
import jax.numpy as jnp
from jax import lax
from jax.experimental import pallas as pl
from jax.experimental.pallas import tpu as pltpu

F32 = jnp.float32
BF16 = jnp.bfloat16
EPS = 1e-6
HEAD_DIM = 64
CONV_WIDTH = 31
CHUNK = 128
KBLK = 128
ATT_BLOCK = 256
ATT_LANES = 256
DW_TOKENS = 2048
CONV_ROWS = 64
MASKED = -1e30
STICK_GONE = -110.0
LANES = 128
HALO = 32
ADAM_LR, ADAM_B1, ADAM_B2, ADAM_EPS, ADAM_WD, ADAM_STEP = 0.001, 0.9, 0.999, 1e-08, 0.01, 10
VMEM_LIMIT = 56 * 1024 * 1024
MESH = pl.DeviceIdType.MESH
N_CHIPS = 4
N_DEV = 8
REDUCE_DTYPE = BF16


def _cparams(sem):
    return pltpu.CompilerParams(dimension_semantics=sem, vmem_limit_bytes=VMEM_LIMIT)


def _nt(a, b):
    return lax.dot_general(a, b, (((1,), (1,)), ((), ())), preferred_element_type=F32)


def _tn(a, b):
    return lax.dot_general(a, b, (((0,), (0,)), ((), ())), preferred_element_type=F32)


def _nn(a, b):
    return jnp.dot(a, b, preferred_element_type=F32)


def _sigmoid(x):
    return 0.5 * jnp.tanh(0.5 * x) + 0.5


def _tile(t, want):
    if t <= want:
        return t
    for cand in range(want - want % 8, 7, -8):
        if t % cand == 0:
            return cand
    raise ValueError((t, want))


def rmsnorm_fwd(x, g, name):
    t, d = x.shape
    tm = _tile(t, 512)

    def body(x_ref, g_ref, h_ref):
        xv = x_ref[...]
        r = lax.rsqrt(jnp.mean(xv * xv, axis=-1, keepdims=True) + EPS)
        h_ref[...] = (xv * r * g_ref[...]).astype(BF16)

    return pl.pallas_call(
        body, name=name, grid=(t // tm,),
        in_specs=[pl.BlockSpec((tm, d), lambda i: (i, 0)), pl.BlockSpec((1, d), lambda i: (0, 0))],
        out_specs=pl.BlockSpec((tm, d), lambda i: (i, 0)),
        out_shape=jax.ShapeDtypeStruct((t, d), BF16),
        compiler_params=_cparams(("parallel",)),
    )(x, g)


def colmm(h, ws, nu, out_dtype, name, carry=None, swiglu=False):
    t, k = h.shape
    j, nj = (ws[0].shape[0], ws[0].shape[1]) if swiglu else (ws[0].shape[0], ws[0].shape[2])
    per = nj // nu
    units = j * per
    tm = _tile(t, 1024)
    nw = len(ws)
    n_out = 3 if swiglu else nw

    def body(*refs):
        h_ref = refs[0]
        hv = h_ref[...]
        if swiglu:
            silu_ref, udsilu_ref, act_ref = refs[1 + nw:]
            gv = _nt(hv, refs[1][0])
            uv = _nt(hv, refs[2][0])
            s = _sigmoid(gv)
            silu = gv * s
            silu_ref[0] = silu.astype(out_dtype)
            udsilu_ref[0] = (uv * (s + silu * (1.0 - s))).astype(out_dtype)
            act_ref[0] = (silu * uv).astype(out_dtype)
            return
        for n in range(nw):
            for s in range(j):
                res = _nn(hv, refs[1 + n][s]).astype(out_dtype)
                for u in range(per):
                    refs[1 + nw + n][s * per + u] = res[:, u * nu:(u + 1) * nu]

    out_shape = [jax.ShapeDtypeStruct((units, t, nu), out_dtype)] * n_out
    if swiglu:
        assert nw == 2 and per == 1
        return _call(
            body, name=name, grid=(j, t // tm),
            in_specs=[pl.BlockSpec((tm, k), lambda s, i: (i, 0))] + [pl.BlockSpec((1, nj, k), lambda s, i: (s, 0, 0))] * nw,
            out_specs=[pl.BlockSpec((1, tm, nu), lambda s, i: (s, i, 0))] * n_out, out_shape=out_shape,
            args=[h, *ws], sem=("parallel", "parallel"), carry=carry)
    return _call(
        body, name=name, grid=(t // tm,),
        in_specs=[pl.BlockSpec((tm, k), lambda i: (i, 0))] + [pl.BlockSpec((j, k, nj), lambda i: (0, 0, 0))] * nw,
        out_specs=[pl.BlockSpec((units, tm, nu), lambda i: (0, i, 0))] * n_out, out_shape=out_shape,
        args=[h, *ws], sem=("parallel",), carry=carry)


def rowmm(a, w, resid, scale, name, carry=None, norm_g=None):
    u_n, t, ku = a.shape
    n = w.shape[2]
    tm = _tile(t, 512)

    def body(a_ref, w_ref, r_ref, *rest):
        acc = jnp.zeros((tm, n), F32)
        for u in range(u_n):
            acc = acc + _nn(a_ref[u], w_ref[u])
        out = r_ref[...] + scale * acc
        if norm_g is None:
            (o_ref,) = rest
        else:
            g_ref, o_ref, h_ref = rest
            r = lax.rsqrt(jnp.mean(out * out, axis=-1, keepdims=True) + EPS)
            h_ref[...] = (out * r * g_ref[...]).astype(BF16)
        o_ref[...] = out

    row = pl.BlockSpec((tm, n), lambda i: (i, 0))
    normed = norm_g is not None
    outs, carried = _call(
        body, name=name, grid=(t // tm,),
        in_specs=[pl.BlockSpec((u_n, tm, ku), lambda i: (0, i, 0)), pl.BlockSpec((u_n, ku, n), lambda i: (0, 0, 0)),
                  row] + [pl.BlockSpec((1, n), lambda i: (0, 0))] * normed,
        out_specs=[row] + [row] * normed,
        out_shape=[jax.ShapeDtypeStruct((t, n), F32)] + [jax.ShapeDtypeStruct((t, n), BF16)] * normed,
        args=[a, w, resid] + [norm_g] * normed, sem=("parallel",), carry=carry)
    return (outs[0], outs[1] if normed else None), carried


def rowmm_t(dyb, w, scale, out_dtype, name, swiglu=None):
    t, n = dyb.shape
    u_n, ku, _ = w.shape
    tm = _tile(t, 512)

    if swiglu is None:
        def body(dy_ref, w_ref, o_ref):
            dy = dy_ref[...]
            for u in range(u_n):
                o_ref[u] = (scale * _nt(dy, w_ref[u])).astype(out_dtype)

        return pl.pallas_call(
            body, name=name, grid=(t // tm,),
            in_specs=[pl.BlockSpec((tm, n), lambda i: (i, 0)), pl.BlockSpec((u_n, ku, n), lambda i: (0, 0, 0))],
            out_specs=pl.BlockSpec((u_n, tm, ku), lambda i: (0, i, 0)),
            out_shape=jax.ShapeDtypeStruct((u_n, t, ku), out_dtype),
            compiler_params=_cparams(("parallel",)),
        )(dyb, w)

    def body(dy_ref, w_ref, silu_ref, udsilu_ref, dg_ref, du_ref):
        dy = dy_ref[...]
        for u in range(u_n):
            dact = scale * _nt(dy, w_ref[u])
            dg_ref[u] = (dact * udsilu_ref[u].astype(F32)).astype(BF16)
            du_ref[u] = (dact * silu_ref[u].astype(F32)).astype(BF16)

    blk = pl.BlockSpec((u_n, tm, ku), lambda i: (0, i, 0))
    return pl.pallas_call(
        body, name=name, grid=(t // tm,),
        in_specs=[pl.BlockSpec((tm, n), lambda i: (i, 0)), pl.BlockSpec((u_n, ku, n), lambda i: (0, 0, 0)), blk, blk],
        out_specs=[blk] * 2, out_shape=[jax.ShapeDtypeStruct((u_n, t, ku), BF16)] * 2,
        compiler_params=_cparams(("parallel",)),
    )(dyb, w, *swiglu)


def colmm_t(dzs, ws, nu, x, g, dy_in, name, carry=None, transposed=False):
    t, k = x.shape
    j, nj = (ws[0].shape[0], ws[0].shape[1]) if transposed else (ws[0].shape[0], ws[0].shape[2])
    per = nj // nu
    units = j * per
    nw = len(ws)
    tm = _tile(t, 512)
    assert not transposed or per == 1

    def body(*refs):
        dz_refs = refs[:nw]
        w_refs = refs[nw:2 * nw]
        x_ref, g_ref, dy_ref, dx_ref, dxb_ref, dg_ref = refs[2 * nw:]
        i = pl.program_id(0)
        dh = jnp.zeros((tm, k), F32)
        for n in range(nw):
            for u in range(units):
                if transposed:
                    dh = dh + _nn(dz_refs[n][u], w_refs[n][u])
                else:
                    wv = w_refs[n][u // per, :, (u % per) * nu:(u % per + 1) * nu]
                    dh = dh + _nt(dz_refs[n][u], wv)
        xv = x_ref[...]
        gv = g_ref[...]
        r = lax.rsqrt(jnp.mean(xv * xv, axis=-1, keepdims=True) + EPS)
        uu = dh * gv
        dx = dy_ref[...] + r * uu - xv * (r * r * r * jnp.mean(uu * xv, axis=-1, keepdims=True))
        dx_ref[...] = dx
        dxb_ref[...] = dx.astype(BF16)
        part = jnp.sum(dh * (xv * r), axis=0, keepdims=True)

        @pl.when(i == 0)
        def _():
            dg_ref[...] = part

        @pl.when(i > 0)
        def _():
            dg_ref[...] += part

    dz_spec = pl.BlockSpec((units, tm, nu), lambda i: (0, i, 0))
    w_spec = pl.BlockSpec((j, nj, k) if transposed else (j, k, nj), lambda i: (0, 0, 0))
    row = pl.BlockSpec((tm, k), lambda i: (i, 0))
    vec = pl.BlockSpec((1, k), lambda i: (0, 0))
    return _call(
        body, name=name, grid=(t // tm,),
        in_specs=[dz_spec] * nw + [w_spec] * nw + [row, vec, row],
        out_specs=[row, row, vec],
        out_shape=[jax.ShapeDtypeStruct((t, k), F32), jax.ShapeDtypeStruct((t, k), BF16),
                   jax.ShapeDtypeStruct((1, k), F32)],
        args=[*dzs, *ws, x, g, dy_in], sem=("arbitrary",), carry=carry)


def dw_col(h, dzs, j, nu, name, transposed=False):
    t, k = h.shape
    units = dzs[0].shape[0]
    per = units // j
    nw = len(dzs)
    tt = _tile(t, DW_TOKENS)
    assert not transposed or per == 1

    def body(*refs):
        h_ref = refs[0]
        s = pl.program_id(1)
        hv = h_ref[...]
        outs, copies = refs[1 + nw:1 + 2 * nw], refs[1 + 2 * nw:]

        @pl.when(s == 0)
        def _():
            for o_ref in outs:
                o_ref[...] = jnp.zeros_like(o_ref)

        for n in range(nw):
            if transposed:
                outs[n][0] += _tn(refs[1 + n][0], hv)
                continue
            for u in range(per):
                outs[n][0, :, u * nu:(u + 1) * nu] += _tn(hv, refs[1 + n][u])

        @pl.when(s == pl.num_programs(1) - 1)
        def _():
            for o_ref, c_ref in zip(outs, copies):
                c_ref[...] = o_ref[...].astype(REDUCE_DTYPE)

    shard = (nu, k) if transposed else (k, per * nu)
    o_spec = pl.BlockSpec((1,) + shard, lambda u, s: (u, 0, 0))
    res = pl.pallas_call(
        body, name=name, grid=(j, t // tt),
        in_specs=[pl.BlockSpec((tt, k), lambda u, s: (s, 0))] + [pl.BlockSpec((per, tt, nu), lambda u, s: (u, s, 0))] * nw,
        out_specs=[o_spec] * (2 * nw),
        out_shape=[jax.ShapeDtypeStruct((j,) + shard, F32)] * nw + [jax.ShapeDtypeStruct((j,) + shard, REDUCE_DTYPE)] * nw,
        compiler_params=_cparams(("parallel", "arbitrary")),
    )(h, *dzs)
    return list(zip(res[:nw], res[nw:]))


def dw_row(a, dyb, scale, name):
    u_n, t, ku = a.shape
    n = dyb.shape[1]
    tt = _tile(t, DW_TOKENS)

    per_step = u_n if u_n * ku <= n else 1

    def body(a_ref, dy_ref, o_ref, c_ref):
        @pl.when(pl.program_id(1) == 0)
        def _():
            o_ref[...] = jnp.zeros_like(o_ref)

        dy = dy_ref[...]
        for u in range(per_step):
            o_ref[u] += scale * _tn(a_ref[u], dy)

        @pl.when(pl.program_id(1) == pl.num_programs(1) - 1)
        def _():
            c_ref[...] = o_ref[...].astype(REDUCE_DTYPE)

    o_spec = pl.BlockSpec((per_step, ku, n), lambda u, s: (u, 0, 0))
    return tuple(pl.pallas_call(
        body, name=name, grid=(u_n // per_step, t // tt),
        in_specs=[pl.BlockSpec((per_step, tt, ku), lambda u, s: (u, s, 0)), pl.BlockSpec((tt, n), lambda u, s: (s, 0))],
        out_specs=[o_spec, o_spec],
        out_shape=[jax.ShapeDtypeStruct((u_n, ku, n), F32), jax.ShapeDtypeStruct((u_n, ku, n), REDUCE_DTYPE)],
        compiler_params=_cparams(("parallel", "arbitrary")),
    )(a, dyb))


def loss_head(x, g, target):
    t, d = x.shape
    tm = _tile(t, 256)

    def body(x_ref, g_ref, t_ref, loss_ref, dx_ref, dxb_ref, dg_ref):
        i = pl.program_id(0)
        xv = x_ref[...]
        gv = g_ref[...]
        r = lax.rsqrt(jnp.mean(xv * xv, axis=-1, keepdims=True) + EPS)
        xh = xv * r
        err = xh * gv - t_ref[...]
        dy = err * (1.0 / d)
        uu = dy * gv
        dx = r * uu - xv * (r * r * r * jnp.mean(uu * xv, axis=-1, keepdims=True))
        dx_ref[...] = dx
        dxb_ref[...] = dx.astype(BF16)
        dg_part = jnp.sum(dy * xh, axis=0, keepdims=True)
        row = jnp.sum(err * err, axis=-1, keepdims=True) * (0.5 / d)
        l_part = jnp.zeros((8, LANES), F32) + jnp.sum(row, axis=0, keepdims=True)

        @pl.when(i == 0)
        def _():
            dg_ref[...] = dg_part
            loss_ref[...] = l_part

        @pl.when(i > 0)
        def _():
            dg_ref[...] += dg_part
            loss_ref[...] += l_part

    row = pl.BlockSpec((tm, d), lambda i: (i, 0))
    vec = pl.BlockSpec((1, d), lambda i: (0, 0))
    return pl.pallas_call(
        body, name="loss_head", grid=(t // tm,),
        in_specs=[row, vec, row],
        out_specs=[pl.BlockSpec((8, LANES), lambda i: (0, 0)), row, row, vec],
        out_shape=[jax.ShapeDtypeStruct((8, LANES), F32), jax.ShapeDtypeStruct((t, d), F32),
                   jax.ShapeDtypeStruct((t, d), BF16), jax.ShapeDtypeStruct((1, d), F32)],
        compiler_params=_cparams(("arbitrary",)),
    )(x, g, target)


def _split(v):
    hi = v.astype(BF16)
    lo = (v - hi.astype(F32)).astype(BF16)
    return hi, lo


def _keysums(v, m_ext):
    hi, lo = _split(v)
    outs = []
    for j in range(v.shape[1] // KBLK):
        sl = slice(j * KBLK, (j + 1) * KBLK)
        cs = _nn(jnp.concatenate([hi[:, sl], lo[:, sl]], axis=1), m_ext)
        outs.append((cs[:, :KBLK], cs[:, KBLK:]))
    return outs


def _softplus_parts(z):
    sp = jnp.maximum(z, 0.0) + jnp.log(1.0 + jnp.exp(-jnp.abs(z)))
    return sp, z - sp


def _sum_matrices():
    r = lax.broadcasted_iota(jnp.int32, (2 * KBLK, 2 * KBLK), 0) % KBLK
    c = lax.broadcasted_iota(jnp.int32, (2 * KBLK, 2 * KBLK), 1)
    suffix = jnp.where((r > c) | (c >= KBLK), 1.0, 0.0).astype(BF16)
    prefix = jnp.where((r <= c) | (c >= KBLK), 1.0, 0.0).astype(BF16)
    return suffix, prefix


def _att_geometry(qkv, seq):
    upp = qkv.shape[0] // 3
    bq = min(ATT_BLOCK, seq)
    per_unit = (2 * LANES) // ATT_LANES
    return upp, bq, seq // bq, bq // KBLK, per_unit, upp * per_unit, ATT_LANES // HEAD_DIM


def _head_lanes(rows, heads):
    lane = lax.broadcasted_iota(jnp.int32, (rows, ATT_LANES), 1)
    return [(lane >= HEAD_DIM * h) & (lane < HEAD_DIM * (h + 1)) for h in range(heads)]


def attn_fwd(qkv, n_seq, seq):
    t = qkv.shape[1]
    upp, bq, nq, nsub, per_unit, groups, heads = _att_geometry(qkv, seq)
    suffix_m, _ = _sum_matrices()

    def body(q_ref, k_ref, v_ref, m_ref, o_ref, tot_ref, cnt_ref):
        qi = pl.program_id(2)
        step_id = (pl.program_id(0) * groups + pl.program_id(1)) * nq + qi
        in_head = _head_lanes(bq, heads)
        only = lambda v, h: jnp.where(in_head[h], v, jnp.zeros_like(v))
        q_all = q_ref[0] * jnp.asarray(HEAD_DIM ** -0.5, BF16)
        qs = [only(q_all, h) for h in range(heads)]
        m_ext = m_ref[...]
        row = lax.broadcasted_iota(jnp.int32, (bq, bq), 0)
        col = lax.broadcasted_iota(jnp.int32, (bq, bq), 1)
        diag_mask = col < row

        def block(kj, carry, mask):
            off = pl.multiple_of(kj * bq, bq)
            k_all = k_ref[0, pl.ds(off, bq), :]
            v_all = v_ref[0, pl.ds(off, bq), :]
            rems, acc = carry
            out = []
            for h in range(heads):
                rem = rems[h]
                z = _nt(qs[h], k_all)
                if mask is not None:
                    z = jnp.where(mask, z, MASKED)
                sp, ls = _softplus_parts(z)
                sums = _keysums(-sp, m_ext)
                parts = [None] * nsub
                for j in reversed(range(nsub)):
                    suf, total = sums[j]
                    parts[j] = jnp.exp(ls[:, j * KBLK:(j + 1) * KBLK] + suf + rem)
                    rem = rem + total
                a = jnp.concatenate(parts, axis=1)
                acc = acc + _nn(a.astype(BF16), only(v_all, h))
                out.append(rem)
            return tuple(out), acc

        def most_left(c):
            return functools.reduce(jnp.maximum, [jnp.max(r) for r in c[0]])

        def more(s):
            return (s[0] < qi) & (s[1] > STICK_GONE)

        def step(s):
            c = block(qi - 1 - s[0], s[2], None)
            return s[0] + 1, most_left(c), c

        zero = jnp.zeros((bq, LANES), F32)
        carry = block(qi, ((zero,) * heads, jnp.zeros((bq, ATT_LANES), F32)), diag_mask)
        n_left, _, (rems, acc) = lax.while_loop(more, step, (jnp.int32(0), most_left(carry), carry))
        o_ref[0] = acc.astype(BF16)
        first = lax.broadcasted_iota(jnp.int32, (bq, LANES), 1) < HEAD_DIM
        tot_ref[...] = jnp.concatenate([jnp.where(first, rems[h], rems[h + 1]) for h in range(0, heads, 2)], axis=1)
        cnt_ref[step_id] = n_left.astype(F32)

    qblk = lambda b, g, i: (g // per_unit, b * nq + i, g % per_unit)
    return pl.pallas_call(
        body, name="attn_fwd", grid=(n_seq, groups, nq),
        in_specs=[pl.BlockSpec((1, bq, ATT_LANES), qblk),
                  pl.BlockSpec((1, seq, ATT_LANES), lambda b, g, i: (upp + g // per_unit, b, g % per_unit)),
                  pl.BlockSpec((1, seq, ATT_LANES), lambda b, g, i: (2 * upp + g // per_unit, b, g % per_unit)),
                  pl.BlockSpec((2 * KBLK, 2 * KBLK), lambda b, g, i: (0, 0))],
        out_specs=[pl.BlockSpec((1, bq, ATT_LANES), qblk),
                   pl.BlockSpec((bq, ATT_LANES), lambda b, g, i: (b * nq + i, g)),
                   pl.BlockSpec(memory_space=pltpu.SMEM)],
        out_shape=[jax.ShapeDtypeStruct((upp, t, 2 * LANES), BF16), jax.ShapeDtypeStruct((t, upp * 2 * LANES), F32),
                   jax.ShapeDtypeStruct((n_seq * groups * nq,), F32)],
        compiler_params=_cparams(("arbitrary", "arbitrary", "arbitrary")),
    )(qkv, qkv, qkv, suffix_m)


def attn_bwd(qkv, do, tot, cnt, n_seq, seq):
    t = qkv.shape[1]
    upp, bq, nq, nsub, per_unit, groups, heads = _att_geometry(qkv, seq)
    _, prefix_m = _sum_matrices()
    scale = HEAD_DIM ** -0.5

    def body(q_ref, k_ref, v_ref, do_ref, tot_ref, m_ref, cnt_ref, dq_ref, dk_ref, dv_ref, dk_acc, dv_acc):
        qi = pl.program_id(2)
        step_id = (pl.program_id(0) * groups + pl.program_id(1)) * nq + qi
        n_left = jnp.clip(cnt_ref[step_id].astype(jnp.int32), 0, qi)
        in_head = _head_lanes(bq, heads)
        only = lambda v, h: jnp.where(in_head[h], v, jnp.zeros_like(v))
        q_all = q_ref[0] * jnp.asarray(scale, BF16)
        do_all = do_ref[0]
        qs = [only(q_all, h) for h in range(heads)]
        dos = [only(do_all, h) for h in range(heads)]
        first = lax.broadcasted_iota(jnp.int32, (bq, LANES), 1) < HEAD_DIM
        tots = []
        for h in range(0, heads, 2):
            both = tot_ref[:, h // 2 * LANES:(h // 2 + 1) * LANES]
            swapped = pltpu.roll(both, HEAD_DIM, 1)
            tots += [jnp.where(first, both, swapped), jnp.where(first, swapped, both)]
        m_ext = m_ref[...]
        row = lax.broadcasted_iota(jnp.int32, (bq, bq), 0)
        col = lax.broadcasted_iota(jnp.int32, (bq, bq), 1)
        diag_mask = col < row

        @pl.when(qi == 0)
        def _():
            dk_acc[...] = jnp.zeros_like(dk_acc)
            dv_acc[...] = jnp.zeros_like(dv_acc)

        def block(kj, carry, mask):
            off = pl.multiple_of(kj * bq, bq)
            k_all = k_ref[0, pl.ds(off, bq), :]
            v_all = v_ref[0, pl.ds(off, bq), :]
            pres, gpres, dq = carry
            dk_part = jnp.zeros((bq, ATT_LANES), F32)
            dv_part = jnp.zeros((bq, ATT_LANES), F32)
            pres_out, gpres_out = [], []
            for h in range(heads):
                pre, gpre = pres[h], gpres[h]
                z = _nt(qs[h], k_all)
                if mask is not None:
                    z = jnp.where(mask, z, MASKED)
                sp, ls = _softplus_parts(z)
                sums = _keysums(-sp, m_ext)
                parts = []
                for j in range(nsub):
                    pin, ptot = sums[j]
                    parts.append(jnp.exp(ls[:, j * KBLK:(j + 1) * KBLK] + (tots[h] - (pre + pin))))
                    pre = pre + ptot
                a = jnp.concatenate(parts, axis=1)
                g = a * _nt(dos[h], v_all)
                gsums = _keysums(g, m_ext)
                parts = []
                for j in range(nsub):
                    gin, gtot = gsums[j]
                    parts.append(gpre + gin)
                    gpre = gpre + gtot
                dz = g - jnp.exp(ls) * jnp.concatenate(parts, axis=1)
                dzb = dz.astype(BF16)
                dq = dq + _nn(dzb, only(k_all, h))
                dk_part = dk_part + _tn(dzb, qs[h])
                dv_part = dv_part + _tn(a.astype(BF16), dos[h])
                pres_out.append(pre)
                gpres_out.append(gpre)
            dk_acc[pl.ds(off, bq), :] += dk_part
            dv_acc[pl.ds(off, bq), :] += dv_part
            return tuple(pres_out), tuple(gpres_out), dq

        zero = jnp.zeros((bq, LANES), F32)
        carry = ((zero,) * heads, (zero,) * heads, jnp.zeros((bq, ATT_LANES), F32))
        carry = lax.fori_loop(qi - n_left, qi, lambda kj, c: block(kj, c, None), carry)
        carry = block(qi, carry, diag_mask)
        dq_ref[0] = (carry[2] * scale).astype(BF16)

        @pl.when(qi == nq - 1)
        def _():
            dk_ref[0] = dk_acc[...].astype(BF16)
            dv_ref[0] = dv_acc[...].astype(BF16)

    qblk = lambda b, g, i: (g // per_unit, b * nq + i, g % per_unit)
    kv_out = pl.BlockSpec((1, seq, ATT_LANES), lambda b, g, i: (g // per_unit, b, g % per_unit))
    shp = jax.ShapeDtypeStruct((upp, t, 2 * LANES), BF16)
    return pl.pallas_call(
        body, name="attn_bwd", grid=(n_seq, groups, nq),
        in_specs=[pl.BlockSpec((1, bq, ATT_LANES), qblk),
                  pl.BlockSpec((1, seq, ATT_LANES), lambda b, g, i: (upp + g // per_unit, b, g % per_unit)),
                  pl.BlockSpec((1, seq, ATT_LANES), lambda b, g, i: (2 * upp + g // per_unit, b, g % per_unit)),
                  pl.BlockSpec((1, bq, ATT_LANES), qblk),
                  pl.BlockSpec((bq, ATT_LANES), lambda b, g, i: (b * nq + i, g)),
                  pl.BlockSpec((2 * KBLK, 2 * KBLK), lambda b, g, i: (0, 0)),
                  pl.BlockSpec(memory_space=pltpu.SMEM)],
        out_specs=[pl.BlockSpec((1, bq, ATT_LANES), qblk), kv_out, kv_out],
        out_shape=[shp, shp, shp],
        scratch_shapes=[pltpu.VMEM((seq, ATT_LANES), F32), pltpu.VMEM((seq, ATT_LANES), F32)],
        compiler_params=_cparams(("parallel", "parallel", "arbitrary")),
    )(qkv, qkv, qkv, do, tot, prefix_m, cnt)


def _ln_stats(v):
    mu = jnp.mean(v, axis=-1, keepdims=True)
    vc = v - mu
    rstd = lax.rsqrt(jnp.mean(vc * vc, axis=-1, keepdims=True) + EPS)
    return vc * rstd, rstd


def _glu_into(a0_ref, av_ref, ag_ref, hv_ref, hg_ref, first):
    hv = hv_ref[0].astype(F32)
    hg = hg_ref[0].astype(F32)
    a0_ref[0:HALO, :] = jnp.where(first, 0.0, hv * _sigmoid(hg))
    av = av_ref[0].astype(F32)
    ag = ag_ref[0].astype(F32)
    a0_ref[HALO:, :] = av * _sigmoid(ag)


def _shifted_taps(ref, shifted_ref, tm, first):
    taps = []
    for b in range(8):
        offs = [o for o in range(first, first + CONV_WIDTH) if o % 8 == b]
        n_rows = max(offs) - b + tm
        shifted_ref[b, 0:n_rows, :] = ref[pl.ds(b, n_rows), :]
        taps += [(b, o - b, o - first) for o in offs]
    return taps


def _tril_mask():
    r = lax.broadcasted_iota(jnp.int32, (CHUNK, CHUNK), 0)
    c = lax.broadcasted_iota(jnp.int32, (CHUNK, CHUNK), 1)
    return c <= r


def mix_fwd(z, conv_w, conv_b, ln_a_g, ln_a_b, ln_v_g, ln_v_b, sp_w, sp_bt, seq):
    _, t, c = z.shape
    tm = _tile(seq, 512)
    tiles_per_seq = seq // tm
    groups = c // LANES
    hb = tm // HALO

    def body(av_ref, ag_ref, u_ref, v_ref, hv_ref, hg_ref, cw_ref, cb_ref, lag_ref, lab_ref, lvg_ref, lvb_ref,
             spw_ref, spb_ref, cat_ref, a1_ref, a0_ref, sh_ref):
        i = pl.program_id(0)
        _glu_into(a0_ref, av_ref, ag_ref, hv_ref, hg_ref, i % tiles_per_seq == 0)
        acc = jnp.zeros((tm, c), F32) + cb_ref[...]
        for b, ro, k in _shifted_taps(a0_ref, sh_ref, tm, HALO - (CONV_WIDTH - 1)):
            acc = acc + cw_ref[k:k + 1, :] * sh_ref[b, pl.ds(ro, tm), :]
        a1_ref[...] = acc
        xh, _ = _ln_stats(acc)
        a2 = xh * lag_ref[...] + lab_ref[...]
        a3 = (a2 * _sigmoid(a2)).astype(BF16)
        half = c // 2
        cat_ref[0] = a3[:, :half]
        cat_ref[1] = a3[:, half:]
        tril = _tril_mask()
        for g in range(groups):
            sl = slice(g * LANES, (g + 1) * LANES)
            xh, _ = _ln_stats(v_ref[0][:, sl].astype(F32))
            vn = (xh * lvg_ref[:, sl] + lvb_ref[:, sl]).astype(BF16)
            w = jnp.where(tril, spw_ref[g], 0.0).astype(BF16)
            bias = spb_ref[:, g:g + 1]
            for ch in range(tm // CHUNK):
                rows = slice(ch * CHUNK, (ch + 1) * CHUNK)
                vs = _nn(w, vn[rows]) + bias
                bo = (u_ref[0][rows, sl].astype(F32) * vs).astype(BF16)
                cat_ref[2 + (g * LANES) // half, rows, (g * LANES) % half:(g * LANES) % half + LANES] = bo

    unit = lambda u: pl.BlockSpec((1, tm, c), lambda i: (u, i, 0))
    halo = lambda u: pl.BlockSpec((1, HALO, c), lambda i: (u, jnp.maximum(i * hb - 1, 0), 0))
    vec = pl.BlockSpec((1, c), lambda i: (0, 0))
    return pl.pallas_call(
        body, name="mix_fwd", grid=(t // tm,),
        in_specs=[unit(0), unit(1), unit(2), unit(3), halo(0), halo(1),
                  pl.BlockSpec((CONV_WIDTH, c), lambda i: (0, 0)), vec, vec, vec, vec, vec,
                  pl.BlockSpec((groups, CHUNK, CHUNK), lambda i: (0, 0, 0)),
                  pl.BlockSpec((CHUNK, groups), lambda i: (0, 0))],
        out_specs=[pl.BlockSpec((4, tm, c // 2), lambda i: (0, i, 0)), pl.BlockSpec((tm, c), lambda i: (i, 0))],
        out_shape=[jax.ShapeDtypeStruct((4, t, c // 2), BF16), jax.ShapeDtypeStruct((t, c), F32)],
        scratch_shapes=[pltpu.VMEM((HALO + tm, c), F32), pltpu.VMEM((8, HALO + tm, c), F32)],
        compiler_params=_cparams(("parallel",)),
    )(z, z, z, z, z, z, conv_w, conv_b, ln_a_g, ln_a_b, ln_v_g, ln_v_b, sp_w, sp_bt)


def mix_bwd_point(dcat, z, a1, ln_a_g, ln_a_b, ln_v_g, ln_v_b, sp_w, sp_wt, sp_bt, seq):
    _, t, c = z.shape
    tm = _tile(seq, 512)
    groups = c // LANES
    half = c // 2

    def body(dc_ref, u_ref, v_ref, a1_ref, lag_ref, lab_ref, lvg_ref, lvb_ref, spw_ref, spwt_ref, spb_ref,
             dz_ref, da1_ref, dcb_ref, dlag_ref, dlab_ref, dlvg_ref, dlvb_ref, dspw_ref, dspb_ref):
        i = pl.program_id(0)
        last = pl.num_programs(0) - 1

        @pl.when(i == 0)
        def _():
            for r in (dcb_ref, dlag_ref, dlab_ref, dlvg_ref, dlvb_ref, dspw_ref, dspb_ref):
                r[...] = jnp.zeros_like(r)

        da3 = jnp.concatenate([dc_ref[0], dc_ref[1]], axis=-1)
        xh, rstd = _ln_stats(a1_ref[...])
        a2 = xh * lag_ref[...] + lab_ref[...]
        s = _sigmoid(a2)
        da2 = da3 * (s * (1.0 + a2 * (1.0 - s)))
        dlag_ref[...] += jnp.sum(da2 * xh, axis=0, keepdims=True)
        dlab_ref[...] += jnp.sum(da2, axis=0, keepdims=True)
        dxh = da2 * lag_ref[...]
        da1 = rstd * (dxh - jnp.mean(dxh, axis=-1, keepdims=True) - xh * jnp.mean(dxh * xh, axis=-1, keepdims=True))
        da1_ref[...] = da1
        dcb_ref[...] += jnp.sum(da1, axis=0, keepdims=True)

        tril = _tril_mask()
        for g in range(groups):
            sl = slice(g * LANES, (g + 1) * LANES)
            xh, rstd = _ln_stats(v_ref[0][:, sl].astype(F32))
            lg = lvg_ref[:, sl]
            vnb = (xh * lg + lvb_ref[:, sl]).astype(BF16)
            w = jnp.where(tril, spw_ref[g], 0.0).astype(BF16)
            wt = jnp.where(tril.T, spwt_ref[g], 0.0).astype(BF16)
            bias = spb_ref[:, g:g + 1]
            dbo_all = dc_ref[2 + (g * LANES) // half][:, (g * LANES) % half:(g * LANES) % half + LANES]
            dvn_parts = []
            dw_acc = jnp.zeros((CHUNK, CHUNK), F32)
            db_acc = jnp.zeros((CHUNK, LANES), F32)
            for ch in range(tm // CHUNK):
                rows = slice(ch * CHUNK, (ch + 1) * CHUNK)
                vs = _nn(w, vnb[rows]) + bias
                dbo = dbo_all[rows]
                uv = u_ref[0][rows, sl].astype(F32)
                dz_ref[0, rows, sl] = (dbo * vs).astype(BF16)
                dvs = dbo * uv
                dvsb = dvs.astype(BF16)
                dvn_parts.append(_nn(wt, dvsb))
                dw_acc = dw_acc + _nt(dvsb, vnb[rows])
                db_acc = db_acc + dvs
            dvn = jnp.concatenate(dvn_parts, axis=0)
            dspw_ref[g] += jnp.where(tril, dw_acc, 0.0)
            dspb_ref[g] += db_acc
            dlvg_ref[:, sl] += jnp.sum(dvn * xh, axis=0, keepdims=True)
            dlvb_ref[:, sl] += jnp.sum(dvn, axis=0, keepdims=True)
            dxh = dvn * lg
            dv = rstd * (dxh - jnp.mean(dxh, axis=-1, keepdims=True) - xh * jnp.mean(dxh * xh, axis=-1, keepdims=True))
            dz_ref[1, :, sl] = dv.astype(BF16)

        @pl.when(i == last)
        def _():
            for g in range(groups):
                dspb_ref[g] = jnp.zeros((CHUNK, LANES), F32) + jnp.sum(dspb_ref[g], axis=-1, keepdims=True)

    unit = lambda u: pl.BlockSpec((1, tm, c), lambda i: (u, i, 0))
    vec = pl.BlockSpec((1, c), lambda i: (0, 0))
    sq = pl.BlockSpec((groups, CHUNK, CHUNK), lambda i: (0, 0, 0))
    vshape = jax.ShapeDtypeStruct((1, c), F32)
    sshape = jax.ShapeDtypeStruct((groups, CHUNK, CHUNK), F32)
    return pl.pallas_call(
        body, name="mix_bwd_point", grid=(t // tm,),
        in_specs=[pl.BlockSpec((4, tm, half), lambda i: (0, i, 0)), unit(2), unit(3),
                  pl.BlockSpec((tm, c), lambda i: (i, 0)), vec, vec, vec, vec, sq, sq,
                  pl.BlockSpec((CHUNK, groups), lambda i: (0, 0))],
        out_specs=[pl.BlockSpec((2, tm, c), lambda i: (1, i, 0)), pl.BlockSpec((tm, c), lambda i: (i, 0)),
                   vec, vec, vec, vec, vec, sq, sq],
        out_shape=[jax.ShapeDtypeStruct((4, t, c), BF16), jax.ShapeDtypeStruct((t, c), F32),
                   vshape, vshape, vshape, vshape, vshape, sshape, sshape],
        compiler_params=_cparams(("arbitrary",)),
    )(dcat, z, z, a1, ln_a_g, ln_a_b, ln_v_g, ln_v_b, sp_w, sp_wt, sp_bt)


def mix_bwd_conv(dz, da1, z, conv_w, seq):
    _, t, c = z.shape
    tm = _tile(seq, 512)
    tiles_per_seq = seq // tm
    hb = tm // HALO
    n_halo_blocks = t // HALO

    rc = _tile(tm, CONV_ROWS)

    def body(dz_in_ref, d_ref, dh_ref, av_ref, ag_ref, cw_ref, dz_ref, dcw_ref, d1_ref, sh_ref, part_ref):
        del dz_in_ref
        i = pl.program_id(0)

        @pl.when(i == 0)
        def _():
            part_ref[...] = jnp.zeros_like(part_ref)

        d1_ref[0:tm, :] = d_ref[...]
        d1_ref[tm:, :] = jnp.where((i + 1) % tiles_per_seq == 0, 0.0, dh_ref[...])
        taps = _shifted_taps(d1_ref, sh_ref, tm, 0)

        def chunk(ci, carry):
            r0 = pl.multiple_of(ci * rc, rc)
            av = av_ref[0, pl.ds(r0, rc), :].astype(F32)
            s = _sigmoid(ag_ref[0, pl.ds(r0, rc), :].astype(F32))
            a0 = av * s
            da0 = jnp.zeros((rc, c), F32)
            for b, ro, back in taps:
                k = CONV_WIDTH - 1 - back
                rows = sh_ref[b, pl.ds(r0 + ro, rc), :]
                da0 = da0 + cw_ref[k:k + 1, :] * rows
                prod = a0 * rows
                part_ref[k] += functools.reduce(lambda p, q: p + q, [prod[8 * r:8 * r + 8] for r in range(rc // 8)])
            dz_ref[0, pl.ds(r0, rc), :] = (da0 * s).astype(BF16)
            dz_ref[1, pl.ds(r0, rc), :] = (da0 * av * s * (1.0 - s)).astype(BF16)
            return carry

        lax.fori_loop(0, tm // rc, chunk, 0)

        @pl.when(i == pl.num_programs(0) - 1)
        def _():
            dcw_ref[...] = jnp.sum(part_ref[...], axis=1)

    unit = lambda u: pl.BlockSpec((1, tm, c), lambda i: (u, i, 0))
    return pl.pallas_call(
        body, name="mix_bwd_conv", grid=(t // tm,),
        in_specs=[pl.BlockSpec(memory_space=pl.ANY), pl.BlockSpec((tm, c), lambda i: (i, 0)),
                  pl.BlockSpec((HALO, c), lambda i: (jnp.minimum((i + 1) * hb, n_halo_blocks - 1), 0)),
                  unit(0), unit(1), pl.BlockSpec((CONV_WIDTH, c), lambda i: (0, 0))],
        out_specs=[pl.BlockSpec((2, tm, c), lambda i: (0, i, 0)), pl.BlockSpec((CONV_WIDTH, c), lambda i: (0, 0))],
        out_shape=[jax.ShapeDtypeStruct(dz.shape, BF16), jax.ShapeDtypeStruct((CONV_WIDTH, c), F32)],
        scratch_shapes=[pltpu.VMEM((tm + HALO, c), F32), pltpu.VMEM((8, tm + HALO, c), F32),
                        pltpu.VMEM((CONV_WIDTH, 8, c), F32)],
        input_output_aliases={0: 0},
        compiler_params=_cparams(("arbitrary",)),
    )(dz, da1, da1, z, z, conv_w)


CHIP_FLIPS = ((1, 0), (0, 1), (1, 1))
ANY = pl.BlockSpec(memory_space=pl.ANY)


def _place():
    return lax.axis_index("x"), lax.axis_index("y"), lax.axis_index("c")


def _flip(v, f):
    return 1 - v if f else v


def place_shard(w, chip, dtype, name):
    n_layers, r, cc = w.shape
    rb = _tile(r, 512)

    def body(chip_ref, w_ref, *o_refs):
        del chip_ref
        for layer, o_ref in enumerate(o_refs):
            o_ref[0] = w_ref[layer].astype(dtype)

    return pl.pallas_call(
        body, name=name,
        grid_spec=pltpu.PrefetchScalarGridSpec(
            num_scalar_prefetch=1, grid=(r // rb,),
            in_specs=[pl.BlockSpec((n_layers, rb, cc), lambda i, chip_ref: (0, i, 0))],
            out_specs=[pl.BlockSpec((1, rb, cc), lambda i, chip_ref: (chip_ref[0], i, 0))] * n_layers),
        out_shape=[jax.ShapeDtypeStruct((N_CHIPS, r, cc), dtype)] * n_layers,
        compiler_params=_cparams(("parallel",)),
    )(chip, w)


class Carry:
    def __init__(self, arrays, out_shapes, aliased, sem_shapes, start, finish):
        self.arrays, self.out_shapes, self.aliased, self.sem_shapes = list(arrays), list(out_shapes), aliased, list(sem_shapes)
        self.start, self.finish = start, finish


def _call(body, *, name, grid, in_specs, out_specs, out_shape, args, sem, scratch_shapes=(), carry=None):
    if carry is None:
        res = pl.pallas_call(body, name=name, grid=grid, in_specs=in_specs, out_specs=out_specs, out_shape=out_shape,
                             scratch_shapes=list(scratch_shapes), compiler_params=_cparams(sem))(*args)
        return list(res), []
    n_in, n_out, n_scr, nc = len(args), len(out_shape), len(scratch_shapes), len(carry.arrays)

    def full_body(*refs):
        ins, refs = refs[:n_in], refs[n_in:]
        c_ins, refs = refs[:nc], refs[nc:]
        outs, refs = refs[:n_out], refs[n_out:]
        c_outs, refs = refs[:nc], refs[nc:]
        scr, sems = refs[:n_scr], refs[n_scr:]
        first = functools.reduce(lambda a, b: a & b, [pl.program_id(d) == 0 for d in range(len(grid))])
        last = functools.reduce(lambda a, b: a & b, [pl.program_id(d) == grid[d] - 1 for d in range(len(grid))])

        @pl.when(first)
        def _():
            carry.start(c_ins, c_outs, sems)

        body(*ins, *outs, *scr)

        @pl.when(last)
        def _():
            carry.finish(c_ins, c_outs, sems)

    res = pl.pallas_call(
        full_body, name=name, grid=grid, in_specs=list(in_specs) + [ANY] * nc, out_specs=list(out_specs) + [ANY] * nc,
        out_shape=list(out_shape) + carry.out_shapes, scratch_shapes=list(scratch_shapes) + carry.sem_shapes,
        input_output_aliases={n_in + i: n_out + i for i in range(nc)} if carry.aliased else {},
        compiler_params=pltpu.CompilerParams(dimension_semantics=("arbitrary",) * len(grid), vmem_limit_bytes=VMEM_LIMIT,
                                             has_side_effects=True),
    )(*args, *carry.arrays)
    return list(res[:n_out]), list(res[n_out:])


def _gather_ops(shapes, whole):
    n = len(shapes)

    def rows(a, c):
        hr = shapes[a][1] // 2
        return pl.ds(pl.multiple_of(c * hr, 16), hr)

    def start(ins, outs, sems):
        ici_send, ici_recv = sems[0], sems[1]
        x, y, c = _place()
        k = 2 * x + y
        for a in range(n):
            for o, (fx, fy) in enumerate(CHIP_FLIPS):
                src = ins[a].at[k] if whole[a] else ins[a].at[k, rows(a, c)]
                dst = outs[a].at[k] if whole[a] else outs[a].at[k, rows(a, c)]
                pltpu.make_async_remote_copy(
                    src_ref=src, dst_ref=dst, send_sem=ici_send.at[3 * a + o], recv_sem=ici_recv.at[3 * a + o],
                    device_id=(_flip(x, fx), _flip(y, fy), c), device_id_type=MESH).start()

    def finish(ins, outs, sems):
        ici_send, ici_recv, d2d_send, d2d_recv = sems
        x, y, c = _place()
        k = 2 * x + y
        sibling = (x, y, 1 - c)

        def copy(ref, send, recv, a, o):
            return pltpu.make_async_remote_copy(src_ref=ref, dst_ref=ref, send_sem=send.at[3 * a + o],
                                                recv_sem=recv.at[3 * a + o], device_id=sibling, device_id_type=MESH)

        for a in range(n):
            for o, (fx, fy) in enumerate(CHIP_FLIPS):
                kk = 2 * _flip(x, fx) + _flip(y, fy)
                landed = outs[a].at[kk] if whole[a] else outs[a].at[kk, rows(a, c)]
                copy(landed, ici_send, ici_recv, a, o).wait_recv()
                if not whole[a]:
                    copy(landed, d2d_send, d2d_recv, a, o).start()
        for a in range(n):
            for o, (fx, fy) in enumerate(CHIP_FLIPS):
                kk = 2 * _flip(x, fx) + _flip(y, fy)
                mine = ins[a].at[k] if whole[a] else ins[a].at[k, rows(a, c)]
                copy(mine, ici_send, ici_recv, a, o).wait_send()
                if not whole[a]:
                    copy(outs[a].at[kk, rows(a, 1 - c)], d2d_send, d2d_recv, a, o).wait_recv()
                    copy(outs[a].at[kk, rows(a, c)], d2d_send, d2d_recv, a, o).wait_send()

    dma = pltpu.SemaphoreType.DMA
    return start, finish, [dma((3 * n,))] * 4


def gather_carry(bufs):
    start, finish, sems = _gather_ops([b.shape for b in bufs], [False] * len(bufs))
    return Carry(bufs, [jax.ShapeDtypeStruct(b.shape, b.dtype) for b in bufs], True, sems, start, finish)


def allgather_weights(shards, smalls):
    bufs = list(shards) + list(smalls)
    n = len(bufs)
    start, finish, sems = _gather_ops([b.shape for b in bufs], [False] * len(shards) + [True] * len(smalls))

    def body(*refs):
        start(refs[:n], refs[n:2 * n], refs[2 * n:])
        finish(refs[:n], refs[n:2 * n], refs[2 * n:])

    res = pl.pallas_call(
        body, name="allgather_weights", in_specs=[ANY] * n, out_specs=[ANY] * n,
        out_shape=[jax.ShapeDtypeStruct(b.shape, b.dtype) for b in bufs], scratch_shapes=sems,
        input_output_aliases={i: i for i in range(n)},
        compiler_params=pltpu.CompilerParams(has_side_effects=True),
    )(*bufs)
    return res[:len(shards)], res[len(shards):]


def rs_exchange(grads):
    n = len(grads)

    def body(*refs):
        ins, outs = refs[:n], refs[n:2 * n]
        send, recv = refs[2 * n:]
        x, y, c = _place()
        cps = []
        for a in range(n):
            cp = pltpu.make_async_remote_copy(
                src_ref=ins[a].at[:, 1 - c], dst_ref=outs[a], send_sem=send.at[a], recv_sem=recv.at[a],
                device_id=(x, y, 1 - c), device_id_type=MESH)
            cp.start()
            cps.append(cp)
        for cp in cps:
            cp.wait()

    dma = pltpu.SemaphoreType.DMA
    return pl.pallas_call(
        body, name="rs_exchange", in_specs=[ANY] * n, out_specs=[ANY] * n,
        out_shape=[jax.ShapeDtypeStruct((g.shape[0],) + g.shape[2:], g.dtype) for g in grads],
        scratch_shapes=[dma((n,)), dma((n,))],
        compiler_params=pltpu.CompilerParams(has_side_effects=True),
    )(*grads)


def rs_add(gs, sibs, core, out_dtype, name):
    n = len(gs)
    nk = gs[0].shape[0]

    def body(core_ref, *refs):
        del core_ref
        for a in range(n):
            refs[2 * n + a][0] = (refs[a][0, 0] + refs[n + a][0]).astype(out_dtype)

    halves = [g.shape[2:] for g in gs]
    return pl.pallas_call(
        body, name=name,
        grid_spec=pltpu.PrefetchScalarGridSpec(
            num_scalar_prefetch=1, grid=(nk,),
            in_specs=[pl.BlockSpec((1, 1) + h, lambda k, core_ref: (k, core_ref[0], 0, 0)) for h in halves]
            + [pl.BlockSpec((1,) + h, lambda k, core_ref: (k, 0, 0)) for h in halves],
            out_specs=[pl.BlockSpec((1,) + h, lambda k, core_ref: (k, 0, 0)) for h in halves]),
        out_shape=[jax.ShapeDtypeStruct((nk,) + h, out_dtype) for h in halves],
        compiler_params=_cparams(("parallel",)),
    )(core, *gs, *sibs)


def send_carry(parts):
    n = len(parts)

    def copies(ins, outs, sems):
        x, y, c = _place()
        for a in range(n):
            for o, (fx, fy) in enumerate(CHIP_FLIPS):
                kk = 2 * _flip(x, fx) + _flip(y, fy)
                yield pltpu.make_async_remote_copy(
                    src_ref=ins[a].at[kk], dst_ref=outs[a].at[o], send_sem=sems[0].at[3 * a + o],
                    recv_sem=sems[1].at[3 * a + o], device_id=(_flip(x, fx), _flip(y, fy), c), device_id_type=MESH)

    def start(ins, outs, sems):
        for cp in copies(ins, outs, sems):
            cp.start()

    def finish(ins, outs, sems):
        for cp in copies(ins, outs, sems):
            cp.wait()

    dma = pltpu.SemaphoreType.DMA
    return Carry(parts, [jax.ShapeDtypeStruct((3,) + p.shape[1:], p.dtype) for p in parts], False,
                 [dma((3 * n,)), dma((3 * n,))], start, finish)


def rs_sum(recvs, parts, where, name):
    n_layers = len(recvs)
    _, hr, cc = recvs[0].shape
    rb = _tile(hr, 256)

    def body(where_ref, *refs):
        del where_ref
        o_ref = refs[-1]
        for layer in range(n_layers):
            r_ref, p_ref = refs[layer], refs[n_layers + layer]
            o_ref[layer, 0] = ((p_ref[0].astype(F32) + r_ref[0].astype(F32)) + r_ref[1].astype(F32)) + r_ref[2].astype(F32)

    return pl.pallas_call(
        body, name=name,
        grid_spec=pltpu.PrefetchScalarGridSpec(
            num_scalar_prefetch=1, grid=(hr // rb,),
            in_specs=[pl.BlockSpec((3, rb, cc), lambda i, w_ref: (0, i, 0))] * n_layers
            + [pl.BlockSpec((1, rb, cc), lambda i, w_ref: (w_ref[0], i, 0))] * n_layers,
            out_specs=pl.BlockSpec((n_layers, 1, rb, cc), lambda i, w_ref: (0, w_ref[1], i, 0))),
        out_shape=jax.ShapeDtypeStruct((n_layers, 2, hr, cc), F32),
        compiler_params=_cparams(("parallel",)),
    )(where, *recvs, *parts)


def rs_share(fulls):
    n = len(fulls)

    def body(*refs):
        ins, outs = refs[:n], refs[n:2 * n]
        send, recv = refs[2 * n:]
        x, y, c = _place()
        cps = []
        for a in range(n):
            cp = pltpu.make_async_remote_copy(
                src_ref=ins[a].at[:, c], dst_ref=outs[a].at[:, c], send_sem=send.at[a], recv_sem=recv.at[a],
                device_id=(x, y, 1 - c), device_id_type=MESH)
            cp.start()
            cps.append(cp)
        for a in range(n):
            got = outs[a].at[:, 1 - c]
            pltpu.make_async_remote_copy(
                src_ref=got, dst_ref=got, send_sem=send.at[a], recv_sem=recv.at[a],
                device_id=(x, y, 1 - c), device_id_type=MESH).wait_recv()
        for cp in cps:
            cp.wait_send()

    dma = pltpu.SemaphoreType.DMA
    return pl.pallas_call(
        body, name="rs_share", in_specs=[ANY] * n, out_specs=[ANY] * n,
        out_shape=[jax.ShapeDtypeStruct(f.shape, f.dtype) for f in fulls],
        scratch_shapes=[dma((n,)), dma((n,))],
        input_output_aliases={i: i for i in range(n)},
        compiler_params=pltpu.CompilerParams(has_side_effects=True),
    )(*fulls)


def allreduce_small(v):
    r, w = v.shape

    def body(v_ref, o_ref, buf, send, recv, loc):
        x, y, c = _place()
        me = 4 * x + 2 * y + c
        mine = pltpu.make_async_copy(v_ref, buf.at[me], loc)
        mine.start()
        cps = []
        for o in range(1, N_DEV):
            fx, fy, fc = (o >> 2) & 1, (o >> 1) & 1, o & 1
            cp = pltpu.make_async_remote_copy(
                src_ref=v_ref, dst_ref=buf.at[me], send_sem=send.at[o - 1], recv_sem=recv.at[o - 1],
                device_id=(_flip(x, fx), _flip(y, fy), _flip(c, fc)), device_id_type=MESH)
            cp.start()
            cps.append(cp)
        for o in range(1, N_DEV):
            fx, fy, fc = (o >> 2) & 1, (o >> 1) & 1, o & 1
            peer = 4 * _flip(x, fx) + 2 * _flip(y, fy) + _flip(c, fc)
            pltpu.make_async_remote_copy(
                src_ref=v_ref, dst_ref=buf.at[peer], send_sem=send.at[o - 1], recv_sem=recv.at[o - 1],
                device_id=(x, y, c), device_id_type=MESH).wait_recv()
        for cp in cps:
            cp.wait_send()
        mine.wait()
        acc = buf[0]
        for d in range(1, N_DEV):
            acc = acc + buf[d]
        o_ref[...] = acc

    dma = pltpu.SemaphoreType.DMA
    vm = pl.BlockSpec(memory_space=pltpu.VMEM)
    return pl.pallas_call(
        body, name="allreduce_small", in_specs=[vm], out_specs=vm,
        out_shape=jax.ShapeDtypeStruct((r, w), F32),
        scratch_shapes=[pltpu.VMEM((N_DEV, r, w), F32), dma((N_DEV - 1,)), dma((N_DEV - 1,)), dma],
        compiler_params=pltpu.CompilerParams(has_side_effects=True, vmem_limit_bytes=VMEM_LIMIT),
    )(v)


def adamw(w, g, m, v, name):
    r, cc = w.shape
    rb = _tile(r, 256)

    def body(w_ref, g_ref, m_ref, v_ref, d_ref, nm_ref, nv_ref):
        gv = g_ref[...]
        nm = ADAM_B1 * m_ref[...] + (1.0 - ADAM_B1) * gv
        nv = ADAM_B2 * v_ref[...] + (1.0 - ADAM_B2) * (gv * gv)
        m_hat = nm / (1.0 - ADAM_B1 ** ADAM_STEP)
        v_hat = nv / (1.0 - ADAM_B2 ** ADAM_STEP)
        d_ref[...] = -ADAM_LR * (m_hat / (jnp.sqrt(v_hat) + ADAM_EPS) + ADAM_WD * w_ref[...])
        nm_ref[...] = nm
        nv_ref[...] = nv

    blk = pl.BlockSpec((rb, cc), lambda i: (i, 0))
    shp = jax.ShapeDtypeStruct((r, cc), F32)
    return pl.pallas_call(
        body, name=name, grid=(r // rb,), in_specs=[blk] * 4, out_specs=[blk] * 3, out_shape=[shp] * 3,
        compiler_params=_cparams(("parallel",)),
    )(w, g, m, v)


WEIGHTS = ['g_ffn1', 'w_ffn1_gate', 'w_ffn1_up', 'w_ffn1_down', 'g_mix', 'w_in_ab', 'conv_w', 'conv_b', 'ln_a_g',
           'ln_a_b', 'ln_v_g', 'ln_v_b', 'sp_w', 'sp_b', 'w_out_ab', 'w_qkv', 'w_o', 'g_ffn2', 'w_ffn2_gate',
           'w_ffn2_up', 'w_ffn2_down', 'g_final']
BIG = ['w_ffn1_gate', 'w_ffn1_up', 'w_ffn1_down', 'w_in_ab', 'w_out_ab', 'w_qkv', 'w_o', 'w_ffn2_gate', 'w_ffn2_up',
       'w_ffn2_down']
SMALL = ['g_ffn1', 'g_mix', 'g_ffn2', 'g_final', 'conv_b', 'ln_a_g', 'ln_a_b', 'ln_v_g', 'ln_v_b', 'sp_b', 'sp_w']
HIDDEN_MAJOR = ['w_ffn1_gate', 'w_ffn1_up', 'w_ffn2_gate', 'w_ffn2_up']


CARRY_WEIGHTS = {"ffn_gateup": 9.2e6, "ffn_down": 6.1e6, "mm_in": 5.9e6, "mm_out": 3.3e6}


def _use_order(depth):
    order = []
    for layer in range(depth):
        order += [('w_ffn1_gate', layer), ('w_ffn1_up', layer), ('w_ffn1_down', layer)]
        order += [('w_in_ab', layer // 2), ('w_out_ab', layer // 2)] if layer % 2 == 0 else [('w_qkv', layer // 2), ('w_o', layer // 2)]
        order += [('w_ffn2_gate', layer), ('w_ffn2_up', layer), ('w_ffn2_down', layer)]
    return order


def _rows(a):
    return a.reshape(-1, LANES)


def _pack(parts):
    v = jnp.concatenate([_rows(p) for p in parts], axis=0)
    pad = (-v.shape[0]) % 8
    return jnp.pad(v, ((0, pad), (0, 0)))


def _unpack(v, shapes):
    out, r = [], 0
    for s in shapes:
        n = 1
        for d in s:
            n *= d
        n //= LANES
        out.append(v[r:r + n].reshape(s))
        r += n
    return out


def kernel(x, g_ffn1, w_ffn1_gate, w_ffn1_up, w_ffn1_down, g_mix, w_in_ab, conv_w, conv_b, ln_a_g, ln_a_b, ln_v_g, ln_v_b, sp_w, sp_b, w_out_ab, w_qkv, w_o, g_ffn2, w_ffn2_gate, w_ffn2_up, w_ffn2_down, g_final, loss_target, m_g_ffn1, m_w_ffn1_gate, m_w_ffn1_up, m_w_ffn1_down, m_g_mix, m_w_in_ab, m_conv_w, m_conv_b, m_ln_a_g, m_ln_a_b, m_ln_v_g, m_ln_v_b, m_sp_w, m_sp_b, m_w_out_ab, m_w_qkv, m_w_o, m_g_ffn2, m_w_ffn2_gate, m_w_ffn2_up, m_w_ffn2_down, m_g_final, v_g_ffn1, v_w_ffn1_gate, v_w_ffn1_up, v_w_ffn1_down, v_g_mix, v_w_in_ab, v_conv_w, v_conv_b, v_ln_a_g, v_ln_a_b, v_ln_v_g, v_ln_v_b, v_sp_w, v_sp_b, v_w_out_ab, v_w_qkv, v_w_o, v_g_ffn2, v_w_ffn2_gate, v_w_ffn2_up, v_w_ffn2_down, v_g_final):
    p = dict(locals())
    for name in HIDDEN_MAJOR:
        for pre in ('', 'm_', 'v_'):
            p[pre + name] = jnp.swapaxes(p[pre + name], 1, 2)
    back = lambda name, a: jnp.swapaxes(a, 1, 2) if name in HIDDEN_MAJOR else a
    n_seq, seq, d = x.shape
    t = n_seq * seq
    depth = g_ffn1.shape[0]
    core = lax.axis_index("c")
    chip = 2 * lax.axis_index("x") + lax.axis_index("y")
    xf = x.reshape(t, d)
    target = loss_target.reshape(t, d)

    items = []
    for name in BIG:
        for layer in range(p[name].shape[0]):
            items.append((name, layer))
    chip1 = chip.reshape(1).astype(jnp.int32)
    placed = {}
    for name in BIG:
        for layer, buf in enumerate(place_shard(p[name], chip1, BF16, "place_shard")):
            placed[(name, layer)] = buf
    first = [('w_ffn1_gate', 0), ('w_ffn1_up', 0)]
    gathered, (conv_w4,) = allgather_weights([placed[it] for it in first],
                                             place_shard(conv_w, chip1, F32, "place_conv_w"))
    wt = dict(zip(first, gathered))
    waiting = [it for it in _use_order(depth) if it not in wt]

    def riders(name):
        room, take = CARRY_WEIGHTS[name], []
        for it in list(waiting):
            if placed[it].size <= room:
                room -= placed[it].size
                take.append(it)
                waiting.remove(it)
        return (take, gather_carry([placed[it] for it in take])) if take else (take, None)

    def landed(take, carried):
        wt.update(zip(take, carried))

    def weight(it):
        if it not in wt:
            waiting.remove(it)
            (wt[it],), _ = allgather_weights([placed[it]], [])
        return wt[it]

    c_mix = conv_w4.shape[2] * N_CHIPS
    conv_full = jnp.transpose(conv_w4, (1, 0, 2)).reshape(CONV_WIDTH, c_mix)
    vec = lambda a: a.reshape(1, -1)
    sp_bt = sp_b[0].T
    sp_wt = jnp.transpose(sp_w[0], (0, 2, 1))
    d_ff = w_ffn1_gate.shape[2]
    n_in = w_in_ab.shape[2]
    n_qkv = w_qkv.shape[2] // 3

    saved = []
    xc = xf
    h = rmsnorm_fwd(xc, vec(g_ffn1[0]), "norm_first")
    for layer in range(depth):
        s = {}
        for half, (gn, wn) in enumerate((('g_ffn1', 'w_ffn1'), ('g_ffn2', 'w_ffn2'))):
            if half == 1:
                s['x_mix'], s['h_mix'] = xc, h
                if layer % 2 == 0:
                    w_in = weight(('w_in_ab', layer // 2))
                    take, carry = riders("mm_in")
                    (z,), got = colmm(h, [w_in], n_in, BF16, "mm_in", carry)
                    landed(take, got)
                    cat, a1 = mix_fwd(z, conv_full, conv_b, ln_a_g, ln_a_b, vec(ln_v_g), vec(ln_v_b), sp_w[0], sp_bt, seq)
                    s.update(z=z, cat=cat, a1=a1)
                    w_out = weight(('w_out_ab', layer // 2))
                    take, carry = riders("mm_out")
                    (xc, h), got = rowmm(cat, w_out, xc, 1.0, "mm_out", carry, vec(g_ffn2[layer]))
                    landed(take, got)
                else:
                    (qkv,), _ = colmm(h, [weight(('w_qkv', layer // 2))], n_qkv, BF16, "mm_qkv")
                    o, tot, cnt = attn_fwd(qkv, n_seq, seq)
                    s.update(qkv=qkv, o=o, tot=tot, cnt=cnt)
                    (xc, h), _ = rowmm(o, weight(('w_o', layer // 2)), xc, 1.0, "mm_o", None, vec(g_ffn2[layer]))
            s['x' + wn] = xc
            w_gate, w_up = weight((wn + '_gate', layer)), weight((wn + '_up', layer))
            take, carry = riders("ffn_gateup")
            (silu, udsilu, act), got = colmm(h, [w_gate, w_up], d_ff, BF16, "ffn_gateup", carry, swiglu=True)
            landed(take, got)
            s.update({'h' + wn: h, 'swiglu' + wn: (silu, udsilu), 'act' + wn: act})
            w_down = weight((wn + '_down', layer))
            take, carry = riders("ffn_down")
            following = g_mix[layer] if half == 0 else (g_ffn1[layer + 1] if layer + 1 < depth else None)
            (xc, h), got = rowmm(act, w_down, xc, 0.5, "ffn_down", carry, None if following is None else vec(following))
            landed(take, got)
        saved.append(s)

    loss8, dx, dxb, dg_final = loss_head(xc, vec(g_final), target)
    loss = lax.psum(loss8[0, 0], ("x", "y", "c"))

    gw = {}
    gs = {}
    core1 = core.reshape(1).astype(jnp.int32)
    ready = []
    part, recv = {}, {}

    def leaving():
        its = list(ready)
        ready.clear()
        halves = lambda a: a.reshape(N_CHIPS, 2, a.shape[1] // 2, a.shape[2])
        theirs = rs_exchange([halves(gw[it][1]) for it in its])
        sums = rs_add([halves(gw[it][0]) for it in its], theirs, core1, REDUCE_DTYPE, "rs_add")
        part.update(zip(its, sums))
        return its, send_carry(sums)

    for layer in reversed(range(depth)):
        s = saved[layer]
        for half, (gn, wn) in reversed(list(enumerate((('g_ffn1', 'w_ffn1'), ('g_ffn2', 'w_ffn2'))))):
            wd = wt[(wn + '_down', layer)]
            dgate, dup = rowmm_t(dxb, wd, 0.5, BF16, "ffn_bwd_act", swiglu=s['swiglu' + wn])
            gw[(wn + '_down', layer)] = dw_row(s['act' + wn], dxb, 0.5, "ffn_dw_down")
            gw[(wn + '_gate', layer)], gw[(wn + '_up', layer)] = dw_col(s['h' + wn], [dgate, dup], N_CHIPS, d_ff,
                                                                        "ffn_dw_gateup", transposed=True)
            ready.extend([(wn + '_down', layer), (wn + '_gate', layer), (wn + '_up', layer)])
            its, carry = leaving()
            (dx, dxb, dg), got = colmm_t([dgate, dup], [wt[(wn + '_gate', layer)], wt[(wn + '_up', layer)]], d_ff,
                                         s['x' + wn], vec(p[gn][layer]), dx, "ffn_bwd_in", carry, transposed=True)
            recv.update(zip(its, got))
            gs[(gn, layer)] = dg
            if half == 1:
                if layer % 2 == 0:
                    i = layer // 2
                    w_out = wt[('w_out_ab', i)]
                    dcat = rowmm_t(dxb, w_out, 1.0, F32, "mm_out_t")
                    gw[('w_out_ab', i)] = dw_row(s['cat'], dxb, 1.0, "dw_out")
                    dz, da1, dcb, dlag, dlab, dlvg, dlvb, dspw, dspb = mix_bwd_point(
                        dcat, s['z'], s['a1'], ln_a_g, ln_a_b, vec(ln_v_g), vec(ln_v_b), sp_w[0], sp_wt, sp_bt, seq)
                    dz, dcw = mix_bwd_conv(dz, da1, s['z'], conv_full, seq)
                    gs.update({('conv_b', i): dcb, ('ln_a_g', i): dlag, ('ln_a_b', i): dlab, ('ln_v_g', i): dlvg,
                               ('ln_v_b', i): dlvb, ('sp_w', i): dspw, ('sp_b', i): dspb[:, :, 0], ('conv_w', i): dcw})
                    (gw[('w_in_ab', i)],) = dw_col(s['h_mix'], [dz], N_CHIPS, n_in, "dw_in")
                    ready.extend([('w_out_ab', i), ('w_in_ab', i)])
                    its, carry = leaving()
                    (dx, dxb, dg), got = colmm_t([dz], [wt[('w_in_ab', i)]], n_in, s['x_mix'], vec(g_mix[layer]), dx,
                                                 "mm_in_t", carry)
                    recv.update(zip(its, got))
                else:
                    i = layer // 2
                    w_o4 = wt[('w_o', i)]
                    do = rowmm_t(dxb, w_o4, 1.0, BF16, "mm_o_t")
                    gw[('w_o', i)] = dw_row(s['o'], dxb, 1.0, "dw_o")
                    dq, dk, dv = attn_bwd(s['qkv'], do, s['tot'], s['cnt'], n_seq, seq)
                    dqkv = jnp.concatenate([dq, dk, dv], axis=0)
                    (gw[('w_qkv', i)],) = dw_col(s['h_mix'], [dqkv], N_CHIPS, n_qkv, "dw_qkv")
                    ready.extend([('w_o', i), ('w_qkv', i)])
                    its, carry = leaving()
                    (dx, dxb, dg), got = colmm_t([dqkv], [wt[('w_qkv', i)]], n_qkv, s['x_mix'], vec(g_mix[layer]), dx,
                                                 "mm_qkv_t", carry)
                    recv.update(zip(its, got))
                gs[('g_mix', layer)] = dg
    grad_x = dx.reshape(x.shape)

    assert not ready and set(recv) == set(items)
    where = jnp.stack([chip, core]).astype(jnp.int32)
    fulls = []
    for name in BIG:
        its = [(name, layer) for layer in range(p[name].shape[0])]
        fulls.append(rs_sum([recv[it] for it in its], [part[it] for it in its], where, "rs_sum"))
    shared = rs_share(fulls)
    grads = {name: sh.reshape(p[name].shape) for name, sh in zip(BIG, shared)}

    stack = lambda name: jnp.concatenate([gs[(name, layer)].reshape((1,) + p[name].shape[1:]) for layer in range(p[name].shape[0])], axis=0)
    small_g = [stack(name) if name != 'g_final' else dg_final.reshape(p[name].shape) for name in SMALL]
    packed = _pack(small_g + [gs[('conv_w', 0)]])
    red = allreduce_small(packed)
    outs = _unpack(red, [p[name].shape for name in SMALL] + [(CONV_WIDTH, c_mix)])
    for name, g in zip(SMALL, outs[:-1]):
        grads[name] = g
    conv_g = outs[-1].reshape(CONV_WIDTH, N_CHIPS, c_mix // N_CHIPS)
    grads['conv_w'] = lax.dynamic_index_in_dim(conv_g, chip, axis=1, keepdims=False).reshape(conv_w.shape)

    delta, new_m, new_v = {}, {}, {}
    for name in BIG:
        shp = p[name].shape
        two = lambda a: a.reshape(shp[0] * shp[1], shp[2])
        dl, nm, nv = adamw(two(p[name]), two(grads[name]), two(p['m_' + name]), two(p['v_' + name]), "adamw")
        delta[name], new_m[name], new_v[name] = dl.reshape(shp), nm.reshape(shp), nv.reshape(shp)
    small_names = SMALL + ['conv_w']
    pk = lambda pre: _pack([p[pre + name] for name in small_names])
    dl, nm, nv = adamw(pk(''), _pack([grads[name] for name in small_names]), pk('m_'), pk('v_'), "adamw_small")
    shapes = [p[name].shape for name in small_names]
    for dst, val in ((delta, dl), (new_m, nm), (new_v, nv)):
        for name, a in zip(small_names, _unpack(val, shapes)):
            dst[name] = a

    return (loss, grad_x, *[back(n, d[n]) for d in (grads, delta, new_m, new_v) for n in WEIGHTS])
```

```python
import functools

import jax
import jax.numpy as jnp
from jax import lax
from jax.experimental import pallas as pl
from jax.experimental.pallas import tpu as pltpu

F32 = jnp.float32
BF16 = jnp.bfloat16
EPS = 1e-6
HEAD_DIM = 64
CONV_WIDTH = 31
CHUNK = 128
KBLK = 128
ATT_BLOCK = 256
ATT_LANES = 256
DW_TOKENS = 2048
CONV_ROWS = 64
STREAM_DEPTH = 3
MASKED = -1e30
STICK_GONE = -110.0
LANES = 128
HALO = 32
ADAM_LR, ADAM_B1, ADAM_B2, ADAM_EPS, ADAM_WD, ADAM_STEP = 0.001, 0.9, 0.999, 1e-08, 0.01, 10
VMEM_LIMIT = 56 * 1024 * 1024
MESH = pl.DeviceIdType.MESH
N_CHIPS = 4
N_DEV = 8
REDUCE_DTYPE = BF16


def _cparams(sem):
    return pltpu.CompilerParams(dimension_semantics=sem, vmem_limit_bytes=VMEM_LIMIT)


def _nt(a, b):
    return lax.dot_general(a, b, (((1,), (1,)), ((), ())), preferred_element_type=F32)


def _tn(a, b):
    return lax.dot_general(a, b, (((0,), (0,)), ((), ())), preferred_element_type=F32)


def _nn(a, b):
    return jnp.dot(a, b, preferred_element_type=F32)


def _sigmoid(x):
    return 0.5 * jnp.tanh(0.5 * x) + 0.5


def _tile(t, want):
    if t <= want:
        return t
    for cand in range(want - want % 8, 7, -8):
        if t % cand == 0:
            return cand
    raise ValueError((t, want))


def rmsnorm_fwd(x, g, name):
    t, d = x.shape
    tm = _tile(t, 512)

    def body(x_ref, g_ref, h_ref):
        xv = x_ref[...]
        r = lax.rsqrt(jnp.mean(xv * xv, axis=-1, keepdims=True) + EPS)
        h_ref[...] = (xv * r * g_ref[...]).astype(BF16)

    return pl.pallas_call(
        body, name=name, grid=(t // tm,),
        in_specs=[pl.BlockSpec((tm, d), lambda i: (i, 0)), pl.BlockSpec((1, d), lambda i: (0, 0))],
        out_specs=pl.BlockSpec((tm, d), lambda i: (i, 0)),
        out_shape=jax.ShapeDtypeStruct((t, d), BF16),
        compiler_params=_cparams(("parallel",)),
    )(x, g)


def colmm(h, ws, nu, out_dtype, name, carry=None, swiglu=False):
    t, k = h.shape
    j, nj = (ws[0].shape[0], ws[0].shape[1]) if swiglu else (ws[0].shape[0], ws[0].shape[2])
    per = nj // nu
    units = j * per
    tm = _tile(t, 1024)
    nw = len(ws)
    n_out = 3 if swiglu else nw

    def body(*refs):
        h_ref = refs[0]
        hv = h_ref[...]
        if swiglu:
            silu_ref, udsilu_ref, act_ref = refs[1 + nw:]
            gv = _nt(hv, refs[1][0])
            uv = _nt(hv, refs[2][0])
            s = _sigmoid(gv)
            silu = gv * s
            silu_ref[0] = silu.astype(out_dtype)
            udsilu_ref[0] = (uv * (s + silu * (1.0 - s))).astype(out_dtype)
            act_ref[0] = (silu * uv).astype(out_dtype)
            return
        for n in range(nw):
            for s in range(j):
                res = _nn(hv, refs[1 + n][s]).astype(out_dtype)
                for u in range(per):
                    refs[1 + nw + n][s * per + u] = res[:, u * nu:(u + 1) * nu]

    out_shape = [jax.ShapeDtypeStruct((units, t, nu), out_dtype)] * n_out
    if swiglu:
        assert nw == 2 and per == 1
        return _call(
            body, name=name, grid=(j, t // tm),
            in_specs=[pl.BlockSpec((tm, k), lambda s, i: (i, 0))] + [pl.BlockSpec((1, nj, k), lambda s, i: (s, 0, 0))] * nw,
            out_specs=[pl.BlockSpec((1, tm, nu), lambda s, i: (s, i, 0))] * n_out, out_shape=out_shape,
            args=[h, *ws], sem=("parallel", "parallel"), carry=carry)
    return _call(
        body, name=name, grid=(t // tm,),
        in_specs=[pl.BlockSpec((tm, k), lambda i: (i, 0))] + [pl.BlockSpec((j, k, nj), lambda i: (0, 0, 0))] * nw,
        out_specs=[pl.BlockSpec((units, tm, nu), lambda i: (0, i, 0))] * n_out, out_shape=out_shape,
        args=[h, *ws], sem=("parallel",), carry=carry)


def rowmm(a, w, resid, scale, name, carry=None, norm_g=None):
    u_n, t, ku = a.shape
    n = w.shape[2]
    tm = _tile(t, 512)

    def body(a_ref, w_ref, r_ref, *rest):
        acc = jnp.zeros((tm, n), F32)
        for u in range(u_n):
            acc = acc + _nn(a_ref[u], w_ref[u])
        out = r_ref[...] + scale * acc
        if norm_g is None:
            (o_ref,) = rest
        else:
            g_ref, o_ref, h_ref = rest
            r = lax.rsqrt(jnp.mean(out * out, axis=-1, keepdims=True) + EPS)
            h_ref[...] = (out * r * g_ref[...]).astype(BF16)
        o_ref[...] = out

    row = pl.BlockSpec((tm, n), lambda i: (i, 0))
    normed = norm_g is not None
    outs, carried = _call(
        body, name=name, grid=(t // tm,),
        in_specs=[pl.BlockSpec((u_n, tm, ku), lambda i: (0, i, 0)), pl.BlockSpec((u_n, ku, n), lambda i: (0, 0, 0)),
                  row] + [pl.BlockSpec((1, n), lambda i: (0, 0))] * normed,
        out_specs=[row] + [row] * normed,
        out_shape=[jax.ShapeDtypeStruct((t, n), F32)] + [jax.ShapeDtypeStruct((t, n), BF16)] * normed,
        args=[a, w, resid] + [norm_g] * normed, sem=("parallel",), carry=carry)
    return (outs[0], outs[1] if normed else None), carried


def rowmm_t(dyb, w, scale, out_dtype, name, swiglu=None):
    t, n = dyb.shape
    u_n, ku, _ = w.shape
    tm = _tile(t, 512)

    if swiglu is None:
        def body(dy_ref, w_ref, o_ref):
            dy = dy_ref[...]
            for u in range(u_n):
                o_ref[u] = (scale * _nt(dy, w_ref[u])).astype(out_dtype)

        return pl.pallas_call(
            body, name=name, grid=(t // tm,),
            in_specs=[pl.BlockSpec((tm, n), lambda i: (i, 0)), pl.BlockSpec((u_n, ku, n), lambda i: (0, 0, 0))],
            out_specs=pl.BlockSpec((u_n, tm, ku), lambda i: (0, i, 0)),
            out_shape=jax.ShapeDtypeStruct((u_n, t, ku), out_dtype),
            compiler_params=_cparams(("parallel",)),
        )(dyb, w)

    def body(dy_ref, w_ref, silu_ref, udsilu_ref, dg_ref, du_ref):
        dy = dy_ref[...]
        for u in range(u_n):
            dact = scale * _nt(dy, w_ref[u])
            dg_ref[u] = (dact * udsilu_ref[u].astype(F32)).astype(BF16)
            du_ref[u] = (dact * silu_ref[u].astype(F32)).astype(BF16)

    blk = pl.BlockSpec((u_n, tm, ku), lambda i: (0, i, 0))
    return pl.pallas_call(
        body, name=name, grid=(t // tm,),
        in_specs=[pl.BlockSpec((tm, n), lambda i: (i, 0)), pl.BlockSpec((u_n, ku, n), lambda i: (0, 0, 0)), blk, blk],
        out_specs=[blk] * 2, out_shape=[jax.ShapeDtypeStruct((u_n, t, ku), BF16)] * 2,
        compiler_params=_cparams(("parallel",)),
    )(dyb, w, *swiglu)


def colmm_t(dzs, ws, nu, x, g, dy_in, name, carry=None, transposed=False):
    t, k = x.shape
    j, nj = (ws[0].shape[0], ws[0].shape[1]) if transposed else (ws[0].shape[0], ws[0].shape[2])
    per = nj // nu
    units = j * per
    nw = len(ws)
    tm = _tile(t, 512)
    assert not transposed or per == 1

    def body(*refs):
        dz_refs = refs[:nw]
        w_refs = refs[nw:2 * nw]
        x_ref, g_ref, dy_ref, dx_ref, dxb_ref, dg_ref = refs[2 * nw:]
        i = pl.program_id(0)
        dh = jnp.zeros((tm, k), F32)
        for n in range(nw):
            for u in range(units):
                if transposed:
                    dh = dh + _nn(dz_refs[n][u], w_refs[n][u])
                else:
                    wv = w_refs[n][u // per, :, (u % per) * nu:(u % per + 1) * nu]
                    dh = dh + _nt(dz_refs[n][u], wv)
        xv = x_ref[...]
        gv = g_ref[...]
        r = lax.rsqrt(jnp.mean(xv * xv, axis=-1, keepdims=True) + EPS)
        uu = dh * gv
        dx = dy_ref[...] + r * uu - xv * (r * r * r * jnp.mean(uu * xv, axis=-1, keepdims=True))
        dx_ref[...] = dx
        dxb_ref[...] = dx.astype(BF16)
        part = jnp.sum(dh * (xv * r), axis=0, keepdims=True)

        @pl.when(i == 0)
        def _():
            dg_ref[...] = part

        @pl.when(i > 0)
        def _():
            dg_ref[...] += part

    dz_spec = pl.BlockSpec((units, tm, nu), lambda i: (0, i, 0))
    w_spec = pl.BlockSpec((j, nj, k) if transposed else (j, k, nj), lambda i: (0, 0, 0))
    row = pl.BlockSpec((tm, k), lambda i: (i, 0))
    vec = pl.BlockSpec((1, k), lambda i: (0, 0))
    return _call(
        body, name=name, grid=(t // tm,),
        in_specs=[dz_spec] * nw + [w_spec] * nw + [row, vec, row],
        out_specs=[row, row, vec],
        out_shape=[jax.ShapeDtypeStruct((t, k), F32), jax.ShapeDtypeStruct((t, k), BF16),
                   jax.ShapeDtypeStruct((1, k), F32)],
        args=[*dzs, *ws, x, g, dy_in], sem=("arbitrary",), carry=carry)


def dw_col(h, dzs, j, nu, name, transposed=False):
    t, k = h.shape
    units = dzs[0].shape[0]
    per = units // j
    nw = len(dzs)
    tt = _tile(t, DW_TOKENS)
    assert not transposed or per == 1

    def body(*refs):
        h_ref = refs[0]
        s = pl.program_id(1)
        hv = h_ref[...]
        outs, copies = refs[1 + nw:1 + 2 * nw], refs[1 + 2 * nw:]

        @pl.when(s == 0)
        def _():
            for o_ref in outs:
                o_ref[...] = jnp.zeros_like(o_ref)

        for n in range(nw):
            if transposed:
                outs[n][0] += _tn(refs[1 + n][0], hv)
                continue
            for u in range(per):
                outs[n][0, :, u * nu:(u + 1) * nu] += _tn(hv, refs[1 + n][u])

        @pl.when(s == pl.num_programs(1) - 1)
        def _():
            for o_ref, c_ref in zip(outs, copies):
                c_ref[...] = o_ref[...].astype(REDUCE_DTYPE)

    shard = (nu, k) if transposed else (k, per * nu)
    o_spec = pl.BlockSpec((1,) + shard, lambda u, s: (u, 0, 0))
    res = pl.pallas_call(
        body, name=name, grid=(j, t // tt),
        in_specs=[pl.BlockSpec((tt, k), lambda u, s: (s, 0))] + [pl.BlockSpec((per, tt, nu), lambda u, s: (u, s, 0))] * nw,
        out_specs=[o_spec] * (2 * nw),
        out_shape=[jax.ShapeDtypeStruct((j,) + shard, F32)] * nw + [jax.ShapeDtypeStruct((j,) + shard, REDUCE_DTYPE)] * nw,
        compiler_params=_cparams(("parallel", "arbitrary")),
    )(h, *dzs)
    return list(zip(res[:nw], res[nw:]))


def dw_row(a, dyb, scale, name):
    u_n, t, ku = a.shape
    n = dyb.shape[1]
    tt = _tile(t, DW_TOKENS)

    per_step = u_n if u_n * ku <= n else 1

    def body(a_ref, dy_ref, o_ref, c_ref):
        @pl.when(pl.program_id(1) == 0)
        def _():
            o_ref[...] = jnp.zeros_like(o_ref)

        dy = dy_ref[...]
        for u in range(per_step):
            o_ref[u] += scale * _tn(a_ref[u], dy)

        @pl.when(pl.program_id(1) == pl.num_programs(1) - 1)
        def _():
            c_ref[...] = o_ref[...].astype(REDUCE_DTYPE)

    o_spec = pl.BlockSpec((per_step, ku, n), lambda u, s: (u, 0, 0))
    return tuple(pl.pallas_call(
        body, name=name, grid=(u_n // per_step, t // tt),
        in_specs=[pl.BlockSpec((per_step, tt, ku), lambda u, s: (u, s, 0)), pl.BlockSpec((tt, n), lambda u, s: (s, 0))],
        out_specs=[o_spec, o_spec],
        out_shape=[jax.ShapeDtypeStruct((u_n, ku, n), F32), jax.ShapeDtypeStruct((u_n, ku, n), REDUCE_DTYPE)],
        compiler_params=_cparams(("parallel", "arbitrary")),
    )(a, dyb))


def loss_head(x, g, target):
    t, d = x.shape
    tm = _tile(t, 256)

    def body(x_ref, g_ref, t_ref, loss_ref, dx_ref, dxb_ref, dg_ref):
        i = pl.program_id(0)
        xv = x_ref[...]
        gv = g_ref[...]
        r = lax.rsqrt(jnp.mean(xv * xv, axis=-1, keepdims=True) + EPS)
        xh = xv * r
        err = xh * gv - t_ref[...]
        dy = err * (1.0 / d)
        uu = dy * gv
        dx = r * uu - xv * (r * r * r * jnp.mean(uu * xv, axis=-1, keepdims=True))
        dx_ref[...] = dx
        dxb_ref[...] = dx.astype(BF16)
        dg_part = jnp.sum(dy * xh, axis=0, keepdims=True)
        row = jnp.sum(err * err, axis=-1, keepdims=True) * (0.5 / d)
        l_part = jnp.zeros((8, LANES), F32) + jnp.sum(row, axis=0, keepdims=True)

        @pl.when(i == 0)
        def _():
            dg_ref[...] = dg_part
            loss_ref[...] = l_part

        @pl.when(i > 0)
        def _():
            dg_ref[...] += dg_part
            loss_ref[...] += l_part

    row = pl.BlockSpec((tm, d), lambda i: (i, 0))
    vec = pl.BlockSpec((1, d), lambda i: (0, 0))
    return pl.pallas_call(
        body, name="loss_head", grid=(t // tm,),
        in_specs=[row, vec, row],
        out_specs=[pl.BlockSpec((8, LANES), lambda i: (0, 0)), row, row, vec],
        out_shape=[jax.ShapeDtypeStruct((8, LANES), F32), jax.ShapeDtypeStruct((t, d), F32),
                   jax.ShapeDtypeStruct((t, d), BF16), jax.ShapeDtypeStruct((1, d), F32)],
        compiler_params=_cparams(("arbitrary",)),
    )(x, g, target)


def _split(v):
    hi = v.astype(BF16)
    lo = (v - hi.astype(F32)).astype(BF16)
    return hi, lo


def _keysums(v, m_ext):
    hi, lo = _split(v)
    outs = []
    for j in range(v.shape[1] // KBLK):
        sl = slice(j * KBLK, (j + 1) * KBLK)
        cs = _nn(jnp.concatenate([hi[:, sl], lo[:, sl]], axis=1), m_ext)
        outs.append((cs[:, :KBLK], cs[:, KBLK:]))
    return outs


def _softplus_parts(z):
    sp = jnp.maximum(z, 0.0) + jnp.log(1.0 + jnp.exp(-jnp.abs(z)))
    return sp, z - sp


def _sum_matrices():
    r = lax.broadcasted_iota(jnp.int32, (2 * KBLK, 2 * KBLK), 0) % KBLK
    c = lax.broadcasted_iota(jnp.int32, (2 * KBLK, 2 * KBLK), 1)
    suffix = jnp.where((r > c) | (c >= KBLK), 1.0, 0.0).astype(BF16)
    prefix = jnp.where((r <= c) | (c >= KBLK), 1.0, 0.0).astype(BF16)
    return suffix, prefix


def _att_geometry(qkv, seq):
    upp = qkv.shape[0] // 3
    bq = min(ATT_BLOCK, seq)
    per_unit = (2 * LANES) // ATT_LANES
    return upp, bq, seq // bq, bq // KBLK, per_unit, upp * per_unit, ATT_LANES // HEAD_DIM


def _head_lanes(rows, heads):
    lane = lax.broadcasted_iota(jnp.int32, (rows, ATT_LANES), 1)
    return [(lane >= HEAD_DIM * h) & (lane < HEAD_DIM * (h + 1)) for h in range(heads)]


def attn_fwd(qkv, n_seq, seq):
    t = qkv.shape[1]
    upp, bq, nq, nsub, per_unit, groups, heads = _att_geometry(qkv, seq)
    suffix_m, _ = _sum_matrices()

    def body(q_ref, k_ref, v_ref, m_ref, o_ref, tot_ref, cnt_ref):
        qi = pl.program_id(2)
        step_id = (pl.program_id(0) * groups + pl.program_id(1)) * nq + qi
        in_head = _head_lanes(bq, heads)
        only = lambda v, h: jnp.where(in_head[h], v, jnp.zeros_like(v))
        q_all = q_ref[0] * jnp.asarray(HEAD_DIM ** -0.5, BF16)
        qs = [only(q_all, h) for h in range(heads)]
        m_ext = m_ref[...]
        row = lax.broadcasted_iota(jnp.int32, (bq, bq), 0)
        col = lax.broadcasted_iota(jnp.int32, (bq, bq), 1)
        diag_mask = col < row

        def block(kj, carry, mask):
            off = pl.multiple_of(kj * bq, bq)
            k_all = k_ref[0, pl.ds(off, bq), :]
            v_all = v_ref[0, pl.ds(off, bq), :]
            rems, acc = carry
            out = []
            for h in range(heads):
                rem = rems[h]
                z = _nt(qs[h], k_all)
                if mask is not None:
                    z = jnp.where(mask, z, MASKED)
                sp, ls = _softplus_parts(z)
                sums = _keysums(-sp, m_ext)
                parts = [None] * nsub
                for j in reversed(range(nsub)):
                    suf, total = sums[j]
                    parts[j] = jnp.exp(ls[:, j * KBLK:(j + 1) * KBLK] + suf + rem)
                    rem = rem + total
                a = jnp.concatenate(parts, axis=1)
                acc = acc + _nn(a.astype(BF16), only(v_all, h))
                out.append(rem)
            return tuple(out), acc

        def most_left(c):
            return functools.reduce(jnp.maximum, [jnp.max(r) for r in c[0]])

        def more(s):
            return (s[0] < qi) & (s[1] > STICK_GONE)

        def step(s):
            c = block(qi - 1 - s[0], s[2], None)
            return s[0] + 1, most_left(c), c

        zero = jnp.zeros((bq, LANES), F32)
        carry = block(qi, ((zero,) * heads, jnp.zeros((bq, ATT_LANES), F32)), diag_mask)
        n_left, _, (rems, acc) = lax.while_loop(more, step, (jnp.int32(0), most_left(carry), carry))
        o_ref[0] = acc.astype(BF16)
        first = lax.broadcasted_iota(jnp.int32, (bq, LANES), 1) < HEAD_DIM
        tot_ref[...] = jnp.concatenate([jnp.where(first, rems[h], rems[h + 1]) for h in range(0, heads, 2)], axis=1)
        cnt_ref[step_id] = n_left.astype(F32)

    qblk = lambda b, g, i: (g // per_unit, b * nq + i, g % per_unit)
    return pl.pallas_call(
        body, name="attn_fwd", grid=(n_seq, groups, nq),
        in_specs=[pl.BlockSpec((1, bq, ATT_LANES), qblk),
                  pl.BlockSpec((1, seq, ATT_LANES), lambda b, g, i: (upp + g // per_unit, b, g % per_unit)),
                  pl.BlockSpec((1, seq, ATT_LANES), lambda b, g, i: (2 * upp + g // per_unit, b, g % per_unit)),
                  pl.BlockSpec((2 * KBLK, 2 * KBLK), lambda b, g, i: (0, 0))],
        out_specs=[pl.BlockSpec((1, bq, ATT_LANES), qblk),
                   pl.BlockSpec((bq, ATT_LANES), lambda b, g, i: (b * nq + i, g)),
                   pl.BlockSpec(memory_space=pltpu.SMEM)],
        out_shape=[jax.ShapeDtypeStruct((upp, t, 2 * LANES), BF16), jax.ShapeDtypeStruct((t, upp * 2 * LANES), F32),
                   jax.ShapeDtypeStruct((n_seq * groups * nq,), F32)],
        compiler_params=_cparams(("arbitrary", "arbitrary", "arbitrary")),
    )(qkv, qkv, qkv, suffix_m)


def attn_bwd(qkv, do, tot, cnt, n_seq, seq):
    t = qkv.shape[1]
    upp, bq, nq, nsub, per_unit, groups, heads = _att_geometry(qkv, seq)
    _, prefix_m = _sum_matrices()
    scale = HEAD_DIM ** -0.5

    def body(q_ref, k_ref, v_ref, do_ref, tot_ref, m_ref, cnt_ref, dq_ref, dk_ref, dv_ref, dk_acc, dv_acc):
        qi = pl.program_id(2)
        step_id = (pl.program_id(0) * groups + pl.program_id(1)) * nq + qi
        n_left = jnp.clip(cnt_ref[step_id].astype(jnp.int32), 0, qi)
        in_head = _head_lanes(bq, heads)
        only = lambda v, h: jnp.where(in_head[h], v, jnp.zeros_like(v))
        q_all = q_ref[0] * jnp.asarray(scale, BF16)
        do_all = do_ref[0]
        qs = [only(q_all, h) for h in range(heads)]
        dos = [only(do_all, h) for h in range(heads)]
        first = lax.broadcasted_iota(jnp.int32, (bq, LANES), 1) < HEAD_DIM
        tots = []
        for h in range(0, heads, 2):
            both = tot_ref[:, h // 2 * LANES:(h // 2 + 1) * LANES]
            swapped = pltpu.roll(both, HEAD_DIM, 1)
            tots += [jnp.where(first, both, swapped), jnp.where(first, swapped, both)]
        m_ext = m_ref[...]
        row = lax.broadcasted_iota(jnp.int32, (bq, bq), 0)
        col = lax.broadcasted_iota(jnp.int32, (bq, bq), 1)
        diag_mask = col < row

        @pl.when(qi == 0)
        def _():
            dk_acc[...] = jnp.zeros_like(dk_acc)
            dv_acc[...] = jnp.zeros_like(dv_acc)

        def block(kj, carry, mask):
            off = pl.multiple_of(kj * bq, bq)
            k_all = k_ref[0, pl.ds(off, bq), :]
            v_all = v_ref[0, pl.ds(off, bq), :]
            pres, gpres, dq = carry
            dk_part = jnp.zeros((bq, ATT_LANES), F32)
            dv_part = jnp.zeros((bq, ATT_LANES), F32)
            pres_out, gpres_out = [], []
            for h in range(heads):
                pre, gpre = pres[h], gpres[h]
                z = _nt(qs[h], k_all)
                if mask is not None:
                    z = jnp.where(mask, z, MASKED)
                sp, ls = _softplus_parts(z)
                sums = _keysums(-sp, m_ext)
                parts = []
                for j in range(nsub):
                    pin, ptot = sums[j]
                    parts.append(jnp.exp(ls[:, j * KBLK:(j + 1) * KBLK] + (tots[h] - (pre + pin))))
                    pre = pre + ptot
                a = jnp.concatenate(parts, axis=1)
                g = a * _nt(dos[h], v_all)
                gsums = _keysums(g, m_ext)
                parts = []
                for j in range(nsub):
                    gin, gtot = gsums[j]
                    parts.append(gpre + gin)
                    gpre = gpre + gtot
                dz = g - jnp.exp(ls) * jnp.concatenate(parts, axis=1)
                dzb = dz.astype(BF16)
                dq = dq + _nn(dzb, only(k_all, h))
                dk_part = dk_part + _tn(dzb, qs[h])
                dv_part = dv_part + _tn(a.astype(BF16), dos[h])
                pres_out.append(pre)
                gpres_out.append(gpre)
            dk_acc[pl.ds(off, bq), :] += dk_part
            dv_acc[pl.ds(off, bq), :] += dv_part
            return tuple(pres_out), tuple(gpres_out), dq

        zero = jnp.zeros((bq, LANES), F32)
        carry = ((zero,) * heads, (zero,) * heads, jnp.zeros((bq, ATT_LANES), F32))
        carry = lax.fori_loop(qi - n_left, qi, lambda kj, c: block(kj, c, None), carry)
        carry = block(qi, carry, diag_mask)
        dq_ref[0] = (carry[2] * scale).astype(BF16)

        @pl.when(qi == nq - 1)
        def _():
            dk_ref[0] = dk_acc[...].astype(BF16)
            dv_ref[0] = dv_acc[...].astype(BF16)

    qblk = lambda b, g, i: (g // per_unit, b * nq + i, g % per_unit)
    kv_out = pl.BlockSpec((1, seq, ATT_LANES), lambda b, g, i: (g // per_unit, b, g % per_unit))
    shp = jax.ShapeDtypeStruct((upp, t, 2 * LANES), BF16)
    return pl.pallas_call(
        body, name="attn_bwd", grid=(n_seq, groups, nq),
        in_specs=[pl.BlockSpec((1, bq, ATT_LANES), qblk),
                  pl.BlockSpec((1, seq, ATT_LANES), lambda b, g, i: (upp + g // per_unit, b, g % per_unit)),
                  pl.BlockSpec((1, seq, ATT_LANES), lambda b, g, i: (2 * upp + g // per_unit, b, g % per_unit)),
                  pl.BlockSpec((1, bq, ATT_LANES), qblk),
                  pl.BlockSpec((bq, ATT_LANES), lambda b, g, i: (b * nq + i, g)),
                  pl.BlockSpec((2 * KBLK, 2 * KBLK), lambda b, g, i: (0, 0)),
                  pl.BlockSpec(memory_space=pltpu.SMEM)],
        out_specs=[pl.BlockSpec((1, bq, ATT_LANES), qblk), kv_out, kv_out],
        out_shape=[shp, shp, shp],
        scratch_shapes=[pltpu.VMEM((seq, ATT_LANES), F32), pltpu.VMEM((seq, ATT_LANES), F32)],
        compiler_params=_cparams(("parallel", "parallel", "arbitrary")),
    )(qkv, qkv, qkv, do, tot, prefix_m, cnt)


def _ln_stats(v):
    mu = jnp.mean(v, axis=-1, keepdims=True)
    vc = v - mu
    rstd = lax.rsqrt(jnp.mean(vc * vc, axis=-1, keepdims=True) + EPS)
    return vc * rstd, rstd


def _glu_into(a0_ref, av_ref, ag_ref, hv_ref, hg_ref, first):
    hv = hv_ref[0].astype(F32)
    hg = hg_ref[0].astype(F32)
    a0_ref[0:HALO, :] = jnp.where(first, 0.0, hv * _sigmoid(hg))
    av = av_ref[0].astype(F32)
    ag = ag_ref[0].astype(F32)
    a0_ref[HALO:, :] = av * _sigmoid(ag)


def _shifted_taps(ref, shifted_ref, tm, first):
    taps = []
    for b in range(8):
        offs = [o for o in range(first, first + CONV_WIDTH) if o % 8 == b]
        n_rows = max(offs) - b + tm
        shifted_ref[b, 0:n_rows, :] = ref[pl.ds(b, n_rows), :]
        taps += [(b, o - b, o - first) for o in offs]
    return taps


def _tril_mask():
    r = lax.broadcasted_iota(jnp.int32, (CHUNK, CHUNK), 0)
    c = lax.broadcasted_iota(jnp.int32, (CHUNK, CHUNK), 1)
    return c <= r


def mix_fwd(z, conv_w, conv_b, ln_a_g, ln_a_b, ln_v_g, ln_v_b, sp_w, sp_bt, seq):
    _, t, c = z.shape
    tm = _tile(seq, 512)
    tiles_per_seq = seq // tm
    groups = c // LANES
    hb = tm // HALO

    def body(av_ref, ag_ref, u_ref, v_ref, hv_ref, hg_ref, cw_ref, cb_ref, lag_ref, lab_ref, lvg_ref, lvb_ref,
             spw_ref, spb_ref, cat_ref, a1_ref, a0_ref, sh_ref):
        i = pl.program_id(0)
        _glu_into(a0_ref, av_ref, ag_ref, hv_ref, hg_ref, i % tiles_per_seq == 0)
        acc = jnp.zeros((tm, c), F32) + cb_ref[...]
        for b, ro, k in _shifted_taps(a0_ref, sh_ref, tm, HALO - (CONV_WIDTH - 1)):
            acc = acc + cw_ref[k:k + 1, :] * sh_ref[b, pl.ds(ro, tm), :]
        a1_ref[...] = acc
        xh, _ = _ln_stats(acc)
        a2 = xh * lag_ref[...] + lab_ref[...]
        a3 = (a2 * _sigmoid(a2)).astype(BF16)
        half = c // 2
        cat_ref[0] = a3[:, :half]
        cat_ref[1] = a3[:, half:]
        tril = _tril_mask()
        for g in range(groups):
            sl = slice(g * LANES, (g + 1) * LANES)
            xh, _ = _ln_stats(v_ref[0][:, sl].astype(F32))
            vn = (xh * lvg_ref[:, sl] + lvb_ref[:, sl]).astype(BF16)
            w = jnp.where(tril, spw_ref[g], 0.0).astype(BF16)
            bias = spb_ref[:, g:g + 1]
            for ch in range(tm // CHUNK):
                rows = slice(ch * CHUNK, (ch + 1) * CHUNK)
                vs = _nn(w, vn[rows]) + bias
                bo = (u_ref[0][rows, sl].astype(F32) * vs).astype(BF16)
                cat_ref[2 + (g * LANES) // half, rows, (g * LANES) % half:(g * LANES) % half + LANES] = bo

    unit = lambda u: pl.BlockSpec((1, tm, c), lambda i: (u, i, 0))
    halo = lambda u: pl.BlockSpec((1, HALO, c), lambda i: (u, jnp.maximum(i * hb - 1, 0), 0))
    vec = pl.BlockSpec((1, c), lambda i: (0, 0))
    return pl.pallas_call(
        body, name="mix_fwd", grid=(t // tm,),
        in_specs=[unit(0), unit(1), unit(2), unit(3), halo(0), halo(1),
                  pl.BlockSpec((CONV_WIDTH, c), lambda i: (0, 0)), vec, vec, vec, vec, vec,
                  pl.BlockSpec((groups, CHUNK, CHUNK), lambda i: (0, 0, 0)),
                  pl.BlockSpec((CHUNK, groups), lambda i: (0, 0))],
        out_specs=[pl.BlockSpec((4, tm, c // 2), lambda i: (0, i, 0)), pl.BlockSpec((tm, c), lambda i: (i, 0))],
        out_shape=[jax.ShapeDtypeStruct((4, t, c // 2), BF16), jax.ShapeDtypeStruct((t, c), F32)],
        scratch_shapes=[pltpu.VMEM((HALO + tm, c), F32), pltpu.VMEM((8, HALO + tm, c), F32)],
        compiler_params=_cparams(("parallel",)),
    )(z, z, z, z, z, z, conv_w, conv_b, ln_a_g, ln_a_b, ln_v_g, ln_v_b, sp_w, sp_bt)


def mix_bwd_point(dcat, z, a1, ln_a_g, ln_a_b, ln_v_g, ln_v_b, sp_w, sp_wt, sp_bt, seq):
    _, t, c = z.shape
    tm = _tile(seq, 512)
    groups = c // LANES
    half = c // 2

    def body(dc_ref, u_ref, v_ref, a1_ref, lag_ref, lab_ref, lvg_ref, lvb_ref, spw_ref, spwt_ref, spb_ref,
             dz_ref, da1_ref, dcb_ref, dlag_ref, dlab_ref, dlvg_ref, dlvb_ref, dspw_ref, dspb_ref):
        i = pl.program_id(0)
        last = pl.num_programs(0) - 1

        @pl.when(i == 0)
        def _():
            for r in (dcb_ref, dlag_ref, dlab_ref, dlvg_ref, dlvb_ref, dspw_ref, dspb_ref):
                r[...] = jnp.zeros_like(r)

        da3 = jnp.concatenate([dc_ref[0], dc_ref[1]], axis=-1)
        xh, rstd = _ln_stats(a1_ref[...])
        a2 = xh * lag_ref[...] + lab_ref[...]
        s = _sigmoid(a2)
        da2 = da3 * (s * (1.0 + a2 * (1.0 - s)))
        dlag_ref[...] += jnp.sum(da2 * xh, axis=0, keepdims=True)
        dlab_ref[...] += jnp.sum(da2, axis=0, keepdims=True)
        dxh = da2 * lag_ref[...]
        da1 = rstd * (dxh - jnp.mean(dxh, axis=-1, keepdims=True) - xh * jnp.mean(dxh * xh, axis=-1, keepdims=True))
        da1_ref[...] = da1
        dcb_ref[...] += jnp.sum(da1, axis=0, keepdims=True)

        tril = _tril_mask()
        for g in range(groups):
            sl = slice(g * LANES, (g + 1) * LANES)
            xh, rstd = _ln_stats(v_ref[0][:, sl].astype(F32))
            lg = lvg_ref[:, sl]
            vnb = (xh * lg + lvb_ref[:, sl]).astype(BF16)
            w = jnp.where(tril, spw_ref[g], 0.0).astype(BF16)
            wt = jnp.where(tril.T, spwt_ref[g], 0.0).astype(BF16)
            bias = spb_ref[:, g:g + 1]
            dbo_all = dc_ref[2 + (g * LANES) // half][:, (g * LANES) % half:(g * LANES) % half + LANES]
            dvn_parts = []
            dw_acc = jnp.zeros((CHUNK, CHUNK), F32)
            db_acc = jnp.zeros((CHUNK, LANES), F32)
            for ch in range(tm // CHUNK):
                rows = slice(ch * CHUNK, (ch + 1) * CHUNK)
                vs = _nn(w, vnb[rows]) + bias
                dbo = dbo_all[rows]
                uv = u_ref[0][rows, sl].astype(F32)
                dz_ref[0, rows, sl] = (dbo * vs).astype(BF16)
                dvs = dbo * uv
                dvsb = dvs.astype(BF16)
                dvn_parts.append(_nn(wt, dvsb))
                dw_acc = dw_acc + _nt(dvsb, vnb[rows])
                db_acc = db_acc + dvs
            dvn = jnp.concatenate(dvn_parts, axis=0)
            dspw_ref[g] += jnp.where(tril, dw_acc, 0.0)
            dspb_ref[g] += db_acc
            dlvg_ref[:, sl] += jnp.sum(dvn * xh, axis=0, keepdims=True)
            dlvb_ref[:, sl] += jnp.sum(dvn, axis=0, keepdims=True)
            dxh = dvn * lg
            dv = rstd * (dxh - jnp.mean(dxh, axis=-1, keepdims=True) - xh * jnp.mean(dxh * xh, axis=-1, keepdims=True))
            dz_ref[1, :, sl] = dv.astype(BF16)

        @pl.when(i == last)
        def _():
            for g in range(groups):
                dspb_ref[g] = jnp.zeros((CHUNK, LANES), F32) + jnp.sum(dspb_ref[g], axis=-1, keepdims=True)

    unit = lambda u: pl.BlockSpec((1, tm, c), lambda i: (u, i, 0))
    vec = pl.BlockSpec((1, c), lambda i: (0, 0))
    sq = pl.BlockSpec((groups, CHUNK, CHUNK), lambda i: (0, 0, 0))
    vshape = jax.ShapeDtypeStruct((1, c), F32)
    sshape = jax.ShapeDtypeStruct((groups, CHUNK, CHUNK), F32)
    return pl.pallas_call(
        body, name="mix_bwd_point", grid=(t // tm,),
        in_specs=[pl.BlockSpec((4, tm, half), lambda i: (0, i, 0)), unit(2), unit(3),
                  pl.BlockSpec((tm, c), lambda i: (i, 0)), vec, vec, vec, vec, sq, sq,
                  pl.BlockSpec((CHUNK, groups), lambda i: (0, 0))],
        out_specs=[pl.BlockSpec((2, tm, c), lambda i: (1, i, 0)), pl.BlockSpec((tm, c), lambda i: (i, 0)),
                   vec, vec, vec, vec, vec, sq, sq],
        out_shape=[jax.ShapeDtypeStruct((4, t, c), BF16), jax.ShapeDtypeStruct((t, c), F32),
                   vshape, vshape, vshape, vshape, vshape, sshape, sshape],
        compiler_params=_cparams(("arbitrary",)),
    )(dcat, z, z, a1, ln_a_g, ln_a_b, ln_v_g, ln_v_b, sp_w, sp_wt, sp_bt)


def mix_bwd_conv(dz, da1, z, conv_w, seq):
    _, t, c = z.shape
    tm = _tile(seq, 512)
    tiles_per_seq = seq // tm
    hb = tm // HALO
    n_halo_blocks = t // HALO

    rc = _tile(tm, CONV_ROWS)

    def body(dz_in_ref, d_ref, dh_ref, av_ref, ag_ref, cw_ref, dz_ref, dcw_ref, d1_ref, sh_ref, part_ref):
        del dz_in_ref
        i = pl.program_id(0)

        @pl.when(i == 0)
        def _():
            part_ref[...] = jnp.zeros_like(part_ref)

        d1_ref[0:tm, :] = d_ref[...]
        d1_ref[tm:, :] = jnp.where((i + 1) % tiles_per_seq == 0, 0.0, dh_ref[...])
        taps = _shifted_taps(d1_ref, sh_ref, tm, 0)

        def chunk(ci, carry):
            r0 = pl.multiple_of(ci * rc, rc)
            av = av_ref[0, pl.ds(r0, rc), :].astype(F32)
            s = _sigmoid(ag_ref[0, pl.ds(r0, rc), :].astype(F32))
            a0 = av * s
            da0 = jnp.zeros((rc, c), F32)
            for b, ro, back in taps:
                k = CONV_WIDTH - 1 - back
                rows = sh_ref[b, pl.ds(r0 + ro, rc), :]
                da0 = da0 + cw_ref[k:k + 1, :] * rows
                prod = a0 * rows
                part_ref[k] += functools.reduce(lambda p, q: p + q, [prod[8 * r:8 * r + 8] for r in range(rc // 8)])
            dz_ref[0, pl.ds(r0, rc), :] = (da0 * s).astype(BF16)
            dz_ref[1, pl.ds(r0, rc), :] = (da0 * av * s * (1.0 - s)).astype(BF16)
            return carry

        lax.fori_loop(0, tm // rc, chunk, 0)

        @pl.when(i == pl.num_programs(0) - 1)
        def _():
            dcw_ref[...] = jnp.sum(part_ref[...], axis=1)

    unit = lambda u: pl.BlockSpec((1, tm, c), lambda i: (u, i, 0))
    return pl.pallas_call(
        body, name="mix_bwd_conv", grid=(t // tm,),
        in_specs=[pl.BlockSpec(memory_space=pl.ANY), pl.BlockSpec((tm, c), lambda i: (i, 0)),
                  pl.BlockSpec((HALO, c), lambda i: (jnp.minimum((i + 1) * hb, n_halo_blocks - 1), 0)),
                  unit(0), unit(1), pl.BlockSpec((CONV_WIDTH, c), lambda i: (0, 0))],
        out_specs=[pl.BlockSpec((2, tm, c), lambda i: (0, i, 0)), pl.BlockSpec((CONV_WIDTH, c), lambda i: (0, 0))],
        out_shape=[jax.ShapeDtypeStruct(dz.shape, BF16), jax.ShapeDtypeStruct((CONV_WIDTH, c), F32)],
        scratch_shapes=[pltpu.VMEM((tm + HALO, c), F32), pltpu.VMEM((8, tm + HALO, c), F32),
                        pltpu.VMEM((CONV_WIDTH, 8, c), F32)],
        input_output_aliases={0: 0},
        compiler_params=_cparams(("arbitrary",)),
    )(dz, da1, da1, z, z, conv_w)


CHIP_FLIPS = ((1, 0), (0, 1), (1, 1))
ANY = pl.BlockSpec(memory_space=pl.ANY)


def _place():
    return lax.axis_index("x"), lax.axis_index("y"), lax.axis_index("c")


def _flip(v, f):
    return 1 - v if f else v


def place_shard(w, chip, dtype, name):
    n_layers, r, cc = w.shape
    rb = _tile(r, 512)

    def body(chip_ref, w_ref, *o_refs):
        del chip_ref
        for layer, o_ref in enumerate(o_refs):
            o_ref[0] = w_ref[layer].astype(dtype)

    return pl.pallas_call(
        body, name=name,
        grid_spec=pltpu.PrefetchScalarGridSpec(
            num_scalar_prefetch=1, grid=(r // rb,),
            in_specs=[pl.BlockSpec((n_layers, rb, cc), lambda i, chip_ref: (0, i, 0))],
            out_specs=[pl.BlockSpec((1, rb, cc), lambda i, chip_ref: (chip_ref[0], i, 0))] * n_layers),
        out_shape=[jax.ShapeDtypeStruct((N_CHIPS, r, cc), dtype)] * n_layers,
        compiler_params=_cparams(("parallel",)),
    )(chip, w)


class Carry:
    def __init__(self, arrays, out_shapes, aliased, sem_shapes, start, finish):
        self.arrays, self.out_shapes, self.aliased, self.sem_shapes = list(arrays), list(out_shapes), aliased, list(sem_shapes)
        self.start, self.finish = start, finish


def _call(body, *, name, grid, in_specs, out_specs, out_shape, args, sem, scratch_shapes=(), carry=None):
    if carry is None:
        res = pl.pallas_call(body, name=name, grid=grid, in_specs=in_specs, out_specs=out_specs, out_shape=out_shape,
                             scratch_shapes=list(scratch_shapes), compiler_params=_cparams(sem))(*args)
        return list(res), []
    n_in, n_out, n_scr, nc = len(args), len(out_shape), len(scratch_shapes), len(carry.arrays)

    def full_body(*refs):
        ins, refs = refs[:n_in], refs[n_in:]
        c_ins, refs = refs[:nc], refs[nc:]
        outs, refs = refs[:n_out], refs[n_out:]
        c_outs, refs = refs[:nc], refs[nc:]
        scr, sems = refs[:n_scr], refs[n_scr:]
        first = functools.reduce(lambda a, b: a & b, [pl.program_id(d) == 0 for d in range(len(grid))])
        last = functools.reduce(lambda a, b: a & b, [pl.program_id(d) == grid[d] - 1 for d in range(len(grid))])

        @pl.when(first)
        def _():
            carry.start(c_ins, c_outs, sems)

        body(*ins, *outs, *scr)

        @pl.when(last)
        def _():
            carry.finish(c_ins, c_outs, sems)

    res = pl.pallas_call(
        full_body, name=name, grid=grid, in_specs=list(in_specs) + [ANY] * nc, out_specs=list(out_specs) + [ANY] * nc,
        out_shape=list(out_shape) + carry.out_shapes, scratch_shapes=list(scratch_shapes) + carry.sem_shapes,
        input_output_aliases={n_in + i: n_out + i for i in range(nc)} if carry.aliased else {},
        compiler_params=pltpu.CompilerParams(dimension_semantics=("arbitrary",) * len(grid), vmem_limit_bytes=VMEM_LIMIT,
                                             has_side_effects=True),
    )(*args, *carry.arrays)
    return list(res[:n_out]), list(res[n_out:])


def _gather_ops(shapes, whole):
    n = len(shapes)

    def rows(a, c):
        hr = shapes[a][1] // 2
        return pl.ds(pl.multiple_of(c * hr, 16), hr)

    def start(ins, outs, sems):
        ici_send, ici_recv = sems[0], sems[1]
        x, y, c = _place()
        k = 2 * x + y
        for a in range(n):
            for o, (fx, fy) in enumerate(CHIP_FLIPS):
                src = ins[a].at[k] if whole[a] else ins[a].at[k, rows(a, c)]
                dst = outs[a].at[k] if whole[a] else outs[a].at[k, rows(a, c)]
                pltpu.make_async_remote_copy(
                    src_ref=src, dst_ref=dst, send_sem=ici_send.at[3 * a + o], recv_sem=ici_recv.at[3 * a + o],
                    device_id=(_flip(x, fx), _flip(y, fy), c), device_id_type=MESH).start()

    def finish(ins, outs, sems):
        ici_send, ici_recv, d2d_send, d2d_recv = sems
        x, y, c = _place()
        k = 2 * x + y
        sibling = (x, y, 1 - c)

        def copy(ref, send, recv, a, o):
            return pltpu.make_async_remote_copy(src_ref=ref, dst_ref=ref, send_sem=send.at[3 * a + o],
                                                recv_sem=recv.at[3 * a + o], device_id=sibling, device_id_type=MESH)

        for a in range(n):
            for o, (fx, fy) in enumerate(CHIP_FLIPS):
                kk = 2 * _flip(x, fx) + _flip(y, fy)
                landed = outs[a].at[kk] if whole[a] else outs[a].at[kk, rows(a, c)]
                copy(landed, ici_send, ici_recv, a, o).wait_recv()
                if not whole[a]:
                    copy(landed, d2d_send, d2d_recv, a, o).start()
        for a in range(n):
            for o, (fx, fy) in enumerate(CHIP_FLIPS):
                kk = 2 * _flip(x, fx) + _flip(y, fy)
                mine = ins[a].at[k] if whole[a] else ins[a].at[k, rows(a, c)]
                copy(mine, ici_send, ici_recv, a, o).wait_send()
                if not whole[a]:
                    copy(outs[a].at[kk, rows(a, 1 - c)], d2d_send, d2d_recv, a, o).wait_recv()
                    copy(outs[a].at[kk, rows(a, c)], d2d_send, d2d_recv, a, o).wait_send()

    dma = pltpu.SemaphoreType.DMA
    return start, finish, [dma((3 * n,))] * 4


def gather_carry(bufs):
    start, finish, sems = _gather_ops([b.shape for b in bufs], [False] * len(bufs))
    return Carry(bufs, [jax.ShapeDtypeStruct(b.shape, b.dtype) for b in bufs], True, sems, start, finish)


def allgather_weights(shards, smalls):
    bufs = list(shards) + list(smalls)
    n = len(bufs)
    start, finish, sems = _gather_ops([b.shape for b in bufs], [False] * len(shards) + [True] * len(smalls))

    def body(*refs):
        start(refs[:n], refs[n:2 * n], refs[2 * n:])
        finish(refs[:n], refs[n:2 * n], refs[2 * n:])

    res = pl.pallas_call(
        body, name="allgather_weights", in_specs=[ANY] * n, out_specs=[ANY] * n,
        out_shape=[jax.ShapeDtypeStruct(b.shape, b.dtype) for b in bufs], scratch_shapes=sems,
        input_output_aliases={i: i for i in range(n)},
        compiler_params=pltpu.CompilerParams(has_side_effects=True),
    )(*bufs)
    return res[:len(shards)], res[len(shards):]


def rs_exchange(grads):
    n = len(grads)

    def body(*refs):
        ins, outs = refs[:n], refs[n:2 * n]
        send, recv = refs[2 * n:]
        x, y, c = _place()
        cps = []
        for a in range(n):
            cp = pltpu.make_async_remote_copy(
                src_ref=ins[a].at[:, 1 - c], dst_ref=outs[a], send_sem=send.at[a], recv_sem=recv.at[a],
                device_id=(x, y, 1 - c), device_id_type=MESH)
            cp.start()
            cps.append(cp)
        for cp in cps:
            cp.wait()

    dma = pltpu.SemaphoreType.DMA
    return pl.pallas_call(
        body, name="rs_exchange", in_specs=[ANY] * n, out_specs=[ANY] * n,
        out_shape=[jax.ShapeDtypeStruct((g.shape[0],) + g.shape[2:], g.dtype) for g in grads],
        scratch_shapes=[dma((n,)), dma((n,))],
        compiler_params=pltpu.CompilerParams(has_side_effects=True),
    )(*grads)


def rs_add(gs, sibs, core, out_dtype, name):
    n = len(gs)
    nk = gs[0].shape[0]

    def body(core_ref, *refs):
        del core_ref
        for a in range(n):
            refs[2 * n + a][0] = (refs[a][0, 0] + refs[n + a][0]).astype(out_dtype)

    halves = [g.shape[2:] for g in gs]
    return pl.pallas_call(
        body, name=name,
        grid_spec=pltpu.PrefetchScalarGridSpec(
            num_scalar_prefetch=1, grid=(nk,),
            in_specs=[pl.BlockSpec((1, 1) + h, lambda k, core_ref: (k, core_ref[0], 0, 0)) for h in halves]
            + [pl.BlockSpec((1,) + h, lambda k, core_ref: (k, 0, 0)) for h in halves],
            out_specs=[pl.BlockSpec((1,) + h, lambda k, core_ref: (k, 0, 0)) for h in halves]),
        out_shape=[jax.ShapeDtypeStruct((nk,) + h, out_dtype) for h in halves],
        compiler_params=_cparams(("parallel",)),
    )(core, *gs, *sibs)


def send_carry(parts):
    n = len(parts)

    def copies(ins, outs, sems):
        x, y, c = _place()
        for a in range(n):
            for o, (fx, fy) in enumerate(CHIP_FLIPS):
                kk = 2 * _flip(x, fx) + _flip(y, fy)
                yield pltpu.make_async_remote_copy(
                    src_ref=ins[a].at[kk], dst_ref=outs[a].at[o], send_sem=sems[0].at[3 * a + o],
                    recv_sem=sems[1].at[3 * a + o], device_id=(_flip(x, fx), _flip(y, fy), c), device_id_type=MESH)

    def start(ins, outs, sems):
        for cp in copies(ins, outs, sems):
            cp.start()

    def finish(ins, outs, sems):
        for cp in copies(ins, outs, sems):
            cp.wait()

    dma = pltpu.SemaphoreType.DMA
    return Carry(parts, [jax.ShapeDtypeStruct((3,) + p.shape[1:], p.dtype) for p in parts], False,
                 [dma((3 * n,)), dma((3 * n,))], start, finish)


def rs_sum(recvs, parts, where, name):
    n_layers = len(recvs)
    _, hr, cc = recvs[0].shape
    rb = _tile(hr, 256)

    def body(where_ref, *refs):
        del where_ref
        o_ref = refs[-1]
        for layer in range(n_layers):
            r_ref, p_ref = refs[layer], refs[n_layers + layer]
            o_ref[layer, 0] = ((p_ref[0].astype(F32) + r_ref[0].astype(F32)) + r_ref[1].astype(F32)) + r_ref[2].astype(F32)

    return pl.pallas_call(
        body, name=name,
        grid_spec=pltpu.PrefetchScalarGridSpec(
            num_scalar_prefetch=1, grid=(hr // rb,),
            in_specs=[pl.BlockSpec((3, rb, cc), lambda i, w_ref: (0, i, 0))] * n_layers
            + [pl.BlockSpec((1, rb, cc), lambda i, w_ref: (w_ref[0], i, 0))] * n_layers,
            out_specs=pl.BlockSpec((n_layers, 1, rb, cc), lambda i, w_ref: (0, w_ref[1], i, 0))),
        out_shape=jax.ShapeDtypeStruct((n_layers, 2, hr, cc), F32),
        compiler_params=_cparams(("parallel",)),
    )(where, *recvs, *parts)


def rs_share(fulls):
    n = len(fulls)

    def body(*refs):
        ins, outs = refs[:n], refs[n:2 * n]
        send, recv = refs[2 * n:]
        x, y, c = _place()
        cps = []
        for a in range(n):
            cp = pltpu.make_async_remote_copy(
                src_ref=ins[a].at[:, c], dst_ref=outs[a].at[:, c], send_sem=send.at[a], recv_sem=recv.at[a],
                device_id=(x, y, 1 - c), device_id_type=MESH)
            cp.start()
            cps.append(cp)
        for a in range(n):
            got = outs[a].at[:, 1 - c]
            pltpu.make_async_remote_copy(
                src_ref=got, dst_ref=got, send_sem=send.at[a], recv_sem=recv.at[a],
                device_id=(x, y, 1 - c), device_id_type=MESH).wait_recv()
        for cp in cps:
            cp.wait_send()

    dma = pltpu.SemaphoreType.DMA
    return pl.pallas_call(
        body, name="rs_share", in_specs=[ANY] * n, out_specs=[ANY] * n,
        out_shape=[jax.ShapeDtypeStruct(f.shape, f.dtype) for f in fulls],
        scratch_shapes=[dma((n,)), dma((n,))],
        input_output_aliases={i: i for i in range(n)},
        compiler_params=pltpu.CompilerParams(has_side_effects=True),
    )(*fulls)


def allreduce_small(v):
    r, w = v.shape

    def body(v_ref, o_ref, buf, send, recv, loc):
        x, y, c = _place()
        me = 4 * x + 2 * y + c
        mine = pltpu.make_async_copy(v_ref, buf.at[me], loc)
        mine.start()
        cps = []
        for o in range(1, N_DEV):
            fx, fy, fc = (o >> 2) & 1, (o >> 1) & 1, o & 1
            cp = pltpu.make_async_remote_copy(
                src_ref=v_ref, dst_ref=buf.at[me], send_sem=send.at[o - 1], recv_sem=recv.at[o - 1],
                device_id=(_flip(x, fx), _flip(y, fy), _flip(c, fc)), device_id_type=MESH)
            cp.start()
            cps.append(cp)
        for o in range(1, N_DEV):
            fx, fy, fc = (o >> 2) & 1, (o >> 1) & 1, o & 1
            peer = 4 * _flip(x, fx) + 2 * _flip(y, fy) + _flip(c, fc)
            pltpu.make_async_remote_copy(
                src_ref=v_ref, dst_ref=buf.at[peer], send_sem=send.at[o - 1], recv_sem=recv.at[o - 1],
                device_id=(x, y, c), device_id_type=MESH).wait_recv()
        for cp in cps:
            cp.wait_send()
        mine.wait()
        acc = buf[0]
        for d in range(1, N_DEV):
            acc = acc + buf[d]
        o_ref[...] = acc

    dma = pltpu.SemaphoreType.DMA
    vm = pl.BlockSpec(memory_space=pltpu.VMEM)
    return pl.pallas_call(
        body, name="allreduce_small", in_specs=[vm], out_specs=vm,
        out_shape=jax.ShapeDtypeStruct((r, w), F32),
        scratch_shapes=[pltpu.VMEM((N_DEV, r, w), F32), dma((N_DEV - 1,)), dma((N_DEV - 1,)), dma],
        compiler_params=pltpu.CompilerParams(has_side_effects=True, vmem_limit_bytes=VMEM_LIMIT),
    )(v)


def adamw(w, g, m, v, name):
    r, cc = w.shape
    rb = _tile(r, 256)
    n = r // rb
    depth = min(STREAM_DEPTH, n)

    def body(w_hbm, g_hbm, m_hbm, v_hbm, d_hbm, nm_hbm, nv_hbm, ibuf, obuf, isem, osem):
        ins, outs = (w_hbm, g_hbm, m_hbm, v_hbm), (d_hbm, nm_hbm, nv_hbm)

        def rows(i):
            return pl.ds(pl.multiple_of(i * rb, 8), rb)

        def reads(i, slot):
            return [pltpu.make_async_copy(ins[a].at[rows(i)], ibuf.at[a, slot], isem.at[a, slot]) for a in range(4)]

        def writes(i, slot):
            return [pltpu.make_async_copy(obuf.at[a, slot], outs[a].at[rows(i)], osem.at[a, slot]) for a in range(3)]

        for s in range(depth):
            for cp in reads(s, s):
                cp.start()

        def step(i, carry):
            slot, oslot = i % depth, i % 2
            for cp in reads(i, slot):
                cp.wait()

            @pl.when(i >= 2)
            def _():
                for cp in writes(i - 2, oslot):
                    cp.wait()

            gv = ibuf[1, slot]
            nm = ADAM_B1 * ibuf[2, slot] + (1.0 - ADAM_B1) * gv
            nv = ADAM_B2 * ibuf[3, slot] + (1.0 - ADAM_B2) * (gv * gv)
            m_hat = nm / (1.0 - ADAM_B1 ** ADAM_STEP)
            v_hat = nv / (1.0 - ADAM_B2 ** ADAM_STEP)
            obuf[0, oslot] = -ADAM_LR * (m_hat / (jnp.sqrt(v_hat) + ADAM_EPS) + ADAM_WD * ibuf[0, slot])
            obuf[1, oslot] = nm
            obuf[2, oslot] = nv
            for cp in writes(i, oslot):
                cp.start()

            @pl.when(i + depth < n)
            def _():
                for cp in reads(i + depth, slot):
                    cp.start()

            return carry

        lax.fori_loop(0, n, step, 0)
        for i in range(max(n - 2, 0), n):
            for cp in writes(i, i % 2):
                cp.wait()

    shp = jax.ShapeDtypeStruct((r, cc), F32)
    dma = pltpu.SemaphoreType.DMA
    return pl.pallas_call(
        body, name=name, in_specs=[ANY] * 4, out_specs=[ANY] * 3, out_shape=[shp] * 3,
        scratch_shapes=[pltpu.VMEM((4, depth, rb, cc), F32), pltpu.VMEM((3, 2, rb, cc), F32), dma((4, depth)), dma((3, 2))],
        compiler_params=pltpu.CompilerParams(vmem_limit_bytes=VMEM_LIMIT),
    )(w, g, m, v)


WEIGHTS = ['g_ffn1', 'w_ffn1_gate', 'w_ffn1_up', 'w_ffn1_down', 'g_mix', 'w_in_ab', 'conv_w', 'conv_b', 'ln_a_g',
           'ln_a_b', 'ln_v_g', 'ln_v_b', 'sp_w', 'sp_b', 'w_out_ab', 'w_qkv', 'w_o', 'g_ffn2', 'w_ffn2_gate',
           'w_ffn2_up', 'w_ffn2_down', 'g_final']
BIG = ['w_ffn1_gate', 'w_ffn1_up', 'w_ffn1_down', 'w_in_ab', 'w_out_ab', 'w_qkv', 'w_o', 'w_ffn2_gate', 'w_ffn2_up',
       'w_ffn2_down']
SMALL = ['g_ffn1', 'g_mix', 'g_ffn2', 'g_final', 'conv_b', 'ln_a_g', 'ln_a_b', 'ln_v_g', 'ln_v_b', 'sp_b', 'sp_w']
HIDDEN_MAJOR = ['w_ffn1_gate', 'w_ffn1_up', 'w_ffn2_gate', 'w_ffn2_up']


CARRY_WEIGHTS = {"ffn_gateup": 9.2e6, "ffn_down": 6.1e6, "mm_in": 5.9e6, "mm_out": 3.3e6}


def _use_order(depth):
    order = []
    for layer in range(depth):
        order += [('w_ffn1_gate', layer), ('w_ffn1_up', layer), ('w_ffn1_down', layer)]
        order += [('w_in_ab', layer // 2), ('w_out_ab', layer // 2)] if layer % 2 == 0 else [('w_qkv', layer // 2), ('w_o', layer // 2)]
        order += [('w_ffn2_gate', layer), ('w_ffn2_up', layer), ('w_ffn2_down', layer)]
    return order


def _rows(a):
    return a.reshape(-1, LANES)


def _pack(parts):
    v = jnp.concatenate([_rows(p) for p in parts], axis=0)
    pad = (-v.shape[0]) % 8
    return jnp.pad(v, ((0, pad), (0, 0)))


def _unpack(v, shapes):
    out, r = [], 0
    for s in shapes:
        n = 1
        for d in s:
            n *= d
        n //= LANES
        out.append(v[r:r + n].reshape(s))
        r += n
    return out


def kernel(x, g_ffn1, w_ffn1_gate, w_ffn1_up, w_ffn1_down, g_mix, w_in_ab, conv_w, conv_b, ln_a_g, ln_a_b, ln_v_g, ln_v_b, sp_w, sp_b, w_out_ab, w_qkv, w_o, g_ffn2, w_ffn2_gate, w_ffn2_up, w_ffn2_down, g_final, loss_target, m_g_ffn1, m_w_ffn1_gate, m_w_ffn1_up, m_w_ffn1_down, m_g_mix, m_w_in_ab, m_conv_w, m_conv_b, m_ln_a_g, m_ln_a_b, m_ln_v_g, m_ln_v_b, m_sp_w, m_sp_b, m_w_out_ab, m_w_qkv, m_w_o, m_g_ffn2, m_w_ffn2_gate, m_w_ffn2_up, m_w_ffn2_down, m_g_final, v_g_ffn1, v_w_ffn1_gate, v_w_ffn1_up, v_w_ffn1_down, v_g_mix, v_w_in_ab, v_conv_w, v_conv_b, v_ln_a_g, v_ln_a_b, v_ln_v_g, v_ln_v_b, v_sp_w, v_sp_b, v_w_out_ab, v_w_qkv, v_w_o, v_g_ffn2, v_w_ffn2_gate, v_w_ffn2_up, v_w_ffn2_down, v_g_final):
    p = dict(locals())
    for name in HIDDEN_MAJOR:
        for pre in ('', 'm_', 'v_'):
            p[pre + name] = jnp.swapaxes(p[pre + name], 1, 2)
    back = lambda name, a: jnp.swapaxes(a, 1, 2) if name in HIDDEN_MAJOR else a
    n_seq, seq, d = x.shape
    t = n_seq * seq
    depth = g_ffn1.shape[0]
    core = lax.axis_index("c")
    chip = 2 * lax.axis_index("x") + lax.axis_index("y")
    xf = x.reshape(t, d)
    target = loss_target.reshape(t, d)

    items = []
    for name in BIG:
        for layer in range(p[name].shape[0]):
            items.append((name, layer))
    chip1 = chip.reshape(1).astype(jnp.int32)
    placed = {}
    for name in BIG:
        for layer, buf in enumerate(place_shard(p[name], chip1, BF16, "place_shard")):
            placed[(name, layer)] = buf
    first = [('w_ffn1_gate', 0), ('w_ffn1_up', 0)]
    gathered, (conv_w4,) = allgather_weights([placed[it] for it in first],
                                             place_shard(conv_w, chip1, F32, "place_conv_w"))
    wt = dict(zip(first, gathered))
    waiting = [it for it in _use_order(depth) if it not in wt]

    def riders(name):
        room, take = CARRY_WEIGHTS[name], []
        for it in list(waiting):
            if placed[it].size <= room:
                room -= placed[it].size
                take.append(it)
                waiting.remove(it)
        return (take, gather_carry([placed[it] for it in take])) if take else (take, None)

    def landed(take, carried):
        wt.update(zip(take, carried))

    def weight(it):
        if it not in wt:
            waiting.remove(it)
            (wt[it],), _ = allgather_weights([placed[it]], [])
        return wt[it]

    c_mix = conv_w4.shape[2] * N_CHIPS
    conv_full = jnp.transpose(conv_w4, (1, 0, 2)).reshape(CONV_WIDTH, c_mix)
    vec = lambda a: a.reshape(1, -1)
    sp_bt = sp_b[0].T
    sp_wt = jnp.transpose(sp_w[0], (0, 2, 1))
    d_ff = w_ffn1_gate.shape[2]
    n_in = w_in_ab.shape[2]
    n_qkv = w_qkv.shape[2] // 3

    saved = []
    xc = xf
    h = rmsnorm_fwd(xc, vec(g_ffn1[0]), "norm_first")
    for layer in range(depth):
        s = {}
        for half, (gn, wn) in enumerate((('g_ffn1', 'w_ffn1'), ('g_ffn2', 'w_ffn2'))):
            if half == 1:
                s['x_mix'], s['h_mix'] = xc, h
                if layer % 2 == 0:
                    w_in = weight(('w_in_ab', layer // 2))
                    take, carry = riders("mm_in")
                    (z,), got = colmm(h, [w_in], n_in, BF16, "mm_in", carry)
                    landed(take, got)
                    cat, a1 = mix_fwd(z, conv_full, conv_b, ln_a_g, ln_a_b, vec(ln_v_g), vec(ln_v_b), sp_w[0], sp_bt, seq)
                    s.update(z=z, cat=cat, a1=a1)
                    w_out = weight(('w_out_ab', layer // 2))
                    take, carry = riders("mm_out")
                    (xc, h), got = rowmm(cat, w_out, xc, 1.0, "mm_out", carry, vec(g_ffn2[layer]))
                    landed(take, got)
                else:
                    (qkv,), _ = colmm(h, [weight(('w_qkv', layer // 2))], n_qkv, BF16, "mm_qkv")
                    o, tot, cnt = attn_fwd(qkv, n_seq, seq)
                    s.update(qkv=qkv, o=o, tot=tot, cnt=cnt)
                    (xc, h), _ = rowmm(o, weight(('w_o', layer // 2)), xc, 1.0, "mm_o", None, vec(g_ffn2[layer]))
            s['x' + wn] = xc
            w_gate, w_up = weight((wn + '_gate', layer)), weight((wn + '_up', layer))
            take, carry = riders("ffn_gateup")
            (silu, udsilu, act), got = colmm(h, [w_gate, w_up], d_ff, BF16, "ffn_gateup", carry, swiglu=True)
            landed(take, got)
            s.update({'h' + wn: h, 'swiglu' + wn: (silu, udsilu), 'act' + wn: act})
            w_down = weight((wn + '_down', layer))
            take, carry = riders("ffn_down")
            following = g_mix[layer] if half == 0 else (g_ffn1[layer + 1] if layer + 1 < depth else None)
            (xc, h), got = rowmm(act, w_down, xc, 0.5, "ffn_down", carry, None if following is None else vec(following))
            landed(take, got)
        saved.append(s)

    loss8, dx, dxb, dg_final = loss_head(xc, vec(g_final), target)
    loss = lax.psum(loss8[0, 0], ("x", "y", "c"))

    gw = {}
    gs = {}
    core1 = core.reshape(1).astype(jnp.int32)
    ready = []
    part, recv = {}, {}

    def leaving():
        its = list(ready)
        ready.clear()
        halves = lambda a: a.reshape(N_CHIPS, 2, a.shape[1] // 2, a.shape[2])
        theirs = rs_exchange([halves(gw[it][1]) for it in its])
        sums = rs_add([halves(gw[it][0]) for it in its], theirs, core1, REDUCE_DTYPE, "rs_add")
        part.update(zip(its, sums))
        return its, send_carry(sums)

    for layer in reversed(range(depth)):
        s = saved[layer]
        for half, (gn, wn) in reversed(list(enumerate((('g_ffn1', 'w_ffn1'), ('g_ffn2', 'w_ffn2'))))):
            wd = wt[(wn + '_down', layer)]
            dgate, dup = rowmm_t(dxb, wd, 0.5, BF16, "ffn_bwd_act", swiglu=s['swiglu' + wn])
            gw[(wn + '_down', layer)] = dw_row(s['act' + wn], dxb, 0.5, "ffn_dw_down")
            gw[(wn + '_gate', layer)], gw[(wn + '_up', layer)] = dw_col(s['h' + wn], [dgate, dup], N_CHIPS, d_ff,
                                                                        "ffn_dw_gateup", transposed=True)
            ready.extend([(wn + '_down', layer), (wn + '_gate', layer), (wn + '_up', layer)])
            its, carry = leaving()
            (dx, dxb, dg), got = colmm_t([dgate, dup], [wt[(wn + '_gate', layer)], wt[(wn + '_up', layer)]], d_ff,
                                         s['x' + wn], vec(p[gn][layer]), dx, "ffn_bwd_in", carry, transposed=True)
            recv.update(zip(its, got))
            gs[(gn, layer)] = dg
            if half == 1:
                if layer % 2 == 0:
                    i = layer // 2
                    w_out = wt[('w_out_ab', i)]
                    dcat = rowmm_t(dxb, w_out, 1.0, F32, "mm_out_t")
                    gw[('w_out_ab', i)] = dw_row(s['cat'], dxb, 1.0, "dw_out")
                    dz, da1, dcb, dlag, dlab, dlvg, dlvb, dspw, dspb = mix_bwd_point(
                        dcat, s['z'], s['a1'], ln_a_g, ln_a_b, vec(ln_v_g), vec(ln_v_b), sp_w[0], sp_wt, sp_bt, seq)
                    dz, dcw = mix_bwd_conv(dz, da1, s['z'], conv_full, seq)
                    gs.update({('conv_b', i): dcb, ('ln_a_g', i): dlag, ('ln_a_b', i): dlab, ('ln_v_g', i): dlvg,
                               ('ln_v_b', i): dlvb, ('sp_w', i): dspw, ('sp_b', i): dspb[:, :, 0], ('conv_w', i): dcw})
                    (gw[('w_in_ab', i)],) = dw_col(s['h_mix'], [dz], N_CHIPS, n_in, "dw_in")
                    ready.extend([('w_out_ab', i), ('w_in_ab', i)])
                    its, carry = leaving()
                    (dx, dxb, dg), got = colmm_t([dz], [wt[('w_in_ab', i)]], n_in, s['x_mix'], vec(g_mix[layer]), dx,
                                                 "mm_in_t", carry)
                    recv.update(zip(its, got))
                else:
                    i = layer // 2
                    w_o4 = wt[('w_o', i)]
                    do = rowmm_t(dxb, w_o4, 1.0, BF16, "mm_o_t")
                    gw[('w_o', i)] = dw_row(s['o'], dxb, 1.0, "dw_o")
                    dq, dk, dv = attn_bwd(s['qkv'], do, s['tot'], s['cnt'], n_seq, seq)
                    dqkv = jnp.concatenate([dq, dk, dv], axis=0)
                    (gw[('w_qkv', i)],) = dw_col(s['h_mix'], [dqkv], N_CHIPS, n_qkv, "dw_qkv")
                    ready.extend([('w_o', i), ('w_qkv', i)])
                    its, carry = leaving()
                    (dx, dxb, dg), got = colmm_t([dqkv], [wt[('w_qkv', i)]], n_qkv, s['x_mix'], vec(g_mix[layer]), dx,
                                                 "mm_qkv_t", carry)
                    recv.update(zip(its, got))
                gs[('g_mix', layer)] = dg
    grad_x = dx.reshape(x.shape)

    assert not ready and set(recv) == set(items)
    where = jnp.stack([chip, core]).astype(jnp.int32)
    fulls = []
    for name in BIG:
        its = [(name, layer) for layer in range(p[name].shape[0])]
        fulls.append(rs_sum([recv[it] for it in its], [part[it] for it in its], where, "rs_sum"))
    shared = rs_share(fulls)
    grads = {name: sh.reshape(p[name].shape) for name, sh in zip(BIG, shared)}

    stack = lambda name: jnp.concatenate([gs[(name, layer)].reshape((1,) + p[name].shape[1:]) for layer in range(p[name].shape[0])], axis=0)
    small_g = [stack(name) if name != 'g_final' else dg_final.reshape(p[name].shape) for name in SMALL]
    packed = _pack(small_g + [gs[('conv_w', 0)]])
    red = allreduce_small(packed)
    outs = _unpack(red, [p[name].shape for name in SMALL] + [(CONV_WIDTH, c_mix)])
    for name, g in zip(SMALL, outs[:-1]):
        grads[name] = g
    conv_g = outs[-1].reshape(CONV_WIDTH, N_CHIPS, c_mix // N_CHIPS)
    grads['conv_w'] = lax.dynamic_index_in_dim(conv_g, chip, axis=1, keepdims=False).reshape(conv_w.shape)

    delta, new_m, new_v = {}, {}, {}
    for name in BIG:
        shp = p[name].shape
        two = lambda a: a.reshape(shp[0] * shp[1], shp[2])
        dl, nm, nv = adamw(two(p[name]), two(grads[name]), two(p['m_' + name]), two(p['v_' + name]), "adamw")
        delta[name], new_m[name], new_v[name] = dl.reshape(shp), nm.reshape(shp), nv.reshape(shp)
    small_names = SMALL + ['conv_w']
    pk = lambda pre: _pack([p[pre + name] for name in small_names])
    dl, nm, nv = adamw(pk(''), _pack([grads[name] for name in small_names]), pk('m_'), pk('v_'), "adamw_small")
    shapes = [p[name].shape for name in small_names]
    for dst, val in ((delta, dl), (new_m, nm), (new_v, nv)):
        for name, a in zip(small_names, _unpack(val, shapes)):
            dst[name] = a

    return (loss, grad_x, *[back(n, d[n]) for d in (grads, delta, new_m, new_v) for n in WEIGHTS])
```

```python
import functools

import jax
import jax.numpy as jnp
from jax import lax
from jax.experimental import pallas as pl
from jax.experimental.pallas import tpu as pltpu

F32 = jnp.float32
BF16 = jnp.bfloat16
EPS = 1e-6
HEAD_DIM = 64
CONV_WIDTH = 31
CHUNK = 128
KBLK = 128
ATT_BLOCK = 256
ATT_LANES = 256
DW_TOKENS = 2048
CONV_ROWS = 64
STREAM_DEPTH = 3
MASKED = -1e30
STICK_GONE = -110.0
LANES = 128
HALO = 32
ADAM_LR, ADAM_B1, ADAM_B2, ADAM_EPS, ADAM_WD, ADAM_STEP = 0.001, 0.9, 0.999, 1e-08, 0.01, 10
VMEM_LIMIT = 56 * 1024 * 1024
MESH = pl.DeviceIdType.MESH
N_CHIPS = 4
N_DEV = 8
REDUCE_DTYPE = BF16


def _cparams(sem):
    return pltpu.CompilerParams(dimension_semantics=sem, vmem_limit_bytes=VMEM_LIMIT)


def _nt(a, b):
    return lax.dot_general(a, b, (((1,), (1,)), ((), ())), preferred_element_type=F32)


def _tn(a, b):
    return lax.dot_general(a, b, (((0,), (0,)), ((), ())), preferred_element_type=F32)


def _nn(a, b):
    return jnp.dot(a, b, preferred_element_type=F32)


def _sigmoid(x):
    return 0.5 * jnp.tanh(0.5 * x) + 0.5


def _tile(t, want):
    if t <= want:
        return t
    for cand in range(want - want % 8, 7, -8):
        if t % cand == 0:
            return cand
    raise ValueError((t, want))


def rmsnorm_fwd(x, g, name):
    t, d = x.shape
    tm = _tile(t, 512)

    def body(x_ref, g_ref, h_ref):
        xv = x_ref[...]
        r = lax.rsqrt(jnp.mean(xv * xv, axis=-1, keepdims=True) + EPS)
        h_ref[...] = (xv * r * g_ref[...]).astype(BF16)

    return pl.pallas_call(
        body, name=name, grid=(t // tm,),
        in_specs=[pl.BlockSpec((tm, d), lambda i: (i, 0)), pl.BlockSpec((1, d), lambda i: (0, 0))],
        out_specs=pl.BlockSpec((tm, d), lambda i: (i, 0)),
        out_shape=jax.ShapeDtypeStruct((t, d), BF16),
        compiler_params=_cparams(("parallel",)),
    )(x, g)


def colmm(h, ws, nu, out_dtype, name, carry=None, swiglu=False):
    t, k = h.shape
    j, nj = (ws[0].shape[0], ws[0].shape[1]) if swiglu else (ws[0].shape[0], ws[0].shape[2])
    per = nj // nu
    units = j * per
    tm = _tile(t, 1024)
    nw = len(ws)
    n_out = 3 if swiglu else nw

    def body(*refs):
        h_ref = refs[0]
        hv = h_ref[...]
        if swiglu:
            silu_ref, udsilu_ref, act_ref = refs[1 + nw:]
            gv = _nt(hv, refs[1][0])
            uv = _nt(hv, refs[2][0])
            s = _sigmoid(gv)
            silu = gv * s
            silu_ref[0] = silu.astype(out_dtype)
            udsilu_ref[0] = (uv * (s + silu * (1.0 - s))).astype(out_dtype)
            act_ref[0] = (silu * uv).astype(out_dtype)
            return
        for n in range(nw):
            for s in range(j):
                res = _nn(hv, refs[1 + n][s]).astype(out_dtype)
                for u in range(per):
                    refs[1 + nw + n][s * per + u] = res[:, u * nu:(u + 1) * nu]

    out_shape = [jax.ShapeDtypeStruct((units, t, nu), out_dtype)] * n_out
    if swiglu:
        assert nw == 2 and per == 1
        return _call(
            body, name=name, grid=(j, t // tm),
            in_specs=[pl.BlockSpec((tm, k), lambda s, i: (i, 0))] + [pl.BlockSpec((1, nj, k), lambda s, i: (s, 0, 0))] * nw,
            out_specs=[pl.BlockSpec((1, tm, nu), lambda s, i: (s, i, 0))] * n_out, out_shape=out_shape,
            args=[h, *ws], sem=("parallel", "parallel"), carry=carry)
    return _call(
        body, name=name, grid=(t // tm,),
        in_specs=[pl.BlockSpec((tm, k), lambda i: (i, 0))] + [pl.BlockSpec((j, k, nj), lambda i: (0, 0, 0))] * nw,
        out_specs=[pl.BlockSpec((units, tm, nu), lambda i: (0, i, 0))] * n_out, out_shape=out_shape,
        args=[h, *ws], sem=("parallel",), carry=carry)


def rowmm(a, w, resid, scale, name, carry=None, norm_g=None):
    u_n, t, ku = a.shape
    n = w.shape[2]
    tm = _tile(t, 512)

    def body(a_ref, w_ref, r_ref, *rest):
        acc = jnp.zeros((tm, n), F32)
        for u in range(u_n):
            acc = acc + _nn(a_ref[u], w_ref[u])
        out = r_ref[...] + scale * acc
        if norm_g is None:
            (o_ref,) = rest
        else:
            g_ref, o_ref, h_ref = rest
            r = lax.rsqrt(jnp.mean(out * out, axis=-1, keepdims=True) + EPS)
            h_ref[...] = (out * r * g_ref[...]).astype(BF16)
        o_ref[...] = out

    row = pl.BlockSpec((tm, n), lambda i: (i, 0))
    normed = norm_g is not None
    outs, carried = _call(
        body, name=name, grid=(t // tm,),
        in_specs=[pl.BlockSpec((u_n, tm, ku), lambda i: (0, i, 0)), pl.BlockSpec((u_n, ku, n), lambda i: (0, 0, 0)),
                  row] + [pl.BlockSpec((1, n), lambda i: (0, 0))] * normed,
        out_specs=[row] + [row] * normed,
        out_shape=[jax.ShapeDtypeStruct((t, n), F32)] + [jax.ShapeDtypeStruct((t, n), BF16)] * normed,
        args=[a, w, resid] + [norm_g] * normed, sem=("parallel",), carry=carry)
    return (outs[0], outs[1] if normed else None), carried


def rowmm_t(dyb, w, scale, out_dtype, name, swiglu=None):
    t, n = dyb.shape
    u_n, ku, _ = w.shape
    tm = _tile(t, 512)

    if swiglu is None:
        def body(dy_ref, w_ref, o_ref):
            dy = dy_ref[...]
            for u in range(u_n):
                o_ref[u] = (scale * _nt(dy, w_ref[u])).astype(out_dtype)

        return pl.pallas_call(
            body, name=name, grid=(t // tm,),
            in_specs=[pl.BlockSpec((tm, n), lambda i: (i, 0)), pl.BlockSpec((u_n, ku, n), lambda i: (0, 0, 0))],
            out_specs=pl.BlockSpec((u_n, tm, ku), lambda i: (0, i, 0)),
            out_shape=jax.ShapeDtypeStruct((u_n, t, ku), out_dtype),
            compiler_params=_cparams(("parallel",)),
        )(dyb, w)

    steps = t // tm
    depth = min(STREAM_DEPTH, steps)

    def body(dy_hbm, w_hbm, silu_hbm, ud_hbm, dg_hbm, du_hbm, wbuf, dybuf, fbuf, obuf, wsem, dysem, fsem, osem):
        facs, outs = (silu_hbm, ud_hbm), (dg_hbm, du_hbm)

        def rows(i):
            return pl.ds(pl.multiple_of(i * tm, 16), tm)

        def reads(i, slot):
            return ([pltpu.make_async_copy(dy_hbm.at[rows(i)], dybuf.at[slot], dysem.at[slot])]
                    + [pltpu.make_async_copy(facs[a].at[:, rows(i)], fbuf.at[a, slot], fsem.at[a, slot]) for a in range(2)])

        def writes(i, slot):
            return [pltpu.make_async_copy(obuf.at[a, slot], outs[a].at[:, rows(i)], osem.at[a, slot]) for a in range(2)]

        weights = pltpu.make_async_copy(w_hbm, wbuf, wsem)
        weights.start()
        for s in range(depth):
            for cp in reads(s, s):
                cp.start()
        weights.wait()

        def step(i, carry):
            slot, oslot = i % depth, i % 2
            for cp in reads(i, slot):
                cp.wait()

            @pl.when(i >= 2)
            def _():
                for cp in writes(i - 2, oslot):
                    cp.wait()

            dy = dybuf[slot]
            for u in range(u_n):
                dact = scale * _nt(dy, wbuf[u])
                obuf[0, oslot, u] = (dact * fbuf[1, slot, u].astype(F32)).astype(BF16)
                obuf[1, oslot, u] = (dact * fbuf[0, slot, u].astype(F32)).astype(BF16)
            for cp in writes(i, oslot):
                cp.start()

            @pl.when(i + depth < steps)
            def _():
                for cp in reads(i + depth, slot):
                    cp.start()

            return carry

        lax.fori_loop(0, steps, step, 0)
        for i in range(max(steps - 2, 0), steps):
            for cp in writes(i, i % 2):
                cp.wait()

    dma = pltpu.SemaphoreType.DMA
    return pl.pallas_call(
        body, name=name, in_specs=[ANY] * 4, out_specs=[ANY] * 2,
        out_shape=[jax.ShapeDtypeStruct((u_n, t, ku), BF16)] * 2,
        scratch_shapes=[pltpu.VMEM((u_n, ku, n), BF16), pltpu.VMEM((depth, tm, n), BF16),
                        pltpu.VMEM((2, depth, u_n, tm, ku), BF16), pltpu.VMEM((2, 2, u_n, tm, ku), BF16),
                        dma, dma((depth,)), dma((2, depth)), dma((2, 2))],
        compiler_params=pltpu.CompilerParams(vmem_limit_bytes=VMEM_LIMIT),
    )(dyb, w, *swiglu)


def colmm_t(dzs, ws, nu, x, g, dy_in, name, carry=None, transposed=False):
    t, k = x.shape
    j, nj = (ws[0].shape[0], ws[0].shape[1]) if transposed else (ws[0].shape[0], ws[0].shape[2])
    per = nj // nu
    units = j * per
    nw = len(ws)
    tm = _tile(t, 512)
    assert not transposed or per == 1

    def body(*refs):
        dz_refs = refs[:nw]
        w_refs = refs[nw:2 * nw]
        x_ref, g_ref, dy_ref, dx_ref, dxb_ref, dg_ref = refs[2 * nw:]
        i = pl.program_id(0)
        dh = jnp.zeros((tm, k), F32)
        for n in range(nw):
            for u in range(units):
                if transposed:
                    dh = dh + _nn(dz_refs[n][u], w_refs[n][u])
                else:
                    wv = w_refs[n][u // per, :, (u % per) * nu:(u % per + 1) * nu]
                    dh = dh + _nt(dz_refs[n][u], wv)
        xv = x_ref[...]
        gv = g_ref[...]
        r = lax.rsqrt(jnp.mean(xv * xv, axis=-1, keepdims=True) + EPS)
        uu = dh * gv
        dx = dy_ref[...] + r * uu - xv * (r * r * r * jnp.mean(uu * xv, axis=-1, keepdims=True))
        dx_ref[...] = dx
        dxb_ref[...] = dx.astype(BF16)
        part = jnp.sum(dh * (xv * r), axis=0, keepdims=True)

        @pl.when(i == 0)
        def _():
            dg_ref[...] = part

        @pl.when(i > 0)
        def _():
            dg_ref[...] += part

    dz_spec = pl.BlockSpec((units, tm, nu), lambda i: (0, i, 0))
    w_spec = pl.BlockSpec((j, nj, k) if transposed else (j, k, nj), lambda i: (0, 0, 0))
    row = pl.BlockSpec((tm, k), lambda i: (i, 0))
    vec = pl.BlockSpec((1, k), lambda i: (0, 0))
    return _call(
        body, name=name, grid=(t // tm,),
        in_specs=[dz_spec] * nw + [w_spec] * nw + [row, vec, row],
        out_specs=[row, row, vec],
        out_shape=[jax.ShapeDtypeStruct((t, k), F32), jax.ShapeDtypeStruct((t, k), BF16),
                   jax.ShapeDtypeStruct((1, k), F32)],
        args=[*dzs, *ws, x, g, dy_in], sem=("arbitrary",), carry=carry)


def dw_col(h, dzs, j, nu, name, transposed=False):
    t, k = h.shape
    units = dzs[0].shape[0]
    per = units // j
    nw = len(dzs)
    tt = _tile(t, DW_TOKENS)
    assert not transposed or per == 1

    def body(*refs):
        h_ref = refs[0]
        s = pl.program_id(1)
        hv = h_ref[...]
        outs, copies = refs[1 + nw:1 + 2 * nw], refs[1 + 2 * nw:]

        @pl.when(s == 0)
        def _():
            for o_ref in outs:
                o_ref[...] = jnp.zeros_like(o_ref)

        for n in range(nw):
            if transposed:
                outs[n][0] += _tn(refs[1 + n][0], hv)
                continue
            for u in range(per):
                outs[n][0, :, u * nu:(u + 1) * nu] += _tn(hv, refs[1 + n][u])

        @pl.when(s == pl.num_programs(1) - 1)
        def _():
            for o_ref, c_ref in zip(outs, copies):
                c_ref[...] = o_ref[...].astype(REDUCE_DTYPE)

    shard = (nu, k) if transposed else (k, per * nu)
    o_spec = pl.BlockSpec((1,) + shard, lambda u, s: (u, 0, 0))
    res = pl.pallas_call(
        body, name=name, grid=(j, t // tt),
        in_specs=[pl.BlockSpec((tt, k), lambda u, s: (s, 0))] + [pl.BlockSpec((per, tt, nu), lambda u, s: (u, s, 0))] * nw,
        out_specs=[o_spec] * (2 * nw),
        out_shape=[jax.ShapeDtypeStruct((j,) + shard, F32)] * nw + [jax.ShapeDtypeStruct((j,) + shard, REDUCE_DTYPE)] * nw,
        compiler_params=_cparams(("parallel", "arbitrary")),
    )(h, *dzs)
    return list(zip(res[:nw], res[nw:]))


def dw_row(a, dyb, scale, name):
    u_n, t, ku = a.shape
    n = dyb.shape[1]
    tt = _tile(t, DW_TOKENS)

    per_step = u_n if u_n * ku <= n else 1

    def body(a_ref, dy_ref, o_ref, c_ref):
        @pl.when(pl.program_id(1) == 0)
        def _():
            o_ref[...] = jnp.zeros_like(o_ref)

        dy = dy_ref[...]
        for u in range(per_step):
            o_ref[u] += scale * _tn(a_ref[u], dy)

        @pl.when(pl.program_id(1) == pl.num_programs(1) - 1)
        def _():
            c_ref[...] = o_ref[...].astype(REDUCE_DTYPE)

    o_spec = pl.BlockSpec((per_step, ku, n), lambda u, s: (u, 0, 0))
    return tuple(pl.pallas_call(
        body, name=name, grid=(u_n // per_step, t // tt),
        in_specs=[pl.BlockSpec((per_step, tt, ku), lambda u, s: (u, s, 0)), pl.BlockSpec((tt, n), lambda u, s: (s, 0))],
        out_specs=[o_spec, o_spec],
        out_shape=[jax.ShapeDtypeStruct((u_n, ku, n), F32), jax.ShapeDtypeStruct((u_n, ku, n), REDUCE_DTYPE)],
        compiler_params=_cparams(("parallel", "arbitrary")),
    )(a, dyb))


def loss_head(x, g, target):
    t, d = x.shape
    tm = _tile(t, 256)

    def body(x_ref, g_ref, t_ref, loss_ref, dx_ref, dxb_ref, dg_ref):
        i = pl.program_id(0)
        xv = x_ref[...]
        gv = g_ref[...]
        r = lax.rsqrt(jnp.mean(xv * xv, axis=-1, keepdims=True) + EPS)
        xh = xv * r
        err = xh * gv - t_ref[...]
        dy = err * (1.0 / d)
        uu = dy * gv
        dx = r * uu - xv * (r * r * r * jnp.mean(uu * xv, axis=-1, keepdims=True))
        dx_ref[...] = dx
        dxb_ref[...] = dx.astype(BF16)
        dg_part = jnp.sum(dy * xh, axis=0, keepdims=True)
        row = jnp.sum(err * err, axis=-1, keepdims=True) * (0.5 / d)
        l_part = jnp.zeros((8, LANES), F32) + jnp.sum(row, axis=0, keepdims=True)

        @pl.when(i == 0)
        def _():
            dg_ref[...] = dg_part
            loss_ref[...] = l_part

        @pl.when(i > 0)
        def _():
            dg_ref[...] += dg_part
            loss_ref[...] += l_part

    row = pl.BlockSpec((tm, d), lambda i: (i, 0))
    vec = pl.BlockSpec((1, d), lambda i: (0, 0))
    return pl.pallas_call(
        body, name="loss_head", grid=(t // tm,),
        in_specs=[row, vec, row],
        out_specs=[pl.BlockSpec((8, LANES), lambda i: (0, 0)), row, row, vec],
        out_shape=[jax.ShapeDtypeStruct((8, LANES), F32), jax.ShapeDtypeStruct((t, d), F32),
                   jax.ShapeDtypeStruct((t, d), BF16), jax.ShapeDtypeStruct((1, d), F32)],
        compiler_params=_cparams(("arbitrary",)),
    )(x, g, target)


def _split(v):
    hi = v.astype(BF16)
    lo = (v - hi.astype(F32)).astype(BF16)
    return hi, lo


def _keysums(v, m_ext):
    hi, lo = _split(v)
    outs = []
    for j in range(v.shape[1] // KBLK):
        sl = slice(j * KBLK, (j + 1) * KBLK)
        cs = _nn(jnp.concatenate([hi[:, sl], lo[:, sl]], axis=1), m_ext)
        outs.append((cs[:, :KBLK], cs[:, KBLK:]))
    return outs


def _softplus_parts(z):
    sp = jnp.maximum(z, 0.0) + jnp.log(1.0 + jnp.exp(-jnp.abs(z)))
    return sp, z - sp


def _sum_matrices():
    r = lax.broadcasted_iota(jnp.int32, (2 * KBLK, 2 * KBLK), 0) % KBLK
    c = lax.broadcasted_iota(jnp.int32, (2 * KBLK, 2 * KBLK), 1)
    suffix = jnp.where((r > c) | (c >= KBLK), 1.0, 0.0).astype(BF16)
    prefix = jnp.where((r <= c) | (c >= KBLK), 1.0, 0.0).astype(BF16)
    return suffix, prefix


def _att_geometry(qkv, seq):
    upp = qkv.shape[0] // 3
    bq = min(ATT_BLOCK, seq)
    per_unit = (2 * LANES) // ATT_LANES
    return upp, bq, seq // bq, bq // KBLK, per_unit, upp * per_unit, ATT_LANES // HEAD_DIM


def _head_lanes(rows, heads):
    lane = lax.broadcasted_iota(jnp.int32, (rows, ATT_LANES), 1)
    return [(lane >= HEAD_DIM * h) & (lane < HEAD_DIM * (h + 1)) for h in range(heads)]


def attn_fwd(qkv, n_seq, seq):
    t = qkv.shape[1]
    upp, bq, nq, nsub, per_unit, groups, heads = _att_geometry(qkv, seq)
    suffix_m, _ = _sum_matrices()

    def body(q_ref, k_ref, v_ref, m_ref, o_ref, tot_ref, cnt_ref):
        qi = pl.program_id(2)
        step_id = (pl.program_id(0) * groups + pl.program_id(1)) * nq + qi
        in_head = _head_lanes(bq, heads)
        only = lambda v, h: jnp.where(in_head[h], v, jnp.zeros_like(v))
        q_all = q_ref[0] * jnp.asarray(HEAD_DIM ** -0.5, BF16)
        qs = [only(q_all, h) for h in range(heads)]
        m_ext = m_ref[...]
        row = lax.broadcasted_iota(jnp.int32, (bq, bq), 0)
        col = lax.broadcasted_iota(jnp.int32, (bq, bq), 1)
        diag_mask = col < row

        def block(kj, carry, mask):
            off = pl.multiple_of(kj * bq, bq)
            k_all = k_ref[0, pl.ds(off, bq), :]
            v_all = v_ref[0, pl.ds(off, bq), :]
            rems, acc = carry
            out = []
            for h in range(heads):
                rem = rems[h]
                z = _nt(qs[h], k_all)
                if mask is not None:
                    z = jnp.where(mask, z, MASKED)
                sp, ls = _softplus_parts(z)
                sums = _keysums(-sp, m_ext)
                parts = [None] * nsub
                for j in reversed(range(nsub)):
                    suf, total = sums[j]
                    parts[j] = jnp.exp(ls[:, j * KBLK:(j + 1) * KBLK] + suf + rem)
                    rem = rem + total
                a = jnp.concatenate(parts, axis=1)
                acc = acc + _nn(a.astype(BF16), only(v_all, h))
                out.append(rem)
            return tuple(out), acc

        def most_left(c):
            return functools.reduce(jnp.maximum, [jnp.max(r) for r in c[0]])

        def more(s):
            return (s[0] < qi) & (s[1] > STICK_GONE)

        def step(s):
            c = block(qi - 1 - s[0], s[2], None)
            return s[0] + 1, most_left(c), c

        zero = jnp.zeros((bq, LANES), F32)
        carry = block(qi, ((zero,) * heads, jnp.zeros((bq, ATT_LANES), F32)), diag_mask)
        n_left, _, (rems, acc) = lax.while_loop(more, step, (jnp.int32(0), most_left(carry), carry))
        o_ref[0] = acc.astype(BF16)
        first = lax.broadcasted_iota(jnp.int32, (bq, LANES), 1) < HEAD_DIM
        tot_ref[...] = jnp.concatenate([jnp.where(first, rems[h], rems[h + 1]) for h in range(0, heads, 2)], axis=1)
        cnt_ref[step_id] = n_left.astype(F32)

    qblk = lambda b, g, i: (g // per_unit, b * nq + i, g % per_unit)
    return pl.pallas_call(
        body, name="attn_fwd", grid=(n_seq, groups, nq),
        in_specs=[pl.BlockSpec((1, bq, ATT_LANES), qblk),
                  pl.BlockSpec((1, seq, ATT_LANES), lambda b, g, i: (upp + g // per_unit, b, g % per_unit)),
                  pl.BlockSpec((1, seq, ATT_LANES), lambda b, g, i: (2 * upp + g // per_unit, b, g % per_unit)),
                  pl.BlockSpec((2 * KBLK, 2 * KBLK), lambda b, g, i: (0, 0))],
        out_specs=[pl.BlockSpec((1, bq, ATT_LANES), qblk),
                   pl.BlockSpec((bq, ATT_LANES), lambda b, g, i: (b * nq + i, g)),
                   pl.BlockSpec(memory_space=pltpu.SMEM)],
        out_shape=[jax.ShapeDtypeStruct((upp, t, 2 * LANES), BF16), jax.ShapeDtypeStruct((t, upp * 2 * LANES), F32),
                   jax.ShapeDtypeStruct((n_seq * groups * nq,), F32)],
        compiler_params=_cparams(("arbitrary", "arbitrary", "arbitrary")),
    )(qkv, qkv, qkv, suffix_m)


def attn_bwd(qkv, do, tot, cnt, n_seq, seq):
    t = qkv.shape[1]
    upp, bq, nq, nsub, per_unit, groups, heads = _att_geometry(qkv, seq)
    _, prefix_m = _sum_matrices()
    scale = HEAD_DIM ** -0.5

    def body(q_ref, k_ref, v_ref, do_ref, tot_ref, m_ref, cnt_ref, dq_ref, dk_ref, dv_ref, dk_acc, dv_acc):
        qi = pl.program_id(2)
        step_id = (pl.program_id(0) * groups + pl.program_id(1)) * nq + qi
        n_left = jnp.clip(cnt_ref[step_id].astype(jnp.int32), 0, qi)
        in_head = _head_lanes(bq, heads)
        only = lambda v, h: jnp.where(in_head[h], v, jnp.zeros_like(v))
        q_all = q_ref[0] * jnp.asarray(scale, BF16)
        do_all = do_ref[0]
        qs = [only(q_all, h) for h in range(heads)]
        dos = [only(do_all, h) for h in range(heads)]
        first = lax.broadcasted_iota(jnp.int32, (bq, LANES), 1) < HEAD_DIM
        tots = []
        for h in range(0, heads, 2):
            both = tot_ref[:, h // 2 * LANES:(h // 2 + 1) * LANES]
            swapped = pltpu.roll(both, HEAD_DIM, 1)
            tots += [jnp.where(first, both, swapped), jnp.where(first, swapped, both)]
        m_ext = m_ref[...]
        row = lax.broadcasted_iota(jnp.int32, (bq, bq), 0)
        col = lax.broadcasted_iota(jnp.int32, (bq, bq), 1)
        diag_mask = col < row

        @pl.when(qi == 0)
        def _():
            dk_acc[...] = jnp.zeros_like(dk_acc)
            dv_acc[...] = jnp.zeros_like(dv_acc)

        def block(kj, carry, mask):
            off = pl.multiple_of(kj * bq, bq)
            k_all = k_ref[0, pl.ds(off, bq), :]
            v_all = v_ref[0, pl.ds(off, bq), :]
            pres, gpres, dq = carry
            dk_part = jnp.zeros((bq, ATT_LANES), F32)
            dv_part = jnp.zeros((bq, ATT_LANES), F32)
            pres_out, gpres_out = [], []
            for h in range(heads):
                pre, gpre = pres[h], gpres[h]
                z = _nt(qs[h], k_all)
                if mask is not None:
                    z = jnp.where(mask, z, MASKED)
                sp, ls = _softplus_parts(z)
                sums = _keysums(-sp, m_ext)
                parts = []
                for j in range(nsub):
                    pin, ptot = sums[j]
                    parts.append(jnp.exp(ls[:, j * KBLK:(j + 1) * KBLK] + (tots[h] - (pre + pin))))
                    pre = pre + ptot
                a = jnp.concatenate(parts, axis=1)
                g = a * _nt(dos[h], v_all)
                gsums = _keysums(g, m_ext)
                parts = []
                for j in range(nsub):
                    gin, gtot = gsums[j]
                    parts.append(gpre + gin)
                    gpre = gpre + gtot
                dz = g - jnp.exp(ls) * jnp.concatenate(parts, axis=1)
                dzb = dz.astype(BF16)
                dq = dq + _nn(dzb, only(k_all, h))
                dk_part = dk_part + _tn(dzb, qs[h])
                dv_part = dv_part + _tn(a.astype(BF16), dos[h])
                pres_out.append(pre)
                gpres_out.append(gpre)
            dk_acc[pl.ds(off, bq), :] += dk_part
            dv_acc[pl.ds(off, bq), :] += dv_part
            return tuple(pres_out), tuple(gpres_out), dq

        zero = jnp.zeros((bq, LANES), F32)
        carry = ((zero,) * heads, (zero,) * heads, jnp.zeros((bq, ATT_LANES), F32))
        carry = lax.fori_loop(qi - n_left, qi, lambda kj, c: block(kj, c, None), carry)
        carry = block(qi, carry, diag_mask)
        dq_ref[0] = (carry[2] * scale).astype(BF16)

        @pl.when(qi == nq - 1)
        def _():
            dk_ref[0] = dk_acc[...].astype(BF16)
            dv_ref[0] = dv_acc[...].astype(BF16)

    qblk = lambda b, g, i: (g // per_unit, b * nq + i, g % per_unit)
    kv_out = pl.BlockSpec((1, seq, ATT_LANES), lambda b, g, i: (g // per_unit, b, g % per_unit))
    shp = jax.ShapeDtypeStruct((upp, t, 2 * LANES), BF16)
    return pl.pallas_call(
        body, name="attn_bwd", grid=(n_seq, groups, nq),
        in_specs=[pl.BlockSpec((1, bq, ATT_LANES), qblk),
                  pl.BlockSpec((1, seq, ATT_LANES), lambda b, g, i: (upp + g // per_unit, b, g % per_unit)),
                  pl.BlockSpec((1, seq, ATT_LANES), lambda b, g, i: (2 * upp + g // per_unit, b, g % per_unit)),
                  pl.BlockSpec((1, bq, ATT_LANES), qblk),
                  pl.BlockSpec((bq, ATT_LANES), lambda b, g, i: (b * nq + i, g)),
                  pl.BlockSpec((2 * KBLK, 2 * KBLK), lambda b, g, i: (0, 0)),
                  pl.BlockSpec(memory_space=pltpu.SMEM)],
        out_specs=[pl.BlockSpec((1, bq, ATT_LANES), qblk), kv_out, kv_out],
        out_shape=[shp, shp, shp],
        scratch_shapes=[pltpu.VMEM((seq, ATT_LANES), F32), pltpu.VMEM((seq, ATT_LANES), F32)],
        compiler_params=_cparams(("parallel", "parallel", "arbitrary")),
    )(qkv, qkv, qkv, do, tot, prefix_m, cnt)


def _ln_stats(v):
    mu = jnp.mean(v, axis=-1, keepdims=True)
    vc = v - mu
    rstd = lax.rsqrt(jnp.mean(vc * vc, axis=-1, keepdims=True) + EPS)
    return vc * rstd, rstd


def _glu_into(a0_ref, av_ref, ag_ref, hv_ref, hg_ref, first):
    hv = hv_ref[0].astype(F32)
    hg = hg_ref[0].astype(F32)
    a0_ref[0:HALO, :] = jnp.where(first, 0.0, hv * _sigmoid(hg))
    av = av_ref[0].astype(F32)
    ag = ag_ref[0].astype(F32)
    a0_ref[HALO:, :] = av * _sigmoid(ag)


def _shifted_taps(ref, shifted_ref, tm, first):
    taps = []
    for b in range(8):
        offs = [o for o in range(first, first + CONV_WIDTH) if o % 8 == b]
        n_rows = max(offs) - b + tm
        shifted_ref[b, 0:n_rows, :] = ref[pl.ds(b, n_rows), :]
        taps += [(b, o - b, o - first) for o in offs]
    return taps


def _tril_mask():
    r = lax.broadcasted_iota(jnp.int32, (CHUNK, CHUNK), 0)
    c = lax.broadcasted_iota(jnp.int32, (CHUNK, CHUNK), 1)
    return c <= r


def mix_fwd(z, conv_w, conv_b, ln_a_g, ln_a_b, ln_v_g, ln_v_b, sp_w, sp_bt, seq):
    _, t, c = z.shape
    tm = _tile(seq, 512)
    tiles_per_seq = seq // tm
    groups = c // LANES
    hb = tm // HALO

    def body(av_ref, ag_ref, u_ref, v_ref, hv_ref, hg_ref, cw_ref, cb_ref, lag_ref, lab_ref, lvg_ref, lvb_ref,
             spw_ref, spb_ref, cat_ref, a1_ref, a0_ref, sh_ref):
        i = pl.program_id(0)
        _glu_into(a0_ref, av_ref, ag_ref, hv_ref, hg_ref, i % tiles_per_seq == 0)
        acc = jnp.zeros((tm, c), F32) + cb_ref[...]
        for b, ro, k in _shifted_taps(a0_ref, sh_ref, tm, HALO - (CONV_WIDTH - 1)):
            acc = acc + cw_ref[k:k + 1, :] * sh_ref[b, pl.ds(ro, tm), :]
        a1_ref[...] = acc
        xh, _ = _ln_stats(acc)
        a2 = xh * lag_ref[...] + lab_ref[...]
        a3 = (a2 * _sigmoid(a2)).astype(BF16)
        half = c // 2
        cat_ref[0] = a3[:, :half]
        cat_ref[1] = a3[:, half:]
        tril = _tril_mask()
        for g in range(groups):
            sl = slice(g * LANES, (g + 1) * LANES)
            xh, _ = _ln_stats(v_ref[0][:, sl].astype(F32))
            vn = (xh * lvg_ref[:, sl] + lvb_ref[:, sl]).astype(BF16)
            w = jnp.where(tril, spw_ref[g], 0.0).astype(BF16)
            bias = spb_ref[:, g:g + 1]
            for ch in range(tm // CHUNK):
                rows = slice(ch * CHUNK, (ch + 1) * CHUNK)
                vs = _nn(w, vn[rows]) + bias
                bo = (u_ref[0][rows, sl].astype(F32) * vs).astype(BF16)
                cat_ref[2 + (g * LANES) // half, rows, (g * LANES) % half:(g * LANES) % half + LANES] = bo

    unit = lambda u: pl.BlockSpec((1, tm, c), lambda i: (u, i, 0))
    halo = lambda u: pl.BlockSpec((1, HALO, c), lambda i: (u, jnp.maximum(i * hb - 1, 0), 0))
    vec = pl.BlockSpec((1, c), lambda i: (0, 0))
    return pl.pallas_call(
        body, name="mix_fwd", grid=(t // tm,),
        in_specs=[unit(0), unit(1), unit(2), unit(3), halo(0), halo(1),
                  pl.BlockSpec((CONV_WIDTH, c), lambda i: (0, 0)), vec, vec, vec, vec, vec,
                  pl.BlockSpec((groups, CHUNK, CHUNK), lambda i: (0, 0, 0)),
                  pl.BlockSpec((CHUNK, groups), lambda i: (0, 0))],
        out_specs=[pl.BlockSpec((4, tm, c // 2), lambda i: (0, i, 0)), pl.BlockSpec((tm, c), lambda i: (i, 0))],
        out_shape=[jax.ShapeDtypeStruct((4, t, c // 2), BF16), jax.ShapeDtypeStruct((t, c), F32)],
        scratch_shapes=[pltpu.VMEM((HALO + tm, c), F32), pltpu.VMEM((8, HALO + tm, c), F32)],
        compiler_params=_cparams(("parallel",)),
    )(z, z, z, z, z, z, conv_w, conv_b, ln_a_g, ln_a_b, ln_v_g, ln_v_b, sp_w, sp_bt)


def mix_bwd_point(dcat, z, a1, ln_a_g, ln_a_b, ln_v_g, ln_v_b, sp_w, sp_wt, sp_bt, seq):
    _, t, c = z.shape
    tm = _tile(seq, 512)
    groups = c // LANES
    half = c // 2

    def body(dc_ref, u_ref, v_ref, a1_ref, lag_ref, lab_ref, lvg_ref, lvb_ref, spw_ref, spwt_ref, spb_ref,
             dz_ref, da1_ref, dcb_ref, dlag_ref, dlab_ref, dlvg_ref, dlvb_ref, dspw_ref, dspb_ref):
        i = pl.program_id(0)
        last = pl.num_programs(0) - 1

        @pl.when(i == 0)
        def _():
            for r in (dcb_ref, dlag_ref, dlab_ref, dlvg_ref, dlvb_ref, dspw_ref, dspb_ref):
                r[...] = jnp.zeros_like(r)

        da3 = jnp.concatenate([dc_ref[0], dc_ref[1]], axis=-1)
        xh, rstd = _ln_stats(a1_ref[...])
        a2 = xh * lag_ref[...] + lab_ref[...]
        s = _sigmoid(a2)
        da2 = da3 * (s * (1.0 + a2 * (1.0 - s)))
        dlag_ref[...] += jnp.sum(da2 * xh, axis=0, keepdims=True)
        dlab_ref[...] += jnp.sum(da2, axis=0, keepdims=True)
        dxh = da2 * lag_ref[...]
        da1 = rstd * (dxh - jnp.mean(dxh, axis=-1, keepdims=True) - xh * jnp.mean(dxh * xh, axis=-1, keepdims=True))
        da1_ref[...] = da1
        dcb_ref[...] += jnp.sum(da1, axis=0, keepdims=True)

        tril = _tril_mask()
        for g in range(groups):
            sl = slice(g * LANES, (g + 1) * LANES)
            xh, rstd = _ln_stats(v_ref[0][:, sl].astype(F32))
            lg = lvg_ref[:, sl]
            vnb = (xh * lg + lvb_ref[:, sl]).astype(BF16)
            w = jnp.where(tril, spw_ref[g], 0.0).astype(BF16)
            wt = jnp.where(tril.T, spwt_ref[g], 0.0).astype(BF16)
            bias = spb_ref[:, g:g + 1]
            dbo_all = dc_ref[2 + (g * LANES) // half][:, (g * LANES) % half:(g * LANES) % half + LANES]
            dvn_parts = []
            dw_acc = jnp.zeros((CHUNK, CHUNK), F32)
            db_acc = jnp.zeros((CHUNK, LANES), F32)
            for ch in range(tm // CHUNK):
                rows = slice(ch * CHUNK, (ch + 1) * CHUNK)
                vs = _nn(w, vnb[rows]) + bias
                dbo = dbo_all[rows]
                uv = u_ref[0][rows, sl].astype(F32)
                dz_ref[0, rows, sl] = (dbo * vs).astype(BF16)
                dvs = dbo * uv
                dvsb = dvs.astype(BF16)
                dvn_parts.append(_nn(wt, dvsb))
                dw_acc = dw_acc + _nt(dvsb, vnb[rows])
                db_acc = db_acc + dvs
            dvn = jnp.concatenate(dvn_parts, axis=0)
            dspw_ref[g] += jnp.where(tril, dw_acc, 0.0)
            dspb_ref[g] += db_acc
            dlvg_ref[:, sl] += jnp.sum(dvn * xh, axis=0, keepdims=True)
            dlvb_ref[:, sl] += jnp.sum(dvn, axis=0, keepdims=True)
            dxh = dvn * lg
            dv = rstd * (dxh - jnp.mean(dxh, axis=-1, keepdims=True) - xh * jnp.mean(dxh * xh, axis=-1, keepdims=True))
            dz_ref[1, :, sl] = dv.astype(BF16)

        @pl.when(i == last)
        def _():
            for g in range(groups):
                dspb_ref[g] = jnp.zeros((CHUNK, LANES), F32) + jnp.sum(dspb_ref[g], axis=-1, keepdims=True)

    unit = lambda u: pl.BlockSpec((1, tm, c), lambda i: (u, i, 0))
    vec = pl.BlockSpec((1, c), lambda i: (0, 0))
    sq = pl.BlockSpec((groups, CHUNK, CHUNK), lambda i: (0, 0, 0))
    vshape = jax.ShapeDtypeStruct((1, c), F32)
    sshape = jax.ShapeDtypeStruct((groups, CHUNK, CHUNK), F32)
    return pl.pallas_call(
        body, name="mix_bwd_point", grid=(t // tm,),
        in_specs=[pl.BlockSpec((4, tm, half), lambda i: (0, i, 0)), unit(2), unit(3),
                  pl.BlockSpec((tm, c), lambda i: (i, 0)), vec, vec, vec, vec, sq, sq,
                  pl.BlockSpec((CHUNK, groups), lambda i: (0, 0))],
        out_specs=[pl.BlockSpec((2, tm, c), lambda i: (1, i, 0)), pl.BlockSpec((tm, c), lambda i: (i, 0)),
                   vec, vec, vec, vec, vec, sq, sq],
        out_shape=[jax.ShapeDtypeStruct((4, t, c), BF16), jax.ShapeDtypeStruct((t, c), F32),
                   vshape, vshape, vshape, vshape, vshape, sshape, sshape],
        compiler_params=_cparams(("arbitrary",)),
    )(dcat, z, z, a1, ln_a_g, ln_a_b, ln_v_g, ln_v_b, sp_w, sp_wt, sp_bt)


def mix_bwd_conv(dz, da1, z, conv_w, seq):
    _, t, c = z.shape
    tm = _tile(seq, 512)
    tiles_per_seq = seq // tm
    hb = tm // HALO
    n_halo_blocks = t // HALO

    rc = _tile(tm, CONV_ROWS)

    def body(dz_in_ref, d_ref, dh_ref, av_ref, ag_ref, cw_ref, dz_ref, dcw_ref, d1_ref, sh_ref, part_ref):
        del dz_in_ref
        i = pl.program_id(0)

        @pl.when(i == 0)
        def _():
            part_ref[...] = jnp.zeros_like(part_ref)

        d1_ref[0:tm, :] = d_ref[...]
        d1_ref[tm:, :] = jnp.where((i + 1) % tiles_per_seq == 0, 0.0, dh_ref[...])
        taps = _shifted_taps(d1_ref, sh_ref, tm, 0)

        def chunk(ci, carry):
            r0 = pl.multiple_of(ci * rc, rc)
            av = av_ref[0, pl.ds(r0, rc), :].astype(F32)
            s = _sigmoid(ag_ref[0, pl.ds(r0, rc), :].astype(F32))
            a0 = av * s
            da0 = jnp.zeros((rc, c), F32)
            for b, ro, back in taps:
                k = CONV_WIDTH - 1 - back
                rows = sh_ref[b, pl.ds(r0 + ro, rc), :]
                da0 = da0 + cw_ref[k:k + 1, :] * rows
                prod = a0 * rows
                part_ref[k] += functools.reduce(lambda p, q: p + q, [prod[8 * r:8 * r + 8] for r in range(rc // 8)])
            dz_ref[0, pl.ds(r0, rc), :] = (da0 * s).astype(BF16)
            dz_ref[1, pl.ds(r0, rc), :] = (da0 * av * s * (1.0 - s)).astype(BF16)
            return carry

        lax.fori_loop(0, tm // rc, chunk, 0)

        @pl.when(i == pl.num_programs(0) - 1)
        def _():
            dcw_ref[...] = jnp.sum(part_ref[...], axis=1)

    unit = lambda u: pl.BlockSpec((1, tm, c), lambda i: (u, i, 0))
    return pl.pallas_call(
        body, name="mix_bwd_conv", grid=(t // tm,),
        in_specs=[pl.BlockSpec(memory_space=pl.ANY), pl.BlockSpec((tm, c), lambda i: (i, 0)),
                  pl.BlockSpec((HALO, c), lambda i: (jnp.minimum((i + 1) * hb, n_halo_blocks - 1), 0)),
                  unit(0), unit(1), pl.BlockSpec((CONV_WIDTH, c), lambda i: (0, 0))],
        out_specs=[pl.BlockSpec((2, tm, c), lambda i: (0, i, 0)), pl.BlockSpec((CONV_WIDTH, c), lambda i: (0, 0))],
        out_shape=[jax.ShapeDtypeStruct(dz.shape, BF16), jax.ShapeDtypeStruct((CONV_WIDTH, c), F32)],
        scratch_shapes=[pltpu.VMEM((tm + HALO, c), F32), pltpu.VMEM((8, tm + HALO, c), F32),
                        pltpu.VMEM((CONV_WIDTH, 8, c), F32)],
        input_output_aliases={0: 0},
        compiler_params=_cparams(("arbitrary",)),
    )(dz, da1, da1, z, z, conv_w)


CHIP_FLIPS = ((1, 0), (0, 1), (1, 1))
ANY = pl.BlockSpec(memory_space=pl.ANY)


def _place():
    return lax.axis_index("x"), lax.axis_index("y"), lax.axis_index("c")


def _flip(v, f):
    return 1 - v if f else v


def place_shard(w, chip, dtype, name):
    n_layers, r, cc = w.shape
    rb = _tile(r, 512)

    def body(chip_ref, w_ref, *o_refs):
        del chip_ref
        for layer, o_ref in enumerate(o_refs):
            o_ref[0] = w_ref[layer].astype(dtype)

    return pl.pallas_call(
        body, name=name,
        grid_spec=pltpu.PrefetchScalarGridSpec(
            num_scalar_prefetch=1, grid=(r // rb,),
            in_specs=[pl.BlockSpec((n_layers, rb, cc), lambda i, chip_ref: (0, i, 0))],
            out_specs=[pl.BlockSpec((1, rb, cc), lambda i, chip_ref: (chip_ref[0], i, 0))] * n_layers),
        out_shape=[jax.ShapeDtypeStruct((N_CHIPS, r, cc), dtype)] * n_layers,
        compiler_params=_cparams(("parallel",)),
    )(chip, w)


class Carry:
    def __init__(self, arrays, out_shapes, aliased, sem_shapes, start, finish):
        self.arrays, self.out_shapes, self.aliased, self.sem_shapes = list(arrays), list(out_shapes), aliased, list(sem_shapes)
        self.start, self.finish = start, finish


def _call(body, *, name, grid, in_specs, out_specs, out_shape, args, sem, scratch_shapes=(), carry=None):
    if carry is None:
        res = pl.pallas_call(body, name=name, grid=grid, in_specs=in_specs, out_specs=out_specs, out_shape=out_shape,
                             scratch_shapes=list(scratch_shapes), compiler_params=_cparams(sem))(*args)
        return list(res), []
    n_in, n_out, n_scr, nc = len(args), len(out_shape), len(scratch_shapes), len(carry.arrays)

    def full_body(*refs):
        ins, refs = refs[:n_in], refs[n_in:]
        c_ins, refs = refs[:nc], refs[nc:]
        outs, refs = refs[:n_out], refs[n_out:]
        c_outs, refs = refs[:nc], refs[nc:]
        scr, sems = refs[:n_scr], refs[n_scr:]
        first = functools.reduce(lambda a, b: a & b, [pl.program_id(d) == 0 for d in range(len(grid))])
        last = functools.reduce(lambda a, b: a & b, [pl.program_id(d) == grid[d] - 1 for d in range(len(grid))])

        @pl.when(first)
        def _():
            carry.start(c_ins, c_outs, sems)

        body(*ins, *outs, *scr)

        @pl.when(last)
        def _():
            carry.finish(c_ins, c_outs, sems)

    res = pl.pallas_call(
        full_body, name=name, grid=grid, in_specs=list(in_specs) + [ANY] * nc, out_specs=list(out_specs) + [ANY] * nc,
        out_shape=list(out_shape) + carry.out_shapes, scratch_shapes=list(scratch_shapes) + carry.sem_shapes,
        input_output_aliases={n_in + i: n_out + i for i in range(nc)} if carry.aliased else {},
        compiler_params=pltpu.CompilerParams(dimension_semantics=("arbitrary",) * len(grid), vmem_limit_bytes=VMEM_LIMIT,
                                             has_side_effects=True),
    )(*args, *carry.arrays)
    return list(res[:n_out]), list(res[n_out:])


def _gather_ops(shapes, whole):
    n = len(shapes)

    def rows(a, c):
        hr = shapes[a][1] // 2
        return pl.ds(pl.multiple_of(c * hr, 16), hr)

    def start(ins, outs, sems):
        ici_send, ici_recv = sems[0], sems[1]
        x, y, c = _place()
        k = 2 * x + y
        for a in range(n):
            for o, (fx, fy) in enumerate(CHIP_FLIPS):
                src = ins[a].at[k] if whole[a] else ins[a].at[k, rows(a, c)]
                dst = outs[a].at[k] if whole[a] else outs[a].at[k, rows(a, c)]
                pltpu.make_async_remote_copy(
                    src_ref=src, dst_ref=dst, send_sem=ici_send.at[3 * a + o], recv_sem=ici_recv.at[3 * a + o],
                    device_id=(_flip(x, fx), _flip(y, fy), c), device_id_type=MESH).start()

    def finish(ins, outs, sems):
        ici_send, ici_recv, d2d_send, d2d_recv = sems
        x, y, c = _place()
        k = 2 * x + y
        sibling = (x, y, 1 - c)

        def copy(ref, send, recv, a, o):
            return pltpu.make_async_remote_copy(src_ref=ref, dst_ref=ref, send_sem=send.at[3 * a + o],
                                                recv_sem=recv.at[3 * a + o], device_id=sibling, device_id_type=MESH)

        for a in range(n):
            for o, (fx, fy) in enumerate(CHIP_FLIPS):
                kk = 2 * _flip(x, fx) + _flip(y, fy)
                landed = outs[a].at[kk] if whole[a] else outs[a].at[kk, rows(a, c)]
                copy(landed, ici_send, ici_recv, a, o).wait_recv()
                if not whole[a]:
                    copy(landed, d2d_send, d2d_recv, a, o).start()
        for a in range(n):
            for o, (fx, fy) in enumerate(CHIP_FLIPS):
                kk = 2 * _flip(x, fx) + _flip(y, fy)
                mine = ins[a].at[k] if whole[a] else ins[a].at[k, rows(a, c)]
                copy(mine, ici_send, ici_recv, a, o).wait_send()
                if not whole[a]:
                    copy(outs[a].at[kk, rows(a, 1 - c)], d2d_send, d2d_recv, a, o).wait_recv()
                    copy(outs[a].at[kk, rows(a, c)], d2d_send, d2d_recv, a, o).wait_send()

    dma = pltpu.SemaphoreType.DMA
    return start, finish, [dma((3 * n,))] * 4


def gather_carry(bufs):
    start, finish, sems = _gather_ops([b.shape for b in bufs], [False] * len(bufs))
    return Carry(bufs, [jax.ShapeDtypeStruct(b.shape, b.dtype) for b in bufs], True, sems, start, finish)


def allgather_weights(shards, smalls):
    bufs = list(shards) + list(smalls)
    n = len(bufs)
    start, finish, sems = _gather_ops([b.shape for b in bufs], [False] * len(shards) + [True] * len(smalls))

    def body(*refs):
        start(refs[:n], refs[n:2 * n], refs[2 * n:])
        finish(refs[:n], refs[n:2 * n], refs[2 * n:])

    res = pl.pallas_call(
        body, name="allgather_weights", in_specs=[ANY] * n, out_specs=[ANY] * n,
        out_shape=[jax.ShapeDtypeStruct(b.shape, b.dtype) for b in bufs], scratch_shapes=sems,
        input_output_aliases={i: i for i in range(n)},
        compiler_params=pltpu.CompilerParams(has_side_effects=True),
    )(*bufs)
    return res[:len(shards)], res[len(shards):]


def rs_exchange(grads):
    n = len(grads)

    def body(*refs):
        ins, outs = refs[:n], refs[n:2 * n]
        send, recv = refs[2 * n:]
        x, y, c = _place()
        cps = []
        for a in range(n):
            cp = pltpu.make_async_remote_copy(
                src_ref=ins[a].at[:, 1 - c], dst_ref=outs[a], send_sem=send.at[a], recv_sem=recv.at[a],
                device_id=(x, y, 1 - c), device_id_type=MESH)
            cp.start()
            cps.append(cp)
        for cp in cps:
            cp.wait()

    dma = pltpu.SemaphoreType.DMA
    return pl.pallas_call(
        body, name="rs_exchange", in_specs=[ANY] * n, out_specs=[ANY] * n,
        out_shape=[jax.ShapeDtypeStruct((g.shape[0],) + g.shape[2:], g.dtype) for g in grads],
        scratch_shapes=[dma((n,)), dma((n,))],
        compiler_params=pltpu.CompilerParams(has_side_effects=True),
    )(*grads)


def rs_add(gs, sibs, core, out_dtype, name):
    n = len(gs)
    nk = gs[0].shape[0]

    def body(core_ref, *refs):
        del core_ref
        for a in range(n):
            refs[2 * n + a][0] = (refs[a][0, 0] + refs[n + a][0]).astype(out_dtype)

    halves = [g.shape[2:] for g in gs]
    return pl.pallas_call(
        body, name=name,
        grid_spec=pltpu.PrefetchScalarGridSpec(
            num_scalar_prefetch=1, grid=(nk,),
            in_specs=[pl.BlockSpec((1, 1) + h, lambda k, core_ref: (k, core_ref[0], 0, 0)) for h in halves]
            + [pl.BlockSpec((1,) + h, lambda k, core_ref: (k, 0, 0)) for h in halves],
            out_specs=[pl.BlockSpec((1,) + h, lambda k, core_ref: (k, 0, 0)) for h in halves]),
        out_shape=[jax.ShapeDtypeStruct((nk,) + h, out_dtype) for h in halves],
        compiler_params=_cparams(("parallel",)),
    )(core, *gs, *sibs)


def send_carry(parts):
    n = len(parts)

    def copies(ins, outs, sems):
        x, y, c = _place()
        for a in range(n):
            for o, (fx, fy) in enumerate(CHIP_FLIPS):
                kk = 2 * _flip(x, fx) + _flip(y, fy)
                yield pltpu.make_async_remote_copy(
                    src_ref=ins[a].at[kk], dst_ref=outs[a].at[o], send_sem=sems[0].at[3 * a + o],
                    recv_sem=sems[1].at[3 * a + o], device_id=(_flip(x, fx), _flip(y, fy), c), device_id_type=MESH)

    def start(ins, outs, sems):
        for cp in copies(ins, outs, sems):
            cp.start()

    def finish(ins, outs, sems):
        for cp in copies(ins, outs, sems):
            cp.wait()

    dma = pltpu.SemaphoreType.DMA
    return Carry(parts, [jax.ShapeDtypeStruct((3,) + p.shape[1:], p.dtype) for p in parts], False,
                 [dma((3 * n,)), dma((3 * n,))], start, finish)


def rs_sum(recvs, parts, where, name):
    n_layers = len(recvs)
    _, hr, cc = recvs[0].shape
    rb = _tile(hr, 256)

    def body(where_ref, *refs):
        del where_ref
        o_ref = refs[-1]
        for layer in range(n_layers):
            r_ref, p_ref = refs[layer], refs[n_layers + layer]
            o_ref[layer, 0] = ((p_ref[0].astype(F32) + r_ref[0].astype(F32)) + r_ref[1].astype(F32)) + r_ref[2].astype(F32)

    return pl.pallas_call(
        body, name=name,
        grid_spec=pltpu.PrefetchScalarGridSpec(
            num_scalar_prefetch=1, grid=(hr // rb,),
            in_specs=[pl.BlockSpec((3, rb, cc), lambda i, w_ref: (0, i, 0))] * n_layers
            + [pl.BlockSpec((1, rb, cc), lambda i, w_ref: (w_ref[0], i, 0))] * n_layers,
            out_specs=pl.BlockSpec((n_layers, 1, rb, cc), lambda i, w_ref: (0, w_ref[1], i, 0))),
        out_shape=jax.ShapeDtypeStruct((n_layers, 2, hr, cc), F32),
        compiler_params=_cparams(("parallel",)),
    )(where, *recvs, *parts)


def rs_share(fulls):
    n = len(fulls)

    def body(*refs):
        ins, outs = refs[:n], refs[n:2 * n]
        send, recv = refs[2 * n:]
        x, y, c = _place()
        cps = []
        for a in range(n):
            cp = pltpu.make_async_remote_copy(
                src_ref=ins[a].at[:, c], dst_ref=outs[a].at[:, c], send_sem=send.at[a], recv_sem=recv.at[a],
                device_id=(x, y, 1 - c), device_id_type=MESH)
            cp.start()
            cps.append(cp)
        for a in range(n):
            got = outs[a].at[:, 1 - c]
            pltpu.make_async_remote_copy(
                src_ref=got, dst_ref=got, send_sem=send.at[a], recv_sem=recv.at[a],
                device_id=(x, y, 1 - c), device_id_type=MESH).wait_recv()
        for cp in cps:
            cp.wait_send()

    dma = pltpu.SemaphoreType.DMA
    return pl.pallas_call(
        body, name="rs_share", in_specs=[ANY] * n, out_specs=[ANY] * n,
        out_shape=[jax.ShapeDtypeStruct(f.shape, f.dtype) for f in fulls],
        scratch_shapes=[dma((n,)), dma((n,))],
        input_output_aliases={i: i for i in range(n)},
        compiler_params=pltpu.CompilerParams(has_side_effects=True),
    )(*fulls)


def allreduce_small(v):
    r, w = v.shape

    def body(v_ref, o_ref, buf, send, recv, loc):
        x, y, c = _place()
        me = 4 * x + 2 * y + c
        mine = pltpu.make_async_copy(v_ref, buf.at[me], loc)
        mine.start()
        cps = []
        for o in range(1, N_DEV):
            fx, fy, fc = (o >> 2) & 1, (o >> 1) & 1, o & 1
            cp = pltpu.make_async_remote_copy(
                src_ref=v_ref, dst_ref=buf.at[me], send_sem=send.at[o - 1], recv_sem=recv.at[o - 1],
                device_id=(_flip(x, fx), _flip(y, fy), _flip(c, fc)), device_id_type=MESH)
            cp.start()
            cps.append(cp)
        for o in range(1, N_DEV):
            fx, fy, fc = (o >> 2) & 1, (o >> 1) & 1, o & 1
            peer = 4 * _flip(x, fx) + 2 * _flip(y, fy) + _flip(c, fc)
            pltpu.make_async_remote_copy(
                src_ref=v_ref, dst_ref=buf.at[peer], send_sem=send.at[o - 1], recv_sem=recv.at[o - 1],
                device_id=(x, y, c), device_id_type=MESH).wait_recv()
        for cp in cps:
            cp.wait_send()
        mine.wait()
        acc = buf[0]
        for d in range(1, N_DEV):
            acc = acc + buf[d]
        o_ref[...] = acc

    dma = pltpu.SemaphoreType.DMA
    vm = pl.BlockSpec(memory_space=pltpu.VMEM)
    return pl.pallas_call(
        body, name="allreduce_small", in_specs=[vm], out_specs=vm,
        out_shape=jax.ShapeDtypeStruct((r, w), F32),
        scratch_shapes=[pltpu.VMEM((N_DEV, r, w), F32), dma((N_DEV - 1,)), dma((N_DEV - 1,)), dma],
        compiler_params=pltpu.CompilerParams(has_side_effects=True, vmem_limit_bytes=VMEM_LIMIT),
    )(v)


def adamw(w, g, m, v, name):
    r, cc = w.shape
    rb = _tile(r, 256)
    n = r // rb
    depth = min(STREAM_DEPTH, n)

    def body(w_hbm, g_hbm, m_hbm, v_hbm, d_hbm, nm_hbm, nv_hbm, ibuf, obuf, isem, osem):
        ins, outs = (w_hbm, g_hbm, m_hbm, v_hbm), (d_hbm, nm_hbm, nv_hbm)

        def rows(i):
            return pl.ds(pl.multiple_of(i * rb, 8), rb)

        def reads(i, slot):
            return [pltpu.make_async_copy(ins[a].at[rows(i)], ibuf.at[a, slot], isem.at[a, slot]) for a in range(4)]

        def writes(i, slot):
            return [pltpu.make_async_copy(obuf.at[a, slot], outs[a].at[rows(i)], osem.at[a, slot]) for a in range(3)]

        for s in range(depth):
            for cp in reads(s, s):
                cp.start()

        def step(i, carry):
            slot, oslot = i % depth, i % 2
            for cp in reads(i, slot):
                cp.wait()

            @pl.when(i >= 2)
            def _():
                for cp in writes(i - 2, oslot):
                    cp.wait()

            gv = ibuf[1, slot]
            nm = ADAM_B1 * ibuf[2, slot] + (1.0 - ADAM_B1) * gv
            nv = ADAM_B2 * ibuf[3, slot] + (1.0 - ADAM_B2) * (gv * gv)
            m_hat = nm / (1.0 - ADAM_B1 ** ADAM_STEP)
            v_hat = nv / (1.0 - ADAM_B2 ** ADAM_STEP)
            obuf[0, oslot] = -ADAM_LR * (m_hat / (jnp.sqrt(v_hat) + ADAM_EPS) + ADAM_WD * ibuf[0, slot])
            obuf[1, oslot] = nm
            obuf[2, oslot] = nv
            for cp in writes(i, oslot):
                cp.start()

            @pl.when(i + depth < n)
            def _():
                for cp in reads(i + depth, slot):
                    cp.start()

            return carry

        lax.fori_loop(0, n, step, 0)
        for i in range(max(n - 2, 0), n):
            for cp in writes(i, i % 2):
                cp.wait()

    shp = jax.ShapeDtypeStruct((r, cc), F32)
    dma = pltpu.SemaphoreType.DMA
    return pl.pallas_call(
        body, name=name, in_specs=[ANY] * 4, out_specs=[ANY] * 3, out_shape=[shp] * 3,
        scratch_shapes=[pltpu.VMEM((4, depth, rb, cc), F32), pltpu.VMEM((3, 2, rb, cc), F32), dma((4, depth)), dma((3, 2))],
        compiler_params=pltpu.CompilerParams(vmem_limit_bytes=VMEM_LIMIT),
    )(w, g, m, v)


WEIGHTS = ['g_ffn1', 'w_ffn1_gate', 'w_ffn1_up', 'w_ffn1_down', 'g_mix', 'w_in_ab', 'conv_w', 'conv_b', 'ln_a_g',
           'ln_a_b', 'ln_v_g', 'ln_v_b', 'sp_w', 'sp_b', 'w_out_ab', 'w_qkv', 'w_o', 'g_ffn2', 'w_ffn2_gate',
           'w_ffn2_up', 'w_ffn2_down', 'g_final']
BIG = ['w_ffn1_gate', 'w_ffn1_up', 'w_ffn1_down', 'w_in_ab', 'w_out_ab', 'w_qkv', 'w_o', 'w_ffn2_gate', 'w_ffn2_up',
       'w_ffn2_down']
SMALL = ['g_ffn1', 'g_mix', 'g_ffn2', 'g_final', 'conv_b', 'ln_a_g', 'ln_a_b', 'ln_v_g', 'ln_v_b', 'sp_b', 'sp_w']
HIDDEN_MAJOR = ['w_ffn1_gate', 'w_ffn1_up', 'w_ffn2_gate', 'w_ffn2_up']


CARRY_WEIGHTS = {"ffn_gateup": 9.2e6, "ffn_down": 6.1e6, "mm_in": 5.9e6, "mm_out": 3.3e6}


def _use_order(depth):
    order = []
    for layer in range(depth):
        order += [('w_ffn1_gate', layer), ('w_ffn1_up', layer), ('w_ffn1_down', layer)]
        order += [('w_in_ab', layer // 2), ('w_out_ab', layer // 2)] if layer % 2 == 0 else [('w_qkv', layer // 2), ('w_o', layer // 2)]
        order += [('w_ffn2_gate', layer), ('w_ffn2_up', layer), ('w_ffn2_down', layer)]
    return order


def _rows(a):
    return a.reshape(-1, LANES)


def _pack(parts):
    v = jnp.concatenate([_rows(p) for p in parts], axis=0)
    pad = (-v.shape[0]) % 8
    return jnp.pad(v, ((0, pad), (0, 0)))


def _unpack(v, shapes):
    out, r = [], 0
    for s in shapes:
        n = 1
        for d in s:
            n *= d
        n //= LANES
        out.append(v[r:r + n].reshape(s))
        r += n
    return out


def kernel(x, g_ffn1, w_ffn1_gate, w_ffn1_up, w_ffn1_down, g_mix, w_in_ab, conv_w, conv_b, ln_a_g, ln_a_b, ln_v_g, ln_v_b, sp_w, sp_b, w_out_ab, w_qkv, w_o, g_ffn2, w_ffn2_gate, w_ffn2_up, w_ffn2_down, g_final, loss_target, m_g_ffn1, m_w_ffn1_gate, m_w_ffn1_up, m_w_ffn1_down, m_g_mix, m_w_in_ab, m_conv_w, m_conv_b, m_ln_a_g, m_ln_a_b, m_ln_v_g, m_ln_v_b, m_sp_w, m_sp_b, m_w_out_ab, m_w_qkv, m_w_o, m_g_ffn2, m_w_ffn2_gate, m_w_ffn2_up, m_w_ffn2_down, m_g_final, v_g_ffn1, v_w_ffn1_gate, v_w_ffn1_up, v_w_ffn1_down, v_g_mix, v_w_in_ab, v_conv_w, v_conv_b, v_ln_a_g, v_ln_a_b, v_ln_v_g, v_ln_v_b, v_sp_w, v_sp_b, v_w_out_ab, v_w_qkv, v_w_o, v_g_ffn2, v_w_ffn2_gate, v_w_ffn2_up, v_w_ffn2_down, v_g_final):
    p = dict(locals())
    for name in HIDDEN_MAJOR:
        for pre in ('', 'm_', 'v_'):
            p[pre + name] = jnp.swapaxes(p[pre + name], 1, 2)
    back = lambda name, a: jnp.swapaxes(a, 1, 2) if name in HIDDEN_MAJOR else a
    n_seq, seq, d = x.shape
    t = n_seq * seq
    depth = g_ffn1.shape[0]
    core = lax.axis_index("c")
    chip = 2 * lax.axis_index("x") + lax.axis_index("y")
    xf = x.reshape(t, d)
    target = loss_target.reshape(t, d)

    items = []
    for name in BIG:
        for layer in range(p[name].shape[0]):
            items.append((name, layer))
    chip1 = chip.reshape(1).astype(jnp.int32)
    placed = {}
    for name in BIG:
        for layer, buf in enumerate(place_shard(p[name], chip1, BF16, "place_shard")):
            placed[(name, layer)] = buf
    first = [('w_ffn1_gate', 0), ('w_ffn1_up', 0)]
    gathered, (conv_w4,) = allgather_weights([placed[it] for it in first],
                                             place_shard(conv_w, chip1, F32, "place_conv_w"))
    wt = dict(zip(first, gathered))
    waiting = [it for it in _use_order(depth) if it not in wt]

    def riders(name):
        room, take = CARRY_WEIGHTS[name], []
        for it in list(waiting):
            if placed[it].size <= room:
                room -= placed[it].size
                take.append(it)
                waiting.remove(it)
        return (take, gather_carry([placed[it] for it in take])) if take else (take, None)

    def landed(take, carried):
        wt.update(zip(take, carried))

    def weight(it):
        if it not in wt:
            waiting.remove(it)
            (wt[it],), _ = allgather_weights([placed[it]], [])
        return wt[it]

    c_mix = conv_w4.shape[2] * N_CHIPS
    conv_full = jnp.transpose(conv_w4, (1, 0, 2)).reshape(CONV_WIDTH, c_mix)
    vec = lambda a: a.reshape(1, -1)
    sp_bt = sp_b[0].T
    sp_wt = jnp.transpose(sp_w[0], (0, 2, 1))
    d_ff = w_ffn1_gate.shape[2]
    n_in = w_in_ab.shape[2]
    n_qkv = w_qkv.shape[2] // 3

    saved = []
    xc = xf
    h = rmsnorm_fwd(xc, vec(g_ffn1[0]), "norm_first")
    for layer in range(depth):
        s = {}
        for half, (gn, wn) in enumerate((('g_ffn1', 'w_ffn1'), ('g_ffn2', 'w_ffn2'))):
            if half == 1:
                s['x_mix'], s['h_mix'] = xc, h
                if layer % 2 == 0:
                    w_in = weight(('w_in_ab', layer // 2))
                    take, carry = riders("mm_in")
                    (z,), got = colmm(h, [w_in], n_in, BF16, "mm_in", carry)
                    landed(take, got)
                    cat, a1 = mix_fwd(z, conv_full, conv_b, ln_a_g, ln_a_b, vec(ln_v_g), vec(ln_v_b), sp_w[0], sp_bt, seq)
                    s.update(z=z, cat=cat, a1=a1)
                    w_out = weight(('w_out_ab', layer // 2))
                    take, carry = riders("mm_out")
                    (xc, h), got = rowmm(cat, w_out, xc, 1.0, "mm_out", carry, vec(g_ffn2[layer]))
                    landed(take, got)
                else:
                    (qkv,), _ = colmm(h, [weight(('w_qkv', layer // 2))], n_qkv, BF16, "mm_qkv")
                    o, tot, cnt = attn_fwd(qkv, n_seq, seq)
                    s.update(qkv=qkv, o=o, tot=tot, cnt=cnt)
                    (xc, h), _ = rowmm(o, weight(('w_o', layer // 2)), xc, 1.0, "mm_o", None, vec(g_ffn2[layer]))
            s['x' + wn] = xc
            w_gate, w_up = weight((wn + '_gate', layer)), weight((wn + '_up', layer))
            take, carry = riders("ffn_gateup")
            (silu, udsilu, act), got = colmm(h, [w_gate, w_up], d_ff, BF16, "ffn_gateup", carry, swiglu=True)
            landed(take, got)
            s.update({'h' + wn: h, 'swiglu' + wn: (silu, udsilu), 'act' + wn: act})
            w_down = weight((wn + '_down', layer))
            take, carry = riders("ffn_down")
            following = g_mix[layer] if half == 0 else (g_ffn1[layer + 1] if layer + 1 < depth else None)
            (xc, h), got = rowmm(act, w_down, xc, 0.5, "ffn_down", carry, None if following is None else vec(following))
            landed(take, got)
        saved.append(s)

    loss8, dx, dxb, dg_final = loss_head(xc, vec(g_final), target)
    loss = lax.psum(loss8[0, 0], ("x", "y", "c"))

    gw = {}
    gs = {}
    core1 = core.reshape(1).astype(jnp.int32)
    ready = []
    part, recv = {}, {}

    def leaving():
        its = list(ready)
        ready.clear()
        halves = lambda a: a.reshape(N_CHIPS, 2, a.shape[1] // 2, a.shape[2])
        theirs = rs_exchange([halves(gw[it][1]) for it in its])
        sums = rs_add([halves(gw[it][0]) for it in its], theirs, core1, REDUCE_DTYPE, "rs_add")
        part.update(zip(its, sums))
        return its, send_carry(sums)

    for layer in reversed(range(depth)):
        s = saved[layer]
        for half, (gn, wn) in reversed(list(enumerate((('g_ffn1', 'w_ffn1'), ('g_ffn2', 'w_ffn2'))))):
            wd = wt[(wn + '_down', layer)]
            dgate, dup = rowmm_t(dxb, wd, 0.5, BF16, "ffn_bwd_act", swiglu=s['swiglu' + wn])
            gw[(wn + '_down', layer)] = dw_row(s['act' + wn], dxb, 0.5, "ffn_dw_down")
            gw[(wn + '_gate', layer)], gw[(wn + '_up', layer)] = dw_col(s['h' + wn], [dgate, dup], N_CHIPS, d_ff,
                                                                        "ffn_dw_gateup", transposed=True)
            ready.extend([(wn + '_down', layer), (wn + '_gate', layer), (wn + '_up', layer)])
            its, carry = leaving()
            (dx, dxb, dg), got = colmm_t([dgate, dup], [wt[(wn + '_gate', layer)], wt[(wn + '_up', layer)]], d_ff,
                                         s['x' + wn], vec(p[gn][layer]), dx, "ffn_bwd_in", carry, transposed=True)
            recv.update(zip(its, got))
            gs[(gn, layer)] = dg
            if half == 1:
                if layer % 2 == 0:
                    i = layer // 2
                    w_out = wt[('w_out_ab', i)]
                    dcat = rowmm_t(dxb, w_out, 1.0, F32, "mm_out_t")
                    gw[('w_out_ab', i)] = dw_row(s['cat'], dxb, 1.0, "dw_out")
                    dz, da1, dcb, dlag, dlab, dlvg, dlvb, dspw, dspb = mix_bwd_point(
                        dcat, s['z'], s['a1'], ln_a_g, ln_a_b, vec(ln_v_g), vec(ln_v_b), sp_w[0], sp_wt, sp_bt, seq)
                    dz, dcw = mix_bwd_conv(dz, da1, s['z'], conv_full, seq)
                    gs.update({('conv_b', i): dcb, ('ln_a_g', i): dlag, ('ln_a_b', i): dlab, ('ln_v_g', i): dlvg,
                               ('ln_v_b', i): dlvb, ('sp_w', i): dspw, ('sp_b', i): dspb[:, :, 0], ('conv_w', i): dcw})
                    (gw[('w_in_ab', i)],) = dw_col(s['h_mix'], [dz], N_CHIPS, n_in, "dw_in")
                    ready.extend([('w_out_ab', i), ('w_in_ab', i)])
                    its, carry = leaving()
                    (dx, dxb, dg), got = colmm_t([dz], [wt[('w_in_ab', i)]], n_in, s['x_mix'], vec(g_mix[layer]), dx,
                                                 "mm_in_t", carry)
                    recv.update(zip(its, got))
                else:
                    i = layer // 2
                    w_o4 = wt[('w_o', i)]
                    do = rowmm_t(dxb, w_o4, 1.0, BF16, "mm_o_t")
                    gw[('w_o', i)] = dw_row(s['o'], dxb, 1.0, "dw_o")
                    dq, dk, dv = attn_bwd(s['qkv'], do, s['tot'], s['cnt'], n_seq, seq)
                    dqkv = jnp.concatenate([dq, dk, dv], axis=0)
                    (gw[('w_qkv', i)],) = dw_col(s['h_mix'], [dqkv], N_CHIPS, n_qkv, "dw_qkv")
                    ready.extend([('w_o', i), ('w_qkv', i)])
                    its, carry = leaving()
                    (dx, dxb, dg), got = colmm_t([dqkv], [wt[('w_qkv', i)]], n_qkv, s['x_mix'], vec(g_mix[layer]), dx,
                                                 "mm_qkv_t", carry)
                    recv.update(zip(its, got))
                gs[('g_mix', layer)] = dg
    grad_x = dx.reshape(x.shape)

    assert not ready and set(recv) == set(items)
    where = jnp.stack([chip, core]).astype(jnp.int32)
    fulls = []
    for name in BIG:
        its = [(name, layer) for layer in range(p[name].shape[0])]
        fulls.append(rs_sum([recv[it] for it in its], [part[it] for it in its], where, "rs_sum"))
    shared = rs_share(fulls)
    grads = {name: sh.reshape(p[name].shape) for name, sh in zip(BIG, shared)}

    stack = lambda name: jnp.concatenate([gs[(name, layer)].reshape((1,) + p[name].shape[1:]) for layer in range(p[name].shape[0])], axis=0)
    small_g = [stack(name) if name != 'g_final' else dg_final.reshape(p[name].shape) for name in SMALL]
    packed = _pack(small_g + [gs[('conv_w', 0)]])
    red = allreduce_small(packed)
    outs = _unpack(red, [p[name].shape for name in SMALL] + [(CONV_WIDTH, c_mix)])
    for name, g in zip(SMALL, outs[:-1]):
        grads[name] = g
    conv_g = outs[-1].reshape(CONV_WIDTH, N_CHIPS, c_mix // N_CHIPS)
    grads['conv_w'] = lax.dynamic_index_in_dim(conv_g, chip, axis=1, keepdims=False).reshape(conv_w.shape)

    delta, new_m, new_v = {}, {}, {}
    for name in BIG:
        shp = p[name].shape
        two = lambda a: a.reshape(shp[0] * shp[1], shp[2])
        dl, nm, nv = adamw(two(p[name]), two(grads[name]), two(p['m_' + name]), two(p['v_' + name]), "adamw")
        delta[name], new_m[name], new_v[name] = dl.reshape(shp), nm.reshape(shp), nv.reshape(shp)
    small_names = SMALL + ['conv_w']
    pk = lambda pre: _pack([p[pre + name] for name in small_names])
    dl, nm, nv = adamw(pk(''), _pack([grads[name] for name in small_names]), pk('m_'), pk('v_'), "adamw_small")
    shapes = [p[name].shape for name in small_names]
    for dst, val in ((delta, dl), (new_m, nm), (new_v, nv)):
        for name, a in zip(small_names, _unpack(val, shapes)):
            dst[name] = a

    return (loss, grad_x, *[back(n, d[n]) for d in (grads, delta, new_m, new_v) for n in WEIGHTS])
```
